```python
import math
import jax, jax.numpy as jnp
from jax import lax
import numpy as np

D_MODEL = 1024
BATCH = 8
SEQ = 2048
DEPTH = 2

SSD_EXPAND = 2
SSD_D_INNER = SSD_EXPAND * D_MODEL
SSD_HEAD_DIM = 64
SSD_N_HEADS = SSD_D_INNER // SSD_HEAD_DIM
SSD_N_GROUPS = 4
SSD_HEADS_PER_GROUP = SSD_N_HEADS // SSD_N_GROUPS
SSD_D_STATE = 128
SSD_CONV_WIDTH = 4
SSD_CHUNK = 128
SSD_CONV_DIM = SSD_D_INNER + 2 * SSD_N_GROUPS * SSD_D_STATE

S5_WIDTH = D_MODEL
S5_GROUP = 16
S5_N_GROUPS = S5_WIDTH // S5_GROUP
S5_STATE = 64
S5_MIN_STEP = 0.001
S5_MAX_STEP = 0.1
S5_MAX_REAL = -1e-4

FFN_HIDDEN = 2816

RMS_EPS = 1e-6

IN_PROJ_DIM = SSD_D_INNER + SSD_CONV_DIM + SSD_N_HEADS + S5_WIDTH + 2 * D_MODEL
SPLITS = list(np.cumsum([SSD_D_INNER, SSD_CONV_DIM, SSD_N_HEADS, S5_WIDTH, D_MODEL]))

kernel_name = "hybrid_macaron_gated_ssd_s5"


def rmsnorm(x, g):
    xf = x.astype(jnp.float32)
    xf = xf * lax.rsqrt(jnp.mean(xf * xf, axis=-1, keepdims=True) + RMS_EPS)
    return (xf * g.astype(jnp.float32)).astype(x.dtype)


def swiglu(x, w_gate, w_up, w_down):
    return (jax.nn.silu(x @ w_gate) * (x @ w_up)) @ w_down


def causal_depthwise_conv(u, w, b):
    k, c = w.shape
    out = lax.conv_general_dilated(
        u, w[:, None, :].astype(u.dtype), window_strides=(1,), padding=[(k - 1, 0)],
        dimension_numbers=("NWC", "WIO", "NWC"), feature_group_count=c)
    return out + b.astype(u.dtype)


def segsum(a):
    t = a.shape[-1]
    cs = jnp.cumsum(a, axis=-1)
    seg = cs[..., :, None] - cs[..., None, :]
    mask = jnp.tril(jnp.ones((t, t), dtype=bool))
    return jnp.where(mask, seg, -jnp.inf)


def ssd_chunked(xdt, adt, bmat, cmat):
    b, l, h, p = xdt.shape
    g, n = bmat.shape[2], bmat.shape[3]
    r = h // g
    q = SSD_CHUNK
    c = l // q
    X = xdt.reshape(b, c, q, g, r, p)
    A = adt.reshape(b, c, q, g, r).transpose(0, 3, 4, 1, 2)
    Bc = bmat.reshape(b, c, q, g, n)
    Cc = cmat.reshape(b, c, q, g, n)
    a_cum = jnp.cumsum(A, axis=-1)
    decay_in = jnp.exp(segsum(A))
    cb = jnp.einsum("bclgn,bcsgn->bcgls", Cc, Bc)
    y_diag = jnp.einsum("bcgls,bgrcls,bcsgrp->bclgrp", cb, decay_in, X)
    decay_states = jnp.exp(a_cum[..., -1:] - a_cum)
    states = jnp.einsum("bclgn,bgrcl,bclgrp->bcgrpn", Bc, decay_states, X)
    states = jnp.concatenate([jnp.zeros_like(states[:, :1]), states], axis=1)
    chunk_a = jnp.pad(a_cum[..., -1], ((0, 0), (0, 0), (0, 0), (1, 0)))
    chunk_decay = jnp.exp(segsum(chunk_a))
    new_states = jnp.einsum("bgrzc,bcgrpn->bzgrpn", chunk_decay, states)
    states = new_states[:, :-1]
    y_off = jnp.einsum("bclgn,bcgrpn,bgrcl->bclgrp", Cc, states, jnp.exp(a_cum))
    return (y_diag + y_off).reshape(b, l, h, p)


def ssd_branch(z, xbc, dt_raw, conv_w, conv_b, dt_bias, a_log, d_skip, norm_g):
    b, l, _ = z.shape
    xbc = jax.nn.silu(causal_depthwise_conv(xbc, conv_w, conv_b)).astype(jnp.float32)
    xs = xbc[..., :SSD_D_INNER].reshape(b, l, SSD_N_HEADS, SSD_HEAD_DIM)
    bm = xbc[..., SSD_D_INNER:SSD_D_INNER + SSD_N_GROUPS * SSD_D_STATE].reshape(b, l, SSD_N_GROUPS, SSD_D_STATE)
    cm = xbc[..., SSD_D_INNER + SSD_N_GROUPS * SSD_D_STATE:].reshape(b, l, SSD_N_GROUPS, SSD_D_STATE)
    dt = jax.nn.softplus(dt_raw.astype(jnp.float32) + dt_bias.astype(jnp.float32))
    a = -jnp.exp(a_log.astype(jnp.float32))
    y = ssd_chunked(xs * dt[..., None], a * dt, bm, cm) + d_skip.astype(jnp.float32)[:, None] * xs
    y = y.reshape(b, l, SSD_D_INNER)
    y = y * jax.nn.silu(z.astype(jnp.float32))
    yg = y.reshape(b, l, SSD_N_GROUPS, SSD_D_INNER // SSD_N_GROUPS)
    yg = yg * lax.rsqrt(jnp.mean(yg * yg, axis=-1, keepdims=True) + RMS_EPS)
    y = yg.reshape(b, l, SSD_D_INNER) * norm_g.astype(jnp.float32)
    return y.astype(z.dtype)


def s5_branch(u, lam_re, lam_im, b_re, b_im, c_re, c_im, log_step, d_skip, w_glu):
    b, l, _ = u.shape
    uf = u.astype(jnp.float32)
    ug = uf.reshape(b, l, S5_N_GROUPS, S5_GROUP)
    lam = lax.complex(jnp.minimum(lam_re.astype(jnp.float32), S5_MAX_REAL), lam_im.astype(jnp.float32))
    step = jnp.exp(log_step.astype(jnp.float32))[:, None]
    lam_bar = jnp.exp(lam * step)
    b_bar = ((lam_bar - 1.0) / lam)[..., None] * lax.complex(b_re.astype(jnp.float32), b_im.astype(jnp.float32))
    bu = jnp.einsum("blgc,gnc->blgn", ug.astype(jnp.complex64), b_bar)
    a_elems = jnp.broadcast_to(lam_bar, bu.shape)

    def combine(e1, e2):
        a1, s1 = e1
        a2, s2 = e2
        return a2 * a1, a2 * s1 + s2

    _, states = lax.associative_scan(combine, (a_elems, bu), axis=1)
    c = lax.complex(c_re.astype(jnp.float32), c_im.astype(jnp.float32))
    y = jnp.real(jnp.einsum("blgn,gcn->blgc", states, c)).reshape(b, l, S5_WIDTH)
    y = y + d_skip.astype(jnp.float32) * uf
    y = jax.nn.gelu(y).astype(u.dtype)
    val, gate = jnp.split(y @ w_glu, 2, axis=-1)
    return val * jax.nn.sigmoid(gate)


def mixer(u, w_in, ssd_conv_w, ssd_conv_b, ssd_dt_bias, ssd_a_log, ssd_d, ssd_norm_g, w_branch_a,
          s5_lambda_re, s5_lambda_im, s5_b_re, s5_b_im, s5_c_re, s5_c_im, s5_log_step, s5_d, s5_w_glu,
          w_branch_b, w_out):
    proj = u @ w_in
    z, xbc, dt_raw, u5, g_a, g_b = jnp.split(proj, SPLITS, axis=-1)
    y_a = ssd_branch(z, xbc, dt_raw, ssd_conv_w, ssd_conv_b, ssd_dt_bias, ssd_a_log, ssd_d, ssd_norm_g) @ w_branch_a
    y_b = s5_branch(u5, s5_lambda_re, s5_lambda_im, s5_b_re, s5_b_im, s5_c_re, s5_c_im,
                    s5_log_step, s5_d, s5_w_glu) @ w_branch_b
    merged = jax.nn.sigmoid(g_a) * y_a + jax.nn.sigmoid(g_b) * y_b
    return merged @ w_out


def _fwd_setup_inputs(seed: int = 0) -> dict:
    key = jax.random.key(seed)
    ks = iter(jax.random.split(key, 40))
    f32 = jnp.float32
    L = DEPTH

    def normal(shape, scale):
        return jax.random.normal(next(ks), shape, f32) * scale

    def gain(shape):
        return 1.0 + 0.02 * jax.random.normal(next(ks), shape, f32)

    x = jax.random.normal(next(ks), (BATCH, SEQ, D_MODEL), f32)
    d_ = D_MODEL
    dt_init = jnp.exp(jax.random.uniform(next(ks), (L, SSD_N_HEADS), f32,
                                         math.log(0.001), math.log(0.1)))
    n_idx = jnp.arange(S5_STATE, dtype=f32)
    return {
        "x": x,
        "ffn1_pre_g": gain((L, d_)),
        "ffn1_post_g": gain((L, d_)),
        "ffn1_w_gate": normal((L, d_, FFN_HIDDEN), d_ ** -0.5),
        "ffn1_w_up": normal((L, d_, FFN_HIDDEN), d_ ** -0.5),
        "ffn1_w_down": normal((L, FFN_HIDDEN, d_), FFN_HIDDEN ** -0.5),
        "mix_pre_g": gain((L, d_)),
        "mix_post_g": gain((L, d_)),
        "w_in": normal((L, d_, IN_PROJ_DIM), d_ ** -0.5),
        "ssd_conv_w": normal((L, SSD_CONV_WIDTH, SSD_CONV_DIM), SSD_CONV_WIDTH ** -0.5),
        "ssd_conv_b": normal((L, SSD_CONV_DIM), 0.02),
        "ssd_dt_bias": dt_init + jnp.log(-jnp.expm1(-dt_init)),
        "ssd_a_log": jnp.log(jax.random.uniform(next(ks), (L, SSD_N_HEADS), f32, 1.0, 16.0)),
        "ssd_d": gain((L, SSD_N_HEADS)),
        "ssd_norm_g": gain((L, SSD_D_INNER)),
        "w_branch_a": normal((L, SSD_D_INNER, d_), SSD_D_INNER ** -0.5),
        "s5_lambda_re": -0.5 + 0.01 * jax.random.normal(next(ks), (L, S5_N_GROUPS, S5_STATE), f32),
        "s5_lambda_im": math.pi * n_idx + 0.01 * jax.random.normal(next(ks), (L, S5_N_GROUPS, S5_STATE), f32),
        "s5_b_re": normal((L, S5_N_GROUPS, S5_STATE, S5_GROUP), (2 * S5_GROUP) ** -0.5),
        "s5_b_im": normal((L, S5_N_GROUPS, S5_STATE, S5_GROUP), (2 * S5_GROUP) ** -0.5),
        "s5_c_re": normal((L, S5_N_GROUPS, S5_GROUP, S5_STATE), (2 * S5_STATE) ** -0.5),
        "s5_c_im": normal((L, S5_N_GROUPS, S5_GROUP, S5_STATE), (2 * S5_STATE) ** -0.5),
        "s5_log_step": jax.random.uniform(next(ks), (L, S5_N_GROUPS), f32,
                                          math.log(S5_MIN_STEP), math.log(S5_MAX_STEP)),
        "s5_d": normal((L, S5_WIDTH), 1.0),
        "s5_w_glu": normal((L, S5_WIDTH, 2 * S5_WIDTH), S5_WIDTH ** -0.5),
        "w_branch_b": normal((L, S5_WIDTH, d_), S5_WIDTH ** -0.5),
        "w_out": normal((L, d_, d_), d_ ** -0.5),
        "ffn2_pre_g": gain((L, d_)),
        "ffn2_post_g": gain((L, d_)),
        "ffn2_w_gate": normal((L, d_, FFN_HIDDEN), d_ ** -0.5),
        "ffn2_w_up": normal((L, d_, FFN_HIDDEN), d_ ** -0.5),
        "ffn2_w_down": normal((L, FFN_HIDDEN, d_), FFN_HIDDEN ** -0.5),
    }


def _fwd_reference(x, ffn1_pre_g, ffn1_post_g, ffn1_w_gate, ffn1_w_up, ffn1_w_down,
              mix_pre_g, mix_post_g, w_in, ssd_conv_w, ssd_conv_b, ssd_dt_bias, ssd_a_log, ssd_d,
              ssd_norm_g, w_branch_a, s5_lambda_re, s5_lambda_im, s5_b_re, s5_b_im, s5_c_re, s5_c_im,
              s5_log_step, s5_d, s5_w_glu, w_branch_b, w_out,
              ffn2_pre_g, ffn2_post_g, ffn2_w_gate, ffn2_w_up, ffn2_w_down):
    h = x
    for i in range(DEPTH):
        f = swiglu(rmsnorm(h, ffn1_pre_g[i]), ffn1_w_gate[i], ffn1_w_up[i], ffn1_w_down[i])
        h = h + 0.5 * rmsnorm(f, ffn1_post_g[i])
        m = mixer(rmsnorm(h, mix_pre_g[i]), w_in[i], ssd_conv_w[i], ssd_conv_b[i], ssd_dt_bias[i],
                  ssd_a_log[i], ssd_d[i], ssd_norm_g[i], w_branch_a[i],
                  s5_lambda_re[i], s5_lambda_im[i], s5_b_re[i], s5_b_im[i], s5_c_re[i], s5_c_im[i],
                  s5_log_step[i], s5_d[i], s5_w_glu[i], w_branch_b[i], w_out[i])
        h = h + rmsnorm(m, mix_post_g[i])
        f = swiglu(rmsnorm(h, ffn2_pre_g[i]), ffn2_w_gate[i], ffn2_w_up[i], ffn2_w_down[i])
        h = h + 0.5 * rmsnorm(f, ffn2_post_g[i])
    return h


import jax as _jax
import jax.numpy as _jnp

TWIN_FORMAT = 'train_step'
FWD_PARAMS = ['x', 'ffn1_pre_g', 'ffn1_post_g', 'ffn1_w_gate', 'ffn1_w_up', 'ffn1_w_down', 'mix_pre_g', 'mix_post_g', 'w_in', 'ssd_conv_w', 'ssd_conv_b', 'ssd_dt_bias', 'ssd_a_log', 'ssd_d', 'ssd_norm_g', 'w_branch_a', 's5_lambda_re', 's5_lambda_im', 's5_b_re', 's5_b_im', 's5_c_re', 's5_c_im', 's5_log_step', 's5_d', 's5_w_glu', 'w_branch_b', 'w_out', 'ffn2_pre_g', 'ffn2_post_g', 'ffn2_w_gate', 'ffn2_w_up', 'ffn2_w_down']
TWIN_WEIGHTS = ['ffn1_pre_g', 'ffn1_post_g', 'ffn1_w_gate', 'ffn1_w_up', 'ffn1_w_down', 'mix_pre_g', 'mix_post_g', 'w_in', 'ssd_conv_w', 'ssd_conv_b', 'ssd_dt_bias', 'ssd_a_log', 'ssd_d', 'ssd_norm_g', 'w_branch_a', 's5_lambda_re', 's5_lambda_im', 's5_b_re', 's5_b_im', 's5_c_re', 's5_c_im', 's5_log_step', 's5_d', 's5_w_glu', 'w_branch_b', 'w_out', 'ffn2_pre_g', 'ffn2_post_g', 'ffn2_w_gate', 'ffn2_w_up', 'ffn2_w_down']
TWIN_DIFF_INPUT = 'x'
TWIN_INPUTS = ['x', 'ffn1_pre_g', 'ffn1_post_g', 'ffn1_w_gate', 'ffn1_w_up', 'ffn1_w_down', 'mix_pre_g', 'mix_post_g', 'w_in', 'ssd_conv_w', 'ssd_conv_b', 'ssd_dt_bias', 'ssd_a_log', 'ssd_d', 'ssd_norm_g', 'w_branch_a', 's5_lambda_re', 's5_lambda_im', 's5_b_re', 's5_b_im', 's5_c_re', 's5_c_im', 's5_log_step', 's5_d', 's5_w_glu', 'w_branch_b', 'w_out', 'ffn2_pre_g', 'ffn2_post_g', 'ffn2_w_gate', 'ffn2_w_up', 'ffn2_w_down', 'loss_target', 'm_ffn1_pre_g', 'm_ffn1_post_g', 'm_ffn1_w_gate', 'm_ffn1_w_up', 'm_ffn1_w_down', 'm_mix_pre_g', 'm_mix_post_g', 'm_w_in', 'm_ssd_conv_w', 'm_ssd_conv_b', 'm_ssd_dt_bias', 'm_ssd_a_log', 'm_ssd_d', 'm_ssd_norm_g', 'm_w_branch_a', 'm_s5_lambda_re', 'm_s5_lambda_im', 'm_s5_b_re', 'm_s5_b_im', 'm_s5_c_re', 'm_s5_c_im', 'm_s5_log_step', 'm_s5_d', 'm_s5_w_glu', 'm_w_branch_b', 'm_w_out', 'm_ffn2_pre_g', 'm_ffn2_post_g', 'm_ffn2_w_gate', 'm_ffn2_w_up', 'm_ffn2_w_down', 'v_ffn1_pre_g', 'v_ffn1_post_g', 'v_ffn1_w_gate', 'v_ffn1_w_up', 'v_ffn1_w_down', 'v_mix_pre_g', 'v_mix_post_g', 'v_w_in', 'v_ssd_conv_w', 'v_ssd_conv_b', 'v_ssd_dt_bias', 'v_ssd_a_log', 'v_ssd_d', 'v_ssd_norm_g', 'v_w_branch_a', 'v_s5_lambda_re', 'v_s5_lambda_im', 'v_s5_b_re', 'v_s5_b_im', 'v_s5_c_re', 'v_s5_c_im', 'v_s5_log_step', 'v_s5_d', 'v_s5_w_glu', 'v_w_branch_b', 'v_w_out', 'v_ffn2_pre_g', 'v_ffn2_post_g', 'v_ffn2_w_gate', 'v_ffn2_w_up', 'v_ffn2_w_down']
TWIN_OUTPUTS = ['loss', 'grad_x', 'grad_ffn1_pre_g', 'grad_ffn1_post_g', 'grad_ffn1_w_gate', 'grad_ffn1_w_up', 'grad_ffn1_w_down', 'grad_mix_pre_g', 'grad_mix_post_g', 'grad_w_in', 'grad_ssd_conv_w', 'grad_ssd_conv_b', 'grad_ssd_dt_bias', 'grad_ssd_a_log', 'grad_ssd_d', 'grad_ssd_norm_g', 'grad_w_branch_a', 'grad_s5_lambda_re', 'grad_s5_lambda_im', 'grad_s5_b_re', 'grad_s5_b_im', 'grad_s5_c_re', 'grad_s5_c_im', 'grad_s5_log_step', 'grad_s5_d', 'grad_s5_w_glu', 'grad_w_branch_b', 'grad_w_out', 'grad_ffn2_pre_g', 'grad_ffn2_post_g', 'grad_ffn2_w_gate', 'grad_ffn2_w_up', 'grad_ffn2_w_down', 'delta_ffn1_pre_g', 'delta_ffn1_post_g', 'delta_ffn1_w_gate', 'delta_ffn1_w_up', 'delta_ffn1_w_down', 'delta_mix_pre_g', 'delta_mix_post_g', 'delta_w_in', 'delta_ssd_conv_w', 'delta_ssd_conv_b', 'delta_ssd_dt_bias', 'delta_ssd_a_log', 'delta_ssd_d', 'delta_ssd_norm_g', 'delta_w_branch_a', 'delta_s5_lambda_re', 'delta_s5_lambda_im', 'delta_s5_b_re', 'delta_s5_b_im', 'delta_s5_c_re', 'delta_s5_c_im', 'delta_s5_log_step', 'delta_s5_d', 'delta_s5_w_glu', 'delta_w_branch_b', 'delta_w_out', 'delta_ffn2_pre_g', 'delta_ffn2_post_g', 'delta_ffn2_w_gate', 'delta_ffn2_w_up', 'delta_ffn2_w_down', 'new_m_ffn1_pre_g', 'new_m_ffn1_post_g', 'new_m_ffn1_w_gate', 'new_m_ffn1_w_up', 'new_m_ffn1_w_down', 'new_m_mix_pre_g', 'new_m_mix_post_g', 'new_m_w_in', 'new_m_ssd_conv_w', 'new_m_ssd_conv_b', 'new_m_ssd_dt_bias', 'new_m_ssd_a_log', 'new_m_ssd_d', 'new_m_ssd_norm_g', 'new_m_w_branch_a', 'new_m_s5_lambda_re', 'new_m_s5_lambda_im', 'new_m_s5_b_re', 'new_m_s5_b_im', 'new_m_s5_c_re', 'new_m_s5_c_im', 'new_m_s5_log_step', 'new_m_s5_d', 'new_m_s5_w_glu', 'new_m_w_branch_b', 'new_m_w_out', 'new_m_ffn2_pre_g', 'new_m_ffn2_post_g', 'new_m_ffn2_w_gate', 'new_m_ffn2_w_up', 'new_m_ffn2_w_down', 'new_v_ffn1_pre_g', 'new_v_ffn1_post_g', 'new_v_ffn1_w_gate', 'new_v_ffn1_w_up', 'new_v_ffn1_w_down', 'new_v_mix_pre_g', 'new_v_mix_post_g', 'new_v_w_in', 'new_v_ssd_conv_w', 'new_v_ssd_conv_b', 'new_v_ssd_dt_bias', 'new_v_ssd_a_log', 'new_v_ssd_d', 'new_v_ssd_norm_g', 'new_v_w_branch_a', 'new_v_s5_lambda_re', 'new_v_s5_lambda_im', 'new_v_s5_b_re', 'new_v_s5_b_im', 'new_v_s5_c_re', 'new_v_s5_c_im', 'new_v_s5_log_step', 'new_v_s5_d', 'new_v_s5_w_glu', 'new_v_w_branch_b', 'new_v_w_out', 'new_v_ffn2_pre_g', 'new_v_ffn2_post_g', 'new_v_ffn2_w_gate', 'new_v_ffn2_w_up', 'new_v_ffn2_w_down']
TWIN_LEAF_KINDS = {'loss': 'loss', 'grad_x': 'grad_x', 'grad_ffn1_pre_g': 'grad_w', 'grad_ffn1_post_g': 'grad_w', 'grad_ffn1_w_gate': 'grad_w', 'grad_ffn1_w_up': 'grad_w', 'grad_ffn1_w_down': 'grad_w', 'grad_mix_pre_g': 'grad_w', 'grad_mix_post_g': 'grad_w', 'grad_w_in': 'grad_w', 'grad_ssd_conv_w': 'grad_w', 'grad_ssd_conv_b': 'grad_w', 'grad_ssd_dt_bias': 'grad_w', 'grad_ssd_a_log': 'grad_w', 'grad_ssd_d': 'grad_w', 'grad_ssd_norm_g': 'grad_w', 'grad_w_branch_a': 'grad_w', 'grad_s5_lambda_re': 'grad_w', 'grad_s5_lambda_im': 'grad_w', 'grad_s5_b_re': 'grad_w', 'grad_s5_b_im': 'grad_w', 'grad_s5_c_re': 'grad_w', 'grad_s5_c_im': 'grad_w', 'grad_s5_log_step': 'grad_w', 'grad_s5_d': 'grad_w', 'grad_s5_w_glu': 'grad_w', 'grad_w_branch_b': 'grad_w', 'grad_w_out': 'grad_w', 'grad_ffn2_pre_g': 'grad_w', 'grad_ffn2_post_g': 'grad_w', 'grad_ffn2_w_gate': 'grad_w', 'grad_ffn2_w_up': 'grad_w', 'grad_ffn2_w_down': 'grad_w', 'delta_ffn1_pre_g': 'delta_w', 'delta_ffn1_post_g': 'delta_w', 'delta_ffn1_w_gate': 'delta_w', 'delta_ffn1_w_up': 'delta_w', 'delta_ffn1_w_down': 'delta_w', 'delta_mix_pre_g': 'delta_w', 'delta_mix_post_g': 'delta_w', 'delta_w_in': 'delta_w', 'delta_ssd_conv_w': 'delta_w', 'delta_ssd_conv_b': 'delta_w', 'delta_ssd_dt_bias': 'delta_w', 'delta_ssd_a_log': 'delta_w', 'delta_ssd_d': 'delta_w', 'delta_ssd_norm_g': 'delta_w', 'delta_w_branch_a': 'delta_w', 'delta_s5_lambda_re': 'delta_w', 'delta_s5_lambda_im': 'delta_w', 'delta_s5_b_re': 'delta_w', 'delta_s5_b_im': 'delta_w', 'delta_s5_c_re': 'delta_w', 'delta_s5_c_im': 'delta_w', 'delta_s5_log_step': 'delta_w', 'delta_s5_d': 'delta_w', 'delta_s5_w_glu': 'delta_w', 'delta_w_branch_b': 'delta_w', 'delta_w_out': 'delta_w', 'delta_ffn2_pre_g': 'delta_w', 'delta_ffn2_post_g': 'delta_w', 'delta_ffn2_w_gate': 'delta_w', 'delta_ffn2_w_up': 'delta_w', 'delta_ffn2_w_down': 'delta_w', 'new_m_ffn1_pre_g': 'new_m', 'new_m_ffn1_post_g': 'new_m', 'new_m_ffn1_w_gate': 'new_m', 'new_m_ffn1_w_up': 'new_m', 'new_m_ffn1_w_down': 'new_m', 'new_m_mix_pre_g': 'new_m', 'new_m_mix_post_g': 'new_m', 'new_m_w_in': 'new_m', 'new_m_ssd_conv_w': 'new_m', 'new_m_ssd_conv_b': 'new_m', 'new_m_ssd_dt_bias': 'new_m', 'new_m_ssd_a_log': 'new_m', 'new_m_ssd_d': 'new_m', 'new_m_ssd_norm_g': 'new_m', 'new_m_w_branch_a': 'new_m', 'new_m_s5_lambda_re': 'new_m', 'new_m_s5_lambda_im': 'new_m', 'new_m_s5_b_re': 'new_m', 'new_m_s5_b_im': 'new_m', 'new_m_s5_c_re': 'new_m', 'new_m_s5_c_im': 'new_m', 'new_m_s5_log_step': 'new_m', 'new_m_s5_d': 'new_m', 'new_m_s5_w_glu': 'new_m', 'new_m_w_branch_b': 'new_m', 'new_m_w_out': 'new_m', 'new_m_ffn2_pre_g': 'new_m', 'new_m_ffn2_post_g': 'new_m', 'new_m_ffn2_w_gate': 'new_m', 'new_m_ffn2_w_up': 'new_m', 'new_m_ffn2_w_down': 'new_m', 'new_v_ffn1_pre_g': 'new_v', 'new_v_ffn1_post_g': 'new_v', 'new_v_ffn1_w_gate': 'new_v', 'new_v_ffn1_w_up': 'new_v', 'new_v_ffn1_w_down': 'new_v', 'new_v_mix_pre_g': 'new_v', 'new_v_mix_post_g': 'new_v', 'new_v_w_in': 'new_v', 'new_v_ssd_conv_w': 'new_v', 'new_v_ssd_conv_b': 'new_v', 'new_v_ssd_dt_bias': 'new_v', 'new_v_ssd_a_log': 'new_v', 'new_v_ssd_d': 'new_v', 'new_v_ssd_norm_g': 'new_v', 'new_v_w_branch_a': 'new_v', 'new_v_s5_lambda_re': 'new_v', 'new_v_s5_lambda_im': 'new_v', 'new_v_s5_b_re': 'new_v', 'new_v_s5_b_im': 'new_v', 'new_v_s5_c_re': 'new_v', 'new_v_s5_c_im': 'new_v', 'new_v_s5_log_step': 'new_v', 'new_v_s5_d': 'new_v', 'new_v_s5_w_glu': 'new_v', 'new_v_w_branch_b': 'new_v', 'new_v_w_out': 'new_v', 'new_v_ffn2_pre_g': 'new_v', 'new_v_ffn2_post_g': 'new_v', 'new_v_ffn2_w_gate': 'new_v', 'new_v_ffn2_w_up': 'new_v', 'new_v_ffn2_w_down': 'new_v'}


def _forward(args):
    return _fwd_reference(*[args[k] for k in FWD_PARAMS])


def _output_shape():
    out = _jax.eval_shape(lambda: _forward(_fwd_setup_inputs(0)))
    return out.shape, out.dtype

N_MICROBATCH = 1
ADAM_LR = 0.001
ADAM_B1 = 0.9
ADAM_B2 = 0.999
ADAM_EPS = 1e-08
ADAM_WD = 0.01
ADAM_STEP = 10
PER_EXAMPLE_BATCH_AXIS = {'x': 0, 'loss_target': 0}
SHARED_INPUTS = []
_WEIGHT_DTYPES = {'ffn1_pre_g': _jnp.float32, 'ffn1_post_g': _jnp.float32, 'ffn1_w_gate': _jnp.float32, 'ffn1_w_up': _jnp.float32, 'ffn1_w_down': _jnp.float32, 'mix_pre_g': _jnp.float32, 'mix_post_g': _jnp.float32, 'w_in': _jnp.float32, 'ssd_conv_w': _jnp.float32, 'ssd_conv_b': _jnp.float32, 'ssd_dt_bias': _jnp.float32, 'ssd_a_log': _jnp.float32, 'ssd_d': _jnp.float32, 'ssd_norm_g': _jnp.float32, 'w_branch_a': _jnp.float32, 's5_lambda_re': _jnp.float32, 's5_lambda_im': _jnp.float32, 's5_b_re': _jnp.float32, 's5_b_im': _jnp.float32, 's5_c_re': _jnp.float32, 's5_c_im': _jnp.float32, 's5_log_step': _jnp.float32, 's5_d': _jnp.float32, 's5_w_glu': _jnp.float32, 'w_branch_b': _jnp.float32, 'w_out': _jnp.float32, 'ffn2_pre_g': _jnp.float32, 'ffn2_post_g': _jnp.float32, 'ffn2_w_gate': _jnp.float32, 'ffn2_w_up': _jnp.float32, 'ffn2_w_down': _jnp.float32}
MOMENT_SCALE = {'ffn1_pre_g': 4.449116e-01, 'ffn1_post_g': 3.900295e+00, 'ffn1_w_gate': 1.905159e-01, 'ffn1_w_up': 1.987507e-01, 'ffn1_w_down': 3.310627e-01, 'mix_pre_g': 6.374666e-01, 'mix_post_g': 1.607308e+01, 'w_in': 2.247844e-01, 'ssd_conv_w': 3.209443e-01, 'ssd_conv_b': 9.362074e-01, 'ssd_dt_bias': 4.516318e-01, 'ssd_a_log': 1.605356e+00, 'ssd_d': 1.933940e+00, 'ssd_norm_g': 4.915587e-01, 'w_branch_a': 7.015325e-01, 's5_lambda_re': 9.253745e-03, 's5_lambda_im': 1.183969e-02, 's5_b_re': 5.946861e-03, 's5_b_im': 6.659082e-03, 's5_c_re': 1.408697e-02, 's5_c_im': 1.208450e-02, 's5_log_step': 5.035745e+00, 's5_d': 5.804466e-01, 's5_w_glu': 4.023514e-01, 'w_branch_b': 5.550807e-01, 'w_out': 9.223334e-01, 'ffn2_pre_g': 3.511684e-01, 'ffn2_post_g': 3.962911e+00, 'ffn2_w_gate': 1.266030e-01, 'ffn2_w_up': 1.646550e-01, 'ffn2_w_down': 2.720891e-01}


def _to_microbatches(a, axis):
    t = _jnp.moveaxis(a, axis, 0)
    t = t.reshape((N_MICROBATCH, t.shape[0] // N_MICROBATCH) + t.shape[1:])
    return _jnp.moveaxis(t, 1, axis + 1)


def setup_inputs(seed: int = 0) -> dict:
    inp = _fwd_setup_inputs(seed)
    key = _jax.random.fold_in(_jax.random.key(seed), 7919)
    shape, _ = _output_shape()
    out = dict(inp)
    out["loss_target"] = _jax.random.normal(_jax.random.fold_in(key, 0), shape, _jnp.float32)
    for i, name in enumerate(TWIN_WEIGHTS):
        w = inp[name].astype(_jnp.float32)
        if MOMENT_SCALE is None:
            s = _jnp.sqrt(_jnp.mean(_jnp.square(w)) + 1e-30)
        else:
            s = MOMENT_SCALE[name]
        km, kv = _jax.random.split(_jax.random.fold_in(key, i + 1))
        out[name] = w
        out["m_" + name] = s * _jax.random.normal(km, w.shape, _jnp.float32)
        out["v_" + name] = (s * s) * _jax.random.uniform(kv, w.shape, _jnp.float32, 0.5, 1.5)
    if N_MICROBATCH > 1:
        for name, axis in PER_EXAMPLE_BATCH_AXIS.items():
            out[name] = _to_microbatches(out[name], axis)
    return {'x': out['x'], 'ffn1_pre_g': out['ffn1_pre_g'], 'ffn1_post_g': out['ffn1_post_g'], 'ffn1_w_gate': out['ffn1_w_gate'], 'ffn1_w_up': out['ffn1_w_up'], 'ffn1_w_down': out['ffn1_w_down'], 'mix_pre_g': out['mix_pre_g'], 'mix_post_g': out['mix_post_g'], 'w_in': out['w_in'], 'ssd_conv_w': out['ssd_conv_w'], 'ssd_conv_b': out['ssd_conv_b'], 'ssd_dt_bias': out['ssd_dt_bias'], 'ssd_a_log': out['ssd_a_log'], 'ssd_d': out['ssd_d'], 'ssd_norm_g': out['ssd_norm_g'], 'w_branch_a': out['w_branch_a'], 's5_lambda_re': out['s5_lambda_re'], 's5_lambda_im': out['s5_lambda_im'], 's5_b_re': out['s5_b_re'], 's5_b_im': out['s5_b_im'], 's5_c_re': out['s5_c_re'], 's5_c_im': out['s5_c_im'], 's5_log_step': out['s5_log_step'], 's5_d': out['s5_d'], 's5_w_glu': out['s5_w_glu'], 'w_branch_b': out['w_branch_b'], 'w_out': out['w_out'], 'ffn2_pre_g': out['ffn2_pre_g'], 'ffn2_post_g': out['ffn2_post_g'], 'ffn2_w_gate': out['ffn2_w_gate'], 'ffn2_w_up': out['ffn2_w_up'], 'ffn2_w_down': out['ffn2_w_down'], 'loss_target': out['loss_target'], 'm_ffn1_pre_g': out['m_ffn1_pre_g'], 'm_ffn1_post_g': out['m_ffn1_post_g'], 'm_ffn1_w_gate': out['m_ffn1_w_gate'], 'm_ffn1_w_up': out['m_ffn1_w_up'], 'm_ffn1_w_down': out['m_ffn1_w_down'], 'm_mix_pre_g': out['m_mix_pre_g'], 'm_mix_post_g': out['m_mix_post_g'], 'm_w_in': out['m_w_in'], 'm_ssd_conv_w': out['m_ssd_conv_w'], 'm_ssd_conv_b': out['m_ssd_conv_b'], 'm_ssd_dt_bias': out['m_ssd_dt_bias'], 'm_ssd_a_log': out['m_ssd_a_log'], 'm_ssd_d': out['m_ssd_d'], 'm_ssd_norm_g': out['m_ssd_norm_g'], 'm_w_branch_a': out['m_w_branch_a'], 'm_s5_lambda_re': out['m_s5_lambda_re'], 'm_s5_lambda_im': out['m_s5_lambda_im'], 'm_s5_b_re': out['m_s5_b_re'], 'm_s5_b_im': out['m_s5_b_im'], 'm_s5_c_re': out['m_s5_c_re'], 'm_s5_c_im': out['m_s5_c_im'], 'm_s5_log_step': out['m_s5_log_step'], 'm_s5_d': out['m_s5_d'], 'm_s5_w_glu': out['m_s5_w_glu'], 'm_w_branch_b': out['m_w_branch_b'], 'm_w_out': out['m_w_out'], 'm_ffn2_pre_g': out['m_ffn2_pre_g'], 'm_ffn2_post_g': out['m_ffn2_post_g'], 'm_ffn2_w_gate': out['m_ffn2_w_gate'], 'm_ffn2_w_up': out['m_ffn2_w_up'], 'm_ffn2_w_down': out['m_ffn2_w_down'], 'v_ffn1_pre_g': out['v_ffn1_pre_g'], 'v_ffn1_post_g': out['v_ffn1_post_g'], 'v_ffn1_w_gate': out['v_ffn1_w_gate'], 'v_ffn1_w_up': out['v_ffn1_w_up'], 'v_ffn1_w_down': out['v_ffn1_w_down'], 'v_mix_pre_g': out['v_mix_pre_g'], 'v_mix_post_g': out['v_mix_post_g'], 'v_w_in': out['v_w_in'], 'v_ssd_conv_w': out['v_ssd_conv_w'], 'v_ssd_conv_b': out['v_ssd_conv_b'], 'v_ssd_dt_bias': out['v_ssd_dt_bias'], 'v_ssd_a_log': out['v_ssd_a_log'], 'v_ssd_d': out['v_ssd_d'], 'v_ssd_norm_g': out['v_ssd_norm_g'], 'v_w_branch_a': out['v_w_branch_a'], 'v_s5_lambda_re': out['v_s5_lambda_re'], 'v_s5_lambda_im': out['v_s5_lambda_im'], 'v_s5_b_re': out['v_s5_b_re'], 'v_s5_b_im': out['v_s5_b_im'], 'v_s5_c_re': out['v_s5_c_re'], 'v_s5_c_im': out['v_s5_c_im'], 'v_s5_log_step': out['v_s5_log_step'], 'v_s5_d': out['v_s5_d'], 'v_s5_w_glu': out['v_s5_w_glu'], 'v_w_branch_b': out['v_w_branch_b'], 'v_w_out': out['v_w_out'], 'v_ffn2_pre_g': out['v_ffn2_pre_g'], 'v_ffn2_post_g': out['v_ffn2_post_g'], 'v_ffn2_w_gate': out['v_ffn2_w_gate'], 'v_ffn2_w_up': out['v_ffn2_w_up'], 'v_ffn2_w_down': out['v_ffn2_w_down']}


def _loss(weights, diff, rest, loss_target):
    with _jax.named_scope("forward"):
        args = {**rest, TWIN_DIFF_INPUT: diff, **{k: w.astype(_WEIGHT_DTYPES[k]) for k, w in weights.items()}}
        y = _forward(args)
    with _jax.named_scope("loss_head"):
        err = _jnp.square(y.astype(_jnp.float32) - loss_target)
        return 0.5 * _jnp.sum(_jnp.mean(err, axis=-1)) if err.ndim else 0.5 * err


def _adamw(w, g, m, v):
    m = ADAM_B1 * m + (1.0 - ADAM_B1) * g
    v = ADAM_B2 * v + (1.0 - ADAM_B2) * _jnp.square(g)
    m_hat = m / (1.0 - ADAM_B1 ** ADAM_STEP)
    v_hat = v / (1.0 - ADAM_B2 ** ADAM_STEP)
    delta = -ADAM_LR * (m_hat / (_jnp.sqrt(v_hat) + ADAM_EPS) + ADAM_WD * w)
    return delta, m, v


def reference(x, ffn1_pre_g, ffn1_post_g, ffn1_w_gate, ffn1_w_up, ffn1_w_down, mix_pre_g, mix_post_g, w_in, ssd_conv_w, ssd_conv_b, ssd_dt_bias, ssd_a_log, ssd_d, ssd_norm_g, w_branch_a, s5_lambda_re, s5_lambda_im, s5_b_re, s5_b_im, s5_c_re, s5_c_im, s5_log_step, s5_d, s5_w_glu, w_branch_b, w_out, ffn2_pre_g, ffn2_post_g, ffn2_w_gate, ffn2_w_up, ffn2_w_down, loss_target, m_ffn1_pre_g, m_ffn1_post_g, m_ffn1_w_gate, m_ffn1_w_up, m_ffn1_w_down, m_mix_pre_g, m_mix_post_g, m_w_in, m_ssd_conv_w, m_ssd_conv_b, m_ssd_dt_bias, m_ssd_a_log, m_ssd_d, m_ssd_norm_g, m_w_branch_a, m_s5_lambda_re, m_s5_lambda_im, m_s5_b_re, m_s5_b_im, m_s5_c_re, m_s5_c_im, m_s5_log_step, m_s5_d, m_s5_w_glu, m_w_branch_b, m_w_out, m_ffn2_pre_g, m_ffn2_post_g, m_ffn2_w_gate, m_ffn2_w_up, m_ffn2_w_down, v_ffn1_pre_g, v_ffn1_post_g, v_ffn1_w_gate, v_ffn1_w_up, v_ffn1_w_down, v_mix_pre_g, v_mix_post_g, v_w_in, v_ssd_conv_w, v_ssd_conv_b, v_ssd_dt_bias, v_ssd_a_log, v_ssd_d, v_ssd_norm_g, v_w_branch_a, v_s5_lambda_re, v_s5_lambda_im, v_s5_b_re, v_s5_b_im, v_s5_c_re, v_s5_c_im, v_s5_log_step, v_s5_d, v_s5_w_glu, v_w_branch_b, v_w_out, v_ffn2_pre_g, v_ffn2_post_g, v_ffn2_w_gate, v_ffn2_w_up, v_ffn2_w_down):
    given = dict(x=x, ffn1_pre_g=ffn1_pre_g, ffn1_post_g=ffn1_post_g, ffn1_w_gate=ffn1_w_gate, ffn1_w_up=ffn1_w_up, ffn1_w_down=ffn1_w_down, mix_pre_g=mix_pre_g, mix_post_g=mix_post_g, w_in=w_in, ssd_conv_w=ssd_conv_w, ssd_conv_b=ssd_conv_b, ssd_dt_bias=ssd_dt_bias, ssd_a_log=ssd_a_log, ssd_d=ssd_d, ssd_norm_g=ssd_norm_g, w_branch_a=w_branch_a, s5_lambda_re=s5_lambda_re, s5_lambda_im=s5_lambda_im, s5_b_re=s5_b_re, s5_b_im=s5_b_im, s5_c_re=s5_c_re, s5_c_im=s5_c_im, s5_log_step=s5_log_step, s5_d=s5_d, s5_w_glu=s5_w_glu, w_branch_b=w_branch_b, w_out=w_out, ffn2_pre_g=ffn2_pre_g, ffn2_post_g=ffn2_post_g, ffn2_w_gate=ffn2_w_gate, ffn2_w_up=ffn2_w_up, ffn2_w_down=ffn2_w_down, loss_target=loss_target, m_ffn1_pre_g=m_ffn1_pre_g, m_ffn1_post_g=m_ffn1_post_g, m_ffn1_w_gate=m_ffn1_w_gate, m_ffn1_w_up=m_ffn1_w_up, m_ffn1_w_down=m_ffn1_w_down, m_mix_pre_g=m_mix_pre_g, m_mix_post_g=m_mix_post_g, m_w_in=m_w_in, m_ssd_conv_w=m_ssd_conv_w, m_ssd_conv_b=m_ssd_conv_b, m_ssd_dt_bias=m_ssd_dt_bias, m_ssd_a_log=m_ssd_a_log, m_ssd_d=m_ssd_d, m_ssd_norm_g=m_ssd_norm_g, m_w_branch_a=m_w_branch_a, m_s5_lambda_re=m_s5_lambda_re, m_s5_lambda_im=m_s5_lambda_im, m_s5_b_re=m_s5_b_re, m_s5_b_im=m_s5_b_im, m_s5_c_re=m_s5_c_re, m_s5_c_im=m_s5_c_im, m_s5_log_step=m_s5_log_step, m_s5_d=m_s5_d, m_s5_w_glu=m_s5_w_glu, m_w_branch_b=m_w_branch_b, m_w_out=m_w_out, m_ffn2_pre_g=m_ffn2_pre_g, m_ffn2_post_g=m_ffn2_post_g, m_ffn2_w_gate=m_ffn2_w_gate, m_ffn2_w_up=m_ffn2_w_up, m_ffn2_w_down=m_ffn2_w_down, v_ffn1_pre_g=v_ffn1_pre_g, v_ffn1_post_g=v_ffn1_post_g, v_ffn1_w_gate=v_ffn1_w_gate, v_ffn1_w_up=v_ffn1_w_up, v_ffn1_w_down=v_ffn1_w_down, v_mix_pre_g=v_mix_pre_g, v_mix_post_g=v_mix_post_g, v_w_in=v_w_in, v_ssd_conv_w=v_ssd_conv_w, v_ssd_conv_b=v_ssd_conv_b, v_ssd_dt_bias=v_ssd_dt_bias, v_ssd_a_log=v_ssd_a_log, v_ssd_d=v_ssd_d, v_ssd_norm_g=v_ssd_norm_g, v_w_branch_a=v_w_branch_a, v_s5_lambda_re=v_s5_lambda_re, v_s5_lambda_im=v_s5_lambda_im, v_s5_b_re=v_s5_b_re, v_s5_b_im=v_s5_b_im, v_s5_c_re=v_s5_c_re, v_s5_c_im=v_s5_c_im, v_s5_log_step=v_s5_log_step, v_s5_d=v_s5_d, v_s5_w_glu=v_s5_w_glu, v_w_branch_b=v_w_branch_b, v_w_out=v_w_out, v_ffn2_pre_g=v_ffn2_pre_g, v_ffn2_post_g=v_ffn2_post_g, v_ffn2_w_gate=v_ffn2_w_gate, v_ffn2_w_up=v_ffn2_w_up, v_ffn2_w_down=v_ffn2_w_down)
    weights = {n: given[n] for n in TWIN_WEIGHTS}
    shared = {n: given[n] for n in SHARED_INPUTS}
    per_example = {n: given[n] for n in ['x']}
    grad_fn = _jax.value_and_grad(_loss, argnums=(0, 1))

    def one_microbatch(ex, loss_target):
        ex = dict(ex)
        diff = ex.pop(TWIN_DIFF_INPUT)
        return grad_fn(weights, diff, {**shared, **ex}, loss_target)

    if N_MICROBATCH == 1:
        loss, (grad_w, grad_x) = one_microbatch(per_example, given["loss_target"])
    else:
        def body(carry, xs):
            loss_sum, grad_sum = carry
            l_k, (gw_k, gx_k) = one_microbatch(xs[0], xs[1])
            with _jax.named_scope("update"):
                return (loss_sum + l_k, _jax.tree.map(_jnp.add, grad_sum, gw_k)), gx_k

        init = (_jnp.zeros((), _jnp.float32), _jax.tree.map(_jnp.zeros_like, weights))
        (loss, grad_w), grad_x = _jax.lax.scan(body, init, (per_example, given["loss_target"]))
    with _jax.named_scope("update"):
        delta_w, new_m, new_v = {}, {}, {}
        for n in TWIN_WEIGHTS:
            delta_w[n], new_m[n], new_v[n] = _adamw(weights[n], grad_w[n], given["m_" + n], given["v_" + n])
    return (loss, grad_x, *[grad_w[n] for n in TWIN_WEIGHTS], *[delta_w[n] for n in TWIN_WEIGHTS],
            *[new_m[n] for n in TWIN_WEIGHTS], *[new_v[n] for n in TWIN_WEIGHTS])
```

```python
import functools

import numpy as np
import jax
import jax.numpy as jnp
from jax import lax
from jax.experimental import pallas as pl
from jax.experimental.pallas import tpu as pltpu

f32, bf16 = jnp.float32, jnp.bfloat16

SSD_N_GROUPS = 4
SSD_CHUNK = 128
RMS_EPS = 1e-6
S5_MAX_REAL = -1e-4
S5_SUPERBLOCK = 256
ADAM_LR, ADAM_B1, ADAM_B2, ADAM_EPS, ADAM_WD, ADAM_STEP = 0.001, 0.9, 0.999, 1e-08, 0.01, 10

LANES = 128
PACK_COLS = 1024
PACK_ROW_MULT = 256
VMEM_LIMIT_BYTES = 48 * 1024 * 1024
N_CHIPS, N_CORES, N_DEV = 4, 2, 8
MESH = pl.DeviceIdType.MESH

BIG = ["ffn1_w_gate", "ffn1_w_up", "ffn1_w_down", "w_in", "w_branch_a", "s5_w_glu", "w_branch_b", "w_out",
       "ffn2_w_gate", "ffn2_w_up", "ffn2_w_down"]
COL_SHARDED = {"ffn1_w_gate", "ffn1_w_up", "w_in", "s5_w_glu", "ffn2_w_gate", "ffn2_w_up"}
SMALL = ["ffn1_pre_g", "ffn1_post_g", "mix_pre_g", "mix_post_g", "ssd_conv_b", "ssd_dt_bias", "ssd_a_log", "ssd_d",
         "ssd_norm_g", "s5_lambda_re", "s5_lambda_im", "s5_b_re", "s5_b_im", "s5_c_re", "s5_c_im", "s5_log_step",
         "s5_d", "ffn2_pre_g", "ffn2_post_g"]
WEIGHTS = ["ffn1_pre_g", "ffn1_post_g", "ffn1_w_gate", "ffn1_w_up", "ffn1_w_down", "mix_pre_g", "mix_post_g", "w_in",
           "ssd_conv_w", "ssd_conv_b", "ssd_dt_bias", "ssd_a_log", "ssd_d", "ssd_norm_g", "w_branch_a", "s5_lambda_re",
           "s5_lambda_im", "s5_b_re", "s5_b_im", "s5_c_re", "s5_c_im", "s5_log_step", "s5_d", "s5_w_glu", "w_branch_b",
           "w_out", "ffn2_pre_g", "ffn2_post_g", "ffn2_w_gate", "ffn2_w_up", "ffn2_w_down"]


def _params(sem=None):
    return pltpu.CompilerParams(dimension_semantics=sem, vmem_limit_bytes=VMEM_LIMIT_BYTES)


def _pick(n, target, mult=LANES):
    best = None
    for d in range(mult, min(n, target) + 1, mult):
        if n % d == 0:
            best = d
    return best if best is not None else n


_DIMS = {"nn": (((1,), (0,)), ((), ())), "nt": (((1,), (1,)), ((), ())), "tn": (((0,), (0,)), ((), ()))}


def _mm(a, b, mode, out_dtype, name, bm_t=512, bn_t=1024, bk_t=2816):
    if mode == "nn":
        (M, K), (K2, N) = a.shape, b.shape
    elif mode == "nt":
        (M, K), (N, K2) = a.shape, b.shape
    else:
        (K, M), (K2, N) = a.shape, b.shape
    assert K == K2, (name, a.shape, b.shape)
    bm, bn, bk = _pick(M, bm_t), _pick(N, bn_t), _pick(K, bk_t)
    nk = K // bk
    dn = _DIMS[mode]

    def body(a_ref, b_ref, o_ref, *scratch):
        p = lax.dot_general(a_ref[...].astype(bf16), b_ref[...].astype(bf16), dn, preferred_element_type=f32)
        if nk == 1:
            o_ref[...] = p.astype(o_ref.dtype)
        else:
            acc = scratch[0]
            k = pl.program_id(2)

            @pl.when(k == 0)
            def _():
                acc[...] = p

            @pl.when(k > 0)
            def _():
                acc[...] += p

            @pl.when(k == nk - 1)
            def _():
                o_ref[...] = acc[...].astype(o_ref.dtype)

    if mode == "tn":
        a_spec = pl.BlockSpec((bk, bm), lambda i, j, k: (k, i))
    else:
        a_spec = pl.BlockSpec((bm, bk), lambda i, j, k: (i, k))
    if mode == "nt":
        b_spec = pl.BlockSpec((bn, bk), lambda i, j, k: (j, k))
    else:
        b_spec = pl.BlockSpec((bk, bn), lambda i, j, k: (k, j))
    return pl.pallas_call(
        body, name=name, grid=(M // bm, N // bn, nk), in_specs=[a_spec, b_spec],
        out_specs=pl.BlockSpec((bm, bn), lambda i, j, k: (i, j)), out_shape=jax.ShapeDtypeStruct((M, N), out_dtype),
        scratch_shapes=[pltpu.VMEM((bm, bn), f32)] if nk > 1 else [],
        compiler_params=_params(("parallel", "parallel", "arbitrary")),
    )(a, b)


def _bdmm(a, w, mode, nb, out_dtype, name, bt_t=512):
    if mode == "tn":
        T = a.shape[0]
        ka, nw = a.shape[1] // nb, w.shape[1] // nb
        bt = _pick(T, bt_t)
        nt = T // bt

        def body_tn(a_ref, b_ref, o_ref):
            p = lax.dot_general(a_ref[...].astype(bf16), b_ref[...].astype(bf16), _DIMS["tn"], preferred_element_type=f32)
            k = pl.program_id(1)

            @pl.when(k == 0)
            def _():
                o_ref[...] = p

            @pl.when(k > 0)
            def _():
                o_ref[...] += p

        return pl.pallas_call(
            body_tn, name=name, grid=(nb, nt),
            in_specs=[pl.BlockSpec((bt, ka), lambda j, k: (k, j)), pl.BlockSpec((bt, nw), lambda j, k: (k, j))],
            out_specs=pl.BlockSpec((ka, nw), lambda j, k: (j, 0)), out_shape=jax.ShapeDtypeStruct((nb * ka, nw), f32),
            compiler_params=_params(("parallel", "arbitrary")),
        )(a, w)
    T = a.shape[0]
    ka, nw = w.shape[0] // nb, w.shape[1]
    bt = _pick(T, bt_t)
    kin, kout = (ka, nw) if mode == "nn" else (nw, ka)
    dn = _DIMS[mode]

    def body(a_ref, w_ref, o_ref):
        o_ref[...] = lax.dot_general(a_ref[...].astype(bf16), w_ref[...].astype(bf16), dn,
                                     preferred_element_type=f32).astype(o_ref.dtype)

    return pl.pallas_call(
        body, name=name, grid=(nb, T // bt),
        in_specs=[pl.BlockSpec((bt, kin), lambda j, i: (i, j)), pl.BlockSpec((ka, nw), lambda j, i: (j, 0))],
        out_specs=pl.BlockSpec((bt, kout), lambda j, i: (i, j)), out_shape=jax.ShapeDtypeStruct((T, nb * kout), out_dtype),
        compiler_params=_params(("parallel", "parallel")),
    )(a, w)


def _row_index(i, cb):
    return (i, cb)


def _row_kernel(name, fn, rows, pars, row_outs, par_outs=(), block_rows=256):
    T = rows[0][0].shape[0]
    R = min(block_rows, T)
    assert T % R == 0
    nr, npar, nro = len(rows), len(pars), len(row_outs)

    def body(*refs):
        rv = [r[...] for r in refs[:nr]]
        pv = [r[...] for r in refs[nr:nr + npar]]
        ro, po = fn(rv, pv)
        for ref, v in zip(refs[nr + npar:nr + npar + nro], ro):
            ref[...] = v.astype(ref.dtype)
        if par_outs:
            i = pl.program_id(0)
            prefs = refs[nr + npar + nro:]

            @pl.when(i == 0)
            def _():
                for ref, v in zip(prefs, po):
                    ref[...] = v.astype(f32)

            @pl.when(i > 0)
            def _():
                for ref, v in zip(prefs, po):
                    ref[...] += v.astype(f32)

    in_specs = [pl.BlockSpec((R, nc), functools.partial(_row_index, cb=cb)) for (_, nc, cb) in rows]
    in_specs += [pl.BlockSpec(p.shape, lambda i: (0, 0)) for p in pars]
    out_specs = [pl.BlockSpec((R, nc), lambda i: (i, 0)) for (nc, _) in row_outs]
    out_specs += [pl.BlockSpec(s, lambda i: (0, 0)) for s in par_outs]
    out_shape = [jax.ShapeDtypeStruct((T, nc), dt) for (nc, dt) in row_outs]
    out_shape += [jax.ShapeDtypeStruct(s, f32) for s in par_outs]
    outs = pl.pallas_call(
        body, name=name, grid=(T // R,), in_specs=in_specs, out_specs=out_specs, out_shape=out_shape,
        compiler_params=_params(("arbitrary",) if par_outs else ("parallel",)),
    )(*[r[0] for r in rows], *pars)
    return list(outs)


def _fwd_of(f):
    def fn(rv, pv):
        return f([v.astype(f32) for v in rv], [v.astype(f32) for v in pv]), []
    return fn


def _vjp_of(f, n_x, n_cot, grad_idx, n_add=0):
    def fn(rv, pv):
        xs = [v.astype(f32) for v in rv[:n_x]]
        cots = [v.astype(f32) for v in rv[n_x:n_x + n_cot]]
        adds = rv[n_x + n_cot:n_x + n_cot + n_add]
        ps = [v.astype(f32) for v in pv]
        _, vjp = jax.vjp(lambda *a: f(list(a[:n_x]), list(a[n_x:])), *xs, *ps)
        g = vjp(cots)
        row_g = [g[i] for i in grad_idx]
        for k, a in enumerate(adds):
            row_g[k] = row_g[k] + a.astype(f32)
        return row_g, list(g[n_x:])
    return fn


def _rms(x, g):
    return x * lax.rsqrt(jnp.mean(x * x, axis=-1, keepdims=True) + RMS_EPS) * g


def _f_norm(xs, ps):
    return [_rms(xs[0], ps[0])]


def _f_post(scale):
    def f(xs, ps):
        return [scale * _rms(xs[0], ps[0])]
    return f


def _f_resnorm(scale):
    def f(xs, ps):
        return [xs[0] + scale * _rms(xs[1], ps[0])]
    return f


def _f_dt(xs, ps):
    dt = jax.nn.softplus(xs[0] + ps[0])
    return [dt, -jnp.exp(ps[1]) * dt]


def _f_ssdpost(n_groups):
    def f(xs, ps):
        y = xs[0] * jax.nn.silu(xs[1])
        width = y.shape[-1] // n_groups
        lane = lax.broadcasted_iota(jnp.int32, y.shape, 1)
        scale = jnp.zeros_like(y)
        for k in range(n_groups):
            m = ((lane >= k * width) & (lane < (k + 1) * width)).astype(f32)
            ms = jnp.sum(y * y * m, axis=-1, keepdims=True) / width
            scale = scale + lax.rsqrt(ms + RMS_EPS) * m
        return [y * scale * ps[0]]
    return f


def _f_s5post(xs, ps):
    return [jax.nn.gelu(xs[0] + ps[0] * xs[1])]


def _f_merge(xs, ps):
    return [jax.nn.sigmoid(xs[0]) * xs[1] + jax.nn.sigmoid(xs[2]) * xs[3]]


def _swiglu_fwd(rv, pv):
    ab = rv[0].astype(f32)
    h = ab.shape[1] // 2
    return [jax.nn.silu(ab[:, :h]) * ab[:, h:]], []


def _swiglu_bwd(rv, pv):
    ab, d = rv[0].astype(f32), rv[1].astype(f32)
    h = ab.shape[1] // 2
    a, b = ab[:, :h], ab[:, h:]
    s = jax.nn.sigmoid(a)
    return [jnp.concatenate([d * b * (s * (1.0 + a * (1.0 - s))), d * (a * s)], axis=1)], []


def _glu_fwd(rv, pv):
    vg = rv[0].astype(f32)
    h = vg.shape[1] // 2
    return [vg[:, :h] * jax.nn.sigmoid(vg[:, h:])], []


def _glu_bwd(rv, pv):
    vg, d = rv[0].astype(f32), rv[1].astype(f32)
    h = vg.shape[1] // 2
    s = jax.nn.sigmoid(vg[:, h:])
    return [jnp.concatenate([d * s, d * vg[:, :h] * s * (1.0 - s)], axis=1)], []


def _loss_fn(rv, pv):
    e = rv[0].astype(f32) - rv[1].astype(f32)
    per_tok = jnp.mean(e * e, axis=-1, keepdims=True)
    part = 0.5 * jnp.sum(per_tok, axis=0, keepdims=True)
    return [e / e.shape[-1]], [jnp.broadcast_to(part, (8, LANES))]


def _add_fn(rv, pv):
    return [rv[0].astype(f32) + rv[1].astype(f32)], []


def _adamw_fn(rv, pv):
    w, g, m, v = [x.astype(f32) for x in rv]
    m = ADAM_B1 * m + (1.0 - ADAM_B1) * g
    v = ADAM_B2 * v + (1.0 - ADAM_B2) * (g * g)
    m_hat = m / (1.0 - ADAM_B1 ** ADAM_STEP)
    v_hat = v / (1.0 - ADAM_B2 ** ADAM_STEP)
    return [-ADAM_LR * (m_hat / (jnp.sqrt(v_hat) + ADAM_EPS) + ADAM_WD * w), m, v], []


def _adamw(w, g, m, v, name):
    shape = w.shape
    cols = shape[-1] if (w.ndim >= 2 and shape[-1] >= LANES) else None
    if cols is None:
        n = int(np.prod(shape))
        cols = LANES if n % LANES == 0 else n
    two_d = (int(np.prod(shape)) // cols, cols)
    rows = [(t.reshape(two_d), cols, 0) for t in (w, g, m, v)]
    R = _pick(two_d[0], 256, 8)
    outs = _row_kernel(name, _adamw_fn, rows, [], [(cols, f32)] * 3, block_rows=R)
    return [o.reshape(shape) for o in outs]


def _shift_down(x, s, row):
    if s == 0:
        return x
    return jnp.where(row >= s, pltpu.roll(x, s, 0), 0.0)


def _shift_up(x, s, row):
    if s == 0:
        return x
    n = x.shape[0]
    return jnp.where(row < n - s, pltpu.roll(x, n - s, 0), 0.0)


def _conv_pre(x, w, b, row):
    kw = w.shape[0]
    c = b
    for k in range(kw):
        c = c + w[k:k + 1, :] * _shift_down(x, kw - 1 - k, row)
    return c


def _conv_fwd(xsrc, col0, w, b, name, bc_t=512):
    T = xsrc.shape[0]
    kw, ncols = w.shape
    bc = _pick(ncols, bc_t)
    off = col0 // bc
    assert col0 % bc == 0

    def body(x_ref, w_ref, b_ref, o_ref):
        x = x_ref[...].astype(f32)
        row = lax.broadcasted_iota(jnp.int32, x.shape, 0)
        c = _conv_pre(x, w_ref[...], b_ref[...], row)
        o_ref[...] = c * jax.nn.sigmoid(c)

    return pl.pallas_call(
        body, name=name, grid=(ncols // bc,),
        in_specs=[pl.BlockSpec((T, bc), lambda j: (0, off + j)), pl.BlockSpec((kw, bc), lambda j: (0, j)),
                  pl.BlockSpec((1, bc), lambda j: (0, j))],
        out_specs=pl.BlockSpec((T, bc), lambda j: (0, j)), out_shape=jax.ShapeDtypeStruct((T, ncols), f32),
        compiler_params=_params(("parallel",)),
    )(xsrc, w, b)


def _conv_bwd(xsrc, col0, w, b, dact, name, bc_t=512):
    T = xsrc.shape[0]
    kw, ncols = w.shape
    bc = _pick(ncols, bc_t)
    off = col0 // bc
    assert col0 % bc == 0

    def body(x_ref, w_ref, b_ref, d_ref, dx_ref, dw_ref, db_ref):
        x = x_ref[...].astype(f32)
        w = w_ref[...]
        row = lax.broadcasted_iota(jnp.int32, x.shape, 0)
        c = _conv_pre(x, w, b_ref[...], row)
        s = jax.nn.sigmoid(c)
        dc = d_ref[...].astype(f32) * (s * (1.0 + c * (1.0 - s)))
        dx = jnp.zeros_like(x)
        dws = []
        for k in range(kw):
            dx = dx + w[k:k + 1, :] * _shift_up(dc, kw - 1 - k, row)
            dws.append(jnp.sum(dc * _shift_down(x, kw - 1 - k, row), axis=0, keepdims=True))
        dx_ref[...] = dx.astype(dx_ref.dtype)
        dw_ref[...] = jnp.concatenate(dws, axis=0)
        db_ref[...] = jnp.sum(dc, axis=0, keepdims=True)

    return pl.pallas_call(
        body, name=name, grid=(ncols // bc,),
        in_specs=[pl.BlockSpec((T, bc), lambda j: (0, off + j)), pl.BlockSpec((kw, bc), lambda j: (0, j)),
                  pl.BlockSpec((1, bc), lambda j: (0, j)), pl.BlockSpec((T, bc), lambda j: (0, j))],
        out_specs=[pl.BlockSpec((T, bc), lambda j: (0, j)), pl.BlockSpec((kw, bc), lambda j: (0, j)),
                   pl.BlockSpec((1, bc), lambda j: (0, j))],
        out_shape=[jax.ShapeDtypeStruct((T, ncols), bf16), jax.ShapeDtypeStruct((kw, ncols), f32),
                   jax.ShapeDtypeStruct((1, ncols), f32)],
        compiler_params=_params(("parallel",)),
    )(xsrc, w, b, dact)


_HI = lax.Precision.HIGHEST


def _dot(a, b, dims="nn", precision=None):
    return lax.dot_general(a, b, _DIMS[dims], preferred_element_type=f32, precision=precision)


def _ssd_common(b_ref, c_ref, adt_ref):
    q = b_ref.shape[0]
    bb, cb = b_ref[...].astype(bf16), c_ref[...].astype(bf16)
    r = lax.broadcasted_iota(jnp.int32, (q, q), 0)
    s = lax.broadcasted_iota(jnp.int32, (q, q), 1)
    tril = r >= s
    trilf = tril.astype(f32)
    adt = adt_ref[...]
    cum = _dot(trilf, adt, "nn", _HI)
    cum_t = _dot(adt, (r <= s).astype(f32), "tn", _HI)
    gmat = _dot(cb, bb, "nt")
    return bb, cb, tril, trilf, cum, cum_t, gmat


def _ssd_head(h, p, x_ref, dt_ref, d_ref, cum, cum_t, tril, gmat):
    x = x_ref[:, h * p:(h + 1) * p]
    dt = dt_ref[:, h:h + 1]
    cm = cum[:, h:h + 1]
    lm = jnp.exp(jnp.where(tril, cm - cum_t[h:h + 1, :], -1e30))
    m = gmat * lm
    xdt = x * dt
    e = jnp.exp(cm)
    q = x.shape[0]
    cl = cm[q - 1:q, :]
    et = jnp.exp(cl)
    dec = jnp.exp(cl - cm)
    return x, dt, cm, lm, m, xdt, e, et, dec, d_ref[:, h:h + 1]


def _ssd_specs(q, hp, n, g_n, nc, rev):
    def cidx(c):
        return (nc - 1 - c) if rev else c
    x_spec = pl.BlockSpec((q, hp), lambda g, c: (cidx(c), g))
    boff = (g_n * hp) // n
    b_spec = pl.BlockSpec((q, n), lambda g, c: (cidx(c), boff + g))
    c_spec = pl.BlockSpec((q, n), lambda g, c: (cidx(c), boff + g_n + g))
    dt_spec = pl.BlockSpec((q, LANES), lambda g, c: (cidx(c), g))
    d_spec = pl.BlockSpec((1, LANES), lambda g, c: (0, g))
    st_spec = pl.BlockSpec((1, 1, hp, n), lambda g, c: (cidx(c), g, 0, 0))
    return x_spec, b_spec, c_spec, dt_spec, d_spec, st_spec


def _ssd_fwd(act, dt, adt, dpad, hpg, p, n, name):
    T = act.shape[0]
    g_n, q = SSD_N_GROUPS, SSD_CHUNK
    nc, hp = T // q, hpg * p
    x_spec, b_spec, c_spec, dt_spec, d_spec, st_spec = _ssd_specs(q, hp, n, g_n, nc, False)

    def body(x_ref, b_ref, c_ref, dt_ref, adt_ref, d_ref, y_ref, st_ref, s_scr):
        @pl.when(pl.program_id(1) == 0)
        def _():
            s_scr[...] = jnp.zeros_like(s_scr)

        bb, cb, tril, _, cum, cum_t, gmat = _ssd_common(b_ref, c_ref, adt_ref)
        st_ref[0, 0] = s_scr[...]
        for h in range(hpg):
            x, dt_h, cm, lm, m, xdt, e, et, dec, dsk = _ssd_head(h, p, x_ref, dt_ref, d_ref, cum, cum_t, tril, gmat)
            s0 = s_scr[h * p:(h + 1) * p, :]
            z = _dot(cb, s0.astype(bf16), "nt")
            y_ref[:, h * p:(h + 1) * p] = _dot(m.astype(bf16), xdt.astype(bf16)) + e * z + dsk * x
            s_scr[h * p:(h + 1) * p, :] = et * s0 + _dot((xdt * dec).astype(bf16), bb, "tn")

    return pl.pallas_call(
        body, name=name, grid=(g_n, nc),
        in_specs=[x_spec, b_spec, c_spec, dt_spec, dt_spec, d_spec],
        out_specs=[pl.BlockSpec((q, hp), lambda g, c: (c, g)), st_spec],
        out_shape=[jax.ShapeDtypeStruct((T, g_n * hp), f32), jax.ShapeDtypeStruct((nc, g_n, hp, n), f32)],
        scratch_shapes=[pltpu.VMEM((hp, n), f32)],
        compiler_params=_params(("parallel", "arbitrary")),
    )(act, act, act, dt, adt, dpad)


def _ssd_bwd(act, dt, adt, dpad, states, dy, hpg, p, n, name):
    T = act.shape[0]
    g_n, q = SSD_N_GROUPS, SSD_CHUNK
    nc, hp = T // q, hpg * p
    x_spec, b_spec, c_spec, dt_spec, d_spec, st_spec = _ssd_specs(q, hp, n, g_n, nc, True)

    def body(x_ref, b_ref, c_ref, dt_ref, adt_ref, d_ref, st_ref, dy_ref,
             dx_ref, db_ref, dc_ref, ddt_ref, dadt_ref, dd_ref, ds_scr):
        first = pl.program_id(1) == 0

        @pl.when(first)
        def _():
            ds_scr[...] = jnp.zeros_like(ds_scr)

        bb, cb, tril, trilf, cum, cum_t, gmat = _ssd_common(b_ref, c_ref, adt_ref)
        lane = lax.broadcasted_iota(jnp.int32, (1, LANES), 1)
        rowq = lax.broadcasted_iota(jnp.int32, (q, 1), 0)
        last_row = (rowq == q - 1).astype(f32)
        d_b = jnp.zeros((q, n), f32)
        d_c = jnp.zeros((q, n), f32)
        dcum_all = jnp.zeros((q, LANES), f32)
        ddt_all = jnp.zeros((q, LANES), f32)
        dd_row = jnp.zeros((1, LANES), f32)
        for h in range(hpg):
            x, dt_h, cm, lm, m, xdt, e, et, dec, dsk = _ssd_head(h, p, x_ref, dt_ref, d_ref, cum, cum_t, tril, gmat)
            onehot = (lane == h).astype(f32)
            s0 = st_ref[0, 0, h * p:(h + 1) * p, :]
            s0b = s0.astype(bf16)
            ds1 = ds_scr[h * p:(h + 1) * p, :]
            ds1b = ds1.astype(bf16)
            dyh = dy_ref[:, h * p:(h + 1) * p]
            dyb = dyh.astype(bf16)
            xdtb = xdt.astype(bf16)
            dd_row = dd_row + jnp.sum(jnp.sum(dyh * x, axis=1, keepdims=True), axis=0, keepdims=True) * onehot
            dm = _dot(dyb, xdtb, "nt")
            dxdt = _dot(m.astype(bf16), dyb, "tn")
            wm = dm * m
            dcum = jnp.sum(wm, axis=1, keepdims=True) - jnp.sum(wm.T, axis=1, keepdims=True)
            dgb = (dm * lm).astype(bf16)
            d_c = d_c + _dot(dgb, bb)
            d_b = d_b + _dot(dgb, cb, "tn")
            z = _dot(cb, s0b, "nt")
            dz = dyh * e
            dcum = dcum + jnp.sum(dz * z, axis=1, keepdims=True)
            dzb = dz.astype(bf16)
            d_c = d_c + _dot(dzb, s0b)
            ds0 = _dot(dzb, cb, "tn") + et * ds1
            dcl = jnp.sum(jnp.sum(ds1 * s0, axis=1, keepdims=True), axis=0, keepdims=True) * et
            dxd = _dot(bb, ds1b, "nt")
            d_b = d_b + _dot((xdt * dec).astype(bf16), ds1b)
            dxdt = dxdt + dxd * dec
            ddec = jnp.sum(dxd * xdt, axis=1, keepdims=True) * dec
            dcl = dcl + jnp.sum(ddec, axis=0, keepdims=True)
            dcum = dcum - ddec + dcl * last_row
            ds_scr[h * p:(h + 1) * p, :] = ds0
            dcum_all = dcum_all + dcum * onehot
            ddt_all = ddt_all + jnp.sum(dxdt * x, axis=1, keepdims=True) * onehot
            dx_ref[:, h * p:(h + 1) * p] = dsk * dyh + dxdt * dt_h
        db_ref[...] = d_b
        dc_ref[...] = d_c
        ddt_ref[...] = ddt_all
        dadt_ref[...] = _dot(trilf, dcum_all, "tn", _HI)
        dd8 = jnp.broadcast_to(dd_row, (8, LANES))

        @pl.when(first)
        def _():
            dd_ref[...] = dd8

        @pl.when(jnp.logical_not(first))
        def _():
            dd_ref[...] += dd8

    rc = lambda g, c: (nc - 1 - c, g)
    return pl.pallas_call(
        body, name=name, grid=(g_n, nc),
        in_specs=[x_spec, b_spec, c_spec, dt_spec, dt_spec, d_spec, st_spec, pl.BlockSpec((q, hp), rc)],
        out_specs=[pl.BlockSpec((q, hp), rc), pl.BlockSpec((q, n), rc), pl.BlockSpec((q, n), rc),
                   pl.BlockSpec((q, LANES), rc), pl.BlockSpec((q, LANES), rc), pl.BlockSpec((8, LANES), lambda g, c: (g, 0))],
        out_shape=[jax.ShapeDtypeStruct((T, g_n * hp), f32), jax.ShapeDtypeStruct((T, g_n * n), f32),
                   jax.ShapeDtypeStruct((T, g_n * n), f32), jax.ShapeDtypeStruct((T, g_n * LANES), f32),
                   jax.ShapeDtypeStruct((T, g_n * LANES), f32), jax.ShapeDtypeStruct((g_n * 8, LANES), f32)],
        scratch_shapes=[pltpu.VMEM((hp, n), f32)],
        compiler_params=_params(("parallel", "arbitrary")),
    )(act, act, act, dt, adt, dpad, states, dy)


def _s5_scan_fwd(bu, lam_re, lam_im, nsb, name, tc_t=512):
    T = bu.shape[0]
    w2 = bu.shape[1] // nsb
    w = w2 // 2
    tc = _pick(T, tc_t, 8)

    def body(bu_ref, lr_ref, li_ref, st_ref, carry):
        @pl.when(pl.program_id(1) == 0)
        def _():
            carry[...] = jnp.zeros_like(carry)

        lr, li = lr_ref[0:1, :], li_ref[0:1, :]

        def step(t, s):
            sr, si = s
            row = bu_ref[pl.ds(t, 1), :]
            nsr = lr * sr - li * si + row[:, :w]
            nsi = lr * si + li * sr + row[:, w:]
            st_ref[pl.ds(t, 1), :] = jnp.concatenate([nsr, nsi], axis=1)
            return nsr, nsi

        sr, si = lax.fori_loop(0, tc, step, (carry[0:1, :], carry[1:2, :]))
        carry[0:1, :] = sr
        carry[1:2, :] = si

    return pl.pallas_call(
        body, name=name, grid=(nsb, T // tc),
        in_specs=[pl.BlockSpec((tc, w2), lambda j, i: (i, j)), pl.BlockSpec((8, w), lambda j, i: (j, 0)),
                  pl.BlockSpec((8, w), lambda j, i: (j, 0))],
        out_specs=pl.BlockSpec((tc, w2), lambda j, i: (i, j)), out_shape=jax.ShapeDtypeStruct(bu.shape, f32),
        scratch_shapes=[pltpu.VMEM((8, w), f32)],
        compiler_params=_params(("parallel", "arbitrary")),
    )(bu, lam_re, lam_im)


def _s5_scan_bwd(gst, states, lam_re, lam_im, nsb, name, tc_t=512):
    T = gst.shape[0]
    w2 = gst.shape[1] // nsb
    w = w2 // 2
    tc = _pick(T, tc_t, 8)
    nt = T // tc

    def body(g_ref, s_ref, sp_ref, lr_ref, li_ref, a_ref, dlr_ref, dli_ref, carry):
        i = pl.program_id(1)

        @pl.when(i == 0)
        def _():
            carry[...] = jnp.zeros_like(carry)

        lr, li = lr_ref[0:1, :], li_ref[0:1, :]

        def adj(t, ar, ai):
            row = g_ref[pl.ds(t, 1), :]
            nar = row[:, :w] + lr * ar + li * ai
            nai = row[:, w:] - li * ar + lr * ai
            a_ref[pl.ds(t, 1), :] = jnp.concatenate([nar, nai], axis=1)
            return nar, nai

        def acc(prev, ar, ai, dlr, dli):
            spr, spi = prev[:, :w], prev[:, w:]
            return dlr + ar * spr + ai * spi, dli - ar * spi + ai * spr

        def step(k, c):
            ar, ai, dlr, dli = c
            t = tc - 1 - k
            ar, ai = adj(t, ar, ai)
            dlr, dli = acc(s_ref[pl.ds(t - 1, 1), :], ar, ai, dlr, dli)
            return ar, ai, dlr, dli

        c0 = (carry[0:1, :], carry[1:2, :], carry[2:3, :], carry[3:4, :])
        ar, ai, dlr, dli = lax.fori_loop(0, tc - 1, step, c0)
        ar, ai = adj(0, ar, ai)
        prev = sp_ref[tc - 1:tc, :] * (i < nt - 1).astype(f32)
        dlr, dli = acc(prev, ar, ai, dlr, dli)
        carry[0:1, :] = ar
        carry[1:2, :] = ai
        carry[2:3, :] = dlr
        carry[3:4, :] = dli
        dlr_ref[...] = jnp.broadcast_to(dlr, (8, w))
        dli_ref[...] = jnp.broadcast_to(dli, (8, w))

    cur = lambda j, i: (nt - 1 - i, j)
    prv = lambda j, i: (jnp.maximum(nt - 2 - i, 0), j)
    return pl.pallas_call(
        body, name=name, grid=(nsb, nt),
        in_specs=[pl.BlockSpec((tc, w2), cur), pl.BlockSpec((tc, w2), cur), pl.BlockSpec((tc, w2), prv),
                  pl.BlockSpec((8, w), lambda j, i: (j, 0)), pl.BlockSpec((8, w), lambda j, i: (j, 0))],
        out_specs=[pl.BlockSpec((tc, w2), cur), pl.BlockSpec((8, w), lambda j, i: (j, 0)),
                   pl.BlockSpec((8, w), lambda j, i: (j, 0))],
        out_shape=[jax.ShapeDtypeStruct(gst.shape, f32), jax.ShapeDtypeStruct((nsb * 8, w), f32),
                   jax.ShapeDtypeStruct((nsb * 8, w), f32)],
        scratch_shapes=[pltpu.VMEM((8, w), f32)],
        compiler_params=_params(("parallel", "arbitrary")),
    )(gst, states, states, lam_re, lam_im)


def _s5_prep_fn(xs, ps):
    lam_re, lam_im, log_step, b_re, b_im, expand = ps
    lr = jnp.minimum(lam_re, S5_MAX_REAL)
    li = lam_im
    step = jnp.exp(log_step)
    er = jnp.exp(lr * step)
    ang = li * step
    lbr, lbi = er * jnp.cos(ang), er * jnp.sin(ang)
    nr, ni = lbr - 1.0, lbi
    den = lr * lr + li * li
    qr, qi = (nr * lr + ni * li) / den, (ni * lr - nr * li) / den
    qre, qie = _dot(qr, expand, "nn", _HI), _dot(qi, expand, "nn", _HI)
    return [lbr, lbi, qre * b_re - qie * b_im, qre * b_im + qie * b_re]


def _s5_prep(pars, name):
    def body(*refs):
        outs = _s5_prep_fn([], [r[...] for r in refs[:6]])
        for ref, v in zip(refs[6:], outs):
            ref[...] = v

    g, nst = pars[0].shape
    nc = pars[3].shape[1]
    return pl.pallas_call(
        body, name=name,
        out_shape=[jax.ShapeDtypeStruct((g, nst), f32)] * 2 + [jax.ShapeDtypeStruct((g, nc), f32)] * 2,
        compiler_params=_params(),
    )(*pars)


def _s5_prep_bwd(pars, cots, name):
    def body(*refs):
        ps = [r[...] for r in refs[:6]]
        ct = [r[...] for r in refs[6:10]]
        _, vjp = jax.vjp(lambda *a: _s5_prep_fn([], list(a)), *ps)
        g = vjp(ct)
        for ref, v in zip(refs[10:], g[:5]):
            ref[...] = v

    return pl.pallas_call(
        body, name=name, out_shape=[jax.ShapeDtypeStruct(p.shape, f32) for p in pars[:5]], compiler_params=_params(),
    )(*pars, *cots)


_ANY = pl.BlockSpec(memory_space=pl.ANY)


def _chip_all_gather(block, name):
    def body(src, out, send_sems, recv_sems, local_sem):
        x, y, c = lax.axis_index("x"), lax.axis_index("y"), lax.axis_index("c")
        me = 2 * x + y
        local = pltpu.make_async_copy(src, out.at[me], local_sem)
        local.start()
        chips = [(1 - x, y), (x, 1 - y), (1 - x, 1 - y)]
        sends = []
        for j, (px, py) in enumerate(chips):
            cp = pltpu.make_async_remote_copy(src_ref=src, dst_ref=out.at[me], send_sem=send_sems.at[j],
                                              recv_sem=recv_sems.at[j], device_id=(px, py, c), device_id_type=MESH)
            cp.start()
            sends.append(cp)
        for j, (px, py) in enumerate(chips):
            pltpu.make_async_remote_copy(src_ref=src, dst_ref=out.at[2 * px + py], send_sem=send_sems.at[j],
                                         recv_sem=recv_sems.at[j], device_id=(px, py, c), device_id_type=MESH).wait_recv()
        for cp in sends:
            cp.wait_send()
        local.wait()

    return pl.pallas_call(
        body, name=name, in_specs=[_ANY], out_specs=_ANY,
        out_shape=jax.ShapeDtypeStruct((N_CHIPS,) + block.shape, block.dtype),
        scratch_shapes=[pltpu.SemaphoreType.DMA((3,)), pltpu.SemaphoreType.DMA((3,)), pltpu.SemaphoreType.DMA(())],
    )(block)


def _chip_scatter(parts, name):
    def body(src, out, send_sems, recv_sems, local_sem):
        x, y, c = lax.axis_index("x"), lax.axis_index("y"), lax.axis_index("c")
        me = 2 * x + y
        local = pltpu.make_async_copy(src.at[me], out.at[me], local_sem)
        local.start()
        chips = [(1 - x, y), (x, 1 - y), (1 - x, 1 - y)]
        sends = []
        for j, (px, py) in enumerate(chips):
            cp = pltpu.make_async_remote_copy(src_ref=src.at[2 * px + py], dst_ref=out.at[me], send_sem=send_sems.at[j],
                                              recv_sem=recv_sems.at[j], device_id=(px, py, c), device_id_type=MESH)
            cp.start()
            sends.append(cp)
        for j, (px, py) in enumerate(chips):
            pltpu.make_async_remote_copy(src_ref=src.at[me], dst_ref=out.at[2 * px + py], send_sem=send_sems.at[j],
                                         recv_sem=recv_sems.at[j], device_id=(px, py, c), device_id_type=MESH).wait_recv()
        for cp in sends:
            cp.wait_send()
        local.wait()

    return pl.pallas_call(
        body, name=name, in_specs=[_ANY], out_specs=_ANY, out_shape=jax.ShapeDtypeStruct(parts.shape, parts.dtype),
        scratch_shapes=[pltpu.SemaphoreType.DMA((3,)), pltpu.SemaphoreType.DMA((3,)), pltpu.SemaphoreType.DMA(())],
    )(parts)


def _core_exchange(block, name):
    def body(src, out, send_sem, recv_sem, local_sem):
        x, y, c = lax.axis_index("x"), lax.axis_index("y"), lax.axis_index("c")
        local = pltpu.make_async_copy(src, out.at[c], local_sem)
        local.start()
        cp = pltpu.make_async_remote_copy(src_ref=src, dst_ref=out.at[c], send_sem=send_sem, recv_sem=recv_sem,
                                          device_id=(x, y, 1 - c), device_id_type=MESH)
        cp.start()
        pltpu.make_async_remote_copy(src_ref=src, dst_ref=out.at[1 - c], send_sem=send_sem, recv_sem=recv_sem,
                                     device_id=(x, y, 1 - c), device_id_type=MESH).wait_recv()
        cp.wait_send()
        local.wait()

    return pl.pallas_call(
        body, name=name, in_specs=[_ANY], out_specs=_ANY,
        out_shape=jax.ShapeDtypeStruct((N_CORES,) + block.shape, block.dtype),
        scratch_shapes=[pltpu.SemaphoreType.DMA(()), pltpu.SemaphoreType.DMA(()), pltpu.SemaphoreType.DMA(())],
    )(block)


def _device_all_gather(block, name):
    def body(src, out, send_sems, recv_sems, local_sem):
        x, y, c = lax.axis_index("x"), lax.axis_index("y"), lax.axis_index("c")
        me = 4 * x + 2 * y + c
        local = pltpu.make_async_copy(src, out.at[me], local_sem)
        local.start()
        peers = []
        for mask in range(1, N_DEV):
            fx, fy, fc = (mask >> 2) & 1, (mask >> 1) & 1, mask & 1
            peers.append((x ^ fx, y ^ fy, c ^ fc))
        sends = []
        for j, peer in enumerate(peers):
            cp = pltpu.make_async_remote_copy(src_ref=src, dst_ref=out.at[me], send_sem=send_sems.at[j],
                                              recv_sem=recv_sems.at[j], device_id=peer, device_id_type=MESH)
            cp.start()
            sends.append(cp)
        for j, (px, py, pc) in enumerate(peers):
            pltpu.make_async_remote_copy(src_ref=src, dst_ref=out.at[4 * px + 2 * py + pc], send_sem=send_sems.at[j],
                                         recv_sem=recv_sems.at[j], device_id=(px, py, pc), device_id_type=MESH).wait_recv()
        for cp in sends:
            cp.wait_send()
        local.wait()

    return pl.pallas_call(
        body, name=name, in_specs=[_ANY], out_specs=_ANY,
        out_shape=jax.ShapeDtypeStruct((N_DEV,) + block.shape, block.dtype),
        scratch_shapes=[pltpu.SemaphoreType.DMA((N_DEV - 1,)), pltpu.SemaphoreType.DMA((N_DEV - 1,)),
                        pltpu.SemaphoreType.DMA(())],
    )(block)


def _sum_slots(stack, name, block_rows=256):
    s_n, r_n, c_n = stack.shape
    br = _pick(r_n, block_rows, 8)

    def body(in_ref, o_ref):
        acc = in_ref[0].astype(f32)
        for s in range(1, s_n):
            acc = acc + in_ref[s].astype(f32)
        o_ref[...] = acc

    return pl.pallas_call(
        body, name=name, grid=(r_n // br,), in_specs=[pl.BlockSpec((s_n, br, c_n), lambda i: (0, i, 0))],
        out_specs=pl.BlockSpec((br, c_n), lambda i: (i, 0)), out_shape=jax.ShapeDtypeStruct((r_n, c_n), f32),
        compiler_params=_params(("parallel",)),
    )(stack)


def _pad_rows(a, mult):
    r = (-a.shape[0]) % mult
    return a if r == 0 else jnp.concatenate([a, jnp.zeros((r,) + a.shape[1:], a.dtype)], axis=0)


def _shard_axis(name):
    return 2 if name in COL_SHARDED else 1


def _pack_weights(w, conv_w):
    parts = [w[n].astype(bf16).reshape(-1, PACK_COLS) for n in BIG]
    parts.append(lax.bitcast_convert_type(conv_w, bf16).reshape(-1, PACK_COLS))
    return _pad_rows(jnp.concatenate(parts, axis=0), PACK_ROW_MULT)


def _unpack_weights(full, w, conv_w):
    out, r0 = {}, 0
    for n in BIG:
        rows = w[n].size // PACK_COLS
        pieces = full[:, r0:r0 + rows].reshape((N_CHIPS,) + w[n].shape)
        out[n] = jnp.concatenate([pieces[k] for k in range(N_CHIPS)], axis=_shard_axis(n))
        r0 += rows
    rows = conv_w.size * 2 // PACK_COLS
    pieces = lax.bitcast_convert_type(full[:, r0:r0 + rows].reshape((N_CHIPS,) + conv_w.shape + (2,)), f32)
    return out, jnp.concatenate([pieces[k] for k in range(N_CHIPS)], axis=2)


def _pack_big_grads(g):
    slots = []
    for k in range(N_CHIPS):
        parts = []
        for n in BIG:
            ax = _shard_axis(n)
            width = g[n].shape[ax] // N_CHIPS
            parts.append(lax.slice_in_dim(g[n], k * width, (k + 1) * width, axis=ax).astype(bf16).reshape(-1, PACK_COLS))
        slots.append(_pad_rows(jnp.concatenate(parts, axis=0), PACK_ROW_MULT))
    return jnp.stack(slots)


def _unpack_big_grads(summed, w):
    out, r0 = {}, 0
    for n in BIG:
        rows = w[n].size // PACK_COLS
        out[n] = summed[r0:r0 + rows].reshape(w[n].shape)
        r0 += rows
    return out


def _pack_small(vals, names):
    parts = []
    for n in names:
        flat = vals[n].reshape(-1)
        pad = (-flat.size) % LANES
        if pad:
            flat = jnp.concatenate([flat, jnp.zeros((pad,), flat.dtype)])
        parts.append(flat.reshape(-1, LANES))
    return _pad_rows(jnp.concatenate(parts, axis=0), PACK_ROW_MULT)


def _unpack_small(packed, like, names):
    out, r0 = {}, 0
    for n in names:
        size = like[n].size
        rows = -(-size // LANES)
        out[n] = packed[r0:r0 + rows].reshape(-1)[:size].reshape(like[n].shape)
        r0 += rows
    return out


def _dims(w, x):
    d = {}
    d["D"] = x.shape[-1]
    d["T"] = x.shape[-2]
    d["DI"] = w["ssd_norm_g"].shape[-1]
    d["NH"] = w["ssd_dt_bias"].shape[-1]
    d["CD"] = w["ssd_conv_b"].shape[-1]
    d["G"] = SSD_N_GROUPS
    d["HPG"] = d["NH"] // d["G"]
    d["P"] = d["DI"] // d["NH"]
    d["N"] = (d["CD"] - d["DI"]) // (2 * d["G"])
    d["S5G"], d["S5N"] = w["s5_lambda_re"].shape[-2:]
    d["S5C"] = w["s5_b_re"].shape[-1]
    d["S5W"] = d["S5G"] * d["S5C"]
    d["NSB"] = d["S5W"] // S5_SUPERBLOCK
    d["GSB"] = d["S5G"] // d["NSB"]
    return d


def _head_pad(v, d):
    lead = v.shape[:-1]
    v = v.reshape(lead + (d["G"], d["HPG"]))
    v = jnp.concatenate([v, jnp.zeros(lead + (d["G"], LANES - d["HPG"]), v.dtype)], axis=-1)
    return v.reshape(lead + (d["G"] * LANES,))


def _head_unpad(v, d):
    lead = v.shape[:-1]
    return v.reshape(lead + (d["G"], LANES))[..., :d["HPG"]].reshape(lead + (d["NH"],))


def _w_in_perm(w_in, d):
    o = d["DI"] + d["CD"]
    return jnp.concatenate([w_in[:, :o], w_in[:, o + d["NH"]:], _head_pad(w_in[:, o:o + d["NH"]], d)], axis=1)


def _w_in_unperm(g, d):
    o = d["DI"] + d["CD"]
    rest = d["S5W"] + 2 * d["D"]
    return jnp.concatenate([g[:, :o], _head_unpad(g[:, o + rest:], d), g[:, o:o + rest]], axis=1)


def _s5_block_diag(v, d):
    nsb, gsb = d["NSB"], d["GSB"]
    a, b = v.shape[1:]
    v = v.reshape(nsb, gsb, a, b)
    eye = jnp.eye(gsb, dtype=v.dtype)
    return (v[:, :, :, None, :] * eye[None, :, None, :, None]).reshape(nsb * gsb * a, gsb * b)


def _s5_diag_blocks(m, d, a, b):
    nsb, gsb = d["NSB"], d["GSB"]
    m = m.reshape(nsb, gsb, a, gsb, b)
    eye = jnp.eye(gsb, dtype=m.dtype)
    return jnp.sum(m * eye[None, :, None, :, None], axis=3).reshape(nsb * gsb, a, b)


def _s5_lam_rows(v, d):
    v = v.reshape(d["NSB"], 1, d["GSB"] * d["S5N"])
    return jnp.broadcast_to(v, (d["NSB"], 8, v.shape[-1])).reshape(d["NSB"] * 8, -1)


def _ffn_fwd(h, pre_g, post_g, wgu, wd, tag):
    D = h.shape[1]
    H2 = wgu.shape[1]
    xn = _row_kernel(f"{tag}_norm", _fwd_of(_f_norm), [(h, D, 0)], [pre_g], [(D, bf16)])[0]
    ab = _mm(xn, wgu, "nn", f32, f"{tag}_mm_up")
    hid = _row_kernel(f"{tag}_swiglu", _swiglu_fwd, [(ab, H2, 0)], [], [(H2 // 2, bf16)])[0]
    f = _mm(hid, wd, "nn", f32, f"{tag}_mm_down")
    out = _row_kernel(f"{tag}_resnorm", _fwd_of(_f_resnorm(0.5)), [(h, D, 0), (f, D, 0)], [post_g], [(D, f32)])[0]
    return out, dict(h=h, xn=xn, ab=ab, hid=hid, f=f)


def _ffn_bwd(dh_out, s, pre_g, post_g, wgu, wd, tag):
    D = dh_out.shape[1]
    H2 = wgu.shape[1]
    df, dpost = _row_kernel(f"{tag}_resnorm_bwd", _vjp_of(_f_post(0.5), 1, 1, [0]), [(s["f"], D, 0), (dh_out, D, 0)],
                            [post_g], [(D, bf16)], [post_g.shape])
    dwd = _mm(s["hid"], df, "tn", f32, f"{tag}_mm_dwd")
    dhid = _mm(df, wd, "nt", bf16, f"{tag}_mm_dhid")
    dab = _row_kernel(f"{tag}_swiglu_bwd", _swiglu_bwd, [(s["ab"], H2, 0), (dhid, H2 // 2, 0)], [], [(H2, bf16)])[0]
    dwgu = _mm(s["xn"], dab, "tn", f32, f"{tag}_mm_dwgu")
    dxn = _mm(dab, wgu, "nt", f32, f"{tag}_mm_dxn")
    dh, dpre = _row_kernel(f"{tag}_norm_bwd", _vjp_of(_f_norm, 1, 1, [0], 1), [(s["h"], D, 0), (dxn, D, 0), (dh_out, D, 0)],
                           [pre_g], [(D, f32)], [pre_g.shape])
    return dh, dict(pre_g=dpre, post_g=dpost, wgu=dwgu, wd=dwd)


def _mixer_fwd(h, p, d):
    D, DI, CD, G, N = d["D"], d["DI"], d["CD"], d["G"], d["N"]
    gl = G * LANES
    c_u5, c_ga, c_gb, c_dt = DI + CD, DI + CD + d["S5W"], DI + CD + d["S5W"] + D, DI + CD + d["S5W"] + 2 * D
    u = _row_kernel("mix_norm", _fwd_of(_f_norm), [(h, D, 0)], [p["mix_pre_g"]], [(D, bf16)])[0]
    proj = _mm(u, p["w_in"], "nn", f32, "mix_mm_in", bn_t=512)
    act = _conv_fwd(proj, DI, p["conv_w"], p["conv_b"], "ssd_conv")
    dt, adt = _row_kernel("ssd_dt", _fwd_of(_f_dt), [(proj, gl, c_dt // gl)], [p["dt_bias"], p["a_log"]], [(gl, f32)] * 2)
    y_ssd, states = _ssd_fwd(act, dt, adt, p["d_skip"], d["HPG"], d["P"], N, "ssd_scan")
    nrm = _row_kernel("ssd_post", _fwd_of(_f_ssdpost(G)), [(y_ssd, DI, 0), (proj, DI, 0)], [p["norm_g"]], [(DI, bf16)])[0]
    y_a = _mm(nrm, p["w_a"], "nn", f32, "mix_mm_a")
    u5 =(proj, d["S5W"], c_u5 // d["S5W"])
    bu = _s5_in(proj, c_u5, p["bsb"], d)
    s5st = _s5_scan_fwd(bu, p["lam_re_rows"], p["lam_im_rows"], d["NSB"], "s5_scan")
    y5 = _bdmm(s5st, p["csb"], "nn", d["NSB"], f32, "s5_mm_c")
    gel = _row_kernel("s5_post", _fwd_of(_f_s5post), [(y5, d["S5W"], 0), u5], [p["s5_d"]], [(d["S5W"], bf16)])[0]
    vg = _mm(gel, p["w_glu"], "nn", f32, "mix_mm_glu")
    glu = _row_kernel("s5_glu", _glu_fwd, [(vg, vg.shape[1], 0)], [], [(vg.shape[1] // 2, bf16)])[0]
    y_b = _mm(glu, p["w_b"], "nn", f32, "mix_mm_b")
    merged = _row_kernel("mix_merge", _fwd_of(_f_merge), [(proj, D, c_ga // D), (y_a, D, 0), (proj, D, c_gb // D), (y_b, D, 0)],
                         [], [(D, bf16)])[0]
    m = _mm(merged, p["w_out"], "nn", f32, "mix_mm_out")
    out = _row_kernel("mix_resnorm", _fwd_of(_f_resnorm(1.0)), [(h, D, 0), (m, D, 0)], [p["mix_post_g"]], [(D, f32)])[0]
    return out, dict(h=h, u=u, proj=proj, act=act, dt=dt, adt=adt, states=states, y_ssd=y_ssd, nrm=nrm, y_a=y_a, s5st=s5st,
                     y5=y5, gel=gel, vg=vg, glu=glu, y_b=y_b, merged=merged, m=m)


def _s5_in(proj, c_u5, bsb, d):
    T = proj.shape[0]
    nsb = d["NSB"]
    ka, nw = S5_SUPERBLOCK, bsb.shape[1]
    off = c_u5 // ka
    assert c_u5 % ka == 0
    bt = _pick(T, 512)

    def body(a_ref, w_ref, o_ref):
        o_ref[...] = _dot(a_ref[...].astype(bf16), w_ref[...].astype(bf16))

    return pl.pallas_call(
        body, name="s5_mm_bu", grid=(nsb, T // bt),
        in_specs=[pl.BlockSpec((bt, ka), lambda j, i: (i, off + j)), pl.BlockSpec((ka, nw), lambda j, i: (j, 0))],
        out_specs=pl.BlockSpec((bt, nw), lambda j, i: (i, j)), out_shape=jax.ShapeDtypeStruct((T, nsb * nw), f32),
        compiler_params=_params(("parallel", "parallel")),
    )(proj, bsb)


def _s5_dbsb(proj, c_u5, a, d):
    T = proj.shape[0]
    nsb = d["NSB"]
    ka, nw = S5_SUPERBLOCK, a.shape[1] // nsb
    off = c_u5 // ka
    bt = _pick(T, 512)

    def body(u_ref, a_ref, o_ref):
        pr = _dot(u_ref[...].astype(bf16), a_ref[...].astype(bf16), "tn")
        k = pl.program_id(1)

        @pl.when(k == 0)
        def _():
            o_ref[...] = pr

        @pl.when(k > 0)
        def _():
            o_ref[...] += pr

    return pl.pallas_call(
        body, name="s5_mm_dbsb", grid=(nsb, T // bt),
        in_specs=[pl.BlockSpec((bt, ka), lambda j, k: (k, off + j)), pl.BlockSpec((bt, nw), lambda j, k: (k, j))],
        out_specs=pl.BlockSpec((ka, nw), lambda j, k: (j, 0)), out_shape=jax.ShapeDtypeStruct((nsb * ka, nw), f32),
        compiler_params=_params(("parallel", "arbitrary")),
    )(proj, a)


def _mixer_bwd(dh_out, s, p, d):
    D, DI, CD, G, N, S5W = d["D"], d["DI"], d["CD"], d["G"], d["N"], d["S5W"]
    gl = G * LANES
    gn = G * N
    c_u5, c_ga, c_gb, c_dt = DI + CD, DI + CD + S5W, DI + CD + S5W + D, DI + CD + S5W + 2 * D
    proj = s["proj"]
    g = {}
    dm, g["mix_post_g"] = _row_kernel("mix_resnorm_bwd", _vjp_of(_f_post(1.0), 1, 1, [0]), [(s["m"], D, 0), (dh_out, D, 0)],
                                      [p["mix_post_g"]], [(D, bf16)], [p["mix_post_g"].shape])
    g["w_out"] = _mm(s["merged"], dm, "tn", f32, "mix_mm_dwout")
    dmerged = _mm(dm, p["w_out"], "nt", f32, "mix_mm_dmerged")
    dga, dya, dgb, dyb = _row_kernel(
        "mix_merge_bwd", _vjp_of(_f_merge, 4, 1, [0, 1, 2, 3]),
        [(proj, D, c_ga // D), (s["y_a"], D, 0), (proj, D, c_gb // D), (s["y_b"], D, 0), (dmerged, D, 0)], [],
        [(D, bf16), (D, bf16), (D, bf16), (D, bf16)])
    g["w_a"] = _mm(s["nrm"], dya, "tn", f32, "mix_mm_dwa")
    dnrm = _mm(dya, p["w_a"], "nt", f32, "mix_mm_dnrm")
    dy_ssd, dz, g["norm_g"] = _row_kernel(
        "ssd_post_bwd", _vjp_of(_f_ssdpost(G), 2, 1, [0, 1]), [(s["y_ssd"], DI, 0), (proj, DI, 0), (dnrm, DI, 0)],
        [p["norm_g"]], [(DI, f32), (DI, bf16)], [p["norm_g"].shape])
    dxs, d_b, d_c, ddt, dadt, dd = _ssd_bwd(s["act"], s["dt"], s["adt"], p["d_skip"], s["states"], dy_ssd,
                                            d["HPG"], d["P"], N, "ssd_scan_bwd")
    g["d_skip"] = dd.reshape(G, 8, LANES)[:, 0, :].reshape(1, gl)
    ddt_raw, g["dt_bias"], g["a_log"] = _row_kernel(
        "ssd_dt_bwd", _vjp_of(_f_dt, 1, 2, [0]), [(proj, gl, c_dt // gl), (ddt, gl, 0), (dadt, gl, 0)],
        [p["dt_bias"], p["a_log"]], [(gl, bf16)], [p["dt_bias"].shape, p["a_log"].shape])
    cw, cb = p["conv_w"], p["conv_b"]
    dxc_x, dw_x, db_x = _conv_bwd(proj, DI, cw[:, :DI], cb[:, :DI], dxs, "ssd_conv_bwd_x")
    dxc_b, dw_b, db_b = _conv_bwd(proj, 2 * DI, cw[:, DI:DI + gn], cb[:, DI:DI + gn], d_b, "ssd_conv_bwd_b")
    dxc_c, dw_c, db_c = _conv_bwd(proj, 2 * DI + gn, cw[:, DI + gn:], cb[:, DI + gn:], d_c, "ssd_conv_bwd_c")
    g["conv_w"] = jnp.concatenate([dw_x, dw_b, dw_c], axis=1)
    g["conv_b"] = jnp.concatenate([db_x, db_b, db_c], axis=1)
    g["w_b"] = _mm(s["glu"], dyb, "tn", f32, "mix_mm_dwb")
    dglu = _mm(dyb, p["w_b"], "nt", f32, "mix_mm_dglu")
    dvg = _row_kernel("s5_glu_bwd", _glu_bwd, [(s["vg"], s["vg"].shape[1], 0), (dglu, S5W, 0)], [], [(s["vg"].shape[1], bf16)])[0]
    g["w_glu"] = _mm(s["gel"], dvg, "tn", f32, "mix_mm_dwglu")
    dgel = _mm(dvg, p["w_glu"], "nt", f32, "mix_mm_dgel")
    dy5, du5a, g["s5_d"] = _row_kernel(
        "s5_post_bwd", _vjp_of(_f_s5post, 2, 1, [0, 1]), [(s["y5"], S5W, 0), (proj, S5W, c_u5 // S5W), (dgel, S5W, 0)],
        [p["s5_d"]], [(S5W, bf16), (S5W, f32)], [p["s5_d"].shape])
    g["csb"] = _bdmm(s["s5st"], dy5, "tn", d["NSB"], f32, "s5_mm_dcsb")
    gst = _bdmm(dy5, p["csb"], "nt", d["NSB"], f32, "s5_mm_gst")
    a, g["lam_re_rows"], g["lam_im_rows"] = _s5_scan_bwd(gst, s["s5st"], p["lam_re_rows"], p["lam_im_rows"], d["NSB"], "s5_scan_bwd")
    g["bsb"] = _s5_dbsb(proj, c_u5, a, d)
    du5b = _bdmm(a, p["bsb"], "nt", d["NSB"], f32, "s5_mm_du5")
    du5 = _row_kernel("s5_du5", _add_fn, [(du5a, S5W, 0), (du5b, S5W, 0)], [], [(S5W, bf16)])[0]
    dproj = jnp.concatenate([dz, dxc_x, dxc_b, dxc_c, du5, dga, dgb, ddt_raw], axis=1)
    g["w_in"] = _mm(s["u"], dproj, "tn", f32, "mix_mm_dwin", bn_t=512)
    du = _mm(dproj, p["w_in"], "nt", f32, "mix_mm_du", bk_t=2176)
    dh, g["mix_pre_g"] = _row_kernel("mix_norm_bwd", _vjp_of(_f_norm, 1, 1, [0], 1), [(s["h"], D, 0), (du, D, 0), (dh_out, D, 0)],
                                     [p["mix_pre_g"]], [(D, f32)], [p["mix_pre_g"].shape])
    return dh, g


def _layer_params(l, w, wf, conv_w_full, d):
    r2 = lambda v: v[l].reshape(1, -1)
    p = {}
    for n in ["ffn1_pre_g", "ffn1_post_g", "mix_pre_g", "mix_post_g", "ffn2_pre_g", "ffn2_post_g", "s5_d"]:
        p[n] = r2(w[n])
    p["wgu1"] = jnp.concatenate([wf["ffn1_w_gate"][l], wf["ffn1_w_up"][l]], axis=1)
    p["wd1"] = wf["ffn1_w_down"][l]
    p["wgu2"] = jnp.concatenate([wf["ffn2_w_gate"][l], wf["ffn2_w_up"][l]], axis=1)
    p["wd2"] = wf["ffn2_w_down"][l]
    p["w_in"] = _w_in_perm(wf["w_in"][l], d)
    p["w_a"], p["w_glu"], p["w_b"], p["w_out"] = wf["w_branch_a"][l], wf["s5_w_glu"][l], wf["w_branch_b"][l], wf["w_out"][l]
    p["conv_w"] = conv_w_full[l]
    p["conv_b"] = r2(w["ssd_conv_b"])
    p["dt_bias"] = _head_pad(r2(w["ssd_dt_bias"]), d)
    p["a_log"] = _head_pad(r2(w["ssd_a_log"]), d)
    p["d_skip"] = _head_pad(r2(w["ssd_d"]), d)
    p["norm_g"] = r2(w["ssd_norm_g"])
    g5, n5, c5 = d["S5G"], d["S5N"], d["S5C"]
    expand = jnp.repeat(jnp.eye(n5, dtype=f32), c5, axis=1)
    prep_in = [w["s5_lambda_re"][l], w["s5_lambda_im"][l], w["s5_log_step"][l].reshape(g5, 1),
               w["s5_b_re"][l].reshape(g5, n5 * c5), w["s5_b_im"][l].reshape(g5, n5 * c5), expand]
    lbr, lbi, bbr, bbi = _s5_prep(prep_in, "s5_prep")
    p["s5_prep_in"] = prep_in
    p["lam_re_rows"], p["lam_im_rows"] = _s5_lam_rows(lbr, d), _s5_lam_rows(lbi, d)
    to_cn = lambda v: v.reshape(g5, n5, c5).transpose(0, 2, 1)
    p["bsb"] = jnp.concatenate([_s5_block_diag(to_cn(bbr), d), _s5_block_diag(to_cn(bbi), d)], axis=1).astype(bf16)
    c_re, c_im = w["s5_c_re"][l].transpose(0, 2, 1), w["s5_c_im"][l].transpose(0, 2, 1)
    nsb = d["NSB"]
    csb = jnp.stack([_s5_block_diag(c_re, d).reshape(nsb, -1, S5_SUPERBLOCK),
                     _s5_block_diag(-c_im, d).reshape(nsb, -1, S5_SUPERBLOCK)], axis=1)
    p["csb"] = csb.reshape(-1, S5_SUPERBLOCK).astype(bf16)
    return p


def _s5_param_grads(g, p, d, l):
    g5, n5, c5, nsb, gsb = d["S5G"], d["S5N"], d["S5C"], d["NSB"], d["GSB"]
    wst = gsb * n5
    dbsb = g["bsb"]
    from_cn = lambda v: v.transpose(0, 2, 1).reshape(g5, n5 * c5)
    dbbr = from_cn(_s5_diag_blocks(dbsb[:, :wst], d, c5, n5))
    dbbi = from_cn(_s5_diag_blocks(dbsb[:, wst:], d, c5, n5))
    rows = lambda v: v.reshape(nsb, 8, wst)[:, 0, :].reshape(g5, n5)
    cots = [rows(g["lam_re_rows"]), rows(g["lam_im_rows"]), dbbr, dbbi]
    dlr, dli, dls, dbr, dbi = _s5_prep_bwd(p["s5_prep_in"], cots, "s5_prep_bwd")
    dcsb = g["csb"].reshape(nsb, 2, wst, S5_SUPERBLOCK)
    dcr = _s5_diag_blocks(dcsb[:, 0].reshape(-1, S5_SUPERBLOCK), d, n5, c5).transpose(0, 2, 1)
    dci = -_s5_diag_blocks(dcsb[:, 1].reshape(-1, S5_SUPERBLOCK), d, n5, c5).transpose(0, 2, 1)
    return dict(s5_lambda_re=dlr, s5_lambda_im=dli, s5_log_step=dls.reshape(g5), s5_b_re=dbr.reshape(g5, n5, c5),
                s5_b_im=dbi.reshape(g5, n5, c5), s5_c_re=dcr, s5_c_im=dci)


def kernel(x, ffn1_pre_g, ffn1_post_g, ffn1_w_gate, ffn1_w_up, ffn1_w_down, mix_pre_g, mix_post_g, w_in, ssd_conv_w, ssd_conv_b, ssd_dt_bias, ssd_a_log, ssd_d, ssd_norm_g, w_branch_a, s5_lambda_re, s5_lambda_im, s5_b_re, s5_b_im, s5_c_re, s5_c_im, s5_log_step, s5_d, s5_w_glu, w_branch_b, w_out, ffn2_pre_g, ffn2_post_g, ffn2_w_gate, ffn2_w_up, ffn2_w_down, loss_target, m_ffn1_pre_g, m_ffn1_post_g, m_ffn1_w_gate, m_ffn1_w_up, m_ffn1_w_down, m_mix_pre_g, m_mix_post_g, m_w_in, m_ssd_conv_w, m_ssd_conv_b, m_ssd_dt_bias, m_ssd_a_log, m_ssd_d, m_ssd_norm_g, m_w_branch_a, m_s5_lambda_re, m_s5_lambda_im, m_s5_b_re, m_s5_b_im, m_s5_c_re, m_s5_c_im, m_s5_log_step, m_s5_d, m_s5_w_glu, m_w_branch_b, m_w_out, m_ffn2_pre_g, m_ffn2_post_g, m_ffn2_w_gate, m_ffn2_w_up, m_ffn2_w_down, v_ffn1_pre_g, v_ffn1_post_g, v_ffn1_w_gate, v_ffn1_w_up, v_ffn1_w_down, v_mix_pre_g, v_mix_post_g, v_w_in, v_ssd_conv_w, v_ssd_conv_b, v_ssd_dt_bias, v_ssd_a_log, v_ssd_d, v_ssd_norm_g, v_w_branch_a, v_s5_lambda_re, v_s5_lambda_im, v_s5_b_re, v_s5_b_im, v_s5_c_re, v_s5_c_im, v_s5_log_step, v_s5_d, v_s5_w_glu, v_w_branch_b, v_w_out, v_ffn2_pre_g, v_ffn2_post_g, v_ffn2_w_gate, v_ffn2_w_up, v_ffn2_w_down):
    given = dict(locals())
    w = {n: given[n] for n in WEIGHTS}
    mom = {n: given["m_" + n] for n in WEIGHTS}
    var = {n: given["v_" + n] for n in WEIGHTS}
    d = _dims(w, x)
    n_layers = w["ffn1_pre_g"].shape[0]
    T, D = d["T"], d["D"]

    gathered = _chip_all_gather(_pack_weights(w, w["ssd_conv_w"]), "gather_weights")
    wf, conv_w_full = _unpack_weights(gathered, w, w["ssd_conv_w"])
    layers = [_layer_params(l, w, wf, conv_w_full, d) for l in range(n_layers)]

    h = x.reshape(T, D)
    saved = []
    for p in layers:
        h, s1 = _ffn_fwd(h, p["ffn1_pre_g"], p["ffn1_post_g"], p["wgu1"], p["wd1"], "ffn1")
        h, sm = _mixer_fwd(h, p, d)
        h, s2 = _ffn_fwd(h, p["ffn2_pre_g"], p["ffn2_post_g"], p["wgu2"], p["wd2"], "ffn2")
        saved.append((s1, sm, s2))
    dh, loss_part = _row_kernel("loss", _loss_fn, [(h, D, 0), (loss_target.reshape(T, D), D, 0)], [], [(D, f32)], [(8, LANES)])
    loss = lax.psum(loss_part[0, 0], ("x", "y", "c"))

    lg = [None] * n_layers
    for l in reversed(range(n_layers)):
        p = layers[l]
        s1, sm, s2 = saved[l]
        dh, g2 = _ffn_bwd(dh, s2, p["ffn2_pre_g"], p["ffn2_post_g"], p["wgu2"], p["wd2"], "ffn2")
        dh, gm = _mixer_bwd(dh, sm, p, d)
        dh, g1 = _ffn_bwd(dh, s1, p["ffn1_pre_g"], p["ffn1_post_g"], p["wgu1"], p["wd1"], "ffn1")
        H = p["wd1"].shape[0]
        gl = dict(ffn1_pre_g=g1["pre_g"], ffn1_post_g=g1["post_g"], ffn1_w_gate=g1["wgu"][:, :H], ffn1_w_up=g1["wgu"][:, H:],
                  ffn1_w_down=g1["wd"], ffn2_pre_g=g2["pre_g"], ffn2_post_g=g2["post_g"], ffn2_w_gate=g2["wgu"][:, :H],
                  ffn2_w_up=g2["wgu"][:, H:], ffn2_w_down=g2["wd"], mix_pre_g=gm["mix_pre_g"], mix_post_g=gm["mix_post_g"],
                  w_in=_w_in_unperm(gm["w_in"], d), ssd_conv_w=gm["conv_w"], ssd_conv_b=gm["conv_b"],
                  ssd_dt_bias=_head_unpad(gm["dt_bias"], d), ssd_a_log=_head_unpad(gm["a_log"], d),
                  ssd_d=_head_unpad(gm["d_skip"], d), ssd_norm_g=gm["norm_g"], w_branch_a=gm["w_a"], s5_d=gm["s5_d"],
                  s5_w_glu=gm["w_glu"], w_branch_b=gm["w_b"], w_out=gm["w_out"])
        gl.update(_s5_param_grads(gm, p, d, l))
        lg[l] = gl
    grad_x = dh.reshape(x.shape)
    full_shape = {n: w[n].shape[1:] for n in WEIGHTS}
    for n in BIG + ["ssd_conv_w"]:
        ax = _shard_axis(n) - 1 if n in BIG else 1
        full_shape[n] = tuple(s * N_CHIPS if i == ax else s for i, s in enumerate(w[n].shape[1:]))
    part = {n: jnp.stack([lg[l][n].reshape(full_shape[n]) for l in range(n_layers)]) for n in WEIGHTS}

    from_chips = _chip_scatter(_pack_big_grads(part), "scatter_grads")
    core_sum = _sum_slots(from_chips, "sum_chip_parts")
    pair = _core_exchange(core_sum, "exchange_core_sums")
    grads = _unpack_big_grads(_sum_slots(pair, "sum_core_parts"), w)
    small_names = SMALL + ["ssd_conv_w"]
    everyone = _device_all_gather(_pack_small(part, small_names), "gather_small_grads")
    small = _unpack_small(_sum_slots(everyone, "sum_small_grads"), part, small_names)
    k_me = 2 * lax.axis_index("x") + lax.axis_index("y")
    cw = w["ssd_conv_w"].shape[-1]
    small["ssd_conv_w"] = lax.dynamic_slice_in_dim(small["ssd_conv_w"], k_me * cw, cw, axis=2)
    grads.update(small)

    delta, new_m, new_v = {}, {}, {}
    for n in BIG + ["ssd_conv_w"]:
        delta[n], new_m[n], new_v[n] = _adamw(w[n], grads[n], mom[n], var[n], "adamw_" + n)
    packs = [_pack_small(t, SMALL) for t in (w, grads, mom, var)]
    sd, sm_, sv = _adamw(*packs, "adamw_small")
    delta.update(_unpack_small(sd, w, SMALL))
    new_m.update(_unpack_small(sm_, w, SMALL))
    new_v.update(_unpack_small(sv, w, SMALL))
    return (loss, grad_x, *[grads[n] for n in WEIGHTS], *[delta[n] for n in WEIGHTS],
            *[new_m[n] for n in WEIGHTS], *[new_v[n] for n in WEIGHTS])
```

```python
import functools

import numpy as np
import jax
import jax.numpy as jnp
from jax import lax
from jax.experimental import pallas as pl
from jax.experimental.pallas import tpu as pltpu

f32, bf16 = jnp.float32, jnp.bfloat16

SSD_N_GROUPS = 4
SSD_CHUNK = 128
RMS_EPS = 1e-6
S5_MAX_REAL = -1e-4
S5_SUPERBLOCK = 256
ADAM_LR, ADAM_B1, ADAM_B2, ADAM_EPS, ADAM_WD, ADAM_STEP = 0.001, 0.9, 0.999, 1e-08, 0.01, 10

LANES = 128
PACK_COLS = 1024
D2D_STREAMS = 16
PACK_ROW_MULT = 2 * D2D_STREAMS * 16
VMEM_LIMIT_BYTES = 48 * 1024 * 1024
N_CHIPS, N_CORES, N_DEV = 4, 2, 8
MESH = pl.DeviceIdType.MESH

BIG = ["ffn1_w_gate", "ffn1_w_up", "ffn1_w_down", "w_in", "w_branch_a", "s5_w_glu", "w_branch_b", "w_out",
       "ffn2_w_gate", "ffn2_w_up", "ffn2_w_down"]
COL_SHARDED = {"ffn1_w_gate", "ffn1_w_up", "w_in", "s5_w_glu", "ffn2_w_gate", "ffn2_w_up"}
SMALL = ["ffn1_pre_g", "ffn1_post_g", "mix_pre_g", "mix_post_g", "ssd_conv_b", "ssd_dt_bias", "ssd_a_log", "ssd_d",
         "ssd_norm_g", "s5_lambda_re", "s5_lambda_im", "s5_b_re", "s5_b_im", "s5_c_re", "s5_c_im", "s5_log_step",
         "s5_d", "ffn2_pre_g", "ffn2_post_g"]
WEIGHTS = ["ffn1_pre_g", "ffn1_post_g", "ffn1_w_gate", "ffn1_w_up", "ffn1_w_down", "mix_pre_g", "mix_post_g", "w_in",
           "ssd_conv_w", "ssd_conv_b", "ssd_dt_bias", "ssd_a_log", "ssd_d", "ssd_norm_g", "w_branch_a", "s5_lambda_re",
           "s5_lambda_im", "s5_b_re", "s5_b_im", "s5_c_re", "s5_c_im", "s5_log_step", "s5_d", "s5_w_glu", "w_branch_b",
           "w_out", "ffn2_pre_g", "ffn2_post_g", "ffn2_w_gate", "ffn2_w_up", "ffn2_w_down"]


def _params(sem=None):
    return pltpu.CompilerParams(dimension_semantics=sem, vmem_limit_bytes=VMEM_LIMIT_BYTES)


def _pick(n, target, mult=LANES):
    best = None
    for d in range(mult, min(n, target) + 1, mult):
        if n % d == 0:
            best = d
    return best if best is not None else n


_DIMS = {"nn": (((1,), (0,)), ((), ())), "nt": (((1,), (1,)), ((), ())), "tn": (((0,), (0,)), ((), ()))}


def _mm(a, b, mode, out_dtype, name, bm_t=512, bn_t=1024, bk_t=2816):
    if mode == "nn":
        (M, K), (K2, N) = a.shape, b.shape
    elif mode == "nt":
        (M, K), (N, K2) = a.shape, b.shape
    else:
        (K, M), (K2, N) = a.shape, b.shape
    assert K == K2, (name, a.shape, b.shape)
    bm, bn, bk = _pick(M, bm_t), _pick(N, bn_t), _pick(K, bk_t)
    nk = K // bk
    dn = _DIMS[mode]

    def body(a_ref, b_ref, o_ref, *scratch):
        p = lax.dot_general(a_ref[...].astype(bf16), b_ref[...].astype(bf16), dn, preferred_element_type=f32)
        if nk == 1:
            o_ref[...] = p.astype(o_ref.dtype)
        else:
            acc = scratch[0]
            k = pl.program_id(2)

            @pl.when(k == 0)
            def _():
                acc[...] = p

            @pl.when(k > 0)
            def _():
                acc[...] += p

            @pl.when(k == nk - 1)
            def _():
                o_ref[...] = acc[...].astype(o_ref.dtype)

    if mode == "tn":
        a_spec = pl.BlockSpec((bk, bm), lambda i, j, k: (k, i))
    else:
        a_spec = pl.BlockSpec((bm, bk), lambda i, j, k: (i, k))
    if mode == "nt":
        b_spec = pl.BlockSpec((bn, bk), lambda i, j, k: (j, k))
    else:
        b_spec = pl.BlockSpec((bk, bn), lambda i, j, k: (k, j))
    return pl.pallas_call(
        body, name=name, grid=(M // bm, N // bn, nk), in_specs=[a_spec, b_spec],
        out_specs=pl.BlockSpec((bm, bn), lambda i, j, k: (i, j)), out_shape=jax.ShapeDtypeStruct((M, N), out_dtype),
        scratch_shapes=[pltpu.VMEM((bm, bn), f32)] if nk > 1 else [],
        compiler_params=_params(("parallel", "parallel", "arbitrary")),
    )(a, b)


def _bdmm(a, w, mode, nb, out_dtype, name, bt_t=512):
    if mode == "tn":
        T = a.shape[0]
        ka, nw = a.shape[1] // nb, w.shape[1] // nb
        bt = _pick(T, bt_t)
        nt = T // bt

        def body_tn(a_ref, b_ref, o_ref):
            p = lax.dot_general(a_ref[...].astype(bf16), b_ref[...].astype(bf16), _DIMS["tn"], preferred_element_type=f32)
            k = pl.program_id(1)

            @pl.when(k == 0)
            def _():
                o_ref[...] = p

            @pl.when(k > 0)
            def _():
                o_ref[...] += p

        return pl.pallas_call(
            body_tn, name=name, grid=(nb, nt),
            in_specs=[pl.BlockSpec((bt, ka), lambda j, k: (k, j)), pl.BlockSpec((bt, nw), lambda j, k: (k, j))],
            out_specs=pl.BlockSpec((ka, nw), lambda j, k: (j, 0)), out_shape=jax.ShapeDtypeStruct((nb * ka, nw), f32),
            compiler_params=_params(("parallel", "arbitrary")),
        )(a, w)
    T = a.shape[0]
    ka, nw = w.shape[0] // nb, w.shape[1]
    bt = _pick(T, bt_t)
    kin, kout = (ka, nw) if mode == "nn" else (nw, ka)
    dn = _DIMS[mode]

    def body(a_ref, w_ref, o_ref):
        o_ref[...] = lax.dot_general(a_ref[...].astype(bf16), w_ref[...].astype(bf16), dn,
                                     preferred_element_type=f32).astype(o_ref.dtype)

    return pl.pallas_call(
        body, name=name, grid=(nb, T // bt),
        in_specs=[pl.BlockSpec((bt, kin), lambda j, i: (i, j)), pl.BlockSpec((ka, nw), lambda j, i: (j, 0))],
        out_specs=pl.BlockSpec((bt, kout), lambda j, i: (i, j)), out_shape=jax.ShapeDtypeStruct((T, nb * kout), out_dtype),
        compiler_params=_params(("parallel", "parallel")),
    )(a, w)


def _row_index(i, cb):
    return (i, cb)


def _row_kernel(name, fn, rows, pars, row_outs, par_outs=(), block_rows=256):
    T = rows[0][0].shape[0]
    R = min(block_rows, T)
    assert T % R == 0
    nr, npar, nro = len(rows), len(pars), len(row_outs)

    def body(*refs):
        rv = [r[...] for r in refs[:nr]]
        pv = [r[...] for r in refs[nr:nr + npar]]
        ro, po = fn(rv, pv)
        for ref, v in zip(refs[nr + npar:nr + npar + nro], ro):
            ref[...] = v.astype(ref.dtype)
        if par_outs:
            i = pl.program_id(0)
            prefs = refs[nr + npar + nro:]

            @pl.when(i == 0)
            def _():
                for ref, v in zip(prefs, po):
                    ref[...] = v.astype(f32)

            @pl.when(i > 0)
            def _():
                for ref, v in zip(prefs, po):
                    ref[...] += v.astype(f32)

    in_specs = [pl.BlockSpec((R, nc), functools.partial(_row_index, cb=cb)) for (_, nc, cb) in rows]
    in_specs += [pl.BlockSpec(p.shape, lambda i: (0, 0)) for p in pars]
    out_specs = [pl.BlockSpec((R, nc), lambda i: (i, 0)) for (nc, _) in row_outs]
    out_specs += [pl.BlockSpec(s, lambda i: (0, 0)) for s in par_outs]
    out_shape = [jax.ShapeDtypeStruct((T, nc), dt) for (nc, dt) in row_outs]
    out_shape += [jax.ShapeDtypeStruct(s, f32) for s in par_outs]
    outs = pl.pallas_call(
        body, name=name, grid=(T // R,), in_specs=in_specs, out_specs=out_specs, out_shape=out_shape,
        compiler_params=_params(("arbitrary",) if par_outs else ("parallel",)),
    )(*[r[0] for r in rows], *pars)
    return list(outs)


def _fwd_of(f):
    def fn(rv, pv):
        return f([v.astype(f32) for v in rv], [v.astype(f32) for v in pv]), []
    return fn


def _vjp_of(f, n_x, n_cot, grad_idx, n_add=0):
    def fn(rv, pv):
        xs = [v.astype(f32) for v in rv[:n_x]]
        cots = [v.astype(f32) for v in rv[n_x:n_x + n_cot]]
        adds = rv[n_x + n_cot:n_x + n_cot + n_add]
        ps = [v.astype(f32) for v in pv]
        _, vjp = jax.vjp(lambda *a: f(list(a[:n_x]), list(a[n_x:])), *xs, *ps)
        g = vjp(cots)
        row_g = [g[i] for i in grad_idx]
        for k, a in enumerate(adds):
            row_g[k] = row_g[k] + a.astype(f32)
        return row_g, list(g[n_x:])
    return fn


def _rms(x, g):
    return x * lax.rsqrt(jnp.mean(x * x, axis=-1, keepdims=True) + RMS_EPS) * g


def _f_norm(xs, ps):
    return [_rms(xs[0], ps[0])]


def _f_post(scale):
    def f(xs, ps):
        return [scale * _rms(xs[0], ps[0])]
    return f


def _f_resnorm(scale):
    def f(xs, ps):
        return [xs[0] + scale * _rms(xs[1], ps[0])]
    return f


def _f_dt(xs, ps):
    dt = jax.nn.softplus(xs[0] + ps[0])
    return [dt, -jnp.exp(ps[1]) * dt]


def _f_ssdpost(n_groups):
    def f(xs, ps):
        y = xs[0] * jax.nn.silu(xs[1])
        width = y.shape[-1] // n_groups
        lane = lax.broadcasted_iota(jnp.int32, y.shape, 1)
        scale = jnp.zeros_like(y)
        for k in range(n_groups):
            m = ((lane >= k * width) & (lane < (k + 1) * width)).astype(f32)
            ms = jnp.sum(y * y * m, axis=-1, keepdims=True) / width
            scale = scale + lax.rsqrt(ms + RMS_EPS) * m
        return [y * scale * ps[0]]
    return f


def _f_s5post(xs, ps):
    return [jax.nn.gelu(xs[0] + ps[0] * xs[1])]


def _f_merge(xs, ps):
    return [jax.nn.sigmoid(xs[0]) * xs[1] + jax.nn.sigmoid(xs[2]) * xs[3]]


def _swiglu_fwd(rv, pv):
    ab = rv[0].astype(f32)
    h = ab.shape[1] // 2
    return [jax.nn.silu(ab[:, :h]) * ab[:, h:]], []


def _swiglu_bwd(rv, pv):
    ab, d = rv[0].astype(f32), rv[1].astype(f32)
    h = ab.shape[1] // 2
    a, b = ab[:, :h], ab[:, h:]
    s = jax.nn.sigmoid(a)
    return [jnp.concatenate([d * b * (s * (1.0 + a * (1.0 - s))), d * (a * s)], axis=1)], []


def _glu_fwd(rv, pv):
    vg = rv[0].astype(f32)
    h = vg.shape[1] // 2
    return [vg[:, :h] * jax.nn.sigmoid(vg[:, h:])], []


def _glu_bwd(rv, pv):
    vg, d = rv[0].astype(f32), rv[1].astype(f32)
    h = vg.shape[1] // 2
    s = jax.nn.sigmoid(vg[:, h:])
    return [jnp.concatenate([d * s, d * vg[:, :h] * s * (1.0 - s)], axis=1)], []


def _loss_fn(rv, pv):
    e = rv[0].astype(f32) - rv[1].astype(f32)
    per_tok = jnp.mean(e * e, axis=-1, keepdims=True)
    part = 0.5 * jnp.sum(per_tok, axis=0, keepdims=True)
    return [e / e.shape[-1]], [jnp.broadcast_to(part, (8, LANES))]


def _add_fn(rv, pv):
    return [rv[0].astype(f32) + rv[1].astype(f32)], []


def _adamw_fn(rv, pv):
    w, g, m, v = [x.astype(f32) for x in rv]
    m = ADAM_B1 * m + (1.0 - ADAM_B1) * g
    v = ADAM_B2 * v + (1.0 - ADAM_B2) * (g * g)
    m_hat = m / (1.0 - ADAM_B1 ** ADAM_STEP)
    v_hat = v / (1.0 - ADAM_B2 ** ADAM_STEP)
    return [-ADAM_LR * (m_hat / (jnp.sqrt(v_hat) + ADAM_EPS) + ADAM_WD * w), m, v], []


def _adamw(w, g, m, v, name):
    shape = w.shape
    cols = shape[-1] if (w.ndim >= 2 and shape[-1] >= LANES) else None
    if cols is None:
        n = int(np.prod(shape))
        cols = LANES if n % LANES == 0 else n
    two_d = (int(np.prod(shape)) // cols, cols)
    rows = [(t.reshape(two_d), cols, 0) for t in (w, g, m, v)]
    R = _pick(two_d[0], 256, 8)
    outs = _row_kernel(name, _adamw_fn, rows, [], [(cols, f32)] * 3, block_rows=R)
    return [o.reshape(shape) for o in outs]


def _shift_down(x, s, row):
    if s == 0:
        return x
    return jnp.where(row >= s, pltpu.roll(x, s, 0), 0.0)


def _shift_up(x, s, row):
    if s == 0:
        return x
    n = x.shape[0]
    return jnp.where(row < n - s, pltpu.roll(x, n - s, 0), 0.0)


def _conv_pre(x, w, b, row):
    kw = w.shape[0]
    c = b
    for k in range(kw):
        c = c + w[k:k + 1, :] * _shift_down(x, kw - 1 - k, row)
    return c


def _conv_fwd(xsrc, col0, w, b, name, bc_t=512):
    T = xsrc.shape[0]
    kw, ncols = w.shape
    bc = _pick(ncols, bc_t)
    off = col0 // bc
    assert col0 % bc == 0

    def body(x_ref, w_ref, b_ref, o_ref):
        x = x_ref[...].astype(f32)
        row = lax.broadcasted_iota(jnp.int32, x.shape, 0)
        c = _conv_pre(x, w_ref[...], b_ref[...], row)
        o_ref[...] = c * jax.nn.sigmoid(c)

    return pl.pallas_call(
        body, name=name, grid=(ncols // bc,),
        in_specs=[pl.BlockSpec((T, bc), lambda j: (0, off + j)), pl.BlockSpec((kw, bc), lambda j: (0, j)),
                  pl.BlockSpec((1, bc), lambda j: (0, j))],
        out_specs=pl.BlockSpec((T, bc), lambda j: (0, j)), out_shape=jax.ShapeDtypeStruct((T, ncols), f32),
        compiler_params=_params(("parallel",)),
    )(xsrc, w, b)


def _conv_bwd(xsrc, col0, w, b, dact, name, bc_t=512):
    T = xsrc.shape[0]
    kw, ncols = w.shape
    bc = _pick(ncols, bc_t)
    off = col0 // bc
    assert col0 % bc == 0

    def body(x_ref, w_ref, b_ref, d_ref, dx_ref, dw_ref, db_ref):
        x = x_ref[...].astype(f32)
        w = w_ref[...]
        row = lax.broadcasted_iota(jnp.int32, x.shape, 0)
        c = _conv_pre(x, w, b_ref[...], row)
        s = jax.nn.sigmoid(c)
        dc = d_ref[...].astype(f32) * (s * (1.0 + c * (1.0 - s)))
        dx = jnp.zeros_like(x)
        dws = []
        for k in range(kw):
            dx = dx + w[k:k + 1, :] * _shift_up(dc, kw - 1 - k, row)
            dws.append(jnp.sum(dc * _shift_down(x, kw - 1 - k, row), axis=0, keepdims=True))
        dx_ref[...] = dx.astype(dx_ref.dtype)
        dw_ref[...] = jnp.concatenate(dws, axis=0)
        db_ref[...] = jnp.sum(dc, axis=0, keepdims=True)

    return pl.pallas_call(
        body, name=name, grid=(ncols // bc,),
        in_specs=[pl.BlockSpec((T, bc), lambda j: (0, off + j)), pl.BlockSpec((kw, bc), lambda j: (0, j)),
                  pl.BlockSpec((1, bc), lambda j: (0, j)), pl.BlockSpec((T, bc), lambda j: (0, j))],
        out_specs=[pl.BlockSpec((T, bc), lambda j: (0, j)), pl.BlockSpec((kw, bc), lambda j: (0, j)),
                   pl.BlockSpec((1, bc), lambda j: (0, j))],
        out_shape=[jax.ShapeDtypeStruct((T, ncols), bf16), jax.ShapeDtypeStruct((kw, ncols), f32),
                   jax.ShapeDtypeStruct((1, ncols), f32)],
        compiler_params=_params(("parallel",)),
    )(xsrc, w, b, dact)


_HI = lax.Precision.HIGHEST


def _dot(a, b, dims="nn", precision=None):
    return lax.dot_general(a, b, _DIMS[dims], preferred_element_type=f32, precision=precision)


def _ssd_common(x_ref, b_ref, c_ref, dt_ref, adt_ref, d_ref, hpg, p):
    q = b_ref.shape[0]
    hp = hpg * p
    bb, cb = b_ref[...].astype(bf16), c_ref[...].astype(bf16)
    r = lax.broadcasted_iota(jnp.int32, (q, q), 0)
    s = lax.broadcasted_iota(jnp.int32, (q, q), 1)
    tril = r >= s
    trilf = tril.astype(f32)
    eh = lax.broadcasted_iota(jnp.int32, (LANES, hp), 0)
    ec = lax.broadcasted_iota(jnp.int32, (LANES, hp), 1)
    expand = ((ec >= eh * p) & (ec < (eh + 1) * p)).astype(f32)
    adt = adt_ref[...]
    cum = _dot(trilf, adt, "nn", _HI)
    cum_t = _dot(adt, (r <= s).astype(f32), "tn", _HI)
    cum_e = _dot(cum, expand, "nn", _HI)
    dt_e = _dot(dt_ref[...], expand, "nn", _HI)
    d_e = _dot(jnp.broadcast_to(d_ref[...], (8, LANES)), expand, "nn", _HI)[0:1, :]
    gmat = _dot(cb, bb, "nt")
    x = x_ref[...]
    xdt = x * dt_e
    e_all = jnp.exp(cum_e)
    dec = jnp.exp(cum_e[q - 1:q, :] - cum_e)
    lms, ms = [], []
    for h in range(hpg):
        lm = jnp.exp(jnp.where(tril, cum[:, h:h + 1] - cum_t[h:h + 1, :], -1e30))
        lms.append(lm)
        ms.append(gmat * lm)
    et = [jnp.exp(cum[q - 1:q, h:h + 1]) for h in range(hpg)]
    return dict(bb=bb, cb=cb, trilf=trilf, expand=expand, cum=cum, x=x, xdt=xdt, dt_e=dt_e, d_e=d_e, e=e_all, dec=dec,
                lms=lms, ms=ms, et=et)


def _ssd_specs(q, hp, n, g_n, nc, rev):
    def cidx(c):
        return (nc - 1 - c) if rev else c
    x_spec = pl.BlockSpec((q, hp), lambda g, c: (cidx(c), g))
    boff = (g_n * hp) // n
    b_spec = pl.BlockSpec((q, n), lambda g, c: (cidx(c), boff + g))
    c_spec = pl.BlockSpec((q, n), lambda g, c: (cidx(c), boff + g_n + g))
    dt_spec = pl.BlockSpec((q, LANES), lambda g, c: (cidx(c), g))
    d_spec = pl.BlockSpec((1, LANES), lambda g, c: (0, g))
    st_spec = pl.BlockSpec((1, 1, hp, n), lambda g, c: (cidx(c), g, 0, 0))
    return x_spec, b_spec, c_spec, dt_spec, d_spec, st_spec


def _ssd_fwd(act, dt, adt, dpad, hpg, p, n, name):
    T = act.shape[0]
    g_n, q = SSD_N_GROUPS, SSD_CHUNK
    nc, hp = T // q, hpg * p
    x_spec, b_spec, c_spec, dt_spec, d_spec, st_spec = _ssd_specs(q, hp, n, g_n, nc, False)

    def body(x_ref, b_ref, c_ref, dt_ref, adt_ref, d_ref, y_ref, st_ref, s_scr):
        @pl.when(pl.program_id(1) == 0)
        def _():
            s_scr[...] = jnp.zeros_like(s_scr)

        k = _ssd_common(x_ref, b_ref, c_ref, dt_ref, adt_ref, d_ref, hpg, p)
        s0 = s_scr[...]
        st_ref[0, 0] = s0
        xdtb = k["xdt"].astype(bf16)
        ydiag = [_dot(k["ms"][h].astype(bf16), xdtb[:, h * p:(h + 1) * p]) for h in range(hpg)]
        z = _dot(k["cb"], s0.astype(bf16), "nt")
        y_ref[...] = jnp.concatenate(ydiag, axis=1) + k["e"] * z + k["d_e"] * k["x"]
        upd = _dot((k["xdt"] * k["dec"]).astype(bf16), k["bb"], "tn")
        for h in range(hpg):
            s_scr[h * p:(h + 1) * p, :] = k["et"][h] * s0[h * p:(h + 1) * p, :] + upd[h * p:(h + 1) * p, :]

    return pl.pallas_call(
        body, name=name, grid=(g_n, nc),
        in_specs=[x_spec, b_spec, c_spec, dt_spec, dt_spec, d_spec],
        out_specs=[pl.BlockSpec((q, hp), lambda g, c: (c, g)), st_spec],
        out_shape=[jax.ShapeDtypeStruct((T, g_n * hp), f32), jax.ShapeDtypeStruct((nc, g_n, hp, n), f32)],
        scratch_shapes=[pltpu.VMEM((hp, n), f32)],
        compiler_params=_params(("parallel", "arbitrary")),
    )(act, act, act, dt, adt, dpad)


def _ssd_bwd(act, dt, adt, dpad, states, dy, hpg, p, n, name):
    T = act.shape[0]
    g_n, q = SSD_N_GROUPS, SSD_CHUNK
    nc, hp = T // q, hpg * p
    x_spec, b_spec, c_spec, dt_spec, d_spec, st_spec = _ssd_specs(q, hp, n, g_n, nc, True)

    def body(x_ref, b_ref, c_ref, dt_ref, adt_ref, d_ref, st_ref, dy_ref,
             dx_ref, db_ref, dc_ref, ddt_ref, dadt_ref, dd_ref, ds_scr):
        first = pl.program_id(1) == 0

        @pl.when(first)
        def _():
            ds_scr[...] = jnp.zeros_like(ds_scr)

        k = _ssd_common(x_ref, b_ref, c_ref, dt_ref, adt_ref, d_ref, hpg, p)
        bb, cb, expand, x, xdt, dec = k["bb"], k["cb"], k["expand"], k["x"], k["xdt"], k["dec"]
        heads = lambda t: _dot(t, expand, "nt", _HI)
        s0 = st_ref[0, 0]
        ds1 = ds_scr[...]
        s0b, ds1b = s0.astype(bf16), ds1.astype(bf16)
        dy = dy_ref[...]
        dyb, xdtb = dy.astype(bf16), xdt.astype(bf16)
        lane = lax.broadcasted_iota(jnp.int32, (1, LANES), 1)
        dg = jnp.zeros((q, q), f32)
        w_rows = jnp.zeros((q, LANES), f32)
        w_cols, dxdt_parts = [], []
        for h in range(hpg):
            hs = slice(h * p, (h + 1) * p)
            dm = _dot(dyb[:, hs], xdtb[:, hs], "nt")
            dg = dg + dm * k["lms"][h]
            wm = dm * k["ms"][h]
            w_rows = w_rows + jnp.sum(wm, axis=1, keepdims=True) * (lane == h).astype(f32)
            w_cols.append(jnp.sum(wm, axis=0, keepdims=True))
            dxdt_parts.append(_dot(k["ms"][h].astype(bf16), dyb[:, hs], "tn"))
        dxdt_diag = jnp.concatenate(dxdt_parts, axis=1)
        w_cols = jnp.concatenate(w_cols + [jnp.zeros((LANES - hpg, q), f32)], axis=0).T
        dgb = dg.astype(bf16)
        z = _dot(cb, s0b, "nt")
        dz = dy * k["e"]
        dzb = dz.astype(bf16)
        dxd = _dot(bb, ds1b, "nt")
        ddec = dxd * xdt * dec
        db_ref[...] = _dot(dgb, cb, "tn") + _dot((xdt * dec).astype(bf16), ds1b)
        dc_ref[...] = _dot(dgb, bb) + _dot(dzb, s0b)
        ds0 = _dot(dzb, cb, "tn")
        for h in range(hpg):
            hs = slice(h * p, (h + 1) * p)
            ds_scr[hs, :] = ds0[hs, :] + k["et"][h] * ds1[hs, :]
        dxdt = dxdt_diag + dxd * dec
        ddec_h = heads(ddec)
        dcum = w_rows - w_cols + heads(dz * z) - ddec_h
        et_row = jnp.exp(k["cum"][q - 1:q, :])
        dsum = _dot(jnp.ones((8, n), f32), _dot(expand, ds1 * s0, "nn", _HI), "nt", _HI)[0:1, :]
        dcl = dsum * et_row + jnp.sum(ddec_h, axis=0, keepdims=True)
        rowq = lax.broadcasted_iota(jnp.int32, (q, 1), 0)
        dcum = dcum + (rowq == q - 1).astype(f32) * dcl
        ddt_ref[...] = heads(dxdt * x)
        dadt_ref[...] = _dot(k["trilf"], dcum, "tn", _HI)
        dx_ref[...] = k["d_e"] * dy + dxdt * k["dt_e"]
        dd8 = heads(jnp.broadcast_to(jnp.sum(dy * x, axis=0, keepdims=True), (8, hp)))

        @pl.when(first)
        def _():
            dd_ref[...] = dd8

        @pl.when(jnp.logical_not(first))
        def _():
            dd_ref[...] += dd8

    rc = lambda g, c: (nc - 1 - c, g)
    return pl.pallas_call(
        body, name=name, grid=(g_n, nc),
        in_specs=[x_spec, b_spec, c_spec, dt_spec, dt_spec, d_spec, st_spec, pl.BlockSpec((q, hp), rc)],
        out_specs=[pl.BlockSpec((q, hp), rc), pl.BlockSpec((q, n), rc), pl.BlockSpec((q, n), rc),
                   pl.BlockSpec((q, LANES), rc), pl.BlockSpec((q, LANES), rc), pl.BlockSpec((8, LANES), lambda g, c: (g, 0))],
        out_shape=[jax.ShapeDtypeStruct((T, g_n * hp), f32), jax.ShapeDtypeStruct((T, g_n * n), f32),
                   jax.ShapeDtypeStruct((T, g_n * n), f32), jax.ShapeDtypeStruct((T, g_n * LANES), f32),
                   jax.ShapeDtypeStruct((T, g_n * LANES), f32), jax.ShapeDtypeStruct((g_n * 8, LANES), f32)],
        scratch_shapes=[pltpu.VMEM((hp, n), f32)],
        compiler_params=_params(("parallel", "arbitrary")),
    )(act, act, act, dt, adt, dpad, states, dy)


def _s5_scan_fwd(bu, lam_re, lam_im, nsb, name, tc_t=512):
    T = bu.shape[0]
    w2 = bu.shape[1] // nsb
    w = w2 // 2
    tc = _pick(T, tc_t, 8)

    def body(bu_ref, lr_ref, li_ref, st_ref, carry):
        @pl.when(pl.program_id(1) == 0)
        def _():
            carry[...] = jnp.zeros_like(carry)

        lr, li = lr_ref[0:1, :], li_ref[0:1, :]

        def step(t, s):
            sr, si = s
            row = bu_ref[pl.ds(t, 1), :]
            nsr = lr * sr - li * si + row[:, :w]
            nsi = lr * si + li * sr + row[:, w:]
            st_ref[pl.ds(t, 1), :] = jnp.concatenate([nsr, nsi], axis=1)
            return nsr, nsi

        sr, si = lax.fori_loop(0, tc, step, (carry[0:1, :], carry[1:2, :]))
        carry[0:1, :] = sr
        carry[1:2, :] = si

    return pl.pallas_call(
        body, name=name, grid=(nsb, T // tc),
        in_specs=[pl.BlockSpec((tc, w2), lambda j, i: (i, j)), pl.BlockSpec((8, w), lambda j, i: (j, 0)),
                  pl.BlockSpec((8, w), lambda j, i: (j, 0))],
        out_specs=pl.BlockSpec((tc, w2), lambda j, i: (i, j)), out_shape=jax.ShapeDtypeStruct(bu.shape, f32),
        scratch_shapes=[pltpu.VMEM((8, w), f32)],
        compiler_params=_params(("parallel", "arbitrary")),
    )(bu, lam_re, lam_im)


def _s5_scan_bwd(gst, states, lam_re, lam_im, nsb, name, tc_t=512):
    T = gst.shape[0]
    w2 = gst.shape[1] // nsb
    w = w2 // 2
    tc = _pick(T, tc_t, 8)
    nt = T // tc

    def body(g_ref, s_ref, sp_ref, lr_ref, li_ref, a_ref, dlr_ref, dli_ref, carry):
        i = pl.program_id(1)

        @pl.when(i == 0)
        def _():
            carry[...] = jnp.zeros_like(carry)

        lr, li = lr_ref[0:1, :], li_ref[0:1, :]

        def adj(t, ar, ai):
            row = g_ref[pl.ds(t, 1), :]
            nar = row[:, :w] + lr * ar + li * ai
            nai = row[:, w:] - li * ar + lr * ai
            a_ref[pl.ds(t, 1), :] = jnp.concatenate([nar, nai], axis=1)
            return nar, nai

        def acc(prev, ar, ai, dlr, dli):
            spr, spi = prev[:, :w], prev[:, w:]
            return dlr + ar * spr + ai * spi, dli - ar * spi + ai * spr

        def step(k, c):
            ar, ai, dlr, dli = c
            t = tc - 1 - k
            ar, ai = adj(t, ar, ai)
            dlr, dli = acc(s_ref[pl.ds(t - 1, 1), :], ar, ai, dlr, dli)
            return ar, ai, dlr, dli

        c0 = (carry[0:1, :], carry[1:2, :], carry[2:3, :], carry[3:4, :])
        ar, ai, dlr, dli = lax.fori_loop(0, tc - 1, step, c0)
        ar, ai = adj(0, ar, ai)
        prev = sp_ref[tc - 1:tc, :] * (i < nt - 1).astype(f32)
        dlr, dli = acc(prev, ar, ai, dlr, dli)
        carry[0:1, :] = ar
        carry[1:2, :] = ai
        carry[2:3, :] = dlr
        carry[3:4, :] = dli
        dlr_ref[...] = jnp.broadcast_to(dlr, (8, w))
        dli_ref[...] = jnp.broadcast_to(dli, (8, w))

    cur = lambda j, i: (nt - 1 - i, j)
    prv = lambda j, i: (jnp.maximum(nt - 2 - i, 0), j)
    return pl.pallas_call(
        body, name=name, grid=(nsb, nt),
        in_specs=[pl.BlockSpec((tc, w2), cur), pl.BlockSpec((tc, w2), cur), pl.BlockSpec((tc, w2), prv),
                  pl.BlockSpec((8, w), lambda j, i: (j, 0)), pl.BlockSpec((8, w), lambda j, i: (j, 0))],
        out_specs=[pl.BlockSpec((tc, w2), cur), pl.BlockSpec((8, w), lambda j, i: (j, 0)),
                   pl.BlockSpec((8, w), lambda j, i: (j, 0))],
        out_shape=[jax.ShapeDtypeStruct(gst.shape, f32), jax.ShapeDtypeStruct((nsb * 8, w), f32),
                   jax.ShapeDtypeStruct((nsb * 8, w), f32)],
        scratch_shapes=[pltpu.VMEM((8, w), f32)],
        compiler_params=_params(("parallel", "arbitrary")),
    )(gst, states, states, lam_re, lam_im)


def _s5_prep_fn(xs, ps):
    lam_re, lam_im, log_step, b_re, b_im, expand = ps
    lr = jnp.minimum(lam_re, S5_MAX_REAL)
    li = lam_im
    step = jnp.exp(log_step)
    er = jnp.exp(lr * step)
    ang = li * step
    lbr, lbi = er * jnp.cos(ang), er * jnp.sin(ang)
    nr, ni = lbr - 1.0, lbi
    den = lr * lr + li * li
    qr, qi = (nr * lr + ni * li) / den, (ni * lr - nr * li) / den
    qre, qie = _dot(qr, expand, "nn", _HI), _dot(qi, expand, "nn", _HI)
    return [lbr, lbi, qre * b_re - qie * b_im, qre * b_im + qie * b_re]


def _s5_prep(pars, name):
    def body(*refs):
        outs = _s5_prep_fn([], [r[...] for r in refs[:6]])
        for ref, v in zip(refs[6:], outs):
            ref[...] = v

    g, nst = pars[0].shape
    nc = pars[3].shape[1]
    return pl.pallas_call(
        body, name=name,
        out_shape=[jax.ShapeDtypeStruct((g, nst), f32)] * 2 + [jax.ShapeDtypeStruct((g, nc), f32)] * 2,
        compiler_params=_params(),
    )(*pars)


def _s5_prep_bwd(pars, cots, name):
    def body(*refs):
        ps = [r[...] for r in refs[:6]]
        ct = [r[...] for r in refs[6:10]]
        _, vjp = jax.vjp(lambda *a: _s5_prep_fn([], list(a)), *ps)
        g = vjp(ct)
        for ref, v in zip(refs[10:], g[:5]):
            ref[...] = v

    return pl.pallas_call(
        body, name=name, out_shape=[jax.ShapeDtypeStruct(p.shape, f32) for p in pars[:5]], compiler_params=_params(),
    )(*pars, *cots)


_ANY = pl.BlockSpec(memory_space=pl.ANY)


def _remote(src, dst, send_sem, recv_sem, device):
    return pltpu.make_async_remote_copy(src_ref=src, dst_ref=dst, send_sem=send_sem, recv_sem=recv_sem, device_id=device,
                                        device_id_type=MESH)


def _chip_all_gather(block, name):
    rows = block.shape[0]
    half = rows // 2
    piece = half // D2D_STREAMS
    assert rows % (2 * D2D_STREAMS * 16) == 0

    def body(src, out, ici_send, ici_recv, d2d_send, d2d_recv, local_sem):
        x, y, c = lax.axis_index("x"), lax.axis_index("y"), lax.axis_index("c")
        me = 2 * x + y
        sibling = (x, y, 1 - c)
        local = pltpu.make_async_copy(src, out.at[me], local_sem)
        local.start()
        chips = [(1 - x, y), (x, 1 - y), (1 - x, 1 - y)]
        mine = pl.ds(pl.multiple_of(c * half, 16), half)
        sends = []
        for j, (px, py) in enumerate(chips):
            cp = _remote(src.at[mine], out.at[me, mine], ici_send.at[j], ici_recv.at[j], (px, py, c))
            cp.start()
            sends.append(cp)
        for j, (px, py) in enumerate(chips):
            slot = 2 * px + py
            _remote(src.at[mine], out.at[slot, mine], ici_send.at[j], ici_recv.at[j], (px, py, c)).wait_recv()
            for s in range(D2D_STREAMS):
                r = pl.ds(pl.multiple_of(c * half + s * piece, 16), piece)
                k = j * D2D_STREAMS + s
                cp = _remote(out.at[slot, r], out.at[slot, r], d2d_send.at[k], d2d_recv.at[k], sibling)
                cp.start()
                sends.append(cp)
        for j, (px, py) in enumerate(chips):
            slot = 2 * px + py
            for s in range(D2D_STREAMS):
                r = pl.ds(pl.multiple_of((1 - c) * half + s * piece, 16), piece)
                k = j * D2D_STREAMS + s
                _remote(out.at[slot, r], out.at[slot, r], d2d_send.at[k], d2d_recv.at[k], sibling).wait_recv()
        for cp in sends:
            cp.wait_send()
        local.wait()

    n_d2d = 3 * D2D_STREAMS
    return pl.pallas_call(
        body, name=name, in_specs=[_ANY], out_specs=_ANY,
        out_shape=jax.ShapeDtypeStruct((N_CHIPS,) + block.shape, block.dtype),
        scratch_shapes=[pltpu.SemaphoreType.DMA((3,)), pltpu.SemaphoreType.DMA((3,)), pltpu.SemaphoreType.DMA((n_d2d,)),
                        pltpu.SemaphoreType.DMA((n_d2d,)), pltpu.SemaphoreType.DMA(())],
    )(block)


def _chip_scatter(parts, name):
    def body(src, out, send_sems, recv_sems, local_sem):
        x, y, c = lax.axis_index("x"), lax.axis_index("y"), lax.axis_index("c")
        me = 2 * x + y
        local = pltpu.make_async_copy(src.at[me], out.at[me], local_sem)
        local.start()
        chips = [(1 - x, y), (x, 1 - y), (1 - x, 1 - y)]
        sends = []
        for j, (px, py) in enumerate(chips):
            cp = pltpu.make_async_remote_copy(src_ref=src.at[2 * px + py], dst_ref=out.at[me], send_sem=send_sems.at[j],
                                              recv_sem=recv_sems.at[j], device_id=(px, py, c), device_id_type=MESH)
            cp.start()
            sends.append(cp)
        for j, (px, py) in enumerate(chips):
            pltpu.make_async_remote_copy(src_ref=src.at[me], dst_ref=out.at[2 * px + py], send_sem=send_sems.at[j],
                                         recv_sem=recv_sems.at[j], device_id=(px, py, c), device_id_type=MESH).wait_recv()
        for cp in sends:
            cp.wait_send()
        local.wait()

    return pl.pallas_call(
        body, name=name, in_specs=[_ANY], out_specs=_ANY, out_shape=jax.ShapeDtypeStruct(parts.shape, parts.dtype),
        scratch_shapes=[pltpu.SemaphoreType.DMA((3,)), pltpu.SemaphoreType.DMA((3,)), pltpu.SemaphoreType.DMA(())],
    )(parts)


def _core_send_other_half(parts, name):
    n_slots, rows, cols = parts.shape
    half = rows // 2
    piece = half // D2D_STREAMS
    assert rows % (2 * D2D_STREAMS * 16) == 0

    def body(src, out, send_sems, recv_sems):
        x, y, c = lax.axis_index("x"), lax.axis_index("y"), lax.axis_index("c")
        sibling = (x, y, 1 - c)
        sends = []
        for k in range(n_slots):
            for s in range(D2D_STREAMS):
                theirs = pl.ds(pl.multiple_of((1 - c) * half + s * piece, 16), piece)
                i = k * D2D_STREAMS + s
                cp = _remote(src.at[k, theirs], out.at[k, pl.ds(s * piece, piece)], send_sems.at[i], recv_sems.at[i], sibling)
                cp.start()
                sends.append(cp)
        for cp in sends:
            cp.wait_recv()
        for cp in sends:
            cp.wait_send()

    n = n_slots * D2D_STREAMS
    return pl.pallas_call(
        body, name=name, in_specs=[_ANY], out_specs=_ANY, out_shape=jax.ShapeDtypeStruct((n_slots, half, cols), parts.dtype),
        scratch_shapes=[pltpu.SemaphoreType.DMA((n,)), pltpu.SemaphoreType.DMA((n,))],
    )(parts)


def _add_my_half(parts, other, core, name, block_rows=256):
    n_slots, rows, cols = parts.shape
    half = rows // 2
    br = _pick(half, block_rows, 16)
    nb = half // br

    def body(c_ref, a_ref, b_ref, o_ref):
        o_ref[...] = (a_ref[...].astype(f32) + b_ref[...].astype(f32)).astype(o_ref.dtype)

    grid_spec = pltpu.PrefetchScalarGridSpec(
        num_scalar_prefetch=1, grid=(n_slots, nb),
        in_specs=[pl.BlockSpec((1, br, cols), lambda k, i, c: (k, c[0] * nb + i, 0)),
                  pl.BlockSpec((1, br, cols), lambda k, i, c: (k, i, 0))],
        out_specs=pl.BlockSpec((1, br, cols), lambda k, i, c: (k, i, 0)))
    return pl.pallas_call(
        body, name=name, grid_spec=grid_spec, out_shape=jax.ShapeDtypeStruct((n_slots, half, cols), bf16),
        compiler_params=_params(("parallel", "parallel")),
    )(core, parts, other)


def _core_join_halves(mine, name):
    half, cols = mine.shape
    piece = half // D2D_STREAMS
    assert half % (D2D_STREAMS * 16) == 0

    def body(src, out, send_sems, recv_sems, local_sem):
        x, y, c = lax.axis_index("x"), lax.axis_index("y"), lax.axis_index("c")
        sibling = (x, y, 1 - c)
        local = pltpu.make_async_copy(src, out.at[pl.ds(pl.multiple_of(c * half, 16), half)], local_sem)
        local.start()
        sends = []
        for s in range(D2D_STREAMS):
            dst = out.at[pl.ds(pl.multiple_of(c * half + s * piece, 16), piece)]
            cp = _remote(src.at[pl.ds(s * piece, piece)], dst, send_sems.at[s], recv_sems.at[s], sibling)
            cp.start()
            sends.append(cp)
        for s in range(D2D_STREAMS):
            dst = out.at[pl.ds(pl.multiple_of((1 - c) * half + s * piece, 16), piece)]
            _remote(src.at[pl.ds(s * piece, piece)], dst, send_sems.at[s], recv_sems.at[s], sibling).wait_recv()
        for cp in sends:
            cp.wait_send()
        local.wait()

    return pl.pallas_call(
        body, name=name, in_specs=[_ANY], out_specs=_ANY, out_shape=jax.ShapeDtypeStruct((2 * half, cols), mine.dtype),
        scratch_shapes=[pltpu.SemaphoreType.DMA((D2D_STREAMS,)), pltpu.SemaphoreType.DMA((D2D_STREAMS,)),
                        pltpu.SemaphoreType.DMA(())],
    )(mine)


def _device_all_gather(block, name):
    def body(src, out, send_sems, recv_sems, local_sem):
        x, y, c = lax.axis_index("x"), lax.axis_index("y"), lax.axis_index("c")
        me = 4 * x + 2 * y + c
        local = pltpu.make_async_copy(src, out.at[me], local_sem)
        local.start()
        peers = []
        for mask in range(1, N_DEV):
            fx, fy, fc = (mask >> 2) & 1, (mask >> 1) & 1, mask & 1
            peers.append((x ^ fx, y ^ fy, c ^ fc))
        sends = []
        for j, peer in enumerate(peers):
            cp = pltpu.make_async_remote_copy(src_ref=src, dst_ref=out.at[me], send_sem=send_sems.at[j],
                                              recv_sem=recv_sems.at[j], device_id=peer, device_id_type=MESH)
            cp.start()
            sends.append(cp)
        for j, (px, py, pc) in enumerate(peers):
            pltpu.make_async_remote_copy(src_ref=src, dst_ref=out.at[4 * px + 2 * py + pc], send_sem=send_sems.at[j],
                                         recv_sem=recv_sems.at[j], device_id=(px, py, pc), device_id_type=MESH).wait_recv()
        for cp in sends:
            cp.wait_send()
        local.wait()

    return pl.pallas_call(
        body, name=name, in_specs=[_ANY], out_specs=_ANY,
        out_shape=jax.ShapeDtypeStruct((N_DEV,) + block.shape, block.dtype),
        scratch_shapes=[pltpu.SemaphoreType.DMA((N_DEV - 1,)), pltpu.SemaphoreType.DMA((N_DEV - 1,)),
                        pltpu.SemaphoreType.DMA(())],
    )(block)


def _sum_slots(stack, name, block_rows=256):
    s_n, r_n, c_n = stack.shape
    br = _pick(r_n, block_rows, 8)

    def body(in_ref, o_ref):
        acc = in_ref[0].astype(f32)
        for s in range(1, s_n):
            acc = acc + in_ref[s].astype(f32)
        o_ref[...] = acc

    return pl.pallas_call(
        body, name=name, grid=(r_n // br,), in_specs=[pl.BlockSpec((s_n, br, c_n), lambda i: (0, i, 0))],
        out_specs=pl.BlockSpec((br, c_n), lambda i: (i, 0)), out_shape=jax.ShapeDtypeStruct((r_n, c_n), f32),
        compiler_params=_params(("parallel",)),
    )(stack)


def _pad_rows(a, mult):
    r = (-a.shape[0]) % mult
    return a if r == 0 else jnp.concatenate([a, jnp.zeros((r,) + a.shape[1:], a.dtype)], axis=0)


def _shard_axis(name):
    return 2 if name in COL_SHARDED else 1


def _pack_weights(w, conv_w):
    parts = [w[n].astype(bf16).reshape(-1, PACK_COLS) for n in BIG]
    parts.append(lax.bitcast_convert_type(conv_w, bf16).reshape(-1, PACK_COLS))
    return _pad_rows(jnp.concatenate(parts, axis=0), PACK_ROW_MULT)


def _unpack_weights(full, w, conv_w):
    out, r0 = {}, 0
    for n in BIG:
        rows = w[n].size // PACK_COLS
        pieces = full[:, r0:r0 + rows].reshape((N_CHIPS,) + w[n].shape)
        out[n] = jnp.concatenate([pieces[k] for k in range(N_CHIPS)], axis=_shard_axis(n))
        r0 += rows
    rows = conv_w.size * 2 // PACK_COLS
    pieces = lax.bitcast_convert_type(full[:, r0:r0 + rows].reshape((N_CHIPS,) + conv_w.shape + (2,)), f32)
    return out, jnp.concatenate([pieces[k] for k in range(N_CHIPS)], axis=2)


def _pack_big_grads(g):
    slots = []
    for k in range(N_CHIPS):
        parts = []
        for n in BIG:
            ax = _shard_axis(n)
            width = g[n].shape[ax] // N_CHIPS
            parts.append(lax.slice_in_dim(g[n], k * width, (k + 1) * width, axis=ax).astype(bf16).reshape(-1, PACK_COLS))
        slots.append(_pad_rows(jnp.concatenate(parts, axis=0), PACK_ROW_MULT))
    return jnp.stack(slots)


def _unpack_big_grads(summed, w):
    out, r0 = {}, 0
    for n in BIG:
        rows = w[n].size // PACK_COLS
        out[n] = summed[r0:r0 + rows].reshape(w[n].shape)
        r0 += rows
    return out


def _pack_small(vals, names):
    parts = []
    for n in names:
        flat = vals[n].reshape(-1)
        pad = (-flat.size) % LANES
        if pad:
            flat = jnp.concatenate([flat, jnp.zeros((pad,), flat.dtype)])
        parts.append(flat.reshape(-1, LANES))
    return _pad_rows(jnp.concatenate(parts, axis=0), PACK_ROW_MULT)


def _unpack_small(packed, like, names):
    out, r0 = {}, 0
    for n in names:
        size = like[n].size
        rows = -(-size // LANES)
        out[n] = packed[r0:r0 + rows].reshape(-1)[:size].reshape(like[n].shape)
        r0 += rows
    return out


def _dims(w, x):
    d = {}
    d["D"] = x.shape[-1]
    d["T"] = x.shape[-2]
    d["DI"] = w["ssd_norm_g"].shape[-1]
    d["NH"] = w["ssd_dt_bias"].shape[-1]
    d["CD"] = w["ssd_conv_b"].shape[-1]
    d["G"] = SSD_N_GROUPS
    d["HPG"] = d["NH"] // d["G"]
    d["P"] = d["DI"] // d["NH"]
    d["N"] = (d["CD"] - d["DI"]) // (2 * d["G"])
    d["S5G"], d["S5N"] = w["s5_lambda_re"].shape[-2:]
    d["S5C"] = w["s5_b_re"].shape[-1]
    d["S5W"] = d["S5G"] * d["S5C"]
    d["NSB"] = d["S5W"] // S5_SUPERBLOCK
    d["GSB"] = d["S5G"] // d["NSB"]
    return d


def _head_pad(v, d):
    lead = v.shape[:-1]
    v = v.reshape(lead + (d["G"], d["HPG"]))
    v = jnp.concatenate([v, jnp.zeros(lead + (d["G"], LANES - d["HPG"]), v.dtype)], axis=-1)
    return v.reshape(lead + (d["G"] * LANES,))


def _head_unpad(v, d):
    lead = v.shape[:-1]
    return v.reshape(lead + (d["G"], LANES))[..., :d["HPG"]].reshape(lead + (d["NH"],))


def _w_in_perm(w_in, d):
    o = d["DI"] + d["CD"]
    return jnp.concatenate([w_in[:, :o], w_in[:, o + d["NH"]:], _head_pad(w_in[:, o:o + d["NH"]], d)], axis=1)


def _w_in_unperm(g, d):
    o = d["DI"] + d["CD"]
    rest = d["S5W"] + 2 * d["D"]
    return jnp.concatenate([g[:, :o], _head_unpad(g[:, o + rest:], d), g[:, o:o + rest]], axis=1)


def _s5_block_diag(v, d):
    nsb, gsb = d["NSB"], d["GSB"]
    a, b = v.shape[1:]
    v = v.reshape(nsb, gsb, a, b)
    eye = jnp.eye(gsb, dtype=v.dtype)
    return (v[:, :, :, None, :] * eye[None, :, None, :, None]).reshape(nsb * gsb * a, gsb * b)


def _s5_diag_blocks(m, d, a, b):
    nsb, gsb = d["NSB"], d["GSB"]
    m = m.reshape(nsb, gsb, a, gsb, b)
    eye = jnp.eye(gsb, dtype=m.dtype)
    return jnp.sum(m * eye[None, :, None, :, None], axis=3).reshape(nsb * gsb, a, b)


def _s5_lam_rows(v, d):
    v = v.reshape(d["NSB"], 1, d["GSB"] * d["S5N"])
    return jnp.broadcast_to(v, (d["NSB"], 8, v.shape[-1])).reshape(d["NSB"] * 8, -1)


def _ffn_fwd(h, pre_g, post_g, wgu, wd, tag):
    D = h.shape[1]
    H2 = wgu.shape[1]
    xn = _row_kernel(f"{tag}_norm", _fwd_of(_f_norm), [(h, D, 0)], [pre_g], [(D, bf16)])[0]
    ab = _mm(xn, wgu, "nn", f32, f"{tag}_mm_up")
    hid = _row_kernel(f"{tag}_swiglu", _swiglu_fwd, [(ab, H2, 0)], [], [(H2 // 2, bf16)])[0]
    f = _mm(hid, wd, "nn", f32, f"{tag}_mm_down")
    out = _row_kernel(f"{tag}_resnorm", _fwd_of(_f_resnorm(0.5)), [(h, D, 0), (f, D, 0)], [post_g], [(D, f32)])[0]
    return out, dict(h=h, xn=xn, ab=ab, hid=hid, f=f)


def _ffn_bwd(dh_out, s, pre_g, post_g, wgu, wd, tag):
    D = dh_out.shape[1]
    H2 = wgu.shape[1]
    df, dpost = _row_kernel(f"{tag}_resnorm_bwd", _vjp_of(_f_post(0.5), 1, 1, [0]), [(s["f"], D, 0), (dh_out, D, 0)],
                            [post_g], [(D, bf16)], [post_g.shape])
    dwd = _mm(s["hid"], df, "tn", f32, f"{tag}_mm_dwd")
    dhid = _mm(df, wd, "nt", bf16, f"{tag}_mm_dhid")
    dab = _row_kernel(f"{tag}_swiglu_bwd", _swiglu_bwd, [(s["ab"], H2, 0), (dhid, H2 // 2, 0)], [], [(H2, bf16)])[0]
    dwgu = _mm(s["xn"], dab, "tn", f32, f"{tag}_mm_dwgu")
    dxn = _mm(dab, wgu, "nt", f32, f"{tag}_mm_dxn")
    dh, dpre = _row_kernel(f"{tag}_norm_bwd", _vjp_of(_f_norm, 1, 1, [0], 1), [(s["h"], D, 0), (dxn, D, 0), (dh_out, D, 0)],
                           [pre_g], [(D, f32)], [pre_g.shape])
    return dh, dict(pre_g=dpre, post_g=dpost, wgu=dwgu, wd=dwd)


def _mixer_fwd(h, p, d):
    D, DI, CD, G, N = d["D"], d["DI"], d["CD"], d["G"], d["N"]
    gl = G * LANES
    c_u5, c_ga, c_gb, c_dt = DI + CD, DI + CD + d["S5W"], DI + CD + d["S5W"] + D, DI + CD + d["S5W"] + 2 * D
    u = _row_kernel("mix_norm", _fwd_of(_f_norm), [(h, D, 0)], [p["mix_pre_g"]], [(D, bf16)])[0]
    proj = _mm(u, p["w_in"], "nn", f32, "mix_mm_in", bn_t=512)
    act = _conv_fwd(proj, DI, p["conv_w"], p["conv_b"], "ssd_conv")
    dt, adt = _row_kernel("ssd_dt", _fwd_of(_f_dt), [(proj, gl, c_dt // gl)], [p["dt_bias"], p["a_log"]], [(gl, f32)] * 2)
    y_ssd, states = _ssd_fwd(act, dt, adt, p["d_skip"], d["HPG"], d["P"], N, "ssd_scan")
    nrm = _row_kernel("ssd_post", _fwd_of(_f_ssdpost(G)), [(y_ssd, DI, 0), (proj, DI, 0)], [p["norm_g"]], [(DI, bf16)])[0]
    y_a = _mm(nrm, p["w_a"], "nn", f32, "mix_mm_a")
    u5 =(proj, d["S5W"], c_u5 // d["S5W"])
    bu = _s5_in(proj, c_u5, p["bsb"], d)
    s5st = _s5_scan_fwd(bu, p["lam_re_rows"], p["lam_im_rows"], d["NSB"], "s5_scan")
    y5 = _bdmm(s5st, p["csb"], "nn", d["NSB"], f32, "s5_mm_c")
    gel = _row_kernel("s5_post", _fwd_of(_f_s5post), [(y5, d["S5W"], 0), u5], [p["s5_d"]], [(d["S5W"], bf16)])[0]
    vg = _mm(gel, p["w_glu"], "nn", f32, "mix_mm_glu")
    glu = _row_kernel("s5_glu", _glu_fwd, [(vg, vg.shape[1], 0)], [], [(vg.shape[1] // 2, bf16)])[0]
    y_b = _mm(glu, p["w_b"], "nn", f32, "mix_mm_b")
    merged = _row_kernel("mix_merge", _fwd_of(_f_merge), [(proj, D, c_ga // D), (y_a, D, 0), (proj, D, c_gb // D), (y_b, D, 0)],
                         [], [(D, bf16)])[0]
    m = _mm(merged, p["w_out"], "nn", f32, "mix_mm_out")
    out = _row_kernel("mix_resnorm", _fwd_of(_f_resnorm(1.0)), [(h, D, 0), (m, D, 0)], [p["mix_post_g"]], [(D, f32)])[0]
    return out, dict(h=h, u=u, proj=proj, act=act, dt=dt, adt=adt, states=states, y_ssd=y_ssd, nrm=nrm, y_a=y_a, s5st=s5st,
                     y5=y5, gel=gel, vg=vg, glu=glu, y_b=y_b, merged=merged, m=m)


def _s5_in(proj, c_u5, bsb, d):
    T = proj.shape[0]
    nsb = d["NSB"]
    ka, nw = S5_SUPERBLOCK, bsb.shape[1]
    off = c_u5 // ka
    assert c_u5 % ka == 0
    bt = _pick(T, 512)

    def body(a_ref, w_ref, o_ref):
        o_ref[...] = _dot(a_ref[...].astype(bf16), w_ref[...].astype(bf16))

    return pl.pallas_call(
        body, name="s5_mm_bu", grid=(nsb, T // bt),
        in_specs=[pl.BlockSpec((bt, ka), lambda j, i: (i, off + j)), pl.BlockSpec((ka, nw), lambda j, i: (j, 0))],
        out_specs=pl.BlockSpec((bt, nw), lambda j, i: (i, j)), out_shape=jax.ShapeDtypeStruct((T, nsb * nw), f32),
        compiler_params=_params(("parallel", "parallel")),
    )(proj, bsb)


def _s5_dbsb(proj, c_u5, a, d):
    T = proj.shape[0]
    nsb = d["NSB"]
    ka, nw = S5_SUPERBLOCK, a.shape[1] // nsb
    off = c_u5 // ka
    bt = _pick(T, 512)

    def body(u_ref, a_ref, o_ref):
        pr = _dot(u_ref[...].astype(bf16), a_ref[...].astype(bf16), "tn")
        k = pl.program_id(1)

        @pl.when(k == 0)
        def _():
            o_ref[...] = pr

        @pl.when(k > 0)
        def _():
            o_ref[...] += pr

    return pl.pallas_call(
        body, name="s5_mm_dbsb", grid=(nsb, T // bt),
        in_specs=[pl.BlockSpec((bt, ka), lambda j, k: (k, off + j)), pl.BlockSpec((bt, nw), lambda j, k: (k, j))],
        out_specs=pl.BlockSpec((ka, nw), lambda j, k: (j, 0)), out_shape=jax.ShapeDtypeStruct((nsb * ka, nw), f32),
        compiler_params=_params(("parallel", "arbitrary")),
    )(proj, a)


def _mixer_bwd(dh_out, s, p, d):
    D, DI, CD, G, N, S5W = d["D"], d["DI"], d["CD"], d["G"], d["N"], d["S5W"]
    gl = G * LANES
    gn = G * N
    c_u5, c_ga, c_gb, c_dt = DI + CD, DI + CD + S5W, DI + CD + S5W + D, DI + CD + S5W + 2 * D
    proj = s["proj"]
    g = {}
    dm, g["mix_post_g"] = _row_kernel("mix_resnorm_bwd", _vjp_of(_f_post(1.0), 1, 1, [0]), [(s["m"], D, 0), (dh_out, D, 0)],
                                      [p["mix_post_g"]], [(D, bf16)], [p["mix_post_g"].shape])
    g["w_out"] = _mm(s["merged"], dm, "tn", f32, "mix_mm_dwout")
    dmerged = _mm(dm, p["w_out"], "nt", f32, "mix_mm_dmerged")
    dga, dya, dgb, dyb = _row_kernel(
        "mix_merge_bwd", _vjp_of(_f_merge, 4, 1, [0, 1, 2, 3]),
        [(proj, D, c_ga // D), (s["y_a"], D, 0), (proj, D, c_gb // D), (s["y_b"], D, 0), (dmerged, D, 0)], [],
        [(D, bf16), (D, bf16), (D, bf16), (D, bf16)])
    g["w_a"] = _mm(s["nrm"], dya, "tn", f32, "mix_mm_dwa")
    dnrm = _mm(dya, p["w_a"], "nt", f32, "mix_mm_dnrm")
    dy_ssd, dz, g["norm_g"] = _row_kernel(
        "ssd_post_bwd", _vjp_of(_f_ssdpost(G), 2, 1, [0, 1]), [(s["y_ssd"], DI, 0), (proj, DI, 0), (dnrm, DI, 0)],
        [p["norm_g"]], [(DI, f32), (DI, bf16)], [p["norm_g"].shape])
    dxs, d_b, d_c, ddt, dadt, dd = _ssd_bwd(s["act"], s["dt"], s["adt"], p["d_skip"], s["states"], dy_ssd,
                                            d["HPG"], d["P"], N, "ssd_scan_bwd")
    g["d_skip"] = dd.reshape(G, 8, LANES)[:, 0, :].reshape(1, gl)
    ddt_raw, g["dt_bias"], g["a_log"] = _row_kernel(
        "ssd_dt_bwd", _vjp_of(_f_dt, 1, 2, [0]), [(proj, gl, c_dt // gl), (ddt, gl, 0), (dadt, gl, 0)],
        [p["dt_bias"], p["a_log"]], [(gl, bf16)], [p["dt_bias"].shape, p["a_log"].shape])
    cw, cb = p["conv_w"], p["conv_b"]
    dxc_x, dw_x, db_x = _conv_bwd(proj, DI, cw[:, :DI], cb[:, :DI], dxs, "ssd_conv_bwd_x")
    dxc_b, dw_b, db_b = _conv_bwd(proj, 2 * DI, cw[:, DI:DI + gn], cb[:, DI:DI + gn], d_b, "ssd_conv_bwd_b")
    dxc_c, dw_c, db_c = _conv_bwd(proj, 2 * DI + gn, cw[:, DI + gn:], cb[:, DI + gn:], d_c, "ssd_conv_bwd_c")
    g["conv_w"] = jnp.concatenate([dw_x, dw_b, dw_c], axis=1)
    g["conv_b"] = jnp.concatenate([db_x, db_b, db_c], axis=1)
    g["w_b"] = _mm(s["glu"], dyb, "tn", f32, "mix_mm_dwb")
    dglu = _mm(dyb, p["w_b"], "nt", f32, "mix_mm_dglu")
    dvg = _row_kernel("s5_glu_bwd", _glu_bwd, [(s["vg"], s["vg"].shape[1], 0), (dglu, S5W, 0)], [], [(s["vg"].shape[1], bf16)])[0]
    g["w_glu"] = _mm(s["gel"], dvg, "tn", f32, "mix_mm_dwglu")
    dgel = _mm(dvg, p["w_glu"], "nt", f32, "mix_mm_dgel")
    dy5, du5a, g["s5_d"] = _row_kernel(
        "s5_post_bwd", _vjp_of(_f_s5post, 2, 1, [0, 1]), [(s["y5"], S5W, 0), (proj, S5W, c_u5 // S5W), (dgel, S5W, 0)],
        [p["s5_d"]], [(S5W, bf16), (S5W, f32)], [p["s5_d"].shape])
    g["csb"] = _bdmm(s["s5st"], dy5, "tn", d["NSB"], f32, "s5_mm_dcsb")
    gst = _bdmm(dy5, p["csb"], "nt", d["NSB"], f32, "s5_mm_gst")
    a, g["lam_re_rows"], g["lam_im_rows"] = _s5_scan_bwd(gst, s["s5st"], p["lam_re_rows"], p["lam_im_rows"], d["NSB"], "s5_scan_bwd")
    g["bsb"] = _s5_dbsb(proj, c_u5, a, d)
    du5b = _bdmm(a, p["bsb"], "nt", d["NSB"], f32, "s5_mm_du5")
    du5 = _row_kernel("s5_du5", _add_fn, [(du5a, S5W, 0), (du5b, S5W, 0)], [], [(S5W, bf16)])[0]
    dproj = jnp.concatenate([dz, dxc_x, dxc_b, dxc_c, du5, dga, dgb, ddt_raw], axis=1)
    g["w_in"] = _mm(s["u"], dproj, "tn", f32, "mix_mm_dwin", bn_t=512)
    du = _mm(dproj, p["w_in"], "nt", f32, "mix_mm_du", bk_t=2176)
    dh, g["mix_pre_g"] = _row_kernel("mix_norm_bwd", _vjp_of(_f_norm, 1, 1, [0], 1), [(s["h"], D, 0), (du, D, 0), (dh_out, D, 0)],
                                     [p["mix_pre_g"]], [(D, f32)], [p["mix_pre_g"].shape])
    return dh, g


def _layer_params(l, w, wf, conv_w_full, d):
    r2 = lambda v: v[l].reshape(1, -1)
    p = {}
    for n in ["ffn1_pre_g", "ffn1_post_g", "mix_pre_g", "mix_post_g", "ffn2_pre_g", "ffn2_post_g", "s5_d"]:
        p[n] = r2(w[n])
    p["wgu1"] = jnp.concatenate([wf["ffn1_w_gate"][l], wf["ffn1_w_up"][l]], axis=1)
    p["wd1"] = wf["ffn1_w_down"][l]
    p["wgu2"] = jnp.concatenate([wf["ffn2_w_gate"][l], wf["ffn2_w_up"][l]], axis=1)
    p["wd2"] = wf["ffn2_w_down"][l]
    p["w_in"] = _w_in_perm(wf["w_in"][l], d)
    p["w_a"], p["w_glu"], p["w_b"], p["w_out"] = wf["w_branch_a"][l], wf["s5_w_glu"][l], wf["w_branch_b"][l], wf["w_out"][l]
    p["conv_w"] = conv_w_full[l]
    p["conv_b"] = r2(w["ssd_conv_b"])
    p["dt_bias"] = _head_pad(r2(w["ssd_dt_bias"]), d)
    p["a_log"] = _head_pad(r2(w["ssd_a_log"]), d)
    p["d_skip"] = _head_pad(r2(w["ssd_d"]), d)
    p["norm_g"] = r2(w["ssd_norm_g"])
    g5, n5, c5 = d["S5G"], d["S5N"], d["S5C"]
    expand = jnp.repeat(jnp.eye(n5, dtype=f32), c5, axis=1)
    prep_in = [w["s5_lambda_re"][l], w["s5_lambda_im"][l], w["s5_log_step"][l].reshape(g5, 1),
               w["s5_b_re"][l].reshape(g5, n5 * c5), w["s5_b_im"][l].reshape(g5, n5 * c5), expand]
    lbr, lbi, bbr, bbi = _s5_prep(prep_in, "s5_prep")
    p["s5_prep_in"] = prep_in
    p["lam_re_rows"], p["lam_im_rows"] = _s5_lam_rows(lbr, d), _s5_lam_rows(lbi, d)
    to_cn = lambda v: v.reshape(g5, n5, c5).transpose(0, 2, 1)
    p["bsb"] = jnp.concatenate([_s5_block_diag(to_cn(bbr), d), _s5_block_diag(to_cn(bbi), d)], axis=1).astype(bf16)
    c_re, c_im = w["s5_c_re"][l].transpose(0, 2, 1), w["s5_c_im"][l].transpose(0, 2, 1)
    nsb = d["NSB"]
    csb = jnp.stack([_s5_block_diag(c_re, d).reshape(nsb, -1, S5_SUPERBLOCK),
                     _s5_block_diag(-c_im, d).reshape(nsb, -1, S5_SUPERBLOCK)], axis=1)
    p["csb"] = csb.reshape(-1, S5_SUPERBLOCK).astype(bf16)
    return p


def _s5_param_grads(g, p, d, l):
    g5, n5, c5, nsb, gsb = d["S5G"], d["S5N"], d["S5C"], d["NSB"], d["GSB"]
    wst = gsb * n5
    dbsb = g["bsb"]
    from_cn = lambda v: v.transpose(0, 2, 1).reshape(g5, n5 * c5)
    dbbr = from_cn(_s5_diag_blocks(dbsb[:, :wst], d, c5, n5))
    dbbi = from_cn(_s5_diag_blocks(dbsb[:, wst:], d, c5, n5))
    rows = lambda v: v.reshape(nsb, 8, wst)[:, 0, :].reshape(g5, n5)
    cots = [rows(g["lam_re_rows"]), rows(g["lam_im_rows"]), dbbr, dbbi]
    dlr, dli, dls, dbr, dbi = _s5_prep_bwd(p["s5_prep_in"], cots, "s5_prep_bwd")
    dcsb = g["csb"].reshape(nsb, 2, wst, S5_SUPERBLOCK)
    dcr = _s5_diag_blocks(dcsb[:, 0].reshape(-1, S5_SUPERBLOCK), d, n5, c5).transpose(0, 2, 1)
    dci = -_s5_diag_blocks(dcsb[:, 1].reshape(-1, S5_SUPERBLOCK), d, n5, c5).transpose(0, 2, 1)
    return dict(s5_lambda_re=dlr, s5_lambda_im=dli, s5_log_step=dls.reshape(g5), s5_b_re=dbr.reshape(g5, n5, c5),
                s5_b_im=dbi.reshape(g5, n5, c5), s5_c_re=dcr, s5_c_im=dci)


def kernel(x, ffn1_pre_g, ffn1_post_g, ffn1_w_gate, ffn1_w_up, ffn1_w_down, mix_pre_g, mix_post_g, w_in, ssd_conv_w, ssd_conv_b, ssd_dt_bias, ssd_a_log, ssd_d, ssd_norm_g, w_branch_a, s5_lambda_re, s5_lambda_im, s5_b_re, s5_b_im, s5_c_re, s5_c_im, s5_log_step, s5_d, s5_w_glu, w_branch_b, w_out, ffn2_pre_g, ffn2_post_g, ffn2_w_gate, ffn2_w_up, ffn2_w_down, loss_target, m_ffn1_pre_g, m_ffn1_post_g, m_ffn1_w_gate, m_ffn1_w_up, m_ffn1_w_down, m_mix_pre_g, m_mix_post_g, m_w_in, m_ssd_conv_w, m_ssd_conv_b, m_ssd_dt_bias, m_ssd_a_log, m_ssd_d, m_ssd_norm_g, m_w_branch_a, m_s5_lambda_re, m_s5_lambda_im, m_s5_b_re, m_s5_b_im, m_s5_c_re, m_s5_c_im, m_s5_log_step, m_s5_d, m_s5_w_glu, m_w_branch_b, m_w_out, m_ffn2_pre_g, m_ffn2_post_g, m_ffn2_w_gate, m_ffn2_w_up, m_ffn2_w_down, v_ffn1_pre_g, v_ffn1_post_g, v_ffn1_w_gate, v_ffn1_w_up, v_ffn1_w_down, v_mix_pre_g, v_mix_post_g, v_w_in, v_ssd_conv_w, v_ssd_conv_b, v_ssd_dt_bias, v_ssd_a_log, v_ssd_d, v_ssd_norm_g, v_w_branch_a, v_s5_lambda_re, v_s5_lambda_im, v_s5_b_re, v_s5_b_im, v_s5_c_re, v_s5_c_im, v_s5_log_step, v_s5_d, v_s5_w_glu, v_w_branch_b, v_w_out, v_ffn2_pre_g, v_ffn2_post_g, v_ffn2_w_gate, v_ffn2_w_up, v_ffn2_w_down):
    given = dict(locals())
    w = {n: given[n] for n in WEIGHTS}
    mom = {n: given["m_" + n] for n in WEIGHTS}
    var = {n: given["v_" + n] for n in WEIGHTS}
    d = _dims(w, x)
    n_layers = w["ffn1_pre_g"].shape[0]
    T, D = d["T"], d["D"]

    gathered = _chip_all_gather(_pack_weights(w, w["ssd_conv_w"]), "gather_weights")
    wf, conv_w_full = _unpack_weights(gathered, w, w["ssd_conv_w"])
    layers = [_layer_params(l, w, wf, conv_w_full, d) for l in range(n_layers)]

    h = x.reshape(T, D)
    saved = []
    for p in layers:
        h, s1 = _ffn_fwd(h, p["ffn1_pre_g"], p["ffn1_post_g"], p["wgu1"], p["wd1"], "ffn1")
        h, sm = _mixer_fwd(h, p, d)
        h, s2 = _ffn_fwd(h, p["ffn2_pre_g"], p["ffn2_post_g"], p["wgu2"], p["wd2"], "ffn2")
        saved.append((s1, sm, s2))
    dh, loss_part = _row_kernel("loss", _loss_fn, [(h, D, 0), (loss_target.reshape(T, D), D, 0)], [], [(D, f32)], [(8, LANES)])
    loss = lax.psum(loss_part[0, 0], ("x", "y", "c"))

    lg = [None] * n_layers
    for l in reversed(range(n_layers)):
        p = layers[l]
        s1, sm, s2 = saved[l]
        dh, g2 = _ffn_bwd(dh, s2, p["ffn2_pre_g"], p["ffn2_post_g"], p["wgu2"], p["wd2"], "ffn2")
        dh, gm = _mixer_bwd(dh, sm, p, d)
        dh, g1 = _ffn_bwd(dh, s1, p["ffn1_pre_g"], p["ffn1_post_g"], p["wgu1"], p["wd1"], "ffn1")
        H = p["wd1"].shape[0]
        gl = dict(ffn1_pre_g=g1["pre_g"], ffn1_post_g=g1["post_g"], ffn1_w_gate=g1["wgu"][:, :H], ffn1_w_up=g1["wgu"][:, H:],
                  ffn1_w_down=g1["wd"], ffn2_pre_g=g2["pre_g"], ffn2_post_g=g2["post_g"], ffn2_w_gate=g2["wgu"][:, :H],
                  ffn2_w_up=g2["wgu"][:, H:], ffn2_w_down=g2["wd"], mix_pre_g=gm["mix_pre_g"], mix_post_g=gm["mix_post_g"],
                  w_in=_w_in_unperm(gm["w_in"], d), ssd_conv_w=gm["conv_w"], ssd_conv_b=gm["conv_b"],
                  ssd_dt_bias=_head_unpad(gm["dt_bias"], d), ssd_a_log=_head_unpad(gm["a_log"], d),
                  ssd_d=_head_unpad(gm["d_skip"], d), ssd_norm_g=gm["norm_g"], w_branch_a=gm["w_a"], s5_d=gm["s5_d"],
                  s5_w_glu=gm["w_glu"], w_branch_b=gm["w_b"], w_out=gm["w_out"])
        gl.update(_s5_param_grads(gm, p, d, l))
        lg[l] = gl
    grad_x = dh.reshape(x.shape)
    full_shape = {n: w[n].shape[1:] for n in WEIGHTS}
    for n in BIG + ["ssd_conv_w"]:
        ax = _shard_axis(n) - 1 if n in BIG else 1
        full_shape[n] = tuple(s * N_CHIPS if i == ax else s for i, s in enumerate(w[n].shape[1:]))
    part = {n: jnp.stack([lg[l][n].reshape(full_shape[n]) for l in range(n_layers)]) for n in WEIGHTS}

    packed = _pack_big_grads(part)
    my_core = lax.axis_index("c").astype(jnp.int32).reshape(1)
    chip_part = _add_my_half(packed, _core_send_other_half(packed, "exchange_core_halves"), my_core, "sum_core_halves")
    my_sum = _sum_slots(_chip_scatter(chip_part, "scatter_grads"), "sum_chip_parts")
    grads = _unpack_big_grads(_core_join_halves(my_sum, "join_core_halves"), w)
    small_names = SMALL + ["ssd_conv_w"]
    everyone = _device_all_gather(_pack_small(part, small_names), "gather_small_grads")
    small = _unpack_small(_sum_slots(everyone, "sum_small_grads"), part, small_names)
    k_me = 2 * lax.axis_index("x") + lax.axis_index("y")
    cw = w["ssd_conv_w"].shape[-1]
    small["ssd_conv_w"] = lax.dynamic_slice_in_dim(small["ssd_conv_w"], k_me * cw, cw, axis=2)
    grads.update(small)

    delta, new_m, new_v = {}, {}, {}
    for n in BIG + ["ssd_conv_w"]:
        delta[n], new_m[n], new_v[n] = _adamw(w[n], grads[n], mom[n], var[n], "adamw_" + n)
    packs = [_pack_small(t, SMALL) for t in (w, grads, mom, var)]
    sd, sm_, sv = _adamw(*packs, "adamw_small")
    delta.update(_unpack_small(sd, w, SMALL))
    new_m.update(_unpack_small(sm_, w, SMALL))
    new_v.update(_unpack_small(sv, w, SMALL))
    return (loss, grad_x, *[grads[n] for n in WEIGHTS], *[delta[n] for n in WEIGHTS],
            *[new_m[n] for n in WEIGHTS], *[new_v[n] for n in WEIGHTS])
```

```python
import functools

import numpy as np
import jax
import jax.numpy as jnp
from jax import lax
from jax.experimental import pallas as pl
from jax.experimental.pallas import tpu as pltpu

f32, bf16 = jnp.float32, jnp.bfloat16

SSD_N_GROUPS = 4
SSD_CHUNK = 128
RMS_EPS = 1e-6
S5_MAX_REAL = -1e-4
S5_SUPERBLOCK = 256
ADAM_LR, ADAM_B1, ADAM_B2, ADAM_EPS, ADAM_WD, ADAM_STEP = 0.001, 0.9, 0.999, 1e-08, 0.01, 10

LANES = 128
PACK_COLS = 1024
D2D_STREAMS = 16
PACK_ROW_MULT = 2 * D2D_STREAMS * 16
VMEM_LIMIT_BYTES = 48 * 1024 * 1024
N_CHIPS, N_CORES, N_DEV = 4, 2, 8
MESH = pl.DeviceIdType.MESH

BIG = ["ffn1_w_gate", "ffn1_w_up", "ffn1_w_down", "w_in", "w_branch_a", "s5_w_glu", "w_branch_b", "w_out",
       "ffn2_w_gate", "ffn2_w_up", "ffn2_w_down"]
COL_SHARDED = {"ffn1_w_gate", "ffn1_w_up", "w_in", "s5_w_glu", "ffn2_w_gate", "ffn2_w_up"}
SMALL = ["ffn1_pre_g", "ffn1_post_g", "mix_pre_g", "mix_post_g", "ssd_conv_b", "ssd_dt_bias", "ssd_a_log", "ssd_d",
         "ssd_norm_g", "s5_lambda_re", "s5_lambda_im", "s5_b_re", "s5_b_im", "s5_c_re", "s5_c_im", "s5_log_step",
         "s5_d", "ffn2_pre_g", "ffn2_post_g"]
WEIGHTS = ["ffn1_pre_g", "ffn1_post_g", "ffn1_w_gate", "ffn1_w_up", "ffn1_w_down", "mix_pre_g", "mix_post_g", "w_in",
           "ssd_conv_w", "ssd_conv_b", "ssd_dt_bias", "ssd_a_log", "ssd_d", "ssd_norm_g", "w_branch_a", "s5_lambda_re",
           "s5_lambda_im", "s5_b_re", "s5_b_im", "s5_c_re", "s5_c_im", "s5_log_step", "s5_d", "s5_w_glu", "w_branch_b",
           "w_out", "ffn2_pre_g", "ffn2_post_g", "ffn2_w_gate", "ffn2_w_up", "ffn2_w_down"]


def _params(sem=None):
    return pltpu.CompilerParams(dimension_semantics=sem, vmem_limit_bytes=VMEM_LIMIT_BYTES)


def _pick(n, target, mult=LANES):
    best = None
    for d in range(mult, min(n, target) + 1, mult):
        if n % d == 0:
            best = d
    return best if best is not None else n


_DIMS = {"nn": (((1,), (0,)), ((), ())), "nt": (((1,), (1,)), ((), ())), "tn": (((0,), (0,)), ((), ()))}


def _mm(a, b, mode, out_dtype, name, bm_t=1024, bn_t=1024, bk_t=2816):
    if mode == "nn":
        (M, K), (K2, N) = a.shape, b.shape
    elif mode == "nt":
        (M, K), (N, K2) = a.shape, b.shape
    else:
        (K, M), (K2, N) = a.shape, b.shape
    assert K == K2, (name, a.shape, b.shape)
    bm, bn, bk = _pick(M, bm_t), _pick(N, bn_t), _pick(K, bk_t)
    nk = K // bk
    dn = _DIMS[mode]

    def body(a_ref, b_ref, o_ref, *scratch):
        p = lax.dot_general(a_ref[...].astype(bf16), b_ref[...].astype(bf16), dn, preferred_element_type=f32)
        if nk == 1:
            o_ref[...] = p.astype(o_ref.dtype)
        else:
            acc = scratch[0]
            k = pl.program_id(2)

            @pl.when(k == 0)
            def _():
                acc[...] = p

            @pl.when(k > 0)
            def _():
                acc[...] += p

            @pl.when(k == nk - 1)
            def _():
                o_ref[...] = acc[...].astype(o_ref.dtype)

    if mode == "tn":
        a_spec = pl.BlockSpec((bk, bm), lambda i, j, k: (k, i))
    else:
        a_spec = pl.BlockSpec((bm, bk), lambda i, j, k: (i, k))
    if mode == "nt":
        b_spec = pl.BlockSpec((bn, bk), lambda i, j, k: (j, k))
    else:
        b_spec = pl.BlockSpec((bk, bn), lambda i, j, k: (k, j))
    return pl.pallas_call(
        body, name=name, grid=(M // bm, N // bn, nk), in_specs=[a_spec, b_spec],
        out_specs=pl.BlockSpec((bm, bn), lambda i, j, k: (i, j)), out_shape=jax.ShapeDtypeStruct((M, N), out_dtype),
        scratch_shapes=[pltpu.VMEM((bm, bn), f32)] if nk > 1 else [],
        compiler_params=_params(("parallel", "parallel", "arbitrary")),
    )(a, b)


def _bdmm(a, w, mode, nb, out_dtype, name, bt_t=512):
    if mode == "tn":
        T = a.shape[0]
        ka, nw = a.shape[1] // nb, w.shape[1] // nb
        bt = _pick(T, bt_t)
        nt = T // bt

        def body_tn(a_ref, b_ref, o_ref):
            p = lax.dot_general(a_ref[...].astype(bf16), b_ref[...].astype(bf16), _DIMS["tn"], preferred_element_type=f32)
            k = pl.program_id(1)

            @pl.when(k == 0)
            def _():
                o_ref[...] = p

            @pl.when(k > 0)
            def _():
                o_ref[...] += p

        return pl.pallas_call(
            body_tn, name=name, grid=(nb, nt),
            in_specs=[pl.BlockSpec((bt, ka), lambda j, k: (k, j)), pl.BlockSpec((bt, nw), lambda j, k: (k, j))],
            out_specs=pl.BlockSpec((ka, nw), lambda j, k: (j, 0)), out_shape=jax.ShapeDtypeStruct((nb * ka, nw), f32),
            compiler_params=_params(("parallel", "arbitrary")),
        )(a, w)
    T = a.shape[0]
    ka, nw = w.shape[0] // nb, w.shape[1]
    bt = _pick(T, bt_t)
    kin, kout = (ka, nw) if mode == "nn" else (nw, ka)
    dn = _DIMS[mode]

    def body(a_ref, w_ref, o_ref):
        o_ref[...] = lax.dot_general(a_ref[...].astype(bf16), w_ref[...].astype(bf16), dn,
                                     preferred_element_type=f32).astype(o_ref.dtype)

    return pl.pallas_call(
        body, name=name, grid=(nb, T // bt),
        in_specs=[pl.BlockSpec((bt, kin), lambda j, i: (i, j)), pl.BlockSpec((ka, nw), lambda j, i: (j, 0))],
        out_specs=pl.BlockSpec((bt, kout), lambda j, i: (i, j)), out_shape=jax.ShapeDtypeStruct((T, nb * kout), out_dtype),
        compiler_params=_params(("parallel", "parallel")),
    )(a, w)


def _row_index(i, cb):
    return (i, cb)


def _row_kernel(name, fn, rows, pars, row_outs, par_outs=(), block_rows=256):
    T = rows[0][0].shape[0]
    R = min(block_rows, T)
    assert T % R == 0
    nr, npar, nro = len(rows), len(pars), len(row_outs)

    def body(*refs):
        rv = [r[...] for r in refs[:nr]]
        pv = [r[...] for r in refs[nr:nr + npar]]
        ro, po = fn(rv, pv)
        for ref, v in zip(refs[nr + npar:nr + npar + nro], ro):
            ref[...] = v.astype(ref.dtype)
        if par_outs:
            i = pl.program_id(0)
            prefs = refs[nr + npar + nro:]

            @pl.when(i == 0)
            def _():
                for ref, v in zip(prefs, po):
                    ref[...] = v.astype(f32)

            @pl.when(i > 0)
            def _():
                for ref, v in zip(prefs, po):
                    ref[...] += v.astype(f32)

    in_specs = [pl.BlockSpec((R, nc), functools.partial(_row_index, cb=cb)) for (_, nc, cb) in rows]
    in_specs += [pl.BlockSpec(p.shape, lambda i: (0, 0)) for p in pars]
    out_specs = [pl.BlockSpec((R, nc), lambda i: (i, 0)) for (nc, _) in row_outs]
    out_specs += [pl.BlockSpec(s, lambda i: (0, 0)) for s in par_outs]
    out_shape = [jax.ShapeDtypeStruct((T, nc), dt) for (nc, dt) in row_outs]
    out_shape += [jax.ShapeDtypeStruct(s, f32) for s in par_outs]
    outs = pl.pallas_call(
        body, name=name, grid=(T // R,), in_specs=in_specs, out_specs=out_specs, out_shape=out_shape,
        compiler_params=_params(("arbitrary",) if par_outs else ("parallel",)),
    )(*[r[0] for r in rows], *pars)
    return list(outs)


def _fwd_of(f):
    def fn(rv, pv):
        return f([v.astype(f32) for v in rv], [v.astype(f32) for v in pv]), []
    return fn


def _vjp_of(f, n_x, n_cot, grad_idx, n_add=0):
    def fn(rv, pv):
        xs = [v.astype(f32) for v in rv[:n_x]]
        cots = [v.astype(f32) for v in rv[n_x:n_x + n_cot]]
        adds = rv[n_x + n_cot:n_x + n_cot + n_add]
        ps = [v.astype(f32) for v in pv]
        _, vjp = jax.vjp(lambda *a: f(list(a[:n_x]), list(a[n_x:])), *xs, *ps)
        g = vjp(cots)
        row_g = [g[i] for i in grad_idx]
        for k, a in enumerate(adds):
            row_g[k] = row_g[k] + a.astype(f32)
        return row_g, list(g[n_x:])
    return fn


def _rms(x, g):
    return x * lax.rsqrt(jnp.mean(x * x, axis=-1, keepdims=True) + RMS_EPS) * g


def _f_norm(xs, ps):
    return [_rms(xs[0], ps[0])]


def _f_post(scale):
    def f(xs, ps):
        return [scale * _rms(xs[0], ps[0])]
    return f


def _f_resnorm(scale):
    def f(xs, ps):
        return [xs[0] + scale * _rms(xs[1], ps[0])]
    return f


def _f_dt(xs, ps):
    dt = jax.nn.softplus(xs[0] + ps[0])
    return [dt, -jnp.exp(ps[1]) * dt]


def _f_ssdpost(n_groups):
    def f(xs, ps):
        y = xs[0] * jax.nn.silu(xs[1])
        width = y.shape[-1] // n_groups
        lane = lax.broadcasted_iota(jnp.int32, y.shape, 1)
        scale = jnp.zeros_like(y)
        for k in range(n_groups):
            m = ((lane >= k * width) & (lane < (k + 1) * width)).astype(f32)
            ms = jnp.sum(y * y * m, axis=-1, keepdims=True) / width
            scale = scale + lax.rsqrt(ms + RMS_EPS) * m
        return [y * scale * ps[0]]
    return f


def _f_s5post(xs, ps):
    return [jax.nn.gelu(xs[0] + ps[0] * xs[1])]


def _f_merge(xs, ps):
    return [jax.nn.sigmoid(xs[0]) * xs[1] + jax.nn.sigmoid(xs[2]) * xs[3]]


def _swiglu_fwd(rv, pv):
    ab = rv[0].astype(f32)
    h = ab.shape[1] // 2
    return [jax.nn.silu(ab[:, :h]) * ab[:, h:]], []


def _swiglu_bwd(rv, pv):
    ab, d = rv[0].astype(f32), rv[1].astype(f32)
    h = ab.shape[1] // 2
    a, b = ab[:, :h], ab[:, h:]
    s = jax.nn.sigmoid(a)
    return [jnp.concatenate([d * b * (s * (1.0 + a * (1.0 - s))), d * (a * s)], axis=1)], []


def _glu_fwd(rv, pv):
    vg = rv[0].astype(f32)
    h = vg.shape[1] // 2
    return [vg[:, :h] * jax.nn.sigmoid(vg[:, h:])], []


def _glu_bwd(rv, pv):
    vg, d = rv[0].astype(f32), rv[1].astype(f32)
    h = vg.shape[1] // 2
    s = jax.nn.sigmoid(vg[:, h:])
    return [jnp.concatenate([d * s, d * vg[:, :h] * s * (1.0 - s)], axis=1)], []


def _loss_fn(rv, pv):
    e = rv[0].astype(f32) - rv[1].astype(f32)
    per_tok = jnp.mean(e * e, axis=-1, keepdims=True)
    part = 0.5 * jnp.sum(per_tok, axis=0, keepdims=True)
    return [e / e.shape[-1]], [jnp.broadcast_to(part, (8, LANES))]


def _add_fn(rv, pv):
    return [rv[0].astype(f32) + rv[1].astype(f32)], []


def _adamw_fn(rv, pv):
    w, g, m, v = [x.astype(f32) for x in rv]
    m = ADAM_B1 * m + (1.0 - ADAM_B1) * g
    v = ADAM_B2 * v + (1.0 - ADAM_B2) * (g * g)
    m_hat = m / (1.0 - ADAM_B1 ** ADAM_STEP)
    v_hat = v / (1.0 - ADAM_B2 ** ADAM_STEP)
    return [-ADAM_LR * (m_hat / (jnp.sqrt(v_hat) + ADAM_EPS) + ADAM_WD * w), m, v], []


def _adamw(w, g, m, v, name):
    shape = w.shape
    cols = shape[-1] if (w.ndim >= 2 and shape[-1] >= LANES) else None
    if cols is None:
        n = int(np.prod(shape))
        cols = LANES if n % LANES == 0 else n
    two_d = (int(np.prod(shape)) // cols, cols)
    rows = [(t.reshape(two_d), cols, 0) for t in (w, g, m, v)]
    R = _pick(two_d[0], 256, 8)
    outs = _row_kernel(name, _adamw_fn, rows, [], [(cols, f32)] * 3, block_rows=R)
    return [o.reshape(shape) for o in outs]


def _shift_down(x, s, row):
    if s == 0:
        return x
    return jnp.where(row >= s, pltpu.roll(x, s, 0), 0.0)


def _shift_up(x, s, row):
    if s == 0:
        return x
    n = x.shape[0]
    return jnp.where(row < n - s, pltpu.roll(x, n - s, 0), 0.0)


def _conv_pre(x, w, b, row):
    kw = w.shape[0]
    c = b
    for k in range(kw):
        c = c + w[k:k + 1, :] * _shift_down(x, kw - 1 - k, row)
    return c


def _conv_fwd(xsrc, col0, w, b, name, bc_t=512):
    T = xsrc.shape[0]
    kw, ncols = w.shape
    bc = _pick(ncols, bc_t)
    off = col0 // bc
    assert col0 % bc == 0

    def body(x_ref, w_ref, b_ref, o_ref):
        x = x_ref[...].astype(f32)
        row = lax.broadcasted_iota(jnp.int32, x.shape, 0)
        c = _conv_pre(x, w_ref[...], b_ref[...], row)
        o_ref[...] = c * jax.nn.sigmoid(c)

    return pl.pallas_call(
        body, name=name, grid=(ncols // bc,),
        in_specs=[pl.BlockSpec((T, bc), lambda j: (0, off + j)), pl.BlockSpec((kw, bc), lambda j: (0, j)),
                  pl.BlockSpec((1, bc), lambda j: (0, j))],
        out_specs=pl.BlockSpec((T, bc), lambda j: (0, j)), out_shape=jax.ShapeDtypeStruct((T, ncols), f32),
        compiler_params=_params(("parallel",)),
    )(xsrc, w, b)


def _conv_bwd(xsrc, col0, w, b, dact, name, bc_t=512):
    T = xsrc.shape[0]
    kw, ncols = w.shape
    bc = _pick(ncols, bc_t)
    off = col0 // bc
    assert col0 % bc == 0

    def body(x_ref, w_ref, b_ref, d_ref, dx_ref, dw_ref, db_ref):
        x = x_ref[...].astype(f32)
        w = w_ref[...]
        row = lax.broadcasted_iota(jnp.int32, x.shape, 0)
        c = _conv_pre(x, w, b_ref[...], row)
        s = jax.nn.sigmoid(c)
        dc = d_ref[...].astype(f32) * (s * (1.0 + c * (1.0 - s)))
        dx = jnp.zeros_like(x)
        dws = []
        for k in range(kw):
            dx = dx + w[k:k + 1, :] * _shift_up(dc, kw - 1 - k, row)
            dws.append(jnp.sum(dc * _shift_down(x, kw - 1 - k, row), axis=0, keepdims=True))
        dx_ref[...] = dx.astype(dx_ref.dtype)
        dw_ref[...] = jnp.concatenate(dws, axis=0)
        db_ref[...] = jnp.sum(dc, axis=0, keepdims=True)

    return pl.pallas_call(
        body, name=name, grid=(ncols // bc,),
        in_specs=[pl.BlockSpec((T, bc), lambda j: (0, off + j)), pl.BlockSpec((kw, bc), lambda j: (0, j)),
                  pl.BlockSpec((1, bc), lambda j: (0, j)), pl.BlockSpec((T, bc), lambda j: (0, j))],
        out_specs=[pl.BlockSpec((T, bc), lambda j: (0, j)), pl.BlockSpec((kw, bc), lambda j: (0, j)),
                   pl.BlockSpec((1, bc), lambda j: (0, j))],
        out_shape=[jax.ShapeDtypeStruct((T, ncols), bf16), jax.ShapeDtypeStruct((kw, ncols), f32),
                   jax.ShapeDtypeStruct((1, ncols), f32)],
        compiler_params=_params(("parallel",)),
    )(xsrc, w, b, dact)


_HI = lax.Precision.HIGHEST


def _dot(a, b, dims="nn", precision=None):
    return lax.dot_general(a, b, _DIMS[dims], preferred_element_type=f32, precision=precision)


def _ssd_common(x_ref, b_ref, c_ref, dt_ref, adt_ref, d_ref, hpg, p):
    q = b_ref.shape[0]
    hp = hpg * p
    bb, cb = b_ref[...].astype(bf16), c_ref[...].astype(bf16)
    r = lax.broadcasted_iota(jnp.int32, (q, q), 0)
    s = lax.broadcasted_iota(jnp.int32, (q, q), 1)
    tril = r >= s
    trilf = tril.astype(f32)
    eh = lax.broadcasted_iota(jnp.int32, (LANES, hp), 0)
    ec = lax.broadcasted_iota(jnp.int32, (LANES, hp), 1)
    expand = ((ec >= eh * p) & (ec < (eh + 1) * p)).astype(f32)
    adt = adt_ref[...]
    cum = _dot(trilf, adt, "nn", _HI)
    cum_t = _dot(adt, (r <= s).astype(f32), "tn", _HI)
    cum_e = _dot(cum, expand, "nn", _HI)
    dt_e = _dot(dt_ref[...], expand, "nn", _HI)
    d_e = _dot(jnp.broadcast_to(d_ref[...], (8, LANES)), expand, "nn", _HI)[0:1, :]
    gmat = _dot(cb, bb, "nt")
    x = x_ref[...]
    xdt = x * dt_e
    e_all = jnp.exp(cum_e)
    dec = jnp.exp(cum_e[q - 1:q, :] - cum_e)
    lms, ms = [], []
    for h in range(hpg):
        lm = jnp.exp(jnp.where(tril, cum[:, h:h + 1] - cum_t[h:h + 1, :], -1e30))
        lms.append(lm)
        ms.append(gmat * lm)
    et = [jnp.exp(cum[q - 1:q, h:h + 1]) for h in range(hpg)]
    return dict(bb=bb, cb=cb, trilf=trilf, expand=expand, cum=cum, x=x, xdt=xdt, dt_e=dt_e, d_e=d_e, e=e_all, dec=dec,
                lms=lms, ms=ms, et=et)


def _ssd_specs(q, hp, n, g_n, nc, rev):
    def cidx(c):
        return (nc - 1 - c) if rev else c
    x_spec = pl.BlockSpec((q, hp), lambda g, c: (cidx(c), g))
    boff = (g_n * hp) // n
    b_spec = pl.BlockSpec((q, n), lambda g, c: (cidx(c), boff + g))
    c_spec = pl.BlockSpec((q, n), lambda g, c: (cidx(c), boff + g_n + g))
    dt_spec = pl.BlockSpec((q, LANES), lambda g, c: (cidx(c), g))
    d_spec = pl.BlockSpec((1, LANES), lambda g, c: (0, g))
    st_spec = pl.BlockSpec((1, 1, hp, n), lambda g, c: (cidx(c), g, 0, 0))
    return x_spec, b_spec, c_spec, dt_spec, d_spec, st_spec


def _ssd_fwd(act, dt, adt, dpad, hpg, p, n, name):
    T = act.shape[0]
    g_n, q = SSD_N_GROUPS, SSD_CHUNK
    nc, hp = T // q, hpg * p
    x_spec, b_spec, c_spec, dt_spec, d_spec, st_spec = _ssd_specs(q, hp, n, g_n, nc, False)

    def body(x_ref, b_ref, c_ref, dt_ref, adt_ref, d_ref, y_ref, st_ref, s_scr):
        @pl.when(pl.program_id(1) == 0)
        def _():
            s_scr[...] = jnp.zeros_like(s_scr)

        k = _ssd_common(x_ref, b_ref, c_ref, dt_ref, adt_ref, d_ref, hpg, p)
        s0 = s_scr[...]
        st_ref[0, 0] = s0
        xdtb = k["xdt"].astype(bf16)
        ydiag = [_dot(k["ms"][h].astype(bf16), xdtb[:, h * p:(h + 1) * p]) for h in range(hpg)]
        z = _dot(k["cb"], s0.astype(bf16), "nt")
        y_ref[...] = jnp.concatenate(ydiag, axis=1) + k["e"] * z + k["d_e"] * k["x"]
        upd = _dot((k["xdt"] * k["dec"]).astype(bf16), k["bb"], "tn")
        for h in range(hpg):
            s_scr[h * p:(h + 1) * p, :] = k["et"][h] * s0[h * p:(h + 1) * p, :] + upd[h * p:(h + 1) * p, :]

    return pl.pallas_call(
        body, name=name, grid=(g_n, nc),
        in_specs=[x_spec, b_spec, c_spec, dt_spec, dt_spec, d_spec],
        out_specs=[pl.BlockSpec((q, hp), lambda g, c: (c, g)), st_spec],
        out_shape=[jax.ShapeDtypeStruct((T, g_n * hp), f32), jax.ShapeDtypeStruct((nc, g_n, hp, n), f32)],
        scratch_shapes=[pltpu.VMEM((hp, n), f32)],
        compiler_params=_params(("parallel", "arbitrary")),
    )(act, act, act, dt, adt, dpad)


def _ssd_bwd(act, dt, adt, dpad, states, dy, hpg, p, n, name):
    T = act.shape[0]
    g_n, q = SSD_N_GROUPS, SSD_CHUNK
    nc, hp = T // q, hpg * p
    x_spec, b_spec, c_spec, dt_spec, d_spec, st_spec = _ssd_specs(q, hp, n, g_n, nc, True)

    def body(x_ref, b_ref, c_ref, dt_ref, adt_ref, d_ref, st_ref, dy_ref,
             dx_ref, db_ref, dc_ref, ddt_ref, dadt_ref, dd_ref, ds_scr):
        first = pl.program_id(1) == 0

        @pl.when(first)
        def _():
            ds_scr[...] = jnp.zeros_like(ds_scr)

        k = _ssd_common(x_ref, b_ref, c_ref, dt_ref, adt_ref, d_ref, hpg, p)
        bb, cb, expand, x, xdt, dec = k["bb"], k["cb"], k["expand"], k["x"], k["xdt"], k["dec"]
        heads = lambda t: _dot(t, expand, "nt", _HI)
        s0 = st_ref[0, 0]
        ds1 = ds_scr[...]
        s0b, ds1b = s0.astype(bf16), ds1.astype(bf16)
        dy = dy_ref[...]
        dyb, xdtb = dy.astype(bf16), xdt.astype(bf16)
        lane = lax.broadcasted_iota(jnp.int32, (1, LANES), 1)
        dg = jnp.zeros((q, q), f32)
        w_rows = jnp.zeros((q, LANES), f32)
        w_cols, dxdt_parts = [], []
        for h in range(hpg):
            hs = slice(h * p, (h + 1) * p)
            dm = _dot(dyb[:, hs], xdtb[:, hs], "nt")
            dg = dg + dm * k["lms"][h]
            wm = dm * k["ms"][h]
            w_rows = w_rows + jnp.sum(wm, axis=1, keepdims=True) * (lane == h).astype(f32)
            w_cols.append(jnp.sum(wm, axis=0, keepdims=True))
            dxdt_parts.append(_dot(k["ms"][h].astype(bf16), dyb[:, hs], "tn"))
        dxdt_diag = jnp.concatenate(dxdt_parts, axis=1)
        w_cols = jnp.concatenate(w_cols + [jnp.zeros((LANES - hpg, q), f32)], axis=0).T
        dgb = dg.astype(bf16)
        z = _dot(cb, s0b, "nt")
        dz = dy * k["e"]
        dzb = dz.astype(bf16)
        dxd = _dot(bb, ds1b, "nt")
        ddec = dxd * xdt * dec
        db_ref[...] = _dot(dgb, cb, "tn") + _dot((xdt * dec).astype(bf16), ds1b)
        dc_ref[...] = _dot(dgb, bb) + _dot(dzb, s0b)
        ds0 = _dot(dzb, cb, "tn")
        for h in range(hpg):
            hs = slice(h * p, (h + 1) * p)
            ds_scr[hs, :] = ds0[hs, :] + k["et"][h] * ds1[hs, :]
        dxdt = dxdt_diag + dxd * dec
        ddec_h = heads(ddec)
        dcum = w_rows - w_cols + heads(dz * z) - ddec_h
        et_row = jnp.exp(k["cum"][q - 1:q, :])
        dsum = _dot(jnp.ones((8, n), f32), _dot(expand, ds1 * s0, "nn", _HI), "nt", _HI)[0:1, :]
        dcl = dsum * et_row + jnp.sum(ddec_h, axis=0, keepdims=True)
        rowq = lax.broadcasted_iota(jnp.int32, (q, 1), 0)
        dcum = dcum + (rowq == q - 1).astype(f32) * dcl
        ddt_ref[...] = heads(dxdt * x)
        dadt_ref[...] = _dot(k["trilf"], dcum, "tn", _HI)
        dx_ref[...] = k["d_e"] * dy + dxdt * k["dt_e"]
        dd8 = heads(jnp.broadcast_to(jnp.sum(dy * x, axis=0, keepdims=True), (8, hp)))

        @pl.when(first)
        def _():
            dd_ref[...] = dd8

        @pl.when(jnp.logical_not(first))
        def _():
            dd_ref[...] += dd8

    rc = lambda g, c: (nc - 1 - c, g)
    return pl.pallas_call(
        body, name=name, grid=(g_n, nc),
        in_specs=[x_spec, b_spec, c_spec, dt_spec, dt_spec, d_spec, st_spec, pl.BlockSpec((q, hp), rc)],
        out_specs=[pl.BlockSpec((q, hp), rc), pl.BlockSpec((q, n), rc), pl.BlockSpec((q, n), rc),
                   pl.BlockSpec((q, LANES), rc), pl.BlockSpec((q, LANES), rc), pl.BlockSpec((8, LANES), lambda g, c: (g, 0))],
        out_shape=[jax.ShapeDtypeStruct((T, g_n * hp), f32), jax.ShapeDtypeStruct((T, g_n * n), f32),
                   jax.ShapeDtypeStruct((T, g_n * n), f32), jax.ShapeDtypeStruct((T, g_n * LANES), f32),
                   jax.ShapeDtypeStruct((T, g_n * LANES), f32), jax.ShapeDtypeStruct((g_n * 8, LANES), f32)],
        scratch_shapes=[pltpu.VMEM((hp, n), f32)],
        compiler_params=_params(("parallel", "arbitrary")),
    )(act, act, act, dt, adt, dpad, states, dy)


def _s5_scan_fwd(bu, lam_re, lam_im, nsb, name, tc_t=512):
    T = bu.shape[0]
    w2 = bu.shape[1] // nsb
    w = w2 // 2
    tc = _pick(T, tc_t, 8)

    def body(bu_ref, lr_ref, li_ref, st_ref, carry):
        @pl.when(pl.program_id(1) == 0)
        def _():
            carry[...] = jnp.zeros_like(carry)

        lr, li = lr_ref[0:1, :], li_ref[0:1, :]

        def step(t, s):
            sr, si = s
            row = bu_ref[pl.ds(t, 1), :]
            nsr = lr * sr - li * si + row[:, :w]
            nsi = lr * si + li * sr + row[:, w:]
            st_ref[pl.ds(t, 1), :] = jnp.concatenate([nsr, nsi], axis=1)
            return nsr, nsi

        sr, si = lax.fori_loop(0, tc, step, (carry[0:1, :], carry[1:2, :]))
        carry[0:1, :] = sr
        carry[1:2, :] = si

    return pl.pallas_call(
        body, name=name, grid=(nsb, T // tc),
        in_specs=[pl.BlockSpec((tc, w2), lambda j, i: (i, j)), pl.BlockSpec((8, w), lambda j, i: (j, 0)),
                  pl.BlockSpec((8, w), lambda j, i: (j, 0))],
        out_specs=pl.BlockSpec((tc, w2), lambda j, i: (i, j)), out_shape=jax.ShapeDtypeStruct(bu.shape, f32),
        scratch_shapes=[pltpu.VMEM((8, w), f32)],
        compiler_params=_params(("parallel", "arbitrary")),
    )(bu, lam_re, lam_im)


def _s5_scan_bwd(gst, states, lam_re, lam_im, nsb, name, tc_t=512):
    T = gst.shape[0]
    w2 = gst.shape[1] // nsb
    w = w2 // 2
    tc = _pick(T, tc_t, 8)
    nt = T // tc

    def body(g_ref, s_ref, sp_ref, lr_ref, li_ref, a_ref, dlr_ref, dli_ref, carry):
        i = pl.program_id(1)

        @pl.when(i == 0)
        def _():
            carry[...] = jnp.zeros_like(carry)

        lr, li = lr_ref[0:1, :], li_ref[0:1, :]

        def adj(t, ar, ai):
            row = g_ref[pl.ds(t, 1), :]
            nar = row[:, :w] + lr * ar + li * ai
            nai = row[:, w:] - li * ar + lr * ai
            a_ref[pl.ds(t, 1), :] = jnp.concatenate([nar, nai], axis=1)
            return nar, nai

        def acc(prev, ar, ai, dlr, dli):
            spr, spi = prev[:, :w], prev[:, w:]
            return dlr + ar * spr + ai * spi, dli - ar * spi + ai * spr

        def step(k, c):
            ar, ai, dlr, dli = c
            t = tc - 1 - k
            ar, ai = adj(t, ar, ai)
            dlr, dli = acc(s_ref[pl.ds(t - 1, 1), :], ar, ai, dlr, dli)
            return ar, ai, dlr, dli

        c0 = (carry[0:1, :], carry[1:2, :], carry[2:3, :], carry[3:4, :])
        ar, ai, dlr, dli = lax.fori_loop(0, tc - 1, step, c0)
        ar, ai = adj(0, ar, ai)
        prev = sp_ref[tc - 1:tc, :] * (i < nt - 1).astype(f32)
        dlr, dli = acc(prev, ar, ai, dlr, dli)
        carry[0:1, :] = ar
        carry[1:2, :] = ai
        carry[2:3, :] = dlr
        carry[3:4, :] = dli
        dlr_ref[...] = jnp.broadcast_to(dlr, (8, w))
        dli_ref[...] = jnp.broadcast_to(dli, (8, w))

    cur = lambda j, i: (nt - 1 - i, j)
    prv = lambda j, i: (jnp.maximum(nt - 2 - i, 0), j)
    return pl.pallas_call(
        body, name=name, grid=(nsb, nt),
        in_specs=[pl.BlockSpec((tc, w2), cur), pl.BlockSpec((tc, w2), cur), pl.BlockSpec((tc, w2), prv),
                  pl.BlockSpec((8, w), lambda j, i: (j, 0)), pl.BlockSpec((8, w), lambda j, i: (j, 0))],
        out_specs=[pl.BlockSpec((tc, w2), cur), pl.BlockSpec((8, w), lambda j, i: (j, 0)),
                   pl.BlockSpec((8, w), lambda j, i: (j, 0))],
        out_shape=[jax.ShapeDtypeStruct(gst.shape, f32), jax.ShapeDtypeStruct((nsb * 8, w), f32),
                   jax.ShapeDtypeStruct((nsb * 8, w), f32)],
        scratch_shapes=[pltpu.VMEM((8, w), f32)],
        compiler_params=_params(("parallel", "arbitrary")),
    )(gst, states, states, lam_re, lam_im)


def _s5_prep_fn(xs, ps):
    lam_re, lam_im, log_step, b_re, b_im, expand = ps
    lr = jnp.minimum(lam_re, S5_MAX_REAL)
    li = lam_im
    step = jnp.exp(log_step)
    er = jnp.exp(lr * step)
    ang = li * step
    lbr, lbi = er * jnp.cos(ang), er * jnp.sin(ang)
    nr, ni = lbr - 1.0, lbi
    den = lr * lr + li * li
    qr, qi = (nr * lr + ni * li) / den, (ni * lr - nr * li) / den
    qre, qie = _dot(qr, expand, "nn", _HI), _dot(qi, expand, "nn", _HI)
    return [lbr, lbi, qre * b_re - qie * b_im, qre * b_im + qie * b_re]


def _s5_prep(pars, name):
    def body(*refs):
        outs = _s5_prep_fn([], [r[...] for r in refs[:6]])
        for ref, v in zip(refs[6:], outs):
            ref[...] = v

    g, nst = pars[0].shape
    nc = pars[3].shape[1]
    return pl.pallas_call(
        body, name=name,
        out_shape=[jax.ShapeDtypeStruct((g, nst), f32)] * 2 + [jax.ShapeDtypeStruct((g, nc), f32)] * 2,
        compiler_params=_params(),
    )(*pars)


def _s5_prep_bwd(pars, cots, name):
    def body(*refs):
        ps = [r[...] for r in refs[:6]]
        ct = [r[...] for r in refs[6:10]]
        _, vjp = jax.vjp(lambda *a: _s5_prep_fn([], list(a)), *ps)
        g = vjp(ct)
        for ref, v in zip(refs[10:], g[:5]):
            ref[...] = v

    return pl.pallas_call(
        body, name=name, out_shape=[jax.ShapeDtypeStruct(p.shape, f32) for p in pars[:5]], compiler_params=_params(),
    )(*pars, *cots)


_ANY = pl.BlockSpec(memory_space=pl.ANY)


def _remote(src, dst, send_sem, recv_sem, device):
    return pltpu.make_async_remote_copy(src_ref=src, dst_ref=dst, send_sem=send_sem, recv_sem=recv_sem, device_id=device,
                                        device_id_type=MESH)


def _local_copies(src, dst, sems):
    piece = src.shape[0] // D2D_STREAMS
    assert src.shape[0] % D2D_STREAMS == 0
    copies = []
    for i in range(D2D_STREAMS):
        r = pl.ds(i * piece, piece)
        cp = pltpu.make_async_copy(src.at[r], dst.at[r], sems.at[i])
        cp.start()
        copies.append(cp)
    return copies


def _chip_all_gather(block, name):
    rows = block.shape[0]
    half = rows // 2
    piece = half // D2D_STREAMS
    assert rows % (2 * D2D_STREAMS * 16) == 0

    def body(src, out, ici_send, ici_recv, d2d_send, d2d_recv, local_sem):
        x, y, c = lax.axis_index("x"), lax.axis_index("y"), lax.axis_index("c")
        me = 2 * x + y
        sibling = (x, y, 1 - c)
        local = _local_copies(src, out.at[me], local_sem)
        chips = [(1 - x, y), (x, 1 - y), (1 - x, 1 - y)]
        mine = pl.ds(pl.multiple_of(c * half, 16), half)
        sends = []
        for j, (px, py) in enumerate(chips):
            cp = _remote(src.at[mine], out.at[me, mine], ici_send.at[j], ici_recv.at[j], (px, py, c))
            cp.start()
            sends.append(cp)
        for j, (px, py) in enumerate(chips):
            slot = 2 * px + py
            _remote(src.at[mine], out.at[slot, mine], ici_send.at[j], ici_recv.at[j], (px, py, c)).wait_recv()
            for s in range(D2D_STREAMS):
                r = pl.ds(pl.multiple_of(c * half + s * piece, 16), piece)
                k = j * D2D_STREAMS + s
                cp = _remote(out.at[slot, r], out.at[slot, r], d2d_send.at[k], d2d_recv.at[k], sibling)
                cp.start()
                sends.append(cp)
        for j, (px, py) in enumerate(chips):
            slot = 2 * px + py
            for s in range(D2D_STREAMS):
                r = pl.ds(pl.multiple_of((1 - c) * half + s * piece, 16), piece)
                k = j * D2D_STREAMS + s
                _remote(out.at[slot, r], out.at[slot, r], d2d_send.at[k], d2d_recv.at[k], sibling).wait_recv()
        for cp in sends:
            cp.wait_send()
        for cp in local:
            cp.wait()

    n_d2d = 3 * D2D_STREAMS
    return pl.pallas_call(
        body, name=name, in_specs=[_ANY], out_specs=_ANY,
        out_shape=jax.ShapeDtypeStruct((N_CHIPS,) + block.shape, block.dtype),
        scratch_shapes=[pltpu.SemaphoreType.DMA((3,)), pltpu.SemaphoreType.DMA((3,)), pltpu.SemaphoreType.DMA((n_d2d,)),
                        pltpu.SemaphoreType.DMA((n_d2d,)), pltpu.SemaphoreType.DMA((D2D_STREAMS,))],
    )(block)


def _chip_scatter(parts, name):
    def body(src, out, send_sems, recv_sems, local_sem):
        x, y, c = lax.axis_index("x"), lax.axis_index("y"), lax.axis_index("c")
        me = 2 * x + y
        local = _local_copies(src.at[me], out.at[me], local_sem)
        chips = [(1 - x, y), (x, 1 - y), (1 - x, 1 - y)]
        sends = []
        for j, (px, py) in enumerate(chips):
            cp = pltpu.make_async_remote_copy(src_ref=src.at[2 * px + py], dst_ref=out.at[me], send_sem=send_sems.at[j],
                                              recv_sem=recv_sems.at[j], device_id=(px, py, c), device_id_type=MESH)
            cp.start()
            sends.append(cp)
        for j, (px, py) in enumerate(chips):
            pltpu.make_async_remote_copy(src_ref=src.at[me], dst_ref=out.at[2 * px + py], send_sem=send_sems.at[j],
                                         recv_sem=recv_sems.at[j], device_id=(px, py, c), device_id_type=MESH).wait_recv()
        for cp in sends:
            cp.wait_send()
        for cp in local:
            cp.wait()

    return pl.pallas_call(
        body, name=name, in_specs=[_ANY], out_specs=_ANY, out_shape=jax.ShapeDtypeStruct(parts.shape, parts.dtype),
        scratch_shapes=[pltpu.SemaphoreType.DMA((3,)), pltpu.SemaphoreType.DMA((3,)), pltpu.SemaphoreType.DMA((D2D_STREAMS,))],
    )(parts)


def _core_send_other_half(parts, name):
    n_slots, rows, cols = parts.shape
    half = rows // 2
    piece = half // D2D_STREAMS
    assert rows % (2 * D2D_STREAMS * 16) == 0

    def body(src, out, send_sems, recv_sems):
        x, y, c = lax.axis_index("x"), lax.axis_index("y"), lax.axis_index("c")
        sibling = (x, y, 1 - c)
        sends = []
        for k in range(n_slots):
            for s in range(D2D_STREAMS):
                theirs = pl.ds(pl.multiple_of((1 - c) * half + s * piece, 16), piece)
                i = k * D2D_STREAMS + s
                cp = _remote(src.at[k, theirs], out.at[k, pl.ds(s * piece, piece)], send_sems.at[i], recv_sems.at[i], sibling)
                cp.start()
                sends.append(cp)
        for cp in sends:
            cp.wait_recv()
        for cp in sends:
            cp.wait_send()

    n = n_slots * D2D_STREAMS
    return pl.pallas_call(
        body, name=name, in_specs=[_ANY], out_specs=_ANY, out_shape=jax.ShapeDtypeStruct((n_slots, half, cols), parts.dtype),
        scratch_shapes=[pltpu.SemaphoreType.DMA((n,)), pltpu.SemaphoreType.DMA((n,))],
    )(parts)


def _add_my_half(parts, other, core, name, block_rows=256):
    n_slots, rows, cols = parts.shape
    half = rows // 2
    br = _pick(half, block_rows, 16)
    nb = half // br

    def body(c_ref, a_ref, b_ref, o_ref):
        o_ref[...] = (a_ref[...].astype(f32) + b_ref[...].astype(f32)).astype(o_ref.dtype)

    grid_spec = pltpu.PrefetchScalarGridSpec(
        num_scalar_prefetch=1, grid=(n_slots, nb),
        in_specs=[pl.BlockSpec((1, br, cols), lambda k, i, c: (k, c[0] * nb + i, 0)),
                  pl.BlockSpec((1, br, cols), lambda k, i, c: (k, i, 0))],
        out_specs=pl.BlockSpec((1, br, cols), lambda k, i, c: (k, i, 0)))
    return pl.pallas_call(
        body, name=name, grid_spec=grid_spec, out_shape=jax.ShapeDtypeStruct((n_slots, half, cols), bf16),
        compiler_params=_params(("parallel", "parallel")),
    )(core, parts, other)


def _core_join_halves(mine, name):
    half, cols = mine.shape
    piece = half // D2D_STREAMS
    assert half % (D2D_STREAMS * 16) == 0

    def body(src, out, send_sems, recv_sems, local_sem):
        x, y, c = lax.axis_index("x"), lax.axis_index("y"), lax.axis_index("c")
        sibling = (x, y, 1 - c)
        local = _local_copies(src, out.at[pl.ds(pl.multiple_of(c * half, 16), half)], local_sem)
        sends = []
        for s in range(D2D_STREAMS):
            dst = out.at[pl.ds(pl.multiple_of(c * half + s * piece, 16), piece)]
            cp = _remote(src.at[pl.ds(s * piece, piece)], dst, send_sems.at[s], recv_sems.at[s], sibling)
            cp.start()
            sends.append(cp)
        for s in range(D2D_STREAMS):
            dst = out.at[pl.ds(pl.multiple_of((1 - c) * half + s * piece, 16), piece)]
            _remote(src.at[pl.ds(s * piece, piece)], dst, send_sems.at[s], recv_sems.at[s], sibling).wait_recv()
        for cp in sends:
            cp.wait_send()
        for cp in local:
            cp.wait()

    return pl.pallas_call(
        body, name=name, in_specs=[_ANY], out_specs=_ANY, out_shape=jax.ShapeDtypeStruct((2 * half, cols), mine.dtype),
        scratch_shapes=[pltpu.SemaphoreType.DMA((D2D_STREAMS,)), pltpu.SemaphoreType.DMA((D2D_STREAMS,)),
                        pltpu.SemaphoreType.DMA((D2D_STREAMS,))],
    )(mine)


def _device_all_gather(block, name):
    def body(src, out, send_sems, recv_sems, local_sem):
        x, y, c = lax.axis_index("x"), lax.axis_index("y"), lax.axis_index("c")
        me = 4 * x + 2 * y + c
        local = _local_copies(src, out.at[me], local_sem)
        peers = []
        for mask in range(1, N_DEV):
            fx, fy, fc = (mask >> 2) & 1, (mask >> 1) & 1, mask & 1
            peers.append((x ^ fx, y ^ fy, c ^ fc))
        sends = []
        for j, peer in enumerate(peers):
            cp = pltpu.make_async_remote_copy(src_ref=src, dst_ref=out.at[me], send_sem=send_sems.at[j],
                                              recv_sem=recv_sems.at[j], device_id=peer, device_id_type=MESH)
            cp.start()
            sends.append(cp)
        for j, (px, py, pc) in enumerate(peers):
            pltpu.make_async_remote_copy(src_ref=src, dst_ref=out.at[4 * px + 2 * py + pc], send_sem=send_sems.at[j],
                                         recv_sem=recv_sems.at[j], device_id=(px, py, pc), device_id_type=MESH).wait_recv()
        for cp in sends:
            cp.wait_send()
        for cp in local:
            cp.wait()

    return pl.pallas_call(
        body, name=name, in_specs=[_ANY], out_specs=_ANY,
        out_shape=jax.ShapeDtypeStruct((N_DEV,) + block.shape, block.dtype),
        scratch_shapes=[pltpu.SemaphoreType.DMA((N_DEV - 1,)), pltpu.SemaphoreType.DMA((N_DEV - 1,)),
                        pltpu.SemaphoreType.DMA((D2D_STREAMS,))],
    )(block)


def _sum_slots(stack, name, block_rows=256):
    s_n, r_n, c_n = stack.shape
    br = _pick(r_n, block_rows, 8)

    def body(in_ref, o_ref):
        acc = in_ref[0].astype(f32)
        for s in range(1, s_n):
            acc = acc + in_ref[s].astype(f32)
        o_ref[...] = acc

    return pl.pallas_call(
        body, name=name, grid=(r_n // br,), in_specs=[pl.BlockSpec((s_n, br, c_n), lambda i: (0, i, 0))],
        out_specs=pl.BlockSpec((br, c_n), lambda i: (i, 0)), out_shape=jax.ShapeDtypeStruct((r_n, c_n), f32),
        compiler_params=_params(("parallel",)),
    )(stack)


def _pad_rows(a, mult):
    r = (-a.shape[0]) % mult
    return a if r == 0 else jnp.concatenate([a, jnp.zeros((r,) + a.shape[1:], a.dtype)], axis=0)


def _shard_axis(name):
    return 2 if name in COL_SHARDED else 1


def _pack_weights(w, conv_w):
    parts = [w[n].astype(bf16).reshape(-1, PACK_COLS) for n in BIG]
    parts.append(lax.bitcast_convert_type(conv_w, bf16).reshape(-1, PACK_COLS))
    return _pad_rows(jnp.concatenate(parts, axis=0), PACK_ROW_MULT)


def _unpack_weights(full, w, conv_w):
    out, r0 = {}, 0
    for n in BIG:
        rows = w[n].size // PACK_COLS
        pieces = full[:, r0:r0 + rows].reshape((N_CHIPS,) + w[n].shape)
        out[n] = jnp.concatenate([pieces[k] for k in range(N_CHIPS)], axis=_shard_axis(n))
        r0 += rows
    rows = conv_w.size * 2 // PACK_COLS
    pieces = lax.bitcast_convert_type(full[:, r0:r0 + rows].reshape((N_CHIPS,) + conv_w.shape + (2,)), f32)
    return out, jnp.concatenate([pieces[k] for k in range(N_CHIPS)], axis=2)


def _pack_big_grads(g):
    slots = []
    for k in range(N_CHIPS):
        parts = []
        for n in BIG:
            ax = _shard_axis(n)
            width = g[n].shape[ax] // N_CHIPS
            parts.append(lax.slice_in_dim(g[n], k * width, (k + 1) * width, axis=ax).astype(bf16).reshape(-1, PACK_COLS))
        slots.append(_pad_rows(jnp.concatenate(parts, axis=0), PACK_ROW_MULT))
    return jnp.stack(slots)


def _unpack_big_grads(summed, w):
    out, r0 = {}, 0
    for n in BIG:
        rows = w[n].size // PACK_COLS
        out[n] = summed[r0:r0 + rows].reshape(w[n].shape)
        r0 += rows
    return out


def _pack_small(vals, names):
    parts = []
    for n in names:
        flat = vals[n].reshape(-1)
        pad = (-flat.size) % LANES
        if pad:
            flat = jnp.concatenate([flat, jnp.zeros((pad,), flat.dtype)])
        parts.append(flat.reshape(-1, LANES))
    return _pad_rows(jnp.concatenate(parts, axis=0), PACK_ROW_MULT)


def _unpack_small(packed, like, names):
    out, r0 = {}, 0
    for n in names:
        size = like[n].size
        rows = -(-size // LANES)
        out[n] = packed[r0:r0 + rows].reshape(-1)[:size].reshape(like[n].shape)
        r0 += rows
    return out


def _dims(w, x):
    d = {}
    d["D"] = x.shape[-1]
    d["T"] = x.shape[-2]
    d["DI"] = w["ssd_norm_g"].shape[-1]
    d["NH"] = w["ssd_dt_bias"].shape[-1]
    d["CD"] = w["ssd_conv_b"].shape[-1]
    d["G"] = SSD_N_GROUPS
    d["HPG"] = d["NH"] // d["G"]
    d["P"] = d["DI"] // d["NH"]
    d["N"] = (d["CD"] - d["DI"]) // (2 * d["G"])
    d["S5G"], d["S5N"] = w["s5_lambda_re"].shape[-2:]
    d["S5C"] = w["s5_b_re"].shape[-1]
    d["S5W"] = d["S5G"] * d["S5C"]
    d["NSB"] = d["S5W"] // S5_SUPERBLOCK
    d["GSB"] = d["S5G"] // d["NSB"]
    return d


def _head_pad(v, d):
    lead = v.shape[:-1]
    v = v.reshape(lead + (d["G"], d["HPG"]))
    v = jnp.concatenate([v, jnp.zeros(lead + (d["G"], LANES - d["HPG"]), v.dtype)], axis=-1)
    return v.reshape(lead + (d["G"] * LANES,))


def _head_unpad(v, d):
    lead = v.shape[:-1]
    return v.reshape(lead + (d["G"], LANES))[..., :d["HPG"]].reshape(lead + (d["NH"],))


def _w_in_perm(w_in, d):
    o = d["DI"] + d["CD"]
    return jnp.concatenate([w_in[:, :o], w_in[:, o + d["NH"]:], _head_pad(w_in[:, o:o + d["NH"]], d)], axis=1)


def _w_in_unperm(g, d):
    o = d["DI"] + d["CD"]
    rest = d["S5W"] + 2 * d["D"]
    return jnp.concatenate([g[:, :o], _head_unpad(g[:, o + rest:], d), g[:, o:o + rest]], axis=1)


def _s5_block_diag(v, d):
    nsb, gsb = d["NSB"], d["GSB"]
    a, b = v.shape[1:]
    v = v.reshape(nsb, gsb, a, b)
    eye = jnp.eye(gsb, dtype=v.dtype)
    return (v[:, :, :, None, :] * eye[None, :, None, :, None]).reshape(nsb * gsb * a, gsb * b)


def _s5_diag_blocks(m, d, a, b):
    nsb, gsb = d["NSB"], d["GSB"]
    m = m.reshape(nsb, gsb, a, gsb, b)
    eye = jnp.eye(gsb, dtype=m.dtype)
    return jnp.sum(m * eye[None, :, None, :, None], axis=3).reshape(nsb * gsb, a, b)


def _s5_lam_rows(v, d):
    v = v.reshape(d["NSB"], 1, d["GSB"] * d["S5N"])
    return jnp.broadcast_to(v, (d["NSB"], 8, v.shape[-1])).reshape(d["NSB"] * 8, -1)


def _ffn_fwd(h, pre_g, post_g, wgu, wd, tag):
    D = h.shape[1]
    H2 = wgu.shape[1]
    xn = _row_kernel(f"{tag}_norm", _fwd_of(_f_norm), [(h, D, 0)], [pre_g], [(D, bf16)])[0]
    ab = _mm(xn, wgu, "nn", f32, f"{tag}_mm_up")
    hid = _row_kernel(f"{tag}_swiglu", _swiglu_fwd, [(ab, H2, 0)], [], [(H2 // 2, bf16)])[0]
    f = _mm(hid, wd, "nn", f32, f"{tag}_mm_down")
    out = _row_kernel(f"{tag}_resnorm", _fwd_of(_f_resnorm(0.5)), [(h, D, 0), (f, D, 0)], [post_g], [(D, f32)])[0]
    return out, dict(h=h, xn=xn, ab=ab, hid=hid, f=f)


def _ffn_bwd(dh_out, s, pre_g, post_g, wgu, wd, tag):
    D = dh_out.shape[1]
    H2 = wgu.shape[1]
    df, dpost = _row_kernel(f"{tag}_resnorm_bwd", _vjp_of(_f_post(0.5), 1, 1, [0]), [(s["f"], D, 0), (dh_out, D, 0)],
                            [post_g], [(D, bf16)], [post_g.shape])
    dwd = _mm(s["hid"], df, "tn", f32, f"{tag}_mm_dwd")
    dhid = _mm(df, wd, "nt", bf16, f"{tag}_mm_dhid")
    dab = _row_kernel(f"{tag}_swiglu_bwd", _swiglu_bwd, [(s["ab"], H2, 0), (dhid, H2 // 2, 0)], [], [(H2, bf16)])[0]
    dwgu = _mm(s["xn"], dab, "tn", f32, f"{tag}_mm_dwgu")
    dxn = _mm(dab, wgu, "nt", f32, f"{tag}_mm_dxn")
    dh, dpre = _row_kernel(f"{tag}_norm_bwd", _vjp_of(_f_norm, 1, 1, [0], 1), [(s["h"], D, 0), (dxn, D, 0), (dh_out, D, 0)],
                           [pre_g], [(D, f32)], [pre_g.shape])
    return dh, dict(pre_g=dpre, post_g=dpost, wgu=dwgu, wd=dwd)


def _mixer_fwd(h, p, d):
    D, DI, CD, G, N = d["D"], d["DI"], d["CD"], d["G"], d["N"]
    gl = G * LANES
    c_u5, c_ga, c_gb, c_dt = DI + CD, DI + CD + d["S5W"], DI + CD + d["S5W"] + D, DI + CD + d["S5W"] + 2 * D
    u = _row_kernel("mix_norm", _fwd_of(_f_norm), [(h, D, 0)], [p["mix_pre_g"]], [(D, bf16)])[0]
    proj = _mm(u, p["w_in"], "nn", f32, "mix_mm_in", bn_t=512)
    act = _conv_fwd(proj, DI, p["conv_w"], p["conv_b"], "ssd_conv")
    dt, adt = _row_kernel("ssd_dt", _fwd_of(_f_dt), [(proj, gl, c_dt // gl)], [p["dt_bias"], p["a_log"]], [(gl, f32)] * 2)
    y_ssd, states = _ssd_fwd(act, dt, adt, p["d_skip"], d["HPG"], d["P"], N, "ssd_scan")
    nrm = _row_kernel("ssd_post", _fwd_of(_f_ssdpost(G)), [(y_ssd, DI, 0), (proj, DI, 0)], [p["norm_g"]], [(DI, bf16)])[0]
    y_a = _mm(nrm, p["w_a"], "nn", f32, "mix_mm_a")
    u5 =(proj, d["S5W"], c_u5 // d["S5W"])
    bu = _s5_in(proj, c_u5, p["bsb"], d)
    s5st = _s5_scan_fwd(bu, p["lam_re_rows"], p["lam_im_rows"], d["NSB"], "s5_scan")
    y5 = _bdmm(s5st, p["csb"], "nn", d["NSB"], f32, "s5_mm_c")
    gel = _row_kernel("s5_post", _fwd_of(_f_s5post), [(y5, d["S5W"], 0), u5], [p["s5_d"]], [(d["S5W"], bf16)])[0]
    vg = _mm(gel, p["w_glu"], "nn", f32, "mix_mm_glu")
    glu = _row_kernel("s5_glu", _glu_fwd, [(vg, vg.shape[1], 0)], [], [(vg.shape[1] // 2, bf16)])[0]
    y_b = _mm(glu, p["w_b"], "nn", f32, "mix_mm_b")
    merged = _row_kernel("mix_merge", _fwd_of(_f_merge), [(proj, D, c_ga // D), (y_a, D, 0), (proj, D, c_gb // D), (y_b, D, 0)],
                         [], [(D, bf16)])[0]
    m = _mm(merged, p["w_out"], "nn", f32, "mix_mm_out")
    out = _row_kernel("mix_resnorm", _fwd_of(_f_resnorm(1.0)), [(h, D, 0), (m, D, 0)], [p["mix_post_g"]], [(D, f32)])[0]
    return out, dict(h=h, u=u, proj=proj, act=act, dt=dt, adt=adt, states=states, y_ssd=y_ssd, nrm=nrm, y_a=y_a, s5st=s5st,
                     y5=y5, gel=gel, vg=vg, glu=glu, y_b=y_b, merged=merged, m=m)


def _s5_in(proj, c_u5, bsb, d):
    T = proj.shape[0]
    nsb = d["NSB"]
    ka, nw = S5_SUPERBLOCK, bsb.shape[1]
    off = c_u5 // ka
    assert c_u5 % ka == 0
    bt = _pick(T, 512)

    def body(a_ref, w_ref, o_ref):
        o_ref[...] = _dot(a_ref[...].astype(bf16), w_ref[...].astype(bf16))

    return pl.pallas_call(
        body, name="s5_mm_bu", grid=(nsb, T // bt),
        in_specs=[pl.BlockSpec((bt, ka), lambda j, i: (i, off + j)), pl.BlockSpec((ka, nw), lambda j, i: (j, 0))],
        out_specs=pl.BlockSpec((bt, nw), lambda j, i: (i, j)), out_shape=jax.ShapeDtypeStruct((T, nsb * nw), f32),
        compiler_params=_params(("parallel", "parallel")),
    )(proj, bsb)


def _s5_dbsb(proj, c_u5, a, d):
    T = proj.shape[0]
    nsb = d["NSB"]
    ka, nw = S5_SUPERBLOCK, a.shape[1] // nsb
    off = c_u5 // ka
    bt = _pick(T, 512)

    def body(u_ref, a_ref, o_ref):
        pr = _dot(u_ref[...].astype(bf16), a_ref[...].astype(bf16), "tn")
        k = pl.program_id(1)

        @pl.when(k == 0)
        def _():
            o_ref[...] = pr

        @pl.when(k > 0)
        def _():
            o_ref[...] += pr

    return pl.pallas_call(
        body, name="s5_mm_dbsb", grid=(nsb, T // bt),
        in_specs=[pl.BlockSpec((bt, ka), lambda j, k: (k, off + j)), pl.BlockSpec((bt, nw), lambda j, k: (k, j))],
        out_specs=pl.BlockSpec((ka, nw), lambda j, k: (j, 0)), out_shape=jax.ShapeDtypeStruct((nsb * ka, nw), f32),
        compiler_params=_params(("parallel", "arbitrary")),
    )(proj, a)


def _mixer_bwd(dh_out, s, p, d):
    D, DI, CD, G, N, S5W = d["D"], d["DI"], d["CD"], d["G"], d["N"], d["S5W"]
    gl = G * LANES
    gn = G * N
    c_u5, c_ga, c_gb, c_dt = DI + CD, DI + CD + S5W, DI + CD + S5W + D, DI + CD + S5W + 2 * D
    proj = s["proj"]
    g = {}
    dm, g["mix_post_g"] = _row_kernel("mix_resnorm_bwd", _vjp_of(_f_post(1.0), 1, 1, [0]), [(s["m"], D, 0), (dh_out, D, 0)],
                                      [p["mix_post_g"]], [(D, bf16)], [p["mix_post_g"].shape])
    g["w_out"] = _mm(s["merged"], dm, "tn", f32, "mix_mm_dwout")
    dmerged = _mm(dm, p["w_out"], "nt", f32, "mix_mm_dmerged")
    dga, dya, dgb, dyb = _row_kernel(
        "mix_merge_bwd", _vjp_of(_f_merge, 4, 1, [0, 1, 2, 3]),
        [(proj, D, c_ga // D), (s["y_a"], D, 0), (proj, D, c_gb // D), (s["y_b"], D, 0), (dmerged, D, 0)], [],
        [(D, bf16), (D, bf16), (D, bf16), (D, bf16)])
    g["w_a"] = _mm(s["nrm"], dya, "tn", f32, "mix_mm_dwa")
    dnrm = _mm(dya, p["w_a"], "nt", f32, "mix_mm_dnrm")
    dy_ssd, dz, g["norm_g"] = _row_kernel(
        "ssd_post_bwd", _vjp_of(_f_ssdpost(G), 2, 1, [0, 1]), [(s["y_ssd"], DI, 0), (proj, DI, 0), (dnrm, DI, 0)],
        [p["norm_g"]], [(DI, f32), (DI, bf16)], [p["norm_g"].shape])
    dxs, d_b, d_c, ddt, dadt, dd = _ssd_bwd(s["act"], s["dt"], s["adt"], p["d_skip"], s["states"], dy_ssd,
                                            d["HPG"], d["P"], N, "ssd_scan_bwd")
    g["d_skip"] = dd.reshape(G, 8, LANES)[:, 0, :].reshape(1, gl)
    ddt_raw, g["dt_bias"], g["a_log"] = _row_kernel(
        "ssd_dt_bwd", _vjp_of(_f_dt, 1, 2, [0]), [(proj, gl, c_dt // gl), (ddt, gl, 0), (dadt, gl, 0)],
        [p["dt_bias"], p["a_log"]], [(gl, bf16)], [p["dt_bias"].shape, p["a_log"].shape])
    cw, cb = p["conv_w"], p["conv_b"]
    dxc_x, dw_x, db_x = _conv_bwd(proj, DI, cw[:, :DI], cb[:, :DI], dxs, "ssd_conv_bwd_x")
    dxc_b, dw_b, db_b = _conv_bwd(proj, 2 * DI, cw[:, DI:DI + gn], cb[:, DI:DI + gn], d_b, "ssd_conv_bwd_b")
    dxc_c, dw_c, db_c = _conv_bwd(proj, 2 * DI + gn, cw[:, DI + gn:], cb[:, DI + gn:], d_c, "ssd_conv_bwd_c")
    g["conv_w"] = jnp.concatenate([dw_x, dw_b, dw_c], axis=1)
    g["conv_b"] = jnp.concatenate([db_x, db_b, db_c], axis=1)
    g["w_b"] = _mm(s["glu"], dyb, "tn", f32, "mix_mm_dwb")
    dglu = _mm(dyb, p["w_b"], "nt", f32, "mix_mm_dglu")
    dvg = _row_kernel("s5_glu_bwd", _glu_bwd, [(s["vg"], s["vg"].shape[1], 0), (dglu, S5W, 0)], [], [(s["vg"].shape[1], bf16)])[0]
    g["w_glu"] = _mm(s["gel"], dvg, "tn", f32, "mix_mm_dwglu")
    dgel = _mm(dvg, p["w_glu"], "nt", f32, "mix_mm_dgel")
    dy5, du5a, g["s5_d"] = _row_kernel(
        "s5_post_bwd", _vjp_of(_f_s5post, 2, 1, [0, 1]), [(s["y5"], S5W, 0), (proj, S5W, c_u5 // S5W), (dgel, S5W, 0)],
        [p["s5_d"]], [(S5W, bf16), (S5W, f32)], [p["s5_d"].shape])
    g["csb"] = _bdmm(s["s5st"], dy5, "tn", d["NSB"], f32, "s5_mm_dcsb")
    gst = _bdmm(dy5, p["csb"], "nt", d["NSB"], f32, "s5_mm_gst")
    a, g["lam_re_rows"], g["lam_im_rows"] = _s5_scan_bwd(gst, s["s5st"], p["lam_re_rows"], p["lam_im_rows"], d["NSB"], "s5_scan_bwd")
    g["bsb"] = _s5_dbsb(proj, c_u5, a, d)
    du5b = _bdmm(a, p["bsb"], "nt", d["NSB"], f32, "s5_mm_du5")
    du5 = _row_kernel("s5_du5", _add_fn, [(du5a, S5W, 0), (du5b, S5W, 0)], [], [(S5W, bf16)])[0]
    dproj = jnp.concatenate([dz, dxc_x, dxc_b, dxc_c, du5, dga, dgb, ddt_raw], axis=1)
    g["w_in"] = _mm(s["u"], dproj, "tn", f32, "mix_mm_dwin", bn_t=512)
    du = _mm(dproj, p["w_in"], "nt", f32, "mix_mm_du", bk_t=2176)
    dh, g["mix_pre_g"] = _row_kernel("mix_norm_bwd", _vjp_of(_f_norm, 1, 1, [0], 1), [(s["h"], D, 0), (du, D, 0), (dh_out, D, 0)],
                                     [p["mix_pre_g"]], [(D, f32)], [p["mix_pre_g"].shape])
    return dh, g


def _layer_params(l, w, wf, conv_w_full, d):
    r2 = lambda v: v[l].reshape(1, -1)
    p = {}
    for n in ["ffn1_pre_g", "ffn1_post_g", "mix_pre_g", "mix_post_g", "ffn2_pre_g", "ffn2_post_g", "s5_d"]:
        p[n] = r2(w[n])
    p["wgu1"] = jnp.concatenate([wf["ffn1_w_gate"][l], wf["ffn1_w_up"][l]], axis=1)
    p["wd1"] = wf["ffn1_w_down"][l]
    p["wgu2"] = jnp.concatenate([wf["ffn2_w_gate"][l], wf["ffn2_w_up"][l]], axis=1)
    p["wd2"] = wf["ffn2_w_down"][l]
    p["w_in"] = _w_in_perm(wf["w_in"][l], d)
    p["w_a"], p["w_glu"], p["w_b"], p["w_out"] = wf["w_branch_a"][l], wf["s5_w_glu"][l], wf["w_branch_b"][l], wf["w_out"][l]
    p["conv_w"] = conv_w_full[l]
    p["conv_b"] = r2(w["ssd_conv_b"])
    p["dt_bias"] = _head_pad(r2(w["ssd_dt_bias"]), d)
    p["a_log"] = _head_pad(r2(w["ssd_a_log"]), d)
    p["d_skip"] = _head_pad(r2(w["ssd_d"]), d)
    p["norm_g"] = r2(w["ssd_norm_g"])
    g5, n5, c5 = d["S5G"], d["S5N"], d["S5C"]
    expand = jnp.repeat(jnp.eye(n5, dtype=f32), c5, axis=1)
    prep_in = [w["s5_lambda_re"][l], w["s5_lambda_im"][l], w["s5_log_step"][l].reshape(g5, 1),
               w["s5_b_re"][l].reshape(g5, n5 * c5), w["s5_b_im"][l].reshape(g5, n5 * c5), expand]
    lbr, lbi, bbr, bbi = _s5_prep(prep_in, "s5_prep")
    p["s5_prep_in"] = prep_in
    p["lam_re_rows"], p["lam_im_rows"] = _s5_lam_rows(lbr, d), _s5_lam_rows(lbi, d)
    to_cn = lambda v: v.reshape(g5, n5, c5).transpose(0, 2, 1)
    p["bsb"] = jnp.concatenate([_s5_block_diag(to_cn(bbr), d), _s5_block_diag(to_cn(bbi), d)], axis=1).astype(bf16)
    c_re, c_im = w["s5_c_re"][l].transpose(0, 2, 1), w["s5_c_im"][l].transpose(0, 2, 1)
    nsb = d["NSB"]
    csb = jnp.stack([_s5_block_diag(c_re, d).reshape(nsb, -1, S5_SUPERBLOCK),
                     _s5_block_diag(-c_im, d).reshape(nsb, -1, S5_SUPERBLOCK)], axis=1)
    p["csb"] = csb.reshape(-1, S5_SUPERBLOCK).astype(bf16)
    return p


def _s5_param_grads(g, p, d, l):
    g5, n5, c5, nsb, gsb = d["S5G"], d["S5N"], d["S5C"], d["NSB"], d["GSB"]
    wst = gsb * n5
    dbsb = g["bsb"]
    from_cn = lambda v: v.transpose(0, 2, 1).reshape(g5, n5 * c5)
    dbbr = from_cn(_s5_diag_blocks(dbsb[:, :wst], d, c5, n5))
    dbbi = from_cn(_s5_diag_blocks(dbsb[:, wst:], d, c5, n5))
    rows = lambda v: v.reshape(nsb, 8, wst)[:, 0, :].reshape(g5, n5)
    cots = [rows(g["lam_re_rows"]), rows(g["lam_im_rows"]), dbbr, dbbi]
    dlr, dli, dls, dbr, dbi = _s5_prep_bwd(p["s5_prep_in"], cots, "s5_prep_bwd")
    dcsb = g["csb"].reshape(nsb, 2, wst, S5_SUPERBLOCK)
    dcr = _s5_diag_blocks(dcsb[:, 0].reshape(-1, S5_SUPERBLOCK), d, n5, c5).transpose(0, 2, 1)
    dci = -_s5_diag_blocks(dcsb[:, 1].reshape(-1, S5_SUPERBLOCK), d, n5, c5).transpose(0, 2, 1)
    return dict(s5_lambda_re=dlr, s5_lambda_im=dli, s5_log_step=dls.reshape(g5), s5_b_re=dbr.reshape(g5, n5, c5),
                s5_b_im=dbi.reshape(g5, n5, c5), s5_c_re=dcr, s5_c_im=dci)


def kernel(x, ffn1_pre_g, ffn1_post_g, ffn1_w_gate, ffn1_w_up, ffn1_w_down, mix_pre_g, mix_post_g, w_in, ssd_conv_w, ssd_conv_b, ssd_dt_bias, ssd_a_log, ssd_d, ssd_norm_g, w_branch_a, s5_lambda_re, s5_lambda_im, s5_b_re, s5_b_im, s5_c_re, s5_c_im, s5_log_step, s5_d, s5_w_glu, w_branch_b, w_out, ffn2_pre_g, ffn2_post_g, ffn2_w_gate, ffn2_w_up, ffn2_w_down, loss_target, m_ffn1_pre_g, m_ffn1_post_g, m_ffn1_w_gate, m_ffn1_w_up, m_ffn1_w_down, m_mix_pre_g, m_mix_post_g, m_w_in, m_ssd_conv_w, m_ssd_conv_b, m_ssd_dt_bias, m_ssd_a_log, m_ssd_d, m_ssd_norm_g, m_w_branch_a, m_s5_lambda_re, m_s5_lambda_im, m_s5_b_re, m_s5_b_im, m_s5_c_re, m_s5_c_im, m_s5_log_step, m_s5_d, m_s5_w_glu, m_w_branch_b, m_w_out, m_ffn2_pre_g, m_ffn2_post_g, m_ffn2_w_gate, m_ffn2_w_up, m_ffn2_w_down, v_ffn1_pre_g, v_ffn1_post_g, v_ffn1_w_gate, v_ffn1_w_up, v_ffn1_w_down, v_mix_pre_g, v_mix_post_g, v_w_in, v_ssd_conv_w, v_ssd_conv_b, v_ssd_dt_bias, v_ssd_a_log, v_ssd_d, v_ssd_norm_g, v_w_branch_a, v_s5_lambda_re, v_s5_lambda_im, v_s5_b_re, v_s5_b_im, v_s5_c_re, v_s5_c_im, v_s5_log_step, v_s5_d, v_s5_w_glu, v_w_branch_b, v_w_out, v_ffn2_pre_g, v_ffn2_post_g, v_ffn2_w_gate, v_ffn2_w_up, v_ffn2_w_down):
    given = dict(locals())
    w = {n: given[n] for n in WEIGHTS}
    mom = {n: given["m_" + n] for n in WEIGHTS}
    var = {n: given["v_" + n] for n in WEIGHTS}
    d = _dims(w, x)
    n_layers = w["ffn1_pre_g"].shape[0]
    T, D = d["T"], d["D"]

    gathered = _chip_all_gather(_pack_weights(w, w["ssd_conv_w"]), "gather_weights")
    wf, conv_w_full = _unpack_weights(gathered, w, w["ssd_conv_w"])
    layers = [_layer_params(l, w, wf, conv_w_full, d) for l in range(n_layers)]

    h = x.reshape(T, D)
    saved = []
    for p in layers:
        h, s1 = _ffn_fwd(h, p["ffn1_pre_g"], p["ffn1_post_g"], p["wgu1"], p["wd1"], "ffn1")
        h, sm = _mixer_fwd(h, p, d)
        h, s2 = _ffn_fwd(h, p["ffn2_pre_g"], p["ffn2_post_g"], p["wgu2"], p["wd2"], "ffn2")
        saved.append((s1, sm, s2))
    dh, loss_part = _row_kernel("loss", _loss_fn, [(h, D, 0), (loss_target.reshape(T, D), D, 0)], [], [(D, f32)], [(8, LANES)])
    loss = lax.psum(loss_part[0, 0], ("x", "y", "c"))

    lg = [None] * n_layers
    for l in reversed(range(n_layers)):
        p = layers[l]
        s1, sm, s2 = saved[l]
        dh, g2 = _ffn_bwd(dh, s2, p["ffn2_pre_g"], p["ffn2_post_g"], p["wgu2"], p["wd2"], "ffn2")
        dh, gm = _mixer_bwd(dh, sm, p, d)
        dh, g1 = _ffn_bwd(dh, s1, p["ffn1_pre_g"], p["ffn1_post_g"], p["wgu1"], p["wd1"], "ffn1")
        H = p["wd1"].shape[0]
        gl = dict(ffn1_pre_g=g1["pre_g"], ffn1_post_g=g1["post_g"], ffn1_w_gate=g1["wgu"][:, :H], ffn1_w_up=g1["wgu"][:, H:],
                  ffn1_w_down=g1["wd"], ffn2_pre_g=g2["pre_g"], ffn2_post_g=g2["post_g"], ffn2_w_gate=g2["wgu"][:, :H],
                  ffn2_w_up=g2["wgu"][:, H:], ffn2_w_down=g2["wd"], mix_pre_g=gm["mix_pre_g"], mix_post_g=gm["mix_post_g"],
                  w_in=_w_in_unperm(gm["w_in"], d), ssd_conv_w=gm["conv_w"], ssd_conv_b=gm["conv_b"],
                  ssd_dt_bias=_head_unpad(gm["dt_bias"], d), ssd_a_log=_head_unpad(gm["a_log"], d),
                  ssd_d=_head_unpad(gm["d_skip"], d), ssd_norm_g=gm["norm_g"], w_branch_a=gm["w_a"], s5_d=gm["s5_d"],
                  s5_w_glu=gm["w_glu"], w_branch_b=gm["w_b"], w_out=gm["w_out"])
        gl.update(_s5_param_grads(gm, p, d, l))
        lg[l] = gl
    grad_x = dh.reshape(x.shape)
    full_shape = {n: w[n].shape[1:] for n in WEIGHTS}
    for n in BIG + ["ssd_conv_w"]:
        ax = _shard_axis(n) - 1 if n in BIG else 1
        full_shape[n] = tuple(s * N_CHIPS if i == ax else s for i, s in enumerate(w[n].shape[1:]))
    part = {n: jnp.stack([lg[l][n].reshape(full_shape[n]) for l in range(n_layers)]) for n in WEIGHTS}

    packed = _pack_big_grads(part)
    my_core = lax.axis_index("c").astype(jnp.int32).reshape(1)
    chip_part = _add_my_half(packed, _core_send_other_half(packed, "exchange_core_halves"), my_core, "sum_core_halves")
    my_sum = _sum_slots(_chip_scatter(chip_part, "scatter_grads"), "sum_chip_parts")
    grads = _unpack_big_grads(_core_join_halves(my_sum, "join_core_halves"), w)
    small_names = SMALL + ["ssd_conv_w"]
    everyone = _device_all_gather(_pack_small(part, small_names), "gather_small_grads")
    small = _unpack_small(_sum_slots(everyone, "sum_small_grads"), part, small_names)
    k_me = 2 * lax.axis_index("x") + lax.axis_index("y")
    cw = w["ssd_conv_w"].shape[-1]
    small["ssd_conv_w"] = lax.dynamic_slice_in_dim(small["ssd_conv_w"], k_me * cw, cw, axis=2)
    grads.update(small)

    delta, new_m, new_v = {}, {}, {}
    for n in BIG + ["ssd_conv_w"]:
        delta[n], new_m[n], new_v[n] = _adamw(w[n], grads[n], mom[n], var[n], "adamw_" + n)
    packs = [_pack_small(t, SMALL) for t in (w, grads, mom, var)]
    sd, sm_, sv = _adamw(*packs, "adamw_small")
    delta.update(_unpack_small(sd, w, SMALL))
    new_m.update(_unpack_small(sm_, w, SMALL))
    new_v.update(_unpack_small(sv, w, SMALL))
    return (loss, grad_x, *[grads[n] for n in WEIGHTS], *[delta[n] for n in WEIGHTS],
            *[new_m[n] for n in WEIGHTS], *[new_v[n] for n in WEIGHTS])
```

```python
import functools

import numpy as np
import jax
import jax.numpy as jnp
from jax import lax
from jax.experimental import pallas as pl
from jax.experimental.pallas import tpu as pltpu

f32, bf16 = jnp.float32, jnp.bfloat16

SSD_N_GROUPS = 4
SSD_CHUNK = 128
RMS_EPS = 1e-6
S5_MAX_REAL = -1e-4
S5_SUPERBLOCK = 256
ADAM_LR, ADAM_B1, ADAM_B2, ADAM_EPS, ADAM_WD, ADAM_STEP = 0.001, 0.9, 0.999, 1e-08, 0.01, 10

LANES = 128
PACK_COLS = 1024
D2D_STREAMS = 16
PACK_ROW_MULT = 2 * D2D_STREAMS * 16
VMEM_LIMIT_BYTES = 48 * 1024 * 1024
N_CHIPS, N_CORES, N_DEV = 4, 2, 8
MESH = pl.DeviceIdType.MESH

BIG = ["ffn1_w_gate", "ffn1_w_up", "ffn1_w_down", "w_in", "w_branch_a", "s5_w_glu", "w_branch_b", "w_out",
       "ffn2_w_gate", "ffn2_w_up", "ffn2_w_down"]
COL_SHARDED = ["ffn1_w_gate", "ffn1_w_up", "w_in", "s5_w_glu", "ffn2_w_gate", "ffn2_w_up"]
SMALL = ["ffn1_pre_g", "ffn1_post_g", "mix_pre_g", "mix_post_g", "ssd_conv_b", "ssd_dt_bias", "ssd_a_log", "ssd_d",
         "ssd_norm_g", "s5_lambda_re", "s5_lambda_im", "s5_b_re", "s5_b_im", "s5_c_re", "s5_c_im", "s5_log_step",
         "s5_d", "ffn2_pre_g", "ffn2_post_g"]
WEIGHTS = ["ffn1_pre_g", "ffn1_post_g", "ffn1_w_gate", "ffn1_w_up", "ffn1_w_down", "mix_pre_g", "mix_post_g", "w_in",
           "ssd_conv_w", "ssd_conv_b", "ssd_dt_bias", "ssd_a_log", "ssd_d", "ssd_norm_g", "w_branch_a", "s5_lambda_re",
           "s5_lambda_im", "s5_b_re", "s5_b_im", "s5_c_re", "s5_c_im", "s5_log_step", "s5_d", "s5_w_glu", "w_branch_b",
           "w_out", "ffn2_pre_g", "ffn2_post_g", "ffn2_w_gate", "ffn2_w_up", "ffn2_w_down"]


def _params(sem=None):
    return pltpu.CompilerParams(dimension_semantics=sem, vmem_limit_bytes=VMEM_LIMIT_BYTES)


def _pick(n, target, mult=LANES):
    best = None
    for d in range(mult, min(n, target) + 1, mult):
        if n % d == 0:
            best = d
    return best if best is not None else n


_DIMS = {"nn": (((1,), (0,)), ((), ())), "nt": (((1,), (1,)), ((), ())), "tn": (((0,), (0,)), ((), ()))}


def _mm(a, b, mode, out_dtype, name, bm_t=1024, bn_t=1024, bk_t=2816):
    if mode == "nn":
        (M, K), (K2, N) = a.shape, b.shape
    elif mode == "nt":
        (M, K), (N, K2) = a.shape, b.shape
    else:
        (K, M), (K2, N) = a.shape, b.shape
    assert K == K2, (name, a.shape, b.shape)
    bm, bn, bk = _pick(M, bm_t), _pick(N, bn_t), _pick(K, bk_t)
    nk = K // bk
    dn = _DIMS[mode]

    def body(a_ref, b_ref, o_ref, *scratch):
        p = lax.dot_general(a_ref[...].astype(bf16), b_ref[...].astype(bf16), dn, preferred_element_type=f32)
        if nk == 1:
            o_ref[...] = p.astype(o_ref.dtype)
        else:
            acc = scratch[0]
            k = pl.program_id(2)

            @pl.when(k == 0)
            def _():
                acc[...] = p

            @pl.when(k > 0)
            def _():
                acc[...] += p

            @pl.when(k == nk - 1)
            def _():
                o_ref[...] = acc[...].astype(o_ref.dtype)

    if mode == "tn":
        a_spec = pl.BlockSpec((bk, bm), lambda i, j, k: (k, i))
    else:
        a_spec = pl.BlockSpec((bm, bk), lambda i, j, k: (i, k))
    if mode == "nt":
        b_spec = pl.BlockSpec((bn, bk), lambda i, j, k: (j, k))
    else:
        b_spec = pl.BlockSpec((bk, bn), lambda i, j, k: (k, j))
    return pl.pallas_call(
        body, name=name, grid=(M // bm, N // bn, nk), in_specs=[a_spec, b_spec],
        out_specs=pl.BlockSpec((bm, bn), lambda i, j, k: (i, j)), out_shape=jax.ShapeDtypeStruct((M, N), out_dtype),
        scratch_shapes=[pltpu.VMEM((bm, bn), f32)] if nk > 1 else [],
        compiler_params=_params(("parallel", "parallel", "arbitrary")),
    )(a, b)


def _bdmm(a, w, mode, nb, out_dtype, name, bt_t=512):
    if mode == "tn":
        T = a.shape[0]
        ka, nw = a.shape[1] // nb, w.shape[1] // nb
        bt = _pick(T, bt_t)
        nt = T // bt

        def body_tn(a_ref, b_ref, o_ref):
            p = lax.dot_general(a_ref[...].astype(bf16), b_ref[...].astype(bf16), _DIMS["tn"], preferred_element_type=f32)
            k = pl.program_id(1)

            @pl.when(k == 0)
            def _():
                o_ref[...] = p

            @pl.when(k > 0)
            def _():
                o_ref[...] += p

        return pl.pallas_call(
            body_tn, name=name, grid=(nb, nt),
            in_specs=[pl.BlockSpec((bt, ka), lambda j, k: (k, j)), pl.BlockSpec((bt, nw), lambda j, k: (k, j))],
            out_specs=pl.BlockSpec((ka, nw), lambda j, k: (j, 0)), out_shape=jax.ShapeDtypeStruct((nb * ka, nw), f32),
            compiler_params=_params(("parallel", "arbitrary")),
        )(a, w)
    T = a.shape[0]
    ka, nw = w.shape[0] // nb, w.shape[1]
    bt = _pick(T, bt_t)
    kin, kout = (ka, nw) if mode == "nn" else (nw, ka)
    dn = _DIMS[mode]

    def body(a_ref, w_ref, o_ref):
        o_ref[...] = lax.dot_general(a_ref[...].astype(bf16), w_ref[...].astype(bf16), dn,
                                     preferred_element_type=f32).astype(o_ref.dtype)

    return pl.pallas_call(
        body, name=name, grid=(nb, T // bt),
        in_specs=[pl.BlockSpec((bt, kin), lambda j, i: (i, j)), pl.BlockSpec((ka, nw), lambda j, i: (j, 0))],
        out_specs=pl.BlockSpec((bt, kout), lambda j, i: (i, j)), out_shape=jax.ShapeDtypeStruct((T, nb * kout), out_dtype),
        compiler_params=_params(("parallel", "parallel")),
    )(a, w)


def _row_index(i, cb):
    return (i, cb)


def _row_kernel(name, fn, rows, pars, row_outs, par_outs=(), block_rows=256):
    T = rows[0][0].shape[0]
    R = min(block_rows, T)
    assert T % R == 0
    nr, npar, nro = len(rows), len(pars), len(row_outs)

    def body(*refs):
        rv = [r[...] for r in refs[:nr]]
        pv = [r[...] for r in refs[nr:nr + npar]]
        ro, po = fn(rv, pv)
        for ref, v in zip(refs[nr + npar:nr + npar + nro], ro):
            ref[...] = v.astype(ref.dtype)
        if par_outs:
            i = pl.program_id(0)
            prefs = refs[nr + npar + nro:]

            @pl.when(i == 0)
            def _():
                for ref, v in zip(prefs, po):
                    ref[...] = v.astype(f32)

            @pl.when(i > 0)
            def _():
                for ref, v in zip(prefs, po):
                    ref[...] += v.astype(f32)

    in_specs = [pl.BlockSpec((R, nc), functools.partial(_row_index, cb=cb)) for (_, nc, cb) in rows]
    in_specs += [pl.BlockSpec(p.shape, lambda i: (0, 0)) for p in pars]
    out_specs = [pl.BlockSpec((R, nc), lambda i: (i, 0)) for (nc, _) in row_outs]
    out_specs += [pl.BlockSpec(s, lambda i: (0, 0)) for s in par_outs]
    out_shape = [jax.ShapeDtypeStruct((T, nc), dt) for (nc, dt) in row_outs]
    out_shape += [jax.ShapeDtypeStruct(s, f32) for s in par_outs]
    outs = pl.pallas_call(
        body, name=name, grid=(T // R,), in_specs=in_specs, out_specs=out_specs, out_shape=out_shape,
        compiler_params=_params(("arbitrary",) if par_outs else ("parallel",)),
    )(*[r[0] for r in rows], *pars)
    return list(outs)


def _fwd_of(f):
    def fn(rv, pv):
        return f([v.astype(f32) for v in rv], [v.astype(f32) for v in pv]), []
    return fn


def _vjp_of(f, n_x, n_cot, grad_idx, n_add=0):
    def fn(rv, pv):
        xs = [v.astype(f32) for v in rv[:n_x]]
        cots = [v.astype(f32) for v in rv[n_x:n_x + n_cot]]
        adds = rv[n_x + n_cot:n_x + n_cot + n_add]
        ps = [v.astype(f32) for v in pv]
        _, vjp = jax.vjp(lambda *a: f(list(a[:n_x]), list(a[n_x:])), *xs, *ps)
        g = vjp(cots)
        row_g = [g[i] for i in grad_idx]
        for k, a in enumerate(adds):
            row_g[k] = row_g[k] + a.astype(f32)
        return row_g, list(g[n_x:])
    return fn


def _rms(x, g):
    return x * lax.rsqrt(jnp.mean(x * x, axis=-1, keepdims=True) + RMS_EPS) * g


def _f_norm(xs, ps):
    return [_rms(xs[0], ps[0])]


def _f_post(scale):
    def f(xs, ps):
        return [scale * _rms(xs[0], ps[0])]
    return f


def _f_resnorm(scale):
    def f(xs, ps):
        return [xs[0] + scale * _rms(xs[1], ps[0])]
    return f


def _f_dt(xs, ps):
    dt = jax.nn.softplus(xs[0] + ps[0])
    return [dt, -jnp.exp(ps[1]) * dt]


def _f_ssdpost(n_groups):
    def f(xs, ps):
        y = xs[0] * jax.nn.silu(xs[1])
        width = y.shape[-1] // n_groups
        lane = lax.broadcasted_iota(jnp.int32, y.shape, 1)
        scale = jnp.zeros_like(y)
        for k in range(n_groups):
            m = ((lane >= k * width) & (lane < (k + 1) * width)).astype(f32)
            ms = jnp.sum(y * y * m, axis=-1, keepdims=True) / width
            scale = scale + lax.rsqrt(ms + RMS_EPS) * m
        return [y * scale * ps[0]]
    return f


def _f_s5post(xs, ps):
    return [jax.nn.gelu(xs[0] + ps[0] * xs[1])]


def _f_merge(xs, ps):
    return [jax.nn.sigmoid(xs[0]) * xs[1] + jax.nn.sigmoid(xs[2]) * xs[3]]


def _swiglu_fwd(rv, pv):
    ab = rv[0].astype(f32)
    h = ab.shape[1] // 2
    return [jax.nn.silu(ab[:, :h]) * ab[:, h:]], []


def _swiglu_bwd(rv, pv):
    ab, d = rv[0].astype(f32), rv[1].astype(f32)
    h = ab.shape[1] // 2
    a, b = ab[:, :h], ab[:, h:]
    s = jax.nn.sigmoid(a)
    return [jnp.concatenate([d * b * (s * (1.0 + a * (1.0 - s))), d * (a * s)], axis=1)], []


def _glu_fwd(rv, pv):
    vg = rv[0].astype(f32)
    h = vg.shape[1] // 2
    return [vg[:, :h] * jax.nn.sigmoid(vg[:, h:])], []


def _glu_bwd(rv, pv):
    vg, d = rv[0].astype(f32), rv[1].astype(f32)
    h = vg.shape[1] // 2
    s = jax.nn.sigmoid(vg[:, h:])
    return [jnp.concatenate([d * s, d * vg[:, :h] * s * (1.0 - s)], axis=1)], []


def _loss_fn(rv, pv):
    e = rv[0].astype(f32) - rv[1].astype(f32)
    per_tok = jnp.mean(e * e, axis=-1, keepdims=True)
    part = 0.5 * jnp.sum(per_tok, axis=0, keepdims=True)
    return [e / e.shape[-1]], [jnp.broadcast_to(part, (8, LANES))]


def _add_fn(rv, pv):
    return [rv[0].astype(f32) + rv[1].astype(f32)], []


def _adamw_fn(rv, pv):
    w, g, m, v = [x.astype(f32) for x in rv]
    m = ADAM_B1 * m + (1.0 - ADAM_B1) * g
    v = ADAM_B2 * v + (1.0 - ADAM_B2) * (g * g)
    m_hat = m / (1.0 - ADAM_B1 ** ADAM_STEP)
    v_hat = v / (1.0 - ADAM_B2 ** ADAM_STEP)
    return [-ADAM_LR * (m_hat / (jnp.sqrt(v_hat) + ADAM_EPS) + ADAM_WD * w), m, v], []


def _adamw(w, g, m, v, name):
    shape = w.shape
    cols = shape[-1] if (w.ndim >= 2 and shape[-1] >= LANES) else None
    if cols is None:
        n = int(np.prod(shape))
        cols = LANES if n % LANES == 0 else n
    n_rows = int(np.prod(shape)) // cols
    br, bc = _pick(n_rows, 256, 8), cols
    if br < 64 and cols % LANES == 0:
        br, bc = n_rows, LANES

    def body(w_ref, g_ref, m_ref, v_ref, d_ref, nm_ref, nv_ref):
        outs, _ = _adamw_fn([w_ref[...], g_ref[...], m_ref[...], v_ref[...]], [])
        d_ref[...], nm_ref[...], nv_ref[...] = outs

    spec = pl.BlockSpec((br, bc), lambda i, j: (i, j))
    outs = pl.pallas_call(
        body, name=name, grid=(n_rows // br, cols // bc), in_specs=[spec] * 4, out_specs=[spec] * 3,
        out_shape=[jax.ShapeDtypeStruct((n_rows, cols), f32)] * 3, compiler_params=_params(("parallel", "parallel")),
    )(*[t.reshape(n_rows, cols) for t in (w, g, m, v)])
    return [o.reshape(shape) for o in outs]


def _shift_down(x, s, row):
    if s == 0:
        return x
    return jnp.where(row >= s, pltpu.roll(x, s, 0), 0.0)


def _shift_up(x, s, row):
    if s == 0:
        return x
    n = x.shape[0]
    return jnp.where(row < n - s, pltpu.roll(x, n - s, 0), 0.0)


def _conv_pre(x, w, b, row):
    kw = w.shape[0]
    c = b
    for k in range(kw):
        c = c + w[k:k + 1, :] * _shift_down(x, kw - 1 - k, row)
    return c


def _conv_fwd(xsrc, col0, w, b, name, bc_t=512):
    T = xsrc.shape[0]
    kw, ncols = w.shape
    bc = _pick(ncols, bc_t)
    off = col0 // bc
    assert col0 % bc == 0

    def body(x_ref, w_ref, b_ref, o_ref):
        x = x_ref[...].astype(f32)
        row = lax.broadcasted_iota(jnp.int32, x.shape, 0)
        c = _conv_pre(x, w_ref[...], b_ref[...], row)
        o_ref[...] = c * jax.nn.sigmoid(c)

    return pl.pallas_call(
        body, name=name, grid=(ncols // bc,),
        in_specs=[pl.BlockSpec((T, bc), lambda j: (0, off + j)), pl.BlockSpec((kw, bc), lambda j: (0, j)),
                  pl.BlockSpec((1, bc), lambda j: (0, j))],
        out_specs=pl.BlockSpec((T, bc), lambda j: (0, j)), out_shape=jax.ShapeDtypeStruct((T, ncols), f32),
        compiler_params=_params(("parallel",)),
    )(xsrc, w, b)


def _conv_bwd(xsrc, col0, w, b, dact, name, bc_t=512):
    T = xsrc.shape[0]
    kw, ncols = w.shape
    bc = _pick(ncols, bc_t)
    off = col0 // bc
    assert col0 % bc == 0

    def body(x_ref, w_ref, b_ref, d_ref, dx_ref, dw_ref, db_ref):
        x = x_ref[...].astype(f32)
        w = w_ref[...]
        row = lax.broadcasted_iota(jnp.int32, x.shape, 0)
        c = _conv_pre(x, w, b_ref[...], row)
        s = jax.nn.sigmoid(c)
        dc = d_ref[...].astype(f32) * (s * (1.0 + c * (1.0 - s)))
        dx = jnp.zeros_like(x)
        dws = []
        for k in range(kw):
            dx = dx + w[k:k + 1, :] * _shift_up(dc, kw - 1 - k, row)
            dws.append(jnp.sum(dc * _shift_down(x, kw - 1 - k, row), axis=0, keepdims=True))
        dx_ref[...] = dx.astype(dx_ref.dtype)
        dw_ref[...] = jnp.concatenate(dws, axis=0)
        db_ref[...] = jnp.sum(dc, axis=0, keepdims=True)

    return pl.pallas_call(
        body, name=name, grid=(ncols // bc,),
        in_specs=[pl.BlockSpec((T, bc), lambda j: (0, off + j)), pl.BlockSpec((kw, bc), lambda j: (0, j)),
                  pl.BlockSpec((1, bc), lambda j: (0, j)), pl.BlockSpec((T, bc), lambda j: (0, j))],
        out_specs=[pl.BlockSpec((T, bc), lambda j: (0, j)), pl.BlockSpec((kw, bc), lambda j: (0, j)),
                   pl.BlockSpec((1, bc), lambda j: (0, j))],
        out_shape=[jax.ShapeDtypeStruct((T, ncols), bf16), jax.ShapeDtypeStruct((kw, ncols), f32),
                   jax.ShapeDtypeStruct((1, ncols), f32)],
        compiler_params=_params(("parallel",)),
    )(xsrc, w, b, dact)


_HI = lax.Precision.HIGHEST


def _dot(a, b, dims="nn", precision=None):
    return lax.dot_general(a, b, _DIMS[dims], preferred_element_type=f32, precision=precision)


def _ssd_common(x_ref, b_ref, c_ref, dt_ref, adt_ref, d_ref, hpg, p):
    q = b_ref.shape[0]
    hp = hpg * p
    bb, cb = b_ref[...].astype(bf16), c_ref[...].astype(bf16)
    r = lax.broadcasted_iota(jnp.int32, (q, q), 0)
    s = lax.broadcasted_iota(jnp.int32, (q, q), 1)
    tril = r >= s
    trilf = tril.astype(f32)
    eh = lax.broadcasted_iota(jnp.int32, (LANES, hp), 0)
    ec = lax.broadcasted_iota(jnp.int32, (LANES, hp), 1)
    expand = ((ec >= eh * p) & (ec < (eh + 1) * p)).astype(f32)
    adt = adt_ref[...]
    cum = _dot(trilf, adt, "nn", _HI)
    cum_t = _dot(adt, (r <= s).astype(f32), "tn", _HI)
    cum_e = _dot(cum, expand, "nn", _HI)
    dt_e = _dot(dt_ref[...], expand, "nn", _HI)
    d_e = _dot(jnp.broadcast_to(d_ref[...], (8, LANES)), expand, "nn", _HI)[0:1, :]
    gmat = _dot(cb, bb, "nt")
    x = x_ref[...]
    xdt = x * dt_e
    e_all = jnp.exp(cum_e)
    dec = jnp.exp(cum_e[q - 1:q, :] - cum_e)
    lms, ms = [], []
    for h in range(hpg):
        lm = jnp.exp(jnp.where(tril, cum[:, h:h + 1] - cum_t[h:h + 1, :], -1e30))
        lms.append(lm)
        ms.append(gmat * lm)
    et = [jnp.exp(cum[q - 1:q, h:h + 1]) for h in range(hpg)]
    return dict(bb=bb, cb=cb, trilf=trilf, expand=expand, cum=cum, x=x, xdt=xdt, dt_e=dt_e, d_e=d_e, e=e_all, dec=dec,
                lms=lms, ms=ms, et=et)


def _ssd_specs(q, hp, n, g_n, nc, rev):
    def cidx(c):
        return (nc - 1 - c) if rev else c
    x_spec = pl.BlockSpec((q, hp), lambda g, c: (cidx(c), g))
    boff = (g_n * hp) // n
    b_spec = pl.BlockSpec((q, n), lambda g, c: (cidx(c), boff + g))
    c_spec = pl.BlockSpec((q, n), lambda g, c: (cidx(c), boff + g_n + g))
    dt_spec = pl.BlockSpec((q, LANES), lambda g, c: (cidx(c), g))
    d_spec = pl.BlockSpec((1, LANES), lambda g, c: (0, g))
    st_spec = pl.BlockSpec((1, 1, hp, n), lambda g, c: (cidx(c), g, 0, 0))
    return x_spec, b_spec, c_spec, dt_spec, d_spec, st_spec


def _ssd_fwd(act, dt, adt, dpad, hpg, p, n, name):
    T = act.shape[0]
    g_n, q = SSD_N_GROUPS, SSD_CHUNK
    nc, hp = T // q, hpg * p
    x_spec, b_spec, c_spec, dt_spec, d_spec, st_spec = _ssd_specs(q, hp, n, g_n, nc, False)

    def body(x_ref, b_ref, c_ref, dt_ref, adt_ref, d_ref, y_ref, st_ref, s_scr):
        @pl.when(pl.program_id(1) == 0)
        def _():
            s_scr[...] = jnp.zeros_like(s_scr)

        k = _ssd_common(x_ref, b_ref, c_ref, dt_ref, adt_ref, d_ref, hpg, p)
        s0 = s_scr[...]
        st_ref[0, 0] = s0
        xdtb = k["xdt"].astype(bf16)
        ydiag = [_dot(k["ms"][h].astype(bf16), xdtb[:, h * p:(h + 1) * p]) for h in range(hpg)]
        z = _dot(k["cb"], s0.astype(bf16), "nt")
        y_ref[...] = jnp.concatenate(ydiag, axis=1) + k["e"] * z + k["d_e"] * k["x"]
        upd = _dot((k["xdt"] * k["dec"]).astype(bf16), k["bb"], "tn")
        for h in range(hpg):
            s_scr[h * p:(h + 1) * p, :] = k["et"][h] * s0[h * p:(h + 1) * p, :] + upd[h * p:(h + 1) * p, :]

    return pl.pallas_call(
        body, name=name, grid=(g_n, nc),
        in_specs=[x_spec, b_spec, c_spec, dt_spec, dt_spec, d_spec],
        out_specs=[pl.BlockSpec((q, hp), lambda g, c: (c, g)), st_spec],
        out_shape=[jax.ShapeDtypeStruct((T, g_n * hp), f32), jax.ShapeDtypeStruct((nc, g_n, hp, n), f32)],
        scratch_shapes=[pltpu.VMEM((hp, n), f32)],
        compiler_params=_params(("parallel", "arbitrary")),
    )(act, act, act, dt, adt, dpad)


def _ssd_bwd(act, dt, adt, dpad, states, dy, hpg, p, n, name):
    T = act.shape[0]
    g_n, q = SSD_N_GROUPS, SSD_CHUNK
    nc, hp = T // q, hpg * p
    x_spec, b_spec, c_spec, dt_spec, d_spec, st_spec = _ssd_specs(q, hp, n, g_n, nc, True)

    def body(x_ref, b_ref, c_ref, dt_ref, adt_ref, d_ref, st_ref, dy_ref,
             dx_ref, db_ref, dc_ref, ddt_ref, dadt_ref, dd_ref, ds_scr):
        first = pl.program_id(1) == 0

        @pl.when(first)
        def _():
            ds_scr[...] = jnp.zeros_like(ds_scr)

        k = _ssd_common(x_ref, b_ref, c_ref, dt_ref, adt_ref, d_ref, hpg, p)
        bb, cb, expand, x, xdt, dec = k["bb"], k["cb"], k["expand"], k["x"], k["xdt"], k["dec"]
        heads = lambda t: _dot(t, expand, "nt", _HI)
        s0 = st_ref[0, 0]
        ds1 = ds_scr[...]
        s0b, ds1b = s0.astype(bf16), ds1.astype(bf16)
        dy = dy_ref[...]
        dyb, xdtb = dy.astype(bf16), xdt.astype(bf16)
        lane = lax.broadcasted_iota(jnp.int32, (1, LANES), 1)
        dg = jnp.zeros((q, q), f32)
        w_rows = jnp.zeros((q, LANES), f32)
        w_cols, dxdt_parts = [], []
        for h in range(hpg):
            hs = slice(h * p, (h + 1) * p)
            dm = _dot(dyb[:, hs], xdtb[:, hs], "nt")
            dg = dg + dm * k["lms"][h]
            wm = dm * k["ms"][h]
            w_rows = w_rows + jnp.sum(wm, axis=1, keepdims=True) * (lane == h).astype(f32)
            w_cols.append(jnp.sum(wm, axis=0, keepdims=True))
            dxdt_parts.append(_dot(k["ms"][h].astype(bf16), dyb[:, hs], "tn"))
        dxdt_diag = jnp.concatenate(dxdt_parts, axis=1)
        w_cols = jnp.concatenate(w_cols + [jnp.zeros((LANES - hpg, q), f32)], axis=0).T
        dgb = dg.astype(bf16)
        z = _dot(cb, s0b, "nt")
        dz = dy * k["e"]
        dzb = dz.astype(bf16)
        dxd = _dot(bb, ds1b, "nt")
        ddec = dxd * xdt * dec
        db_ref[...] = _dot(dgb, cb, "tn") + _dot((xdt * dec).astype(bf16), ds1b)
        dc_ref[...] = _dot(dgb, bb) + _dot(dzb, s0b)
        ds0 = _dot(dzb, cb, "tn")
        for h in range(hpg):
            hs = slice(h * p, (h + 1) * p)
            ds_scr[hs, :] = ds0[hs, :] + k["et"][h] * ds1[hs, :]
        dxdt = dxdt_diag + dxd * dec
        ddec_h = heads(ddec)
        dcum = w_rows - w_cols + heads(dz * z) - ddec_h
        et_row = jnp.exp(k["cum"][q - 1:q, :])
        dsum = _dot(jnp.ones((8, n), f32), _dot(expand, ds1 * s0, "nn", _HI), "nt", _HI)[0:1, :]
        dcl = dsum * et_row + jnp.sum(ddec_h, axis=0, keepdims=True)
        rowq = lax.broadcasted_iota(jnp.int32, (q, 1), 0)
        dcum = dcum + (rowq == q - 1).astype(f32) * dcl
        ddt_ref[...] = heads(dxdt * x)
        dadt_ref[...] = _dot(k["trilf"], dcum, "tn", _HI)
        dx_ref[...] = k["d_e"] * dy + dxdt * k["dt_e"]
        dd8 = heads(jnp.broadcast_to(jnp.sum(dy * x, axis=0, keepdims=True), (8, hp)))

        @pl.when(first)
        def _():
            dd_ref[...] = dd8

        @pl.when(jnp.logical_not(first))
        def _():
            dd_ref[...] += dd8

    rc = lambda g, c: (nc - 1 - c, g)
    return pl.pallas_call(
        body, name=name, grid=(g_n, nc),
        in_specs=[x_spec, b_spec, c_spec, dt_spec, dt_spec, d_spec, st_spec, pl.BlockSpec((q, hp), rc)],
        out_specs=[pl.BlockSpec((q, hp), rc), pl.BlockSpec((q, n), rc), pl.BlockSpec((q, n), rc),
                   pl.BlockSpec((q, LANES), rc), pl.BlockSpec((q, LANES), rc), pl.BlockSpec((8, LANES), lambda g, c: (g, 0))],
        out_shape=[jax.ShapeDtypeStruct((T, g_n * hp), f32), jax.ShapeDtypeStruct((T, g_n * n), f32),
                   jax.ShapeDtypeStruct((T, g_n * n), f32), jax.ShapeDtypeStruct((T, g_n * LANES), f32),
                   jax.ShapeDtypeStruct((T, g_n * LANES), f32), jax.ShapeDtypeStruct((g_n * 8, LANES), f32)],
        scratch_shapes=[pltpu.VMEM((hp, n), f32)],
        compiler_params=_params(("parallel", "arbitrary")),
    )(act, act, act, dt, adt, dpad, states, dy)


def _s5_scan_fwd(bu, lam_re, lam_im, nsb, name, tc_t=512):
    T = bu.shape[0]
    w2 = bu.shape[1] // nsb
    w = w2 // 2
    tc = _pick(T, tc_t, 8)

    def body(bu_ref, lr_ref, li_ref, st_ref, carry):
        @pl.when(pl.program_id(1) == 0)
        def _():
            carry[...] = jnp.zeros_like(carry)

        lr, li = lr_ref[0:1, :], li_ref[0:1, :]

        def step(t, s):
            sr, si = s
            row = bu_ref[pl.ds(t, 1), :]
            nsr = lr * sr - li * si + row[:, :w]
            nsi = lr * si + li * sr + row[:, w:]
            st_ref[pl.ds(t, 1), :] = jnp.concatenate([nsr, nsi], axis=1)
            return nsr, nsi

        sr, si = lax.fori_loop(0, tc, step, (carry[0:1, :], carry[1:2, :]))
        carry[0:1, :] = sr
        carry[1:2, :] = si

    return pl.pallas_call(
        body, name=name, grid=(nsb, T // tc),
        in_specs=[pl.BlockSpec((tc, w2), lambda j, i: (i, j)), pl.BlockSpec((8, w), lambda j, i: (j, 0)),
                  pl.BlockSpec((8, w), lambda j, i: (j, 0))],
        out_specs=pl.BlockSpec((tc, w2), lambda j, i: (i, j)), out_shape=jax.ShapeDtypeStruct(bu.shape, f32),
        scratch_shapes=[pltpu.VMEM((8, w), f32)],
        compiler_params=_params(("parallel", "arbitrary")),
    )(bu, lam_re, lam_im)


def _s5_scan_bwd(gst, states, lam_re, lam_im, nsb, name, tc_t=512):
    T = gst.shape[0]
    w2 = gst.shape[1] // nsb
    w = w2 // 2
    tc = _pick(T, tc_t, 8)
    nt = T // tc

    def body(g_ref, s_ref, sp_ref, lr_ref, li_ref, a_ref, dlr_ref, dli_ref, carry):
        i = pl.program_id(1)

        @pl.when(i == 0)
        def _():
            carry[...] = jnp.zeros_like(carry)

        lr, li = lr_ref[0:1, :], li_ref[0:1, :]

        def adj(t, ar, ai):
            row = g_ref[pl.ds(t, 1), :]
            nar = row[:, :w] + lr * ar + li * ai
            nai = row[:, w:] - li * ar + lr * ai
            a_ref[pl.ds(t, 1), :] = jnp.concatenate([nar, nai], axis=1)
            return nar, nai

        def acc(prev, ar, ai, dlr, dli):
            spr, spi = prev[:, :w], prev[:, w:]
            return dlr + ar * spr + ai * spi, dli - ar * spi + ai * spr

        def step(k, c):
            ar, ai, dlr, dli = c
            t = tc - 1 - k
            ar, ai = adj(t, ar, ai)
            dlr, dli = acc(s_ref[pl.ds(t - 1, 1), :], ar, ai, dlr, dli)
            return ar, ai, dlr, dli

        c0 = (carry[0:1, :], carry[1:2, :], carry[2:3, :], carry[3:4, :])
        ar, ai, dlr, dli = lax.fori_loop(0, tc - 1, step, c0)
        ar, ai = adj(0, ar, ai)
        prev = sp_ref[tc - 1:tc, :] * (i < nt - 1).astype(f32)
        dlr, dli = acc(prev, ar, ai, dlr, dli)
        carry[0:1, :] = ar
        carry[1:2, :] = ai
        carry[2:3, :] = dlr
        carry[3:4, :] = dli
        dlr_ref[...] = jnp.broadcast_to(dlr, (8, w))
        dli_ref[...] = jnp.broadcast_to(dli, (8, w))

    cur = lambda j, i: (nt - 1 - i, j)
    prv = lambda j, i: (jnp.maximum(nt - 2 - i, 0), j)
    return pl.pallas_call(
        body, name=name, grid=(nsb, nt),
        in_specs=[pl.BlockSpec((tc, w2), cur), pl.BlockSpec((tc, w2), cur), pl.BlockSpec((tc, w2), prv),
                  pl.BlockSpec((8, w), lambda j, i: (j, 0)), pl.BlockSpec((8, w), lambda j, i: (j, 0))],
        out_specs=[pl.BlockSpec((tc, w2), cur), pl.BlockSpec((8, w), lambda j, i: (j, 0)),
                   pl.BlockSpec((8, w), lambda j, i: (j, 0))],
        out_shape=[jax.ShapeDtypeStruct(gst.shape, f32), jax.ShapeDtypeStruct((nsb * 8, w), f32),
                   jax.ShapeDtypeStruct((nsb * 8, w), f32)],
        scratch_shapes=[pltpu.VMEM((8, w), f32)],
        compiler_params=_params(("parallel", "arbitrary")),
    )(gst, states, states, lam_re, lam_im)


def _s5_prep_fn(xs, ps):
    lam_re, lam_im, log_step, b_re, b_im, expand = ps
    lr = jnp.minimum(lam_re, S5_MAX_REAL)
    li = lam_im
    step = jnp.exp(log_step)
    er = jnp.exp(lr * step)
    ang = li * step
    lbr, lbi = er * jnp.cos(ang), er * jnp.sin(ang)
    nr, ni = lbr - 1.0, lbi
    den = lr * lr + li * li
    qr, qi = (nr * lr + ni * li) / den, (ni * lr - nr * li) / den
    qre, qie = _dot(qr, expand, "nn", _HI), _dot(qi, expand, "nn", _HI)
    return [lbr, lbi, qre * b_re - qie * b_im, qre * b_im + qie * b_re]


def _s5_prep(pars, name):
    def body(*refs):
        outs = _s5_prep_fn([], [r[...] for r in refs[:6]])
        for ref, v in zip(refs[6:], outs):
            ref[...] = v

    g, nst = pars[0].shape
    nc = pars[3].shape[1]
    return pl.pallas_call(
        body, name=name,
        out_shape=[jax.ShapeDtypeStruct((g, nst), f32)] * 2 + [jax.ShapeDtypeStruct((g, nc), f32)] * 2,
        compiler_params=_params(),
    )(*pars)


def _s5_prep_bwd(pars, cots, name):
    def body(*refs):
        ps = [r[...] for r in refs[:6]]
        ct = [r[...] for r in refs[6:10]]
        _, vjp = jax.vjp(lambda *a: _s5_prep_fn([], list(a)), *ps)
        g = vjp(ct)
        for ref, v in zip(refs[10:], g[:5]):
            ref[...] = v

    return pl.pallas_call(
        body, name=name, out_shape=[jax.ShapeDtypeStruct(p.shape, f32) for p in pars[:5]], compiler_params=_params(),
    )(*pars, *cots)


_ANY = pl.BlockSpec(memory_space=pl.ANY)


def _remote(src, dst, send_sem, recv_sem, device):
    return pltpu.make_async_remote_copy(src_ref=src, dst_ref=dst, send_sem=send_sem, recv_sem=recv_sem, device_id=device,
                                        device_id_type=MESH)


def _staged_copy(src, dst, buf, in_sems, out_sems):
    n = D2D_STREAMS
    piece = src.shape[0] // n
    assert src.shape[0] % n == 0

    def load(i):
        return pltpu.make_async_copy(src.at[pl.ds(i * piece, piece)], buf.at[i % 2], in_sems.at[i % 2])

    def store(i):
        return pltpu.make_async_copy(buf.at[i % 2], dst.at[pl.ds(i * piece, piece)], out_sems.at[i % 2])

    load(0).start()
    for i in range(n):
        if i + 1 < n:
            if i >= 1:
                store(i - 1).wait()
            load(i + 1).start()
        load(i).wait()
        store(i).start()
    store(n - 2).wait()
    store(n - 1).wait()


def _stage_scratch(rows, cols, dtype):
    return [pltpu.VMEM((2, rows // D2D_STREAMS, cols), dtype), pltpu.SemaphoreType.DMA((2,)), pltpu.SemaphoreType.DMA((2,))]


def _chip_all_gather(block, name):
    rows = block.shape[0]
    half = rows // 2
    piece = half // D2D_STREAMS
    assert rows % (2 * D2D_STREAMS * 16) == 0

    def body(src, out, ici_send, ici_recv, d2d_send, d2d_recv, *stage):
        x, y, c = lax.axis_index("x"), lax.axis_index("y"), lax.axis_index("c")
        me = 2 * x + y
        sibling = (x, y, 1 - c)
        chips = [(1 - x, y), (x, 1 - y), (1 - x, 1 - y)]
        mine = pl.ds(pl.multiple_of(c * half, 16), half)
        sends = []
        for j, (px, py) in enumerate(chips):
            cp = _remote(src.at[mine], out.at[me, mine], ici_send.at[j], ici_recv.at[j], (px, py, c))
            cp.start()
            sends.append(cp)
        _staged_copy(src, out.at[me], *stage)
        for j, (px, py) in enumerate(chips):
            slot = 2 * px + py
            _remote(src.at[mine], out.at[slot, mine], ici_send.at[j], ici_recv.at[j], (px, py, c)).wait_recv()
            for s in range(D2D_STREAMS):
                r = pl.ds(pl.multiple_of(c * half + s * piece, 16), piece)
                k = j * D2D_STREAMS + s
                cp = _remote(out.at[slot, r], out.at[slot, r], d2d_send.at[k], d2d_recv.at[k], sibling)
                cp.start()
                sends.append(cp)
        for j, (px, py) in enumerate(chips):
            slot = 2 * px + py
            for s in range(D2D_STREAMS):
                r = pl.ds(pl.multiple_of((1 - c) * half + s * piece, 16), piece)
                k = j * D2D_STREAMS + s
                _remote(out.at[slot, r], out.at[slot, r], d2d_send.at[k], d2d_recv.at[k], sibling).wait_recv()
        for cp in sends:
            cp.wait_send()

    n_d2d = 3 * D2D_STREAMS
    return pl.pallas_call(
        body, name=name, in_specs=[_ANY], out_specs=_ANY,
        out_shape=jax.ShapeDtypeStruct((N_CHIPS,) + block.shape, block.dtype),
        scratch_shapes=[pltpu.SemaphoreType.DMA((3,)), pltpu.SemaphoreType.DMA((3,)), pltpu.SemaphoreType.DMA((n_d2d,)),
                        pltpu.SemaphoreType.DMA((n_d2d,))] + _stage_scratch(rows, block.shape[1], block.dtype),
    )(block)


def _chip_scatter(parts, name):
    def body(src, out, send_sems, recv_sems, *stage):
        x, y, c = lax.axis_index("x"), lax.axis_index("y"), lax.axis_index("c")
        me = 2 * x + y
        chips = [(1 - x, y), (x, 1 - y), (1 - x, 1 - y)]
        sends = []
        for j, (px, py) in enumerate(chips):
            cp = pltpu.make_async_remote_copy(src_ref=src.at[2 * px + py], dst_ref=out.at[me], send_sem=send_sems.at[j],
                                              recv_sem=recv_sems.at[j], device_id=(px, py, c), device_id_type=MESH)
            cp.start()
            sends.append(cp)
        _staged_copy(src.at[me], out.at[me], *stage)
        for j, (px, py) in enumerate(chips):
            pltpu.make_async_remote_copy(src_ref=src.at[me], dst_ref=out.at[2 * px + py], send_sem=send_sems.at[j],
                                         recv_sem=recv_sems.at[j], device_id=(px, py, c), device_id_type=MESH).wait_recv()
        for cp in sends:
            cp.wait_send()

    return pl.pallas_call(
        body, name=name, in_specs=[_ANY], out_specs=_ANY, out_shape=jax.ShapeDtypeStruct(parts.shape, parts.dtype),
        scratch_shapes=[pltpu.SemaphoreType.DMA((3,)), pltpu.SemaphoreType.DMA((3,))]
        + _stage_scratch(parts.shape[1], parts.shape[2], parts.dtype),
    )(parts)


def _core_send_other_half(parts, name):
    n_slots, rows, cols = parts.shape
    half = rows // 2
    piece = half // D2D_STREAMS
    assert rows % (2 * D2D_STREAMS * 16) == 0

    def body(src, out, send_sems, recv_sems):
        x, y, c = lax.axis_index("x"), lax.axis_index("y"), lax.axis_index("c")
        sibling = (x, y, 1 - c)
        sends = []
        for k in range(n_slots):
            for s in range(D2D_STREAMS):
                theirs = pl.ds(pl.multiple_of((1 - c) * half + s * piece, 16), piece)
                i = k * D2D_STREAMS + s
                cp = _remote(src.at[k, theirs], out.at[k, pl.ds(s * piece, piece)], send_sems.at[i], recv_sems.at[i], sibling)
                cp.start()
                sends.append(cp)
        for cp in sends:
            cp.wait_recv()
        for cp in sends:
            cp.wait_send()

    n = n_slots * D2D_STREAMS
    return pl.pallas_call(
        body, name=name, in_specs=[_ANY], out_specs=_ANY, out_shape=jax.ShapeDtypeStruct((n_slots, half, cols), parts.dtype),
        scratch_shapes=[pltpu.SemaphoreType.DMA((n,)), pltpu.SemaphoreType.DMA((n,))],
    )(parts)


def _add_my_half(parts, other, core, name, block_rows=256):
    n_slots, rows, cols = parts.shape
    half = rows // 2
    br = _pick(half, block_rows, 16)
    nb = half // br

    def body(c_ref, a_ref, b_ref, o_ref):
        o_ref[...] = (a_ref[...].astype(f32) + b_ref[...].astype(f32)).astype(o_ref.dtype)

    grid_spec = pltpu.PrefetchScalarGridSpec(
        num_scalar_prefetch=1, grid=(n_slots, nb),
        in_specs=[pl.BlockSpec((1, br, cols), lambda k, i, c: (k, c[0] * nb + i, 0)),
                  pl.BlockSpec((1, br, cols), lambda k, i, c: (k, i, 0))],
        out_specs=pl.BlockSpec((1, br, cols), lambda k, i, c: (k, i, 0)))
    return pl.pallas_call(
        body, name=name, grid_spec=grid_spec, out_shape=jax.ShapeDtypeStruct((n_slots, half, cols), bf16),
        compiler_params=_params(("parallel", "parallel")),
    )(core, parts, other)


def _core_join_halves(mine, name):
    half, cols = mine.shape
    piece = half // D2D_STREAMS
    assert half % (D2D_STREAMS * 16) == 0

    def body(src, out, send_sems, recv_sems, *stage):
        x, y, c = lax.axis_index("x"), lax.axis_index("y"), lax.axis_index("c")
        sibling = (x, y, 1 - c)
        sends = []
        for s in range(D2D_STREAMS):
            dst = out.at[pl.ds(pl.multiple_of(c * half + s * piece, 16), piece)]
            cp = _remote(src.at[pl.ds(s * piece, piece)], dst, send_sems.at[s], recv_sems.at[s], sibling)
            cp.start()
            sends.append(cp)
        _staged_copy(src, out.at[pl.ds(pl.multiple_of(c * half, 16), half)], *stage)
        for s in range(D2D_STREAMS):
            dst = out.at[pl.ds(pl.multiple_of((1 - c) * half + s * piece, 16), piece)]
            _remote(src.at[pl.ds(s * piece, piece)], dst, send_sems.at[s], recv_sems.at[s], sibling).wait_recv()
        for cp in sends:
            cp.wait_send()

    return pl.pallas_call(
        body, name=name, in_specs=[_ANY], out_specs=_ANY, out_shape=jax.ShapeDtypeStruct((2 * half, cols), mine.dtype),
        scratch_shapes=[pltpu.SemaphoreType.DMA((D2D_STREAMS,)), pltpu.SemaphoreType.DMA((D2D_STREAMS,))]
        + _stage_scratch(half, cols, mine.dtype),
    )(mine)


def _device_all_gather(block, name):
    def body(src, out, send_sems, recv_sems, *stage):
        x, y, c = lax.axis_index("x"), lax.axis_index("y"), lax.axis_index("c")
        me = 4 * x + 2 * y + c
        peers = []
        for mask in range(1, N_DEV):
            fx, fy, fc = (mask >> 2) & 1, (mask >> 1) & 1, mask & 1
            peers.append((x ^ fx, y ^ fy, c ^ fc))
        sends = []
        for j, peer in enumerate(peers):
            cp = pltpu.make_async_remote_copy(src_ref=src, dst_ref=out.at[me], send_sem=send_sems.at[j],
                                              recv_sem=recv_sems.at[j], device_id=peer, device_id_type=MESH)
            cp.start()
            sends.append(cp)
        _staged_copy(src, out.at[me], *stage)
        for j, (px, py, pc) in enumerate(peers):
            pltpu.make_async_remote_copy(src_ref=src, dst_ref=out.at[4 * px + 2 * py + pc], send_sem=send_sems.at[j],
                                         recv_sem=recv_sems.at[j], device_id=(px, py, pc), device_id_type=MESH).wait_recv()
        for cp in sends:
            cp.wait_send()

    return pl.pallas_call(
        body, name=name, in_specs=[_ANY], out_specs=_ANY,
        out_shape=jax.ShapeDtypeStruct((N_DEV,) + block.shape, block.dtype),
        scratch_shapes=[pltpu.SemaphoreType.DMA((N_DEV - 1,)), pltpu.SemaphoreType.DMA((N_DEV - 1,))]
        + _stage_scratch(block.shape[0], block.shape[1], block.dtype),
    )(block)


def _sum_slots(stack, name, block_rows=256):
    s_n, r_n, c_n = stack.shape
    br = _pick(r_n, block_rows, 8)

    def body(in_ref, o_ref):
        acc = in_ref[0].astype(f32)
        for s in range(1, s_n):
            acc = acc + in_ref[s].astype(f32)
        o_ref[...] = acc

    return pl.pallas_call(
        body, name=name, grid=(r_n // br,), in_specs=[pl.BlockSpec((s_n, br, c_n), lambda i: (0, i, 0))],
        out_specs=pl.BlockSpec((br, c_n), lambda i: (i, 0)), out_shape=jax.ShapeDtypeStruct((r_n, c_n), f32),
        compiler_params=_params(("parallel",)),
    )(stack)


def _pad_rows(a, mult):
    r = (-a.shape[0]) % mult
    return a if r == 0 else jnp.concatenate([a, jnp.zeros((r,) + a.shape[1:], a.dtype)], axis=0)


def _pack_weights(w, conv_w):
    parts = [w[n].astype(bf16).reshape(-1, PACK_COLS) for n in BIG]
    parts.append(lax.bitcast_convert_type(conv_w, bf16).reshape(-1, PACK_COLS))
    return _pad_rows(jnp.concatenate(parts, axis=0), PACK_ROW_MULT)


def _unpack_weights(full, w, conv_w):
    out, r0 = {}, 0
    for n in BIG:
        rows = w[n].size // PACK_COLS
        pieces = full[:, r0:r0 + rows].reshape((N_CHIPS,) + w[n].shape)
        out[n] = jnp.concatenate([pieces[k] for k in range(N_CHIPS)], axis=1)
        r0 += rows
    rows = conv_w.size * 2 // PACK_COLS
    pieces = lax.bitcast_convert_type(full[:, r0:r0 + rows].reshape((N_CHIPS,) + conv_w.shape + (2,)), f32)
    return out, jnp.concatenate([pieces[k] for k in range(N_CHIPS)], axis=2)


def _pack_big_grads(g):
    slots = []
    for k in range(N_CHIPS):
        parts = []
        for n in BIG:
            width = g[n].shape[1] // N_CHIPS
            parts.append(lax.slice_in_dim(g[n], k * width, (k + 1) * width, axis=1).astype(bf16).reshape(-1, PACK_COLS))
        slots.append(_pad_rows(jnp.concatenate(parts, axis=0), PACK_ROW_MULT))
    return jnp.stack(slots)


def _unpack_big_grads(summed, w):
    out, r0 = {}, 0
    for n in BIG:
        rows = w[n].size // PACK_COLS
        out[n] = summed[r0:r0 + rows].reshape(w[n].shape)
        r0 += rows
    return out


def _pack_small(vals, names):
    parts = []
    for n in names:
        flat = vals[n].reshape(-1)
        pad = (-flat.size) % LANES
        if pad:
            flat = jnp.concatenate([flat, jnp.zeros((pad,), flat.dtype)])
        parts.append(flat.reshape(-1, LANES))
    return _pad_rows(jnp.concatenate(parts, axis=0), PACK_ROW_MULT)


def _unpack_small(packed, like, names):
    out, r0 = {}, 0
    for n in names:
        size = like[n].size
        rows = -(-size // LANES)
        out[n] = packed[r0:r0 + rows].reshape(-1)[:size].reshape(like[n].shape)
        r0 += rows
    return out


def _dims(w, x):
    d = {}
    d["D"] = x.shape[-1]
    d["T"] = x.shape[-2]
    d["DI"] = w["ssd_norm_g"].shape[-1]
    d["NH"] = w["ssd_dt_bias"].shape[-1]
    d["CD"] = w["ssd_conv_b"].shape[-1]
    d["G"] = SSD_N_GROUPS
    d["HPG"] = d["NH"] // d["G"]
    d["P"] = d["DI"] // d["NH"]
    d["N"] = (d["CD"] - d["DI"]) // (2 * d["G"])
    d["S5G"], d["S5N"] = w["s5_lambda_re"].shape[-2:]
    d["S5C"] = w["s5_b_re"].shape[-1]
    d["S5W"] = d["S5G"] * d["S5C"]
    d["NSB"] = d["S5W"] // S5_SUPERBLOCK
    d["GSB"] = d["S5G"] // d["NSB"]
    return d


def _head_pad(v, d):
    lead = v.shape[:-1]
    v = v.reshape(lead + (d["G"], d["HPG"]))
    v = jnp.concatenate([v, jnp.zeros(lead + (d["G"], LANES - d["HPG"]), v.dtype)], axis=-1)
    return v.reshape(lead + (d["G"] * LANES,))


def _head_unpad(v, d):
    lead = v.shape[:-1]
    return v.reshape(lead + (d["G"], LANES))[..., :d["HPG"]].reshape(lead + (d["NH"],))


def _w_in_perm(w_in_t, d):
    o = d["DI"] + d["CD"]
    dt = _head_pad(w_in_t[o:o + d["NH"]].T, d).T
    return jnp.concatenate([w_in_t[:o], w_in_t[o + d["NH"]:], dt], axis=0)


def _w_in_unperm(g, d):
    o = d["DI"] + d["CD"]
    rest = d["S5W"] + 2 * d["D"]
    return jnp.concatenate([g[:o], _head_unpad(g[o + rest:].T, d).T, g[o:o + rest]], axis=0)


def _s5_block_diag(v, d):
    nsb, gsb = d["NSB"], d["GSB"]
    a, b = v.shape[1:]
    v = v.reshape(nsb, gsb, a, b)
    eye = jnp.eye(gsb, dtype=v.dtype)
    return (v[:, :, :, None, :] * eye[None, :, None, :, None]).reshape(nsb * gsb * a, gsb * b)


def _s5_diag_blocks(m, d, a, b):
    nsb, gsb = d["NSB"], d["GSB"]
    m = m.reshape(nsb, gsb, a, gsb, b)
    eye = jnp.eye(gsb, dtype=m.dtype)
    return jnp.sum(m * eye[None, :, None, :, None], axis=3).reshape(nsb * gsb, a, b)


def _s5_lam_rows(v, d):
    v = v.reshape(d["NSB"], 1, d["GSB"] * d["S5N"])
    return jnp.broadcast_to(v, (d["NSB"], 8, v.shape[-1])).reshape(d["NSB"] * 8, -1)


def _ffn_fwd(h, pre_g, post_g, wgu, wd, tag):
    D = h.shape[1]
    H2 = wgu.shape[0]
    xn = _row_kernel(f"{tag}_norm", _fwd_of(_f_norm), [(h, D, 0)], [pre_g], [(D, bf16)])[0]
    ab = _mm(xn, wgu, "nt", f32, f"{tag}_mm_up")
    hid = _row_kernel(f"{tag}_swiglu", _swiglu_fwd, [(ab, H2, 0)], [], [(H2 // 2, bf16)])[0]
    f = _mm(hid, wd, "nn", f32, f"{tag}_mm_down")
    out = _row_kernel(f"{tag}_resnorm", _fwd_of(_f_resnorm(0.5)), [(h, D, 0), (f, D, 0)], [post_g], [(D, f32)])[0]
    return out, dict(h=h, xn=xn, ab=ab, hid=hid, f=f)


def _ffn_bwd(dh_out, s, pre_g, post_g, wgu, wd, tag):
    D = dh_out.shape[1]
    H2 = wgu.shape[0]
    df, dpost = _row_kernel(f"{tag}_resnorm_bwd", _vjp_of(_f_post(0.5), 1, 1, [0]), [(s["f"], D, 0), (dh_out, D, 0)],
                            [post_g], [(D, bf16)], [post_g.shape])
    dwd = _mm(s["hid"], df, "tn", f32, f"{tag}_mm_dwd")
    dhid = _mm(df, wd, "nt", bf16, f"{tag}_mm_dhid")
    dab = _row_kernel(f"{tag}_swiglu_bwd", _swiglu_bwd, [(s["ab"], H2, 0), (dhid, H2 // 2, 0)], [], [(H2, bf16)])[0]
    dwgu = _mm(dab, s["xn"], "tn", f32, f"{tag}_mm_dwgu")
    dxn = _mm(dab, wgu, "nn", f32, f"{tag}_mm_dxn")
    dh, dpre = _row_kernel(f"{tag}_norm_bwd", _vjp_of(_f_norm, 1, 1, [0], 1), [(s["h"], D, 0), (dxn, D, 0), (dh_out, D, 0)],
                           [pre_g], [(D, f32)], [pre_g.shape])
    return dh, dict(pre_g=dpre, post_g=dpost, wgu=dwgu, wd=dwd)


def _mixer_fwd(h, p, d):
    D, DI, CD, G, N = d["D"], d["DI"], d["CD"], d["G"], d["N"]
    gl = G * LANES
    c_u5, c_ga, c_gb, c_dt = DI + CD, DI + CD + d["S5W"], DI + CD + d["S5W"] + D, DI + CD + d["S5W"] + 2 * D
    u = _row_kernel("mix_norm", _fwd_of(_f_norm), [(h, D, 0)], [p["mix_pre_g"]], [(D, bf16)])[0]
    proj = _mm(u, p["w_in"], "nt", f32, "mix_mm_in", bn_t=512)
    act = _conv_fwd(proj, DI, p["conv_w"], p["conv_b"], "ssd_conv")
    dt, adt = _row_kernel("ssd_dt", _fwd_of(_f_dt), [(proj, gl, c_dt // gl)], [p["dt_bias"], p["a_log"]], [(gl, f32)] * 2)
    y_ssd, states = _ssd_fwd(act, dt, adt, p["d_skip"], d["HPG"], d["P"], N, "ssd_scan")
    nrm = _row_kernel("ssd_post", _fwd_of(_f_ssdpost(G)), [(y_ssd, DI, 0), (proj, DI, 0)], [p["norm_g"]], [(DI, bf16)])[0]
    y_a = _mm(nrm, p["w_a"], "nn", f32, "mix_mm_a")
    u5 =(proj, d["S5W"], c_u5 // d["S5W"])
    bu = _s5_in(proj, c_u5, p["bsb"], d)
    s5st = _s5_scan_fwd(bu, p["lam_re_rows"], p["lam_im_rows"], d["NSB"], "s5_scan")
    y5 = _bdmm(s5st, p["csb"], "nn", d["NSB"], f32, "s5_mm_c")
    gel = _row_kernel("s5_post", _fwd_of(_f_s5post), [(y5, d["S5W"], 0), u5], [p["s5_d"]], [(d["S5W"], bf16)])[0]
    vg = _mm(gel, p["w_glu"], "nt", f32, "mix_mm_glu")
    glu = _row_kernel("s5_glu", _glu_fwd, [(vg, vg.shape[1], 0)], [], [(vg.shape[1] // 2, bf16)])[0]
    y_b = _mm(glu, p["w_b"], "nn", f32, "mix_mm_b")
    merged = _row_kernel("mix_merge", _fwd_of(_f_merge), [(proj, D, c_ga // D), (y_a, D, 0), (proj, D, c_gb // D), (y_b, D, 0)],
                         [], [(D, bf16)])[0]
    m = _mm(merged, p["w_out"], "nn", f32, "mix_mm_out")
    out = _row_kernel("mix_resnorm", _fwd_of(_f_resnorm(1.0)), [(h, D, 0), (m, D, 0)], [p["mix_post_g"]], [(D, f32)])[0]
    return out, dict(h=h, u=u, proj=proj, act=act, dt=dt, adt=adt, states=states, y_ssd=y_ssd, nrm=nrm, y_a=y_a, s5st=s5st,
                     y5=y5, gel=gel, vg=vg, glu=glu, y_b=y_b, merged=merged, m=m)


def _s5_in(proj, c_u5, bsb, d):
    T = proj.shape[0]
    nsb = d["NSB"]
    ka, nw = S5_SUPERBLOCK, bsb.shape[1]
    off = c_u5 // ka
    assert c_u5 % ka == 0
    bt = _pick(T, 512)

    def body(a_ref, w_ref, o_ref):
        o_ref[...] = _dot(a_ref[...].astype(bf16), w_ref[...].astype(bf16))

    return pl.pallas_call(
        body, name="s5_mm_bu", grid=(nsb, T // bt),
        in_specs=[pl.BlockSpec((bt, ka), lambda j, i: (i, off + j)), pl.BlockSpec((ka, nw), lambda j, i: (j, 0))],
        out_specs=pl.BlockSpec((bt, nw), lambda j, i: (i, j)), out_shape=jax.ShapeDtypeStruct((T, nsb * nw), f32),
        compiler_params=_params(("parallel", "parallel")),
    )(proj, bsb)


def _s5_dbsb(proj, c_u5, a, d):
    T = proj.shape[0]
    nsb = d["NSB"]
    ka, nw = S5_SUPERBLOCK, a.shape[1] // nsb
    off = c_u5 // ka
    bt = _pick(T, 512)

    def body(u_ref, a_ref, o_ref):
        pr = _dot(u_ref[...].astype(bf16), a_ref[...].astype(bf16), "tn")
        k = pl.program_id(1)

        @pl.when(k == 0)
        def _():
            o_ref[...] = pr

        @pl.when(k > 0)
        def _():
            o_ref[...] += pr

    return pl.pallas_call(
        body, name="s5_mm_dbsb", grid=(nsb, T // bt),
        in_specs=[pl.BlockSpec((bt, ka), lambda j, k: (k, off + j)), pl.BlockSpec((bt, nw), lambda j, k: (k, j))],
        out_specs=pl.BlockSpec((ka, nw), lambda j, k: (j, 0)), out_shape=jax.ShapeDtypeStruct((nsb * ka, nw), f32),
        compiler_params=_params(("parallel", "arbitrary")),
    )(proj, a)


def _mixer_bwd(dh_out, s, p, d):
    D, DI, CD, G, N, S5W = d["D"], d["DI"], d["CD"], d["G"], d["N"], d["S5W"]
    gl = G * LANES
    gn = G * N
    c_u5, c_ga, c_gb, c_dt = DI + CD, DI + CD + S5W, DI + CD + S5W + D, DI + CD + S5W + 2 * D
    proj = s["proj"]
    g = {}
    dm, g["mix_post_g"] = _row_kernel("mix_resnorm_bwd", _vjp_of(_f_post(1.0), 1, 1, [0]), [(s["m"], D, 0), (dh_out, D, 0)],
                                      [p["mix_post_g"]], [(D, bf16)], [p["mix_post_g"].shape])
    g["w_out"] = _mm(s["merged"], dm, "tn", f32, "mix_mm_dwout")
    dmerged = _mm(dm, p["w_out"], "nt", f32, "mix_mm_dmerged")
    dga, dya, dgb, dyb = _row_kernel(
        "mix_merge_bwd", _vjp_of(_f_merge, 4, 1, [0, 1, 2, 3]),
        [(proj, D, c_ga // D), (s["y_a"], D, 0), (proj, D, c_gb // D), (s["y_b"], D, 0), (dmerged, D, 0)], [],
        [(D, bf16), (D, bf16), (D, bf16), (D, bf16)])
    g["w_a"] = _mm(s["nrm"], dya, "tn", f32, "mix_mm_dwa")
    dnrm = _mm(dya, p["w_a"], "nt", f32, "mix_mm_dnrm")
    dy_ssd, dz, g["norm_g"] = _row_kernel(
        "ssd_post_bwd", _vjp_of(_f_ssdpost(G), 2, 1, [0, 1]), [(s["y_ssd"], DI, 0), (proj, DI, 0), (dnrm, DI, 0)],
        [p["norm_g"]], [(DI, f32), (DI, bf16)], [p["norm_g"].shape])
    dxs, d_b, d_c, ddt, dadt, dd = _ssd_bwd(s["act"], s["dt"], s["adt"], p["d_skip"], s["states"], dy_ssd,
                                            d["HPG"], d["P"], N, "ssd_scan_bwd")
    g["d_skip"] = dd.reshape(G, 8, LANES)[:, 0, :].reshape(1, gl)
    ddt_raw, g["dt_bias"], g["a_log"] = _row_kernel(
        "ssd_dt_bwd", _vjp_of(_f_dt, 1, 2, [0]), [(proj, gl, c_dt // gl), (ddt, gl, 0), (dadt, gl, 0)],
        [p["dt_bias"], p["a_log"]], [(gl, bf16)], [p["dt_bias"].shape, p["a_log"].shape])
    cw, cb = p["conv_w"], p["conv_b"]
    dxc_x, dw_x, db_x = _conv_bwd(proj, DI, cw[:, :DI], cb[:, :DI], dxs, "ssd_conv_bwd_x")
    dxc_b, dw_b, db_b = _conv_bwd(proj, 2 * DI, cw[:, DI:DI + gn], cb[:, DI:DI + gn], d_b, "ssd_conv_bwd_b")
    dxc_c, dw_c, db_c = _conv_bwd(proj, 2 * DI + gn, cw[:, DI + gn:], cb[:, DI + gn:], d_c, "ssd_conv_bwd_c")
    g["conv_w"] = jnp.concatenate([dw_x, dw_b, dw_c], axis=1)
    g["conv_b"] = jnp.concatenate([db_x, db_b, db_c], axis=1)
    g["w_b"] = _mm(s["glu"], dyb, "tn", f32, "mix_mm_dwb")
    dglu = _mm(dyb, p["w_b"], "nt", f32, "mix_mm_dglu")
    dvg = _row_kernel("s5_glu_bwd", _glu_bwd, [(s["vg"], s["vg"].shape[1], 0), (dglu, S5W, 0)], [], [(s["vg"].shape[1], bf16)])[0]
    g["w_glu"] = _mm(dvg, s["gel"], "tn", f32, "mix_mm_dwglu")
    dgel = _mm(dvg, p["w_glu"], "nn", f32, "mix_mm_dgel")
    dy5, du5a, g["s5_d"] = _row_kernel(
        "s5_post_bwd", _vjp_of(_f_s5post, 2, 1, [0, 1]), [(s["y5"], S5W, 0), (proj, S5W, c_u5 // S5W), (dgel, S5W, 0)],
        [p["s5_d"]], [(S5W, bf16), (S5W, f32)], [p["s5_d"].shape])
    g["csb"] = _bdmm(s["s5st"], dy5, "tn", d["NSB"], f32, "s5_mm_dcsb")
    gst = _bdmm(dy5, p["csb"], "nt", d["NSB"], f32, "s5_mm_gst")
    a, g["lam_re_rows"], g["lam_im_rows"] = _s5_scan_bwd(gst, s["s5st"], p["lam_re_rows"], p["lam_im_rows"], d["NSB"], "s5_scan_bwd")
    g["bsb"] = _s5_dbsb(proj, c_u5, a, d)
    du5b = _bdmm(a, p["bsb"], "nt", d["NSB"], f32, "s5_mm_du5")
    du5 = _row_kernel("s5_du5", _add_fn, [(du5a, S5W, 0), (du5b, S5W, 0)], [], [(S5W, bf16)])[0]
    dproj = jnp.concatenate([dz, dxc_x, dxc_b, dxc_c, du5, dga, dgb, ddt_raw], axis=1)
    g["w_in"] = _mm(dproj, s["u"], "tn", f32, "mix_mm_dwin")
    du = _mm(dproj, p["w_in"], "nn", f32, "mix_mm_du", bk_t=2176)
    dh, g["mix_pre_g"] = _row_kernel("mix_norm_bwd", _vjp_of(_f_norm, 1, 1, [0], 1), [(s["h"], D, 0), (du, D, 0), (dh_out, D, 0)],
                                     [p["mix_pre_g"]], [(D, f32)], [p["mix_pre_g"].shape])
    return dh, g


def _layer_params(l, w, wf, conv_w_full, d):
    r2 = lambda v: v[l].reshape(1, -1)
    p = {}
    for n in ["ffn1_pre_g", "ffn1_post_g", "mix_pre_g", "mix_post_g", "ffn2_pre_g", "ffn2_post_g", "s5_d"]:
        p[n] = r2(w[n])
    p["wgu1"] = jnp.concatenate([wf["ffn1_w_gate"][l], wf["ffn1_w_up"][l]], axis=0)
    p["wd1"] = wf["ffn1_w_down"][l]
    p["wgu2"] = jnp.concatenate([wf["ffn2_w_gate"][l], wf["ffn2_w_up"][l]], axis=0)
    p["wd2"] = wf["ffn2_w_down"][l]
    p["w_in"] = _w_in_perm(wf["w_in"][l], d)
    p["w_a"], p["w_glu"], p["w_b"], p["w_out"] = wf["w_branch_a"][l], wf["s5_w_glu"][l], wf["w_branch_b"][l], wf["w_out"][l]
    p["conv_w"] = conv_w_full[l]
    p["conv_b"] = r2(w["ssd_conv_b"])
    p["dt_bias"] = _head_pad(r2(w["ssd_dt_bias"]), d)
    p["a_log"] = _head_pad(r2(w["ssd_a_log"]), d)
    p["d_skip"] = _head_pad(r2(w["ssd_d"]), d)
    p["norm_g"] = r2(w["ssd_norm_g"])
    g5, n5, c5 = d["S5G"], d["S5N"], d["S5C"]
    expand = jnp.repeat(jnp.eye(n5, dtype=f32), c5, axis=1)
    prep_in = [w["s5_lambda_re"][l], w["s5_lambda_im"][l], w["s5_log_step"][l].reshape(g5, 1),
               w["s5_b_re"][l].reshape(g5, n5 * c5), w["s5_b_im"][l].reshape(g5, n5 * c5), expand]
    lbr, lbi, bbr, bbi = _s5_prep(prep_in, "s5_prep")
    p["s5_prep_in"] = prep_in
    p["lam_re_rows"], p["lam_im_rows"] = _s5_lam_rows(lbr, d), _s5_lam_rows(lbi, d)
    to_cn = lambda v: v.reshape(g5, n5, c5).transpose(0, 2, 1)
    p["bsb"] = jnp.concatenate([_s5_block_diag(to_cn(bbr), d), _s5_block_diag(to_cn(bbi), d)], axis=1).astype(bf16)
    c_re, c_im = w["s5_c_re"][l].transpose(0, 2, 1), w["s5_c_im"][l].transpose(0, 2, 1)
    nsb = d["NSB"]
    csb = jnp.stack([_s5_block_diag(c_re, d).reshape(nsb, -1, S5_SUPERBLOCK),
                     _s5_block_diag(-c_im, d).reshape(nsb, -1, S5_SUPERBLOCK)], axis=1)
    p["csb"] = csb.reshape(-1, S5_SUPERBLOCK).astype(bf16)
    return p


def _s5_param_grads(g, p, d, l):
    g5, n5, c5, nsb, gsb = d["S5G"], d["S5N"], d["S5C"], d["NSB"], d["GSB"]
    wst = gsb * n5
    dbsb = g["bsb"]
    from_cn = lambda v: v.transpose(0, 2, 1).reshape(g5, n5 * c5)
    dbbr = from_cn(_s5_diag_blocks(dbsb[:, :wst], d, c5, n5))
    dbbi = from_cn(_s5_diag_blocks(dbsb[:, wst:], d, c5, n5))
    rows = lambda v: v.reshape(nsb, 8, wst)[:, 0, :].reshape(g5, n5)
    cots = [rows(g["lam_re_rows"]), rows(g["lam_im_rows"]), dbbr, dbbi]
    dlr, dli, dls, dbr, dbi = _s5_prep_bwd(p["s5_prep_in"], cots, "s5_prep_bwd")
    dcsb = g["csb"].reshape(nsb, 2, wst, S5_SUPERBLOCK)
    dcr = _s5_diag_blocks(dcsb[:, 0].reshape(-1, S5_SUPERBLOCK), d, n5, c5).transpose(0, 2, 1)
    dci = -_s5_diag_blocks(dcsb[:, 1].reshape(-1, S5_SUPERBLOCK), d, n5, c5).transpose(0, 2, 1)
    return dict(s5_lambda_re=dlr, s5_lambda_im=dli, s5_log_step=dls.reshape(g5), s5_b_re=dbr.reshape(g5, n5, c5),
                s5_b_im=dbi.reshape(g5, n5, c5), s5_c_re=dcr, s5_c_im=dci)


def kernel(x, ffn1_pre_g, ffn1_post_g, ffn1_w_gate, ffn1_w_up, ffn1_w_down, mix_pre_g, mix_post_g, w_in, ssd_conv_w, ssd_conv_b, ssd_dt_bias, ssd_a_log, ssd_d, ssd_norm_g, w_branch_a, s5_lambda_re, s5_lambda_im, s5_b_re, s5_b_im, s5_c_re, s5_c_im, s5_log_step, s5_d, s5_w_glu, w_branch_b, w_out, ffn2_pre_g, ffn2_post_g, ffn2_w_gate, ffn2_w_up, ffn2_w_down, loss_target, m_ffn1_pre_g, m_ffn1_post_g, m_ffn1_w_gate, m_ffn1_w_up, m_ffn1_w_down, m_mix_pre_g, m_mix_post_g, m_w_in, m_ssd_conv_w, m_ssd_conv_b, m_ssd_dt_bias, m_ssd_a_log, m_ssd_d, m_ssd_norm_g, m_w_branch_a, m_s5_lambda_re, m_s5_lambda_im, m_s5_b_re, m_s5_b_im, m_s5_c_re, m_s5_c_im, m_s5_log_step, m_s5_d, m_s5_w_glu, m_w_branch_b, m_w_out, m_ffn2_pre_g, m_ffn2_post_g, m_ffn2_w_gate, m_ffn2_w_up, m_ffn2_w_down, v_ffn1_pre_g, v_ffn1_post_g, v_ffn1_w_gate, v_ffn1_w_up, v_ffn1_w_down, v_mix_pre_g, v_mix_post_g, v_w_in, v_ssd_conv_w, v_ssd_conv_b, v_ssd_dt_bias, v_ssd_a_log, v_ssd_d, v_ssd_norm_g, v_w_branch_a, v_s5_lambda_re, v_s5_lambda_im, v_s5_b_re, v_s5_b_im, v_s5_c_re, v_s5_c_im, v_s5_log_step, v_s5_d, v_s5_w_glu, v_w_branch_b, v_w_out, v_ffn2_pre_g, v_ffn2_post_g, v_ffn2_w_gate, v_ffn2_w_up, v_ffn2_w_down):
    given = dict(locals())
    for n in COL_SHARDED:
        for prefix in ("", "m_", "v_"):
            given[prefix + n] = given[prefix + n].transpose(0, 2, 1)
    w = {n: given[n] for n in WEIGHTS}
    mom = {n: given["m_" + n] for n in WEIGHTS}
    var = {n: given["v_" + n] for n in WEIGHTS}
    d = _dims(w, x)
    n_layers = w["ffn1_pre_g"].shape[0]
    T, D = d["T"], d["D"]

    gathered = _chip_all_gather(_pack_weights(w, w["ssd_conv_w"]), "gather_weights")
    wf, conv_w_full = _unpack_weights(gathered, w, w["ssd_conv_w"])
    layers = [_layer_params(l, w, wf, conv_w_full, d) for l in range(n_layers)]

    h = x.reshape(T, D)
    saved = []
    for p in layers:
        h, s1 = _ffn_fwd(h, p["ffn1_pre_g"], p["ffn1_post_g"], p["wgu1"], p["wd1"], "ffn1")
        h, sm = _mixer_fwd(h, p, d)
        h, s2 = _ffn_fwd(h, p["ffn2_pre_g"], p["ffn2_post_g"], p["wgu2"], p["wd2"], "ffn2")
        saved.append((s1, sm, s2))
    dh, loss_part = _row_kernel("loss", _loss_fn, [(h, D, 0), (loss_target.reshape(T, D), D, 0)], [], [(D, f32)], [(8, LANES)])
    loss = lax.psum(loss_part[0, 0], ("x", "y", "c"))

    lg = [None] * n_layers
    for l in reversed(range(n_layers)):
        p = layers[l]
        s1, sm, s2 = saved[l]
        dh, g2 = _ffn_bwd(dh, s2, p["ffn2_pre_g"], p["ffn2_post_g"], p["wgu2"], p["wd2"], "ffn2")
        dh, gm = _mixer_bwd(dh, sm, p, d)
        dh, g1 = _ffn_bwd(dh, s1, p["ffn1_pre_g"], p["ffn1_post_g"], p["wgu1"], p["wd1"], "ffn1")
        H = p["wd1"].shape[0]
        gl = dict(ffn1_pre_g=g1["pre_g"], ffn1_post_g=g1["post_g"], ffn1_w_gate=g1["wgu"][:H], ffn1_w_up=g1["wgu"][H:],
                  ffn1_w_down=g1["wd"], ffn2_pre_g=g2["pre_g"], ffn2_post_g=g2["post_g"], ffn2_w_gate=g2["wgu"][:H],
                  ffn2_w_up=g2["wgu"][H:], ffn2_w_down=g2["wd"], mix_pre_g=gm["mix_pre_g"], mix_post_g=gm["mix_post_g"],
                  w_in=_w_in_unperm(gm["w_in"], d), ssd_conv_w=gm["conv_w"], ssd_conv_b=gm["conv_b"],
                  ssd_dt_bias=_head_unpad(gm["dt_bias"], d), ssd_a_log=_head_unpad(gm["a_log"], d),
                  ssd_d=_head_unpad(gm["d_skip"], d), ssd_norm_g=gm["norm_g"], w_branch_a=gm["w_a"], s5_d=gm["s5_d"],
                  s5_w_glu=gm["w_glu"], w_branch_b=gm["w_b"], w_out=gm["w_out"])
        gl.update(_s5_param_grads(gm, p, d, l))
        lg[l] = gl
    grad_x = dh.reshape(x.shape)
    full_shape = {n: w[n].shape[1:] for n in WEIGHTS}
    for n in BIG + ["ssd_conv_w"]:
        ax = 0 if n in BIG else 1
        full_shape[n] = tuple(s * N_CHIPS if i == ax else s for i, s in enumerate(w[n].shape[1:]))
    part = {n: jnp.stack([lg[l][n].reshape(full_shape[n]) for l in range(n_layers)]) for n in WEIGHTS}

    packed = _pack_big_grads(part)
    my_core = lax.axis_index("c").astype(jnp.int32).reshape(1)
    chip_part = _add_my_half(packed, _core_send_other_half(packed, "exchange_core_halves"), my_core, "sum_core_halves")
    my_sum = _sum_slots(_chip_scatter(chip_part, "scatter_grads"), "sum_chip_parts")
    grads = _unpack_big_grads(_core_join_halves(my_sum, "join_core_halves"), w)
    small_names = SMALL + ["ssd_conv_w"]
    everyone = _device_all_gather(_pack_small(part, small_names), "gather_small_grads")
    small = _unpack_small(_sum_slots(everyone, "sum_small_grads"), part, small_names)
    k_me = 2 * lax.axis_index("x") + lax.axis_index("y")
    cw = w["ssd_conv_w"].shape[-1]
    small["ssd_conv_w"] = lax.dynamic_slice_in_dim(small["ssd_conv_w"], k_me * cw, cw, axis=2)
    grads.update(small)

    delta, new_m, new_v = {}, {}, {}
    for n in BIG + ["ssd_conv_w"]:
        delta[n], new_m[n], new_v[n] = _adamw(w[n], grads[n], mom[n], var[n], "adamw_" + n)
    packs = [_pack_small(t, SMALL) for t in (w, grads, mom, var)]
    sd, sm_, sv = _adamw(*packs, "adamw_small")
    delta.update(_unpack_small(sd, w, SMALL))
    new_m.update(_unpack_small(sm_, w, SMALL))
    new_v.update(_unpack_small(sv, w, SMALL))
    for n in COL_SHARDED:
        for out in (grads, delta, new_m, new_v):
            out[n] = out[n].transpose(0, 2, 1)
    return (loss, grad_x, *[grads[n] for n in WEIGHTS], *[delta[n] for n in WEIGHTS],
            *[new_m[n] for n in WEIGHTS], *[new_v[n] for n in WEIGHTS])
```

```python
import functools

import numpy as np
import jax
import jax.numpy as jnp
from jax import lax
from jax.experimental import pallas as pl
from jax.experimental.pallas import tpu as pltpu

f32, bf16 = jnp.float32, jnp.bfloat16

SSD_N_GROUPS = 4
SSD_CHUNK = 128
RMS_EPS = 1e-6
S5_MAX_REAL = -1e-4
S5_SUPERBLOCK = 256
ADAM_LR, ADAM_B1, ADAM_B2, ADAM_EPS, ADAM_WD, ADAM_STEP = 0.001, 0.9, 0.999, 1e-08, 0.01, 10

LANES = 128
PACK_COLS = 1024
D2D_STREAMS = 16
PACK_ROW_MULT = 2 * D2D_STREAMS * 16
VMEM_LIMIT_BYTES = 48 * 1024 * 1024
N_CHIPS, N_CORES, N_DEV = 4, 2, 8
MESH = pl.DeviceIdType.MESH

BIG = ["ffn1_w_gate", "ffn1_w_up", "ffn1_w_down", "w_in", "w_branch_a", "s5_w_glu", "w_branch_b", "w_out",
       "ffn2_w_gate", "ffn2_w_up", "ffn2_w_down"]
COL_SHARDED = ["ffn1_w_gate", "ffn1_w_up", "w_in", "s5_w_glu", "ffn2_w_gate", "ffn2_w_up"]
SMALL = ["ffn1_pre_g", "ffn1_post_g", "mix_pre_g", "mix_post_g", "ssd_conv_b", "ssd_dt_bias", "ssd_a_log", "ssd_d",
         "ssd_norm_g", "s5_lambda_re", "s5_lambda_im", "s5_b_re", "s5_b_im", "s5_c_re", "s5_c_im", "s5_log_step",
         "s5_d", "ffn2_pre_g", "ffn2_post_g"]
WEIGHTS = ["ffn1_pre_g", "ffn1_post_g", "ffn1_w_gate", "ffn1_w_up", "ffn1_w_down", "mix_pre_g", "mix_post_g", "w_in",
           "ssd_conv_w", "ssd_conv_b", "ssd_dt_bias", "ssd_a_log", "ssd_d", "ssd_norm_g", "w_branch_a", "s5_lambda_re",
           "s5_lambda_im", "s5_b_re", "s5_b_im", "s5_c_re", "s5_c_im", "s5_log_step", "s5_d", "s5_w_glu", "w_branch_b",
           "w_out", "ffn2_pre_g", "ffn2_post_g", "ffn2_w_gate", "ffn2_w_up", "ffn2_w_down"]


def _params(sem=None):
    return pltpu.CompilerParams(dimension_semantics=sem, vmem_limit_bytes=VMEM_LIMIT_BYTES)


def _pick(n, target, mult=LANES):
    best = None
    for d in range(mult, min(n, target) + 1, mult):
        if n % d == 0:
            best = d
    return best if best is not None else n


_DIMS = {"nn": (((1,), (0,)), ((), ())), "nt": (((1,), (1,)), ((), ())), "tn": (((0,), (0,)), ((), ()))}


def _mm(a, b, mode, out_dtype, name, bm_t=1024, bn_t=1024, bk_t=2816):
    if mode == "nn":
        (M, K), (K2, N) = a.shape, b.shape
    elif mode == "nt":
        (M, K), (N, K2) = a.shape, b.shape
    else:
        (K, M), (K2, N) = a.shape, b.shape
    assert K == K2, (name, a.shape, b.shape)
    bm, bn, bk = _pick(M, bm_t), _pick(N, bn_t), _pick(K, bk_t)
    nk = K // bk
    dn = _DIMS[mode]

    def body(a_ref, b_ref, o_ref, *scratch):
        p = lax.dot_general(a_ref[...].astype(bf16), b_ref[...].astype(bf16), dn, preferred_element_type=f32)
        if nk == 1:
            o_ref[...] = p.astype(o_ref.dtype)
        else:
            acc = scratch[0]
            k = pl.program_id(2)

            @pl.when(k == 0)
            def _():
                acc[...] = p

            @pl.when(k > 0)
            def _():
                acc[...] += p

            @pl.when(k == nk - 1)
            def _():
                o_ref[...] = acc[...].astype(o_ref.dtype)

    if mode == "tn":
        a_spec = pl.BlockSpec((bk, bm), lambda i, j, k: (k, i))
    else:
        a_spec = pl.BlockSpec((bm, bk), lambda i, j, k: (i, k))
    if mode == "nt":
        b_spec = pl.BlockSpec((bn, bk), lambda i, j, k: (j, k))
    else:
        b_spec = pl.BlockSpec((bk, bn), lambda i, j, k: (k, j))
    return pl.pallas_call(
        body, name=name, grid=(M // bm, N // bn, nk), in_specs=[a_spec, b_spec],
        out_specs=pl.BlockSpec((bm, bn), lambda i, j, k: (i, j)), out_shape=jax.ShapeDtypeStruct((M, N), out_dtype),
        scratch_shapes=[pltpu.VMEM((bm, bn), f32)] if nk > 1 else [],
        compiler_params=_params(("parallel", "parallel", "arbitrary")),
    )(a, b)


def _bdmm(a, w, mode, nb, out_dtype, name, bt_t=512):
    if mode == "tn":
        T = a.shape[0]
        ka, nw = a.shape[1] // nb, w.shape[1] // nb
        bt = _pick(T, bt_t)
        nt = T // bt

        def body_tn(a_ref, b_ref, o_ref):
            p = lax.dot_general(a_ref[...].astype(bf16), b_ref[...].astype(bf16), _DIMS["tn"], preferred_element_type=f32)
            k = pl.program_id(1)

            @pl.when(k == 0)
            def _():
                o_ref[...] = p

            @pl.when(k > 0)
            def _():
                o_ref[...] += p

        return pl.pallas_call(
            body_tn, name=name, grid=(nb, nt),
            in_specs=[pl.BlockSpec((bt, ka), lambda j, k: (k, j)), pl.BlockSpec((bt, nw), lambda j, k: (k, j))],
            out_specs=pl.BlockSpec((ka, nw), lambda j, k: (j, 0)), out_shape=jax.ShapeDtypeStruct((nb * ka, nw), f32),
            compiler_params=_params(("parallel", "arbitrary")),
        )(a, w)
    T = a.shape[0]
    ka, nw = w.shape[0] // nb, w.shape[1]
    bt = _pick(T, bt_t)
    kin, kout = (ka, nw) if mode == "nn" else (nw, ka)
    dn = _DIMS[mode]

    def body(a_ref, w_ref, o_ref):
        o_ref[...] = lax.dot_general(a_ref[...].astype(bf16), w_ref[...].astype(bf16), dn,
                                     preferred_element_type=f32).astype(o_ref.dtype)

    return pl.pallas_call(
        body, name=name, grid=(nb, T // bt),
        in_specs=[pl.BlockSpec((bt, kin), lambda j, i: (i, j)), pl.BlockSpec((ka, nw), lambda j, i: (j, 0))],
        out_specs=pl.BlockSpec((bt, kout), lambda j, i: (i, j)), out_shape=jax.ShapeDtypeStruct((T, nb * kout), out_dtype),
        compiler_params=_params(("parallel", "parallel")),
    )(a, w)


def _row_index(i, cb):
    return (i, cb)


def _row_kernel(name, fn, rows, pars, row_outs, par_outs=(), block_rows=256):
    T = rows[0][0].shape[0]
    R = min(block_rows, T)
    assert T % R == 0
    nr, npar, nro = len(rows), len(pars), len(row_outs)

    def body(*refs):
        rv = [r[...] for r in refs[:nr]]
        pv = [r[...] for r in refs[nr:nr + npar]]
        ro, po = fn(rv, pv)
        for ref, v in zip(refs[nr + npar:nr + npar + nro], ro):
            ref[...] = v.astype(ref.dtype)
        if par_outs:
            i = pl.program_id(0)
            prefs = refs[nr + npar + nro:]

            @pl.when(i == 0)
            def _():
                for ref, v in zip(prefs, po):
                    ref[...] = v.astype(f32)

            @pl.when(i > 0)
            def _():
                for ref, v in zip(prefs, po):
                    ref[...] += v.astype(f32)

    in_specs = [pl.BlockSpec((R, nc), functools.partial(_row_index, cb=cb)) for (_, nc, cb) in rows]
    in_specs += [pl.BlockSpec(p.shape, lambda i: (0, 0)) for p in pars]
    out_specs = [pl.BlockSpec((R, nc), lambda i: (i, 0)) for (nc, _) in row_outs]
    out_specs += [pl.BlockSpec(s, lambda i: (0, 0)) for s in par_outs]
    out_shape = [jax.ShapeDtypeStruct((T, nc), dt) for (nc, dt) in row_outs]
    out_shape += [jax.ShapeDtypeStruct(s, f32) for s in par_outs]
    outs = pl.pallas_call(
        body, name=name, grid=(T // R,), in_specs=in_specs, out_specs=out_specs, out_shape=out_shape,
        compiler_params=_params(("arbitrary",) if par_outs else ("parallel",)),
    )(*[r[0] for r in rows], *pars)
    return list(outs)


def _fwd_of(f):
    def fn(rv, pv):
        return f([v.astype(f32) for v in rv], [v.astype(f32) for v in pv]), []
    return fn


def _vjp_of(f, n_x, n_cot, grad_idx, n_add=0):
    def fn(rv, pv):
        xs = [v.astype(f32) for v in rv[:n_x]]
        cots = [v.astype(f32) for v in rv[n_x:n_x + n_cot]]
        adds = rv[n_x + n_cot:n_x + n_cot + n_add]
        ps = [v.astype(f32) for v in pv]
        _, vjp = jax.vjp(lambda *a: f(list(a[:n_x]), list(a[n_x:])), *xs, *ps)
        g = vjp(cots)
        row_g = [g[i] for i in grad_idx]
        for k, a in enumerate(adds):
            row_g[k] = row_g[k] + a.astype(f32)
        return row_g, list(g[n_x:])
    return fn


def _rms(x, g):
    return x * lax.rsqrt(jnp.mean(x * x, axis=-1, keepdims=True) + RMS_EPS) * g


def _f_norm(xs, ps):
    return [_rms(xs[0], ps[0])]


def _f_post(scale):
    def f(xs, ps):
        return [scale * _rms(xs[0], ps[0])]
    return f


def _f_resnorm(scale):
    def f(xs, ps):
        return [xs[0] + scale * _rms(xs[1], ps[0])]
    return f


def _f_dt(xs, ps):
    dt = jax.nn.softplus(xs[0] + ps[0])
    return [dt, -jnp.exp(ps[1]) * dt]


def _f_ssdpost(n_groups):
    def f(xs, ps):
        y = xs[0] * jax.nn.silu(xs[1])
        width = y.shape[-1] // n_groups
        lane = lax.broadcasted_iota(jnp.int32, y.shape, 1)
        scale = jnp.zeros_like(y)
        for k in range(n_groups):
            m = ((lane >= k * width) & (lane < (k + 1) * width)).astype(f32)
            ms = jnp.sum(y * y * m, axis=-1, keepdims=True) / width
            scale = scale + lax.rsqrt(ms + RMS_EPS) * m
        return [y * scale * ps[0]]
    return f


def _f_s5post(xs, ps):
    return [jax.nn.gelu(xs[0] + ps[0] * xs[1])]


def _f_merge(xs, ps):
    return [jax.nn.sigmoid(xs[0]) * xs[1] + jax.nn.sigmoid(xs[2]) * xs[3]]


def _swiglu_fwd(rv, pv):
    ab = rv[0].astype(f32)
    h = ab.shape[1] // 2
    return [jax.nn.silu(ab[:, :h]) * ab[:, h:]], []


def _swiglu_bwd(rv, pv):
    ab, d = rv[0].astype(f32), rv[1].astype(f32)
    h = ab.shape[1] // 2
    a, b = ab[:, :h], ab[:, h:]
    s = jax.nn.sigmoid(a)
    return [jnp.concatenate([d * b * (s * (1.0 + a * (1.0 - s))), d * (a * s)], axis=1)], []


def _glu_fwd(rv, pv):
    vg = rv[0].astype(f32)
    h = vg.shape[1] // 2
    return [vg[:, :h] * jax.nn.sigmoid(vg[:, h:])], []


def _glu_bwd(rv, pv):
    vg, d = rv[0].astype(f32), rv[1].astype(f32)
    h = vg.shape[1] // 2
    s = jax.nn.sigmoid(vg[:, h:])
    return [jnp.concatenate([d * s, d * vg[:, :h] * s * (1.0 - s)], axis=1)], []


def _loss_fn(rv, pv):
    e = rv[0].astype(f32) - rv[1].astype(f32)
    per_tok = jnp.mean(e * e, axis=-1, keepdims=True)
    part = 0.5 * jnp.sum(per_tok, axis=0, keepdims=True)
    return [e / e.shape[-1]], [jnp.broadcast_to(part, (8, LANES))]


def _add_fn(rv, pv):
    return [rv[0].astype(f32) + rv[1].astype(f32)], []


def _adamw_fn(rv, pv):
    w, g, m, v = [x.astype(f32) for x in rv]
    m = ADAM_B1 * m + (1.0 - ADAM_B1) * g
    v = ADAM_B2 * v + (1.0 - ADAM_B2) * (g * g)
    m_hat = m / (1.0 - ADAM_B1 ** ADAM_STEP)
    v_hat = v / (1.0 - ADAM_B2 ** ADAM_STEP)
    return [-ADAM_LR * (m_hat / (jnp.sqrt(v_hat) + ADAM_EPS) + ADAM_WD * w), m, v], []


def _adamw(w, g, m, v, name):
    shape = w.shape
    cols = shape[-1] if (w.ndim >= 2 and shape[-1] >= LANES) else None
    if cols is None:
        n = int(np.prod(shape))
        cols = LANES if n % LANES == 0 else n
    n_rows = int(np.prod(shape)) // cols
    br, bc = _pick(n_rows, 256, 8), cols
    if br < 64 and cols % LANES == 0:
        br, bc = n_rows, LANES

    def body(w_ref, g_ref, m_ref, v_ref, d_ref, nm_ref, nv_ref):
        outs, _ = _adamw_fn([w_ref[...], g_ref[...], m_ref[...], v_ref[...]], [])
        d_ref[...], nm_ref[...], nv_ref[...] = outs

    spec = pl.BlockSpec((br, bc), lambda i, j: (i, j))
    outs = pl.pallas_call(
        body, name=name, grid=(n_rows // br, cols // bc), in_specs=[spec] * 4, out_specs=[spec] * 3,
        out_shape=[jax.ShapeDtypeStruct((n_rows, cols), f32)] * 3, compiler_params=_params(("parallel", "parallel")),
    )(*[t.reshape(n_rows, cols) for t in (w, g, m, v)])
    return [o.reshape(shape) for o in outs]


def _shift_down(x, s, row):
    if s == 0:
        return x
    return jnp.where(row >= s, pltpu.roll(x, s, 0), 0.0)


def _shift_up(x, s, row):
    if s == 0:
        return x
    n = x.shape[0]
    return jnp.where(row < n - s, pltpu.roll(x, n - s, 0), 0.0)


def _conv_pre(x, w, b, row):
    kw = w.shape[0]
    c = b
    for k in range(kw):
        c = c + w[k:k + 1, :] * _shift_down(x, kw - 1 - k, row)
    return c


def _conv_fwd(xsrc, col0, w, b, name, bc_t=512):
    T = xsrc.shape[0]
    kw, ncols = w.shape
    bc = _pick(ncols, bc_t)
    off = col0 // bc
    assert col0 % bc == 0

    def body(x_ref, w_ref, b_ref, o_ref):
        x = x_ref[...].astype(f32)
        row = lax.broadcasted_iota(jnp.int32, x.shape, 0)
        c = _conv_pre(x, w_ref[...], b_ref[...], row)
        o_ref[...] = c * jax.nn.sigmoid(c)

    return pl.pallas_call(
        body, name=name, grid=(ncols // bc,),
        in_specs=[pl.BlockSpec((T, bc), lambda j: (0, off + j)), pl.BlockSpec((kw, bc), lambda j: (0, j)),
                  pl.BlockSpec((1, bc), lambda j: (0, j))],
        out_specs=pl.BlockSpec((T, bc), lambda j: (0, j)), out_shape=jax.ShapeDtypeStruct((T, ncols), f32),
        compiler_params=_params(("parallel",)),
    )(xsrc, w, b)


def _conv_bwd(xsrc, col0, w, b, dact, name, bc_t=512):
    T = xsrc.shape[0]
    kw, ncols = w.shape
    bc = _pick(ncols, bc_t)
    off = col0 // bc
    assert col0 % bc == 0

    def body(x_ref, w_ref, b_ref, d_ref, dx_ref, dw_ref, db_ref):
        x = x_ref[...].astype(f32)
        w = w_ref[...]
        row = lax.broadcasted_iota(jnp.int32, x.shape, 0)
        c = _conv_pre(x, w, b_ref[...], row)
        s = jax.nn.sigmoid(c)
        dc = d_ref[...].astype(f32) * (s * (1.0 + c * (1.0 - s)))
        dx = jnp.zeros_like(x)
        dws = []
        for k in range(kw):
            dx = dx + w[k:k + 1, :] * _shift_up(dc, kw - 1 - k, row)
            dws.append(jnp.sum(dc * _shift_down(x, kw - 1 - k, row), axis=0, keepdims=True))
        dx_ref[...] = dx.astype(dx_ref.dtype)
        dw_ref[...] = jnp.concatenate(dws, axis=0)
        db_ref[...] = jnp.sum(dc, axis=0, keepdims=True)

    return pl.pallas_call(
        body, name=name, grid=(ncols // bc,),
        in_specs=[pl.BlockSpec((T, bc), lambda j: (0, off + j)), pl.BlockSpec((kw, bc), lambda j: (0, j)),
                  pl.BlockSpec((1, bc), lambda j: (0, j)), pl.BlockSpec((T, bc), lambda j: (0, j))],
        out_specs=[pl.BlockSpec((T, bc), lambda j: (0, j)), pl.BlockSpec((kw, bc), lambda j: (0, j)),
                   pl.BlockSpec((1, bc), lambda j: (0, j))],
        out_shape=[jax.ShapeDtypeStruct((T, ncols), bf16), jax.ShapeDtypeStruct((kw, ncols), f32),
                   jax.ShapeDtypeStruct((1, ncols), f32)],
        compiler_params=_params(("parallel",)),
    )(xsrc, w, b, dact)


_HI = lax.Precision.HIGHEST


def _dot(a, b, dims="nn", precision=None):
    return lax.dot_general(a, b, _DIMS[dims], preferred_element_type=f32, precision=precision)


def _ssd_common(x_ref, b_ref, c_ref, dt_ref, adt_ref, d_ref, hpg, p):
    q = b_ref.shape[0]
    hp = hpg * p
    bb, cb = b_ref[...].astype(bf16), c_ref[...].astype(bf16)
    r = lax.broadcasted_iota(jnp.int32, (q, q), 0)
    s = lax.broadcasted_iota(jnp.int32, (q, q), 1)
    tril = r >= s
    trilf = tril.astype(f32)
    eh = lax.broadcasted_iota(jnp.int32, (LANES, hp), 0)
    ec = lax.broadcasted_iota(jnp.int32, (LANES, hp), 1)
    expand = ((ec >= eh * p) & (ec < (eh + 1) * p)).astype(f32)
    adt = adt_ref[...]
    cum = _dot(trilf, adt, "nn", _HI)
    cum_t = _dot(adt, (r <= s).astype(f32), "tn", _HI)
    cum_e = _dot(cum, expand, "nn", _HI)
    dt_e = _dot(dt_ref[...], expand, "nn", _HI)
    d_e = _dot(jnp.broadcast_to(d_ref[...], (8, LANES)), expand, "nn", _HI)[0:1, :]
    gmat = _dot(cb, bb, "nt")
    x = x_ref[...]
    xdt = x * dt_e
    e_all = jnp.exp(cum_e)
    dec = jnp.exp(cum_e[q - 1:q, :] - cum_e)
    lms, ms = [], []
    for h in range(hpg):
        lm = jnp.exp(jnp.where(tril, cum[:, h:h + 1] - cum_t[h:h + 1, :], -1e30))
        lms.append(lm)
        ms.append(gmat * lm)
    et = [jnp.exp(cum[q - 1:q, h:h + 1]) for h in range(hpg)]
    return dict(bb=bb, cb=cb, trilf=trilf, expand=expand, cum=cum, x=x, xdt=xdt, dt_e=dt_e, d_e=d_e, e=e_all, dec=dec,
                lms=lms, ms=ms, et=et)


def _ssd_specs(q, hp, n, g_n, nc, rev):
    def cidx(c):
        return (nc - 1 - c) if rev else c
    x_spec = pl.BlockSpec((q, hp), lambda g, c: (cidx(c), g))
    boff = (g_n * hp) // n
    b_spec = pl.BlockSpec((q, n), lambda g, c: (cidx(c), boff + g))
    c_spec = pl.BlockSpec((q, n), lambda g, c: (cidx(c), boff + g_n + g))
    dt_spec = pl.BlockSpec((q, LANES), lambda g, c: (cidx(c), g))
    d_spec = pl.BlockSpec((1, LANES), lambda g, c: (0, g))
    st_spec = pl.BlockSpec((1, 1, hp, n), lambda g, c: (cidx(c), g, 0, 0))
    return x_spec, b_spec, c_spec, dt_spec, d_spec, st_spec


def _ssd_fwd(act, dt, adt, dpad, hpg, p, n, name):
    T = act.shape[0]
    g_n, q = SSD_N_GROUPS, SSD_CHUNK
    nc, hp = T // q, hpg * p
    x_spec, b_spec, c_spec, dt_spec, d_spec, st_spec = _ssd_specs(q, hp, n, g_n, nc, False)

    def body(x_ref, b_ref, c_ref, dt_ref, adt_ref, d_ref, y_ref, st_ref, s_scr):
        @pl.when(pl.program_id(1) == 0)
        def _():
            s_scr[...] = jnp.zeros_like(s_scr)

        k = _ssd_common(x_ref, b_ref, c_ref, dt_ref, adt_ref, d_ref, hpg, p)
        s0 = s_scr[...]
        st_ref[0, 0] = s0
        xdtb = k["xdt"].astype(bf16)
        ydiag = [_dot(k["ms"][h].astype(bf16), xdtb[:, h * p:(h + 1) * p]) for h in range(hpg)]
        z = _dot(k["cb"], s0.astype(bf16), "nt")
        y_ref[...] = jnp.concatenate(ydiag, axis=1) + k["e"] * z + k["d_e"] * k["x"]
        upd = _dot((k["xdt"] * k["dec"]).astype(bf16), k["bb"], "tn")
        for h in range(hpg):
            s_scr[h * p:(h + 1) * p, :] = k["et"][h] * s0[h * p:(h + 1) * p, :] + upd[h * p:(h + 1) * p, :]

    return pl.pallas_call(
        body, name=name, grid=(g_n, nc),
        in_specs=[x_spec, b_spec, c_spec, dt_spec, dt_spec, d_spec],
        out_specs=[pl.BlockSpec((q, hp), lambda g, c: (c, g)), st_spec],
        out_shape=[jax.ShapeDtypeStruct((T, g_n * hp), f32), jax.ShapeDtypeStruct((nc, g_n, hp, n), f32)],
        scratch_shapes=[pltpu.VMEM((hp, n), f32)],
        compiler_params=_params(("parallel", "arbitrary")),
    )(act, act, act, dt, adt, dpad)


def _ssd_bwd(act, dt, adt, dpad, states, dy, hpg, p, n, name):
    T = act.shape[0]
    g_n, q = SSD_N_GROUPS, SSD_CHUNK
    nc, hp = T // q, hpg * p
    x_spec, b_spec, c_spec, dt_spec, d_spec, st_spec = _ssd_specs(q, hp, n, g_n, nc, True)

    def body(x_ref, b_ref, c_ref, dt_ref, adt_ref, d_ref, st_ref, dy_ref,
             dx_ref, db_ref, dc_ref, ddt_ref, dadt_ref, dd_ref, ds_scr):
        first = pl.program_id(1) == 0

        @pl.when(first)
        def _():
            ds_scr[...] = jnp.zeros_like(ds_scr)

        k = _ssd_common(x_ref, b_ref, c_ref, dt_ref, adt_ref, d_ref, hpg, p)
        bb, cb, expand, x, xdt, dec = k["bb"], k["cb"], k["expand"], k["x"], k["xdt"], k["dec"]
        heads = lambda t: _dot(t, expand, "nt", _HI)
        s0 = st_ref[0, 0]
        ds1 = ds_scr[...]
        s0b, ds1b = s0.astype(bf16), ds1.astype(bf16)
        dy = dy_ref[...]
        dyb, xdtb = dy.astype(bf16), xdt.astype(bf16)
        lane = lax.broadcasted_iota(jnp.int32, (1, LANES), 1)
        dg = jnp.zeros((q, q), f32)
        w_rows = jnp.zeros((q, LANES), f32)
        w_cols, dxdt_parts = [], []
        for h in range(hpg):
            hs = slice(h * p, (h + 1) * p)
            dm = _dot(dyb[:, hs], xdtb[:, hs], "nt")
            dg = dg + dm * k["lms"][h]
            wm = dm * k["ms"][h]
            w_rows = w_rows + jnp.sum(wm, axis=1, keepdims=True) * (lane == h).astype(f32)
            w_cols.append(jnp.sum(wm, axis=0, keepdims=True))
            dxdt_parts.append(_dot(k["ms"][h].astype(bf16), dyb[:, hs], "tn"))
        dxdt_diag = jnp.concatenate(dxdt_parts, axis=1)
        w_cols = jnp.concatenate(w_cols + [jnp.zeros((LANES - hpg, q), f32)], axis=0).T
        dgb = dg.astype(bf16)
        z = _dot(cb, s0b, "nt")
        dz = dy * k["e"]
        dzb = dz.astype(bf16)
        dxd = _dot(bb, ds1b, "nt")
        ddec = dxd * xdt * dec
        db_ref[...] = _dot(dgb, cb, "tn") + _dot((xdt * dec).astype(bf16), ds1b)
        dc_ref[...] = _dot(dgb, bb) + _dot(dzb, s0b)
        ds0 = _dot(dzb, cb, "tn")
        for h in range(hpg):
            hs = slice(h * p, (h + 1) * p)
            ds_scr[hs, :] = ds0[hs, :] + k["et"][h] * ds1[hs, :]
        dxdt = dxdt_diag + dxd * dec
        ddec_h = heads(ddec)
        dcum = w_rows - w_cols + heads(dz * z) - ddec_h
        et_row = jnp.exp(k["cum"][q - 1:q, :])
        dsum = _dot(jnp.ones((8, n), f32), _dot(expand, ds1 * s0, "nn", _HI), "nt", _HI)[0:1, :]
        dcl = dsum * et_row + jnp.sum(ddec_h, axis=0, keepdims=True)
        rowq = lax.broadcasted_iota(jnp.int32, (q, 1), 0)
        dcum = dcum + (rowq == q - 1).astype(f32) * dcl
        ddt_ref[...] = heads(dxdt * x)
        dadt_ref[...] = _dot(k["trilf"], dcum, "tn", _HI)
        dx_ref[...] = k["d_e"] * dy + dxdt * k["dt_e"]
        dd8 = heads(jnp.broadcast_to(jnp.sum(dy * x, axis=0, keepdims=True), (8, hp)))

        @pl.when(first)
        def _():
            dd_ref[...] = dd8

        @pl.when(jnp.logical_not(first))
        def _():
            dd_ref[...] += dd8

    rc = lambda g, c: (nc - 1 - c, g)
    return pl.pallas_call(
        body, name=name, grid=(g_n, nc),
        in_specs=[x_spec, b_spec, c_spec, dt_spec, dt_spec, d_spec, st_spec, pl.BlockSpec((q, hp), rc)],
        out_specs=[pl.BlockSpec((q, hp), rc), pl.BlockSpec((q, n), rc), pl.BlockSpec((q, n), rc),
                   pl.BlockSpec((q, LANES), rc), pl.BlockSpec((q, LANES), rc), pl.BlockSpec((8, LANES), lambda g, c: (g, 0))],
        out_shape=[jax.ShapeDtypeStruct((T, g_n * hp), f32), jax.ShapeDtypeStruct((T, g_n * n), f32),
                   jax.ShapeDtypeStruct((T, g_n * n), f32), jax.ShapeDtypeStruct((T, g_n * LANES), f32),
                   jax.ShapeDtypeStruct((T, g_n * LANES), f32), jax.ShapeDtypeStruct((g_n * 8, LANES), f32)],
        scratch_shapes=[pltpu.VMEM((hp, n), f32)],
        compiler_params=_params(("parallel", "arbitrary")),
    )(act, act, act, dt, adt, dpad, states, dy)


def _s5_scan_fwd(bu, lam_re, lam_im, nsb, name, tc_t=512):
    T = bu.shape[0]
    w2 = bu.shape[1] // nsb
    w = w2 // 2
    tc = _pick(T, tc_t, 8)

    def body(bu_ref, lr_ref, li_ref, st_ref, carry):
        @pl.when(pl.program_id(1) == 0)
        def _():
            carry[...] = jnp.zeros_like(carry)

        lr, li = lr_ref[0:1, :], li_ref[0:1, :]

        def step(t, s):
            sr, si = s
            row = bu_ref[pl.ds(t, 1), :]
            nsr = lr * sr - li * si + row[:, :w]
            nsi = lr * si + li * sr + row[:, w:]
            st_ref[pl.ds(t, 1), :] = jnp.concatenate([nsr, nsi], axis=1)
            return nsr, nsi

        sr, si = lax.fori_loop(0, tc, step, (carry[0:1, :], carry[1:2, :]))
        carry[0:1, :] = sr
        carry[1:2, :] = si

    return pl.pallas_call(
        body, name=name, grid=(nsb, T // tc),
        in_specs=[pl.BlockSpec((tc, w2), lambda j, i: (i, j)), pl.BlockSpec((8, w), lambda j, i: (j, 0)),
                  pl.BlockSpec((8, w), lambda j, i: (j, 0))],
        out_specs=pl.BlockSpec((tc, w2), lambda j, i: (i, j)), out_shape=jax.ShapeDtypeStruct(bu.shape, f32),
        scratch_shapes=[pltpu.VMEM((8, w), f32)],
        compiler_params=_params(("parallel", "arbitrary")),
    )(bu, lam_re, lam_im)


def _s5_scan_bwd(gst, states, lam_re, lam_im, nsb, name, tc_t=512):
    T = gst.shape[0]
    w2 = gst.shape[1] // nsb
    w = w2 // 2
    tc = _pick(T, tc_t, 8)
    nt = T // tc

    def body(g_ref, s_ref, sp_ref, lr_ref, li_ref, a_ref, dlr_ref, dli_ref, carry):
        i = pl.program_id(1)

        @pl.when(i == 0)
        def _():
            carry[...] = jnp.zeros_like(carry)

        lr, li = lr_ref[0:1, :], li_ref[0:1, :]

        def adj(t, ar, ai):
            row = g_ref[pl.ds(t, 1), :]
            nar = row[:, :w] + lr * ar + li * ai
            nai = row[:, w:] - li * ar + lr * ai
            a_ref[pl.ds(t, 1), :] = jnp.concatenate([nar, nai], axis=1)
            return nar, nai

        def acc(prev, ar, ai, dlr, dli):
            spr, spi = prev[:, :w], prev[:, w:]
            return dlr + ar * spr + ai * spi, dli - ar * spi + ai * spr

        def step(k, c):
            ar, ai, dlr, dli = c
            t = tc - 1 - k
            ar, ai = adj(t, ar, ai)
            dlr, dli = acc(s_ref[pl.ds(t - 1, 1), :], ar, ai, dlr, dli)
            return ar, ai, dlr, dli

        c0 = (carry[0:1, :], carry[1:2, :], carry[2:3, :], carry[3:4, :])
        ar, ai, dlr, dli = lax.fori_loop(0, tc - 1, step, c0)
        ar, ai = adj(0, ar, ai)
        prev = sp_ref[tc - 1:tc, :] * (i < nt - 1).astype(f32)
        dlr, dli = acc(prev, ar, ai, dlr, dli)
        carry[0:1, :] = ar
        carry[1:2, :] = ai
        carry[2:3, :] = dlr
        carry[3:4, :] = dli
        dlr_ref[...] = jnp.broadcast_to(dlr, (8, w))
        dli_ref[...] = jnp.broadcast_to(dli, (8, w))

    cur = lambda j, i: (nt - 1 - i, j)
    prv = lambda j, i: (jnp.maximum(nt - 2 - i, 0), j)
    return pl.pallas_call(
        body, name=name, grid=(nsb, nt),
        in_specs=[pl.BlockSpec((tc, w2), cur), pl.BlockSpec((tc, w2), cur), pl.BlockSpec((tc, w2), prv),
                  pl.BlockSpec((8, w), lambda j, i: (j, 0)), pl.BlockSpec((8, w), lambda j, i: (j, 0))],
        out_specs=[pl.BlockSpec((tc, w2), cur), pl.BlockSpec((8, w), lambda j, i: (j, 0)),
                   pl.BlockSpec((8, w), lambda j, i: (j, 0))],
        out_shape=[jax.ShapeDtypeStruct(gst.shape, f32), jax.ShapeDtypeStruct((nsb * 8, w), f32),
                   jax.ShapeDtypeStruct((nsb * 8, w), f32)],
        scratch_shapes=[pltpu.VMEM((8, w), f32)],
        compiler_params=_params(("parallel", "arbitrary")),
    )(gst, states, states, lam_re, lam_im)


def _s5_prep_fn(xs, ps):
    lam_re, lam_im, log_step, b_re, b_im, expand = ps
    lr = jnp.minimum(lam_re, S5_MAX_REAL)
    li = lam_im
    step = jnp.exp(log_step)
    er = jnp.exp(lr * step)
    ang = li * step
    lbr, lbi = er * jnp.cos(ang), er * jnp.sin(ang)
    nr, ni = lbr - 1.0, lbi
    den = lr * lr + li * li
    qr, qi = (nr * lr + ni * li) / den, (ni * lr - nr * li) / den
    qre, qie = _dot(qr, expand, "nn", _HI), _dot(qi, expand, "nn", _HI)
    return [lbr, lbi, qre * b_re - qie * b_im, qre * b_im + qie * b_re]


def _s5_prep(pars, name):
    def body(*refs):
        outs = _s5_prep_fn([], [r[...] for r in refs[:6]])
        for ref, v in zip(refs[6:], outs):
            ref[...] = v

    g, nst = pars[0].shape
    nc = pars[3].shape[1]
    return pl.pallas_call(
        body, name=name,
        out_shape=[jax.ShapeDtypeStruct((g, nst), f32)] * 2 + [jax.ShapeDtypeStruct((g, nc), f32)] * 2,
        compiler_params=_params(),
    )(*pars)


def _s5_prep_bwd(pars, cots, name):
    def body(*refs):
        ps = [r[...] for r in refs[:6]]
        ct = [r[...] for r in refs[6:10]]
        _, vjp = jax.vjp(lambda *a: _s5_prep_fn([], list(a)), *ps)
        g = vjp(ct)
        for ref, v in zip(refs[10:], g[:5]):
            ref[...] = v

    return pl.pallas_call(
        body, name=name, out_shape=[jax.ShapeDtypeStruct(p.shape, f32) for p in pars[:5]], compiler_params=_params(),
    )(*pars, *cots)


_ANY = pl.BlockSpec(memory_space=pl.ANY)


def _remote(src, dst, send_sem, recv_sem, device):
    return pltpu.make_async_remote_copy(src_ref=src, dst_ref=dst, send_sem=send_sem, recv_sem=recv_sem, device_id=device,
                                        device_id_type=MESH)


def _staged_copy(src, dst, buf, in_sems, out_sems):
    n = D2D_STREAMS
    piece = src.shape[0] // n
    assert src.shape[0] % n == 0

    def load(i):
        return pltpu.make_async_copy(src.at[pl.ds(i * piece, piece)], buf.at[i % 2], in_sems.at[i % 2])

    def store(i):
        return pltpu.make_async_copy(buf.at[i % 2], dst.at[pl.ds(i * piece, piece)], out_sems.at[i % 2])

    load(0).start()
    for i in range(n):
        if i + 1 < n:
            if i >= 1:
                store(i - 1).wait()
            load(i + 1).start()
        load(i).wait()
        store(i).start()
    store(n - 2).wait()
    store(n - 1).wait()


def _stage_scratch(rows, cols, dtype):
    return [pltpu.VMEM((2, rows // D2D_STREAMS, cols), dtype), pltpu.SemaphoreType.DMA((2,)), pltpu.SemaphoreType.DMA((2,))]


def _chip_all_gather(block, name):
    rows = block.shape[0]
    half = rows // 2
    piece = half // D2D_STREAMS
    assert rows % (2 * D2D_STREAMS * 16) == 0

    def body(src, out, ici_send, ici_recv, d2d_send, d2d_recv, *stage):
        x, y, c = lax.axis_index("x"), lax.axis_index("y"), lax.axis_index("c")
        me = 2 * x + y
        sibling = (x, y, 1 - c)
        chips = [(1 - x, y), (x, 1 - y), (1 - x, 1 - y)]
        mine = pl.ds(pl.multiple_of(c * half, 16), half)
        sends = []
        for j, (px, py) in enumerate(chips):
            cp = _remote(src.at[mine], out.at[me, mine], ici_send.at[j], ici_recv.at[j], (px, py, c))
            cp.start()
            sends.append(cp)
        _staged_copy(src, out.at[me], *stage)
        for j, (px, py) in enumerate(chips):
            slot = 2 * px + py
            _remote(src.at[mine], out.at[slot, mine], ici_send.at[j], ici_recv.at[j], (px, py, c)).wait_recv()
            for s in range(D2D_STREAMS):
                r = pl.ds(pl.multiple_of(c * half + s * piece, 16), piece)
                k = j * D2D_STREAMS + s
                cp = _remote(out.at[slot, r], out.at[slot, r], d2d_send.at[k], d2d_recv.at[k], sibling)
                cp.start()
                sends.append(cp)
        for j, (px, py) in enumerate(chips):
            slot = 2 * px + py
            for s in range(D2D_STREAMS):
                r = pl.ds(pl.multiple_of((1 - c) * half + s * piece, 16), piece)
                k = j * D2D_STREAMS + s
                _remote(out.at[slot, r], out.at[slot, r], d2d_send.at[k], d2d_recv.at[k], sibling).wait_recv()
        for cp in sends:
            cp.wait_send()

    n_d2d = 3 * D2D_STREAMS
    return pl.pallas_call(
        body, name=name, in_specs=[_ANY], out_specs=_ANY,
        out_shape=jax.ShapeDtypeStruct((N_CHIPS,) + block.shape, block.dtype),
        scratch_shapes=[pltpu.SemaphoreType.DMA((3,)), pltpu.SemaphoreType.DMA((3,)), pltpu.SemaphoreType.DMA((n_d2d,)),
                        pltpu.SemaphoreType.DMA((n_d2d,))] + _stage_scratch(rows, block.shape[1], block.dtype),
    )(block)


def _chip_scatter(parts, name):
    def body(src, out, send_sems, recv_sems, *stage):
        x, y, c = lax.axis_index("x"), lax.axis_index("y"), lax.axis_index("c")
        me = 2 * x + y
        chips = [(1 - x, y), (x, 1 - y), (1 - x, 1 - y)]
        sends = []
        for j, (px, py) in enumerate(chips):
            cp = pltpu.make_async_remote_copy(src_ref=src.at[2 * px + py], dst_ref=out.at[me], send_sem=send_sems.at[j],
                                              recv_sem=recv_sems.at[j], device_id=(px, py, c), device_id_type=MESH)
            cp.start()
            sends.append(cp)
        _staged_copy(src.at[me], out.at[me], *stage)
        for j, (px, py) in enumerate(chips):
            pltpu.make_async_remote_copy(src_ref=src.at[me], dst_ref=out.at[2 * px + py], send_sem=send_sems.at[j],
                                         recv_sem=recv_sems.at[j], device_id=(px, py, c), device_id_type=MESH).wait_recv()
        for cp in sends:
            cp.wait_send()

    return pl.pallas_call(
        body, name=name, in_specs=[_ANY], out_specs=_ANY, out_shape=jax.ShapeDtypeStruct(parts.shape, parts.dtype),
        scratch_shapes=[pltpu.SemaphoreType.DMA((3,)), pltpu.SemaphoreType.DMA((3,))]
        + _stage_scratch(parts.shape[1], parts.shape[2], parts.dtype),
    )(parts)


def _core_send_other_half(parts, name):
    n_slots, rows, cols = parts.shape
    half = rows // 2
    piece = half // D2D_STREAMS
    assert rows % (2 * D2D_STREAMS * 16) == 0

    def body(src, out, send_sems, recv_sems):
        x, y, c = lax.axis_index("x"), lax.axis_index("y"), lax.axis_index("c")
        sibling = (x, y, 1 - c)
        sends = []
        for k in range(n_slots):
            for s in range(D2D_STREAMS):
                theirs = pl.ds(pl.multiple_of((1 - c) * half + s * piece, 16), piece)
                i = k * D2D_STREAMS + s
                cp = _remote(src.at[k, theirs], out.at[k, pl.ds(s * piece, piece)], send_sems.at[i], recv_sems.at[i], sibling)
                cp.start()
                sends.append(cp)
        for cp in sends:
            cp.wait_recv()
        for cp in sends:
            cp.wait_send()

    n = n_slots * D2D_STREAMS
    return pl.pallas_call(
        body, name=name, in_specs=[_ANY], out_specs=_ANY, out_shape=jax.ShapeDtypeStruct((n_slots, half, cols), parts.dtype),
        scratch_shapes=[pltpu.SemaphoreType.DMA((n,)), pltpu.SemaphoreType.DMA((n,))],
    )(parts)


def _add_my_half(parts, other, core, name, block_rows=256):
    n_slots, rows, cols = parts.shape
    half = rows // 2
    br = _pick(half, block_rows, 16)
    nb = half // br

    def body(c_ref, a_ref, b_ref, o_ref):
        o_ref[...] = (a_ref[...].astype(f32) + b_ref[...].astype(f32)).astype(o_ref.dtype)

    grid_spec = pltpu.PrefetchScalarGridSpec(
        num_scalar_prefetch=1, grid=(n_slots, nb),
        in_specs=[pl.BlockSpec((1, br, cols), lambda k, i, c: (k, c[0] * nb + i, 0)),
                  pl.BlockSpec((1, br, cols), lambda k, i, c: (k, i, 0))],
        out_specs=pl.BlockSpec((1, br, cols), lambda k, i, c: (k, i, 0)))
    return pl.pallas_call(
        body, name=name, grid_spec=grid_spec, out_shape=jax.ShapeDtypeStruct((n_slots, half, cols), bf16),
        compiler_params=_params(("parallel", "parallel")),
    )(core, parts, other)


def _core_join_halves(mine, name):
    half, cols = mine.shape
    piece = half // D2D_STREAMS
    assert half % (D2D_STREAMS * 16) == 0

    def body(src, out, send_sems, recv_sems, *stage):
        x, y, c = lax.axis_index("x"), lax.axis_index("y"), lax.axis_index("c")
        sibling = (x, y, 1 - c)
        sends = []
        for s in range(D2D_STREAMS):
            dst = out.at[pl.ds(pl.multiple_of(c * half + s * piece, 16), piece)]
            cp = _remote(src.at[pl.ds(s * piece, piece)], dst, send_sems.at[s], recv_sems.at[s], sibling)
            cp.start()
            sends.append(cp)
        _staged_copy(src, out.at[pl.ds(pl.multiple_of(c * half, 16), half)], *stage)
        for s in range(D2D_STREAMS):
            dst = out.at[pl.ds(pl.multiple_of((1 - c) * half + s * piece, 16), piece)]
            _remote(src.at[pl.ds(s * piece, piece)], dst, send_sems.at[s], recv_sems.at[s], sibling).wait_recv()
        for cp in sends:
            cp.wait_send()

    return pl.pallas_call(
        body, name=name, in_specs=[_ANY], out_specs=_ANY, out_shape=jax.ShapeDtypeStruct((2 * half, cols), mine.dtype),
        scratch_shapes=[pltpu.SemaphoreType.DMA((D2D_STREAMS,)), pltpu.SemaphoreType.DMA((D2D_STREAMS,))]
        + _stage_scratch(half, cols, mine.dtype),
    )(mine)


def _device_all_gather(block, name):
    def body(src, out, send_sems, recv_sems, *stage):
        x, y, c = lax.axis_index("x"), lax.axis_index("y"), lax.axis_index("c")
        me = 4 * x + 2 * y + c
        peers = []
        for mask in range(1, N_DEV):
            fx, fy, fc = (mask >> 2) & 1, (mask >> 1) & 1, mask & 1
            peers.append((x ^ fx, y ^ fy, c ^ fc))
        sends = []
        for j, peer in enumerate(peers):
            cp = pltpu.make_async_remote_copy(src_ref=src, dst_ref=out.at[me], send_sem=send_sems.at[j],
                                              recv_sem=recv_sems.at[j], device_id=peer, device_id_type=MESH)
            cp.start()
            sends.append(cp)
        _staged_copy(src, out.at[me], *stage)
        for j, (px, py, pc) in enumerate(peers):
            pltpu.make_async_remote_copy(src_ref=src, dst_ref=out.at[4 * px + 2 * py + pc], send_sem=send_sems.at[j],
                                         recv_sem=recv_sems.at[j], device_id=(px, py, pc), device_id_type=MESH).wait_recv()
        for cp in sends:
            cp.wait_send()

    return pl.pallas_call(
        body, name=name, in_specs=[_ANY], out_specs=_ANY,
        out_shape=jax.ShapeDtypeStruct((N_DEV,) + block.shape, block.dtype),
        scratch_shapes=[pltpu.SemaphoreType.DMA((N_DEV - 1,)), pltpu.SemaphoreType.DMA((N_DEV - 1,))]
        + _stage_scratch(block.shape[0], block.shape[1], block.dtype),
    )(block)


def _sum_slots(stack, name, block_rows=256):
    s_n, r_n, c_n = stack.shape
    br = _pick(r_n, block_rows, 8)

    def body(in_ref, o_ref):
        acc = in_ref[0].astype(f32)
        for s in range(1, s_n):
            acc = acc + in_ref[s].astype(f32)
        o_ref[...] = acc

    return pl.pallas_call(
        body, name=name, grid=(r_n // br,), in_specs=[pl.BlockSpec((s_n, br, c_n), lambda i: (0, i, 0))],
        out_specs=pl.BlockSpec((br, c_n), lambda i: (i, 0)), out_shape=jax.ShapeDtypeStruct((r_n, c_n), f32),
        compiler_params=_params(("parallel",)),
    )(stack)


def _concat_padded(parts, mult):
    rows = sum(p.shape[0] for p in parts)
    pad = (-rows) % mult
    if pad:
        parts = parts + [jnp.zeros((pad,) + parts[0].shape[1:], parts[0].dtype)]
    return jnp.concatenate(parts, axis=0)


def _pack_weights(w, conv_w):
    parts = [w[n].astype(bf16).reshape(-1, PACK_COLS) for n in BIG]
    parts.append(lax.bitcast_convert_type(conv_w, bf16).reshape(-1, PACK_COLS))
    return _concat_padded(parts, PACK_ROW_MULT)


def _unpack_weights(full, w, conv_w):
    out, r0 = {}, 0
    for n in BIG:
        rows = w[n].size // PACK_COLS
        pieces = full[:, r0:r0 + rows].reshape((N_CHIPS,) + w[n].shape)
        out[n] = jnp.concatenate([pieces[k] for k in range(N_CHIPS)], axis=1)
        r0 += rows
    rows = conv_w.size * 2 // PACK_COLS
    pieces = lax.bitcast_convert_type(full[:, r0:r0 + rows].reshape((N_CHIPS,) + conv_w.shape + (2,)), f32)
    return out, jnp.concatenate([pieces[k] for k in range(N_CHIPS)], axis=2)


def _pack_big_grads(layer_grads):
    parts, slot_rows = [], 0
    for k in range(N_CHIPS):
        slot = []
        for n in BIG:
            for g in layer_grads:
                width = g[n].shape[0] // N_CHIPS
                slot.append(g[n][k * width:(k + 1) * width].astype(bf16).reshape(-1, PACK_COLS))
        slot_rows = sum(p.shape[0] for p in slot)
        pad = (-slot_rows) % PACK_ROW_MULT
        if pad:
            slot.append(jnp.zeros((pad, PACK_COLS), bf16))
        slot_rows += pad
        parts += slot
    return jnp.concatenate(parts, axis=0).reshape(N_CHIPS, slot_rows, PACK_COLS)


def _unpack_big_grads(summed, w):
    out, r0 = {}, 0
    for n in BIG:
        rows = w[n].size // PACK_COLS
        out[n] = summed[r0:r0 + rows].reshape(w[n].shape)
        r0 += rows
    return out


def _pack_small(vals, names):
    parts, total = [], 0
    for n in names:
        pieces = vals[n] if isinstance(vals[n], list) else [vals[n]]
        size = sum(p.size for p in pieces)
        parts += [p.reshape(-1) for p in pieces]
        pad = (-size) % LANES
        if pad:
            parts.append(jnp.zeros((pad,), f32))
        total += size + pad
    pad = (-total) % (PACK_ROW_MULT * LANES)
    if pad:
        parts.append(jnp.zeros((pad,), f32))
    return jnp.concatenate(parts).reshape(-1, LANES)


def _unpack_small(packed, like, names):
    out, r0 = {}, 0
    for n in names:
        size = like[n].size
        rows = -(-size // LANES)
        out[n] = packed[r0:r0 + rows].reshape(-1)[:size].reshape(like[n].shape)
        r0 += rows
    return out


def _dims(w, x):
    d = {}
    d["D"] = x.shape[-1]
    d["T"] = x.shape[-2]
    d["DI"] = w["ssd_norm_g"].shape[-1]
    d["NH"] = w["ssd_dt_bias"].shape[-1]
    d["CD"] = w["ssd_conv_b"].shape[-1]
    d["G"] = SSD_N_GROUPS
    d["HPG"] = d["NH"] // d["G"]
    d["P"] = d["DI"] // d["NH"]
    d["N"] = (d["CD"] - d["DI"]) // (2 * d["G"])
    d["S5G"], d["S5N"] = w["s5_lambda_re"].shape[-2:]
    d["S5C"] = w["s5_b_re"].shape[-1]
    d["S5W"] = d["S5G"] * d["S5C"]
    d["NSB"] = d["S5W"] // S5_SUPERBLOCK
    d["GSB"] = d["S5G"] // d["NSB"]
    return d


def _head_pad(v, d):
    lead = v.shape[:-1]
    v = v.reshape(lead + (d["G"], d["HPG"]))
    v = jnp.concatenate([v, jnp.zeros(lead + (d["G"], LANES - d["HPG"]), v.dtype)], axis=-1)
    return v.reshape(lead + (d["G"] * LANES,))


def _head_unpad(v, d):
    lead = v.shape[:-1]
    return v.reshape(lead + (d["G"], LANES))[..., :d["HPG"]].reshape(lead + (d["NH"],))


def _w_in_perm(w_in_t, d):
    o = d["DI"] + d["CD"]
    dt = _head_pad(w_in_t[o:o + d["NH"]].T, d).T
    return jnp.concatenate([w_in_t[:o], w_in_t[o + d["NH"]:], dt], axis=0)


def _w_in_unperm(g, d):
    o = d["DI"] + d["CD"]
    rest = d["S5W"] + 2 * d["D"]
    return jnp.concatenate([g[:o], _head_unpad(g[o + rest:].T, d).T, g[o:o + rest]], axis=0)


def _s5_block_diag(v, d):
    gsb = d["GSB"]
    g, a, b = v.shape
    row_group = (lax.broadcasted_iota(jnp.int32, (g * a, gsb * b), 0) // a) % gsb
    col_group = lax.broadcasted_iota(jnp.int32, (g * a, gsb * b), 1) // b
    return jnp.where(row_group == col_group, jnp.tile(v.reshape(g * a, b), (1, gsb)), 0)


def _s5_diag_blocks(m, d, a, b):
    gsb = d["GSB"]
    rows = m.shape[0]
    m = m.reshape(rows, gsb, b)
    row_group = (lax.broadcasted_iota(jnp.int32, (rows, gsb, 1), 0) // a) % gsb
    col_group = lax.broadcasted_iota(jnp.int32, (rows, gsb, 1), 1)
    return jnp.sum(jnp.where(row_group == col_group, m, 0), axis=1).reshape(rows // a, a, b)


def _s5_lam_rows(v, d):
    v = v.reshape(d["NSB"], 1, d["GSB"] * d["S5N"])
    return jnp.broadcast_to(v, (d["NSB"], 8, v.shape[-1])).reshape(d["NSB"] * 8, -1)


def _ffn_fwd(h, pre_g, post_g, wgu, wd, tag):
    D = h.shape[1]
    H2 = wgu.shape[0]
    xn = _row_kernel(f"{tag}_norm", _fwd_of(_f_norm), [(h, D, 0)], [pre_g], [(D, bf16)])[0]
    ab = _mm(xn, wgu, "nt", f32, f"{tag}_mm_up")
    hid = _row_kernel(f"{tag}_swiglu", _swiglu_fwd, [(ab, H2, 0)], [], [(H2 // 2, bf16)])[0]
    f = _mm(hid, wd, "nn", f32, f"{tag}_mm_down")
    out = _row_kernel(f"{tag}_resnorm", _fwd_of(_f_resnorm(0.5)), [(h, D, 0), (f, D, 0)], [post_g], [(D, f32)])[0]
    return out, dict(h=h, xn=xn, ab=ab, hid=hid, f=f)


def _ffn_bwd(dh_out, s, pre_g, post_g, wgu, wd, tag):
    D = dh_out.shape[1]
    H2 = wgu.shape[0]
    df, dpost = _row_kernel(f"{tag}_resnorm_bwd", _vjp_of(_f_post(0.5), 1, 1, [0]), [(s["f"], D, 0), (dh_out, D, 0)],
                            [post_g], [(D, bf16)], [post_g.shape])
    dwd = _mm(s["hid"], df, "tn", bf16, f"{tag}_mm_dwd")
    dhid = _mm(df, wd, "nt", bf16, f"{tag}_mm_dhid")
    dab = _row_kernel(f"{tag}_swiglu_bwd", _swiglu_bwd, [(s["ab"], H2, 0), (dhid, H2 // 2, 0)], [], [(H2, bf16)])[0]
    dwgu = _mm(dab, s["xn"], "tn", bf16, f"{tag}_mm_dwgu")
    dxn = _mm(dab, wgu, "nn", f32, f"{tag}_mm_dxn")
    dh, dpre = _row_kernel(f"{tag}_norm_bwd", _vjp_of(_f_norm, 1, 1, [0], 1), [(s["h"], D, 0), (dxn, D, 0), (dh_out, D, 0)],
                           [pre_g], [(D, f32)], [pre_g.shape])
    return dh, dict(pre_g=dpre, post_g=dpost, wgu=dwgu, wd=dwd)


def _mixer_fwd(h, p, d):
    D, DI, CD, G, N = d["D"], d["DI"], d["CD"], d["G"], d["N"]
    gl = G * LANES
    c_u5, c_ga, c_gb, c_dt = DI + CD, DI + CD + d["S5W"], DI + CD + d["S5W"] + D, DI + CD + d["S5W"] + 2 * D
    u = _row_kernel("mix_norm", _fwd_of(_f_norm), [(h, D, 0)], [p["mix_pre_g"]], [(D, bf16)])[0]
    proj = _mm(u, p["w_in"], "nt", f32, "mix_mm_in", bn_t=512)
    act = _conv_fwd(proj, DI, p["conv_w"], p["conv_b"], "ssd_conv")
    dt, adt = _row_kernel("ssd_dt", _fwd_of(_f_dt), [(proj, gl, c_dt // gl)], [p["dt_bias"], p["a_log"]], [(gl, f32)] * 2)
    y_ssd, states = _ssd_fwd(act, dt, adt, p["d_skip"], d["HPG"], d["P"], N, "ssd_scan")
    nrm = _row_kernel("ssd_post", _fwd_of(_f_ssdpost(G)), [(y_ssd, DI, 0), (proj, DI, 0)], [p["norm_g"]], [(DI, bf16)])[0]
    y_a = _mm(nrm, p["w_a"], "nn", f32, "mix_mm_a")
    u5 =(proj, d["S5W"], c_u5 // d["S5W"])
    bu = _s5_in(proj, c_u5, p["bsb"], d)
    s5st = _s5_scan_fwd(bu, p["lam_re_rows"], p["lam_im_rows"], d["NSB"], "s5_scan")
    y5 = _bdmm(s5st, p["csb"], "nn", d["NSB"], f32, "s5_mm_c")
    gel = _row_kernel("s5_post", _fwd_of(_f_s5post), [(y5, d["S5W"], 0), u5], [p["s5_d"]], [(d["S5W"], bf16)])[0]
    vg = _mm(gel, p["w_glu"], "nt", f32, "mix_mm_glu")
    glu = _row_kernel("s5_glu", _glu_fwd, [(vg, vg.shape[1], 0)], [], [(vg.shape[1] // 2, bf16)])[0]
    y_b = _mm(glu, p["w_b"], "nn", f32, "mix_mm_b")
    merged = _row_kernel("mix_merge", _fwd_of(_f_merge), [(proj, D, c_ga // D), (y_a, D, 0), (proj, D, c_gb // D), (y_b, D, 0)],
                         [], [(D, bf16)])[0]
    m = _mm(merged, p["w_out"], "nn", f32, "mix_mm_out")
    out = _row_kernel("mix_resnorm", _fwd_of(_f_resnorm(1.0)), [(h, D, 0), (m, D, 0)], [p["mix_post_g"]], [(D, f32)])[0]
    return out, dict(h=h, u=u, proj=proj, act=act, dt=dt, adt=adt, states=states, y_ssd=y_ssd, nrm=nrm, y_a=y_a, s5st=s5st,
                     y5=y5, gel=gel, vg=vg, glu=glu, y_b=y_b, merged=merged, m=m)


def _s5_in(proj, c_u5, bsb, d):
    T = proj.shape[0]
    nsb = d["NSB"]
    ka, nw = S5_SUPERBLOCK, bsb.shape[1]
    off = c_u5 // ka
    assert c_u5 % ka == 0
    bt = _pick(T, 512)

    def body(a_ref, w_ref, o_ref):
        o_ref[...] = _dot(a_ref[...].astype(bf16), w_ref[...].astype(bf16))

    return pl.pallas_call(
        body, name="s5_mm_bu", grid=(nsb, T // bt),
        in_specs=[pl.BlockSpec((bt, ka), lambda j, i: (i, off + j)), pl.BlockSpec((ka, nw), lambda j, i: (j, 0))],
        out_specs=pl.BlockSpec((bt, nw), lambda j, i: (i, j)), out_shape=jax.ShapeDtypeStruct((T, nsb * nw), f32),
        compiler_params=_params(("parallel", "parallel")),
    )(proj, bsb)


def _s5_dbsb(proj, c_u5, a, d):
    T = proj.shape[0]
    nsb = d["NSB"]
    ka, nw = S5_SUPERBLOCK, a.shape[1] // nsb
    off = c_u5 // ka
    bt = _pick(T, 512)

    def body(u_ref, a_ref, o_ref):
        pr = _dot(u_ref[...].astype(bf16), a_ref[...].astype(bf16), "tn")
        k = pl.program_id(1)

        @pl.when(k == 0)
        def _():
            o_ref[...] = pr

        @pl.when(k > 0)
        def _():
            o_ref[...] += pr

    return pl.pallas_call(
        body, name="s5_mm_dbsb", grid=(nsb, T // bt),
        in_specs=[pl.BlockSpec((bt, ka), lambda j, k: (k, off + j)), pl.BlockSpec((bt, nw), lambda j, k: (k, j))],
        out_specs=pl.BlockSpec((ka, nw), lambda j, k: (j, 0)), out_shape=jax.ShapeDtypeStruct((nsb * ka, nw), f32),
        compiler_params=_params(("parallel", "arbitrary")),
    )(proj, a)


def _mixer_bwd(dh_out, s, p, d):
    D, DI, CD, G, N, S5W = d["D"], d["DI"], d["CD"], d["G"], d["N"], d["S5W"]
    gl = G * LANES
    gn = G * N
    c_u5, c_ga, c_gb, c_dt = DI + CD, DI + CD + S5W, DI + CD + S5W + D, DI + CD + S5W + 2 * D
    proj = s["proj"]
    g = {}
    dm, g["mix_post_g"] = _row_kernel("mix_resnorm_bwd", _vjp_of(_f_post(1.0), 1, 1, [0]), [(s["m"], D, 0), (dh_out, D, 0)],
                                      [p["mix_post_g"]], [(D, bf16)], [p["mix_post_g"].shape])
    g["w_out"] = _mm(s["merged"], dm, "tn", bf16, "mix_mm_dwout")
    dmerged = _mm(dm, p["w_out"], "nt", f32, "mix_mm_dmerged")
    dga, dya, dgb, dyb = _row_kernel(
        "mix_merge_bwd", _vjp_of(_f_merge, 4, 1, [0, 1, 2, 3]),
        [(proj, D, c_ga // D), (s["y_a"], D, 0), (proj, D, c_gb // D), (s["y_b"], D, 0), (dmerged, D, 0)], [],
        [(D, bf16), (D, bf16), (D, bf16), (D, bf16)])
    g["w_a"] = _mm(s["nrm"], dya, "tn", bf16, "mix_mm_dwa")
    dnrm = _mm(dya, p["w_a"], "nt", f32, "mix_mm_dnrm")
    dy_ssd, dz, g["norm_g"] = _row_kernel(
        "ssd_post_bwd", _vjp_of(_f_ssdpost(G), 2, 1, [0, 1]), [(s["y_ssd"], DI, 0), (proj, DI, 0), (dnrm, DI, 0)],
        [p["norm_g"]], [(DI, f32), (DI, bf16)], [p["norm_g"].shape])
    dxs, d_b, d_c, ddt, dadt, dd = _ssd_bwd(s["act"], s["dt"], s["adt"], p["d_skip"], s["states"], dy_ssd,
                                            d["HPG"], d["P"], N, "ssd_scan_bwd")
    g["d_skip"] = dd.reshape(G, 8, LANES)[:, 0, :].reshape(1, gl)
    ddt_raw, g["dt_bias"], g["a_log"] = _row_kernel(
        "ssd_dt_bwd", _vjp_of(_f_dt, 1, 2, [0]), [(proj, gl, c_dt // gl), (ddt, gl, 0), (dadt, gl, 0)],
        [p["dt_bias"], p["a_log"]], [(gl, bf16)], [p["dt_bias"].shape, p["a_log"].shape])
    cw, cb = p["conv_w"], p["conv_b"]
    dxc_x, dw_x, db_x = _conv_bwd(proj, DI, cw[:, :DI], cb[:, :DI], dxs, "ssd_conv_bwd_x")
    dxc_b, dw_b, db_b = _conv_bwd(proj, 2 * DI, cw[:, DI:DI + gn], cb[:, DI:DI + gn], d_b, "ssd_conv_bwd_b")
    dxc_c, dw_c, db_c = _conv_bwd(proj, 2 * DI + gn, cw[:, DI + gn:], cb[:, DI + gn:], d_c, "ssd_conv_bwd_c")
    g["conv_w"] = jnp.concatenate([dw_x, dw_b, dw_c], axis=1)
    g["conv_b"] = jnp.concatenate([db_x, db_b, db_c], axis=1)
    g["w_b"] = _mm(s["glu"], dyb, "tn", bf16, "mix_mm_dwb")
    dglu = _mm(dyb, p["w_b"], "nt", f32, "mix_mm_dglu")
    dvg = _row_kernel("s5_glu_bwd", _glu_bwd, [(s["vg"], s["vg"].shape[1], 0), (dglu, S5W, 0)], [], [(s["vg"].shape[1], bf16)])[0]
    g["w_glu"] = _mm(dvg, s["gel"], "tn", bf16, "mix_mm_dwglu")
    dgel = _mm(dvg, p["w_glu"], "nn", f32, "mix_mm_dgel")
    dy5, du5a, g["s5_d"] = _row_kernel(
        "s5_post_bwd", _vjp_of(_f_s5post, 2, 1, [0, 1]), [(s["y5"], S5W, 0), (proj, S5W, c_u5 // S5W), (dgel, S5W, 0)],
        [p["s5_d"]], [(S5W, bf16), (S5W, f32)], [p["s5_d"].shape])
    g["csb"] = _bdmm(s["s5st"], dy5, "tn", d["NSB"], f32, "s5_mm_dcsb")
    gst = _bdmm(dy5, p["csb"], "nt", d["NSB"], f32, "s5_mm_gst")
    a, g["lam_re_rows"], g["lam_im_rows"] = _s5_scan_bwd(gst, s["s5st"], p["lam_re_rows"], p["lam_im_rows"], d["NSB"], "s5_scan_bwd")
    g["bsb"] = _s5_dbsb(proj, c_u5, a, d)
    du5b = _bdmm(a, p["bsb"], "nt", d["NSB"], f32, "s5_mm_du5")
    du5 = _row_kernel("s5_du5", _add_fn, [(du5a, S5W, 0), (du5b, S5W, 0)], [], [(S5W, bf16)])[0]
    dproj = jnp.concatenate([dz, dxc_x, dxc_b, dxc_c, du5, dga, dgb, ddt_raw], axis=1)
    g["w_in"] = _mm(dproj, s["u"], "tn", bf16, "mix_mm_dwin")
    du = _mm(dproj, p["w_in"], "nn", f32, "mix_mm_du", bk_t=2176)
    dh, g["mix_pre_g"] = _row_kernel("mix_norm_bwd", _vjp_of(_f_norm, 1, 1, [0], 1), [(s["h"], D, 0), (du, D, 0), (dh_out, D, 0)],
                                     [p["mix_pre_g"]], [(D, f32)], [p["mix_pre_g"].shape])
    return dh, g


def _layer_params(l, w, wf, conv_w_full, d):
    r2 = lambda v: v[l].reshape(1, -1)
    p = {}
    for n in ["ffn1_pre_g", "ffn1_post_g", "mix_pre_g", "mix_post_g", "ffn2_pre_g", "ffn2_post_g", "s5_d"]:
        p[n] = r2(w[n])
    p["wgu1"] = jnp.concatenate([wf["ffn1_w_gate"][l], wf["ffn1_w_up"][l]], axis=0)
    p["wd1"] = wf["ffn1_w_down"][l]
    p["wgu2"] = jnp.concatenate([wf["ffn2_w_gate"][l], wf["ffn2_w_up"][l]], axis=0)
    p["wd2"] = wf["ffn2_w_down"][l]
    p["w_in"] = _w_in_perm(wf["w_in"][l], d)
    p["w_a"], p["w_glu"], p["w_b"], p["w_out"] = wf["w_branch_a"][l], wf["s5_w_glu"][l], wf["w_branch_b"][l], wf["w_out"][l]
    p["conv_w"] = conv_w_full[l]
    p["conv_b"] = r2(w["ssd_conv_b"])
    p["dt_bias"] = _head_pad(r2(w["ssd_dt_bias"]), d)
    p["a_log"] = _head_pad(r2(w["ssd_a_log"]), d)
    p["d_skip"] = _head_pad(r2(w["ssd_d"]), d)
    p["norm_g"] = r2(w["ssd_norm_g"])
    g5, n5, c5 = d["S5G"], d["S5N"], d["S5C"]
    expand = jnp.repeat(jnp.eye(n5, dtype=f32), c5, axis=1)
    prep_in = [w["s5_lambda_re"][l], w["s5_lambda_im"][l], w["s5_log_step"][l].reshape(g5, 1),
               w["s5_b_re"][l].reshape(g5, n5 * c5), w["s5_b_im"][l].reshape(g5, n5 * c5), expand]
    lbr, lbi, bbr, bbi = _s5_prep(prep_in, "s5_prep")
    p["s5_prep_in"] = prep_in
    p["lam_re_rows"], p["lam_im_rows"] = _s5_lam_rows(lbr, d), _s5_lam_rows(lbi, d)
    to_cn = lambda v: v.reshape(g5, n5, c5).transpose(0, 2, 1)
    p["bsb"] = jnp.concatenate([_s5_block_diag(to_cn(bbr), d), _s5_block_diag(to_cn(bbi), d)], axis=1).astype(bf16)
    c_re, c_im = w["s5_c_re"][l].transpose(0, 2, 1), w["s5_c_im"][l].transpose(0, 2, 1)
    nsb = d["NSB"]
    csb = jnp.stack([_s5_block_diag(c_re, d).reshape(nsb, -1, S5_SUPERBLOCK),
                     _s5_block_diag(-c_im, d).reshape(nsb, -1, S5_SUPERBLOCK)], axis=1)
    p["csb"] = csb.reshape(-1, S5_SUPERBLOCK).astype(bf16)
    return p


def _s5_param_grads(g, p, d, l):
    g5, n5, c5, nsb, gsb = d["S5G"], d["S5N"], d["S5C"], d["NSB"], d["GSB"]
    wst = gsb * n5
    dbsb = g["bsb"]
    from_cn = lambda v: v.transpose(0, 2, 1).reshape(g5, n5 * c5)
    dbbr = from_cn(_s5_diag_blocks(dbsb[:, :wst], d, c5, n5))
    dbbi = from_cn(_s5_diag_blocks(dbsb[:, wst:], d, c5, n5))
    rows = lambda v: v.reshape(nsb, 8, wst)[:, 0, :].reshape(g5, n5)
    cots = [rows(g["lam_re_rows"]), rows(g["lam_im_rows"]), dbbr, dbbi]
    dlr, dli, dls, dbr, dbi = _s5_prep_bwd(p["s5_prep_in"], cots, "s5_prep_bwd")
    dcsb = g["csb"].reshape(nsb, 2, wst, S5_SUPERBLOCK)
    dcr = _s5_diag_blocks(dcsb[:, 0].reshape(-1, S5_SUPERBLOCK), d, n5, c5).transpose(0, 2, 1)
    dci = -_s5_diag_blocks(dcsb[:, 1].reshape(-1, S5_SUPERBLOCK), d, n5, c5).transpose(0, 2, 1)
    return dict(s5_lambda_re=dlr, s5_lambda_im=dli, s5_log_step=dls.reshape(g5), s5_b_re=dbr.reshape(g5, n5, c5),
                s5_b_im=dbi.reshape(g5, n5, c5), s5_c_re=dcr, s5_c_im=dci)


def kernel(x, ffn1_pre_g, ffn1_post_g, ffn1_w_gate, ffn1_w_up, ffn1_w_down, mix_pre_g, mix_post_g, w_in, ssd_conv_w, ssd_conv_b, ssd_dt_bias, ssd_a_log, ssd_d, ssd_norm_g, w_branch_a, s5_lambda_re, s5_lambda_im, s5_b_re, s5_b_im, s5_c_re, s5_c_im, s5_log_step, s5_d, s5_w_glu, w_branch_b, w_out, ffn2_pre_g, ffn2_post_g, ffn2_w_gate, ffn2_w_up, ffn2_w_down, loss_target, m_ffn1_pre_g, m_ffn1_post_g, m_ffn1_w_gate, m_ffn1_w_up, m_ffn1_w_down, m_mix_pre_g, m_mix_post_g, m_w_in, m_ssd_conv_w, m_ssd_conv_b, m_ssd_dt_bias, m_ssd_a_log, m_ssd_d, m_ssd_norm_g, m_w_branch_a, m_s5_lambda_re, m_s5_lambda_im, m_s5_b_re, m_s5_b_im, m_s5_c_re, m_s5_c_im, m_s5_log_step, m_s5_d, m_s5_w_glu, m_w_branch_b, m_w_out, m_ffn2_pre_g, m_ffn2_post_g, m_ffn2_w_gate, m_ffn2_w_up, m_ffn2_w_down, v_ffn1_pre_g, v_ffn1_post_g, v_ffn1_w_gate, v_ffn1_w_up, v_ffn1_w_down, v_mix_pre_g, v_mix_post_g, v_w_in, v_ssd_conv_w, v_ssd_conv_b, v_ssd_dt_bias, v_ssd_a_log, v_ssd_d, v_ssd_norm_g, v_w_branch_a, v_s5_lambda_re, v_s5_lambda_im, v_s5_b_re, v_s5_b_im, v_s5_c_re, v_s5_c_im, v_s5_log_step, v_s5_d, v_s5_w_glu, v_w_branch_b, v_w_out, v_ffn2_pre_g, v_ffn2_post_g, v_ffn2_w_gate, v_ffn2_w_up, v_ffn2_w_down):
    given = dict(locals())
    for n in COL_SHARDED:
        for prefix in ("", "m_", "v_"):
            given[prefix + n] = given[prefix + n].transpose(0, 2, 1)
    w = {n: given[n] for n in WEIGHTS}
    mom = {n: given["m_" + n] for n in WEIGHTS}
    var = {n: given["v_" + n] for n in WEIGHTS}
    d = _dims(w, x)
    n_layers = w["ffn1_pre_g"].shape[0]
    T, D = d["T"], d["D"]

    gathered = _chip_all_gather(_pack_weights(w, w["ssd_conv_w"]), "gather_weights")
    wf, conv_w_full = _unpack_weights(gathered, w, w["ssd_conv_w"])
    layers = [_layer_params(l, w, wf, conv_w_full, d) for l in range(n_layers)]

    h = x.reshape(T, D)
    saved = []
    for p in layers:
        h, s1 = _ffn_fwd(h, p["ffn1_pre_g"], p["ffn1_post_g"], p["wgu1"], p["wd1"], "ffn1")
        h, sm = _mixer_fwd(h, p, d)
        h, s2 = _ffn_fwd(h, p["ffn2_pre_g"], p["ffn2_post_g"], p["wgu2"], p["wd2"], "ffn2")
        saved.append((s1, sm, s2))
    dh, loss_part = _row_kernel("loss", _loss_fn, [(h, D, 0), (loss_target.reshape(T, D), D, 0)], [], [(D, f32)], [(8, LANES)])
    loss = lax.psum(loss_part[0, 0], ("x", "y", "c"))

    lg = [None] * n_layers
    for l in reversed(range(n_layers)):
        p = layers[l]
        s1, sm, s2 = saved[l]
        dh, g2 = _ffn_bwd(dh, s2, p["ffn2_pre_g"], p["ffn2_post_g"], p["wgu2"], p["wd2"], "ffn2")
        dh, gm = _mixer_bwd(dh, sm, p, d)
        dh, g1 = _ffn_bwd(dh, s1, p["ffn1_pre_g"], p["ffn1_post_g"], p["wgu1"], p["wd1"], "ffn1")
        H = p["wd1"].shape[0]
        gl = dict(ffn1_pre_g=g1["pre_g"], ffn1_post_g=g1["post_g"], ffn1_w_gate=g1["wgu"][:H], ffn1_w_up=g1["wgu"][H:],
                  ffn1_w_down=g1["wd"], ffn2_pre_g=g2["pre_g"], ffn2_post_g=g2["post_g"], ffn2_w_gate=g2["wgu"][:H],
                  ffn2_w_up=g2["wgu"][H:], ffn2_w_down=g2["wd"], mix_pre_g=gm["mix_pre_g"], mix_post_g=gm["mix_post_g"],
                  w_in=_w_in_unperm(gm["w_in"], d), ssd_conv_w=gm["conv_w"], ssd_conv_b=gm["conv_b"],
                  ssd_dt_bias=_head_unpad(gm["dt_bias"], d), ssd_a_log=_head_unpad(gm["a_log"], d),
                  ssd_d=_head_unpad(gm["d_skip"], d), ssd_norm_g=gm["norm_g"], w_branch_a=gm["w_a"], s5_d=gm["s5_d"],
                  s5_w_glu=gm["w_glu"], w_branch_b=gm["w_b"], w_out=gm["w_out"])
        gl.update(_s5_param_grads(gm, p, d, l))
        lg[l] = gl
    grad_x = dh.reshape(x.shape)
    packed = _pack_big_grads(lg)
    my_core = lax.axis_index("c").astype(jnp.int32).reshape(1)
    chip_part = _add_my_half(packed, _core_send_other_half(packed, "exchange_core_halves"), my_core, "sum_core_halves")
    my_sum = _sum_slots(_chip_scatter(chip_part, "scatter_grads"), "sum_chip_parts")
    grads = _unpack_big_grads(_core_join_halves(my_sum, "join_core_halves"), w)
    small_names = SMALL + ["ssd_conv_w"]
    small_parts = {n: [g[n] for g in lg] for n in small_names}
    small_like = {n: jax.ShapeDtypeStruct((n_layers,) + lg[0][n].shape, f32) for n in small_names}
    small_like.update({n: w[n] for n in SMALL})
    everyone = _device_all_gather(_pack_small(small_parts, small_names), "gather_small_grads")
    small = _unpack_small(_sum_slots(everyone, "sum_small_grads"), small_like, small_names)
    k_me = 2 * lax.axis_index("x") + lax.axis_index("y")
    cw = w["ssd_conv_w"].shape[-1]
    small["ssd_conv_w"] = lax.dynamic_slice_in_dim(small["ssd_conv_w"], k_me * cw, cw, axis=2)
    grads.update(small)

    delta, new_m, new_v = {}, {}, {}
    for n in BIG + ["ssd_conv_w"]:
        delta[n], new_m[n], new_v[n] = _adamw(w[n], grads[n], mom[n], var[n], "adamw_" + n)
    packs = [_pack_small(t, SMALL) for t in (w, grads, mom, var)]
    sd, sm_, sv = _adamw(*packs, "adamw_small")
    delta.update(_unpack_small(sd, w, SMALL))
    new_m.update(_unpack_small(sm_, w, SMALL))
    new_v.update(_unpack_small(sv, w, SMALL))
    for n in COL_SHARDED:
        for out in (grads, delta, new_m, new_v):
            out[n] = out[n].transpose(0, 2, 1)
    return (loss, grad_x, *[grads[n] for n in WEIGHTS], *[delta[n] for n in WEIGHTS],
            *[new_m[n] for n in WEIGHTS], *[new_v[n] for n in WEIGHTS])
```

```python
import functools

import numpy as np
import jax
import jax.numpy as jnp
from jax import lax
from jax.experimental import pallas as pl
from jax.experimental.pallas import tpu as pltpu

f32, bf16 = jnp.float32, jnp.bfloat16

SSD_N_GROUPS = 4
SSD_CHUNK = 128
RMS_EPS = 1e-6
S5_MAX_REAL = -1e-4
S5_SUPERBLOCK = 256
ADAM_LR, ADAM_B1, ADAM_B2, ADAM_EPS, ADAM_WD, ADAM_STEP = 0.001, 0.9, 0.999, 1e-08, 0.01, 10

LANES = 128
PACK_COLS = 1024
D2D_STREAMS = 16
PACK_ROW_MULT = 2 * D2D_STREAMS * 16
VMEM_LIMIT_BYTES = 48 * 1024 * 1024
N_CHIPS, N_CORES, N_DEV = 4, 2, 8
MESH = pl.DeviceIdType.MESH

BIG = ["ffn1_w_gate", "ffn1_w_up", "ffn1_w_down", "w_in", "w_branch_a", "s5_w_glu", "w_branch_b", "w_out",
       "ffn2_w_gate", "ffn2_w_up", "ffn2_w_down"]
COL_SHARDED = ["ffn1_w_gate", "ffn1_w_up", "w_in", "s5_w_glu", "ffn2_w_gate", "ffn2_w_up"]
SMALL = ["ffn1_pre_g", "ffn1_post_g", "mix_pre_g", "mix_post_g", "ssd_conv_b", "ssd_dt_bias", "ssd_a_log", "ssd_d",
         "ssd_norm_g", "s5_lambda_re", "s5_lambda_im", "s5_b_re", "s5_b_im", "s5_c_re", "s5_c_im", "s5_log_step",
         "s5_d", "ffn2_pre_g", "ffn2_post_g"]
WEIGHTS = ["ffn1_pre_g", "ffn1_post_g", "ffn1_w_gate", "ffn1_w_up", "ffn1_w_down", "mix_pre_g", "mix_post_g", "w_in",
           "ssd_conv_w", "ssd_conv_b", "ssd_dt_bias", "ssd_a_log", "ssd_d", "ssd_norm_g", "w_branch_a", "s5_lambda_re",
           "s5_lambda_im", "s5_b_re", "s5_b_im", "s5_c_re", "s5_c_im", "s5_log_step", "s5_d", "s5_w_glu", "w_branch_b",
           "w_out", "ffn2_pre_g", "ffn2_post_g", "ffn2_w_gate", "ffn2_w_up", "ffn2_w_down"]


def _params(sem=None):
    return pltpu.CompilerParams(dimension_semantics=sem, vmem_limit_bytes=VMEM_LIMIT_BYTES)


def _pick(n, target, mult=LANES):
    best = None
    for d in range(mult, min(n, target) + 1, mult):
        if n % d == 0:
            best = d
    return best if best is not None else n


_DIMS = {"nn": (((1,), (0,)), ((), ())), "nt": (((1,), (1,)), ((), ())), "tn": (((0,), (0,)), ((), ()))}


def _mm(a, b, mode, out_dtype, name, bm_t=1024, bn_t=1024, bk_t=2816):
    if mode == "nn":
        (M, K), (K2, N) = a.shape, b.shape
    elif mode == "nt":
        (M, K), (N, K2) = a.shape, b.shape
    else:
        (K, M), (K2, N) = a.shape, b.shape
    assert K == K2, (name, a.shape, b.shape)
    bm, bn, bk = _pick(M, bm_t), _pick(N, bn_t), _pick(K, bk_t)
    nk = K // bk
    dn = _DIMS[mode]

    def body(a_ref, b_ref, o_ref, *scratch):
        p = lax.dot_general(a_ref[...].astype(bf16), b_ref[...].astype(bf16), dn, preferred_element_type=f32)
        if nk == 1:
            o_ref[...] = p.astype(o_ref.dtype)
        else:
            acc = scratch[0]
            k = pl.program_id(2)

            @pl.when(k == 0)
            def _():
                acc[...] = p

            @pl.when(k > 0)
            def _():
                acc[...] += p

            @pl.when(k == nk - 1)
            def _():
                o_ref[...] = acc[...].astype(o_ref.dtype)

    if mode == "tn":
        a_spec = pl.BlockSpec((bk, bm), lambda i, j, k: (k, i))
    else:
        a_spec = pl.BlockSpec((bm, bk), lambda i, j, k: (i, k))
    if mode == "nt":
        b_spec = pl.BlockSpec((bn, bk), lambda i, j, k: (j, k))
    else:
        b_spec = pl.BlockSpec((bk, bn), lambda i, j, k: (k, j))
    return pl.pallas_call(
        body, name=name, grid=(M // bm, N // bn, nk), in_specs=[a_spec, b_spec],
        out_specs=pl.BlockSpec((bm, bn), lambda i, j, k: (i, j)), out_shape=jax.ShapeDtypeStruct((M, N), out_dtype),
        scratch_shapes=[pltpu.VMEM((bm, bn), f32)] if nk > 1 else [],
        compiler_params=_params(("parallel", "parallel", "arbitrary")),
    )(a, b)


def _bdmm(a, w, mode, nb, out_dtype, name, bt_t=512):
    if mode == "tn":
        T = a.shape[0]
        ka, nw = a.shape[1] // nb, w.shape[1] // nb
        bt = _pick(T, bt_t)
        nt = T // bt

        def body_tn(a_ref, b_ref, o_ref):
            p = lax.dot_general(a_ref[...].astype(bf16), b_ref[...].astype(bf16), _DIMS["tn"], preferred_element_type=f32)
            k = pl.program_id(1)

            @pl.when(k == 0)
            def _():
                o_ref[...] = p

            @pl.when(k > 0)
            def _():
                o_ref[...] += p

        return pl.pallas_call(
            body_tn, name=name, grid=(nb, nt),
            in_specs=[pl.BlockSpec((bt, ka), lambda j, k: (k, j)), pl.BlockSpec((bt, nw), lambda j, k: (k, j))],
            out_specs=pl.BlockSpec((ka, nw), lambda j, k: (j, 0)), out_shape=jax.ShapeDtypeStruct((nb * ka, nw), f32),
            compiler_params=_params(("parallel", "arbitrary")),
        )(a, w)
    T = a.shape[0]
    ka, nw = w.shape[0] // nb, w.shape[1]
    bt = _pick(T, bt_t)
    kin, kout = (ka, nw) if mode == "nn" else (nw, ka)
    dn = _DIMS[mode]

    def body(a_ref, w_ref, o_ref):
        o_ref[...] = lax.dot_general(a_ref[...].astype(bf16), w_ref[...].astype(bf16), dn,
                                     preferred_element_type=f32).astype(o_ref.dtype)

    return pl.pallas_call(
        body, name=name, grid=(nb, T // bt),
        in_specs=[pl.BlockSpec((bt, kin), lambda j, i: (i, j)), pl.BlockSpec((ka, nw), lambda j, i: (j, 0))],
        out_specs=pl.BlockSpec((bt, kout), lambda j, i: (i, j)), out_shape=jax.ShapeDtypeStruct((T, nb * kout), out_dtype),
        compiler_params=_params(("parallel", "parallel")),
    )(a, w)


def _row_index(i, cb):
    return (i, cb)


def _row_kernel(name, fn, rows, pars, row_outs, par_outs=(), block_rows=256):
    T = rows[0][0].shape[0]
    R = min(block_rows, T)
    assert T % R == 0
    nr, npar, nro = len(rows), len(pars), len(row_outs)

    def body(*refs):
        rv = [r[...] for r in refs[:nr]]
        pv = [r[...] for r in refs[nr:nr + npar]]
        ro, po = fn(rv, pv)
        for ref, v in zip(refs[nr + npar:nr + npar + nro], ro):
            ref[...] = v.astype(ref.dtype)
        if par_outs:
            i = pl.program_id(0)
            prefs = refs[nr + npar + nro:]

            @pl.when(i == 0)
            def _():
                for ref, v in zip(prefs, po):
                    ref[...] = v.astype(f32)

            @pl.when(i > 0)
            def _():
                for ref, v in zip(prefs, po):
                    ref[...] += v.astype(f32)

    in_specs = [pl.BlockSpec((R, nc), functools.partial(_row_index, cb=cb)) for (_, nc, cb) in rows]
    in_specs += [pl.BlockSpec(p.shape, lambda i: (0, 0)) for p in pars]
    out_specs = [pl.BlockSpec((R, nc), lambda i: (i, 0)) for (nc, _) in row_outs]
    out_specs += [pl.BlockSpec(s, lambda i: (0, 0)) for s in par_outs]
    out_shape = [jax.ShapeDtypeStruct((T, nc), dt) for (nc, dt) in row_outs]
    out_shape += [jax.ShapeDtypeStruct(s, f32) for s in par_outs]
    outs = pl.pallas_call(
        body, name=name, grid=(T // R,), in_specs=in_specs, out_specs=out_specs, out_shape=out_shape,
        compiler_params=_params(("arbitrary",) if par_outs else ("parallel",)),
    )(*[r[0] for r in rows], *pars)
    return list(outs)


def _fwd_of(f):
    def fn(rv, pv):
        return f([v.astype(f32) for v in rv], [v.astype(f32) for v in pv]), []
    return fn


def _vjp_of(f, n_x, n_cot, grad_idx, n_add=0):
    def fn(rv, pv):
        xs = [v.astype(f32) for v in rv[:n_x]]
        cots = [v.astype(f32) for v in rv[n_x:n_x + n_cot]]
        adds = rv[n_x + n_cot:n_x + n_cot + n_add]
        ps = [v.astype(f32) for v in pv]
        _, vjp = jax.vjp(lambda *a: f(list(a[:n_x]), list(a[n_x:])), *xs, *ps)
        g = vjp(cots)
        row_g = [g[i] for i in grad_idx]
        for k, a in enumerate(adds):
            row_g[k] = row_g[k] + a.astype(f32)
        return row_g, list(g[n_x:])
    return fn


def _rms(x, g):
    return x * lax.rsqrt(jnp.mean(x * x, axis=-1, keepdims=True) + RMS_EPS) * g


def _f_norm(xs, ps):
    return [_rms(xs[0], ps[0])]


def _f_post(scale):
    def f(xs, ps):
        return [scale * _rms(xs[0], ps[0])]
    return f


def _f_resnorm(scale):
    def f(xs, ps):
        return [xs[0] + scale * _rms(xs[1], ps[0])]
    return f


def _f_dt(xs, ps):
    dt = jax.nn.softplus(xs[0] + ps[0])
    return [dt, -jnp.exp(ps[1]) * dt]


def _f_ssdpost(n_groups):
    def f(xs, ps):
        y = xs[0] * jax.nn.silu(xs[1])
        width = y.shape[-1] // n_groups
        lane = lax.broadcasted_iota(jnp.int32, y.shape, 1)
        scale = jnp.zeros_like(y)
        for k in range(n_groups):
            m = ((lane >= k * width) & (lane < (k + 1) * width)).astype(f32)
            ms = jnp.sum(y * y * m, axis=-1, keepdims=True) / width
            scale = scale + lax.rsqrt(ms + RMS_EPS) * m
        return [y * scale * ps[0]]
    return f


def _f_s5post(xs, ps):
    return [jax.nn.gelu(xs[0] + ps[0] * xs[1])]


def _f_merge(xs, ps):
    return [jax.nn.sigmoid(xs[0]) * xs[1] + jax.nn.sigmoid(xs[2]) * xs[3]]


def _swiglu_fwd(rv, pv):
    ab = rv[0].astype(f32)
    h = ab.shape[1] // 2
    return [jax.nn.silu(ab[:, :h]) * ab[:, h:]], []


def _swiglu_bwd(rv, pv):
    ab, d = rv[0].astype(f32), rv[1].astype(f32)
    h = ab.shape[1] // 2
    a, b = ab[:, :h], ab[:, h:]
    s = jax.nn.sigmoid(a)
    return [jnp.concatenate([d * b * (s * (1.0 + a * (1.0 - s))), d * (a * s)], axis=1)], []


def _glu_fwd(rv, pv):
    vg = rv[0].astype(f32)
    h = vg.shape[1] // 2
    return [vg[:, :h] * jax.nn.sigmoid(vg[:, h:])], []


def _glu_bwd(rv, pv):
    vg, d = rv[0].astype(f32), rv[1].astype(f32)
    h = vg.shape[1] // 2
    s = jax.nn.sigmoid(vg[:, h:])
    return [jnp.concatenate([d * s, d * vg[:, :h] * s * (1.0 - s)], axis=1)], []


def _loss_fn(rv, pv):
    e = rv[0].astype(f32) - rv[1].astype(f32)
    per_tok = jnp.mean(e * e, axis=-1, keepdims=True)
    part = 0.5 * jnp.sum(per_tok, axis=0, keepdims=True)
    return [e / e.shape[-1]], [jnp.broadcast_to(part, (8, LANES))]


def _add_fn(rv, pv):
    return [rv[0].astype(f32) + rv[1].astype(f32)], []


def _adamw_fn(rv, pv):
    w, g, m, v = [x.astype(f32) for x in rv]
    m = ADAM_B1 * m + (1.0 - ADAM_B1) * g
    v = ADAM_B2 * v + (1.0 - ADAM_B2) * (g * g)
    m_hat = m / (1.0 - ADAM_B1 ** ADAM_STEP)
    v_hat = v / (1.0 - ADAM_B2 ** ADAM_STEP)
    return [-ADAM_LR * (m_hat / (jnp.sqrt(v_hat) + ADAM_EPS) + ADAM_WD * w), m, v], []


def _adamw(w, g, m, v, name):
    shape = w.shape
    cols = shape[-1] if (w.ndim >= 2 and shape[-1] >= LANES) else None
    if cols is None:
        n = int(np.prod(shape))
        cols = LANES if n % LANES == 0 else n
    n_rows = int(np.prod(shape)) // cols
    br, bc = _pick(n_rows, 256, 8), cols
    if br < 64 and cols % LANES == 0:
        br, bc = n_rows, LANES

    def body(w_ref, g_ref, m_ref, v_ref, d_ref, nm_ref, nv_ref):
        outs, _ = _adamw_fn([w_ref[...], g_ref[...], m_ref[...], v_ref[...]], [])
        d_ref[...], nm_ref[...], nv_ref[...] = outs

    spec = pl.BlockSpec((br, bc), lambda i, j: (i, j))
    outs = pl.pallas_call(
        body, name=name, grid=(n_rows // br, cols // bc), in_specs=[spec] * 4, out_specs=[spec] * 3,
        out_shape=[jax.ShapeDtypeStruct((n_rows, cols), f32)] * 3, compiler_params=_params(("parallel", "parallel")),
    )(*[t.reshape(n_rows, cols) for t in (w, g, m, v)])
    return [o.reshape(shape) for o in outs]


def _shift_down(x, s, row):
    if s == 0:
        return x
    return jnp.where(row >= s, pltpu.roll(x, s, 0), 0.0)


def _shift_up(x, s, row):
    if s == 0:
        return x
    n = x.shape[0]
    return jnp.where(row < n - s, pltpu.roll(x, n - s, 0), 0.0)


def _conv_pre(x, w, b, row):
    kw = w.shape[0]
    c = b
    for k in range(kw):
        c = c + w[k:k + 1, :] * _shift_down(x, kw - 1 - k, row)
    return c


def _conv_fwd(xsrc, col0, w, b, name, bc_t=512):
    T = xsrc.shape[0]
    kw, ncols = w.shape
    bc = _pick(ncols, bc_t)
    off = col0 // bc
    assert col0 % bc == 0

    def body(x_ref, w_ref, b_ref, o_ref):
        x = x_ref[...].astype(f32)
        row = lax.broadcasted_iota(jnp.int32, x.shape, 0)
        c = _conv_pre(x, w_ref[...], b_ref[...], row)
        o_ref[...] = c * jax.nn.sigmoid(c)

    return pl.pallas_call(
        body, name=name, grid=(ncols // bc,),
        in_specs=[pl.BlockSpec((T, bc), lambda j: (0, off + j)), pl.BlockSpec((kw, bc), lambda j: (0, j)),
                  pl.BlockSpec((1, bc), lambda j: (0, j))],
        out_specs=pl.BlockSpec((T, bc), lambda j: (0, j)), out_shape=jax.ShapeDtypeStruct((T, ncols), f32),
        compiler_params=_params(("parallel",)),
    )(xsrc, w, b)


def _conv_bwd(xsrc, col0, w, b, dact, name, bc_t=512):
    T = xsrc.shape[0]
    kw, ncols = w.shape
    bc = _pick(ncols, bc_t)
    off = col0 // bc
    assert col0 % bc == 0

    def body(x_ref, w_ref, b_ref, d_ref, dx_ref, dw_ref, db_ref):
        x = x_ref[...].astype(f32)
        w = w_ref[...]
        row = lax.broadcasted_iota(jnp.int32, x.shape, 0)
        c = _conv_pre(x, w, b_ref[...], row)
        s = jax.nn.sigmoid(c)
        dc = d_ref[...].astype(f32) * (s * (1.0 + c * (1.0 - s)))
        dx = jnp.zeros_like(x)
        dws = []
        for k in range(kw):
            dx = dx + w[k:k + 1, :] * _shift_up(dc, kw - 1 - k, row)
            dws.append(jnp.sum(dc * _shift_down(x, kw - 1 - k, row), axis=0, keepdims=True))
        dx_ref[...] = dx.astype(dx_ref.dtype)
        dw_ref[...] = jnp.concatenate(dws, axis=0)
        db_ref[...] = jnp.sum(dc, axis=0, keepdims=True)

    return pl.pallas_call(
        body, name=name, grid=(ncols // bc,),
        in_specs=[pl.BlockSpec((T, bc), lambda j: (0, off + j)), pl.BlockSpec((kw, bc), lambda j: (0, j)),
                  pl.BlockSpec((1, bc), lambda j: (0, j)), pl.BlockSpec((T, bc), lambda j: (0, j))],
        out_specs=[pl.BlockSpec((T, bc), lambda j: (0, j)), pl.BlockSpec((kw, bc), lambda j: (0, j)),
                   pl.BlockSpec((1, bc), lambda j: (0, j))],
        out_shape=[jax.ShapeDtypeStruct((T, ncols), bf16), jax.ShapeDtypeStruct((kw, ncols), f32),
                   jax.ShapeDtypeStruct((1, ncols), f32)],
        compiler_params=_params(("parallel",)),
    )(xsrc, w, b, dact)


_HI = lax.Precision.HIGHEST


def _dot(a, b, dims="nn", precision=None):
    return lax.dot_general(a, b, _DIMS[dims], preferred_element_type=f32, precision=precision)


def _ssd_common(x_ref, b_ref, c_ref, dt_ref, adt_ref, d_ref, hpg, p):
    q = b_ref.shape[0]
    hp = hpg * p
    bb, cb = b_ref[...].astype(bf16), c_ref[...].astype(bf16)
    r = lax.broadcasted_iota(jnp.int32, (q, q), 0)
    s = lax.broadcasted_iota(jnp.int32, (q, q), 1)
    tril = r >= s
    trilf = tril.astype(f32)
    eh = lax.broadcasted_iota(jnp.int32, (LANES, hp), 0)
    ec = lax.broadcasted_iota(jnp.int32, (LANES, hp), 1)
    expand = ((ec >= eh * p) & (ec < (eh + 1) * p)).astype(f32)
    adt = adt_ref[...]
    cum = _dot(trilf, adt, "nn", _HI)
    cum_t = _dot(adt, (r <= s).astype(f32), "tn", _HI)
    cum_e = _dot(cum, expand, "nn", _HI)
    dt_e = _dot(dt_ref[...], expand, "nn", _HI)
    d_e = _dot(jnp.broadcast_to(d_ref[...], (8, LANES)), expand, "nn", _HI)[0:1, :]
    gmat = _dot(cb, bb, "nt")
    x = x_ref[...]
    xdt = x * dt_e
    e_all = jnp.exp(cum_e)
    dec = jnp.exp(cum_e[q - 1:q, :] - cum_e)
    lms, ms = [], []
    for h in range(hpg):
        lm = jnp.exp(jnp.where(tril, cum[:, h:h + 1] - cum_t[h:h + 1, :], -1e30))
        lms.append(lm)
        ms.append(gmat * lm)
    et = [jnp.exp(cum[q - 1:q, h:h + 1]) for h in range(hpg)]
    return dict(bb=bb, cb=cb, trilf=trilf, expand=expand, cum=cum, x=x, xdt=xdt, dt_e=dt_e, d_e=d_e, e=e_all, dec=dec,
                lms=lms, ms=ms, et=et)


def _ssd_specs(q, hp, n, g_n, nc, rev):
    def cidx(c):
        return (nc - 1 - c) if rev else c
    x_spec = pl.BlockSpec((q, hp), lambda g, c: (cidx(c), g))
    boff = (g_n * hp) // n
    b_spec = pl.BlockSpec((q, n), lambda g, c: (cidx(c), boff + g))
    c_spec = pl.BlockSpec((q, n), lambda g, c: (cidx(c), boff + g_n + g))
    dt_spec = pl.BlockSpec((q, LANES), lambda g, c: (cidx(c), g))
    d_spec = pl.BlockSpec((1, LANES), lambda g, c: (0, g))
    st_spec = pl.BlockSpec((1, 1, hp, n), lambda g, c: (cidx(c), g, 0, 0))
    return x_spec, b_spec, c_spec, dt_spec, d_spec, st_spec


def _ssd_fwd(act, dt, adt, dpad, hpg, p, n, name):
    T = act.shape[0]
    g_n, q = SSD_N_GROUPS, SSD_CHUNK
    nc, hp = T // q, hpg * p
    x_spec, b_spec, c_spec, dt_spec, d_spec, st_spec = _ssd_specs(q, hp, n, g_n, nc, False)

    def body(x_ref, b_ref, c_ref, dt_ref, adt_ref, d_ref, y_ref, st_ref, s_scr):
        @pl.when(pl.program_id(1) == 0)
        def _():
            s_scr[...] = jnp.zeros_like(s_scr)

        k = _ssd_common(x_ref, b_ref, c_ref, dt_ref, adt_ref, d_ref, hpg, p)
        s0 = s_scr[...]
        st_ref[0, 0] = s0
        xdtb = k["xdt"].astype(bf16)
        ydiag = [_dot(k["ms"][h].astype(bf16), xdtb[:, h * p:(h + 1) * p]) for h in range(hpg)]
        z = _dot(k["cb"], s0.astype(bf16), "nt")
        y_ref[...] = jnp.concatenate(ydiag, axis=1) + k["e"] * z + k["d_e"] * k["x"]
        upd = _dot((k["xdt"] * k["dec"]).astype(bf16), k["bb"], "tn")
        for h in range(hpg):
            s_scr[h * p:(h + 1) * p, :] = k["et"][h] * s0[h * p:(h + 1) * p, :] + upd[h * p:(h + 1) * p, :]

    return pl.pallas_call(
        body, name=name, grid=(g_n, nc),
        in_specs=[x_spec, b_spec, c_spec, dt_spec, dt_spec, d_spec],
        out_specs=[pl.BlockSpec((q, hp), lambda g, c: (c, g)), st_spec],
        out_shape=[jax.ShapeDtypeStruct((T, g_n * hp), f32), jax.ShapeDtypeStruct((nc, g_n, hp, n), f32)],
        scratch_shapes=[pltpu.VMEM((hp, n), f32)],
        compiler_params=_params(("parallel", "arbitrary")),
    )(act, act, act, dt, adt, dpad)


def _ssd_bwd(act, dt, adt, dpad, states, dy, hpg, p, n, name):
    T = act.shape[0]
    g_n, q = SSD_N_GROUPS, SSD_CHUNK
    nc, hp = T // q, hpg * p
    x_spec, b_spec, c_spec, dt_spec, d_spec, st_spec = _ssd_specs(q, hp, n, g_n, nc, True)

    def body(x_ref, b_ref, c_ref, dt_ref, adt_ref, d_ref, st_ref, dy_ref,
             dx_ref, db_ref, dc_ref, ddt_ref, dadt_ref, dd_ref, ds_scr):
        first = pl.program_id(1) == 0

        @pl.when(first)
        def _():
            ds_scr[...] = jnp.zeros_like(ds_scr)

        k = _ssd_common(x_ref, b_ref, c_ref, dt_ref, adt_ref, d_ref, hpg, p)
        bb, cb, expand, x, xdt, dec = k["bb"], k["cb"], k["expand"], k["x"], k["xdt"], k["dec"]
        heads = lambda t: _dot(t, expand, "nt", _HI)
        s0 = st_ref[0, 0]
        ds1 = ds_scr[...]
        s0b, ds1b = s0.astype(bf16), ds1.astype(bf16)
        dy = dy_ref[...]
        dyb, xdtb = dy.astype(bf16), xdt.astype(bf16)
        lane = lax.broadcasted_iota(jnp.int32, (1, LANES), 1)
        dg = jnp.zeros((q, q), f32)
        w_rows = jnp.zeros((q, LANES), f32)
        w_cols, dxdt_parts = [], []
        for h in range(hpg):
            hs = slice(h * p, (h + 1) * p)
            dm = _dot(dyb[:, hs], xdtb[:, hs], "nt")
            dg = dg + dm * k["lms"][h]
            wm = dm * k["ms"][h]
            w_rows = w_rows + jnp.sum(wm, axis=1, keepdims=True) * (lane == h).astype(f32)
            w_cols.append(jnp.sum(wm, axis=0, keepdims=True))
            dxdt_parts.append(_dot(k["ms"][h].astype(bf16), dyb[:, hs], "tn"))
        dxdt_diag = jnp.concatenate(dxdt_parts, axis=1)
        w_cols = jnp.concatenate(w_cols + [jnp.zeros((LANES - hpg, q), f32)], axis=0).T
        dgb = dg.astype(bf16)
        z = _dot(cb, s0b, "nt")
        dz = dy * k["e"]
        dzb = dz.astype(bf16)
        dxd = _dot(bb, ds1b, "nt")
        ddec = dxd * xdt * dec
        db_ref[...] = _dot(dgb, cb, "tn") + _dot((xdt * dec).astype(bf16), ds1b)
        dc_ref[...] = _dot(dgb, bb) + _dot(dzb, s0b)
        ds0 = _dot(dzb, cb, "tn")
        for h in range(hpg):
            hs = slice(h * p, (h + 1) * p)
            ds_scr[hs, :] = ds0[hs, :] + k["et"][h] * ds1[hs, :]
        dxdt = dxdt_diag + dxd * dec
        ddec_h = heads(ddec)
        dcum = w_rows - w_cols + heads(dz * z) - ddec_h
        et_row = jnp.exp(k["cum"][q - 1:q, :])
        dsum = _dot(jnp.ones((8, n), f32), _dot(expand, ds1 * s0, "nn", _HI), "nt", _HI)[0:1, :]
        dcl = dsum * et_row + jnp.sum(ddec_h, axis=0, keepdims=True)
        rowq = lax.broadcasted_iota(jnp.int32, (q, 1), 0)
        dcum = dcum + (rowq == q - 1).astype(f32) * dcl
        ddt_ref[...] = heads(dxdt * x)
        dadt_ref[...] = _dot(k["trilf"], dcum, "tn", _HI)
        dx_ref[...] = k["d_e"] * dy + dxdt * k["dt_e"]
        dd8 = heads(jnp.broadcast_to(jnp.sum(dy * x, axis=0, keepdims=True), (8, hp)))

        @pl.when(first)
        def _():
            dd_ref[...] = dd8

        @pl.when(jnp.logical_not(first))
        def _():
            dd_ref[...] += dd8

    rc = lambda g, c: (nc - 1 - c, g)
    return pl.pallas_call(
        body, name=name, grid=(g_n, nc),
        in_specs=[x_spec, b_spec, c_spec, dt_spec, dt_spec, d_spec, st_spec, pl.BlockSpec((q, hp), rc)],
        out_specs=[pl.BlockSpec((q, hp), rc), pl.BlockSpec((q, n), rc), pl.BlockSpec((q, n), rc),
                   pl.BlockSpec((q, LANES), rc), pl.BlockSpec((q, LANES), rc), pl.BlockSpec((8, LANES), lambda g, c: (g, 0))],
        out_shape=[jax.ShapeDtypeStruct((T, g_n * hp), f32), jax.ShapeDtypeStruct((T, g_n * n), f32),
                   jax.ShapeDtypeStruct((T, g_n * n), f32), jax.ShapeDtypeStruct((T, g_n * LANES), f32),
                   jax.ShapeDtypeStruct((T, g_n * LANES), f32), jax.ShapeDtypeStruct((g_n * 8, LANES), f32)],
        scratch_shapes=[pltpu.VMEM((hp, n), f32)],
        compiler_params=_params(("parallel", "arbitrary")),
    )(act, act, act, dt, adt, dpad, states, dy)


def _cmul(ar, ai, br, bi):
    return ar * br - ai * bi, ar * bi + ai * br


def _s5_tile_powers(lr, li):
    p = [(lr, li)]
    for _ in range(7):
        p.append(_cmul(p[-1][0], p[-1][1], lr, li))
    tile = (jnp.concatenate([q[0] for q in p], axis=0), jnp.concatenate([q[1] for q in p], axis=0))
    return tile, (p[0], p[1], p[3])


def _s5_tile_scan(xr, xi, steps, reverse):
    row = lax.broadcasted_iota(jnp.int32, xr.shape, 0)
    for d, (pr, pi) in zip((1, 2, 4), steps):
        if reverse:
            keep = row < 8 - d
            sr, si = pltpu.roll(xr, 8 - d, 0), pltpu.roll(xi, 8 - d, 0)
        else:
            keep = row >= d
            sr, si = pltpu.roll(xr, d, 0), pltpu.roll(xi, d, 0)
        sr, si = jnp.where(keep, sr, 0.0), jnp.where(keep, si, 0.0)
        ar, ai = _cmul(sr, si, pr, pi)
        xr, xi = xr + ar, xi + ai
    return xr, xi


def _s5_scan_fwd(bu, lam_re, lam_im, nsb, name, tc_t=512):
    T = bu.shape[0]
    w2 = bu.shape[1] // nsb
    w = w2 // 2
    tc = _pick(T, tc_t, 8)

    def body(bu_ref, lr_ref, li_ref, st_ref, carry):
        @pl.when(pl.program_id(1) == 0)
        def _():
            carry[...] = jnp.zeros_like(carry)

        (pr8, pi8), steps = _s5_tile_powers(lr_ref[0:1, :], li_ref[0:1, :])

        def tile(i, c):
            r = pl.ds(pl.multiple_of(i * 8, 8), 8)
            x = bu_ref[r, :]
            xr, xi = _s5_tile_scan(x[:, :w], x[:, w:], steps, False)
            ar, ai = _cmul(pr8, pi8, c[0], c[1])
            xr, xi = xr + ar, xi + ai
            st_ref[r, :] = jnp.concatenate([xr, xi], axis=1)
            return xr[7:8, :], xi[7:8, :]

        c = lax.fori_loop(0, tc // 8, tile, (carry[0:1, :], carry[1:2, :]), unroll=2)
        carry[0:1, :] = c[0]
        carry[1:2, :] = c[1]

    return pl.pallas_call(
        body, name=name, grid=(nsb, T // tc),
        in_specs=[pl.BlockSpec((tc, w2), lambda j, i: (i, j)), pl.BlockSpec((8, w), lambda j, i: (j, 0)),
                  pl.BlockSpec((8, w), lambda j, i: (j, 0))],
        out_specs=pl.BlockSpec((tc, w2), lambda j, i: (i, j)), out_shape=jax.ShapeDtypeStruct(bu.shape, f32),
        scratch_shapes=[pltpu.VMEM((8, w), f32)],
        compiler_params=_params(("parallel", "arbitrary")),
    )(bu, lam_re, lam_im)


def _s5_scan_bwd(gst, states, lam_re, lam_im, nsb, name, tc_t=512):
    T = gst.shape[0]
    w2 = gst.shape[1] // nsb
    w = w2 // 2
    tc = _pick(T, tc_t, 8)
    nt = T // tc
    n_tiles = tc // 8

    def body(g_ref, s_ref, sp_ref, lr_ref, li_ref, a_ref, dlr_ref, dli_ref, carry, acc):
        chunk = pl.program_id(1)

        @pl.when(chunk == 0)
        def _():
            carry[...] = jnp.zeros_like(carry)
            acc[...] = jnp.zeros_like(acc)

        (qr8, qi8), steps = _s5_tile_powers(lr_ref[0:1, :], -li_ref[0:1, :])
        row = lax.broadcasted_iota(jnp.int32, (8, w), 0)
        rev_r, rev_i = jnp.zeros((8, w), f32), jnp.zeros((8, w), f32)
        for r in range(8):
            rev_r = jnp.where(row == r, qr8[7 - r:8 - r, :], rev_r)
            rev_i = jnp.where(row == r, qi8[7 - r:8 - r, :], rev_i)
        row2 = lax.broadcasted_iota(jnp.int32, (8, w2), 0)

        def tile(k, c):
            ar_in, ai_in, dr, di = c
            i = n_tiles - 1 - k
            r = pl.ds(pl.multiple_of(i * 8, 8), 8)
            x = g_ref[r, :]
            xr, xi = _s5_tile_scan(x[:, :w], x[:, w:], steps, True)
            pr, pi = _cmul(rev_r, rev_i, ar_in, ai_in)
            xr, xi = xr + pr, xi + pi
            a_ref[r, :] = jnp.concatenate([xr, xi], axis=1)
            before = jnp.where(i > 0, s_ref[pl.ds(pl.multiple_of(jnp.maximum(i - 1, 0) * 8, 8), 8), :],
                               sp_ref[tc - 8:tc, :] * (chunk < nt - 1).astype(f32))
            prev = jnp.where(row2 == 0, pltpu.roll(before, 1, 0), pltpu.roll(s_ref[r, :], 1, 0))
            spr, spi = prev[:, :w], prev[:, w:]
            return xr[0:1, :], xi[0:1, :], dr + xr * spr + xi * spi, di - xr * spi + xi * spr

        c0 = (carry[0:1, :], carry[1:2, :], acc[0:8, :], acc[8:16, :])
        ar, ai, dr, di = lax.fori_loop(0, n_tiles, tile, c0, unroll=2)
        carry[0:1, :] = ar
        carry[1:2, :] = ai
        acc[0:8, :] = dr
        acc[8:16, :] = di
        dlr_ref[...] = jnp.broadcast_to(jnp.sum(dr, axis=0, keepdims=True), (8, w))
        dli_ref[...] = jnp.broadcast_to(jnp.sum(di, axis=0, keepdims=True), (8, w))

    cur = lambda j, i: (nt - 1 - i, j)
    prv = lambda j, i: (jnp.maximum(nt - 2 - i, 0), j)
    return pl.pallas_call(
        body, name=name, grid=(nsb, nt),
        in_specs=[pl.BlockSpec((tc, w2), cur), pl.BlockSpec((tc, w2), cur), pl.BlockSpec((tc, w2), prv),
                  pl.BlockSpec((8, w), lambda j, i: (j, 0)), pl.BlockSpec((8, w), lambda j, i: (j, 0))],
        out_specs=[pl.BlockSpec((tc, w2), cur), pl.BlockSpec((8, w), lambda j, i: (j, 0)),
                   pl.BlockSpec((8, w), lambda j, i: (j, 0))],
        out_shape=[jax.ShapeDtypeStruct(gst.shape, f32), jax.ShapeDtypeStruct((nsb * 8, w), f32),
                   jax.ShapeDtypeStruct((nsb * 8, w), f32)],
        scratch_shapes=[pltpu.VMEM((8, w), f32), pltpu.VMEM((16, w), f32)],
        compiler_params=_params(("parallel", "arbitrary")),
    )(gst, states, states, lam_re, lam_im)


def _s5_prep_fn(xs, ps):
    lam_re, lam_im, log_step, b_re, b_im, expand = ps
    lr = jnp.minimum(lam_re, S5_MAX_REAL)
    li = lam_im
    step = jnp.exp(log_step)
    er = jnp.exp(lr * step)
    ang = li * step
    lbr, lbi = er * jnp.cos(ang), er * jnp.sin(ang)
    nr, ni = lbr - 1.0, lbi
    den = lr * lr + li * li
    qr, qi = (nr * lr + ni * li) / den, (ni * lr - nr * li) / den
    qre, qie = _dot(qr, expand, "nn", _HI), _dot(qi, expand, "nn", _HI)
    return [lbr, lbi, qre * b_re - qie * b_im, qre * b_im + qie * b_re]


def _s5_prep(pars, name):
    def body(*refs):
        outs = _s5_prep_fn([], [r[...] for r in refs[:6]])
        for ref, v in zip(refs[6:], outs):
            ref[...] = v

    g, nst = pars[0].shape
    nc = pars[3].shape[1]
    return pl.pallas_call(
        body, name=name,
        out_shape=[jax.ShapeDtypeStruct((g, nst), f32)] * 2 + [jax.ShapeDtypeStruct((g, nc), f32)] * 2,
        compiler_params=_params(),
    )(*pars)


def _s5_prep_bwd(pars, cots, name):
    def body(*refs):
        ps = [r[...] for r in refs[:6]]
        ct = [r[...] for r in refs[6:10]]
        _, vjp = jax.vjp(lambda *a: _s5_prep_fn([], list(a)), *ps)
        g = vjp(ct)
        for ref, v in zip(refs[10:], g[:5]):
            ref[...] = v

    return pl.pallas_call(
        body, name=name, out_shape=[jax.ShapeDtypeStruct(p.shape, f32) for p in pars[:5]], compiler_params=_params(),
    )(*pars, *cots)


_ANY = pl.BlockSpec(memory_space=pl.ANY)


def _remote(src, dst, send_sem, recv_sem, device):
    return pltpu.make_async_remote_copy(src_ref=src, dst_ref=dst, send_sem=send_sem, recv_sem=recv_sem, device_id=device,
                                        device_id_type=MESH)


def _staged_copy(src, dst, buf, in_sems, out_sems):
    n = D2D_STREAMS
    piece = src.shape[0] // n
    assert src.shape[0] % n == 0

    def load(i):
        return pltpu.make_async_copy(src.at[pl.ds(i * piece, piece)], buf.at[i % 2], in_sems.at[i % 2])

    def store(i):
        return pltpu.make_async_copy(buf.at[i % 2], dst.at[pl.ds(i * piece, piece)], out_sems.at[i % 2])

    load(0).start()
    for i in range(n):
        if i + 1 < n:
            if i >= 1:
                store(i - 1).wait()
            load(i + 1).start()
        load(i).wait()
        store(i).start()
    store(n - 2).wait()
    store(n - 1).wait()


def _stage_scratch(rows, cols, dtype):
    return [pltpu.VMEM((2, rows // D2D_STREAMS, cols), dtype), pltpu.SemaphoreType.DMA((2,)), pltpu.SemaphoreType.DMA((2,))]


def _chip_all_gather(block, name):
    rows = block.shape[0]
    half = rows // 2
    piece = half // D2D_STREAMS
    assert rows % (2 * D2D_STREAMS * 16) == 0

    def body(src, out, ici_send, ici_recv, d2d_send, d2d_recv, *stage):
        x, y, c = lax.axis_index("x"), lax.axis_index("y"), lax.axis_index("c")
        me = 2 * x + y
        sibling = (x, y, 1 - c)
        chips = [(1 - x, y), (x, 1 - y), (1 - x, 1 - y)]
        mine = pl.ds(pl.multiple_of(c * half, 16), half)
        sends = []
        for j, (px, py) in enumerate(chips):
            cp = _remote(src.at[mine], out.at[me, mine], ici_send.at[j], ici_recv.at[j], (px, py, c))
            cp.start()
            sends.append(cp)
        _staged_copy(src, out.at[me], *stage)
        for j, (px, py) in enumerate(chips):
            slot = 2 * px + py
            _remote(src.at[mine], out.at[slot, mine], ici_send.at[j], ici_recv.at[j], (px, py, c)).wait_recv()
            for s in range(D2D_STREAMS):
                r = pl.ds(pl.multiple_of(c * half + s * piece, 16), piece)
                k = j * D2D_STREAMS + s
                cp = _remote(out.at[slot, r], out.at[slot, r], d2d_send.at[k], d2d_recv.at[k], sibling)
                cp.start()
                sends.append(cp)
        for j, (px, py) in enumerate(chips):
            slot = 2 * px + py
            for s in range(D2D_STREAMS):
                r = pl.ds(pl.multiple_of((1 - c) * half + s * piece, 16), piece)
                k = j * D2D_STREAMS + s
                _remote(out.at[slot, r], out.at[slot, r], d2d_send.at[k], d2d_recv.at[k], sibling).wait_recv()
        for cp in sends:
            cp.wait_send()

    n_d2d = 3 * D2D_STREAMS
    return pl.pallas_call(
        body, name=name, in_specs=[_ANY], out_specs=_ANY,
        out_shape=jax.ShapeDtypeStruct((N_CHIPS,) + block.shape, block.dtype),
        scratch_shapes=[pltpu.SemaphoreType.DMA((3,)), pltpu.SemaphoreType.DMA((3,)), pltpu.SemaphoreType.DMA((n_d2d,)),
                        pltpu.SemaphoreType.DMA((n_d2d,))] + _stage_scratch(rows, block.shape[1], block.dtype),
    )(block)


def _chip_scatter(parts, name):
    def body(src, out, send_sems, recv_sems, *stage):
        x, y, c = lax.axis_index("x"), lax.axis_index("y"), lax.axis_index("c")
        me = 2 * x + y
        chips = [(1 - x, y), (x, 1 - y), (1 - x, 1 - y)]
        sends = []
        for j, (px, py) in enumerate(chips):
            cp = pltpu.make_async_remote_copy(src_ref=src.at[2 * px + py], dst_ref=out.at[me], send_sem=send_sems.at[j],
                                              recv_sem=recv_sems.at[j], device_id=(px, py, c), device_id_type=MESH)
            cp.start()
            sends.append(cp)
        _staged_copy(src.at[me], out.at[me], *stage)
        for j, (px, py) in enumerate(chips):
            pltpu.make_async_remote_copy(src_ref=src.at[me], dst_ref=out.at[2 * px + py], send_sem=send_sems.at[j],
                                         recv_sem=recv_sems.at[j], device_id=(px, py, c), device_id_type=MESH).wait_recv()
        for cp in sends:
            cp.wait_send()

    return pl.pallas_call(
        body, name=name, in_specs=[_ANY], out_specs=_ANY, out_shape=jax.ShapeDtypeStruct(parts.shape, parts.dtype),
        scratch_shapes=[pltpu.SemaphoreType.DMA((3,)), pltpu.SemaphoreType.DMA((3,))]
        + _stage_scratch(parts.shape[1], parts.shape[2], parts.dtype),
    )(parts)


def _core_send_other_half(parts, name):
    n_slots, rows, cols = parts.shape
    half = rows // 2
    piece = half // D2D_STREAMS
    assert rows % (2 * D2D_STREAMS * 16) == 0

    def body(src, out, send_sems, recv_sems):
        x, y, c = lax.axis_index("x"), lax.axis_index("y"), lax.axis_index("c")
        sibling = (x, y, 1 - c)
        sends = []
        for k in range(n_slots):
            for s in range(D2D_STREAMS):
                theirs = pl.ds(pl.multiple_of((1 - c) * half + s * piece, 16), piece)
                i = k * D2D_STREAMS + s
                cp = _remote(src.at[k, theirs], out.at[k, pl.ds(s * piece, piece)], send_sems.at[i], recv_sems.at[i], sibling)
                cp.start()
                sends.append(cp)
        for cp in sends:
            cp.wait_recv()
        for cp in sends:
            cp.wait_send()

    n = n_slots * D2D_STREAMS
    return pl.pallas_call(
        body, name=name, in_specs=[_ANY], out_specs=_ANY, out_shape=jax.ShapeDtypeStruct((n_slots, half, cols), parts.dtype),
        scratch_shapes=[pltpu.SemaphoreType.DMA((n,)), pltpu.SemaphoreType.DMA((n,))],
    )(parts)


def _add_my_half(parts, other, core, name, block_rows=256):
    n_slots, rows, cols = parts.shape
    half = rows // 2
    br = _pick(half, block_rows, 16)
    nb = half // br

    def body(c_ref, a_ref, b_ref, o_ref):
        o_ref[...] = (a_ref[...].astype(f32) + b_ref[...].astype(f32)).astype(o_ref.dtype)

    grid_spec = pltpu.PrefetchScalarGridSpec(
        num_scalar_prefetch=1, grid=(n_slots, nb),
        in_specs=[pl.BlockSpec((1, br, cols), lambda k, i, c: (k, c[0] * nb + i, 0)),
                  pl.BlockSpec((1, br, cols), lambda k, i, c: (k, i, 0))],
        out_specs=pl.BlockSpec((1, br, cols), lambda k, i, c: (k, i, 0)))
    return pl.pallas_call(
        body, name=name, grid_spec=grid_spec, out_shape=jax.ShapeDtypeStruct((n_slots, half, cols), parts.dtype),
        compiler_params=_params(("parallel", "parallel")),
    )(core, parts, other)


def _core_join_halves(mine, name):
    half, cols = mine.shape
    piece = half // D2D_STREAMS
    assert half % (D2D_STREAMS * 16) == 0

    def body(src, out, send_sems, recv_sems, *stage):
        x, y, c = lax.axis_index("x"), lax.axis_index("y"), lax.axis_index("c")
        sibling = (x, y, 1 - c)
        sends = []
        for s in range(D2D_STREAMS):
            dst = out.at[pl.ds(pl.multiple_of(c * half + s * piece, 16), piece)]
            cp = _remote(src.at[pl.ds(s * piece, piece)], dst, send_sems.at[s], recv_sems.at[s], sibling)
            cp.start()
            sends.append(cp)
        _staged_copy(src, out.at[pl.ds(pl.multiple_of(c * half, 16), half)], *stage)
        for s in range(D2D_STREAMS):
            dst = out.at[pl.ds(pl.multiple_of((1 - c) * half + s * piece, 16), piece)]
            _remote(src.at[pl.ds(s * piece, piece)], dst, send_sems.at[s], recv_sems.at[s], sibling).wait_recv()
        for cp in sends:
            cp.wait_send()

    return pl.pallas_call(
        body, name=name, in_specs=[_ANY], out_specs=_ANY, out_shape=jax.ShapeDtypeStruct((2 * half, cols), mine.dtype),
        scratch_shapes=[pltpu.SemaphoreType.DMA((D2D_STREAMS,)), pltpu.SemaphoreType.DMA((D2D_STREAMS,))]
        + _stage_scratch(half, cols, mine.dtype),
    )(mine)


def _reduce_to_chips(parts, core, tag):
    chip_part = _add_my_half(parts, _core_send_other_half(parts, f"exchange_core_halves_{tag}"), core, f"sum_core_halves_{tag}")
    my_sum = _sum_slots(_chip_scatter(chip_part, f"scatter_{tag}"), f"sum_chip_parts_{tag}")
    return _core_join_halves(my_sum, f"join_core_halves_{tag}")


def _sum_slots(stack, name, block_rows=256):
    s_n, r_n, c_n = stack.shape
    br = _pick(r_n, block_rows, 8)

    def body(in_ref, o_ref):
        acc = in_ref[0].astype(f32)
        for s in range(1, s_n):
            acc = acc + in_ref[s].astype(f32)
        o_ref[...] = acc

    return pl.pallas_call(
        body, name=name, grid=(r_n // br,), in_specs=[pl.BlockSpec((s_n, br, c_n), lambda i: (0, i, 0))],
        out_specs=pl.BlockSpec((br, c_n), lambda i: (i, 0)), out_shape=jax.ShapeDtypeStruct((r_n, c_n), f32),
        compiler_params=_params(("parallel",)),
    )(stack)


def _concat_padded(parts, mult):
    rows = sum(p.shape[0] for p in parts)
    pad = (-rows) % mult
    if pad:
        parts = parts + [jnp.zeros((pad,) + parts[0].shape[1:], parts[0].dtype)]
    return jnp.concatenate(parts, axis=0)


def _pack_weights(w, conv_w):
    parts = [w[n].astype(bf16).reshape(-1, PACK_COLS) for n in BIG]
    parts.append(lax.bitcast_convert_type(conv_w, bf16).reshape(-1, PACK_COLS))
    return _concat_padded(parts, PACK_ROW_MULT)


def _unpack_weights(full, w, conv_w):
    start, r0 = {}, 0
    for n in BIG:
        start[n] = r0
        r0 += w[n].size // PACK_COLS

    def shards(n, l):
        rows = w[n][l].size // PACK_COLS
        first = start[n] + l * rows
        return [full[k, first:first + rows].reshape(w[n].shape[1:]) for k in range(N_CHIPS)]

    rows = conv_w.size * 2 // PACK_COLS
    pieces = lax.bitcast_convert_type(full[:, r0:r0 + rows].reshape((N_CHIPS,) + conv_w.shape + (2,)), f32)
    return shards, jnp.concatenate([pieces[k] for k in range(N_CHIPS)], axis=2)


def _pack_big_grads(layer_grads):
    parts, slot_rows = [], 0
    for k in range(N_CHIPS):
        slot = []
        for n in BIG:
            for g in layer_grads:
                width = g[n].shape[0] // N_CHIPS
                slot.append(g[n][k * width:(k + 1) * width].astype(bf16).reshape(-1, PACK_COLS))
        slot_rows = sum(p.shape[0] for p in slot)
        pad = (-slot_rows) % PACK_ROW_MULT
        if pad:
            slot.append(jnp.zeros((pad, PACK_COLS), bf16))
        slot_rows += pad
        parts += slot
    return jnp.concatenate(parts, axis=0).reshape(N_CHIPS, slot_rows, PACK_COLS)


def _unpack_big_grads(summed, w):
    out, r0 = {}, 0
    for n in BIG:
        rows = w[n].size // PACK_COLS
        out[n] = summed[r0:r0 + rows].reshape(w[n].shape)
        r0 += rows
    return out


def _pack_small(vals, names):
    parts, total = [], 0
    for n in names:
        pieces = vals[n] if isinstance(vals[n], list) else [vals[n]]
        size = sum(p.size for p in pieces)
        parts += [p.reshape(-1) for p in pieces]
        pad = (-size) % LANES
        if pad:
            parts.append(jnp.zeros((pad,), f32))
        total += size + pad
    pad = (-total) % (PACK_ROW_MULT * LANES)
    if pad:
        parts.append(jnp.zeros((pad,), f32))
    return jnp.concatenate(parts).reshape(-1, LANES)


def _unpack_small(packed, like, names):
    out, r0 = {}, 0
    for n in names:
        size = like[n].size
        rows = -(-size // LANES)
        out[n] = packed[r0:r0 + rows].reshape(-1)[:size].reshape(like[n].shape)
        r0 += rows
    return out


def _dims(w, x):
    d = {}
    d["D"] = x.shape[-1]
    d["T"] = x.shape[-2]
    d["DI"] = w["ssd_norm_g"].shape[-1]
    d["NH"] = w["ssd_dt_bias"].shape[-1]
    d["CD"] = w["ssd_conv_b"].shape[-1]
    d["G"] = SSD_N_GROUPS
    d["HPG"] = d["NH"] // d["G"]
    d["P"] = d["DI"] // d["NH"]
    d["N"] = (d["CD"] - d["DI"]) // (2 * d["G"])
    d["S5G"], d["S5N"] = w["s5_lambda_re"].shape[-2:]
    d["S5C"] = w["s5_b_re"].shape[-1]
    d["S5W"] = d["S5G"] * d["S5C"]
    d["NSB"] = d["S5W"] // S5_SUPERBLOCK
    d["GSB"] = d["S5G"] // d["NSB"]
    return d


def _head_pad(v, d):
    lead = v.shape[:-1]
    v = v.reshape(lead + (d["G"], d["HPG"]))
    v = jnp.concatenate([v, jnp.zeros(lead + (d["G"], LANES - d["HPG"]), v.dtype)], axis=-1)
    return v.reshape(lead + (d["G"] * LANES,))


def _head_unpad(v, d):
    lead = v.shape[:-1]
    return v.reshape(lead + (d["G"], LANES))[..., :d["HPG"]].reshape(lead + (d["NH"],))


def _w_in_perm(w_in_t, d):
    o = d["DI"] + d["CD"]
    dt = _head_pad(w_in_t[o:o + d["NH"]].T, d).T
    return jnp.concatenate([w_in_t[:o], w_in_t[o + d["NH"]:], dt], axis=0)


def _w_in_unperm(g, d):
    o = d["DI"] + d["CD"]
    rest = d["S5W"] + 2 * d["D"]
    return jnp.concatenate([g[:o], _head_unpad(g[o + rest:].T, d).T, g[o:o + rest]], axis=0)


def _s5_block_diag(v, d):
    gsb = d["GSB"]
    g, a, b = v.shape
    row_group = (lax.broadcasted_iota(jnp.int32, (g * a, gsb * b), 0) // a) % gsb
    col_group = lax.broadcasted_iota(jnp.int32, (g * a, gsb * b), 1) // b
    return jnp.where(row_group == col_group, jnp.tile(v.reshape(g * a, b), (1, gsb)), 0)


def _s5_diag_blocks(m, d, a, b):
    gsb = d["GSB"]
    rows = m.shape[0]
    m = m.reshape(rows, gsb, b)
    row_group = (lax.broadcasted_iota(jnp.int32, (rows, gsb, 1), 0) // a) % gsb
    col_group = lax.broadcasted_iota(jnp.int32, (rows, gsb, 1), 1)
    return jnp.sum(jnp.where(row_group == col_group, m, 0), axis=1).reshape(rows // a, a, b)


def _s5_lam_rows(v, d):
    v = v.reshape(d["NSB"], 1, d["GSB"] * d["S5N"])
    return jnp.broadcast_to(v, (d["NSB"], 8, v.shape[-1])).reshape(d["NSB"] * 8, -1)


def _ffn_fwd(h, pre_g, post_g, wgu, wd, tag):
    D = h.shape[1]
    H2 = wgu.shape[0]
    xn = _row_kernel(f"{tag}_norm", _fwd_of(_f_norm), [(h, D, 0)], [pre_g], [(D, bf16)])[0]
    ab = _mm(xn, wgu, "nt", f32, f"{tag}_mm_up")
    hid = _row_kernel(f"{tag}_swiglu", _swiglu_fwd, [(ab, H2, 0)], [], [(H2 // 2, bf16)])[0]
    f = _mm(hid, wd, "nn", f32, f"{tag}_mm_down")
    out = _row_kernel(f"{tag}_resnorm", _fwd_of(_f_resnorm(0.5)), [(h, D, 0), (f, D, 0)], [post_g], [(D, f32)])[0]
    return out, dict(h=h, xn=xn, ab=ab, hid=hid, f=f)


def _ffn_bwd(dh_out, s, pre_g, post_g, wgu, wd, tag):
    D = dh_out.shape[1]
    H2 = wgu.shape[0]
    df, dpost = _row_kernel(f"{tag}_resnorm_bwd", _vjp_of(_f_post(0.5), 1, 1, [0]), [(s["f"], D, 0), (dh_out, D, 0)],
                            [post_g], [(D, bf16)], [post_g.shape])
    dwd = _mm(s["hid"], df, "tn", bf16, f"{tag}_mm_dwd")
    dhid = _mm(df, wd, "nt", bf16, f"{tag}_mm_dhid")
    dab = _row_kernel(f"{tag}_swiglu_bwd", _swiglu_bwd, [(s["ab"], H2, 0), (dhid, H2 // 2, 0)], [], [(H2, bf16)])[0]
    dwgu = _mm(dab, s["xn"], "tn", bf16, f"{tag}_mm_dwgu")
    dxn = _mm(dab, wgu, "nn", f32, f"{tag}_mm_dxn")
    dh, dpre = _row_kernel(f"{tag}_norm_bwd", _vjp_of(_f_norm, 1, 1, [0], 1), [(s["h"], D, 0), (dxn, D, 0), (dh_out, D, 0)],
                           [pre_g], [(D, f32)], [pre_g.shape])
    return dh, dict(pre_g=dpre, post_g=dpost, wgu=dwgu, wd=dwd)


def _mixer_fwd(h, p, d):
    D, DI, CD, G, N = d["D"], d["DI"], d["CD"], d["G"], d["N"]
    gl = G * LANES
    c_u5, c_ga, c_gb, c_dt = DI + CD, DI + CD + d["S5W"], DI + CD + d["S5W"] + D, DI + CD + d["S5W"] + 2 * D
    u = _row_kernel("mix_norm", _fwd_of(_f_norm), [(h, D, 0)], [p["mix_pre_g"]], [(D, bf16)])[0]
    proj = _mm(u, p["w_in"], "nt", f32, "mix_mm_in", bn_t=512)
    act = _conv_fwd(proj, DI, p["conv_w"], p["conv_b"], "ssd_conv")
    dt, adt = _row_kernel("ssd_dt", _fwd_of(_f_dt), [(proj, gl, c_dt // gl)], [p["dt_bias"], p["a_log"]], [(gl, f32)] * 2)
    y_ssd, states = _ssd_fwd(act, dt, adt, p["d_skip"], d["HPG"], d["P"], N, "ssd_scan")
    nrm = _row_kernel("ssd_post", _fwd_of(_f_ssdpost(G)), [(y_ssd, DI, 0), (proj, DI, 0)], [p["norm_g"]], [(DI, bf16)])[0]
    y_a = _mm(nrm, p["w_a"], "nn", f32, "mix_mm_a")
    u5 =(proj, d["S5W"], c_u5 // d["S5W"])
    bu = _s5_in(proj, c_u5, p["bsb"], d)
    s5st = _s5_scan_fwd(bu, p["lam_re_rows"], p["lam_im_rows"], d["NSB"], "s5_scan")
    y5 = _bdmm(s5st, p["csb"], "nn", d["NSB"], f32, "s5_mm_c")
    gel = _row_kernel("s5_post", _fwd_of(_f_s5post), [(y5, d["S5W"], 0), u5], [p["s5_d"]], [(d["S5W"], bf16)])[0]
    vg = _mm(gel, p["w_glu"], "nt", f32, "mix_mm_glu")
    glu = _row_kernel("s5_glu", _glu_fwd, [(vg, vg.shape[1], 0)], [], [(vg.shape[1] // 2, bf16)])[0]
    y_b = _mm(glu, p["w_b"], "nn", f32, "mix_mm_b")
    merged = _row_kernel("mix_merge", _fwd_of(_f_merge), [(proj, D, c_ga // D), (y_a, D, 0), (proj, D, c_gb // D), (y_b, D, 0)],
                         [], [(D, bf16)])[0]
    m = _mm(merged, p["w_out"], "nn", f32, "mix_mm_out")
    out = _row_kernel("mix_resnorm", _fwd_of(_f_resnorm(1.0)), [(h, D, 0), (m, D, 0)], [p["mix_post_g"]], [(D, f32)])[0]
    return out, dict(h=h, u=u, proj=proj, act=act, dt=dt, adt=adt, states=states, y_ssd=y_ssd, nrm=nrm, y_a=y_a, s5st=s5st,
                     y5=y5, gel=gel, vg=vg, glu=glu, y_b=y_b, merged=merged, m=m)


def _s5_in(proj, c_u5, bsb, d):
    T = proj.shape[0]
    nsb = d["NSB"]
    ka, nw = S5_SUPERBLOCK, bsb.shape[1]
    off = c_u5 // ka
    assert c_u5 % ka == 0
    bt = _pick(T, 512)

    def body(a_ref, w_ref, o_ref):
        o_ref[...] = _dot(a_ref[...].astype(bf16), w_ref[...].astype(bf16))

    return pl.pallas_call(
        body, name="s5_mm_bu", grid=(nsb, T // bt),
        in_specs=[pl.BlockSpec((bt, ka), lambda j, i: (i, off + j)), pl.BlockSpec((ka, nw), lambda j, i: (j, 0))],
        out_specs=pl.BlockSpec((bt, nw), lambda j, i: (i, j)), out_shape=jax.ShapeDtypeStruct((T, nsb * nw), f32),
        compiler_params=_params(("parallel", "parallel")),
    )(proj, bsb)


def _s5_dbsb(proj, c_u5, a, d):
    T = proj.shape[0]
    nsb = d["NSB"]
    ka, nw = S5_SUPERBLOCK, a.shape[1] // nsb
    off = c_u5 // ka
    bt = _pick(T, 512)

    def body(u_ref, a_ref, o_ref):
        pr = _dot(u_ref[...].astype(bf16), a_ref[...].astype(bf16), "tn")
        k = pl.program_id(1)

        @pl.when(k == 0)
        def _():
            o_ref[...] = pr

        @pl.when(k > 0)
        def _():
            o_ref[...] += pr

    return pl.pallas_call(
        body, name="s5_mm_dbsb", grid=(nsb, T // bt),
        in_specs=[pl.BlockSpec((bt, ka), lambda j, k: (k, off + j)), pl.BlockSpec((bt, nw), lambda j, k: (k, j))],
        out_specs=pl.BlockSpec((ka, nw), lambda j, k: (j, 0)), out_shape=jax.ShapeDtypeStruct((nsb * ka, nw), f32),
        compiler_params=_params(("parallel", "arbitrary")),
    )(proj, a)


def _mixer_bwd(dh_out, s, p, d):
    D, DI, CD, G, N, S5W = d["D"], d["DI"], d["CD"], d["G"], d["N"], d["S5W"]
    gl = G * LANES
    gn = G * N
    c_u5, c_ga, c_gb, c_dt = DI + CD, DI + CD + S5W, DI + CD + S5W + D, DI + CD + S5W + 2 * D
    proj = s["proj"]
    g = {}
    dm, g["mix_post_g"] = _row_kernel("mix_resnorm_bwd", _vjp_of(_f_post(1.0), 1, 1, [0]), [(s["m"], D, 0), (dh_out, D, 0)],
                                      [p["mix_post_g"]], [(D, bf16)], [p["mix_post_g"].shape])
    g["w_out"] = _mm(s["merged"], dm, "tn", bf16, "mix_mm_dwout")
    dmerged = _mm(dm, p["w_out"], "nt", f32, "mix_mm_dmerged")
    dga, dya, dgb, dyb = _row_kernel(
        "mix_merge_bwd", _vjp_of(_f_merge, 4, 1, [0, 1, 2, 3]),
        [(proj, D, c_ga // D), (s["y_a"], D, 0), (proj, D, c_gb // D), (s["y_b"], D, 0), (dmerged, D, 0)], [],
        [(D, bf16), (D, bf16), (D, bf16), (D, bf16)])
    g["w_a"] = _mm(s["nrm"], dya, "tn", bf16, "mix_mm_dwa")
    dnrm = _mm(dya, p["w_a"], "nt", f32, "mix_mm_dnrm")
    dy_ssd, dz, g["norm_g"] = _row_kernel(
        "ssd_post_bwd", _vjp_of(_f_ssdpost(G), 2, 1, [0, 1]), [(s["y_ssd"], DI, 0), (proj, DI, 0), (dnrm, DI, 0)],
        [p["norm_g"]], [(DI, f32), (DI, bf16)], [p["norm_g"].shape])
    dxs, d_b, d_c, ddt, dadt, dd = _ssd_bwd(s["act"], s["dt"], s["adt"], p["d_skip"], s["states"], dy_ssd,
                                            d["HPG"], d["P"], N, "ssd_scan_bwd")
    g["d_skip"] = dd.reshape(G, 8, LANES)[:, 0, :].reshape(1, gl)
    ddt_raw, g["dt_bias"], g["a_log"] = _row_kernel(
        "ssd_dt_bwd", _vjp_of(_f_dt, 1, 2, [0]), [(proj, gl, c_dt // gl), (ddt, gl, 0), (dadt, gl, 0)],
        [p["dt_bias"], p["a_log"]], [(gl, bf16)], [p["dt_bias"].shape, p["a_log"].shape])
    cw, cb = p["conv_w"], p["conv_b"]
    dxc_x, dw_x, db_x = _conv_bwd(proj, DI, cw[:, :DI], cb[:, :DI], dxs, "ssd_conv_bwd_x")
    dxc_b, dw_b, db_b = _conv_bwd(proj, 2 * DI, cw[:, DI:DI + gn], cb[:, DI:DI + gn], d_b, "ssd_conv_bwd_b")
    dxc_c, dw_c, db_c = _conv_bwd(proj, 2 * DI + gn, cw[:, DI + gn:], cb[:, DI + gn:], d_c, "ssd_conv_bwd_c")
    g["conv_w"] = jnp.concatenate([dw_x, dw_b, dw_c], axis=1)
    g["conv_b"] = jnp.concatenate([db_x, db_b, db_c], axis=1)
    g["w_b"] = _mm(s["glu"], dyb, "tn", bf16, "mix_mm_dwb")
    dglu = _mm(dyb, p["w_b"], "nt", f32, "mix_mm_dglu")
    dvg = _row_kernel("s5_glu_bwd", _glu_bwd, [(s["vg"], s["vg"].shape[1], 0), (dglu, S5W, 0)], [], [(s["vg"].shape[1], bf16)])[0]
    g["w_glu"] = _mm(dvg, s["gel"], "tn", bf16, "mix_mm_dwglu")
    dgel = _mm(dvg, p["w_glu"], "nn", f32, "mix_mm_dgel")
    dy5, du5a, g["s5_d"] = _row_kernel(
        "s5_post_bwd", _vjp_of(_f_s5post, 2, 1, [0, 1]), [(s["y5"], S5W, 0), (proj, S5W, c_u5 // S5W), (dgel, S5W, 0)],
        [p["s5_d"]], [(S5W, bf16), (S5W, f32)], [p["s5_d"].shape])
    g["csb"] = _bdmm(s["s5st"], dy5, "tn", d["NSB"], f32, "s5_mm_dcsb")
    gst = _bdmm(dy5, p["csb"], "nt", d["NSB"], f32, "s5_mm_gst")
    a, g["lam_re_rows"], g["lam_im_rows"] = _s5_scan_bwd(gst, s["s5st"], p["lam_re_rows"], p["lam_im_rows"], d["NSB"], "s5_scan_bwd")
    g["bsb"] = _s5_dbsb(proj, c_u5, a, d)
    du5b = _bdmm(a, p["bsb"], "nt", d["NSB"], f32, "s5_mm_du5")
    du5 = _row_kernel("s5_du5", _add_fn, [(du5a, S5W, 0), (du5b, S5W, 0)], [], [(S5W, bf16)])[0]
    dproj = jnp.concatenate([dz, dxc_x, dxc_b, dxc_c, du5, dga, dgb, ddt_raw], axis=1)
    g["w_in"] = _mm(dproj, s["u"], "tn", bf16, "mix_mm_dwin")
    du = _mm(dproj, p["w_in"], "nn", f32, "mix_mm_du", bk_t=2176)
    dh, g["mix_pre_g"] = _row_kernel("mix_norm_bwd", _vjp_of(_f_norm, 1, 1, [0], 1), [(s["h"], D, 0), (du, D, 0), (dh_out, D, 0)],
                                     [p["mix_pre_g"]], [(D, f32)], [p["mix_pre_g"].shape])
    return dh, g


def _layer_params(l, w, wf, conv_w_full, d):
    r2 = lambda v: v[l].reshape(1, -1)
    p = {}
    for n in ["ffn1_pre_g", "ffn1_post_g", "mix_pre_g", "mix_post_g", "ffn2_pre_g", "ffn2_post_g", "s5_d"]:
        p[n] = r2(w[n])
    whole = lambda *names: jnp.concatenate([s for n in names for s in wf(n, l)], axis=0)
    p["wgu1"] = whole("ffn1_w_gate", "ffn1_w_up")
    p["wd1"] = whole("ffn1_w_down")
    p["wgu2"] = whole("ffn2_w_gate", "ffn2_w_up")
    p["wd2"] = whole("ffn2_w_down")
    p["w_in"] = _w_in_perm(whole("w_in"), d)
    p["w_a"], p["w_glu"], p["w_b"], p["w_out"] = whole("w_branch_a"), whole("s5_w_glu"), whole("w_branch_b"), whole("w_out")
    p["conv_w"] = conv_w_full[l]
    p["conv_b"] = r2(w["ssd_conv_b"])
    p["dt_bias"] = _head_pad(r2(w["ssd_dt_bias"]), d)
    p["a_log"] = _head_pad(r2(w["ssd_a_log"]), d)
    p["d_skip"] = _head_pad(r2(w["ssd_d"]), d)
    p["norm_g"] = r2(w["ssd_norm_g"])
    g5, n5, c5 = d["S5G"], d["S5N"], d["S5C"]
    expand = jnp.repeat(jnp.eye(n5, dtype=f32), c5, axis=1)
    prep_in = [w["s5_lambda_re"][l], w["s5_lambda_im"][l], w["s5_log_step"][l].reshape(g5, 1),
               w["s5_b_re"][l].reshape(g5, n5 * c5), w["s5_b_im"][l].reshape(g5, n5 * c5), expand]
    lbr, lbi, bbr, bbi = _s5_prep(prep_in, "s5_prep")
    p["s5_prep_in"] = prep_in
    p["lam_re_rows"], p["lam_im_rows"] = _s5_lam_rows(lbr, d), _s5_lam_rows(lbi, d)
    to_cn = lambda v: v.reshape(g5, n5, c5).transpose(0, 2, 1)
    p["bsb"] = jnp.concatenate([_s5_block_diag(to_cn(bbr), d), _s5_block_diag(to_cn(bbi), d)], axis=1).astype(bf16)
    c_re, c_im = w["s5_c_re"][l].transpose(0, 2, 1), w["s5_c_im"][l].transpose(0, 2, 1)
    nsb = d["NSB"]
    csb = jnp.stack([_s5_block_diag(c_re, d).reshape(nsb, -1, S5_SUPERBLOCK),
                     _s5_block_diag(-c_im, d).reshape(nsb, -1, S5_SUPERBLOCK)], axis=1)
    p["csb"] = csb.reshape(-1, S5_SUPERBLOCK).astype(bf16)
    return p


def _s5_param_grads(g, p, d, l):
    g5, n5, c5, nsb, gsb = d["S5G"], d["S5N"], d["S5C"], d["NSB"], d["GSB"]
    wst = gsb * n5
    dbsb = g["bsb"]
    from_cn = lambda v: v.transpose(0, 2, 1).reshape(g5, n5 * c5)
    dbbr = from_cn(_s5_diag_blocks(dbsb[:, :wst], d, c5, n5))
    dbbi = from_cn(_s5_diag_blocks(dbsb[:, wst:], d, c5, n5))
    rows = lambda v: v.reshape(nsb, 8, wst)[:, 0, :].reshape(g5, n5)
    cots = [rows(g["lam_re_rows"]), rows(g["lam_im_rows"]), dbbr, dbbi]
    dlr, dli, dls, dbr, dbi = _s5_prep_bwd(p["s5_prep_in"], cots, "s5_prep_bwd")
    dcsb = g["csb"].reshape(nsb, 2, wst, S5_SUPERBLOCK)
    dcr = _s5_diag_blocks(dcsb[:, 0].reshape(-1, S5_SUPERBLOCK), d, n5, c5).transpose(0, 2, 1)
    dci = -_s5_diag_blocks(dcsb[:, 1].reshape(-1, S5_SUPERBLOCK), d, n5, c5).transpose(0, 2, 1)
    return dict(s5_lambda_re=dlr, s5_lambda_im=dli, s5_log_step=dls.reshape(g5), s5_b_re=dbr.reshape(g5, n5, c5),
                s5_b_im=dbi.reshape(g5, n5, c5), s5_c_re=dcr, s5_c_im=dci)


def kernel(x, ffn1_pre_g, ffn1_post_g, ffn1_w_gate, ffn1_w_up, ffn1_w_down, mix_pre_g, mix_post_g, w_in, ssd_conv_w, ssd_conv_b, ssd_dt_bias, ssd_a_log, ssd_d, ssd_norm_g, w_branch_a, s5_lambda_re, s5_lambda_im, s5_b_re, s5_b_im, s5_c_re, s5_c_im, s5_log_step, s5_d, s5_w_glu, w_branch_b, w_out, ffn2_pre_g, ffn2_post_g, ffn2_w_gate, ffn2_w_up, ffn2_w_down, loss_target, m_ffn1_pre_g, m_ffn1_post_g, m_ffn1_w_gate, m_ffn1_w_up, m_ffn1_w_down, m_mix_pre_g, m_mix_post_g, m_w_in, m_ssd_conv_w, m_ssd_conv_b, m_ssd_dt_bias, m_ssd_a_log, m_ssd_d, m_ssd_norm_g, m_w_branch_a, m_s5_lambda_re, m_s5_lambda_im, m_s5_b_re, m_s5_b_im, m_s5_c_re, m_s5_c_im, m_s5_log_step, m_s5_d, m_s5_w_glu, m_w_branch_b, m_w_out, m_ffn2_pre_g, m_ffn2_post_g, m_ffn2_w_gate, m_ffn2_w_up, m_ffn2_w_down, v_ffn1_pre_g, v_ffn1_post_g, v_ffn1_w_gate, v_ffn1_w_up, v_ffn1_w_down, v_mix_pre_g, v_mix_post_g, v_w_in, v_ssd_conv_w, v_ssd_conv_b, v_ssd_dt_bias, v_ssd_a_log, v_ssd_d, v_ssd_norm_g, v_w_branch_a, v_s5_lambda_re, v_s5_lambda_im, v_s5_b_re, v_s5_b_im, v_s5_c_re, v_s5_c_im, v_s5_log_step, v_s5_d, v_s5_w_glu, v_w_branch_b, v_w_out, v_ffn2_pre_g, v_ffn2_post_g, v_ffn2_w_gate, v_ffn2_w_up, v_ffn2_w_down):
    given = dict(locals())
    for n in COL_SHARDED:
        for prefix in ("", "m_", "v_"):
            given[prefix + n] = given[prefix + n].transpose(0, 2, 1)
    w = {n: given[n] for n in WEIGHTS}
    mom = {n: given["m_" + n] for n in WEIGHTS}
    var = {n: given["v_" + n] for n in WEIGHTS}
    d = _dims(w, x)
    n_layers = w["ffn1_pre_g"].shape[0]
    T, D = d["T"], d["D"]

    gathered = _chip_all_gather(_pack_weights(w, w["ssd_conv_w"]), "gather_weights")
    wf, conv_w_full = _unpack_weights(gathered, w, w["ssd_conv_w"])
    layers = [_layer_params(l, w, wf, conv_w_full, d) for l in range(n_layers)]

    h = x.reshape(T, D)
    saved = []
    for p in layers:
        h, s1 = _ffn_fwd(h, p["ffn1_pre_g"], p["ffn1_post_g"], p["wgu1"], p["wd1"], "ffn1")
        h, sm = _mixer_fwd(h, p, d)
        h, s2 = _ffn_fwd(h, p["ffn2_pre_g"], p["ffn2_post_g"], p["wgu2"], p["wd2"], "ffn2")
        saved.append((s1, sm, s2))
    dh, loss_part = _row_kernel("loss", _loss_fn, [(h, D, 0), (loss_target.reshape(T, D), D, 0)], [], [(D, f32)], [(8, LANES)])
    loss = lax.psum(loss_part[0, 0], ("x", "y", "c"))

    lg = [None] * n_layers
    for l in reversed(range(n_layers)):
        p = layers[l]
        s1, sm, s2 = saved[l]
        dh, g2 = _ffn_bwd(dh, s2, p["ffn2_pre_g"], p["ffn2_post_g"], p["wgu2"], p["wd2"], "ffn2")
        dh, gm = _mixer_bwd(dh, sm, p, d)
        dh, g1 = _ffn_bwd(dh, s1, p["ffn1_pre_g"], p["ffn1_post_g"], p["wgu1"], p["wd1"], "ffn1")
        H = p["wd1"].shape[0]
        gl = dict(ffn1_pre_g=g1["pre_g"], ffn1_post_g=g1["post_g"], ffn1_w_gate=g1["wgu"][:H], ffn1_w_up=g1["wgu"][H:],
                  ffn1_w_down=g1["wd"], ffn2_pre_g=g2["pre_g"], ffn2_post_g=g2["post_g"], ffn2_w_gate=g2["wgu"][:H],
                  ffn2_w_up=g2["wgu"][H:], ffn2_w_down=g2["wd"], mix_pre_g=gm["mix_pre_g"], mix_post_g=gm["mix_post_g"],
                  w_in=_w_in_unperm(gm["w_in"], d), ssd_conv_w=gm["conv_w"], ssd_conv_b=gm["conv_b"],
                  ssd_dt_bias=_head_unpad(gm["dt_bias"], d), ssd_a_log=_head_unpad(gm["a_log"], d),
                  ssd_d=_head_unpad(gm["d_skip"], d), ssd_norm_g=gm["norm_g"], w_branch_a=gm["w_a"], s5_d=gm["s5_d"],
                  s5_w_glu=gm["w_glu"], w_branch_b=gm["w_b"], w_out=gm["w_out"])
        gl.update(_s5_param_grads(gm, p, d, l))
        lg[l] = gl
    grad_x = dh.reshape(x.shape)
    my_core = lax.axis_index("c").astype(jnp.int32).reshape(1)
    grads = _unpack_big_grads(_reduce_to_chips(_pack_big_grads(lg), my_core, "grads"), w)
    small_names = SMALL + ["ssd_conv_w"]
    small_parts = {n: [g[n] for g in lg] for n in small_names}
    small_like = {n: jax.ShapeDtypeStruct((n_layers,) + lg[0][n].shape, f32) for n in small_names}
    small_like.update({n: w[n] for n in SMALL})
    small_pack = _pack_small(small_parts, small_names)
    small_sum = _reduce_to_chips(jnp.broadcast_to(small_pack, (N_CHIPS,) + small_pack.shape), my_core, "small")
    small = _unpack_small(small_sum, small_like, small_names)
    k_me = 2 * lax.axis_index("x") + lax.axis_index("y")
    cw = w["ssd_conv_w"].shape[-1]
    small["ssd_conv_w"] = lax.dynamic_slice_in_dim(small["ssd_conv_w"], k_me * cw, cw, axis=2)
    grads.update(small)

    delta, new_m, new_v = {}, {}, {}
    for n in BIG + ["ssd_conv_w"]:
        delta[n], new_m[n], new_v[n] = _adamw(w[n], grads[n], mom[n], var[n], "adamw_" + n)
    packs = [_pack_small(t, SMALL) for t in (w, grads, mom, var)]
    sd, sm_, sv = _adamw(*packs, "adamw_small")
    delta.update(_unpack_small(sd, w, SMALL))
    new_m.update(_unpack_small(sm_, w, SMALL))
    new_v.update(_unpack_small(sv, w, SMALL))
    for n in COL_SHARDED:
        for out in (grads, delta, new_m, new_v):
            out[n] = out[n].transpose(0, 2, 1)
    return (loss, grad_x, *[grads[n] for n in WEIGHTS], *[delta[n] for n in WEIGHTS],
            *[new_m[n] for n in WEIGHTS], *[new_v[n] for n in WEIGHTS])
```

```python
import functools

import numpy as np
import jax
import jax.numpy as jnp
from jax import lax
from jax.experimental import pallas as pl
from jax.experimental.pallas import tpu as pltpu

f32, bf16 = jnp.float32, jnp.bfloat16

SSD_N_GROUPS = 4
SSD_CHUNK = 128
RMS_EPS = 1e-6
S5_MAX_REAL = -1e-4
S5_SUPERBLOCK = 256
ADAM_LR, ADAM_B1, ADAM_B2, ADAM_EPS, ADAM_WD, ADAM_STEP = 0.001, 0.9, 0.999, 1e-08, 0.01, 10

LANES = 128
PACK_COLS = 1024
D2D_STREAMS = 16
PACK_ROW_MULT = 2 * D2D_STREAMS * 16
VMEM_LIMIT_BYTES = 48 * 1024 * 1024
N_CHIPS, N_CORES, N_DEV = 4, 2, 8
MESH = pl.DeviceIdType.MESH

BIG = ["ffn1_w_gate", "ffn1_w_up", "ffn1_w_down", "w_in", "w_branch_a", "s5_w_glu", "w_branch_b", "w_out",
       "ffn2_w_gate", "ffn2_w_up", "ffn2_w_down"]
COL_SHARDED = ["ffn1_w_gate", "ffn1_w_up", "w_in", "s5_w_glu", "ffn2_w_gate", "ffn2_w_up"]
SMALL = ["ffn1_pre_g", "ffn1_post_g", "mix_pre_g", "mix_post_g", "ssd_conv_b", "ssd_norm_g", "s5_lambda_re", "s5_lambda_im",
         "s5_b_re", "s5_b_im", "s5_c_re", "s5_c_im", "s5_d", "ffn2_pre_g", "ffn2_post_g", "s5_log_step", "ssd_dt_bias",
         "ssd_a_log", "ssd_d"]
WEIGHTS = ["ffn1_pre_g", "ffn1_post_g", "ffn1_w_gate", "ffn1_w_up", "ffn1_w_down", "mix_pre_g", "mix_post_g", "w_in",
           "ssd_conv_w", "ssd_conv_b", "ssd_dt_bias", "ssd_a_log", "ssd_d", "ssd_norm_g", "w_branch_a", "s5_lambda_re",
           "s5_lambda_im", "s5_b_re", "s5_b_im", "s5_c_re", "s5_c_im", "s5_log_step", "s5_d", "s5_w_glu", "w_branch_b",
           "w_out", "ffn2_pre_g", "ffn2_post_g", "ffn2_w_gate", "ffn2_w_up", "ffn2_w_down"]


def _params(sem=None):
    return pltpu.CompilerParams(dimension_semantics=sem, vmem_limit_bytes=VMEM_LIMIT_BYTES)


def _pick(n, target, mult=LANES):
    best = None
    for d in range(mult, min(n, target) + 1, mult):
        if n % d == 0:
            best = d
    return best if best is not None else n


_DIMS = {"nn": (((1,), (0,)), ((), ())), "nt": (((1,), (1,)), ((), ())), "tn": (((0,), (0,)), ((), ()))}


def _mm(a, b, mode, out_dtype, name, bm_t=1024, bn_t=1024, bk_t=2816):
    if mode == "nn":
        (M, K), (K2, N) = a.shape, b.shape
    elif mode == "nt":
        (M, K), (N, K2) = a.shape, b.shape
    else:
        (K, M), (K2, N) = a.shape, b.shape
    assert K == K2, (name, a.shape, b.shape)
    bm, bn, bk = _pick(M, bm_t), _pick(N, bn_t), _pick(K, bk_t)
    nk = K // bk
    dn = _DIMS[mode]

    def body(a_ref, b_ref, o_ref, *scratch):
        p = lax.dot_general(a_ref[...].astype(bf16), b_ref[...].astype(bf16), dn, preferred_element_type=f32)
        if nk == 1:
            o_ref[...] = p.astype(o_ref.dtype)
        else:
            acc = scratch[0]
            k = pl.program_id(2)

            @pl.when(k == 0)
            def _():
                acc[...] = p

            @pl.when(k > 0)
            def _():
                acc[...] += p

            @pl.when(k == nk - 1)
            def _():
                o_ref[...] = acc[...].astype(o_ref.dtype)

    if mode == "tn":
        a_spec = pl.BlockSpec((bk, bm), lambda i, j, k: (k, i))
    else:
        a_spec = pl.BlockSpec((bm, bk), lambda i, j, k: (i, k))
    if mode == "nt":
        b_spec = pl.BlockSpec((bn, bk), lambda i, j, k: (j, k))
    else:
        b_spec = pl.BlockSpec((bk, bn), lambda i, j, k: (k, j))
    return pl.pallas_call(
        body, name=name, grid=(M // bm, N // bn, nk), in_specs=[a_spec, b_spec],
        out_specs=pl.BlockSpec((bm, bn), lambda i, j, k: (i, j)), out_shape=jax.ShapeDtypeStruct((M, N), out_dtype),
        scratch_shapes=[pltpu.VMEM((bm, bn), f32)] if nk > 1 else [],
        compiler_params=_params(("parallel", "parallel", "arbitrary")),
    )(a, b)


def _bdmm(a, w, mode, nb, out_dtype, name, bt_t=512):
    if mode == "tn":
        T = a.shape[0]
        ka, nw = a.shape[1] // nb, w.shape[1] // nb
        bt = _pick(T, bt_t)
        nt = T // bt

        def body_tn(a_ref, b_ref, o_ref):
            p = lax.dot_general(a_ref[...].astype(bf16), b_ref[...].astype(bf16), _DIMS["tn"], preferred_element_type=f32)
            k = pl.program_id(1)

            @pl.when(k == 0)
            def _():
                o_ref[...] = p

            @pl.when(k > 0)
            def _():
                o_ref[...] += p

        return pl.pallas_call(
            body_tn, name=name, grid=(nb, nt),
            in_specs=[pl.BlockSpec((bt, ka), lambda j, k: (k, j)), pl.BlockSpec((bt, nw), lambda j, k: (k, j))],
            out_specs=pl.BlockSpec((ka, nw), lambda j, k: (j, 0)), out_shape=jax.ShapeDtypeStruct((nb * ka, nw), f32),
            compiler_params=_params(("parallel", "arbitrary")),
        )(a, w)
    T = a.shape[0]
    ka, nw = w.shape[0] // nb, w.shape[1]
    bt = _pick(T, bt_t)
    kin, kout = (ka, nw) if mode == "nn" else (nw, ka)
    dn = _DIMS[mode]

    def body(a_ref, w_ref, o_ref):
        o_ref[...] = lax.dot_general(a_ref[...].astype(bf16), w_ref[...].astype(bf16), dn,
                                     preferred_element_type=f32).astype(o_ref.dtype)

    return pl.pallas_call(
        body, name=name, grid=(nb, T // bt),
        in_specs=[pl.BlockSpec((bt, kin), lambda j, i: (i, j)), pl.BlockSpec((ka, nw), lambda j, i: (j, 0))],
        out_specs=pl.BlockSpec((bt, kout), lambda j, i: (i, j)), out_shape=jax.ShapeDtypeStruct((T, nb * kout), out_dtype),
        compiler_params=_params(("parallel", "parallel")),
    )(a, w)


def _row_index(i, cb):
    return (i, cb)


def _row_kernel(name, fn, rows, pars, row_outs, par_outs=(), block_rows=256):
    T = rows[0][0].shape[0]
    R = min(block_rows, T)
    assert T % R == 0
    nr, npar, nro = len(rows), len(pars), len(row_outs)

    def body(*refs):
        rv = [r[...] for r in refs[:nr]]
        pv = [r[...] for r in refs[nr:nr + npar]]
        ro, po = fn(rv, pv)
        for ref, v in zip(refs[nr + npar:nr + npar + nro], ro):
            ref[...] = v.astype(ref.dtype)
        if par_outs:
            i = pl.program_id(0)
            prefs = refs[nr + npar + nro:]

            @pl.when(i == 0)
            def _():
                for ref, v in zip(prefs, po):
                    ref[...] = v.astype(f32)

            @pl.when(i > 0)
            def _():
                for ref, v in zip(prefs, po):
                    ref[...] += v.astype(f32)

    in_specs = [pl.BlockSpec((R, nc), functools.partial(_row_index, cb=cb)) for (_, nc, cb) in rows]
    in_specs += [pl.BlockSpec(p.shape, lambda i: (0, 0)) for p in pars]
    out_specs = [pl.BlockSpec((R, nc), lambda i: (i, 0)) for (nc, _) in row_outs]
    out_specs += [pl.BlockSpec(s, lambda i: (0, 0)) for s in par_outs]
    out_shape = [jax.ShapeDtypeStruct((T, nc), dt) for (nc, dt) in row_outs]
    out_shape += [jax.ShapeDtypeStruct(s, f32) for s in par_outs]
    outs = pl.pallas_call(
        body, name=name, grid=(T // R,), in_specs=in_specs, out_specs=out_specs, out_shape=out_shape,
        compiler_params=_params(("arbitrary",) if par_outs else ("parallel",)),
    )(*[r[0] for r in rows], *pars)
    return list(outs)


def _fwd_of(f):
    def fn(rv, pv):
        return f([v.astype(f32) for v in rv], [v.astype(f32) for v in pv]), []
    return fn


def _vjp_of(f, n_x, n_cot, grad_idx, n_add=0):
    def fn(rv, pv):
        xs = [v.astype(f32) for v in rv[:n_x]]
        cots = [v.astype(f32) for v in rv[n_x:n_x + n_cot]]
        adds = rv[n_x + n_cot:n_x + n_cot + n_add]
        ps = [v.astype(f32) for v in pv]
        _, vjp = jax.vjp(lambda *a: f(list(a[:n_x]), list(a[n_x:])), *xs, *ps)
        g = vjp(cots)
        row_g = [g[i] for i in grad_idx]
        for k, a in enumerate(adds):
            row_g[k] = row_g[k] + a.astype(f32)
        return row_g, list(g[n_x:])
    return fn


def _rms(x, g):
    return x * lax.rsqrt(jnp.mean(x * x, axis=-1, keepdims=True) + RMS_EPS) * g


def _f_norm(xs, ps):
    return [_rms(xs[0], ps[0])]


def _f_post(scale):
    def f(xs, ps):
        return [scale * _rms(xs[0], ps[0])]
    return f


def _f_resnorm(scale):
    def f(xs, ps):
        return [xs[0] + scale * _rms(xs[1], ps[0])]
    return f


def _f_dt(xs, ps):
    dt = jax.nn.softplus(xs[0] + ps[0])
    return [dt, -jnp.exp(ps[1]) * dt]


def _f_ssdpost(n_groups):
    def f(xs, ps):
        y = xs[0] * jax.nn.silu(xs[1])
        width = y.shape[-1] // n_groups
        lane = lax.broadcasted_iota(jnp.int32, y.shape, 1)
        scale = jnp.zeros_like(y)
        for k in range(n_groups):
            m = ((lane >= k * width) & (lane < (k + 1) * width)).astype(f32)
            ms = jnp.sum(y * y * m, axis=-1, keepdims=True) / width
            scale = scale + lax.rsqrt(ms + RMS_EPS) * m
        return [y * scale * ps[0]]
    return f


def _f_s5post(xs, ps):
    return [jax.nn.gelu(xs[0] + ps[0] * xs[1])]


def _f_merge(xs, ps):
    return [jax.nn.sigmoid(xs[0]) * xs[1] + jax.nn.sigmoid(xs[2]) * xs[3]]


def _swiglu_fwd(rv, pv):
    ab = rv[0].astype(f32)
    h = ab.shape[1] // 2
    return [jax.nn.silu(ab[:, :h]) * ab[:, h:]], []


def _swiglu_bwd(rv, pv):
    ab, d = rv[0].astype(f32), rv[1].astype(f32)
    h = ab.shape[1] // 2
    a, b = ab[:, :h], ab[:, h:]
    s = jax.nn.sigmoid(a)
    return [jnp.concatenate([d * b * (s * (1.0 + a * (1.0 - s))), d * (a * s)], axis=1)], []


def _glu_fwd(rv, pv):
    vg = rv[0].astype(f32)
    h = vg.shape[1] // 2
    return [vg[:, :h] * jax.nn.sigmoid(vg[:, h:])], []


def _glu_bwd(rv, pv):
    vg, d = rv[0].astype(f32), rv[1].astype(f32)
    h = vg.shape[1] // 2
    s = jax.nn.sigmoid(vg[:, h:])
    return [jnp.concatenate([d * s, d * vg[:, :h] * s * (1.0 - s)], axis=1)], []


def _loss_fn(rv, pv):
    e = rv[0].astype(f32) - rv[1].astype(f32)
    per_tok = jnp.mean(e * e, axis=-1, keepdims=True)
    part = 0.5 * jnp.sum(per_tok, axis=0, keepdims=True)
    return [e / e.shape[-1]], [jnp.broadcast_to(part, (8, LANES))]


def _add_fn(rv, pv):
    return [rv[0].astype(f32) + rv[1].astype(f32)], []


def _adamw_fn(rv, pv):
    w, g, m, v = [x.astype(f32) for x in rv]
    m = ADAM_B1 * m + (1.0 - ADAM_B1) * g
    v = ADAM_B2 * v + (1.0 - ADAM_B2) * (g * g)
    m_hat = m / (1.0 - ADAM_B1 ** ADAM_STEP)
    v_hat = v / (1.0 - ADAM_B2 ** ADAM_STEP)
    return [-ADAM_LR * (m_hat / (jnp.sqrt(v_hat) + ADAM_EPS) + ADAM_WD * w), m, v], []


def _adamw(w, g, m, v, name):
    shape = w.shape
    cols = shape[-1] if (w.ndim >= 2 and shape[-1] >= LANES) else None
    if cols is None:
        n = int(np.prod(shape))
        cols = LANES if n % LANES == 0 else n
    n_rows = int(np.prod(shape)) // cols
    br, bc = _pick(n_rows, 256, 8), cols
    if br < 64 and cols % LANES == 0:
        br, bc = n_rows, LANES

    def body(w_ref, g_ref, m_ref, v_ref, d_ref, nm_ref, nv_ref):
        outs, _ = _adamw_fn([w_ref[...], g_ref[...], m_ref[...], v_ref[...]], [])
        d_ref[...], nm_ref[...], nv_ref[...] = outs

    spec = pl.BlockSpec((br, bc), lambda i, j: (i, j))
    outs = pl.pallas_call(
        body, name=name, grid=(n_rows // br, cols // bc), in_specs=[spec] * 4, out_specs=[spec] * 3,
        out_shape=[jax.ShapeDtypeStruct((n_rows, cols), f32)] * 3, compiler_params=_params(("parallel", "parallel")),
    )(*[t.reshape(n_rows, cols) for t in (w, g, m, v)])
    return [o.reshape(shape) for o in outs]


def _shift_down(x, s, row):
    if s == 0:
        return x
    return jnp.where(row >= s, pltpu.roll(x, s, 0), 0.0)


def _shift_up(x, s, row):
    if s == 0:
        return x
    n = x.shape[0]
    return jnp.where(row < n - s, pltpu.roll(x, n - s, 0), 0.0)


def _conv_pre(x, w, b, row):
    kw = w.shape[0]
    c = b
    for k in range(kw):
        c = c + w[k:k + 1, :] * _shift_down(x, kw - 1 - k, row)
    return c


def _conv_fwd(xsrc, col0, w, b, name, bc_t=512):
    T = xsrc.shape[0]
    kw, ncols = w.shape
    bc = _pick(ncols, bc_t)
    off = col0 // bc
    assert col0 % bc == 0

    def body(x_ref, w_ref, b_ref, o_ref):
        x = x_ref[...].astype(f32)
        row = lax.broadcasted_iota(jnp.int32, x.shape, 0)
        c = _conv_pre(x, w_ref[...], b_ref[...], row)
        o_ref[...] = c * jax.nn.sigmoid(c)

    return pl.pallas_call(
        body, name=name, grid=(ncols // bc,),
        in_specs=[pl.BlockSpec((T, bc), lambda j: (0, off + j)), pl.BlockSpec((kw, bc), lambda j: (0, j)),
                  pl.BlockSpec((1, bc), lambda j: (0, j))],
        out_specs=pl.BlockSpec((T, bc), lambda j: (0, j)), out_shape=jax.ShapeDtypeStruct((T, ncols), f32),
        compiler_params=_params(("parallel",)),
    )(xsrc, w, b)


def _conv_bwd(xsrc, col0, w, b, dact, name, bc_t=512):
    T = xsrc.shape[0]
    kw, ncols = w.shape
    bc = _pick(ncols, bc_t)
    off = col0 // bc
    assert col0 % bc == 0

    def body(x_ref, w_ref, b_ref, d_ref, dx_ref, dw_ref, db_ref):
        x = x_ref[...].astype(f32)
        w = w_ref[...]
        row = lax.broadcasted_iota(jnp.int32, x.shape, 0)
        c = _conv_pre(x, w, b_ref[...], row)
        s = jax.nn.sigmoid(c)
        dc = d_ref[...].astype(f32) * (s * (1.0 + c * (1.0 - s)))
        dx = jnp.zeros_like(x)
        dws = []
        for k in range(kw):
            dx = dx + w[k:k + 1, :] * _shift_up(dc, kw - 1 - k, row)
            dws.append(jnp.sum(dc * _shift_down(x, kw - 1 - k, row), axis=0, keepdims=True))
        dx_ref[...] = dx.astype(dx_ref.dtype)
        dw_ref[...] = jnp.concatenate(dws, axis=0)
        db_ref[...] = jnp.sum(dc, axis=0, keepdims=True)

    return pl.pallas_call(
        body, name=name, grid=(ncols // bc,),
        in_specs=[pl.BlockSpec((T, bc), lambda j: (0, off + j)), pl.BlockSpec((kw, bc), lambda j: (0, j)),
                  pl.BlockSpec((1, bc), lambda j: (0, j)), pl.BlockSpec((T, bc), lambda j: (0, j))],
        out_specs=[pl.BlockSpec((T, bc), lambda j: (0, j)), pl.BlockSpec((kw, bc), lambda j: (0, j)),
                   pl.BlockSpec((1, bc), lambda j: (0, j))],
        out_shape=[jax.ShapeDtypeStruct((T, ncols), bf16), jax.ShapeDtypeStruct((kw, ncols), f32),
                   jax.ShapeDtypeStruct((1, ncols), f32)],
        compiler_params=_params(("parallel",)),
    )(xsrc, w, b, dact)


_HI = lax.Precision.HIGHEST


def _dot(a, b, dims="nn", precision=None):
    return lax.dot_general(a, b, _DIMS[dims], preferred_element_type=f32, precision=precision)


def _ssd_common(x_ref, b_ref, c_ref, dt_ref, adt_ref, d_ref, hpg, p):
    q = b_ref.shape[0]
    hp = hpg * p
    bb, cb = b_ref[...].astype(bf16), c_ref[...].astype(bf16)
    r = lax.broadcasted_iota(jnp.int32, (q, q), 0)
    s = lax.broadcasted_iota(jnp.int32, (q, q), 1)
    tril = r >= s
    trilf = tril.astype(f32)
    eh = lax.broadcasted_iota(jnp.int32, (LANES, hp), 0)
    ec = lax.broadcasted_iota(jnp.int32, (LANES, hp), 1)
    expand = ((ec >= eh * p) & (ec < (eh + 1) * p)).astype(f32)
    adt = adt_ref[...]
    cum = _dot(trilf, adt, "nn", _HI)
    cum_t = _dot(adt, (r <= s).astype(f32), "tn", _HI)
    cum_e = _dot(cum, expand, "nn", _HI)
    dt_e = _dot(dt_ref[...], expand, "nn", _HI)
    d_e = _dot(jnp.broadcast_to(d_ref[...], (8, LANES)), expand, "nn", _HI)[0:1, :]
    gmat = _dot(cb, bb, "nt")
    x = x_ref[...]
    xdt = x * dt_e
    e_all = jnp.exp(cum_e)
    dec = jnp.exp(cum_e[q - 1:q, :] - cum_e)
    lms, ms = [], []
    for h in range(hpg):
        lm = jnp.exp(jnp.where(tril, cum[:, h:h + 1] - cum_t[h:h + 1, :], -1e30))
        lms.append(lm)
        ms.append(gmat * lm)
    et = [jnp.exp(cum[q - 1:q, h:h + 1]) for h in range(hpg)]
    return dict(bb=bb, cb=cb, trilf=trilf, expand=expand, cum=cum, x=x, xdt=xdt, dt_e=dt_e, d_e=d_e, e=e_all, dec=dec,
                lms=lms, ms=ms, et=et)


def _ssd_specs(q, hp, n, g_n, nc, rev):
    def cidx(c):
        return (nc - 1 - c) if rev else c
    x_spec = pl.BlockSpec((q, hp), lambda g, c: (cidx(c), g))
    boff = (g_n * hp) // n
    b_spec = pl.BlockSpec((q, n), lambda g, c: (cidx(c), boff + g))
    c_spec = pl.BlockSpec((q, n), lambda g, c: (cidx(c), boff + g_n + g))
    dt_spec = pl.BlockSpec((q, LANES), lambda g, c: (cidx(c), g))
    d_spec = pl.BlockSpec((1, LANES), lambda g, c: (0, g))
    st_spec = pl.BlockSpec((1, 1, hp, n), lambda g, c: (cidx(c), g, 0, 0))
    return x_spec, b_spec, c_spec, dt_spec, d_spec, st_spec


def _ssd_fwd(act, dt, adt, dpad, hpg, p, n, name):
    T = act.shape[0]
    g_n, q = SSD_N_GROUPS, SSD_CHUNK
    nc, hp = T // q, hpg * p
    x_spec, b_spec, c_spec, dt_spec, d_spec, st_spec = _ssd_specs(q, hp, n, g_n, nc, False)

    def body(x_ref, b_ref, c_ref, dt_ref, adt_ref, d_ref, y_ref, st_ref, s_scr):
        @pl.when(pl.program_id(1) == 0)
        def _():
            s_scr[...] = jnp.zeros_like(s_scr)

        k = _ssd_common(x_ref, b_ref, c_ref, dt_ref, adt_ref, d_ref, hpg, p)
        s0 = s_scr[...]
        st_ref[0, 0] = s0
        xdtb = k["xdt"].astype(bf16)
        ydiag = [_dot(k["ms"][h].astype(bf16), xdtb[:, h * p:(h + 1) * p]) for h in range(hpg)]
        z = _dot(k["cb"], s0.astype(bf16), "nt")
        y_ref[...] = jnp.concatenate(ydiag, axis=1) + k["e"] * z + k["d_e"] * k["x"]
        upd = _dot((k["xdt"] * k["dec"]).astype(bf16), k["bb"], "tn")
        for h in range(hpg):
            s_scr[h * p:(h + 1) * p, :] = k["et"][h] * s0[h * p:(h + 1) * p, :] + upd[h * p:(h + 1) * p, :]

    return pl.pallas_call(
        body, name=name, grid=(g_n, nc),
        in_specs=[x_spec, b_spec, c_spec, dt_spec, dt_spec, d_spec],
        out_specs=[pl.BlockSpec((q, hp), lambda g, c: (c, g)), st_spec],
        out_shape=[jax.ShapeDtypeStruct((T, g_n * hp), f32), jax.ShapeDtypeStruct((nc, g_n, hp, n), f32)],
        scratch_shapes=[pltpu.VMEM((hp, n), f32)],
        compiler_params=_params(("parallel", "arbitrary")),
    )(act, act, act, dt, adt, dpad)


def _ssd_bwd(act, dt, adt, dpad, states, dy, hpg, p, n, name):
    T = act.shape[0]
    g_n, q = SSD_N_GROUPS, SSD_CHUNK
    nc, hp = T // q, hpg * p
    x_spec, b_spec, c_spec, dt_spec, d_spec, st_spec = _ssd_specs(q, hp, n, g_n, nc, True)

    def body(x_ref, b_ref, c_ref, dt_ref, adt_ref, d_ref, st_ref, dy_ref,
             dx_ref, db_ref, dc_ref, ddt_ref, dadt_ref, dd_ref, ds_scr):
        first = pl.program_id(1) == 0

        @pl.when(first)
        def _():
            ds_scr[...] = jnp.zeros_like(ds_scr)

        k = _ssd_common(x_ref, b_ref, c_ref, dt_ref, adt_ref, d_ref, hpg, p)
        bb, cb, expand, x, xdt, dec = k["bb"], k["cb"], k["expand"], k["x"], k["xdt"], k["dec"]
        heads = lambda t: _dot(t, expand, "nt", _HI)
        s0 = st_ref[0, 0]
        ds1 = ds_scr[...]
        s0b, ds1b = s0.astype(bf16), ds1.astype(bf16)
        dy = dy_ref[...]
        dyb, xdtb = dy.astype(bf16), xdt.astype(bf16)
        lane = lax.broadcasted_iota(jnp.int32, (1, LANES), 1)
        dg = jnp.zeros((q, q), f32)
        w_rows = jnp.zeros((q, LANES), f32)
        w_cols, dxdt_parts = [], []
        for h in range(hpg):
            hs = slice(h * p, (h + 1) * p)
            dm = _dot(dyb[:, hs], xdtb[:, hs], "nt")
            dg = dg + dm * k["lms"][h]
            wm = dm * k["ms"][h]
            w_rows = w_rows + jnp.sum(wm, axis=1, keepdims=True) * (lane == h).astype(f32)
            w_cols.append(jnp.sum(wm, axis=0, keepdims=True))
            dxdt_parts.append(_dot(k["ms"][h].astype(bf16), dyb[:, hs], "tn"))
        dxdt_diag = jnp.concatenate(dxdt_parts, axis=1)
        w_cols = jnp.concatenate(w_cols + [jnp.zeros((LANES - hpg, q), f32)], axis=0).T
        dgb = dg.astype(bf16)
        z = _dot(cb, s0b, "nt")
        dz = dy * k["e"]
        dzb = dz.astype(bf16)
        dxd = _dot(bb, ds1b, "nt")
        ddec = dxd * xdt * dec
        db_ref[...] = _dot(dgb, cb, "tn") + _dot((xdt * dec).astype(bf16), ds1b)
        dc_ref[...] = _dot(dgb, bb) + _dot(dzb, s0b)
        ds0 = _dot(dzb, cb, "tn")
        for h in range(hpg):
            hs = slice(h * p, (h + 1) * p)
            ds_scr[hs, :] = ds0[hs, :] + k["et"][h] * ds1[hs, :]
        dxdt = dxdt_diag + dxd * dec
        ddec_h = heads(ddec)
        dcum = w_rows - w_cols + heads(dz * z) - ddec_h
        et_row = jnp.exp(k["cum"][q - 1:q, :])
        dsum = _dot(jnp.ones((8, n), f32), _dot(expand, ds1 * s0, "nn", _HI), "nt", _HI)[0:1, :]
        dcl = dsum * et_row + jnp.sum(ddec_h, axis=0, keepdims=True)
        rowq = lax.broadcasted_iota(jnp.int32, (q, 1), 0)
        dcum = dcum + (rowq == q - 1).astype(f32) * dcl
        ddt_ref[...] = heads(dxdt * x)
        dadt_ref[...] = _dot(k["trilf"], dcum, "tn", _HI)
        dx_ref[...] = k["d_e"] * dy + dxdt * k["dt_e"]
        dd8 = heads(jnp.broadcast_to(jnp.sum(dy * x, axis=0, keepdims=True), (8, hp)))

        @pl.when(first)
        def _():
            dd_ref[...] = dd8

        @pl.when(jnp.logical_not(first))
        def _():
            dd_ref[...] += dd8

    rc = lambda g, c: (nc - 1 - c, g)
    return pl.pallas_call(
        body, name=name, grid=(g_n, nc),
        in_specs=[x_spec, b_spec, c_spec, dt_spec, dt_spec, d_spec, st_spec, pl.BlockSpec((q, hp), rc)],
        out_specs=[pl.BlockSpec((q, hp), rc), pl.BlockSpec((q, n), rc), pl.BlockSpec((q, n), rc),
                   pl.BlockSpec((q, LANES), rc), pl.BlockSpec((q, LANES), rc), pl.BlockSpec((8, LANES), lambda g, c: (g, 0))],
        out_shape=[jax.ShapeDtypeStruct((T, g_n * hp), f32), jax.ShapeDtypeStruct((T, g_n * n), f32),
                   jax.ShapeDtypeStruct((T, g_n * n), f32), jax.ShapeDtypeStruct((T, g_n * LANES), f32),
                   jax.ShapeDtypeStruct((T, g_n * LANES), f32), jax.ShapeDtypeStruct((g_n * 8, LANES), f32)],
        scratch_shapes=[pltpu.VMEM((hp, n), f32)],
        compiler_params=_params(("parallel", "arbitrary")),
    )(act, act, act, dt, adt, dpad, states, dy)


def _cmul(ar, ai, br, bi):
    return ar * br - ai * bi, ar * bi + ai * br


def _s5_tile_powers(lr, li):
    p = [(lr, li)]
    for _ in range(7):
        p.append(_cmul(p[-1][0], p[-1][1], lr, li))
    tile = (jnp.concatenate([q[0] for q in p], axis=0), jnp.concatenate([q[1] for q in p], axis=0))
    return tile, (p[0], p[1], p[3])


def _s5_tile_scan(xr, xi, steps, reverse):
    row = lax.broadcasted_iota(jnp.int32, xr.shape, 0)
    for d, (pr, pi) in zip((1, 2, 4), steps):
        if reverse:
            keep = row < 8 - d
            sr, si = pltpu.roll(xr, 8 - d, 0), pltpu.roll(xi, 8 - d, 0)
        else:
            keep = row >= d
            sr, si = pltpu.roll(xr, d, 0), pltpu.roll(xi, d, 0)
        sr, si = jnp.where(keep, sr, 0.0), jnp.where(keep, si, 0.0)
        ar, ai = _cmul(sr, si, pr, pi)
        xr, xi = xr + ar, xi + ai
    return xr, xi


def _s5_scan_fwd(bu, lam_re, lam_im, nsb, name, tc_t=512):
    T = bu.shape[0]
    w2 = bu.shape[1] // nsb
    w = w2 // 2
    tc = _pick(T, tc_t, 8)

    def body(bu_ref, lr_ref, li_ref, st_ref, carry):
        @pl.when(pl.program_id(1) == 0)
        def _():
            carry[...] = jnp.zeros_like(carry)

        (pr8, pi8), steps = _s5_tile_powers(lr_ref[0:1, :], li_ref[0:1, :])

        def tile(i, c):
            r = pl.ds(pl.multiple_of(i * 8, 8), 8)
            x = bu_ref[r, :]
            xr, xi = _s5_tile_scan(x[:, :w], x[:, w:], steps, False)
            ar, ai = _cmul(pr8, pi8, c[0], c[1])
            xr, xi = xr + ar, xi + ai
            st_ref[r, :] = jnp.concatenate([xr, xi], axis=1)
            return xr[7:8, :], xi[7:8, :]

        c = lax.fori_loop(0, tc // 8, tile, (carry[0:1, :], carry[1:2, :]), unroll=2)
        carry[0:1, :] = c[0]
        carry[1:2, :] = c[1]

    return pl.pallas_call(
        body, name=name, grid=(nsb, T // tc),
        in_specs=[pl.BlockSpec((tc, w2), lambda j, i: (i, j)), pl.BlockSpec((8, w), lambda j, i: (j, 0)),
                  pl.BlockSpec((8, w), lambda j, i: (j, 0))],
        out_specs=pl.BlockSpec((tc, w2), lambda j, i: (i, j)), out_shape=jax.ShapeDtypeStruct(bu.shape, f32),
        scratch_shapes=[pltpu.VMEM((8, w), f32)],
        compiler_params=_params(("parallel", "arbitrary")),
    )(bu, lam_re, lam_im)


def _s5_scan_bwd(gst, states, lam_re, lam_im, nsb, name, tc_t=512):
    T = gst.shape[0]
    w2 = gst.shape[1] // nsb
    w = w2 // 2
    tc = _pick(T, tc_t, 8)
    nt = T // tc
    n_tiles = tc // 8

    def body(g_ref, s_ref, sp_ref, lr_ref, li_ref, a_ref, dlr_ref, dli_ref, carry, acc):
        chunk = pl.program_id(1)

        @pl.when(chunk == 0)
        def _():
            carry[...] = jnp.zeros_like(carry)
            acc[...] = jnp.zeros_like(acc)

        (qr8, qi8), steps = _s5_tile_powers(lr_ref[0:1, :], -li_ref[0:1, :])
        row = lax.broadcasted_iota(jnp.int32, (8, w), 0)
        rev_r, rev_i = jnp.zeros((8, w), f32), jnp.zeros((8, w), f32)
        for r in range(8):
            rev_r = jnp.where(row == r, qr8[7 - r:8 - r, :], rev_r)
            rev_i = jnp.where(row == r, qi8[7 - r:8 - r, :], rev_i)
        row2 = lax.broadcasted_iota(jnp.int32, (8, w2), 0)

        def tile(k, c):
            ar_in, ai_in, dr, di = c
            i = n_tiles - 1 - k
            r = pl.ds(pl.multiple_of(i * 8, 8), 8)
            x = g_ref[r, :]
            xr, xi = _s5_tile_scan(x[:, :w], x[:, w:], steps, True)
            pr, pi = _cmul(rev_r, rev_i, ar_in, ai_in)
            xr, xi = xr + pr, xi + pi
            a_ref[r, :] = jnp.concatenate([xr, xi], axis=1)
            before = jnp.where(i > 0, s_ref[pl.ds(pl.multiple_of(jnp.maximum(i - 1, 0) * 8, 8), 8), :],
                               sp_ref[tc - 8:tc, :] * (chunk < nt - 1).astype(f32))
            prev = jnp.where(row2 == 0, pltpu.roll(before, 1, 0), pltpu.roll(s_ref[r, :], 1, 0))
            spr, spi = prev[:, :w], prev[:, w:]
            return xr[0:1, :], xi[0:1, :], dr + xr * spr + xi * spi, di - xr * spi + xi * spr

        c0 = (carry[0:1, :], carry[1:2, :], acc[0:8, :], acc[8:16, :])
        ar, ai, dr, di = lax.fori_loop(0, n_tiles, tile, c0, unroll=2)
        carry[0:1, :] = ar
        carry[1:2, :] = ai
        acc[0:8, :] = dr
        acc[8:16, :] = di
        dlr_ref[...] = jnp.broadcast_to(jnp.sum(dr, axis=0, keepdims=True), (8, w))
        dli_ref[...] = jnp.broadcast_to(jnp.sum(di, axis=0, keepdims=True), (8, w))

    cur = lambda j, i: (nt - 1 - i, j)
    prv = lambda j, i: (jnp.maximum(nt - 2 - i, 0), j)
    return pl.pallas_call(
        body, name=name, grid=(nsb, nt),
        in_specs=[pl.BlockSpec((tc, w2), cur), pl.BlockSpec((tc, w2), cur), pl.BlockSpec((tc, w2), prv),
                  pl.BlockSpec((8, w), lambda j, i: (j, 0)), pl.BlockSpec((8, w), lambda j, i: (j, 0))],
        out_specs=[pl.BlockSpec((tc, w2), cur), pl.BlockSpec((8, w), lambda j, i: (j, 0)),
                   pl.BlockSpec((8, w), lambda j, i: (j, 0))],
        out_shape=[jax.ShapeDtypeStruct(gst.shape, f32), jax.ShapeDtypeStruct((nsb * 8, w), f32),
                   jax.ShapeDtypeStruct((nsb * 8, w), f32)],
        scratch_shapes=[pltpu.VMEM((8, w), f32), pltpu.VMEM((16, w), f32)],
        compiler_params=_params(("parallel", "arbitrary")),
    )(gst, states, states, lam_re, lam_im)


def _s5_prep_fn(xs, ps):
    lam_re, lam_im, log_step, b_re, b_im, expand = ps
    lr = jnp.minimum(lam_re, S5_MAX_REAL)
    li = lam_im
    step = jnp.exp(log_step)
    er = jnp.exp(lr * step)
    ang = li * step
    lbr, lbi = er * jnp.cos(ang), er * jnp.sin(ang)
    nr, ni = lbr - 1.0, lbi
    den = lr * lr + li * li
    qr, qi = (nr * lr + ni * li) / den, (ni * lr - nr * li) / den
    qre, qie = _dot(qr, expand, "nn", _HI), _dot(qi, expand, "nn", _HI)
    return [lbr, lbi, qre * b_re - qie * b_im, qre * b_im + qie * b_re]


def _s5_prep(pars, name):
    def body(*refs):
        outs = _s5_prep_fn([], [r[...] for r in refs[:6]])
        for ref, v in zip(refs[6:], outs):
            ref[...] = v

    g, nst = pars[0].shape
    nc = pars[3].shape[1]
    return pl.pallas_call(
        body, name=name,
        out_shape=[jax.ShapeDtypeStruct((g, nst), f32)] * 2 + [jax.ShapeDtypeStruct((g, nc), f32)] * 2,
        compiler_params=_params(),
    )(*pars)


def _s5_prep_bwd(pars, cots, name):
    def body(*refs):
        ps = [r[...] for r in refs[:6]]
        ct = [r[...] for r in refs[6:10]]
        _, vjp = jax.vjp(lambda *a: _s5_prep_fn([], list(a)), *ps)
        g = vjp(ct)
        for ref, v in zip(refs[10:], g[:5]):
            ref[...] = v

    return pl.pallas_call(
        body, name=name, out_shape=[jax.ShapeDtypeStruct(p.shape, f32) for p in pars[:5]], compiler_params=_params(),
    )(*pars, *cots)


_ANY = pl.BlockSpec(memory_space=pl.ANY)


def _remote(src, dst, send_sem, recv_sem, device):
    return pltpu.make_async_remote_copy(src_ref=src, dst_ref=dst, send_sem=send_sem, recv_sem=recv_sem, device_id=device,
                                        device_id_type=MESH)


def _staged_copy(src, dst, buf, in_sems, out_sems):
    n = D2D_STREAMS
    piece = src.shape[0] // n
    assert src.shape[0] % n == 0

    def load(i):
        return pltpu.make_async_copy(src.at[pl.ds(i * piece, piece)], buf.at[i % 2], in_sems.at[i % 2])

    def store(i):
        return pltpu.make_async_copy(buf.at[i % 2], dst.at[pl.ds(i * piece, piece)], out_sems.at[i % 2])

    load(0).start()
    for i in range(n):
        if i + 1 < n:
            if i >= 1:
                store(i - 1).wait()
            load(i + 1).start()
        load(i).wait()
        store(i).start()
    store(n - 2).wait()
    store(n - 1).wait()


def _stage_scratch(rows, cols, dtype):
    return [pltpu.VMEM((2, rows // D2D_STREAMS, cols), dtype), pltpu.SemaphoreType.DMA((2,)), pltpu.SemaphoreType.DMA((2,))]


def _chip_all_gather(block, name):
    rows = block.shape[0]
    half = rows // 2
    piece = half // D2D_STREAMS
    assert rows % (2 * D2D_STREAMS * 16) == 0

    def body(src, out, ici_send, ici_recv, d2d_send, d2d_recv, *stage):
        x, y, c = lax.axis_index("x"), lax.axis_index("y"), lax.axis_index("c")
        me = 2 * x + y
        sibling = (x, y, 1 - c)
        chips = [(1 - x, y), (x, 1 - y), (1 - x, 1 - y)]
        mine = pl.ds(pl.multiple_of(c * half, 16), half)
        sends = []
        for j, (px, py) in enumerate(chips):
            cp = _remote(src.at[mine], out.at[me, mine], ici_send.at[j], ici_recv.at[j], (px, py, c))
            cp.start()
            sends.append(cp)
        _staged_copy(src, out.at[me], *stage)
        for j, (px, py) in enumerate(chips):
            slot = 2 * px + py
            _remote(src.at[mine], out.at[slot, mine], ici_send.at[j], ici_recv.at[j], (px, py, c)).wait_recv()
            for s in range(D2D_STREAMS):
                r = pl.ds(pl.multiple_of(c * half + s * piece, 16), piece)
                k = j * D2D_STREAMS + s
                cp = _remote(out.at[slot, r], out.at[slot, r], d2d_send.at[k], d2d_recv.at[k], sibling)
                cp.start()
                sends.append(cp)
        for j, (px, py) in enumerate(chips):
            slot = 2 * px + py
            for s in range(D2D_STREAMS):
                r = pl.ds(pl.multiple_of((1 - c) * half + s * piece, 16), piece)
                k = j * D2D_STREAMS + s
                _remote(out.at[slot, r], out.at[slot, r], d2d_send.at[k], d2d_recv.at[k], sibling).wait_recv()
        for cp in sends:
            cp.wait_send()

    n_d2d = 3 * D2D_STREAMS
    return pl.pallas_call(
        body, name=name, in_specs=[_ANY], out_specs=_ANY,
        out_shape=jax.ShapeDtypeStruct((N_CHIPS,) + block.shape, block.dtype),
        scratch_shapes=[pltpu.SemaphoreType.DMA((3,)), pltpu.SemaphoreType.DMA((3,)), pltpu.SemaphoreType.DMA((n_d2d,)),
                        pltpu.SemaphoreType.DMA((n_d2d,))] + _stage_scratch(rows, block.shape[1], block.dtype),
    )(block)


def _chip_scatter(parts, name):
    def body(src, out, send_sems, recv_sems, *stage):
        x, y, c = lax.axis_index("x"), lax.axis_index("y"), lax.axis_index("c")
        me = 2 * x + y
        chips = [(1 - x, y), (x, 1 - y), (1 - x, 1 - y)]
        sends = []
        for j, (px, py) in enumerate(chips):
            cp = pltpu.make_async_remote_copy(src_ref=src.at[2 * px + py], dst_ref=out.at[me], send_sem=send_sems.at[j],
                                              recv_sem=recv_sems.at[j], device_id=(px, py, c), device_id_type=MESH)
            cp.start()
            sends.append(cp)
        _staged_copy(src.at[me], out.at[me], *stage)
        for j, (px, py) in enumerate(chips):
            pltpu.make_async_remote_copy(src_ref=src.at[me], dst_ref=out.at[2 * px + py], send_sem=send_sems.at[j],
                                         recv_sem=recv_sems.at[j], device_id=(px, py, c), device_id_type=MESH).wait_recv()
        for cp in sends:
            cp.wait_send()

    return pl.pallas_call(
        body, name=name, in_specs=[_ANY], out_specs=_ANY, out_shape=jax.ShapeDtypeStruct(parts.shape, parts.dtype),
        scratch_shapes=[pltpu.SemaphoreType.DMA((3,)), pltpu.SemaphoreType.DMA((3,))]
        + _stage_scratch(parts.shape[1], parts.shape[2], parts.dtype),
    )(parts)


def _core_send_other_half(parts, name):
    n_slots, rows, cols = parts.shape
    half = rows // 2
    piece = half // D2D_STREAMS
    assert rows % (2 * D2D_STREAMS * 16) == 0

    def body(src, out, send_sems, recv_sems):
        x, y, c = lax.axis_index("x"), lax.axis_index("y"), lax.axis_index("c")
        sibling = (x, y, 1 - c)
        sends = []
        for k in range(n_slots):
            for s in range(D2D_STREAMS):
                theirs = pl.ds(pl.multiple_of((1 - c) * half + s * piece, 16), piece)
                i = k * D2D_STREAMS + s
                cp = _remote(src.at[k, theirs], out.at[k, pl.ds(s * piece, piece)], send_sems.at[i], recv_sems.at[i], sibling)
                cp.start()
                sends.append(cp)
        for cp in sends:
            cp.wait_recv()
        for cp in sends:
            cp.wait_send()

    n = n_slots * D2D_STREAMS
    return pl.pallas_call(
        body, name=name, in_specs=[_ANY], out_specs=_ANY, out_shape=jax.ShapeDtypeStruct((n_slots, half, cols), parts.dtype),
        scratch_shapes=[pltpu.SemaphoreType.DMA((n,)), pltpu.SemaphoreType.DMA((n,))],
    )(parts)


def _add_my_half(parts, other, core, name, block_rows=256):
    n_slots, rows, cols = parts.shape
    half = rows // 2
    br = _pick(half, block_rows, 16)
    nb = half // br

    def body(c_ref, a_ref, b_ref, o_ref):
        o_ref[...] = (a_ref[...].astype(f32) + b_ref[...].astype(f32)).astype(o_ref.dtype)

    grid_spec = pltpu.PrefetchScalarGridSpec(
        num_scalar_prefetch=1, grid=(n_slots, nb),
        in_specs=[pl.BlockSpec((1, br, cols), lambda k, i, c: (k, c[0] * nb + i, 0)),
                  pl.BlockSpec((1, br, cols), lambda k, i, c: (k, i, 0))],
        out_specs=pl.BlockSpec((1, br, cols), lambda k, i, c: (k, i, 0)))
    return pl.pallas_call(
        body, name=name, grid_spec=grid_spec, out_shape=jax.ShapeDtypeStruct((n_slots, half, cols), parts.dtype),
        compiler_params=_params(("parallel", "parallel")),
    )(core, parts, other)


def _core_join_halves(mine, name):
    half, cols = mine.shape
    piece = half // D2D_STREAMS
    assert half % (D2D_STREAMS * 16) == 0

    def body(src, out, send_sems, recv_sems, *stage):
        x, y, c = lax.axis_index("x"), lax.axis_index("y"), lax.axis_index("c")
        sibling = (x, y, 1 - c)
        sends = []
        for s in range(D2D_STREAMS):
            dst = out.at[pl.ds(pl.multiple_of(c * half + s * piece, 16), piece)]
            cp = _remote(src.at[pl.ds(s * piece, piece)], dst, send_sems.at[s], recv_sems.at[s], sibling)
            cp.start()
            sends.append(cp)
        _staged_copy(src, out.at[pl.ds(pl.multiple_of(c * half, 16), half)], *stage)
        for s in range(D2D_STREAMS):
            dst = out.at[pl.ds(pl.multiple_of((1 - c) * half + s * piece, 16), piece)]
            _remote(src.at[pl.ds(s * piece, piece)], dst, send_sems.at[s], recv_sems.at[s], sibling).wait_recv()
        for cp in sends:
            cp.wait_send()

    return pl.pallas_call(
        body, name=name, in_specs=[_ANY], out_specs=_ANY, out_shape=jax.ShapeDtypeStruct((2 * half, cols), mine.dtype),
        scratch_shapes=[pltpu.SemaphoreType.DMA((D2D_STREAMS,)), pltpu.SemaphoreType.DMA((D2D_STREAMS,))]
        + _stage_scratch(half, cols, mine.dtype),
    )(mine)


def _reduce_to_chips(parts, core, tag):
    chip_part = _add_my_half(parts, _core_send_other_half(parts, f"exchange_core_halves_{tag}"), core, f"sum_core_halves_{tag}")
    my_sum = _sum_slots(_chip_scatter(chip_part, f"scatter_{tag}"), f"sum_chip_parts_{tag}")
    return _core_join_halves(my_sum, f"join_core_halves_{tag}")


def _sum_slots(stack, name, block_rows=256):
    s_n, r_n, c_n = stack.shape
    br = _pick(r_n, block_rows, 8)

    def body(in_ref, o_ref):
        acc = in_ref[0].astype(f32)
        for s in range(1, s_n):
            acc = acc + in_ref[s].astype(f32)
        o_ref[...] = acc

    return pl.pallas_call(
        body, name=name, grid=(r_n // br,), in_specs=[pl.BlockSpec((s_n, br, c_n), lambda i: (0, i, 0))],
        out_specs=pl.BlockSpec((br, c_n), lambda i: (i, 0)), out_shape=jax.ShapeDtypeStruct((r_n, c_n), f32),
        compiler_params=_params(("parallel",)),
    )(stack)


def _concat_padded(parts, mult):
    rows = sum(p.shape[0] for p in parts)
    pad = (-rows) % mult
    if pad:
        parts = parts + [jnp.zeros((pad,) + parts[0].shape[1:], parts[0].dtype)]
    return jnp.concatenate(parts, axis=0)


def _pack_weights(w, conv_w):
    parts = [w[n].astype(bf16).reshape(-1, PACK_COLS) for n in BIG]
    conv = lax.bitcast_convert_type(conv_w, bf16).reshape(-1, PACK_COLS)
    parts.append(_concat_padded([conv], 16))
    return _concat_padded(parts, PACK_ROW_MULT)


def _unpack_weights(full, w, conv_w):
    start, r0 = {}, 0
    for n in BIG:
        start[n] = r0
        r0 += w[n].size // PACK_COLS

    def shards(n, l):
        rows = w[n][l].size // PACK_COLS
        first = start[n] + l * rows
        return [full[k, first:first + rows].reshape(w[n].shape[1:]) for k in range(N_CHIPS)]

    rows = conv_w.size * 2 // PACK_COLS
    pieces = lax.bitcast_convert_type(full[:, r0:r0 + rows].reshape((N_CHIPS,) + conv_w.shape + (2,)), f32)
    return shards, jnp.concatenate([pieces[k] for k in range(N_CHIPS)], axis=2)


def _pack_big_grads(layer_grads):
    parts, slot_rows = [], 0
    for k in range(N_CHIPS):
        slot = []
        for n in BIG:
            for g in layer_grads:
                width = g[n].shape[0] // N_CHIPS
                slot.append(g[n][k * width:(k + 1) * width].astype(bf16).reshape(-1, PACK_COLS))
        slot_rows = sum(p.shape[0] for p in slot)
        pad = (-slot_rows) % PACK_ROW_MULT
        if pad:
            slot.append(jnp.zeros((pad, PACK_COLS), bf16))
        slot_rows += pad
        parts += slot
    return jnp.concatenate(parts, axis=0).reshape(N_CHIPS, slot_rows, PACK_COLS)


def _unpack_big_grads(summed, w):
    out, r0 = {}, 0
    for n in BIG:
        rows = w[n].size // PACK_COLS
        out[n] = summed[r0:r0 + rows].reshape(w[n].shape)
        r0 += rows
    return out


def _pack_small(vals, names):
    parts = []
    for n in names:
        pieces = vals[n] if isinstance(vals[n], list) else [vals[n]]
        size = sum(p.size for p in pieces)
        if all(p.size % LANES == 0 for p in pieces):
            parts += [p.reshape(-1, LANES) for p in pieces]
        else:
            flat = [p.reshape(-1) for p in pieces] + [jnp.zeros(((-size) % LANES,), f32)]
            parts.append(jnp.concatenate(flat).reshape(-1, LANES))
    return _concat_padded(parts, PACK_ROW_MULT)


def _unpack_small(packed, like, names):
    out, r0 = {}, 0
    for n in names:
        size = like[n].size
        rows = -(-size // LANES)
        out[n] = packed[r0:r0 + rows].reshape(-1)[:size].reshape(like[n].shape)
        r0 += rows
    return out


def _dims(w, x):
    d = {}
    d["D"] = x.shape[-1]
    d["T"] = x.shape[-2]
    d["DI"] = w["ssd_norm_g"].shape[-1]
    d["NH"] = w["ssd_dt_bias"].shape[-1]
    d["CD"] = w["ssd_conv_b"].shape[-1]
    d["G"] = SSD_N_GROUPS
    d["HPG"] = d["NH"] // d["G"]
    d["P"] = d["DI"] // d["NH"]
    d["N"] = (d["CD"] - d["DI"]) // (2 * d["G"])
    d["S5G"], d["S5N"] = w["s5_lambda_re"].shape[-2:]
    d["S5C"] = w["s5_b_re"].shape[-1]
    d["S5W"] = d["S5G"] * d["S5C"]
    d["NSB"] = d["S5W"] // S5_SUPERBLOCK
    d["GSB"] = d["S5G"] // d["NSB"]
    return d


def _head_pad(v, d):
    lead = v.shape[:-1]
    v = v.reshape(lead + (d["G"], d["HPG"]))
    v = jnp.concatenate([v, jnp.zeros(lead + (d["G"], LANES - d["HPG"]), v.dtype)], axis=-1)
    return v.reshape(lead + (d["G"] * LANES,))


def _head_unpad(v, d):
    lead = v.shape[:-1]
    return v.reshape(lead + (d["G"], LANES))[..., :d["HPG"]].reshape(lead + (d["NH"],))


def _w_in_perm(shards, d):
    o, nh = d["DI"] + d["CD"], d["NH"]
    r = shards[0].shape[0]

    def rows(lo, hi):
        out = []
        for k, s in enumerate(shards):
            a, b = max(lo, k * r), min(hi, (k + 1) * r)
            if a < b:
                out.append(s[a - k * r:b - k * r])
        return out

    dt = _head_pad(jnp.concatenate(rows(o, o + nh), axis=0).T, d).T
    return jnp.concatenate(rows(0, o) + rows(o + nh, len(shards) * r) + [dt], axis=0)


def _w_in_unperm(g, d):
    o = d["DI"] + d["CD"]
    rest = d["S5W"] + 2 * d["D"]
    return jnp.concatenate([g[:o], _head_unpad(g[o + rest:].T, d).T, g[o:o + rest]], axis=0)


def _s5_block_diag(v, d):
    gsb = d["GSB"]
    g, a, b = v.shape
    row_group = (lax.broadcasted_iota(jnp.int32, (g * a, gsb * b), 0) // a) % gsb
    col_group = lax.broadcasted_iota(jnp.int32, (g * a, gsb * b), 1) // b
    return jnp.where(row_group == col_group, jnp.tile(v.reshape(g * a, b), (1, gsb)), 0)


def _s5_diag_blocks(m, d, a, b):
    gsb = d["GSB"]
    rows = m.shape[0]
    m = m.reshape(rows, gsb, b)
    row_group = (lax.broadcasted_iota(jnp.int32, (rows, gsb, 1), 0) // a) % gsb
    col_group = lax.broadcasted_iota(jnp.int32, (rows, gsb, 1), 1)
    return jnp.sum(jnp.where(row_group == col_group, m, 0), axis=1).reshape(rows // a, a, b)


def _s5_lam_rows(v, d):
    v = v.reshape(d["NSB"], 1, d["GSB"] * d["S5N"])
    return jnp.broadcast_to(v, (d["NSB"], 8, v.shape[-1])).reshape(d["NSB"] * 8, -1)


def _ffn_fwd(h, pre_g, post_g, wgu, wd, tag):
    D = h.shape[1]
    H2 = wgu.shape[0]
    xn = _row_kernel(f"{tag}_norm", _fwd_of(_f_norm), [(h, D, 0)], [pre_g], [(D, bf16)])[0]
    ab = _mm(xn, wgu, "nt", f32, f"{tag}_mm_up")
    hid = _row_kernel(f"{tag}_swiglu", _swiglu_fwd, [(ab, H2, 0)], [], [(H2 // 2, bf16)])[0]
    f = _mm(hid, wd, "nn", f32, f"{tag}_mm_down")
    out = _row_kernel(f"{tag}_resnorm", _fwd_of(_f_resnorm(0.5)), [(h, D, 0), (f, D, 0)], [post_g], [(D, f32)])[0]
    return out, dict(h=h, xn=xn, ab=ab, hid=hid, f=f)


def _ffn_bwd(dh_out, s, pre_g, post_g, wgu, wd, tag):
    D = dh_out.shape[1]
    H2 = wgu.shape[0]
    df, dpost = _row_kernel(f"{tag}_resnorm_bwd", _vjp_of(_f_post(0.5), 1, 1, [0]), [(s["f"], D, 0), (dh_out, D, 0)],
                            [post_g], [(D, bf16)], [post_g.shape])
    dwd = _mm(s["hid"], df, "tn", bf16, f"{tag}_mm_dwd")
    dhid = _mm(df, wd, "nt", bf16, f"{tag}_mm_dhid")
    dab = _row_kernel(f"{tag}_swiglu_bwd", _swiglu_bwd, [(s["ab"], H2, 0), (dhid, H2 // 2, 0)], [], [(H2, bf16)])[0]
    dwgu = _mm(dab, s["xn"], "tn", bf16, f"{tag}_mm_dwgu")
    dxn = _mm(dab, wgu, "nn", f32, f"{tag}_mm_dxn")
    dh, dpre = _row_kernel(f"{tag}_norm_bwd", _vjp_of(_f_norm, 1, 1, [0], 1), [(s["h"], D, 0), (dxn, D, 0), (dh_out, D, 0)],
                           [pre_g], [(D, f32)], [pre_g.shape])
    return dh, dict(pre_g=dpre, post_g=dpost, wgu=dwgu, wd=dwd)


def _mixer_fwd(h, p, d):
    D, DI, CD, G, N = d["D"], d["DI"], d["CD"], d["G"], d["N"]
    gl = G * LANES
    c_u5, c_ga, c_gb, c_dt = DI + CD, DI + CD + d["S5W"], DI + CD + d["S5W"] + D, DI + CD + d["S5W"] + 2 * D
    u = _row_kernel("mix_norm", _fwd_of(_f_norm), [(h, D, 0)], [p["mix_pre_g"]], [(D, bf16)])[0]
    proj = _mm(u, p["w_in"], "nt", f32, "mix_mm_in", bn_t=512)
    act = _conv_fwd(proj, DI, p["conv_w"], p["conv_b"], "ssd_conv")
    dt, adt = _row_kernel("ssd_dt", _fwd_of(_f_dt), [(proj, gl, c_dt // gl)], [p["dt_bias"], p["a_log"]], [(gl, f32)] * 2)
    y_ssd, states = _ssd_fwd(act, dt, adt, p["d_skip"], d["HPG"], d["P"], N, "ssd_scan")
    nrm = _row_kernel("ssd_post", _fwd_of(_f_ssdpost(G)), [(y_ssd, DI, 0), (proj, DI, 0)], [p["norm_g"]], [(DI, bf16)])[0]
    y_a = _mm(nrm, p["w_a"], "nn", f32, "mix_mm_a")
    u5 =(proj, d["S5W"], c_u5 // d["S5W"])
    bu = _s5_in(proj, c_u5, p["bsb"], d)
    s5st = _s5_scan_fwd(bu, p["lam_re_rows"], p["lam_im_rows"], d["NSB"], "s5_scan")
    y5 = _bdmm(s5st, p["csb"], "nn", d["NSB"], f32, "s5_mm_c")
    gel = _row_kernel("s5_post", _fwd_of(_f_s5post), [(y5, d["S5W"], 0), u5], [p["s5_d"]], [(d["S5W"], bf16)])[0]
    vg = _mm(gel, p["w_glu"], "nt", f32, "mix_mm_glu")
    glu = _row_kernel("s5_glu", _glu_fwd, [(vg, vg.shape[1], 0)], [], [(vg.shape[1] // 2, bf16)])[0]
    y_b = _mm(glu, p["w_b"], "nn", f32, "mix_mm_b")
    merged = _row_kernel("mix_merge", _fwd_of(_f_merge), [(proj, D, c_ga // D), (y_a, D, 0), (proj, D, c_gb // D), (y_b, D, 0)],
                         [], [(D, bf16)])[0]
    m = _mm(merged, p["w_out"], "nn", f32, "mix_mm_out")
    out = _row_kernel("mix_resnorm", _fwd_of(_f_resnorm(1.0)), [(h, D, 0), (m, D, 0)], [p["mix_post_g"]], [(D, f32)])[0]
    return out, dict(h=h, u=u, proj=proj, act=act, dt=dt, adt=adt, states=states, y_ssd=y_ssd, nrm=nrm, y_a=y_a, s5st=s5st,
                     y5=y5, gel=gel, vg=vg, glu=glu, y_b=y_b, merged=merged, m=m)


def _s5_in(proj, c_u5, bsb, d):
    T = proj.shape[0]
    nsb = d["NSB"]
    ka, nw = S5_SUPERBLOCK, bsb.shape[1]
    off = c_u5 // ka
    assert c_u5 % ka == 0
    bt = _pick(T, 512)

    def body(a_ref, w_ref, o_ref):
        o_ref[...] = _dot(a_ref[...].astype(bf16), w_ref[...].astype(bf16))

    return pl.pallas_call(
        body, name="s5_mm_bu", grid=(nsb, T // bt),
        in_specs=[pl.BlockSpec((bt, ka), lambda j, i: (i, off + j)), pl.BlockSpec((ka, nw), lambda j, i: (j, 0))],
        out_specs=pl.BlockSpec((bt, nw), lambda j, i: (i, j)), out_shape=jax.ShapeDtypeStruct((T, nsb * nw), f32),
        compiler_params=_params(("parallel", "parallel")),
    )(proj, bsb)


def _s5_dbsb(proj, c_u5, a, d):
    T = proj.shape[0]
    nsb = d["NSB"]
    ka, nw = S5_SUPERBLOCK, a.shape[1] // nsb
    off = c_u5 // ka
    bt = _pick(T, 512)

    def body(u_ref, a_ref, o_ref):
        pr = _dot(u_ref[...].astype(bf16), a_ref[...].astype(bf16), "tn")
        k = pl.program_id(1)

        @pl.when(k == 0)
        def _():
            o_ref[...] = pr

        @pl.when(k > 0)
        def _():
            o_ref[...] += pr

    return pl.pallas_call(
        body, name="s5_mm_dbsb", grid=(nsb, T // bt),
        in_specs=[pl.BlockSpec((bt, ka), lambda j, k: (k, off + j)), pl.BlockSpec((bt, nw), lambda j, k: (k, j))],
        out_specs=pl.BlockSpec((ka, nw), lambda j, k: (j, 0)), out_shape=jax.ShapeDtypeStruct((nsb * ka, nw), f32),
        compiler_params=_params(("parallel", "arbitrary")),
    )(proj, a)


def _mixer_bwd(dh_out, s, p, d):
    D, DI, CD, G, N, S5W = d["D"], d["DI"], d["CD"], d["G"], d["N"], d["S5W"]
    gl = G * LANES
    gn = G * N
    c_u5, c_ga, c_gb, c_dt = DI + CD, DI + CD + S5W, DI + CD + S5W + D, DI + CD + S5W + 2 * D
    proj = s["proj"]
    g = {}
    dm, g["mix_post_g"] = _row_kernel("mix_resnorm_bwd", _vjp_of(_f_post(1.0), 1, 1, [0]), [(s["m"], D, 0), (dh_out, D, 0)],
                                      [p["mix_post_g"]], [(D, bf16)], [p["mix_post_g"].shape])
    g["w_out"] = _mm(s["merged"], dm, "tn", bf16, "mix_mm_dwout")
    dmerged = _mm(dm, p["w_out"], "nt", f32, "mix_mm_dmerged")
    dga, dya, dgb, dyb = _row_kernel(
        "mix_merge_bwd", _vjp_of(_f_merge, 4, 1, [0, 1, 2, 3]),
        [(proj, D, c_ga // D), (s["y_a"], D, 0), (proj, D, c_gb // D), (s["y_b"], D, 0), (dmerged, D, 0)], [],
        [(D, bf16), (D, bf16), (D, bf16), (D, bf16)])
    g["w_a"] = _mm(s["nrm"], dya, "tn", bf16, "mix_mm_dwa")
    dnrm = _mm(dya, p["w_a"], "nt", f32, "mix_mm_dnrm")
    dy_ssd, dz, g["norm_g"] = _row_kernel(
        "ssd_post_bwd", _vjp_of(_f_ssdpost(G), 2, 1, [0, 1]), [(s["y_ssd"], DI, 0), (proj, DI, 0), (dnrm, DI, 0)],
        [p["norm_g"]], [(DI, f32), (DI, bf16)], [p["norm_g"].shape])
    dxs, d_b, d_c, ddt, dadt, dd = _ssd_bwd(s["act"], s["dt"], s["adt"], p["d_skip"], s["states"], dy_ssd,
                                            d["HPG"], d["P"], N, "ssd_scan_bwd")
    g["d_skip"] = dd.reshape(G, 8, LANES)[:, 0, :].reshape(1, gl)
    ddt_raw, g["dt_bias"], g["a_log"] = _row_kernel(
        "ssd_dt_bwd", _vjp_of(_f_dt, 1, 2, [0]), [(proj, gl, c_dt // gl), (ddt, gl, 0), (dadt, gl, 0)],
        [p["dt_bias"], p["a_log"]], [(gl, bf16)], [p["dt_bias"].shape, p["a_log"].shape])
    cw, cb = p["conv_w"], p["conv_b"]
    dxc_x, dw_x, db_x = _conv_bwd(proj, DI, cw[:, :DI], cb[:, :DI], dxs, "ssd_conv_bwd_x")
    dxc_b, dw_b, db_b = _conv_bwd(proj, 2 * DI, cw[:, DI:DI + gn], cb[:, DI:DI + gn], d_b, "ssd_conv_bwd_b")
    dxc_c, dw_c, db_c = _conv_bwd(proj, 2 * DI + gn, cw[:, DI + gn:], cb[:, DI + gn:], d_c, "ssd_conv_bwd_c")
    g["conv_w"] = jnp.concatenate([dw_x, dw_b, dw_c], axis=1)
    g["conv_b"] = jnp.concatenate([db_x, db_b, db_c], axis=1)
    g["w_b"] = _mm(s["glu"], dyb, "tn", bf16, "mix_mm_dwb")
    dglu = _mm(dyb, p["w_b"], "nt", f32, "mix_mm_dglu")
    dvg = _row_kernel("s5_glu_bwd", _glu_bwd, [(s["vg"], s["vg"].shape[1], 0), (dglu, S5W, 0)], [], [(s["vg"].shape[1], bf16)])[0]
    g["w_glu"] = _mm(dvg, s["gel"], "tn", bf16, "mix_mm_dwglu")
    dgel = _mm(dvg, p["w_glu"], "nn", f32, "mix_mm_dgel")
    dy5, du5a, g["s5_d"] = _row_kernel(
        "s5_post_bwd", _vjp_of(_f_s5post, 2, 1, [0, 1]), [(s["y5"], S5W, 0), (proj, S5W, c_u5 // S5W), (dgel, S5W, 0)],
        [p["s5_d"]], [(S5W, bf16), (S5W, f32)], [p["s5_d"].shape])
    g["csb"] = _bdmm(s["s5st"], dy5, "tn", d["NSB"], f32, "s5_mm_dcsb")
    gst = _bdmm(dy5, p["csb"], "nt", d["NSB"], f32, "s5_mm_gst")
    a, g["lam_re_rows"], g["lam_im_rows"] = _s5_scan_bwd(gst, s["s5st"], p["lam_re_rows"], p["lam_im_rows"], d["NSB"], "s5_scan_bwd")
    g["bsb"] = _s5_dbsb(proj, c_u5, a, d)
    du5b = _bdmm(a, p["bsb"], "nt", d["NSB"], f32, "s5_mm_du5")
    du5 = _row_kernel("s5_du5", _add_fn, [(du5a, S5W, 0), (du5b, S5W, 0)], [], [(S5W, bf16)])[0]
    dproj = jnp.concatenate([dz, dxc_x, dxc_b, dxc_c, du5, dga, dgb, ddt_raw], axis=1)
    g["w_in"] = _mm(dproj, s["u"], "tn", bf16, "mix_mm_dwin")
    du = _mm(dproj, p["w_in"], "nn", f32, "mix_mm_du", bk_t=2176)
    dh, g["mix_pre_g"] = _row_kernel("mix_norm_bwd", _vjp_of(_f_norm, 1, 1, [0], 1), [(s["h"], D, 0), (du, D, 0), (dh_out, D, 0)],
                                     [p["mix_pre_g"]], [(D, f32)], [p["mix_pre_g"].shape])
    return dh, g


def _layer_params(l, w, wf, conv_w_full, d):
    r2 = lambda v: v[l].reshape(1, -1)
    p = {}
    for n in ["ffn1_pre_g", "ffn1_post_g", "mix_pre_g", "mix_post_g", "ffn2_pre_g", "ffn2_post_g", "s5_d"]:
        p[n] = r2(w[n])
    whole = lambda *names: jnp.concatenate([s for n in names for s in wf(n, l)], axis=0)
    p["wgu1"] = whole("ffn1_w_gate", "ffn1_w_up")
    p["wd1"] = whole("ffn1_w_down")
    p["wgu2"] = whole("ffn2_w_gate", "ffn2_w_up")
    p["wd2"] = whole("ffn2_w_down")
    p["w_in"] = _w_in_perm(wf("w_in", l), d)
    p["w_a"], p["w_glu"], p["w_b"], p["w_out"] = whole("w_branch_a"), whole("s5_w_glu"), whole("w_branch_b"), whole("w_out")
    p["conv_w"] = conv_w_full[l]
    p["conv_b"] = r2(w["ssd_conv_b"])
    p["dt_bias"] = _head_pad(r2(w["ssd_dt_bias"]), d)
    p["a_log"] = _head_pad(r2(w["ssd_a_log"]), d)
    p["d_skip"] = _head_pad(r2(w["ssd_d"]), d)
    p["norm_g"] = r2(w["ssd_norm_g"])
    g5, n5, c5 = d["S5G"], d["S5N"], d["S5C"]
    expand = jnp.repeat(jnp.eye(n5, dtype=f32), c5, axis=1)
    prep_in = [w["s5_lambda_re"][l], w["s5_lambda_im"][l], w["s5_log_step"][l].reshape(g5, 1),
               w["s5_b_re"][l].reshape(g5, n5 * c5), w["s5_b_im"][l].reshape(g5, n5 * c5), expand]
    lbr, lbi, bbr, bbi = _s5_prep(prep_in, "s5_prep")
    p["s5_prep_in"] = prep_in
    p["lam_re_rows"], p["lam_im_rows"] = _s5_lam_rows(lbr, d), _s5_lam_rows(lbi, d)
    to_cn = lambda v: v.reshape(g5, n5, c5).transpose(0, 2, 1)
    p["bsb"] = jnp.concatenate([_s5_block_diag(to_cn(bbr), d), _s5_block_diag(to_cn(bbi), d)], axis=1).astype(bf16)
    c_re, c_im = w["s5_c_re"][l].transpose(0, 2, 1), w["s5_c_im"][l].transpose(0, 2, 1)
    nsb = d["NSB"]
    csb = jnp.stack([_s5_block_diag(c_re, d).reshape(nsb, -1, S5_SUPERBLOCK),
                     _s5_block_diag(-c_im, d).reshape(nsb, -1, S5_SUPERBLOCK)], axis=1)
    p["csb"] = csb.reshape(-1, S5_SUPERBLOCK).astype(bf16)
    return p


def _s5_param_grads(g, p, d, l):
    g5, n5, c5, nsb, gsb = d["S5G"], d["S5N"], d["S5C"], d["NSB"], d["GSB"]
    wst = gsb * n5
    dbsb = g["bsb"]
    from_cn = lambda v: v.transpose(0, 2, 1).reshape(g5, n5 * c5)
    dbbr = from_cn(_s5_diag_blocks(dbsb[:, :wst], d, c5, n5))
    dbbi = from_cn(_s5_diag_blocks(dbsb[:, wst:], d, c5, n5))
    rows = lambda v: v.reshape(nsb, 8, wst)[:, 0, :].reshape(g5, n5)
    cots = [rows(g["lam_re_rows"]), rows(g["lam_im_rows"]), dbbr, dbbi]
    dlr, dli, dls, dbr, dbi = _s5_prep_bwd(p["s5_prep_in"], cots, "s5_prep_bwd")
    dcsb = g["csb"].reshape(nsb, 2, wst, S5_SUPERBLOCK)
    dcr = _s5_diag_blocks(dcsb[:, 0].reshape(-1, S5_SUPERBLOCK), d, n5, c5).transpose(0, 2, 1)
    dci = -_s5_diag_blocks(dcsb[:, 1].reshape(-1, S5_SUPERBLOCK), d, n5, c5).transpose(0, 2, 1)
    return dict(s5_lambda_re=dlr, s5_lambda_im=dli, s5_log_step=dls.reshape(g5), s5_b_re=dbr.reshape(g5, n5, c5),
                s5_b_im=dbi.reshape(g5, n5, c5), s5_c_re=dcr, s5_c_im=dci)


def kernel(x, ffn1_pre_g, ffn1_post_g, ffn1_w_gate, ffn1_w_up, ffn1_w_down, mix_pre_g, mix_post_g, w_in, ssd_conv_w, ssd_conv_b, ssd_dt_bias, ssd_a_log, ssd_d, ssd_norm_g, w_branch_a, s5_lambda_re, s5_lambda_im, s5_b_re, s5_b_im, s5_c_re, s5_c_im, s5_log_step, s5_d, s5_w_glu, w_branch_b, w_out, ffn2_pre_g, ffn2_post_g, ffn2_w_gate, ffn2_w_up, ffn2_w_down, loss_target, m_ffn1_pre_g, m_ffn1_post_g, m_ffn1_w_gate, m_ffn1_w_up, m_ffn1_w_down, m_mix_pre_g, m_mix_post_g, m_w_in, m_ssd_conv_w, m_ssd_conv_b, m_ssd_dt_bias, m_ssd_a_log, m_ssd_d, m_ssd_norm_g, m_w_branch_a, m_s5_lambda_re, m_s5_lambda_im, m_s5_b_re, m_s5_b_im, m_s5_c_re, m_s5_c_im, m_s5_log_step, m_s5_d, m_s5_w_glu, m_w_branch_b, m_w_out, m_ffn2_pre_g, m_ffn2_post_g, m_ffn2_w_gate, m_ffn2_w_up, m_ffn2_w_down, v_ffn1_pre_g, v_ffn1_post_g, v_ffn1_w_gate, v_ffn1_w_up, v_ffn1_w_down, v_mix_pre_g, v_mix_post_g, v_w_in, v_ssd_conv_w, v_ssd_conv_b, v_ssd_dt_bias, v_ssd_a_log, v_ssd_d, v_ssd_norm_g, v_w_branch_a, v_s5_lambda_re, v_s5_lambda_im, v_s5_b_re, v_s5_b_im, v_s5_c_re, v_s5_c_im, v_s5_log_step, v_s5_d, v_s5_w_glu, v_w_branch_b, v_w_out, v_ffn2_pre_g, v_ffn2_post_g, v_ffn2_w_gate, v_ffn2_w_up, v_ffn2_w_down):
    given = dict(locals())
    for n in COL_SHARDED:
        for prefix in ("", "m_", "v_"):
            given[prefix + n] = given[prefix + n].transpose(0, 2, 1)
    w = {n: given[n] for n in WEIGHTS}
    mom = {n: given["m_" + n] for n in WEIGHTS}
    var = {n: given["v_" + n] for n in WEIGHTS}
    d = _dims(w, x)
    n_layers = w["ffn1_pre_g"].shape[0]
    T, D = d["T"], d["D"]

    gathered = _chip_all_gather(_pack_weights(w, w["ssd_conv_w"]), "gather_weights")
    wf, conv_w_full = _unpack_weights(gathered, w, w["ssd_conv_w"])
    layers = [_layer_params(l, w, wf, conv_w_full, d) for l in range(n_layers)]

    h = x.reshape(T, D)
    saved = []
    for p in layers:
        h, s1 = _ffn_fwd(h, p["ffn1_pre_g"], p["ffn1_post_g"], p["wgu1"], p["wd1"], "ffn1")
        h, sm = _mixer_fwd(h, p, d)
        h, s2 = _ffn_fwd(h, p["ffn2_pre_g"], p["ffn2_post_g"], p["wgu2"], p["wd2"], "ffn2")
        saved.append((s1, sm, s2))
    dh, loss_part = _row_kernel("loss", _loss_fn, [(h, D, 0), (loss_target.reshape(T, D), D, 0)], [], [(D, f32)], [(8, LANES)])
    loss = lax.psum(loss_part[0, 0], ("x", "y", "c"))

    lg = [None] * n_layers
    for l in reversed(range(n_layers)):
        p = layers[l]
        s1, sm, s2 = saved[l]
        dh, g2 = _ffn_bwd(dh, s2, p["ffn2_pre_g"], p["ffn2_post_g"], p["wgu2"], p["wd2"], "ffn2")
        dh, gm = _mixer_bwd(dh, sm, p, d)
        dh, g1 = _ffn_bwd(dh, s1, p["ffn1_pre_g"], p["ffn1_post_g"], p["wgu1"], p["wd1"], "ffn1")
        H = p["wd1"].shape[0]
        gl = dict(ffn1_pre_g=g1["pre_g"], ffn1_post_g=g1["post_g"], ffn1_w_gate=g1["wgu"][:H], ffn1_w_up=g1["wgu"][H:],
                  ffn1_w_down=g1["wd"], ffn2_pre_g=g2["pre_g"], ffn2_post_g=g2["post_g"], ffn2_w_gate=g2["wgu"][:H],
                  ffn2_w_up=g2["wgu"][H:], ffn2_w_down=g2["wd"], mix_pre_g=gm["mix_pre_g"], mix_post_g=gm["mix_post_g"],
                  w_in=_w_in_unperm(gm["w_in"], d), ssd_conv_w=gm["conv_w"], ssd_conv_b=gm["conv_b"],
                  ssd_dt_bias=_head_unpad(gm["dt_bias"], d), ssd_a_log=_head_unpad(gm["a_log"], d),
                  ssd_d=_head_unpad(gm["d_skip"], d), ssd_norm_g=gm["norm_g"], w_branch_a=gm["w_a"], s5_d=gm["s5_d"],
                  s5_w_glu=gm["w_glu"], w_branch_b=gm["w_b"], w_out=gm["w_out"])
        gl.update(_s5_param_grads(gm, p, d, l))
        lg[l] = gl
    grad_x = dh.reshape(x.shape)
    my_core = lax.axis_index("c").astype(jnp.int32).reshape(1)
    grads = _unpack_big_grads(_reduce_to_chips(_pack_big_grads(lg), my_core, "grads"), w)
    small_names = SMALL + ["ssd_conv_w"]
    small_parts = {n: [g[n] for g in lg] for n in small_names}
    small_like = {n: jax.ShapeDtypeStruct((n_layers,) + lg[0][n].shape, f32) for n in small_names}
    small_like.update({n: w[n] for n in SMALL})
    small_pack = _pack_small(small_parts, small_names)
    small_sum = _reduce_to_chips(jnp.broadcast_to(small_pack, (N_CHIPS,) + small_pack.shape), my_core, "small")
    small = _unpack_small(small_sum, small_like, small_names)
    k_me = 2 * lax.axis_index("x") + lax.axis_index("y")
    cw = w["ssd_conv_w"].shape[-1]
    small["ssd_conv_w"] = lax.dynamic_slice_in_dim(small["ssd_conv_w"], k_me * cw, cw, axis=2)
    grads.update(small)

    delta, new_m, new_v = {}, {}, {}
    for n in BIG + ["ssd_conv_w"]:
        delta[n], new_m[n], new_v[n] = _adamw(w[n], grads[n], mom[n], var[n], "adamw_" + n)
    pw, pm, pv = [_pack_small(t, SMALL) for t in (w, mom, var)]
    assert pw.shape[0] <= small_sum.shape[0]
    sd, sm_, sv = _adamw(pw, small_sum[:pw.shape[0]], pm, pv, "adamw_small")
    delta.update(_unpack_small(sd, w, SMALL))
    new_m.update(_unpack_small(sm_, w, SMALL))
    new_v.update(_unpack_small(sv, w, SMALL))
    for n in COL_SHARDED:
        for out in (grads, delta, new_m, new_v):
            out[n] = out[n].transpose(0, 2, 1)
    return (loss, grad_x, *[grads[n] for n in WEIGHTS], *[delta[n] for n in WEIGHTS],
            *[new_m[n] for n in WEIGHTS], *[new_v[n] for n in WEIGHTS])
```

```python
import functools

import numpy as np
import jax
import jax.numpy as jnp
from jax import lax
from jax.experimental import pallas as pl
from jax.experimental.pallas import tpu as pltpu

f32, bf16 = jnp.float32, jnp.bfloat16

SSD_N_GROUPS = 4
SSD_CHUNK = 128
RMS_EPS = 1e-6
S5_MAX_REAL = -1e-4
S5_SUPERBLOCK = 256
ADAM_LR, ADAM_B1, ADAM_B2, ADAM_EPS, ADAM_WD, ADAM_STEP = 0.001, 0.9, 0.999, 1e-08, 0.01, 10

LANES = 128
PACK_COLS = 1024
D2D_STREAMS = 16
PACK_ROW_MULT = 2 * D2D_STREAMS * 16
VMEM_LIMIT_BYTES = 48 * 1024 * 1024
N_CHIPS, N_CORES, N_DEV = 4, 2, 8
MESH = pl.DeviceIdType.MESH

BIG = ["ffn1_w_gate", "ffn1_w_up", "ffn1_w_down", "w_in", "w_branch_a", "s5_w_glu", "w_branch_b", "w_out",
       "ffn2_w_gate", "ffn2_w_up", "ffn2_w_down"]
COL_SHARDED = ["ffn1_w_gate", "ffn1_w_up", "w_in", "s5_w_glu", "ffn2_w_gate", "ffn2_w_up"]
SMALL = ["ffn1_pre_g", "ffn1_post_g", "mix_pre_g", "mix_post_g", "ssd_conv_b", "ssd_norm_g", "s5_lambda_re", "s5_lambda_im",
         "s5_b_re", "s5_b_im", "s5_c_re", "s5_c_im", "s5_d", "ffn2_pre_g", "ffn2_post_g", "s5_log_step", "ssd_dt_bias",
         "ssd_a_log", "ssd_d"]
WEIGHTS = ["ffn1_pre_g", "ffn1_post_g", "ffn1_w_gate", "ffn1_w_up", "ffn1_w_down", "mix_pre_g", "mix_post_g", "w_in",
           "ssd_conv_w", "ssd_conv_b", "ssd_dt_bias", "ssd_a_log", "ssd_d", "ssd_norm_g", "w_branch_a", "s5_lambda_re",
           "s5_lambda_im", "s5_b_re", "s5_b_im", "s5_c_re", "s5_c_im", "s5_log_step", "s5_d", "s5_w_glu", "w_branch_b",
           "w_out", "ffn2_pre_g", "ffn2_post_g", "ffn2_w_gate", "ffn2_w_up", "ffn2_w_down"]


def _params(sem=None):
    return pltpu.CompilerParams(dimension_semantics=sem, vmem_limit_bytes=VMEM_LIMIT_BYTES)


def _pick(n, target, mult=LANES):
    best = None
    for d in range(mult, min(n, target) + 1, mult):
        if n % d == 0:
            best = d
    return best if best is not None else n


_DIMS = {"nn": (((1,), (0,)), ((), ())), "nt": (((1,), (1,)), ((), ())), "tn": (((0,), (0,)), ((), ()))}


def _mm(a, b, mode, out_dtype, name, bm_t=1024, bn_t=1024, bk_t=2816):
    if mode == "nn":
        (M, K), (K2, N) = a.shape, b.shape
    elif mode == "nt":
        (M, K), (N, K2) = a.shape, b.shape
    else:
        (K, M), (K2, N) = a.shape, b.shape
    assert K == K2, (name, a.shape, b.shape)
    bm, bn, bk = _pick(M, bm_t), _pick(N, bn_t), _pick(K, bk_t)
    nk = K // bk
    dn = _DIMS[mode]

    def body(a_ref, b_ref, o_ref, *scratch):
        p = lax.dot_general(a_ref[...].astype(bf16), b_ref[...].astype(bf16), dn, preferred_element_type=f32)
        if nk == 1:
            o_ref[...] = p.astype(o_ref.dtype)
        else:
            acc = scratch[0]
            k = pl.program_id(2)

            @pl.when(k == 0)
            def _():
                acc[...] = p

            @pl.when(k > 0)
            def _():
                acc[...] += p

            @pl.when(k == nk - 1)
            def _():
                o_ref[...] = acc[...].astype(o_ref.dtype)

    if mode == "tn":
        a_spec = pl.BlockSpec((bk, bm), lambda i, j, k: (k, i))
    else:
        a_spec = pl.BlockSpec((bm, bk), lambda i, j, k: (i, k))
    if mode == "nt":
        b_spec = pl.BlockSpec((bn, bk), lambda i, j, k: (j, k))
    else:
        b_spec = pl.BlockSpec((bk, bn), lambda i, j, k: (k, j))
    return pl.pallas_call(
        body, name=name, grid=(M // bm, N // bn, nk), in_specs=[a_spec, b_spec],
        out_specs=pl.BlockSpec((bm, bn), lambda i, j, k: (i, j)), out_shape=jax.ShapeDtypeStruct((M, N), out_dtype),
        scratch_shapes=[pltpu.VMEM((bm, bn), f32)] if nk > 1 else [],
        compiler_params=_params(("parallel", "parallel", "arbitrary")),
    )(a, b)


def _bdmm(a, w, mode, nb, out_dtype, name, bt_t=512):
    if mode == "tn":
        T = a.shape[0]
        ka, nw = a.shape[1] // nb, w.shape[1] // nb
        bt = _pick(T, bt_t)
        nt = T // bt

        def body_tn(a_ref, b_ref, o_ref):
            p = lax.dot_general(a_ref[...].astype(bf16), b_ref[...].astype(bf16), _DIMS["tn"], preferred_element_type=f32)
            k = pl.program_id(1)

            @pl.when(k == 0)
            def _():
                o_ref[...] = p

            @pl.when(k > 0)
            def _():
                o_ref[...] += p

        return pl.pallas_call(
            body_tn, name=name, grid=(nb, nt),
            in_specs=[pl.BlockSpec((bt, ka), lambda j, k: (k, j)), pl.BlockSpec((bt, nw), lambda j, k: (k, j))],
            out_specs=pl.BlockSpec((ka, nw), lambda j, k: (j, 0)), out_shape=jax.ShapeDtypeStruct((nb * ka, nw), f32),
            compiler_params=_params(("parallel", "arbitrary")),
        )(a, w)
    T = a.shape[0]
    ka, nw = w.shape[0] // nb, w.shape[1]
    bt = _pick(T, bt_t)
    kin, kout = (ka, nw) if mode == "nn" else (nw, ka)
    dn = _DIMS[mode]

    def body(a_ref, w_ref, o_ref):
        o_ref[...] = lax.dot_general(a_ref[...].astype(bf16), w_ref[...].astype(bf16), dn,
                                     preferred_element_type=f32).astype(o_ref.dtype)

    return pl.pallas_call(
        body, name=name, grid=(nb, T // bt),
        in_specs=[pl.BlockSpec((bt, kin), lambda j, i: (i, j)), pl.BlockSpec((ka, nw), lambda j, i: (j, 0))],
        out_specs=pl.BlockSpec((bt, kout), lambda j, i: (i, j)), out_shape=jax.ShapeDtypeStruct((T, nb * kout), out_dtype),
        compiler_params=_params(("parallel", "parallel")),
    )(a, w)


def _row_index(i, cb):
    return (i, cb)


def _row_kernel(name, fn, rows, pars, row_outs, par_outs=(), block_rows=256):
    T = rows[0][0].shape[0]
    R = min(block_rows, T)
    assert T % R == 0
    nr, npar, nro = len(rows), len(pars), len(row_outs)

    def body(*refs):
        rv = [r[...] for r in refs[:nr]]
        pv = [r[...] for r in refs[nr:nr + npar]]
        ro, po = fn(rv, pv)
        for ref, v in zip(refs[nr + npar:nr + npar + nro], ro):
            ref[...] = v.astype(ref.dtype)
        if par_outs:
            i = pl.program_id(0)
            prefs = refs[nr + npar + nro:]

            @pl.when(i == 0)
            def _():
                for ref, v in zip(prefs, po):
                    ref[...] = v.astype(f32)

            @pl.when(i > 0)
            def _():
                for ref, v in zip(prefs, po):
                    ref[...] += v.astype(f32)

    in_specs = [pl.BlockSpec((R, nc), functools.partial(_row_index, cb=cb)) for (_, nc, cb) in rows]
    in_specs += [pl.BlockSpec(p.shape, lambda i: (0, 0)) for p in pars]
    out_specs = [pl.BlockSpec((R, nc), lambda i: (i, 0)) for (nc, _) in row_outs]
    out_specs += [pl.BlockSpec(s, lambda i: (0, 0)) for s in par_outs]
    out_shape = [jax.ShapeDtypeStruct((T, nc), dt) for (nc, dt) in row_outs]
    out_shape += [jax.ShapeDtypeStruct(s, f32) for s in par_outs]
    outs = pl.pallas_call(
        body, name=name, grid=(T // R,), in_specs=in_specs, out_specs=out_specs, out_shape=out_shape,
        compiler_params=_params(("arbitrary",) if par_outs else ("parallel",)),
    )(*[r[0] for r in rows], *pars)
    return list(outs)


def _fwd_of(f):
    def fn(rv, pv):
        return f([v.astype(f32) for v in rv], [v.astype(f32) for v in pv]), []
    return fn


def _vjp_of(f, n_x, n_cot, grad_idx, n_add=0):
    def fn(rv, pv):
        xs = [v.astype(f32) for v in rv[:n_x]]
        cots = [v.astype(f32) for v in rv[n_x:n_x + n_cot]]
        adds = rv[n_x + n_cot:n_x + n_cot + n_add]
        ps = [v.astype(f32) for v in pv]
        _, vjp = jax.vjp(lambda *a: f(list(a[:n_x]), list(a[n_x:])), *xs, *ps)
        g = vjp(cots)
        row_g = [g[i] for i in grad_idx]
        for k, a in enumerate(adds):
            row_g[k] = row_g[k] + a.astype(f32)
        return row_g, list(g[n_x:])
    return fn


def _rms(x, g):
    return x * lax.rsqrt(jnp.mean(x * x, axis=-1, keepdims=True) + RMS_EPS) * g


def _f_norm(xs, ps):
    return [_rms(xs[0], ps[0])]


def _f_post(scale):
    def f(xs, ps):
        return [scale * _rms(xs[0], ps[0])]
    return f


def _f_resnorm(scale):
    def f(xs, ps):
        return [xs[0] + scale * _rms(xs[1], ps[0])]
    return f


def _f_dt(xs, ps):
    dt = jax.nn.softplus(xs[0] + ps[0])
    return [dt, -jnp.exp(ps[1]) * dt]


def _f_ssdpost(n_groups):
    def f(xs, ps):
        y = xs[0] * jax.nn.silu(xs[1])
        width = y.shape[-1] // n_groups
        lane = lax.broadcasted_iota(jnp.int32, y.shape, 1)
        scale = jnp.zeros_like(y)
        for k in range(n_groups):
            m = ((lane >= k * width) & (lane < (k + 1) * width)).astype(f32)
            ms = jnp.sum(y * y * m, axis=-1, keepdims=True) / width
            scale = scale + lax.rsqrt(ms + RMS_EPS) * m
        return [y * scale * ps[0]]
    return f


def _f_s5post(xs, ps):
    return [jax.nn.gelu(xs[0] + ps[0] * xs[1])]


def _f_merge(xs, ps):
    return [jax.nn.sigmoid(xs[0]) * xs[1] + jax.nn.sigmoid(xs[2]) * xs[3]]


def _swiglu_fwd(rv, pv):
    ab = rv[0].astype(f32)
    h = ab.shape[1] // 2
    return [jax.nn.silu(ab[:, :h]) * ab[:, h:]], []


def _swiglu_bwd(rv, pv):
    ab, d = rv[0].astype(f32), rv[1].astype(f32)
    h = ab.shape[1] // 2
    a, b = ab[:, :h], ab[:, h:]
    s = jax.nn.sigmoid(a)
    return [jnp.concatenate([d * b * (s * (1.0 + a * (1.0 - s))), d * (a * s)], axis=1)], []


def _glu_fwd(rv, pv):
    vg = rv[0].astype(f32)
    h = vg.shape[1] // 2
    return [vg[:, :h] * jax.nn.sigmoid(vg[:, h:])], []


def _glu_bwd(rv, pv):
    vg, d = rv[0].astype(f32), rv[1].astype(f32)
    h = vg.shape[1] // 2
    s = jax.nn.sigmoid(vg[:, h:])
    return [jnp.concatenate([d * s, d * vg[:, :h] * s * (1.0 - s)], axis=1)], []


def _loss_fn(rv, pv):
    e = rv[0].astype(f32) - rv[1].astype(f32)
    per_tok = jnp.mean(e * e, axis=-1, keepdims=True)
    part = 0.5 * jnp.sum(per_tok, axis=0, keepdims=True)
    return [e / e.shape[-1]], [jnp.broadcast_to(part, (8, LANES))]


def _add_fn(rv, pv):
    return [rv[0].astype(f32) + rv[1].astype(f32)], []


def _adamw_fn(rv, pv):
    w, g, m, v = [x.astype(f32) for x in rv]
    m = ADAM_B1 * m + (1.0 - ADAM_B1) * g
    v = ADAM_B2 * v + (1.0 - ADAM_B2) * (g * g)
    m_hat = m / (1.0 - ADAM_B1 ** ADAM_STEP)
    v_hat = v / (1.0 - ADAM_B2 ** ADAM_STEP)
    return [-ADAM_LR * (m_hat / (jnp.sqrt(v_hat) + ADAM_EPS) + ADAM_WD * w), m, v], []


def _adamw(w, g, m, v, name):
    shape = w.shape
    cols = shape[-1] if (w.ndim >= 2 and shape[-1] >= LANES) else None
    if cols is None:
        n = int(np.prod(shape))
        cols = LANES if n % LANES == 0 else n
    n_rows = int(np.prod(shape)) // cols
    br, bc = _pick(n_rows, 256, 8), cols
    if br < 64 and cols % LANES == 0:
        br, bc = n_rows, LANES

    def body(w_ref, g_ref, m_ref, v_ref, d_ref, nm_ref, nv_ref):
        outs, _ = _adamw_fn([w_ref[...], g_ref[...], m_ref[...], v_ref[...]], [])
        d_ref[...], nm_ref[...], nv_ref[...] = outs

    spec = pl.BlockSpec((br, bc), lambda i, j: (i, j))
    outs = pl.pallas_call(
        body, name=name, grid=(n_rows // br, cols // bc), in_specs=[spec] * 4, out_specs=[spec] * 3,
        out_shape=[jax.ShapeDtypeStruct((n_rows, cols), f32)] * 3, compiler_params=_params(("parallel", "parallel")),
    )(*[t.reshape(n_rows, cols) for t in (w, g, m, v)])
    return [o.reshape(shape) for o in outs]


def _shift_down(x, s, row):
    if s == 0:
        return x
    return jnp.where(row >= s, pltpu.roll(x, s, 0), 0.0)


def _shift_up(x, s, row):
    if s == 0:
        return x
    n = x.shape[0]
    return jnp.where(row < n - s, pltpu.roll(x, n - s, 0), 0.0)


def _conv_pre(x, w, b, row):
    kw = w.shape[0]
    c = b
    for k in range(kw):
        c = c + w[k:k + 1, :] * _shift_down(x, kw - 1 - k, row)
    return c


def _conv_fwd(xsrc, col0, w, b, name, bc_t=512):
    T = xsrc.shape[0]
    kw, ncols = w.shape
    bc = _pick(ncols, bc_t)
    off = col0 // bc
    assert col0 % bc == 0

    def body(x_ref, w_ref, b_ref, o_ref):
        x = x_ref[...].astype(f32)
        row = lax.broadcasted_iota(jnp.int32, x.shape, 0)
        c = _conv_pre(x, w_ref[...], b_ref[...], row)
        o_ref[...] = c * jax.nn.sigmoid(c)

    return pl.pallas_call(
        body, name=name, grid=(ncols // bc,),
        in_specs=[pl.BlockSpec((T, bc), lambda j: (0, off + j)), pl.BlockSpec((kw, bc), lambda j: (0, j)),
                  pl.BlockSpec((1, bc), lambda j: (0, j))],
        out_specs=pl.BlockSpec((T, bc), lambda j: (0, j)), out_shape=jax.ShapeDtypeStruct((T, ncols), f32),
        compiler_params=_params(("parallel",)),
    )(xsrc, w, b)


def _conv_bwd(xsrc, col0, w, b, dact, name, bc_t=512):
    T = xsrc.shape[0]
    kw, ncols = w.shape
    bc = _pick(ncols, bc_t)
    off = col0 // bc
    assert col0 % bc == 0

    def body(x_ref, w_ref, b_ref, d_ref, dx_ref, dw_ref, db_ref):
        x = x_ref[...].astype(f32)
        w = w_ref[...]
        row = lax.broadcasted_iota(jnp.int32, x.shape, 0)
        c = _conv_pre(x, w, b_ref[...], row)
        s = jax.nn.sigmoid(c)
        dc = d_ref[...].astype(f32) * (s * (1.0 + c * (1.0 - s)))
        dx = jnp.zeros_like(x)
        dws = []
        for k in range(kw):
            dx = dx + w[k:k + 1, :] * _shift_up(dc, kw - 1 - k, row)
            dws.append(jnp.sum(dc * _shift_down(x, kw - 1 - k, row), axis=0, keepdims=True))
        dx_ref[...] = dx.astype(dx_ref.dtype)
        dw_ref[...] = jnp.concatenate(dws, axis=0)
        db_ref[...] = jnp.sum(dc, axis=0, keepdims=True)

    return pl.pallas_call(
        body, name=name, grid=(ncols // bc,),
        in_specs=[pl.BlockSpec((T, bc), lambda j: (0, off + j)), pl.BlockSpec((kw, bc), lambda j: (0, j)),
                  pl.BlockSpec((1, bc), lambda j: (0, j)), pl.BlockSpec((T, bc), lambda j: (0, j))],
        out_specs=[pl.BlockSpec((T, bc), lambda j: (0, j)), pl.BlockSpec((kw, bc), lambda j: (0, j)),
                   pl.BlockSpec((1, bc), lambda j: (0, j))],
        out_shape=[jax.ShapeDtypeStruct((T, ncols), bf16), jax.ShapeDtypeStruct((kw, ncols), f32),
                   jax.ShapeDtypeStruct((1, ncols), f32)],
        compiler_params=_params(("parallel",)),
    )(xsrc, w, b, dact)


_HI = lax.Precision.HIGHEST


def _dot(a, b, dims="nn", precision=None):
    return lax.dot_general(a, b, _DIMS[dims], preferred_element_type=f32, precision=precision)


def _ssd_common(x_ref, b_ref, c_ref, dt_ref, adt_ref, d_ref, hpg, p):
    q = b_ref.shape[0]
    hp = hpg * p
    bb, cb = b_ref[...].astype(bf16), c_ref[...].astype(bf16)
    r = lax.broadcasted_iota(jnp.int32, (q, q), 0)
    s = lax.broadcasted_iota(jnp.int32, (q, q), 1)
    tril = r >= s
    trilf = tril.astype(f32)
    eh = lax.broadcasted_iota(jnp.int32, (LANES, hp), 0)
    ec = lax.broadcasted_iota(jnp.int32, (LANES, hp), 1)
    expand = ((ec >= eh * p) & (ec < (eh + 1) * p)).astype(f32)
    adt = adt_ref[...]
    cum = _dot(trilf, adt, "nn", _HI)
    cum_t = _dot(adt, (r <= s).astype(f32), "tn", _HI)
    cum_e = _dot(cum, expand, "nn", _HI)
    dt_e = _dot(dt_ref[...], expand, "nn", _HI)
    d_e = _dot(jnp.broadcast_to(d_ref[...], (8, LANES)), expand, "nn", _HI)[0:1, :]
    gmat = _dot(cb, bb, "nt")
    x = x_ref[...]
    xdt = x * dt_e
    e_all = jnp.exp(cum_e)
    dec = jnp.exp(cum_e[q - 1:q, :] - cum_e)
    lms, ms = [], []
    for h in range(hpg):
        lm = jnp.exp(jnp.where(tril, cum[:, h:h + 1] - cum_t[h:h + 1, :], -1e30))
        lms.append(lm)
        ms.append(gmat * lm)
    et = [jnp.exp(cum[q - 1:q, h:h + 1]) for h in range(hpg)]
    return dict(bb=bb, cb=cb, trilf=trilf, expand=expand, cum=cum, x=x, xdt=xdt, dt_e=dt_e, d_e=d_e, e=e_all, dec=dec,
                lms=lms, ms=ms, et=et)


def _ssd_specs(q, hp, n, g_n, nc, rev):
    def cidx(c):
        return (nc - 1 - c) if rev else c
    x_spec = pl.BlockSpec((q, hp), lambda g, c: (cidx(c), g))
    boff = (g_n * hp) // n
    b_spec = pl.BlockSpec((q, n), lambda g, c: (cidx(c), boff + g))
    c_spec = pl.BlockSpec((q, n), lambda g, c: (cidx(c), boff + g_n + g))
    dt_spec = pl.BlockSpec((q, LANES), lambda g, c: (cidx(c), g))
    d_spec = pl.BlockSpec((1, LANES), lambda g, c: (0, g))
    st_spec = pl.BlockSpec((1, 1, hp, n), lambda g, c: (cidx(c), g, 0, 0))
    return x_spec, b_spec, c_spec, dt_spec, d_spec, st_spec


def _ssd_fwd(act, dt, adt, dpad, hpg, p, n, name):
    T = act.shape[0]
    g_n, q = SSD_N_GROUPS, SSD_CHUNK
    nc, hp = T // q, hpg * p
    x_spec, b_spec, c_spec, dt_spec, d_spec, st_spec = _ssd_specs(q, hp, n, g_n, nc, False)

    def body(x_ref, b_ref, c_ref, dt_ref, adt_ref, d_ref, y_ref, st_ref, s_scr):
        @pl.when(pl.program_id(1) == 0)
        def _():
            s_scr[...] = jnp.zeros_like(s_scr)

        k = _ssd_common(x_ref, b_ref, c_ref, dt_ref, adt_ref, d_ref, hpg, p)
        s0 = s_scr[...]
        st_ref[0, 0] = s0
        xdtb = k["xdt"].astype(bf16)
        ydiag = [_dot(k["ms"][h].astype(bf16), xdtb[:, h * p:(h + 1) * p]) for h in range(hpg)]
        z = _dot(k["cb"], s0.astype(bf16), "nt")
        y_ref[...] = jnp.concatenate(ydiag, axis=1) + k["e"] * z + k["d_e"] * k["x"]
        upd = _dot((k["xdt"] * k["dec"]).astype(bf16), k["bb"], "tn")
        for h in range(hpg):
            s_scr[h * p:(h + 1) * p, :] = k["et"][h] * s0[h * p:(h + 1) * p, :] + upd[h * p:(h + 1) * p, :]

    return pl.pallas_call(
        body, name=name, grid=(g_n, nc),
        in_specs=[x_spec, b_spec, c_spec, dt_spec, dt_spec, d_spec],
        out_specs=[pl.BlockSpec((q, hp), lambda g, c: (c, g)), st_spec],
        out_shape=[jax.ShapeDtypeStruct((T, g_n * hp), f32), jax.ShapeDtypeStruct((nc, g_n, hp, n), f32)],
        scratch_shapes=[pltpu.VMEM((hp, n), f32)],
        compiler_params=_params(("parallel", "arbitrary")),
    )(act, act, act, dt, adt, dpad)


def _ssd_bwd(act, dt, adt, dpad, states, dy, hpg, p, n, name):
    T = act.shape[0]
    g_n, q = SSD_N_GROUPS, SSD_CHUNK
    nc, hp = T // q, hpg * p
    x_spec, b_spec, c_spec, dt_spec, d_spec, st_spec = _ssd_specs(q, hp, n, g_n, nc, True)

    def body(x_ref, b_ref, c_ref, dt_ref, adt_ref, d_ref, st_ref, dy_ref,
             dx_ref, db_ref, dc_ref, ddt_ref, dadt_ref, dd_ref, ds_scr):
        first = pl.program_id(1) == 0

        @pl.when(first)
        def _():
            ds_scr[...] = jnp.zeros_like(ds_scr)

        k = _ssd_common(x_ref, b_ref, c_ref, dt_ref, adt_ref, d_ref, hpg, p)
        bb, cb, expand, x, xdt, dec = k["bb"], k["cb"], k["expand"], k["x"], k["xdt"], k["dec"]
        heads = lambda t: _dot(t, expand, "nt", _HI)
        s0 = st_ref[0, 0]
        ds1 = ds_scr[...]
        s0b, ds1b = s0.astype(bf16), ds1.astype(bf16)
        dy = dy_ref[...]
        dyb, xdtb = dy.astype(bf16), xdt.astype(bf16)
        lane = lax.broadcasted_iota(jnp.int32, (1, LANES), 1)
        dg = jnp.zeros((q, q), f32)
        w_rows = jnp.zeros((q, LANES), f32)
        w_cols, dxdt_parts = [], []
        for h in range(hpg):
            hs = slice(h * p, (h + 1) * p)
            dm = _dot(dyb[:, hs], xdtb[:, hs], "nt")
            dg = dg + dm * k["lms"][h]
            wm = dm * k["ms"][h]
            w_rows = w_rows + jnp.sum(wm, axis=1, keepdims=True) * (lane == h).astype(f32)
            w_cols.append(jnp.sum(wm, axis=0, keepdims=True))
            dxdt_parts.append(_dot(k["ms"][h].astype(bf16), dyb[:, hs], "tn"))
        dxdt_diag = jnp.concatenate(dxdt_parts, axis=1)
        w_cols = jnp.concatenate(w_cols + [jnp.zeros((LANES - hpg, q), f32)], axis=0).T
        dgb = dg.astype(bf16)
        z = _dot(cb, s0b, "nt")
        dz = dy * k["e"]
        dzb = dz.astype(bf16)
        dxd = _dot(bb, ds1b, "nt")
        ddec = dxd * xdt * dec
        db_ref[...] = _dot(dgb, cb, "tn") + _dot((xdt * dec).astype(bf16), ds1b)
        dc_ref[...] = _dot(dgb, bb) + _dot(dzb, s0b)
        ds0 = _dot(dzb, cb, "tn")
        for h in range(hpg):
            hs = slice(h * p, (h + 1) * p)
            ds_scr[hs, :] = ds0[hs, :] + k["et"][h] * ds1[hs, :]
        dxdt = dxdt_diag + dxd * dec
        ddec_h = heads(ddec)
        dcum = w_rows - w_cols + heads(dz * z) - ddec_h
        et_row = jnp.exp(k["cum"][q - 1:q, :])
        dsum = _dot(jnp.ones((8, n), f32), _dot(expand, ds1 * s0, "nn", _HI), "nt", _HI)[0:1, :]
        dcl = dsum * et_row + jnp.sum(ddec_h, axis=0, keepdims=True)
        rowq = lax.broadcasted_iota(jnp.int32, (q, 1), 0)
        dcum = dcum + (rowq == q - 1).astype(f32) * dcl
        ddt_ref[...] = heads(dxdt * x)
        dadt_ref[...] = _dot(k["trilf"], dcum, "tn", _HI)
        dx_ref[...] = k["d_e"] * dy + dxdt * k["dt_e"]
        dd8 = heads(jnp.broadcast_to(jnp.sum(dy * x, axis=0, keepdims=True), (8, hp)))

        @pl.when(first)
        def _():
            dd_ref[...] = dd8

        @pl.when(jnp.logical_not(first))
        def _():
            dd_ref[...] += dd8

    rc = lambda g, c: (nc - 1 - c, g)
    return pl.pallas_call(
        body, name=name, grid=(g_n, nc),
        in_specs=[x_spec, b_spec, c_spec, dt_spec, dt_spec, d_spec, st_spec, pl.BlockSpec((q, hp), rc)],
        out_specs=[pl.BlockSpec((q, hp), rc), pl.BlockSpec((q, n), rc), pl.BlockSpec((q, n), rc),
                   pl.BlockSpec((q, LANES), rc), pl.BlockSpec((q, LANES), rc), pl.BlockSpec((8, LANES), lambda g, c: (g, 0))],
        out_shape=[jax.ShapeDtypeStruct((T, g_n * hp), f32), jax.ShapeDtypeStruct((T, g_n * n), f32),
                   jax.ShapeDtypeStruct((T, g_n * n), f32), jax.ShapeDtypeStruct((T, g_n * LANES), f32),
                   jax.ShapeDtypeStruct((T, g_n * LANES), f32), jax.ShapeDtypeStruct((g_n * 8, LANES), f32)],
        scratch_shapes=[pltpu.VMEM((hp, n), f32)],
        compiler_params=_params(("parallel", "arbitrary")),
    )(act, act, act, dt, adt, dpad, states, dy)


def _cmul(ar, ai, br, bi):
    return ar * br - ai * bi, ar * bi + ai * br


def _s5_tile_powers(lr, li):
    p = [(lr, li)]
    for _ in range(7):
        p.append(_cmul(p[-1][0], p[-1][1], lr, li))
    tile = (jnp.concatenate([q[0] for q in p], axis=0), jnp.concatenate([q[1] for q in p], axis=0))
    return tile, (p[0], p[1], p[3])


def _s5_tile_scan(xr, xi, steps, reverse):
    row = lax.broadcasted_iota(jnp.int32, xr.shape, 0)
    for d, (pr, pi) in zip((1, 2, 4), steps):
        if reverse:
            keep = row < 8 - d
            sr, si = pltpu.roll(xr, 8 - d, 0), pltpu.roll(xi, 8 - d, 0)
        else:
            keep = row >= d
            sr, si = pltpu.roll(xr, d, 0), pltpu.roll(xi, d, 0)
        sr, si = jnp.where(keep, sr, 0.0), jnp.where(keep, si, 0.0)
        ar, ai = _cmul(sr, si, pr, pi)
        xr, xi = xr + ar, xi + ai
    return xr, xi


def _s5_scan_fwd(bu, lam_re, lam_im, nsb, name, tc_t=512):
    T = bu.shape[0]
    w2 = bu.shape[1] // nsb
    w = w2 // 2
    tc = _pick(T, tc_t, 8)

    def body(bu_ref, lr_ref, li_ref, st_ref, carry):
        @pl.when(pl.program_id(1) == 0)
        def _():
            carry[...] = jnp.zeros_like(carry)

        (pr8, pi8), steps = _s5_tile_powers(lr_ref[0:1, :], li_ref[0:1, :])

        def tile(i, c):
            r = pl.ds(pl.multiple_of(i * 8, 8), 8)
            x = bu_ref[r, :]
            xr, xi = _s5_tile_scan(x[:, :w], x[:, w:], steps, False)
            ar, ai = _cmul(pr8, pi8, c[0], c[1])
            xr, xi = xr + ar, xi + ai
            st_ref[r, :] = jnp.concatenate([xr, xi], axis=1)
            return xr[7:8, :], xi[7:8, :]

        c = lax.fori_loop(0, tc // 8, tile, (carry[0:1, :], carry[1:2, :]), unroll=2)
        carry[0:1, :] = c[0]
        carry[1:2, :] = c[1]

    return pl.pallas_call(
        body, name=name, grid=(nsb, T // tc),
        in_specs=[pl.BlockSpec((tc, w2), lambda j, i: (i, j)), pl.BlockSpec((8, w), lambda j, i: (j, 0)),
                  pl.BlockSpec((8, w), lambda j, i: (j, 0))],
        out_specs=pl.BlockSpec((tc, w2), lambda j, i: (i, j)), out_shape=jax.ShapeDtypeStruct(bu.shape, f32),
        scratch_shapes=[pltpu.VMEM((8, w), f32)],
        compiler_params=_params(("parallel", "arbitrary")),
    )(bu, lam_re, lam_im)


def _s5_scan_bwd(gst, states, lam_re, lam_im, nsb, name, tc_t=512):
    T = gst.shape[0]
    w2 = gst.shape[1] // nsb
    w = w2 // 2
    tc = _pick(T, tc_t, 8)
    nt = T // tc
    n_tiles = tc // 8

    def body(g_ref, s_ref, sp_ref, lr_ref, li_ref, a_ref, dlr_ref, dli_ref, carry, acc):
        chunk = pl.program_id(1)

        @pl.when(chunk == 0)
        def _():
            carry[...] = jnp.zeros_like(carry)
            acc[...] = jnp.zeros_like(acc)

        (qr8, qi8), steps = _s5_tile_powers(lr_ref[0:1, :], -li_ref[0:1, :])
        row = lax.broadcasted_iota(jnp.int32, (8, w), 0)
        rev_r, rev_i = jnp.zeros((8, w), f32), jnp.zeros((8, w), f32)
        for r in range(8):
            rev_r = jnp.where(row == r, qr8[7 - r:8 - r, :], rev_r)
            rev_i = jnp.where(row == r, qi8[7 - r:8 - r, :], rev_i)
        row2 = lax.broadcasted_iota(jnp.int32, (8, w2), 0)

        def tile(k, c):
            ar_in, ai_in, dr, di = c
            i = n_tiles - 1 - k
            r = pl.ds(pl.multiple_of(i * 8, 8), 8)
            x = g_ref[r, :]
            xr, xi = _s5_tile_scan(x[:, :w], x[:, w:], steps, True)
            pr, pi = _cmul(rev_r, rev_i, ar_in, ai_in)
            xr, xi = xr + pr, xi + pi
            a_ref[r, :] = jnp.concatenate([xr, xi], axis=1)
            before = jnp.where(i > 0, s_ref[pl.ds(pl.multiple_of(jnp.maximum(i - 1, 0) * 8, 8), 8), :],
                               sp_ref[tc - 8:tc, :] * (chunk < nt - 1).astype(f32))
            prev = jnp.where(row2 == 0, pltpu.roll(before, 1, 0), pltpu.roll(s_ref[r, :], 1, 0))
            spr, spi = prev[:, :w], prev[:, w:]
            return xr[0:1, :], xi[0:1, :], dr + xr * spr + xi * spi, di - xr * spi + xi * spr

        c0 = (carry[0:1, :], carry[1:2, :], acc[0:8, :], acc[8:16, :])
        ar, ai, dr, di = lax.fori_loop(0, n_tiles, tile, c0, unroll=2)
        carry[0:1, :] = ar
        carry[1:2, :] = ai
        acc[0:8, :] = dr
        acc[8:16, :] = di
        dlr_ref[...] = jnp.broadcast_to(jnp.sum(dr, axis=0, keepdims=True), (8, w))
        dli_ref[...] = jnp.broadcast_to(jnp.sum(di, axis=0, keepdims=True), (8, w))

    cur = lambda j, i: (nt - 1 - i, j)
    prv = lambda j, i: (jnp.maximum(nt - 2 - i, 0), j)
    return pl.pallas_call(
        body, name=name, grid=(nsb, nt),
        in_specs=[pl.BlockSpec((tc, w2), cur), pl.BlockSpec((tc, w2), cur), pl.BlockSpec((tc, w2), prv),
                  pl.BlockSpec((8, w), lambda j, i: (j, 0)), pl.BlockSpec((8, w), lambda j, i: (j, 0))],
        out_specs=[pl.BlockSpec((tc, w2), cur), pl.BlockSpec((8, w), lambda j, i: (j, 0)),
                   pl.BlockSpec((8, w), lambda j, i: (j, 0))],
        out_shape=[jax.ShapeDtypeStruct(gst.shape, f32), jax.ShapeDtypeStruct((nsb * 8, w), f32),
                   jax.ShapeDtypeStruct((nsb * 8, w), f32)],
        scratch_shapes=[pltpu.VMEM((8, w), f32), pltpu.VMEM((16, w), f32)],
        compiler_params=_params(("parallel", "arbitrary")),
    )(gst, states, states, lam_re, lam_im)


def _s5_prep_fn(xs, ps):
    lam_re, lam_im, log_step, b_re, b_im, expand = ps
    lr = jnp.minimum(lam_re, S5_MAX_REAL)
    li = lam_im
    step = jnp.exp(log_step)
    er = jnp.exp(lr * step)
    ang = li * step
    lbr, lbi = er * jnp.cos(ang), er * jnp.sin(ang)
    nr, ni = lbr - 1.0, lbi
    den = lr * lr + li * li
    qr, qi = (nr * lr + ni * li) / den, (ni * lr - nr * li) / den
    qre, qie = _dot(qr, expand, "nn", _HI), _dot(qi, expand, "nn", _HI)
    return [lbr, lbi, qre * b_re - qie * b_im, qre * b_im + qie * b_re]


def _s5_prep(pars, name):
    def body(*refs):
        outs = _s5_prep_fn([], [r[...] for r in refs[:6]])
        for ref, v in zip(refs[6:], outs):
            ref[...] = v

    g, nst = pars[0].shape
    nc = pars[3].shape[1]
    return pl.pallas_call(
        body, name=name,
        out_shape=[jax.ShapeDtypeStruct((g, nst), f32)] * 2 + [jax.ShapeDtypeStruct((g, nc), f32)] * 2,
        compiler_params=_params(),
    )(*pars)


def _s5_prep_bwd(pars, cots, name):
    def body(*refs):
        ps = [r[...] for r in refs[:6]]
        ct = [r[...] for r in refs[6:10]]
        _, vjp = jax.vjp(lambda *a: _s5_prep_fn([], list(a)), *ps)
        g = vjp(ct)
        for ref, v in zip(refs[10:], g[:5]):
            ref[...] = v

    return pl.pallas_call(
        body, name=name, out_shape=[jax.ShapeDtypeStruct(p.shape, f32) for p in pars[:5]], compiler_params=_params(),
    )(*pars, *cots)


_ANY = pl.BlockSpec(memory_space=pl.ANY)


def _remote(src, dst, send_sem, recv_sem, device):
    return pltpu.make_async_remote_copy(src_ref=src, dst_ref=dst, send_sem=send_sem, recv_sem=recv_sem, device_id=device,
                                        device_id_type=MESH)


def _staged_copy(src, dst, buf, in_sems, out_sems):
    n = D2D_STREAMS
    piece = src.shape[0] // n
    assert src.shape[0] % n == 0

    def load(i):
        return pltpu.make_async_copy(src.at[pl.ds(i * piece, piece)], buf.at[i % 2], in_sems.at[i % 2])

    def store(i):
        return pltpu.make_async_copy(buf.at[i % 2], dst.at[pl.ds(i * piece, piece)], out_sems.at[i % 2])

    load(0).start()
    for i in range(n):
        if i + 1 < n:
            if i >= 1:
                store(i - 1).wait()
            load(i + 1).start()
        load(i).wait()
        store(i).start()
    store(n - 2).wait()
    store(n - 1).wait()


def _stage_scratch(rows, cols, dtype):
    return [pltpu.VMEM((2, rows // D2D_STREAMS, cols), dtype), pltpu.SemaphoreType.DMA((2,)), pltpu.SemaphoreType.DMA((2,))]


def _chip_all_gather(block, name):
    rows = block.shape[0]
    half = rows // 2
    piece = half // D2D_STREAMS
    assert rows % (2 * D2D_STREAMS * 16) == 0

    def body(src, out, ici_send, ici_recv, d2d_send, d2d_recv, *stage):
        x, y, c = lax.axis_index("x"), lax.axis_index("y"), lax.axis_index("c")
        me = 2 * x + y
        sibling = (x, y, 1 - c)
        chips = [(1 - x, y), (x, 1 - y), (1 - x, 1 - y)]
        mine = pl.ds(pl.multiple_of(c * half, 16), half)
        sends = []
        for j, (px, py) in enumerate(chips):
            cp = _remote(src.at[mine], out.at[me, mine], ici_send.at[j], ici_recv.at[j], (px, py, c))
            cp.start()
            sends.append(cp)
        _staged_copy(src, out.at[me], *stage)
        for j, (px, py) in enumerate(chips):
            slot = 2 * px + py
            _remote(src.at[mine], out.at[slot, mine], ici_send.at[j], ici_recv.at[j], (px, py, c)).wait_recv()
            for s in range(D2D_STREAMS):
                r = pl.ds(pl.multiple_of(c * half + s * piece, 16), piece)
                k = j * D2D_STREAMS + s
                cp = _remote(out.at[slot, r], out.at[slot, r], d2d_send.at[k], d2d_recv.at[k], sibling)
                cp.start()
                sends.append(cp)
        for j, (px, py) in enumerate(chips):
            slot = 2 * px + py
            for s in range(D2D_STREAMS):
                r = pl.ds(pl.multiple_of((1 - c) * half + s * piece, 16), piece)
                k = j * D2D_STREAMS + s
                _remote(out.at[slot, r], out.at[slot, r], d2d_send.at[k], d2d_recv.at[k], sibling).wait_recv()
        for cp in sends:
            cp.wait_send()

    n_d2d = 3 * D2D_STREAMS
    return pl.pallas_call(
        body, name=name, in_specs=[_ANY], out_specs=_ANY,
        out_shape=jax.ShapeDtypeStruct((N_CHIPS,) + block.shape, block.dtype),
        scratch_shapes=[pltpu.SemaphoreType.DMA((3,)), pltpu.SemaphoreType.DMA((3,)), pltpu.SemaphoreType.DMA((n_d2d,)),
                        pltpu.SemaphoreType.DMA((n_d2d,))] + _stage_scratch(rows, block.shape[1], block.dtype),
    )(block)


def _chip_scatter(parts, name):
    def body(src, out, send_sems, recv_sems, *stage):
        x, y, c = lax.axis_index("x"), lax.axis_index("y"), lax.axis_index("c")
        me = 2 * x + y
        chips = [(1 - x, y), (x, 1 - y), (1 - x, 1 - y)]
        sends = []
        for j, (px, py) in enumerate(chips):
            cp = pltpu.make_async_remote_copy(src_ref=src.at[2 * px + py], dst_ref=out.at[me], send_sem=send_sems.at[j],
                                              recv_sem=recv_sems.at[j], device_id=(px, py, c), device_id_type=MESH)
            cp.start()
            sends.append(cp)
        _staged_copy(src.at[me], out.at[me], *stage)
        for j, (px, py) in enumerate(chips):
            pltpu.make_async_remote_copy(src_ref=src.at[me], dst_ref=out.at[2 * px + py], send_sem=send_sems.at[j],
                                         recv_sem=recv_sems.at[j], device_id=(px, py, c), device_id_type=MESH).wait_recv()
        for cp in sends:
            cp.wait_send()

    return pl.pallas_call(
        body, name=name, in_specs=[_ANY], out_specs=_ANY, out_shape=jax.ShapeDtypeStruct(parts.shape, parts.dtype),
        scratch_shapes=[pltpu.SemaphoreType.DMA((3,)), pltpu.SemaphoreType.DMA((3,))]
        + _stage_scratch(parts.shape[1], parts.shape[2], parts.dtype),
    )(parts)


_HBM = pl.BlockSpec(memory_space=pltpu.HBM)
_SEM = pl.BlockSpec(memory_space=pltpu.SEMAPHORE)
_EFFECT = pltpu.SideEffectType.DATAFLOW_SIDE_EFFECTING


def _chip_scatter_start(parts, name):
    def body(src, land, send_sems, recv_sems, src_out, land_out, token):
        x, y, c = lax.axis_index("x"), lax.axis_index("y"), lax.axis_index("c")
        me = 2 * x + y
        for j, (px, py) in enumerate([(1 - x, y), (x, 1 - y), (1 - x, 1 - y)]):
            _remote(src.at[2 * px + py], land.at[me], send_sems.at[j], recv_sems.at[j], (px, py, c)).start()
        token[...] = jnp.zeros_like(token)

    return pl.pallas_call(
        body, name=name,
        out_shape=(pltpu.SemaphoreType.DMA((3,)), pltpu.SemaphoreType.DMA((3,)), pltpu.HBM(parts.shape, parts.dtype),
                   pltpu.HBM(parts.shape, parts.dtype), jax.ShapeDtypeStruct((8, LANES), f32)),
        in_specs=(_HBM, _HBM), out_specs=(_SEM, _SEM, _HBM, _HBM, pl.BlockSpec(memory_space=pltpu.VMEM)),
        input_output_aliases={0: 2, 1: 3}, compiler_params=pltpu.CompilerParams(has_side_effects=_EFFECT),
    )(pltpu.with_memory_space_constraint(parts, pltpu.HBM),
      pltpu.with_memory_space_constraint(lax.empty(parts.shape, parts.dtype), pltpu.HBM))


def _chip_scatter_wait(send_sems, recv_sems, parts, land, after, name):
    def body(src, land_ref, send_ref, recv_ref, after_ref, src_dead, land_out):
        x, y, c = lax.axis_index("x"), lax.axis_index("y"), lax.axis_index("c")
        me = 2 * x + y
        for j, (px, py) in enumerate([(1 - x, y), (x, 1 - y), (1 - x, 1 - y)]):
            cp = _remote(src.at[2 * px + py], land_ref.at[2 * px + py], send_ref.at[j], recv_ref.at[j], (px, py, c))
            cp.wait_send()
            cp.wait_recv()

    return pl.pallas_call(
        body, name=name, out_shape=(pltpu.HBM(parts.shape, parts.dtype), pltpu.HBM(land.shape, land.dtype)),
        in_specs=(_HBM, _HBM, _SEM, _SEM, _ANY), out_specs=(_HBM, _HBM), input_output_aliases={0: 0, 1: 1},
        compiler_params=pltpu.CompilerParams(has_side_effects=_EFFECT),
    )(parts, land, send_sems, recv_sems, after)


def _sum_slots_own(landed, own, chip, name, block_rows=256):
    s_n, r_n, c_n = landed.shape
    br = _pick(r_n, block_rows, 8)

    def body(chip_ref, land_ref, own_ref, o_ref):
        acc = jnp.zeros((br, c_n), f32)
        for s in range(s_n):
            acc = acc + jnp.where(chip_ref[0] == s, own_ref[s], land_ref[s]).astype(f32)
        o_ref[...] = acc

    spec = pl.BlockSpec((s_n, br, c_n), lambda i, c: (0, i, 0))
    grid_spec = pltpu.PrefetchScalarGridSpec(num_scalar_prefetch=1, grid=(r_n // br,), in_specs=[spec, spec],
                                             out_specs=pl.BlockSpec((br, c_n), lambda i, c: (i, 0)))
    return pl.pallas_call(
        body, name=name, grid_spec=grid_spec, out_shape=jax.ShapeDtypeStruct((r_n, c_n), f32),
        compiler_params=_params(("parallel",)),
    )(chip, landed, own)


def _core_send_other_half(parts, name):
    n_slots, rows, cols = parts.shape
    half = rows // 2
    piece = half // D2D_STREAMS
    assert rows % (2 * D2D_STREAMS * 16) == 0

    def body(src, out, send_sems, recv_sems):
        x, y, c = lax.axis_index("x"), lax.axis_index("y"), lax.axis_index("c")
        sibling = (x, y, 1 - c)
        sends = []
        for k in range(n_slots):
            for s in range(D2D_STREAMS):
                theirs = pl.ds(pl.multiple_of((1 - c) * half + s * piece, 16), piece)
                i = k * D2D_STREAMS + s
                cp = _remote(src.at[k, theirs], out.at[k, pl.ds(s * piece, piece)], send_sems.at[i], recv_sems.at[i], sibling)
                cp.start()
                sends.append(cp)
        for cp in sends:
            cp.wait_recv()
        for cp in sends:
            cp.wait_send()

    n = n_slots * D2D_STREAMS
    return pl.pallas_call(
        body, name=name, in_specs=[_ANY], out_specs=_ANY, out_shape=jax.ShapeDtypeStruct((n_slots, half, cols), parts.dtype),
        scratch_shapes=[pltpu.SemaphoreType.DMA((n,)), pltpu.SemaphoreType.DMA((n,))],
    )(parts)


def _add_my_half(parts, other, core, name, block_rows=256):
    n_slots, rows, cols = parts.shape
    half = rows // 2
    br = _pick(half, block_rows, 16)
    nb = half // br

    def body(c_ref, a_ref, b_ref, o_ref):
        o_ref[...] = (a_ref[...].astype(f32) + b_ref[...].astype(f32)).astype(o_ref.dtype)

    grid_spec = pltpu.PrefetchScalarGridSpec(
        num_scalar_prefetch=1, grid=(n_slots, nb),
        in_specs=[pl.BlockSpec((1, br, cols), lambda k, i, c: (k, c[0] * nb + i, 0)),
                  pl.BlockSpec((1, br, cols), lambda k, i, c: (k, i, 0))],
        out_specs=pl.BlockSpec((1, br, cols), lambda k, i, c: (k, i, 0)))
    return pl.pallas_call(
        body, name=name, grid_spec=grid_spec, out_shape=jax.ShapeDtypeStruct((n_slots, half, cols), parts.dtype),
        compiler_params=_params(("parallel", "parallel")),
    )(core, parts, other)


def _core_join_halves(mine, name):
    half, cols = mine.shape
    piece = half // D2D_STREAMS
    assert half % (D2D_STREAMS * 16) == 0

    def body(src, out, send_sems, recv_sems, *stage):
        x, y, c = lax.axis_index("x"), lax.axis_index("y"), lax.axis_index("c")
        sibling = (x, y, 1 - c)
        sends = []
        for s in range(D2D_STREAMS):
            dst = out.at[pl.ds(pl.multiple_of(c * half + s * piece, 16), piece)]
            cp = _remote(src.at[pl.ds(s * piece, piece)], dst, send_sems.at[s], recv_sems.at[s], sibling)
            cp.start()
            sends.append(cp)
        _staged_copy(src, out.at[pl.ds(pl.multiple_of(c * half, 16), half)], *stage)
        for s in range(D2D_STREAMS):
            dst = out.at[pl.ds(pl.multiple_of((1 - c) * half + s * piece, 16), piece)]
            _remote(src.at[pl.ds(s * piece, piece)], dst, send_sems.at[s], recv_sems.at[s], sibling).wait_recv()
        for cp in sends:
            cp.wait_send()

    return pl.pallas_call(
        body, name=name, in_specs=[_ANY], out_specs=_ANY, out_shape=jax.ShapeDtypeStruct((2 * half, cols), mine.dtype),
        scratch_shapes=[pltpu.SemaphoreType.DMA((D2D_STREAMS,)), pltpu.SemaphoreType.DMA((D2D_STREAMS,))]
        + _stage_scratch(half, cols, mine.dtype),
    )(mine)


def _reduce_start(parts, core, tag):
    chip_part = _add_my_half(parts, _core_send_other_half(parts, f"exchange_core_halves_{tag}"), core, f"sum_core_halves_{tag}")
    return _chip_scatter_start(chip_part, f"scatter_start_{tag}")


def _reduce_finish(started, chip, after, tag):
    send_sems, recv_sems, chip_part, land, _ = started
    own, landed = _chip_scatter_wait(send_sems, recv_sems, chip_part, land, after, f"scatter_wait_{tag}")
    return _core_join_halves(_sum_slots_own(landed, own, chip, f"sum_chip_parts_{tag}"), f"join_core_halves_{tag}")


def _reduce_to_chips(parts, core, tag):
    chip_part = _add_my_half(parts, _core_send_other_half(parts, f"exchange_core_halves_{tag}"), core, f"sum_core_halves_{tag}")
    my_sum = _sum_slots(_chip_scatter(chip_part, f"scatter_{tag}"), f"sum_chip_parts_{tag}")
    return _core_join_halves(my_sum, f"join_core_halves_{tag}")


def _sum_slots(stack, name, block_rows=256):
    s_n, r_n, c_n = stack.shape
    br = _pick(r_n, block_rows, 8)

    def body(in_ref, o_ref):
        acc = in_ref[0].astype(f32)
        for s in range(1, s_n):
            acc = acc + in_ref[s].astype(f32)
        o_ref[...] = acc

    return pl.pallas_call(
        body, name=name, grid=(r_n // br,), in_specs=[pl.BlockSpec((s_n, br, c_n), lambda i: (0, i, 0))],
        out_specs=pl.BlockSpec((br, c_n), lambda i: (i, 0)), out_shape=jax.ShapeDtypeStruct((r_n, c_n), f32),
        compiler_params=_params(("parallel",)),
    )(stack)


def _concat_padded(parts, mult):
    rows = sum(p.shape[0] for p in parts)
    pad = (-rows) % mult
    if pad:
        parts = parts + [jnp.zeros((pad,) + parts[0].shape[1:], parts[0].dtype)]
    return jnp.concatenate(parts, axis=0)


def _pack_weights(w, conv_w):
    parts = [w[n].astype(bf16).reshape(-1, PACK_COLS) for n in BIG]
    conv = lax.bitcast_convert_type(conv_w, bf16).reshape(-1, PACK_COLS)
    parts.append(_concat_padded([conv], 16))
    return _concat_padded(parts, PACK_ROW_MULT)


def _unpack_weights(full, w, conv_w):
    start, r0 = {}, 0
    for n in BIG:
        start[n] = r0
        r0 += w[n].size // PACK_COLS

    def shards(n, l):
        rows = w[n][l].size // PACK_COLS
        first = start[n] + l * rows
        return [full[k, first:first + rows].reshape(w[n].shape[1:]) for k in range(N_CHIPS)]

    rows = conv_w.size * 2 // PACK_COLS
    pieces = lax.bitcast_convert_type(full[:, r0:r0 + rows].reshape((N_CHIPS,) + conv_w.shape + (2,)), f32)
    return shards, jnp.concatenate([pieces[k] for k in range(N_CHIPS)], axis=2)


def _pack_big_grads(layer_grads):
    parts, slot_rows = [], 0
    for k in range(N_CHIPS):
        slot = []
        for n in BIG:
            for g in layer_grads:
                width = g[n].shape[0] // N_CHIPS
                slot.append(g[n][k * width:(k + 1) * width].astype(bf16).reshape(-1, PACK_COLS))
        slot_rows = sum(p.shape[0] for p in slot)
        pad = (-slot_rows) % PACK_ROW_MULT
        if pad:
            slot.append(jnp.zeros((pad, PACK_COLS), bf16))
        slot_rows += pad
        parts += slot
    return jnp.concatenate(parts, axis=0).reshape(N_CHIPS, slot_rows, PACK_COLS)


def _unpack_big_grads(summed, w):
    out, r0 = {}, 0
    for n in BIG:
        rows = w[n][0].size // PACK_COLS
        out[n] = jnp.stack([s[r0:r0 + rows].reshape(w[n].shape[1:]) for s in summed])
        r0 += rows
    return out


_SMALL_TILE = 8 * LANES


def _pack_small(vals, names):
    parts = []
    for n in names:
        pieces = vals[n] if isinstance(vals[n], list) else [vals[n]]
        size = sum(p.size for p in pieces)
        if all(p.size % _SMALL_TILE == 0 for p in pieces):
            parts += [p.reshape(-1, LANES) for p in pieces]
        else:
            flat = [p.reshape(-1) for p in pieces] + [jnp.zeros(((-size) % _SMALL_TILE,), f32)]
            parts.append(jnp.concatenate(flat).reshape(-1, LANES))
    return _concat_padded(parts, PACK_ROW_MULT)


def _unpack_small(packed, like, names):
    out, r0 = {}, 0
    for n in names:
        size = like[n].size
        rows = -(-size // _SMALL_TILE) * 8
        out[n] = packed[r0:r0 + rows].reshape(-1)[:size].reshape(like[n].shape)
        r0 += rows
    return out


def _dims(w, x):
    d = {}
    d["D"] = x.shape[-1]
    d["T"] = x.shape[-2]
    d["DI"] = w["ssd_norm_g"].shape[-1]
    d["NH"] = w["ssd_dt_bias"].shape[-1]
    d["CD"] = w["ssd_conv_b"].shape[-1]
    d["G"] = SSD_N_GROUPS
    d["HPG"] = d["NH"] // d["G"]
    d["P"] = d["DI"] // d["NH"]
    d["N"] = (d["CD"] - d["DI"]) // (2 * d["G"])
    d["S5G"], d["S5N"] = w["s5_lambda_re"].shape[-2:]
    d["S5C"] = w["s5_b_re"].shape[-1]
    d["S5W"] = d["S5G"] * d["S5C"]
    d["NSB"] = d["S5W"] // S5_SUPERBLOCK
    d["GSB"] = d["S5G"] // d["NSB"]
    return d


def _head_pad(v, d):
    lead = v.shape[:-1]
    v = v.reshape(lead + (d["G"], d["HPG"]))
    v = jnp.concatenate([v, jnp.zeros(lead + (d["G"], LANES - d["HPG"]), v.dtype)], axis=-1)
    return v.reshape(lead + (d["G"] * LANES,))


def _head_unpad(v, d):
    lead = v.shape[:-1]
    return v.reshape(lead + (d["G"], LANES))[..., :d["HPG"]].reshape(lead + (d["NH"],))


def _w_in_perm(shards, d):
    o, nh = d["DI"] + d["CD"], d["NH"]
    r = shards[0].shape[0]

    def rows(lo, hi):
        out = []
        for k, s in enumerate(shards):
            a, b = max(lo, k * r), min(hi, (k + 1) * r)
            if a < b:
                out.append(s[a - k * r:b - k * r])
        return out

    dt = _head_pad(jnp.concatenate(rows(o, o + nh), axis=0).T, d).T
    return jnp.concatenate(rows(0, o) + rows(o + nh, len(shards) * r) + [dt], axis=0)


def _w_in_unperm(g, d):
    o = d["DI"] + d["CD"]
    rest = d["S5W"] + 2 * d["D"]
    return jnp.concatenate([g[:o], _head_unpad(g[o + rest:].T, d).T, g[o:o + rest]], axis=0)


def _s5_block_diag(v, d):
    gsb = d["GSB"]
    g, a, b = v.shape
    row_group = (lax.broadcasted_iota(jnp.int32, (g * a, gsb * b), 0) // a) % gsb
    col_group = lax.broadcasted_iota(jnp.int32, (g * a, gsb * b), 1) // b
    return jnp.where(row_group == col_group, jnp.tile(v.reshape(g * a, b), (1, gsb)), 0)


def _s5_diag_blocks(m, d, a, b):
    gsb = d["GSB"]
    rows = m.shape[0]
    m = m.reshape(rows, gsb, b)
    row_group = (lax.broadcasted_iota(jnp.int32, (rows, gsb, 1), 0) // a) % gsb
    col_group = lax.broadcasted_iota(jnp.int32, (rows, gsb, 1), 1)
    return jnp.sum(jnp.where(row_group == col_group, m, 0), axis=1).reshape(rows // a, a, b)


def _s5_lam_rows(v, d):
    v = v.reshape(d["NSB"], 1, d["GSB"] * d["S5N"])
    return jnp.broadcast_to(v, (d["NSB"], 8, v.shape[-1])).reshape(d["NSB"] * 8, -1)


def _ffn_fwd(h, pre_g, post_g, wgu, wd, tag):
    D = h.shape[1]
    H2 = wgu.shape[0]
    xn = _row_kernel(f"{tag}_norm", _fwd_of(_f_norm), [(h, D, 0)], [pre_g], [(D, bf16)])[0]
    ab = _mm(xn, wgu, "nt", f32, f"{tag}_mm_up")
    hid = _row_kernel(f"{tag}_swiglu", _swiglu_fwd, [(ab, H2, 0)], [], [(H2 // 2, bf16)])[0]
    f = _mm(hid, wd, "nn", f32, f"{tag}_mm_down")
    out = _row_kernel(f"{tag}_resnorm", _fwd_of(_f_resnorm(0.5)), [(h, D, 0), (f, D, 0)], [post_g], [(D, f32)])[0]
    return out, dict(h=h, xn=xn, ab=ab, hid=hid, f=f)


def _ffn_bwd(dh_out, s, pre_g, post_g, wgu, wd, tag):
    D = dh_out.shape[1]
    H2 = wgu.shape[0]
    df, dpost = _row_kernel(f"{tag}_resnorm_bwd", _vjp_of(_f_post(0.5), 1, 1, [0]), [(s["f"], D, 0), (dh_out, D, 0)],
                            [post_g], [(D, bf16)], [post_g.shape])
    dwd = _mm(s["hid"], df, "tn", bf16, f"{tag}_mm_dwd")
    dhid = _mm(df, wd, "nt", bf16, f"{tag}_mm_dhid")
    dab = _row_kernel(f"{tag}_swiglu_bwd", _swiglu_bwd, [(s["ab"], H2, 0), (dhid, H2 // 2, 0)], [], [(H2, bf16)])[0]
    dwgu = _mm(dab, s["xn"], "tn", bf16, f"{tag}_mm_dwgu")
    dxn = _mm(dab, wgu, "nn", f32, f"{tag}_mm_dxn")
    dh, dpre = _row_kernel(f"{tag}_norm_bwd", _vjp_of(_f_norm, 1, 1, [0], 1), [(s["h"], D, 0), (dxn, D, 0), (dh_out, D, 0)],
                           [pre_g], [(D, f32)], [pre_g.shape])
    return dh, dict(pre_g=dpre, post_g=dpost, wgu=dwgu, wd=dwd)


def _mixer_fwd(h, p, d):
    D, DI, CD, G, N = d["D"], d["DI"], d["CD"], d["G"], d["N"]
    gl = G * LANES
    c_u5, c_ga, c_gb, c_dt = DI + CD, DI + CD + d["S5W"], DI + CD + d["S5W"] + D, DI + CD + d["S5W"] + 2 * D
    u = _row_kernel("mix_norm", _fwd_of(_f_norm), [(h, D, 0)], [p["mix_pre_g"]], [(D, bf16)])[0]
    proj = _mm(u, p["w_in"], "nt", f32, "mix_mm_in", bn_t=512)
    act = _conv_fwd(proj, DI, p["conv_w"], p["conv_b"], "ssd_conv")
    dt, adt = _row_kernel("ssd_dt", _fwd_of(_f_dt), [(proj, gl, c_dt // gl)], [p["dt_bias"], p["a_log"]], [(gl, f32)] * 2)
    y_ssd, states = _ssd_fwd(act, dt, adt, p["d_skip"], d["HPG"], d["P"], N, "ssd_scan")
    nrm = _row_kernel("ssd_post", _fwd_of(_f_ssdpost(G)), [(y_ssd, DI, 0), (proj, DI, 0)], [p["norm_g"]], [(DI, bf16)])[0]
    y_a = _mm(nrm, p["w_a"], "nn", f32, "mix_mm_a")
    u5 =(proj, d["S5W"], c_u5 // d["S5W"])
    bu = _s5_in(proj, c_u5, p["bsb"], d)
    s5st = _s5_scan_fwd(bu, p["lam_re_rows"], p["lam_im_rows"], d["NSB"], "s5_scan")
    y5 = _bdmm(s5st, p["csb"], "nn", d["NSB"], f32, "s5_mm_c")
    gel = _row_kernel("s5_post", _fwd_of(_f_s5post), [(y5, d["S5W"], 0), u5], [p["s5_d"]], [(d["S5W"], bf16)])[0]
    vg = _mm(gel, p["w_glu"], "nt", f32, "mix_mm_glu")
    glu = _row_kernel("s5_glu", _glu_fwd, [(vg, vg.shape[1], 0)], [], [(vg.shape[1] // 2, bf16)])[0]
    y_b = _mm(glu, p["w_b"], "nn", f32, "mix_mm_b")
    merged = _row_kernel("mix_merge", _fwd_of(_f_merge), [(proj, D, c_ga // D), (y_a, D, 0), (proj, D, c_gb // D), (y_b, D, 0)],
                         [], [(D, bf16)])[0]
    m = _mm(merged, p["w_out"], "nn", f32, "mix_mm_out")
    out = _row_kernel("mix_resnorm", _fwd_of(_f_resnorm(1.0)), [(h, D, 0), (m, D, 0)], [p["mix_post_g"]], [(D, f32)])[0]
    return out, dict(h=h, u=u, proj=proj, act=act, dt=dt, adt=adt, states=states, y_ssd=y_ssd, nrm=nrm, y_a=y_a, s5st=s5st,
                     y5=y5, gel=gel, vg=vg, glu=glu, y_b=y_b, merged=merged, m=m)


def _s5_in(proj, c_u5, bsb, d):
    T = proj.shape[0]
    nsb = d["NSB"]
    ka, nw = S5_SUPERBLOCK, bsb.shape[1]
    off = c_u5 // ka
    assert c_u5 % ka == 0
    bt = _pick(T, 512)

    def body(a_ref, w_ref, o_ref):
        o_ref[...] = _dot(a_ref[...].astype(bf16), w_ref[...].astype(bf16))

    return pl.pallas_call(
        body, name="s5_mm_bu", grid=(nsb, T // bt),
        in_specs=[pl.BlockSpec((bt, ka), lambda j, i: (i, off + j)), pl.BlockSpec((ka, nw), lambda j, i: (j, 0))],
        out_specs=pl.BlockSpec((bt, nw), lambda j, i: (i, j)), out_shape=jax.ShapeDtypeStruct((T, nsb * nw), f32),
        compiler_params=_params(("parallel", "parallel")),
    )(proj, bsb)


def _s5_dbsb(proj, c_u5, a, d):
    T = proj.shape[0]
    nsb = d["NSB"]
    ka, nw = S5_SUPERBLOCK, a.shape[1] // nsb
    off = c_u5 // ka
    bt = _pick(T, 512)

    def body(u_ref, a_ref, o_ref):
        pr = _dot(u_ref[...].astype(bf16), a_ref[...].astype(bf16), "tn")
        k = pl.program_id(1)

        @pl.when(k == 0)
        def _():
            o_ref[...] = pr

        @pl.when(k > 0)
        def _():
            o_ref[...] += pr

    return pl.pallas_call(
        body, name="s5_mm_dbsb", grid=(nsb, T // bt),
        in_specs=[pl.BlockSpec((bt, ka), lambda j, k: (k, off + j)), pl.BlockSpec((bt, nw), lambda j, k: (k, j))],
        out_specs=pl.BlockSpec((ka, nw), lambda j, k: (j, 0)), out_shape=jax.ShapeDtypeStruct((nsb * ka, nw), f32),
        compiler_params=_params(("parallel", "arbitrary")),
    )(proj, a)


def _mixer_bwd(dh_out, s, p, d):
    D, DI, CD, G, N, S5W = d["D"], d["DI"], d["CD"], d["G"], d["N"], d["S5W"]
    gl = G * LANES
    gn = G * N
    c_u5, c_ga, c_gb, c_dt = DI + CD, DI + CD + S5W, DI + CD + S5W + D, DI + CD + S5W + 2 * D
    proj = s["proj"]
    g = {}
    dm, g["mix_post_g"] = _row_kernel("mix_resnorm_bwd", _vjp_of(_f_post(1.0), 1, 1, [0]), [(s["m"], D, 0), (dh_out, D, 0)],
                                      [p["mix_post_g"]], [(D, bf16)], [p["mix_post_g"].shape])
    g["w_out"] = _mm(s["merged"], dm, "tn", bf16, "mix_mm_dwout")
    dmerged = _mm(dm, p["w_out"], "nt", f32, "mix_mm_dmerged")
    dga, dya, dgb, dyb = _row_kernel(
        "mix_merge_bwd", _vjp_of(_f_merge, 4, 1, [0, 1, 2, 3]),
        [(proj, D, c_ga // D), (s["y_a"], D, 0), (proj, D, c_gb // D), (s["y_b"], D, 0), (dmerged, D, 0)], [],
        [(D, bf16), (D, bf16), (D, bf16), (D, bf16)])
    g["w_a"] = _mm(s["nrm"], dya, "tn", bf16, "mix_mm_dwa")
    dnrm = _mm(dya, p["w_a"], "nt", f32, "mix_mm_dnrm")
    dy_ssd, dz, g["norm_g"] = _row_kernel(
        "ssd_post_bwd", _vjp_of(_f_ssdpost(G), 2, 1, [0, 1]), [(s["y_ssd"], DI, 0), (proj, DI, 0), (dnrm, DI, 0)],
        [p["norm_g"]], [(DI, f32), (DI, bf16)], [p["norm_g"].shape])
    dxs, d_b, d_c, ddt, dadt, dd = _ssd_bwd(s["act"], s["dt"], s["adt"], p["d_skip"], s["states"], dy_ssd,
                                            d["HPG"], d["P"], N, "ssd_scan_bwd")
    g["d_skip"] = dd.reshape(G, 8, LANES)[:, 0, :].reshape(1, gl)
    ddt_raw, g["dt_bias"], g["a_log"] = _row_kernel(
        "ssd_dt_bwd", _vjp_of(_f_dt, 1, 2, [0]), [(proj, gl, c_dt // gl), (ddt, gl, 0), (dadt, gl, 0)],
        [p["dt_bias"], p["a_log"]], [(gl, bf16)], [p["dt_bias"].shape, p["a_log"].shape])
    cw, cb = p["conv_w"], p["conv_b"]
    dxc_x, dw_x, db_x = _conv_bwd(proj, DI, cw[:, :DI], cb[:, :DI], dxs, "ssd_conv_bwd_x")
    dxc_b, dw_b, db_b = _conv_bwd(proj, 2 * DI, cw[:, DI:DI + gn], cb[:, DI:DI + gn], d_b, "ssd_conv_bwd_b")
    dxc_c, dw_c, db_c = _conv_bwd(proj, 2 * DI + gn, cw[:, DI + gn:], cb[:, DI + gn:], d_c, "ssd_conv_bwd_c")
    g["conv_w"] = jnp.concatenate([dw_x, dw_b, dw_c], axis=1)
    g["conv_b"] = jnp.concatenate([db_x, db_b, db_c], axis=1)
    g["w_b"] = _mm(s["glu"], dyb, "tn", bf16, "mix_mm_dwb")
    dglu = _mm(dyb, p["w_b"], "nt", f32, "mix_mm_dglu")
    dvg = _row_kernel("s5_glu_bwd", _glu_bwd, [(s["vg"], s["vg"].shape[1], 0), (dglu, S5W, 0)], [], [(s["vg"].shape[1], bf16)])[0]
    g["w_glu"] = _mm(dvg, s["gel"], "tn", bf16, "mix_mm_dwglu")
    dgel = _mm(dvg, p["w_glu"], "nn", f32, "mix_mm_dgel")
    dy5, du5a, g["s5_d"] = _row_kernel(
        "s5_post_bwd", _vjp_of(_f_s5post, 2, 1, [0, 1]), [(s["y5"], S5W, 0), (proj, S5W, c_u5 // S5W), (dgel, S5W, 0)],
        [p["s5_d"]], [(S5W, bf16), (S5W, f32)], [p["s5_d"].shape])
    g["csb"] = _bdmm(s["s5st"], dy5, "tn", d["NSB"], f32, "s5_mm_dcsb")
    gst = _bdmm(dy5, p["csb"], "nt", d["NSB"], f32, "s5_mm_gst")
    a, g["lam_re_rows"], g["lam_im_rows"] = _s5_scan_bwd(gst, s["s5st"], p["lam_re_rows"], p["lam_im_rows"], d["NSB"], "s5_scan_bwd")
    g["bsb"] = _s5_dbsb(proj, c_u5, a, d)
    du5b = _bdmm(a, p["bsb"], "nt", d["NSB"], f32, "s5_mm_du5")
    du5 = _row_kernel("s5_du5", _add_fn, [(du5a, S5W, 0), (du5b, S5W, 0)], [], [(S5W, bf16)])[0]
    dproj = jnp.concatenate([dz, dxc_x, dxc_b, dxc_c, du5, dga, dgb, ddt_raw], axis=1)
    g["w_in"] = _mm(dproj, s["u"], "tn", bf16, "mix_mm_dwin")
    du = _mm(dproj, p["w_in"], "nn", f32, "mix_mm_du", bk_t=2176)
    dh, g["mix_pre_g"] = _row_kernel("mix_norm_bwd", _vjp_of(_f_norm, 1, 1, [0], 1), [(s["h"], D, 0), (du, D, 0), (dh_out, D, 0)],
                                     [p["mix_pre_g"]], [(D, f32)], [p["mix_pre_g"].shape])
    return dh, g


def _layer_params(l, w, wf, conv_w_full, d):
    r2 = lambda v: v[l].reshape(1, -1)
    p = {}
    for n in ["ffn1_pre_g", "ffn1_post_g", "mix_pre_g", "mix_post_g", "ffn2_pre_g", "ffn2_post_g", "s5_d"]:
        p[n] = r2(w[n])
    whole = lambda *names: jnp.concatenate([s for n in names for s in wf(n, l)], axis=0)
    p["wgu1"] = whole("ffn1_w_gate", "ffn1_w_up")
    p["wd1"] = whole("ffn1_w_down")
    p["wgu2"] = whole("ffn2_w_gate", "ffn2_w_up")
    p["wd2"] = whole("ffn2_w_down")
    p["w_in"] = _w_in_perm(wf("w_in", l), d)
    p["w_a"], p["w_glu"], p["w_b"], p["w_out"] = whole("w_branch_a"), whole("s5_w_glu"), whole("w_branch_b"), whole("w_out")
    p["conv_w"] = conv_w_full[l]
    p["conv_b"] = r2(w["ssd_conv_b"])
    p["dt_bias"] = _head_pad(r2(w["ssd_dt_bias"]), d)
    p["a_log"] = _head_pad(r2(w["ssd_a_log"]), d)
    p["d_skip"] = _head_pad(r2(w["ssd_d"]), d)
    p["norm_g"] = r2(w["ssd_norm_g"])
    g5, n5, c5 = d["S5G"], d["S5N"], d["S5C"]
    expand = jnp.repeat(jnp.eye(n5, dtype=f32), c5, axis=1)
    prep_in = [w["s5_lambda_re"][l], w["s5_lambda_im"][l], w["s5_log_step"][l].reshape(g5, 1),
               w["s5_b_re"][l].reshape(g5, n5 * c5), w["s5_b_im"][l].reshape(g5, n5 * c5), expand]
    lbr, lbi, bbr, bbi = _s5_prep(prep_in, "s5_prep")
    p["s5_prep_in"] = prep_in
    p["lam_re_rows"], p["lam_im_rows"] = _s5_lam_rows(lbr, d), _s5_lam_rows(lbi, d)
    to_cn = lambda v: v.reshape(g5, n5, c5).transpose(0, 2, 1)
    p["bsb"] = jnp.concatenate([_s5_block_diag(to_cn(bbr), d), _s5_block_diag(to_cn(bbi), d)], axis=1).astype(bf16)
    c_re, c_im = w["s5_c_re"][l].transpose(0, 2, 1), w["s5_c_im"][l].transpose(0, 2, 1)
    nsb = d["NSB"]
    csb = jnp.stack([_s5_block_diag(c_re, d).reshape(nsb, -1, S5_SUPERBLOCK),
                     _s5_block_diag(-c_im, d).reshape(nsb, -1, S5_SUPERBLOCK)], axis=1)
    p["csb"] = csb.reshape(-1, S5_SUPERBLOCK).astype(bf16)
    return p


def _s5_param_grads(g, p, d, l):
    g5, n5, c5, nsb, gsb = d["S5G"], d["S5N"], d["S5C"], d["NSB"], d["GSB"]
    wst = gsb * n5
    dbsb = g["bsb"]
    from_cn = lambda v: v.transpose(0, 2, 1).reshape(g5, n5 * c5)
    dbbr = from_cn(_s5_diag_blocks(dbsb[:, :wst], d, c5, n5))
    dbbi = from_cn(_s5_diag_blocks(dbsb[:, wst:], d, c5, n5))
    rows = lambda v: v.reshape(nsb, 8, wst)[:, 0, :].reshape(g5, n5)
    cots = [rows(g["lam_re_rows"]), rows(g["lam_im_rows"]), dbbr, dbbi]
    dlr, dli, dls, dbr, dbi = _s5_prep_bwd(p["s5_prep_in"], cots, "s5_prep_bwd")
    dcsb = g["csb"].reshape(nsb, 2, wst, S5_SUPERBLOCK)
    dcr = _s5_diag_blocks(dcsb[:, 0].reshape(-1, S5_SUPERBLOCK), d, n5, c5).transpose(0, 2, 1)
    dci = -_s5_diag_blocks(dcsb[:, 1].reshape(-1, S5_SUPERBLOCK), d, n5, c5).transpose(0, 2, 1)
    return dict(s5_lambda_re=dlr, s5_lambda_im=dli, s5_log_step=dls.reshape(g5), s5_b_re=dbr.reshape(g5, n5, c5),
                s5_b_im=dbi.reshape(g5, n5, c5), s5_c_re=dcr, s5_c_im=dci)


def kernel(x, ffn1_pre_g, ffn1_post_g, ffn1_w_gate, ffn1_w_up, ffn1_w_down, mix_pre_g, mix_post_g, w_in, ssd_conv_w, ssd_conv_b, ssd_dt_bias, ssd_a_log, ssd_d, ssd_norm_g, w_branch_a, s5_lambda_re, s5_lambda_im, s5_b_re, s5_b_im, s5_c_re, s5_c_im, s5_log_step, s5_d, s5_w_glu, w_branch_b, w_out, ffn2_pre_g, ffn2_post_g, ffn2_w_gate, ffn2_w_up, ffn2_w_down, loss_target, m_ffn1_pre_g, m_ffn1_post_g, m_ffn1_w_gate, m_ffn1_w_up, m_ffn1_w_down, m_mix_pre_g, m_mix_post_g, m_w_in, m_ssd_conv_w, m_ssd_conv_b, m_ssd_dt_bias, m_ssd_a_log, m_ssd_d, m_ssd_norm_g, m_w_branch_a, m_s5_lambda_re, m_s5_lambda_im, m_s5_b_re, m_s5_b_im, m_s5_c_re, m_s5_c_im, m_s5_log_step, m_s5_d, m_s5_w_glu, m_w_branch_b, m_w_out, m_ffn2_pre_g, m_ffn2_post_g, m_ffn2_w_gate, m_ffn2_w_up, m_ffn2_w_down, v_ffn1_pre_g, v_ffn1_post_g, v_ffn1_w_gate, v_ffn1_w_up, v_ffn1_w_down, v_mix_pre_g, v_mix_post_g, v_w_in, v_ssd_conv_w, v_ssd_conv_b, v_ssd_dt_bias, v_ssd_a_log, v_ssd_d, v_ssd_norm_g, v_w_branch_a, v_s5_lambda_re, v_s5_lambda_im, v_s5_b_re, v_s5_b_im, v_s5_c_re, v_s5_c_im, v_s5_log_step, v_s5_d, v_s5_w_glu, v_w_branch_b, v_w_out, v_ffn2_pre_g, v_ffn2_post_g, v_ffn2_w_gate, v_ffn2_w_up, v_ffn2_w_down):
    given = dict(locals())
    for n in COL_SHARDED:
        for prefix in ("", "m_", "v_"):
            given[prefix + n] = given[prefix + n].transpose(0, 2, 1)
    w = {n: given[n] for n in WEIGHTS}
    mom = {n: given["m_" + n] for n in WEIGHTS}
    var = {n: given["v_" + n] for n in WEIGHTS}
    d = _dims(w, x)
    n_layers = w["ffn1_pre_g"].shape[0]
    T, D = d["T"], d["D"]

    gathered = _chip_all_gather(_pack_weights(w, w["ssd_conv_w"]), "gather_weights")
    wf, conv_w_full = _unpack_weights(gathered, w, w["ssd_conv_w"])
    layers = [_layer_params(l, w, wf, conv_w_full, d) for l in range(n_layers)]

    h = x.reshape(T, D)
    saved = []
    for p in layers:
        h, s1 = _ffn_fwd(h, p["ffn1_pre_g"], p["ffn1_post_g"], p["wgu1"], p["wd1"], "ffn1")
        h, sm = _mixer_fwd(h, p, d)
        h, s2 = _ffn_fwd(h, p["ffn2_pre_g"], p["ffn2_post_g"], p["wgu2"], p["wd2"], "ffn2")
        saved.append((s1, sm, s2))
    dh, loss_part = _row_kernel("loss", _loss_fn, [(h, D, 0), (loss_target.reshape(T, D), D, 0)], [], [(D, f32)], [(8, LANES)])
    loss = lax.psum(loss_part[0, 0], ("x", "y", "c"))

    my_core = lax.axis_index("c").astype(jnp.int32).reshape(1)
    my_chip = (2 * lax.axis_index("x") + lax.axis_index("y")).astype(jnp.int32).reshape(1)
    lg, in_flight = [None] * n_layers, [None] * n_layers
    for l in reversed(range(n_layers)):
        p = layers[l]
        s1, sm, s2 = saved[l]
        dh, g2 = _ffn_bwd(dh, s2, p["ffn2_pre_g"], p["ffn2_post_g"], p["wgu2"], p["wd2"], "ffn2")
        dh, gm = _mixer_bwd(dh, sm, p, d)
        dh, g1 = _ffn_bwd(dh, s1, p["ffn1_pre_g"], p["ffn1_post_g"], p["wgu1"], p["wd1"], "ffn1")
        H = p["wd1"].shape[0]
        gl = dict(ffn1_pre_g=g1["pre_g"], ffn1_post_g=g1["post_g"], ffn1_w_gate=g1["wgu"][:H], ffn1_w_up=g1["wgu"][H:],
                  ffn1_w_down=g1["wd"], ffn2_pre_g=g2["pre_g"], ffn2_post_g=g2["post_g"], ffn2_w_gate=g2["wgu"][:H],
                  ffn2_w_up=g2["wgu"][H:], ffn2_w_down=g2["wd"], mix_pre_g=gm["mix_pre_g"], mix_post_g=gm["mix_post_g"],
                  w_in=_w_in_unperm(gm["w_in"], d), ssd_conv_w=gm["conv_w"], ssd_conv_b=gm["conv_b"],
                  ssd_dt_bias=_head_unpad(gm["dt_bias"], d), ssd_a_log=_head_unpad(gm["a_log"], d),
                  ssd_d=_head_unpad(gm["d_skip"], d), ssd_norm_g=gm["norm_g"], w_branch_a=gm["w_a"], s5_d=gm["s5_d"],
                  s5_w_glu=gm["w_glu"], w_branch_b=gm["w_b"], w_out=gm["w_out"])
        gl.update(_s5_param_grads(gm, p, d, l))
        lg[l] = gl
        if l > 0:
            in_flight[l] = _reduce_start(_pack_big_grads([gl]), my_core, f"grads_l{l}")
            dh = dh + in_flight[l][-1][0, 0]
    grad_x = dh.reshape(x.shape)
    summed = [None] * n_layers
    for l in range(1, n_layers):
        summed[l] = _reduce_finish(in_flight[l], my_chip, grad_x, f"grads_l{l}")
    summed[0] = _reduce_to_chips(_pack_big_grads([lg[0]]), my_core, "grads_l0")
    grads = _unpack_big_grads(summed, w)
    small_names = SMALL + ["ssd_conv_w"]
    small_parts = {n: [g[n] for g in lg] for n in small_names}
    small_like = {n: jax.ShapeDtypeStruct((n_layers,) + lg[0][n].shape, f32) for n in small_names}
    small_like.update({n: w[n] for n in SMALL})
    small_pack = _pack_small(small_parts, small_names)
    small_sum = _reduce_to_chips(jnp.broadcast_to(small_pack, (N_CHIPS,) + small_pack.shape), my_core, "small")
    small = _unpack_small(small_sum, small_like, small_names)
    k_me = 2 * lax.axis_index("x") + lax.axis_index("y")
    cw = w["ssd_conv_w"].shape[-1]
    small["ssd_conv_w"] = lax.dynamic_slice_in_dim(small["ssd_conv_w"], k_me * cw, cw, axis=2)
    grads.update(small)

    delta, new_m, new_v = {}, {}, {}
    for n in BIG + ["ssd_conv_w"]:
        delta[n], new_m[n], new_v[n] = _adamw(w[n], grads[n], mom[n], var[n], "adamw_" + n)
    pw, pm, pv = [_pack_small(t, SMALL) for t in (w, mom, var)]
    assert pw.shape[0] <= small_sum.shape[0]
    sd, sm_, sv = _adamw(pw, small_sum[:pw.shape[0]], pm, pv, "adamw_small")
    delta.update(_unpack_small(sd, w, SMALL))
    new_m.update(_unpack_small(sm_, w, SMALL))
    new_v.update(_unpack_small(sv, w, SMALL))
    for n in COL_SHARDED:
        for out in (grads, delta, new_m, new_v):
            out[n] = out[n].transpose(0, 2, 1)
    return (loss, grad_x, *[grads[n] for n in WEIGHTS], *[delta[n] for n in WEIGHTS],
            *[new_m[n] for n in WEIGHTS], *[new_v[n] for n in WEIGHTS])
```

```python
import functools

import numpy as np
import jax
import jax.numpy as jnp
from jax import lax
from jax.experimental import pallas as pl
from jax.experimental.pallas import tpu as pltpu

f32, bf16 = jnp.float32, jnp.bfloat16

SSD_N_GROUPS = 4
SSD_CHUNK = 128
RMS_EPS = 1e-6
S5_MAX_REAL = -1e-4
S5_SUPERBLOCK = 256
ADAM_LR, ADAM_B1, ADAM_B2, ADAM_EPS, ADAM_WD, ADAM_STEP = 0.001, 0.9, 0.999, 1e-08, 0.01, 10

LANES = 128
PACK_COLS = 1024
D2D_STREAMS = 16
PACK_ROW_MULT = 2 * D2D_STREAMS * 16
VMEM_LIMIT_BYTES = 48 * 1024 * 1024
N_CHIPS, N_CORES, N_DEV = 4, 2, 8
MESH = pl.DeviceIdType.MESH

BIG = ["ffn1_w_gate", "ffn1_w_up", "ffn1_w_down", "w_in", "w_branch_a", "s5_w_glu", "w_branch_b", "w_out",
       "ffn2_w_gate", "ffn2_w_up", "ffn2_w_down"]
COL_SHARDED = ["ffn1_w_gate", "ffn1_w_up", "w_in", "s5_w_glu", "ffn2_w_gate", "ffn2_w_up"]
SMALL = ["ffn1_pre_g", "ffn1_post_g", "mix_pre_g", "mix_post_g", "ssd_conv_b", "ssd_norm_g", "s5_lambda_re", "s5_lambda_im",
         "s5_b_re", "s5_b_im", "s5_c_re", "s5_c_im", "s5_d", "ffn2_pre_g", "ffn2_post_g", "s5_log_step", "ssd_dt_bias",
         "ssd_a_log", "ssd_d"]
WEIGHTS = ["ffn1_pre_g", "ffn1_post_g", "ffn1_w_gate", "ffn1_w_up", "ffn1_w_down", "mix_pre_g", "mix_post_g", "w_in",
           "ssd_conv_w", "ssd_conv_b", "ssd_dt_bias", "ssd_a_log", "ssd_d", "ssd_norm_g", "w_branch_a", "s5_lambda_re",
           "s5_lambda_im", "s5_b_re", "s5_b_im", "s5_c_re", "s5_c_im", "s5_log_step", "s5_d", "s5_w_glu", "w_branch_b",
           "w_out", "ffn2_pre_g", "ffn2_post_g", "ffn2_w_gate", "ffn2_w_up", "ffn2_w_down"]


def _params(sem=None):
    return pltpu.CompilerParams(dimension_semantics=sem, vmem_limit_bytes=VMEM_LIMIT_BYTES)


def _pick(n, target, mult=LANES):
    best = None
    for d in range(mult, min(n, target) + 1, mult):
        if n % d == 0:
            best = d
    return best if best is not None else n


_DIMS = {"nn": (((1,), (0,)), ((), ())), "nt": (((1,), (1,)), ((), ())), "tn": (((0,), (0,)), ((), ()))}


def _mm(a, b, mode, out_dtype, name, bm_t=1024, bn_t=1024, bk_t=2816):
    if mode == "nn":
        (M, K), (K2, N) = a.shape, b.shape
    elif mode == "nt":
        (M, K), (N, K2) = a.shape, b.shape
    else:
        (K, M), (K2, N) = a.shape, b.shape
    assert K == K2, (name, a.shape, b.shape)
    bm, bn, bk = _pick(M, bm_t), _pick(N, bn_t), _pick(K, bk_t)
    nk = K // bk
    dn = _DIMS[mode]

    def body(a_ref, b_ref, o_ref, *scratch):
        p = lax.dot_general(a_ref[...].astype(bf16), b_ref[...].astype(bf16), dn, preferred_element_type=f32)
        if nk == 1:
            o_ref[...] = p.astype(o_ref.dtype)
        else:
            acc = scratch[0]
            k = pl.program_id(2)

            @pl.when(k == 0)
            def _():
                acc[...] = p

            @pl.when(k > 0)
            def _():
                acc[...] += p

            @pl.when(k == nk - 1)
            def _():
                o_ref[...] = acc[...].astype(o_ref.dtype)

    if mode == "tn":
        a_spec = pl.BlockSpec((bk, bm), lambda i, j, k: (k, i))
    else:
        a_spec = pl.BlockSpec((bm, bk), lambda i, j, k: (i, k))
    if mode == "nt":
        b_spec = pl.BlockSpec((bn, bk), lambda i, j, k: (j, k))
    else:
        b_spec = pl.BlockSpec((bk, bn), lambda i, j, k: (k, j))
    return pl.pallas_call(
        body, name=name, grid=(M // bm, N // bn, nk), in_specs=[a_spec, b_spec],
        out_specs=pl.BlockSpec((bm, bn), lambda i, j, k: (i, j)), out_shape=jax.ShapeDtypeStruct((M, N), out_dtype),
        scratch_shapes=[pltpu.VMEM((bm, bn), f32)] if nk > 1 else [],
        compiler_params=_params(("parallel", "parallel", "arbitrary")),
    )(a, b)


def _bdmm(a, w, mode, nb, out_dtype, name, bt_t=512):
    if mode == "tn":
        T = a.shape[0]
        ka, nw = a.shape[1] // nb, w.shape[1] // nb
        bt = _pick(T, bt_t)
        nt = T // bt

        def body_tn(a_ref, b_ref, o_ref):
            p = lax.dot_general(a_ref[...].astype(bf16), b_ref[...].astype(bf16), _DIMS["tn"], preferred_element_type=f32)
            k = pl.program_id(1)

            @pl.when(k == 0)
            def _():
                o_ref[...] = p

            @pl.when(k > 0)
            def _():
                o_ref[...] += p

        return pl.pallas_call(
            body_tn, name=name, grid=(nb, nt),
            in_specs=[pl.BlockSpec((bt, ka), lambda j, k: (k, j)), pl.BlockSpec((bt, nw), lambda j, k: (k, j))],
            out_specs=pl.BlockSpec((ka, nw), lambda j, k: (j, 0)), out_shape=jax.ShapeDtypeStruct((nb * ka, nw), f32),
            compiler_params=_params(("parallel", "arbitrary")),
        )(a, w)
    T = a.shape[0]
    ka, nw = w.shape[0] // nb, w.shape[1]
    bt = _pick(T, bt_t)
    kin, kout = (ka, nw) if mode == "nn" else (nw, ka)
    dn = _DIMS[mode]

    def body(a_ref, w_ref, o_ref):
        o_ref[...] = lax.dot_general(a_ref[...].astype(bf16), w_ref[...].astype(bf16), dn,
                                     preferred_element_type=f32).astype(o_ref.dtype)

    return pl.pallas_call(
        body, name=name, grid=(nb, T // bt),
        in_specs=[pl.BlockSpec((bt, kin), lambda j, i: (i, j)), pl.BlockSpec((ka, nw), lambda j, i: (j, 0))],
        out_specs=pl.BlockSpec((bt, kout), lambda j, i: (i, j)), out_shape=jax.ShapeDtypeStruct((T, nb * kout), out_dtype),
        compiler_params=_params(("parallel", "parallel")),
    )(a, w)


def _row_index(i, cb):
    return (i, cb)


def _row_kernel(name, fn, rows, pars, row_outs, par_outs=(), block_rows=256):
    T = rows[0][0].shape[0]
    R = min(block_rows, T)
    assert T % R == 0
    nr, npar, nro = len(rows), len(pars), len(row_outs)

    def body(*refs):
        rv = [r[...] for r in refs[:nr]]
        pv = [r[...] for r in refs[nr:nr + npar]]
        ro, po = fn(rv, pv)
        for ref, v in zip(refs[nr + npar:nr + npar + nro], ro):
            ref[...] = v.astype(ref.dtype)
        if par_outs:
            i = pl.program_id(0)
            prefs = refs[nr + npar + nro:]

            @pl.when(i == 0)
            def _():
                for ref, v in zip(prefs, po):
                    ref[...] = v.astype(f32)

            @pl.when(i > 0)
            def _():
                for ref, v in zip(prefs, po):
                    ref[...] += v.astype(f32)

    in_specs = [pl.BlockSpec((R, nc), functools.partial(_row_index, cb=cb)) for (_, nc, cb) in rows]
    in_specs += [pl.BlockSpec(p.shape, lambda i: (0, 0)) for p in pars]
    out_specs = [pl.BlockSpec((R, nc), lambda i: (i, 0)) for (nc, _) in row_outs]
    out_specs += [pl.BlockSpec(s, lambda i: (0, 0)) for s in par_outs]
    out_shape = [jax.ShapeDtypeStruct((T, nc), dt) for (nc, dt) in row_outs]
    out_shape += [jax.ShapeDtypeStruct(s, f32) for s in par_outs]
    outs = pl.pallas_call(
        body, name=name, grid=(T // R,), in_specs=in_specs, out_specs=out_specs, out_shape=out_shape,
        compiler_params=_params(("arbitrary",) if par_outs else ("parallel",)),
    )(*[r[0] for r in rows], *pars)
    return list(outs)


def _fwd_of(f):
    def fn(rv, pv):
        return f([v.astype(f32) for v in rv], [v.astype(f32) for v in pv]), []
    return fn


def _vjp_of(f, n_x, n_cot, grad_idx, n_add=0):
    def fn(rv, pv):
        xs = [v.astype(f32) for v in rv[:n_x]]
        cots = [v.astype(f32) for v in rv[n_x:n_x + n_cot]]
        adds = rv[n_x + n_cot:n_x + n_cot + n_add]
        ps = [v.astype(f32) for v in pv]
        _, vjp = jax.vjp(lambda *a: f(list(a[:n_x]), list(a[n_x:])), *xs, *ps)
        g = vjp(cots)
        row_g = [g[i] for i in grad_idx]
        for k, a in enumerate(adds):
            row_g[k] = row_g[k] + a.astype(f32)
        return row_g, list(g[n_x:])
    return fn


def _rms(x, g):
    return x * lax.rsqrt(jnp.mean(x * x, axis=-1, keepdims=True) + RMS_EPS) * g


def _f_norm(xs, ps):
    return [_rms(xs[0], ps[0])]


def _f_post(scale):
    def f(xs, ps):
        return [scale * _rms(xs[0], ps[0])]
    return f


def _f_resnorm(scale):
    def f(xs, ps):
        return [xs[0] + scale * _rms(xs[1], ps[0])]
    return f


def _f_dt(xs, ps):
    dt = jax.nn.softplus(xs[0] + ps[0])
    return [dt, -jnp.exp(ps[1]) * dt]


def _f_ssdpost(n_groups):
    def f(xs, ps):
        y = xs[0] * jax.nn.silu(xs[1])
        width = y.shape[-1] // n_groups
        lane = lax.broadcasted_iota(jnp.int32, y.shape, 1)
        scale = jnp.zeros_like(y)
        for k in range(n_groups):
            m = ((lane >= k * width) & (lane < (k + 1) * width)).astype(f32)
            ms = jnp.sum(y * y * m, axis=-1, keepdims=True) / width
            scale = scale + lax.rsqrt(ms + RMS_EPS) * m
        return [y * scale * ps[0]]
    return f


def _f_s5post(xs, ps):
    return [jax.nn.gelu(xs[0] + ps[0] * xs[1])]


def _f_merge(xs, ps):
    return [jax.nn.sigmoid(xs[0]) * xs[1] + jax.nn.sigmoid(xs[2]) * xs[3]]


def _swiglu_fwd(rv, pv):
    ab = rv[0].astype(f32)
    h = ab.shape[1] // 2
    return [jax.nn.silu(ab[:, :h]) * ab[:, h:]], []


def _swiglu_bwd(rv, pv):
    ab, d = rv[0].astype(f32), rv[1].astype(f32)
    h = ab.shape[1] // 2
    a, b = ab[:, :h], ab[:, h:]
    s = jax.nn.sigmoid(a)
    return [jnp.concatenate([d * b * (s * (1.0 + a * (1.0 - s))), d * (a * s)], axis=1)], []


def _glu_fwd(rv, pv):
    vg = rv[0].astype(f32)
    h = vg.shape[1] // 2
    return [vg[:, :h] * jax.nn.sigmoid(vg[:, h:])], []


def _glu_bwd(rv, pv):
    vg, d = rv[0].astype(f32), rv[1].astype(f32)
    h = vg.shape[1] // 2
    s = jax.nn.sigmoid(vg[:, h:])
    return [jnp.concatenate([d * s, d * vg[:, :h] * s * (1.0 - s)], axis=1)], []


def _loss_fn(rv, pv):
    e = rv[0].astype(f32) - rv[1].astype(f32)
    per_tok = jnp.mean(e * e, axis=-1, keepdims=True)
    part = 0.5 * jnp.sum(per_tok, axis=0, keepdims=True)
    return [e / e.shape[-1]], [jnp.broadcast_to(part, (8, LANES))]


def _add_fn(rv, pv):
    return [rv[0].astype(f32) + rv[1].astype(f32)], []


def _adamw_fn(rv, pv):
    w, g, m, v = [x.astype(f32) for x in rv]
    m = ADAM_B1 * m + (1.0 - ADAM_B1) * g
    v = ADAM_B2 * v + (1.0 - ADAM_B2) * (g * g)
    m_hat = m / (1.0 - ADAM_B1 ** ADAM_STEP)
    v_hat = v / (1.0 - ADAM_B2 ** ADAM_STEP)
    return [-ADAM_LR * (m_hat / (jnp.sqrt(v_hat) + ADAM_EPS) + ADAM_WD * w), m, v], []


def _adamw(w, g, m, v, name):
    shape = w.shape
    cols = shape[-1] if (w.ndim >= 2 and shape[-1] >= LANES) else None
    if cols is None:
        n = int(np.prod(shape))
        cols = LANES if n % LANES == 0 else n
    n_rows = int(np.prod(shape)) // cols
    br, bc = _pick(n_rows, 256, 8), cols
    if br < 64 and cols % LANES == 0:
        br, bc = n_rows, LANES

    def body(w_ref, g_ref, m_ref, v_ref, d_ref, nm_ref, nv_ref):
        outs, _ = _adamw_fn([w_ref[...], g_ref[...], m_ref[...], v_ref[...]], [])
        d_ref[...], nm_ref[...], nv_ref[...] = outs

    spec = pl.BlockSpec((br, bc), lambda i, j: (i, j))
    outs = pl.pallas_call(
        body, name=name, grid=(n_rows // br, cols // bc), in_specs=[spec] * 4, out_specs=[spec] * 3,
        out_shape=[jax.ShapeDtypeStruct((n_rows, cols), f32)] * 3, compiler_params=_params(("parallel", "parallel")),
    )(*[t.reshape(n_rows, cols) for t in (w, g, m, v)])
    return [o.reshape(shape) for o in outs]


def _shift_down(x, s, row):
    if s == 0:
        return x
    return jnp.where(row >= s, pltpu.roll(x, s, 0), 0.0)


def _shift_up(x, s, row):
    if s == 0:
        return x
    n = x.shape[0]
    return jnp.where(row < n - s, pltpu.roll(x, n - s, 0), 0.0)


def _conv_pre(x, w, b, row):
    kw = w.shape[0]
    c = b
    for k in range(kw):
        c = c + w[k:k + 1, :] * _shift_down(x, kw - 1 - k, row)
    return c


def _conv_fwd(xsrc, col0, w, b, name, bc_t=512):
    T = xsrc.shape[0]
    kw, ncols = w.shape
    bc = _pick(ncols, bc_t)
    off = col0 // bc
    assert col0 % bc == 0

    def body(x_ref, w_ref, b_ref, o_ref):
        x = x_ref[...].astype(f32)
        row = lax.broadcasted_iota(jnp.int32, x.shape, 0)
        c = _conv_pre(x, w_ref[...], b_ref[...], row)
        o_ref[...] = c * jax.nn.sigmoid(c)

    return pl.pallas_call(
        body, name=name, grid=(ncols // bc,),
        in_specs=[pl.BlockSpec((T, bc), lambda j: (0, off + j)), pl.BlockSpec((kw, bc), lambda j: (0, j)),
                  pl.BlockSpec((1, bc), lambda j: (0, j))],
        out_specs=pl.BlockSpec((T, bc), lambda j: (0, j)), out_shape=jax.ShapeDtypeStruct((T, ncols), f32),
        compiler_params=_params(("parallel",)),
    )(xsrc, w, b)


def _conv_bwd(xsrc, col0, w, b, dact, name, bc_t=512):
    T = xsrc.shape[0]
    kw, ncols = w.shape
    bc = _pick(ncols, bc_t)
    off = col0 // bc
    assert col0 % bc == 0

    def body(x_ref, w_ref, b_ref, d_ref, dx_ref, dw_ref, db_ref):
        x = x_ref[...].astype(f32)
        w = w_ref[...]
        row = lax.broadcasted_iota(jnp.int32, x.shape, 0)
        c = _conv_pre(x, w, b_ref[...], row)
        s = jax.nn.sigmoid(c)
        dc = d_ref[...].astype(f32) * (s * (1.0 + c * (1.0 - s)))
        dx = jnp.zeros_like(x)
        dws = []
        for k in range(kw):
            dx = dx + w[k:k + 1, :] * _shift_up(dc, kw - 1 - k, row)
            dws.append(jnp.sum(dc * _shift_down(x, kw - 1 - k, row), axis=0, keepdims=True))
        dx_ref[...] = dx.astype(dx_ref.dtype)
        dw_ref[...] = jnp.concatenate(dws, axis=0)
        db_ref[...] = jnp.sum(dc, axis=0, keepdims=True)

    return pl.pallas_call(
        body, name=name, grid=(ncols // bc,),
        in_specs=[pl.BlockSpec((T, bc), lambda j: (0, off + j)), pl.BlockSpec((kw, bc), lambda j: (0, j)),
                  pl.BlockSpec((1, bc), lambda j: (0, j)), pl.BlockSpec((T, bc), lambda j: (0, j))],
        out_specs=[pl.BlockSpec((T, bc), lambda j: (0, j)), pl.BlockSpec((kw, bc), lambda j: (0, j)),
                   pl.BlockSpec((1, bc), lambda j: (0, j))],
        out_shape=[jax.ShapeDtypeStruct((T, ncols), bf16), jax.ShapeDtypeStruct((kw, ncols), f32),
                   jax.ShapeDtypeStruct((1, ncols), f32)],
        compiler_params=_params(("parallel",)),
    )(xsrc, w, b, dact)


_HI = lax.Precision.HIGHEST


def _dot(a, b, dims="nn", precision=None):
    return lax.dot_general(a, b, _DIMS[dims], preferred_element_type=f32, precision=precision)


def _ssd_common(x_ref, b_ref, c_ref, dt_ref, adt_ref, d_ref, hpg, p):
    q = b_ref.shape[0]
    hp = hpg * p
    bb, cb = b_ref[...].astype(bf16), c_ref[...].astype(bf16)
    r = lax.broadcasted_iota(jnp.int32, (q, q), 0)
    s = lax.broadcasted_iota(jnp.int32, (q, q), 1)
    tril = r >= s
    trilf = tril.astype(f32)
    eh = lax.broadcasted_iota(jnp.int32, (LANES, hp), 0)
    ec = lax.broadcasted_iota(jnp.int32, (LANES, hp), 1)
    expand = ((ec >= eh * p) & (ec < (eh + 1) * p)).astype(f32)
    adt = adt_ref[...]
    cum = _dot(trilf, adt, "nn", _HI)
    cum_t = _dot(adt, (r <= s).astype(f32), "tn", _HI)
    cum_e = _dot(cum, expand, "nn", _HI)
    dt_e = _dot(dt_ref[...], expand, "nn", _HI)
    d_e = _dot(jnp.broadcast_to(d_ref[...], (8, LANES)), expand, "nn", _HI)[0:1, :]
    gmat = _dot(cb, bb, "nt")
    x = x_ref[...]
    xdt = x * dt_e
    e_all = jnp.exp(cum_e)
    dec = jnp.exp(cum_e[q - 1:q, :] - cum_e)
    lms, ms = [], []
    for h in range(hpg):
        lm = jnp.exp(jnp.where(tril, cum[:, h:h + 1] - cum_t[h:h + 1, :], -1e30))
        lms.append(lm)
        ms.append(gmat * lm)
    et = [jnp.exp(cum[q - 1:q, h:h + 1]) for h in range(hpg)]
    return dict(bb=bb, cb=cb, trilf=trilf, expand=expand, cum=cum, x=x, xdt=xdt, dt_e=dt_e, d_e=d_e, e=e_all, dec=dec,
                lms=lms, ms=ms, et=et)


def _ssd_specs(q, hp, n, g_n, nc, rev):
    def cidx(c):
        return (nc - 1 - c) if rev else c
    x_spec = pl.BlockSpec((q, hp), lambda g, c: (cidx(c), g))
    boff = (g_n * hp) // n
    b_spec = pl.BlockSpec((q, n), lambda g, c: (cidx(c), boff + g))
    c_spec = pl.BlockSpec((q, n), lambda g, c: (cidx(c), boff + g_n + g))
    dt_spec = pl.BlockSpec((q, LANES), lambda g, c: (cidx(c), g))
    d_spec = pl.BlockSpec((1, LANES), lambda g, c: (0, g))
    st_spec = pl.BlockSpec((1, 1, hp, n), lambda g, c: (cidx(c), g, 0, 0))
    return x_spec, b_spec, c_spec, dt_spec, d_spec, st_spec


def _ssd_fwd(act, dt, adt, dpad, hpg, p, n, name):
    T = act.shape[0]
    g_n, q = SSD_N_GROUPS, SSD_CHUNK
    nc, hp = T // q, hpg * p
    x_spec, b_spec, c_spec, dt_spec, d_spec, st_spec = _ssd_specs(q, hp, n, g_n, nc, False)

    def body(x_ref, b_ref, c_ref, dt_ref, adt_ref, d_ref, y_ref, st_ref, s_scr):
        @pl.when(pl.program_id(1) == 0)
        def _():
            s_scr[...] = jnp.zeros_like(s_scr)

        k = _ssd_common(x_ref, b_ref, c_ref, dt_ref, adt_ref, d_ref, hpg, p)
        s0 = s_scr[...]
        st_ref[0, 0] = s0
        xdtb = k["xdt"].astype(bf16)
        ydiag = [_dot(k["ms"][h].astype(bf16), xdtb[:, h * p:(h + 1) * p]) for h in range(hpg)]
        z = _dot(k["cb"], s0.astype(bf16), "nt")
        y_ref[...] = jnp.concatenate(ydiag, axis=1) + k["e"] * z + k["d_e"] * k["x"]
        upd = _dot((k["xdt"] * k["dec"]).astype(bf16), k["bb"], "tn")
        for h in range(hpg):
            s_scr[h * p:(h + 1) * p, :] = k["et"][h] * s0[h * p:(h + 1) * p, :] + upd[h * p:(h + 1) * p, :]

    return pl.pallas_call(
        body, name=name, grid=(g_n, nc),
        in_specs=[x_spec, b_spec, c_spec, dt_spec, dt_spec, d_spec],
        out_specs=[pl.BlockSpec((q, hp), lambda g, c: (c, g)), st_spec],
        out_shape=[jax.ShapeDtypeStruct((T, g_n * hp), f32), jax.ShapeDtypeStruct((nc, g_n, hp, n), f32)],
        scratch_shapes=[pltpu.VMEM((hp, n), f32)],
        compiler_params=_params(("parallel", "arbitrary")),
    )(act, act, act, dt, adt, dpad)


def _ssd_bwd(act, dt, adt, dpad, states, dy, hpg, p, n, name):
    T = act.shape[0]
    g_n, q = SSD_N_GROUPS, SSD_CHUNK
    nc, hp = T // q, hpg * p
    x_spec, b_spec, c_spec, dt_spec, d_spec, st_spec = _ssd_specs(q, hp, n, g_n, nc, True)

    def body(x_ref, b_ref, c_ref, dt_ref, adt_ref, d_ref, st_ref, dy_ref,
             dx_ref, db_ref, dc_ref, ddt_ref, dadt_ref, dd_ref, ds_scr):
        first = pl.program_id(1) == 0

        @pl.when(first)
        def _():
            ds_scr[...] = jnp.zeros_like(ds_scr)

        k = _ssd_common(x_ref, b_ref, c_ref, dt_ref, adt_ref, d_ref, hpg, p)
        bb, cb, expand, x, xdt, dec = k["bb"], k["cb"], k["expand"], k["x"], k["xdt"], k["dec"]
        heads = lambda t: _dot(t, expand, "nt", _HI)
        s0 = st_ref[0, 0]
        ds1 = ds_scr[...]
        s0b, ds1b = s0.astype(bf16), ds1.astype(bf16)
        dy = dy_ref[...]
        dyb, xdtb = dy.astype(bf16), xdt.astype(bf16)
        lane = lax.broadcasted_iota(jnp.int32, (1, LANES), 1)
        dg = jnp.zeros((q, q), f32)
        w_rows = jnp.zeros((q, LANES), f32)
        w_cols, dxdt_parts = [], []
        for h in range(hpg):
            hs = slice(h * p, (h + 1) * p)
            dm = _dot(dyb[:, hs], xdtb[:, hs], "nt")
            dg = dg + dm * k["lms"][h]
            wm = dm * k["ms"][h]
            w_rows = w_rows + jnp.sum(wm, axis=1, keepdims=True) * (lane == h).astype(f32)
            w_cols.append(jnp.sum(wm, axis=0, keepdims=True))
            dxdt_parts.append(_dot(k["ms"][h].astype(bf16), dyb[:, hs], "tn"))
        dxdt_diag = jnp.concatenate(dxdt_parts, axis=1)
        w_cols = jnp.concatenate(w_cols + [jnp.zeros((LANES - hpg, q), f32)], axis=0).T
        dgb = dg.astype(bf16)
        z = _dot(cb, s0b, "nt")
        dz = dy * k["e"]
        dzb = dz.astype(bf16)
        dxd = _dot(bb, ds1b, "nt")
        ddec = dxd * xdt * dec
        db_ref[...] = _dot(dgb, cb, "tn") + _dot((xdt * dec).astype(bf16), ds1b)
        dc_ref[...] = _dot(dgb, bb) + _dot(dzb, s0b)
        ds0 = _dot(dzb, cb, "tn")
        for h in range(hpg):
            hs = slice(h * p, (h + 1) * p)
            ds_scr[hs, :] = ds0[hs, :] + k["et"][h] * ds1[hs, :]
        dxdt = dxdt_diag + dxd * dec
        ddec_h = heads(ddec)
        dcum = w_rows - w_cols + heads(dz * z) - ddec_h
        et_row = jnp.exp(k["cum"][q - 1:q, :])
        dsum = _dot(jnp.ones((8, n), f32), _dot(expand, ds1 * s0, "nn", _HI), "nt", _HI)[0:1, :]
        dcl = dsum * et_row + jnp.sum(ddec_h, axis=0, keepdims=True)
        rowq = lax.broadcasted_iota(jnp.int32, (q, 1), 0)
        dcum = dcum + (rowq == q - 1).astype(f32) * dcl
        ddt_ref[...] = heads(dxdt * x)
        dadt_ref[...] = _dot(k["trilf"], dcum, "tn", _HI)
        dx_ref[...] = k["d_e"] * dy + dxdt * k["dt_e"]
        dd8 = heads(jnp.broadcast_to(jnp.sum(dy * x, axis=0, keepdims=True), (8, hp)))

        @pl.when(first)
        def _():
            dd_ref[...] = dd8

        @pl.when(jnp.logical_not(first))
        def _():
            dd_ref[...] += dd8

    rc = lambda g, c: (nc - 1 - c, g)
    return pl.pallas_call(
        body, name=name, grid=(g_n, nc),
        in_specs=[x_spec, b_spec, c_spec, dt_spec, dt_spec, d_spec, st_spec, pl.BlockSpec((q, hp), rc)],
        out_specs=[pl.BlockSpec((q, hp), rc), pl.BlockSpec((q, n), rc), pl.BlockSpec((q, n), rc),
                   pl.BlockSpec((q, LANES), rc), pl.BlockSpec((q, LANES), rc), pl.BlockSpec((8, LANES), lambda g, c: (g, 0))],
        out_shape=[jax.ShapeDtypeStruct((T, g_n * hp), f32), jax.ShapeDtypeStruct((T, g_n * n), f32),
                   jax.ShapeDtypeStruct((T, g_n * n), f32), jax.ShapeDtypeStruct((T, g_n * LANES), f32),
                   jax.ShapeDtypeStruct((T, g_n * LANES), f32), jax.ShapeDtypeStruct((g_n * 8, LANES), f32)],
        scratch_shapes=[pltpu.VMEM((hp, n), f32)],
        compiler_params=_params(("parallel", "arbitrary")),
    )(act, act, act, dt, adt, dpad, states, dy)


def _cmul(ar, ai, br, bi):
    return ar * br - ai * bi, ar * bi + ai * br


def _s5_tile_powers(lr, li):
    p = [(lr, li)]
    for _ in range(7):
        p.append(_cmul(p[-1][0], p[-1][1], lr, li))
    tile = (jnp.concatenate([q[0] for q in p], axis=0), jnp.concatenate([q[1] for q in p], axis=0))
    return tile, (p[0], p[1], p[3])


def _s5_tile_scan(xr, xi, steps, reverse):
    row = lax.broadcasted_iota(jnp.int32, xr.shape, 0)
    for d, (pr, pi) in zip((1, 2, 4), steps):
        if reverse:
            keep = row < 8 - d
            sr, si = pltpu.roll(xr, 8 - d, 0), pltpu.roll(xi, 8 - d, 0)
        else:
            keep = row >= d
            sr, si = pltpu.roll(xr, d, 0), pltpu.roll(xi, d, 0)
        sr, si = jnp.where(keep, sr, 0.0), jnp.where(keep, si, 0.0)
        ar, ai = _cmul(sr, si, pr, pi)
        xr, xi = xr + ar, xi + ai
    return xr, xi


def _s5_scan_fwd(bu, lam_re, lam_im, nsb, name, tc_t=512):
    T = bu.shape[0]
    w2 = bu.shape[1] // nsb
    w = w2 // 2
    tc = _pick(T, tc_t, 8)

    def body(bu_ref, lr_ref, li_ref, st_ref, carry):
        @pl.when(pl.program_id(1) == 0)
        def _():
            carry[...] = jnp.zeros_like(carry)

        (pr8, pi8), steps = _s5_tile_powers(lr_ref[0:1, :], li_ref[0:1, :])

        def tile(i, c):
            r = pl.ds(pl.multiple_of(i * 8, 8), 8)
            x = bu_ref[r, :]
            xr, xi = _s5_tile_scan(x[:, :w], x[:, w:], steps, False)
            ar, ai = _cmul(pr8, pi8, c[0], c[1])
            xr, xi = xr + ar, xi + ai
            st_ref[r, :] = jnp.concatenate([xr, xi], axis=1)
            return xr[7:8, :], xi[7:8, :]

        c = lax.fori_loop(0, tc // 8, tile, (carry[0:1, :], carry[1:2, :]), unroll=2)
        carry[0:1, :] = c[0]
        carry[1:2, :] = c[1]

    return pl.pallas_call(
        body, name=name, grid=(nsb, T // tc),
        in_specs=[pl.BlockSpec((tc, w2), lambda j, i: (i, j)), pl.BlockSpec((8, w), lambda j, i: (j, 0)),
                  pl.BlockSpec((8, w), lambda j, i: (j, 0))],
        out_specs=pl.BlockSpec((tc, w2), lambda j, i: (i, j)), out_shape=jax.ShapeDtypeStruct(bu.shape, f32),
        scratch_shapes=[pltpu.VMEM((8, w), f32)],
        compiler_params=_params(("parallel", "arbitrary")),
    )(bu, lam_re, lam_im)


def _s5_scan_bwd(gst, states, lam_re, lam_im, nsb, name, tc_t=512):
    T = gst.shape[0]
    w2 = gst.shape[1] // nsb
    w = w2 // 2
    tc = _pick(T, tc_t, 8)
    nt = T // tc
    n_tiles = tc // 8

    def body(g_ref, s_ref, sp_ref, lr_ref, li_ref, a_ref, dlr_ref, dli_ref, carry, acc):
        chunk = pl.program_id(1)

        @pl.when(chunk == 0)
        def _():
            carry[...] = jnp.zeros_like(carry)
            acc[...] = jnp.zeros_like(acc)

        (qr8, qi8), steps = _s5_tile_powers(lr_ref[0:1, :], -li_ref[0:1, :])
        row = lax.broadcasted_iota(jnp.int32, (8, w), 0)
        rev_r, rev_i = jnp.zeros((8, w), f32), jnp.zeros((8, w), f32)
        for r in range(8):
            rev_r = jnp.where(row == r, qr8[7 - r:8 - r, :], rev_r)
            rev_i = jnp.where(row == r, qi8[7 - r:8 - r, :], rev_i)
        row2 = lax.broadcasted_iota(jnp.int32, (8, w2), 0)

        def tile(k, c):
            ar_in, ai_in, dr, di = c
            i = n_tiles - 1 - k
            r = pl.ds(pl.multiple_of(i * 8, 8), 8)
            x = g_ref[r, :]
            xr, xi = _s5_tile_scan(x[:, :w], x[:, w:], steps, True)
            pr, pi = _cmul(rev_r, rev_i, ar_in, ai_in)
            xr, xi = xr + pr, xi + pi
            a_ref[r, :] = jnp.concatenate([xr, xi], axis=1)
            before = jnp.where(i > 0, s_ref[pl.ds(pl.multiple_of(jnp.maximum(i - 1, 0) * 8, 8), 8), :],
                               sp_ref[tc - 8:tc, :] * (chunk < nt - 1).astype(f32))
            prev = jnp.where(row2 == 0, pltpu.roll(before, 1, 0), pltpu.roll(s_ref[r, :], 1, 0))
            spr, spi = prev[:, :w], prev[:, w:]
            return xr[0:1, :], xi[0:1, :], dr + xr * spr + xi * spi, di - xr * spi + xi * spr

        c0 = (carry[0:1, :], carry[1:2, :], acc[0:8, :], acc[8:16, :])
        ar, ai, dr, di = lax.fori_loop(0, n_tiles, tile, c0, unroll=2)
        carry[0:1, :] = ar
        carry[1:2, :] = ai
        acc[0:8, :] = dr
        acc[8:16, :] = di
        dlr_ref[...] = jnp.broadcast_to(jnp.sum(dr, axis=0, keepdims=True), (8, w))
        dli_ref[...] = jnp.broadcast_to(jnp.sum(di, axis=0, keepdims=True), (8, w))

    cur = lambda j, i: (nt - 1 - i, j)
    prv = lambda j, i: (jnp.maximum(nt - 2 - i, 0), j)
    return pl.pallas_call(
        body, name=name, grid=(nsb, nt),
        in_specs=[pl.BlockSpec((tc, w2), cur), pl.BlockSpec((tc, w2), cur), pl.BlockSpec((tc, w2), prv),
                  pl.BlockSpec((8, w), lambda j, i: (j, 0)), pl.BlockSpec((8, w), lambda j, i: (j, 0))],
        out_specs=[pl.BlockSpec((tc, w2), cur), pl.BlockSpec((8, w), lambda j, i: (j, 0)),
                   pl.BlockSpec((8, w), lambda j, i: (j, 0))],
        out_shape=[jax.ShapeDtypeStruct(gst.shape, f32), jax.ShapeDtypeStruct((nsb * 8, w), f32),
                   jax.ShapeDtypeStruct((nsb * 8, w), f32)],
        scratch_shapes=[pltpu.VMEM((8, w), f32), pltpu.VMEM((16, w), f32)],
        compiler_params=_params(("parallel", "arbitrary")),
    )(gst, states, states, lam_re, lam_im)


def _s5_prep_fn(xs, ps):
    lam_re, lam_im, log_step, b_re, b_im, expand = ps
    lr = jnp.minimum(lam_re, S5_MAX_REAL)
    li = lam_im
    step = jnp.exp(log_step)
    er = jnp.exp(lr * step)
    ang = li * step
    lbr, lbi = er * jnp.cos(ang), er * jnp.sin(ang)
    nr, ni = lbr - 1.0, lbi
    den = lr * lr + li * li
    qr, qi = (nr * lr + ni * li) / den, (ni * lr - nr * li) / den
    qre, qie = _dot(qr, expand, "nn", _HI), _dot(qi, expand, "nn", _HI)
    return [lbr, lbi, qre * b_re - qie * b_im, qre * b_im + qie * b_re]


def _s5_prep(pars, name):
    def body(*refs):
        outs = _s5_prep_fn([], [r[...] for r in refs[:6]])
        for ref, v in zip(refs[6:], outs):
            ref[...] = v

    g, nst = pars[0].shape
    nc = pars[3].shape[1]
    return pl.pallas_call(
        body, name=name,
        out_shape=[jax.ShapeDtypeStruct((g, nst), f32)] * 2 + [jax.ShapeDtypeStruct((g, nc), f32)] * 2,
        compiler_params=_params(),
    )(*pars)


def _s5_prep_bwd(pars, cots, name):
    def body(*refs):
        ps = [r[...] for r in refs[:6]]
        ct = [r[...] for r in refs[6:10]]
        _, vjp = jax.vjp(lambda *a: _s5_prep_fn([], list(a)), *ps)
        g = vjp(ct)
        for ref, v in zip(refs[10:], g[:5]):
            ref[...] = v

    return pl.pallas_call(
        body, name=name, out_shape=[jax.ShapeDtypeStruct(p.shape, f32) for p in pars[:5]], compiler_params=_params(),
    )(*pars, *cots)


_ANY = pl.BlockSpec(memory_space=pl.ANY)


def _remote(src, dst, send_sem, recv_sem, device):
    return pltpu.make_async_remote_copy(src_ref=src, dst_ref=dst, send_sem=send_sem, recv_sem=recv_sem, device_id=device,
                                        device_id_type=MESH)


def _staged_copy(src, dst, buf, in_sems, out_sems):
    n = D2D_STREAMS
    piece = src.shape[0] // n
    assert src.shape[0] % n == 0

    def load(i):
        return pltpu.make_async_copy(src.at[pl.ds(i * piece, piece)], buf.at[i % 2], in_sems.at[i % 2])

    def store(i):
        return pltpu.make_async_copy(buf.at[i % 2], dst.at[pl.ds(i * piece, piece)], out_sems.at[i % 2])

    load(0).start()
    for i in range(n):
        if i + 1 < n:
            if i >= 1:
                store(i - 1).wait()
            load(i + 1).start()
        load(i).wait()
        store(i).start()
    store(n - 2).wait()
    store(n - 1).wait()


def _stage_scratch(rows, cols, dtype):
    return [pltpu.VMEM((2, rows // D2D_STREAMS, cols), dtype), pltpu.SemaphoreType.DMA((2,)), pltpu.SemaphoreType.DMA((2,))]


def _chip_all_gather(block, name):
    rows = block.shape[0]
    half = rows // 2
    piece = half // D2D_STREAMS
    assert rows % (2 * D2D_STREAMS * 16) == 0

    def body(src, out, ici_send, ici_recv, d2d_send, d2d_recv, *stage):
        x, y, c = lax.axis_index("x"), lax.axis_index("y"), lax.axis_index("c")
        me = 2 * x + y
        sibling = (x, y, 1 - c)
        chips = [(1 - x, y), (x, 1 - y), (1 - x, 1 - y)]
        mine = pl.ds(pl.multiple_of(c * half, 16), half)
        sends = []
        for j, (px, py) in enumerate(chips):
            cp = _remote(src.at[mine], out.at[me, mine], ici_send.at[j], ici_recv.at[j], (px, py, c))
            cp.start()
            sends.append(cp)
        _staged_copy(src, out.at[me], *stage)
        for j, (px, py) in enumerate(chips):
            slot = 2 * px + py
            _remote(src.at[mine], out.at[slot, mine], ici_send.at[j], ici_recv.at[j], (px, py, c)).wait_recv()
            for s in range(D2D_STREAMS):
                r = pl.ds(pl.multiple_of(c * half + s * piece, 16), piece)
                k = j * D2D_STREAMS + s
                cp = _remote(out.at[slot, r], out.at[slot, r], d2d_send.at[k], d2d_recv.at[k], sibling)
                cp.start()
                sends.append(cp)
        for j, (px, py) in enumerate(chips):
            slot = 2 * px + py
            for s in range(D2D_STREAMS):
                r = pl.ds(pl.multiple_of((1 - c) * half + s * piece, 16), piece)
                k = j * D2D_STREAMS + s
                _remote(out.at[slot, r], out.at[slot, r], d2d_send.at[k], d2d_recv.at[k], sibling).wait_recv()
        for cp in sends:
            cp.wait_send()

    n_d2d = 3 * D2D_STREAMS
    return pl.pallas_call(
        body, name=name, in_specs=[_ANY], out_specs=_ANY,
        out_shape=jax.ShapeDtypeStruct((N_CHIPS,) + block.shape, block.dtype),
        scratch_shapes=[pltpu.SemaphoreType.DMA((3,)), pltpu.SemaphoreType.DMA((3,)), pltpu.SemaphoreType.DMA((n_d2d,)),
                        pltpu.SemaphoreType.DMA((n_d2d,))] + _stage_scratch(rows, block.shape[1], block.dtype),
    )(block)


def _chip_scatter(parts, name):
    def body(src, out, send_sems, recv_sems, *stage):
        x, y, c = lax.axis_index("x"), lax.axis_index("y"), lax.axis_index("c")
        me = 2 * x + y
        chips = [(1 - x, y), (x, 1 - y), (1 - x, 1 - y)]
        sends = []
        for j, (px, py) in enumerate(chips):
            cp = pltpu.make_async_remote_copy(src_ref=src.at[2 * px + py], dst_ref=out.at[me], send_sem=send_sems.at[j],
                                              recv_sem=recv_sems.at[j], device_id=(px, py, c), device_id_type=MESH)
            cp.start()
            sends.append(cp)
        _staged_copy(src.at[me], out.at[me], *stage)
        for j, (px, py) in enumerate(chips):
            pltpu.make_async_remote_copy(src_ref=src.at[me], dst_ref=out.at[2 * px + py], send_sem=send_sems.at[j],
                                         recv_sem=recv_sems.at[j], device_id=(px, py, c), device_id_type=MESH).wait_recv()
        for cp in sends:
            cp.wait_send()

    return pl.pallas_call(
        body, name=name, in_specs=[_ANY], out_specs=_ANY, out_shape=jax.ShapeDtypeStruct(parts.shape, parts.dtype),
        scratch_shapes=[pltpu.SemaphoreType.DMA((3,)), pltpu.SemaphoreType.DMA((3,))]
        + _stage_scratch(parts.shape[1], parts.shape[2], parts.dtype),
    )(parts)


_HBM = pl.BlockSpec(memory_space=pltpu.HBM)
_SEM = pl.BlockSpec(memory_space=pltpu.SEMAPHORE)
_EFFECT = pltpu.SideEffectType.DATAFLOW_SIDE_EFFECTING


def _gather_peers():
    x, y, c = lax.axis_index("x"), lax.axis_index("y"), lax.axis_index("c")
    return x, y, c, 2 * x + y, [(1 - x, y), (x, 1 - y), (1 - x, 1 - y)]


def _chip_gather_start(block, name):
    half = block.shape[0] // 2

    def body(src, land, send_sems, recv_sems, src_out, land_out, token):
        x, y, c, me, chips = _gather_peers()
        mine = pl.ds(pl.multiple_of(c * half, 16), half)
        for j, (px, py) in enumerate(chips):
            _remote(src.at[mine], land.at[me, mine], send_sems.at[j], recv_sems.at[j], (px, py, c)).start()
        token[...] = jnp.zeros_like(token)

    land_shape = (N_CHIPS,) + block.shape
    return pl.pallas_call(
        body, name=name,
        out_shape=(pltpu.SemaphoreType.DMA((3,)), pltpu.SemaphoreType.DMA((3,)), pltpu.HBM(block.shape, block.dtype),
                   pltpu.HBM(land_shape, block.dtype), jax.ShapeDtypeStruct((8, LANES), f32)),
        in_specs=(_HBM, _HBM), out_specs=(_SEM, _SEM, _HBM, _HBM, pl.BlockSpec(memory_space=pltpu.VMEM)),
        input_output_aliases={0: 2, 1: 3}, compiler_params=pltpu.CompilerParams(has_side_effects=_EFFECT),
    )(pltpu.with_memory_space_constraint(block, pltpu.HBM),
      pltpu.with_memory_space_constraint(lax.empty(land_shape, block.dtype), pltpu.HBM))


def _chip_gather_wait(send_sems, recv_sems, block, land, after, name):
    half = block.shape[0] // 2

    def body(src, land_ref, send_ref, recv_ref, after_ref, src_dead, land_out):
        x, y, c, me, chips = _gather_peers()
        mine = pl.ds(pl.multiple_of(c * half, 16), half)
        for j, (px, py) in enumerate(chips):
            cp = _remote(src.at[mine], land_ref.at[2 * px + py, mine], send_ref.at[j], recv_ref.at[j], (px, py, c))
            cp.wait_send()
            cp.wait_recv()

    return pl.pallas_call(
        body, name=name, out_shape=(pltpu.HBM(block.shape, block.dtype), pltpu.HBM(land.shape, land.dtype)),
        in_specs=(_HBM, _HBM, _SEM, _SEM, _ANY), out_specs=(_HBM, _HBM), input_output_aliases={0: 0, 1: 1},
        compiler_params=pltpu.CompilerParams(has_side_effects=_EFFECT),
    )(block, land, send_sems, recv_sems, after)


def _chip_gather_finish(block, land, name):
    rows = block.shape[0]
    half = rows // 2
    piece = half // D2D_STREAMS

    def body(src, land_ref, out, d2d_send, d2d_recv, *stage):
        x, y, c, me, chips = _gather_peers()
        sibling = (x, y, 1 - c)
        sends = []
        for j, (px, py) in enumerate(chips):
            slot = 2 * px + py
            for s in range(D2D_STREAMS):
                r = pl.ds(pl.multiple_of(c * half + s * piece, 16), piece)
                k = j * D2D_STREAMS + s
                cp = _remote(land_ref.at[slot, r], out.at[slot, r], d2d_send.at[k], d2d_recv.at[k], sibling)
                cp.start()
                sends.append(cp)
        _staged_copy(src, out.at[me], *stage)
        for j, (px, py) in enumerate(chips):
            slot = 2 * px + py
            for s in range(D2D_STREAMS):
                r = pl.ds(pl.multiple_of((1 - c) * half + s * piece, 16), piece)
                k = j * D2D_STREAMS + s
                _remote(land_ref.at[slot, r], out.at[slot, r], d2d_send.at[k], d2d_recv.at[k], sibling).wait_recv()
        for cp in sends:
            cp.wait_send()

    n_d2d = 3 * D2D_STREAMS
    return pl.pallas_call(
        body, name=name, in_specs=[_ANY, _ANY], out_specs=_ANY, out_shape=jax.ShapeDtypeStruct(land.shape, land.dtype),
        input_output_aliases={1: 0},
        scratch_shapes=[pltpu.SemaphoreType.DMA((n_d2d,)), pltpu.SemaphoreType.DMA((n_d2d,))]
        + _stage_scratch(rows, block.shape[1], block.dtype),
    )(block, land)


def _chip_scatter_start(parts, name):
    def body(src, land, send_sems, recv_sems, src_out, land_out, token):
        x, y, c = lax.axis_index("x"), lax.axis_index("y"), lax.axis_index("c")
        me = 2 * x + y
        for j, (px, py) in enumerate([(1 - x, y), (x, 1 - y), (1 - x, 1 - y)]):
            _remote(src.at[2 * px + py], land.at[me], send_sems.at[j], recv_sems.at[j], (px, py, c)).start()
        token[...] = jnp.zeros_like(token)

    return pl.pallas_call(
        body, name=name,
        out_shape=(pltpu.SemaphoreType.DMA((3,)), pltpu.SemaphoreType.DMA((3,)), pltpu.HBM(parts.shape, parts.dtype),
                   pltpu.HBM(parts.shape, parts.dtype), jax.ShapeDtypeStruct((8, LANES), f32)),
        in_specs=(_HBM, _HBM), out_specs=(_SEM, _SEM, _HBM, _HBM, pl.BlockSpec(memory_space=pltpu.VMEM)),
        input_output_aliases={0: 2, 1: 3}, compiler_params=pltpu.CompilerParams(has_side_effects=_EFFECT),
    )(pltpu.with_memory_space_constraint(parts, pltpu.HBM),
      pltpu.with_memory_space_constraint(lax.empty(parts.shape, parts.dtype), pltpu.HBM))


def _chip_scatter_wait(send_sems, recv_sems, parts, land, after, name):
    def body(src, land_ref, send_ref, recv_ref, after_ref, src_dead, land_out):
        x, y, c = lax.axis_index("x"), lax.axis_index("y"), lax.axis_index("c")
        me = 2 * x + y
        for j, (px, py) in enumerate([(1 - x, y), (x, 1 - y), (1 - x, 1 - y)]):
            cp = _remote(src.at[2 * px + py], land_ref.at[2 * px + py], send_ref.at[j], recv_ref.at[j], (px, py, c))
            cp.wait_send()
            cp.wait_recv()

    return pl.pallas_call(
        body, name=name, out_shape=(pltpu.HBM(parts.shape, parts.dtype), pltpu.HBM(land.shape, land.dtype)),
        in_specs=(_HBM, _HBM, _SEM, _SEM, _ANY), out_specs=(_HBM, _HBM), input_output_aliases={0: 0, 1: 1},
        compiler_params=pltpu.CompilerParams(has_side_effects=_EFFECT),
    )(parts, land, send_sems, recv_sems, after)


def _sum_slots_own(landed, own, chip, name, block_rows=256):
    s_n, r_n, c_n = landed.shape
    br = _pick(r_n, block_rows, 8)

    def body(chip_ref, land_ref, own_ref, o_ref):
        acc = jnp.zeros((br, c_n), f32)
        for s in range(s_n):
            acc = acc + jnp.where(chip_ref[0] == s, own_ref[s], land_ref[s]).astype(f32)
        o_ref[...] = acc

    spec = pl.BlockSpec((s_n, br, c_n), lambda i, c: (0, i, 0))
    grid_spec = pltpu.PrefetchScalarGridSpec(num_scalar_prefetch=1, grid=(r_n // br,), in_specs=[spec, spec],
                                             out_specs=pl.BlockSpec((br, c_n), lambda i, c: (i, 0)))
    return pl.pallas_call(
        body, name=name, grid_spec=grid_spec, out_shape=jax.ShapeDtypeStruct((r_n, c_n), f32),
        compiler_params=_params(("parallel",)),
    )(chip, landed, own)


def _core_send_other_half(parts, name):
    n_slots, rows, cols = parts.shape
    half = rows // 2
    piece = half // D2D_STREAMS
    assert rows % (2 * D2D_STREAMS * 16) == 0

    def body(src, out, send_sems, recv_sems):
        x, y, c = lax.axis_index("x"), lax.axis_index("y"), lax.axis_index("c")
        sibling = (x, y, 1 - c)
        sends = []
        for k in range(n_slots):
            for s in range(D2D_STREAMS):
                theirs = pl.ds(pl.multiple_of((1 - c) * half + s * piece, 16), piece)
                i = k * D2D_STREAMS + s
                cp = _remote(src.at[k, theirs], out.at[k, pl.ds(s * piece, piece)], send_sems.at[i], recv_sems.at[i], sibling)
                cp.start()
                sends.append(cp)
        for cp in sends:
            cp.wait_recv()
        for cp in sends:
            cp.wait_send()

    n = n_slots * D2D_STREAMS
    return pl.pallas_call(
        body, name=name, in_specs=[_ANY], out_specs=_ANY, out_shape=jax.ShapeDtypeStruct((n_slots, half, cols), parts.dtype),
        scratch_shapes=[pltpu.SemaphoreType.DMA((n,)), pltpu.SemaphoreType.DMA((n,))],
    )(parts)


def _add_my_half(parts, other, core, name, block_rows=256):
    n_slots, rows, cols = parts.shape
    half = rows // 2
    br = _pick(half, block_rows, 16)
    nb = half // br

    def body(c_ref, a_ref, b_ref, o_ref):
        o_ref[...] = (a_ref[...].astype(f32) + b_ref[...].astype(f32)).astype(o_ref.dtype)

    grid_spec = pltpu.PrefetchScalarGridSpec(
        num_scalar_prefetch=1, grid=(n_slots, nb),
        in_specs=[pl.BlockSpec((1, br, cols), lambda k, i, c: (k, c[0] * nb + i, 0)),
                  pl.BlockSpec((1, br, cols), lambda k, i, c: (k, i, 0))],
        out_specs=pl.BlockSpec((1, br, cols), lambda k, i, c: (k, i, 0)))
    return pl.pallas_call(
        body, name=name, grid_spec=grid_spec, out_shape=jax.ShapeDtypeStruct((n_slots, half, cols), parts.dtype),
        compiler_params=_params(("parallel", "parallel")),
    )(core, parts, other)


def _core_join_halves(mine, name):
    half, cols = mine.shape
    piece = half // D2D_STREAMS
    assert half % (D2D_STREAMS * 16) == 0

    def body(src, out, send_sems, recv_sems, *stage):
        x, y, c = lax.axis_index("x"), lax.axis_index("y"), lax.axis_index("c")
        sibling = (x, y, 1 - c)
        sends = []
        for s in range(D2D_STREAMS):
            dst = out.at[pl.ds(pl.multiple_of(c * half + s * piece, 16), piece)]
            cp = _remote(src.at[pl.ds(s * piece, piece)], dst, send_sems.at[s], recv_sems.at[s], sibling)
            cp.start()
            sends.append(cp)
        _staged_copy(src, out.at[pl.ds(pl.multiple_of(c * half, 16), half)], *stage)
        for s in range(D2D_STREAMS):
            dst = out.at[pl.ds(pl.multiple_of((1 - c) * half + s * piece, 16), piece)]
            _remote(src.at[pl.ds(s * piece, piece)], dst, send_sems.at[s], recv_sems.at[s], sibling).wait_recv()
        for cp in sends:
            cp.wait_send()

    return pl.pallas_call(
        body, name=name, in_specs=[_ANY], out_specs=_ANY, out_shape=jax.ShapeDtypeStruct((2 * half, cols), mine.dtype),
        scratch_shapes=[pltpu.SemaphoreType.DMA((D2D_STREAMS,)), pltpu.SemaphoreType.DMA((D2D_STREAMS,))]
        + _stage_scratch(half, cols, mine.dtype),
    )(mine)


def _reduce_start(parts, core, tag):
    chip_part = _add_my_half(parts, _core_send_other_half(parts, f"exchange_core_halves_{tag}"), core, f"sum_core_halves_{tag}")
    return _chip_scatter_start(chip_part, f"scatter_start_{tag}")


def _reduce_finish(started, chip, after, tag):
    send_sems, recv_sems, chip_part, land, _ = started
    own, landed = _chip_scatter_wait(send_sems, recv_sems, chip_part, land, after, f"scatter_wait_{tag}")
    return _core_join_halves(_sum_slots_own(landed, own, chip, f"sum_chip_parts_{tag}"), f"join_core_halves_{tag}")


def _reduce_to_chips(parts, core, tag):
    chip_part = _add_my_half(parts, _core_send_other_half(parts, f"exchange_core_halves_{tag}"), core, f"sum_core_halves_{tag}")
    my_sum = _sum_slots(_chip_scatter(chip_part, f"scatter_{tag}"), f"sum_chip_parts_{tag}")
    return _core_join_halves(my_sum, f"join_core_halves_{tag}")


def _sum_slots(stack, name, block_rows=256):
    s_n, r_n, c_n = stack.shape
    br = _pick(r_n, block_rows, 8)

    def body(in_ref, o_ref):
        acc = in_ref[0].astype(f32)
        for s in range(1, s_n):
            acc = acc + in_ref[s].astype(f32)
        o_ref[...] = acc

    return pl.pallas_call(
        body, name=name, grid=(r_n // br,), in_specs=[pl.BlockSpec((s_n, br, c_n), lambda i: (0, i, 0))],
        out_specs=pl.BlockSpec((br, c_n), lambda i: (i, 0)), out_shape=jax.ShapeDtypeStruct((r_n, c_n), f32),
        compiler_params=_params(("parallel",)),
    )(stack)


def _concat_padded(parts, mult):
    rows = sum(p.shape[0] for p in parts)
    pad = (-rows) % mult
    if pad:
        parts = parts + [jnp.zeros((pad,) + parts[0].shape[1:], parts[0].dtype)]
    return jnp.concatenate(parts, axis=0)


def _pack_weights(w, l, conv_w=None):
    parts = [w[n][l].astype(bf16).reshape(-1, PACK_COLS) for n in BIG]
    if conv_w is not None:
        parts.append(_concat_padded([lax.bitcast_convert_type(conv_w, bf16).reshape(-1, PACK_COLS)], 16))
    return _concat_padded(parts, PACK_ROW_MULT)


def _unpack_weights(full, w, l, conv_w=None):
    start, r0 = {}, 0
    for n in BIG:
        start[n] = r0
        r0 += w[n][l].size // PACK_COLS

    def shards(n):
        rows = w[n][l].size // PACK_COLS
        return [full[k, start[n]:start[n] + rows].reshape(w[n].shape[1:]) for k in range(N_CHIPS)]

    if conv_w is None:
        return shards, None
    rows = conv_w.size * 2 // PACK_COLS
    pieces = lax.bitcast_convert_type(full[:, r0:r0 + rows].reshape((N_CHIPS,) + conv_w.shape + (2,)), f32)
    return shards, jnp.concatenate([pieces[k] for k in range(N_CHIPS)], axis=2)


def _pack_big_grads(layer_grads):
    parts, slot_rows = [], 0
    for k in range(N_CHIPS):
        slot = []
        for n in BIG:
            for g in layer_grads:
                width = g[n].shape[0] // N_CHIPS
                slot.append(g[n][k * width:(k + 1) * width].astype(bf16).reshape(-1, PACK_COLS))
        slot_rows = sum(p.shape[0] for p in slot)
        pad = (-slot_rows) % PACK_ROW_MULT
        if pad:
            slot.append(jnp.zeros((pad, PACK_COLS), bf16))
        slot_rows += pad
        parts += slot
    return jnp.concatenate(parts, axis=0).reshape(N_CHIPS, slot_rows, PACK_COLS)


def _unpack_big_grads(summed, w):
    out, r0 = {}, 0
    for n in BIG:
        rows = w[n][0].size // PACK_COLS
        out[n] = jnp.stack([s[r0:r0 + rows].reshape(w[n].shape[1:]) for s in summed])
        r0 += rows
    return out


_SMALL_TILE = 8 * LANES


def _pack_small(vals, names):
    parts = []
    for n in names:
        pieces = vals[n] if isinstance(vals[n], list) else [vals[n]]
        size = sum(p.size for p in pieces)
        if all(p.size % _SMALL_TILE == 0 for p in pieces):
            parts += [p.reshape(-1, LANES) for p in pieces]
        else:
            flat = [p.reshape(-1) for p in pieces] + [jnp.zeros(((-size) % _SMALL_TILE,), f32)]
            parts.append(jnp.concatenate(flat).reshape(-1, LANES))
    return _concat_padded(parts, PACK_ROW_MULT)


def _unpack_small(packed, like, names):
    out, r0 = {}, 0
    for n in names:
        size = like[n].size
        rows = -(-size // _SMALL_TILE) * 8
        out[n] = packed[r0:r0 + rows].reshape(-1)[:size].reshape(like[n].shape)
        r0 += rows
    return out


def _dims(w, x):
    d = {}
    d["D"] = x.shape[-1]
    d["T"] = x.shape[-2]
    d["DI"] = w["ssd_norm_g"].shape[-1]
    d["NH"] = w["ssd_dt_bias"].shape[-1]
    d["CD"] = w["ssd_conv_b"].shape[-1]
    d["G"] = SSD_N_GROUPS
    d["HPG"] = d["NH"] // d["G"]
    d["P"] = d["DI"] // d["NH"]
    d["N"] = (d["CD"] - d["DI"]) // (2 * d["G"])
    d["S5G"], d["S5N"] = w["s5_lambda_re"].shape[-2:]
    d["S5C"] = w["s5_b_re"].shape[-1]
    d["S5W"] = d["S5G"] * d["S5C"]
    d["NSB"] = d["S5W"] // S5_SUPERBLOCK
    d["GSB"] = d["S5G"] // d["NSB"]
    return d


def _head_pad(v, d):
    lead = v.shape[:-1]
    v = v.reshape(lead + (d["G"], d["HPG"]))
    v = jnp.concatenate([v, jnp.zeros(lead + (d["G"], LANES - d["HPG"]), v.dtype)], axis=-1)
    return v.reshape(lead + (d["G"] * LANES,))


def _head_unpad(v, d):
    lead = v.shape[:-1]
    return v.reshape(lead + (d["G"], LANES))[..., :d["HPG"]].reshape(lead + (d["NH"],))


def _w_in_perm(shards, d):
    o, nh = d["DI"] + d["CD"], d["NH"]
    r = shards[0].shape[0]

    def rows(lo, hi):
        out = []
        for k, s in enumerate(shards):
            a, b = max(lo, k * r), min(hi, (k + 1) * r)
            if a < b:
                out.append(s[a - k * r:b - k * r])
        return out

    dt = _head_pad(jnp.concatenate(rows(o, o + nh), axis=0).T, d).T
    return jnp.concatenate(rows(0, o) + rows(o + nh, len(shards) * r) + [dt], axis=0)


def _w_in_unperm(g, d):
    o = d["DI"] + d["CD"]
    rest = d["S5W"] + 2 * d["D"]
    return jnp.concatenate([g[:o], _head_unpad(g[o + rest:].T, d).T, g[o:o + rest]], axis=0)


def _s5_block_diag(v, d):
    gsb = d["GSB"]
    g, a, b = v.shape
    row_group = (lax.broadcasted_iota(jnp.int32, (g * a, gsb * b), 0) // a) % gsb
    col_group = lax.broadcasted_iota(jnp.int32, (g * a, gsb * b), 1) // b
    return jnp.where(row_group == col_group, jnp.tile(v.reshape(g * a, b), (1, gsb)), 0)


def _s5_diag_blocks(m, d, a, b):
    gsb = d["GSB"]
    rows = m.shape[0]
    m = m.reshape(rows, gsb, b)
    row_group = (lax.broadcasted_iota(jnp.int32, (rows, gsb, 1), 0) // a) % gsb
    col_group = lax.broadcasted_iota(jnp.int32, (rows, gsb, 1), 1)
    return jnp.sum(jnp.where(row_group == col_group, m, 0), axis=1).reshape(rows // a, a, b)


def _s5_lam_rows(v, d):
    v = v.reshape(d["NSB"], 1, d["GSB"] * d["S5N"])
    return jnp.broadcast_to(v, (d["NSB"], 8, v.shape[-1])).reshape(d["NSB"] * 8, -1)


def _ffn_fwd(h, pre_g, post_g, wgu, wd, tag):
    D = h.shape[1]
    H2 = wgu.shape[0]
    xn = _row_kernel(f"{tag}_norm", _fwd_of(_f_norm), [(h, D, 0)], [pre_g], [(D, bf16)])[0]
    ab = _mm(xn, wgu, "nt", f32, f"{tag}_mm_up")
    hid = _row_kernel(f"{tag}_swiglu", _swiglu_fwd, [(ab, H2, 0)], [], [(H2 // 2, bf16)])[0]
    f = _mm(hid, wd, "nn", f32, f"{tag}_mm_down")
    out = _row_kernel(f"{tag}_resnorm", _fwd_of(_f_resnorm(0.5)), [(h, D, 0), (f, D, 0)], [post_g], [(D, f32)])[0]
    return out, dict(h=h, xn=xn, ab=ab, hid=hid, f=f)


def _ffn_bwd(dh_out, s, pre_g, post_g, wgu, wd, tag):
    D = dh_out.shape[1]
    H2 = wgu.shape[0]
    df, dpost = _row_kernel(f"{tag}_resnorm_bwd", _vjp_of(_f_post(0.5), 1, 1, [0]), [(s["f"], D, 0), (dh_out, D, 0)],
                            [post_g], [(D, bf16)], [post_g.shape])
    dwd = _mm(s["hid"], df, "tn", bf16, f"{tag}_mm_dwd")
    dhid = _mm(df, wd, "nt", bf16, f"{tag}_mm_dhid")
    dab = _row_kernel(f"{tag}_swiglu_bwd", _swiglu_bwd, [(s["ab"], H2, 0), (dhid, H2 // 2, 0)], [], [(H2, bf16)])[0]
    dwgu = _mm(dab, s["xn"], "tn", bf16, f"{tag}_mm_dwgu")
    dxn = _mm(dab, wgu, "nn", f32, f"{tag}_mm_dxn")
    dh, dpre = _row_kernel(f"{tag}_norm_bwd", _vjp_of(_f_norm, 1, 1, [0], 1), [(s["h"], D, 0), (dxn, D, 0), (dh_out, D, 0)],
                           [pre_g], [(D, f32)], [pre_g.shape])
    return dh, dict(pre_g=dpre, post_g=dpost, wgu=dwgu, wd=dwd)


def _mixer_fwd(h, p, d):
    D, DI, CD, G, N = d["D"], d["DI"], d["CD"], d["G"], d["N"]
    gl = G * LANES
    c_u5, c_ga, c_gb, c_dt = DI + CD, DI + CD + d["S5W"], DI + CD + d["S5W"] + D, DI + CD + d["S5W"] + 2 * D
    u = _row_kernel("mix_norm", _fwd_of(_f_norm), [(h, D, 0)], [p["mix_pre_g"]], [(D, bf16)])[0]
    proj = _mm(u, p["w_in"], "nt", f32, "mix_mm_in", bn_t=512)
    act = _conv_fwd(proj, DI, p["conv_w"], p["conv_b"], "ssd_conv")
    dt, adt = _row_kernel("ssd_dt", _fwd_of(_f_dt), [(proj, gl, c_dt // gl)], [p["dt_bias"], p["a_log"]], [(gl, f32)] * 2)
    y_ssd, states = _ssd_fwd(act, dt, adt, p["d_skip"], d["HPG"], d["P"], N, "ssd_scan")
    nrm = _row_kernel("ssd_post", _fwd_of(_f_ssdpost(G)), [(y_ssd, DI, 0), (proj, DI, 0)], [p["norm_g"]], [(DI, bf16)])[0]
    y_a = _mm(nrm, p["w_a"], "nn", f32, "mix_mm_a")
    u5 =(proj, d["S5W"], c_u5 // d["S5W"])
    bu = _s5_in(proj, c_u5, p["bsb"], d)
    s5st = _s5_scan_fwd(bu, p["lam_re_rows"], p["lam_im_rows"], d["NSB"], "s5_scan")
    y5 = _bdmm(s5st, p["csb"], "nn", d["NSB"], f32, "s5_mm_c")
    gel = _row_kernel("s5_post", _fwd_of(_f_s5post), [(y5, d["S5W"], 0), u5], [p["s5_d"]], [(d["S5W"], bf16)])[0]
    vg = _mm(gel, p["w_glu"], "nt", f32, "mix_mm_glu")
    glu = _row_kernel("s5_glu", _glu_fwd, [(vg, vg.shape[1], 0)], [], [(vg.shape[1] // 2, bf16)])[0]
    y_b = _mm(glu, p["w_b"], "nn", f32, "mix_mm_b")
    merged = _row_kernel("mix_merge", _fwd_of(_f_merge), [(proj, D, c_ga // D), (y_a, D, 0), (proj, D, c_gb // D), (y_b, D, 0)],
                         [], [(D, bf16)])[0]
    m = _mm(merged, p["w_out"], "nn", f32, "mix_mm_out")
    out = _row_kernel("mix_resnorm", _fwd_of(_f_resnorm(1.0)), [(h, D, 0), (m, D, 0)], [p["mix_post_g"]], [(D, f32)])[0]
    return out, dict(h=h, u=u, proj=proj, act=act, dt=dt, adt=adt, states=states, y_ssd=y_ssd, nrm=nrm, y_a=y_a, s5st=s5st,
                     y5=y5, gel=gel, vg=vg, glu=glu, y_b=y_b, merged=merged, m=m)


def _s5_in(proj, c_u5, bsb, d):
    T = proj.shape[0]
    nsb = d["NSB"]
    ka, nw = S5_SUPERBLOCK, bsb.shape[1]
    off = c_u5 // ka
    assert c_u5 % ka == 0
    bt = _pick(T, 512)

    def body(a_ref, w_ref, o_ref):
        o_ref[...] = _dot(a_ref[...].astype(bf16), w_ref[...].astype(bf16))

    return pl.pallas_call(
        body, name="s5_mm_bu", grid=(nsb, T // bt),
        in_specs=[pl.BlockSpec((bt, ka), lambda j, i: (i, off + j)), pl.BlockSpec((ka, nw), lambda j, i: (j, 0))],
        out_specs=pl.BlockSpec((bt, nw), lambda j, i: (i, j)), out_shape=jax.ShapeDtypeStruct((T, nsb * nw), f32),
        compiler_params=_params(("parallel", "parallel")),
    )(proj, bsb)


def _s5_dbsb(proj, c_u5, a, d):
    T = proj.shape[0]
    nsb = d["NSB"]
    ka, nw = S5_SUPERBLOCK, a.shape[1] // nsb
    off = c_u5 // ka
    bt = _pick(T, 512)

    def body(u_ref, a_ref, o_ref):
        pr = _dot(u_ref[...].astype(bf16), a_ref[...].astype(bf16), "tn")
        k = pl.program_id(1)

        @pl.when(k == 0)
        def _():
            o_ref[...] = pr

        @pl.when(k > 0)
        def _():
            o_ref[...] += pr

    return pl.pallas_call(
        body, name="s5_mm_dbsb", grid=(nsb, T // bt),
        in_specs=[pl.BlockSpec((bt, ka), lambda j, k: (k, off + j)), pl.BlockSpec((bt, nw), lambda j, k: (k, j))],
        out_specs=pl.BlockSpec((ka, nw), lambda j, k: (j, 0)), out_shape=jax.ShapeDtypeStruct((nsb * ka, nw), f32),
        compiler_params=_params(("parallel", "arbitrary")),
    )(proj, a)


def _mixer_bwd(dh_out, s, p, d):
    D, DI, CD, G, N, S5W = d["D"], d["DI"], d["CD"], d["G"], d["N"], d["S5W"]
    gl = G * LANES
    gn = G * N
    c_u5, c_ga, c_gb, c_dt = DI + CD, DI + CD + S5W, DI + CD + S5W + D, DI + CD + S5W + 2 * D
    proj = s["proj"]
    g = {}
    dm, g["mix_post_g"] = _row_kernel("mix_resnorm_bwd", _vjp_of(_f_post(1.0), 1, 1, [0]), [(s["m"], D, 0), (dh_out, D, 0)],
                                      [p["mix_post_g"]], [(D, bf16)], [p["mix_post_g"].shape])
    g["w_out"] = _mm(s["merged"], dm, "tn", bf16, "mix_mm_dwout")
    dmerged = _mm(dm, p["w_out"], "nt", f32, "mix_mm_dmerged")
    dga, dya, dgb, dyb = _row_kernel(
        "mix_merge_bwd", _vjp_of(_f_merge, 4, 1, [0, 1, 2, 3]),
        [(proj, D, c_ga // D), (s["y_a"], D, 0), (proj, D, c_gb // D), (s["y_b"], D, 0), (dmerged, D, 0)], [],
        [(D, bf16), (D, bf16), (D, bf16), (D, bf16)])
    g["w_a"] = _mm(s["nrm"], dya, "tn", bf16, "mix_mm_dwa")
    dnrm = _mm(dya, p["w_a"], "nt", f32, "mix_mm_dnrm")
    dy_ssd, dz, g["norm_g"] = _row_kernel(
        "ssd_post_bwd", _vjp_of(_f_ssdpost(G), 2, 1, [0, 1]), [(s["y_ssd"], DI, 0), (proj, DI, 0), (dnrm, DI, 0)],
        [p["norm_g"]], [(DI, f32), (DI, bf16)], [p["norm_g"].shape])
    dxs, d_b, d_c, ddt, dadt, dd = _ssd_bwd(s["act"], s["dt"], s["adt"], p["d_skip"], s["states"], dy_ssd,
                                            d["HPG"], d["P"], N, "ssd_scan_bwd")
    g["d_skip"] = dd.reshape(G, 8, LANES)[:, 0, :].reshape(1, gl)
    ddt_raw, g["dt_bias"], g["a_log"] = _row_kernel(
        "ssd_dt_bwd", _vjp_of(_f_dt, 1, 2, [0]), [(proj, gl, c_dt // gl), (ddt, gl, 0), (dadt, gl, 0)],
        [p["dt_bias"], p["a_log"]], [(gl, bf16)], [p["dt_bias"].shape, p["a_log"].shape])
    cw, cb = p["conv_w"], p["conv_b"]
    dxc_x, dw_x, db_x = _conv_bwd(proj, DI, cw[:, :DI], cb[:, :DI], dxs, "ssd_conv_bwd_x")
    dxc_b, dw_b, db_b = _conv_bwd(proj, 2 * DI, cw[:, DI:DI + gn], cb[:, DI:DI + gn], d_b, "ssd_conv_bwd_b")
    dxc_c, dw_c, db_c = _conv_bwd(proj, 2 * DI + gn, cw[:, DI + gn:], cb[:, DI + gn:], d_c, "ssd_conv_bwd_c")
    g["conv_w"] = jnp.concatenate([dw_x, dw_b, dw_c], axis=1)
    g["conv_b"] = jnp.concatenate([db_x, db_b, db_c], axis=1)
    g["w_b"] = _mm(s["glu"], dyb, "tn", bf16, "mix_mm_dwb")
    dglu = _mm(dyb, p["w_b"], "nt", f32, "mix_mm_dglu")
    dvg = _row_kernel("s5_glu_bwd", _glu_bwd, [(s["vg"], s["vg"].shape[1], 0), (dglu, S5W, 0)], [], [(s["vg"].shape[1], bf16)])[0]
    g["w_glu"] = _mm(dvg, s["gel"], "tn", bf16, "mix_mm_dwglu")
    dgel = _mm(dvg, p["w_glu"], "nn", f32, "mix_mm_dgel")
    dy5, du5a, g["s5_d"] = _row_kernel(
        "s5_post_bwd", _vjp_of(_f_s5post, 2, 1, [0, 1]), [(s["y5"], S5W, 0), (proj, S5W, c_u5 // S5W), (dgel, S5W, 0)],
        [p["s5_d"]], [(S5W, bf16), (S5W, f32)], [p["s5_d"].shape])
    g["csb"] = _bdmm(s["s5st"], dy5, "tn", d["NSB"], f32, "s5_mm_dcsb")
    gst = _bdmm(dy5, p["csb"], "nt", d["NSB"], f32, "s5_mm_gst")
    a, g["lam_re_rows"], g["lam_im_rows"] = _s5_scan_bwd(gst, s["s5st"], p["lam_re_rows"], p["lam_im_rows"], d["NSB"], "s5_scan_bwd")
    g["bsb"] = _s5_dbsb(proj, c_u5, a, d)
    du5b = _bdmm(a, p["bsb"], "nt", d["NSB"], f32, "s5_mm_du5")
    du5 = _row_kernel("s5_du5", _add_fn, [(du5a, S5W, 0), (du5b, S5W, 0)], [], [(S5W, bf16)])[0]
    dproj = jnp.concatenate([dz, dxc_x, dxc_b, dxc_c, du5, dga, dgb, ddt_raw], axis=1)
    g["w_in"] = _mm(dproj, s["u"], "tn", bf16, "mix_mm_dwin")
    du = _mm(dproj, p["w_in"], "nn", f32, "mix_mm_du", bk_t=2176)
    dh, g["mix_pre_g"] = _row_kernel("mix_norm_bwd", _vjp_of(_f_norm, 1, 1, [0], 1), [(s["h"], D, 0), (du, D, 0), (dh_out, D, 0)],
                                     [p["mix_pre_g"]], [(D, f32)], [p["mix_pre_g"].shape])
    return dh, g


def _layer_params(l, w, wf, conv_w_full, d):
    r2 = lambda v: v[l].reshape(1, -1)
    p = {}
    for n in ["ffn1_pre_g", "ffn1_post_g", "mix_pre_g", "mix_post_g", "ffn2_pre_g", "ffn2_post_g", "s5_d"]:
        p[n] = r2(w[n])
    whole = lambda *names: jnp.concatenate([s for n in names for s in wf(n)], axis=0)
    p["wgu1"] = whole("ffn1_w_gate", "ffn1_w_up")
    p["wd1"] = whole("ffn1_w_down")
    p["wgu2"] = whole("ffn2_w_gate", "ffn2_w_up")
    p["wd2"] = whole("ffn2_w_down")
    p["w_in"] = _w_in_perm(wf("w_in"), d)
    p["w_a"], p["w_glu"], p["w_b"], p["w_out"] = whole("w_branch_a"), whole("s5_w_glu"), whole("w_branch_b"), whole("w_out")
    p["conv_w"] = conv_w_full[l]
    p["conv_b"] = r2(w["ssd_conv_b"])
    p["dt_bias"] = _head_pad(r2(w["ssd_dt_bias"]), d)
    p["a_log"] = _head_pad(r2(w["ssd_a_log"]), d)
    p["d_skip"] = _head_pad(r2(w["ssd_d"]), d)
    p["norm_g"] = r2(w["ssd_norm_g"])
    g5, n5, c5 = d["S5G"], d["S5N"], d["S5C"]
    expand = jnp.repeat(jnp.eye(n5, dtype=f32), c5, axis=1)
    prep_in = [w["s5_lambda_re"][l], w["s5_lambda_im"][l], w["s5_log_step"][l].reshape(g5, 1),
               w["s5_b_re"][l].reshape(g5, n5 * c5), w["s5_b_im"][l].reshape(g5, n5 * c5), expand]
    lbr, lbi, bbr, bbi = _s5_prep(prep_in, "s5_prep")
    p["s5_prep_in"] = prep_in
    p["lam_re_rows"], p["lam_im_rows"] = _s5_lam_rows(lbr, d), _s5_lam_rows(lbi, d)
    to_cn = lambda v: v.reshape(g5, n5, c5).transpose(0, 2, 1)
    p["bsb"] = jnp.concatenate([_s5_block_diag(to_cn(bbr), d), _s5_block_diag(to_cn(bbi), d)], axis=1).astype(bf16)
    c_re, c_im = w["s5_c_re"][l].transpose(0, 2, 1), w["s5_c_im"][l].transpose(0, 2, 1)
    nsb = d["NSB"]
    csb = jnp.stack([_s5_block_diag(c_re, d).reshape(nsb, -1, S5_SUPERBLOCK),
                     _s5_block_diag(-c_im, d).reshape(nsb, -1, S5_SUPERBLOCK)], axis=1)
    p["csb"] = csb.reshape(-1, S5_SUPERBLOCK).astype(bf16)
    return p


def _s5_param_grads(g, p, d, l):
    g5, n5, c5, nsb, gsb = d["S5G"], d["S5N"], d["S5C"], d["NSB"], d["GSB"]
    wst = gsb * n5
    dbsb = g["bsb"]
    from_cn = lambda v: v.transpose(0, 2, 1).reshape(g5, n5 * c5)
    dbbr = from_cn(_s5_diag_blocks(dbsb[:, :wst], d, c5, n5))
    dbbi = from_cn(_s5_diag_blocks(dbsb[:, wst:], d, c5, n5))
    rows = lambda v: v.reshape(nsb, 8, wst)[:, 0, :].reshape(g5, n5)
    cots = [rows(g["lam_re_rows"]), rows(g["lam_im_rows"]), dbbr, dbbi]
    dlr, dli, dls, dbr, dbi = _s5_prep_bwd(p["s5_prep_in"], cots, "s5_prep_bwd")
    dcsb = g["csb"].reshape(nsb, 2, wst, S5_SUPERBLOCK)
    dcr = _s5_diag_blocks(dcsb[:, 0].reshape(-1, S5_SUPERBLOCK), d, n5, c5).transpose(0, 2, 1)
    dci = -_s5_diag_blocks(dcsb[:, 1].reshape(-1, S5_SUPERBLOCK), d, n5, c5).transpose(0, 2, 1)
    return dict(s5_lambda_re=dlr, s5_lambda_im=dli, s5_log_step=dls.reshape(g5), s5_b_re=dbr.reshape(g5, n5, c5),
                s5_b_im=dbi.reshape(g5, n5, c5), s5_c_re=dcr, s5_c_im=dci)


def kernel(x, ffn1_pre_g, ffn1_post_g, ffn1_w_gate, ffn1_w_up, ffn1_w_down, mix_pre_g, mix_post_g, w_in, ssd_conv_w, ssd_conv_b, ssd_dt_bias, ssd_a_log, ssd_d, ssd_norm_g, w_branch_a, s5_lambda_re, s5_lambda_im, s5_b_re, s5_b_im, s5_c_re, s5_c_im, s5_log_step, s5_d, s5_w_glu, w_branch_b, w_out, ffn2_pre_g, ffn2_post_g, ffn2_w_gate, ffn2_w_up, ffn2_w_down, loss_target, m_ffn1_pre_g, m_ffn1_post_g, m_ffn1_w_gate, m_ffn1_w_up, m_ffn1_w_down, m_mix_pre_g, m_mix_post_g, m_w_in, m_ssd_conv_w, m_ssd_conv_b, m_ssd_dt_bias, m_ssd_a_log, m_ssd_d, m_ssd_norm_g, m_w_branch_a, m_s5_lambda_re, m_s5_lambda_im, m_s5_b_re, m_s5_b_im, m_s5_c_re, m_s5_c_im, m_s5_log_step, m_s5_d, m_s5_w_glu, m_w_branch_b, m_w_out, m_ffn2_pre_g, m_ffn2_post_g, m_ffn2_w_gate, m_ffn2_w_up, m_ffn2_w_down, v_ffn1_pre_g, v_ffn1_post_g, v_ffn1_w_gate, v_ffn1_w_up, v_ffn1_w_down, v_mix_pre_g, v_mix_post_g, v_w_in, v_ssd_conv_w, v_ssd_conv_b, v_ssd_dt_bias, v_ssd_a_log, v_ssd_d, v_ssd_norm_g, v_w_branch_a, v_s5_lambda_re, v_s5_lambda_im, v_s5_b_re, v_s5_b_im, v_s5_c_re, v_s5_c_im, v_s5_log_step, v_s5_d, v_s5_w_glu, v_w_branch_b, v_w_out, v_ffn2_pre_g, v_ffn2_post_g, v_ffn2_w_gate, v_ffn2_w_up, v_ffn2_w_down):
    given = dict(locals())
    for n in COL_SHARDED:
        for prefix in ("", "m_", "v_"):
            given[prefix + n] = given[prefix + n].transpose(0, 2, 1)
    w = {n: given[n] for n in WEIGHTS}
    mom = {n: given["m_" + n] for n in WEIGHTS}
    var = {n: given["v_" + n] for n in WEIGHTS}
    d = _dims(w, x)
    n_layers = w["ffn1_pre_g"].shape[0]
    T, D = d["T"], d["D"]

    conv_w = w["ssd_conv_w"]
    wf, conv_w_full = _unpack_weights(_chip_all_gather(_pack_weights(w, 0, conv_w), "gather_weights_l0"), w, 0, conv_w)
    coming = [None] + [_chip_gather_start(_pack_weights(w, l), f"gather_start_l{l}") for l in range(1, n_layers)]

    h = x.reshape(T, D)
    for started in coming[1:]:
        h = h + started[-1][0, 0]
    saved, layers = [], []
    for l in range(n_layers):
        if l > 0:
            send_sems, recv_sems, pack, land, _ = coming[l]
            pack, land = _chip_gather_wait(send_sems, recv_sems, pack, land, h, f"gather_wait_l{l}")
            wf, _ = _unpack_weights(_chip_gather_finish(pack, land, f"gather_finish_l{l}"), w, l)
        p = _layer_params(l, w, wf, conv_w_full, d)
        layers.append(p)
        h, s1 = _ffn_fwd(h, p["ffn1_pre_g"], p["ffn1_post_g"], p["wgu1"], p["wd1"], "ffn1")
        h, sm = _mixer_fwd(h, p, d)
        h, s2 = _ffn_fwd(h, p["ffn2_pre_g"], p["ffn2_post_g"], p["wgu2"], p["wd2"], "ffn2")
        saved.append((s1, sm, s2))
    dh, loss_part = _row_kernel("loss", _loss_fn, [(h, D, 0), (loss_target.reshape(T, D), D, 0)], [], [(D, f32)], [(8, LANES)])
    loss = lax.psum(loss_part[0, 0], ("x", "y", "c"))

    my_core = lax.axis_index("c").astype(jnp.int32).reshape(1)
    my_chip = (2 * lax.axis_index("x") + lax.axis_index("y")).astype(jnp.int32).reshape(1)
    lg, in_flight = [None] * n_layers, [None] * n_layers
    for l in reversed(range(n_layers)):
        p = layers[l]
        s1, sm, s2 = saved[l]
        dh, g2 = _ffn_bwd(dh, s2, p["ffn2_pre_g"], p["ffn2_post_g"], p["wgu2"], p["wd2"], "ffn2")
        dh, gm = _mixer_bwd(dh, sm, p, d)
        dh, g1 = _ffn_bwd(dh, s1, p["ffn1_pre_g"], p["ffn1_post_g"], p["wgu1"], p["wd1"], "ffn1")
        H = p["wd1"].shape[0]
        gl = dict(ffn1_pre_g=g1["pre_g"], ffn1_post_g=g1["post_g"], ffn1_w_gate=g1["wgu"][:H], ffn1_w_up=g1["wgu"][H:],
                  ffn1_w_down=g1["wd"], ffn2_pre_g=g2["pre_g"], ffn2_post_g=g2["post_g"], ffn2_w_gate=g2["wgu"][:H],
                  ffn2_w_up=g2["wgu"][H:], ffn2_w_down=g2["wd"], mix_pre_g=gm["mix_pre_g"], mix_post_g=gm["mix_post_g"],
                  w_in=_w_in_unperm(gm["w_in"], d), ssd_conv_w=gm["conv_w"], ssd_conv_b=gm["conv_b"],
                  ssd_dt_bias=_head_unpad(gm["dt_bias"], d), ssd_a_log=_head_unpad(gm["a_log"], d),
                  ssd_d=_head_unpad(gm["d_skip"], d), ssd_norm_g=gm["norm_g"], w_branch_a=gm["w_a"], s5_d=gm["s5_d"],
                  s5_w_glu=gm["w_glu"], w_branch_b=gm["w_b"], w_out=gm["w_out"])
        gl.update(_s5_param_grads(gm, p, d, l))
        lg[l] = gl
        if l > 0:
            in_flight[l] = _reduce_start(_pack_big_grads([gl]), my_core, f"grads_l{l}")
            dh = dh + in_flight[l][-1][0, 0]
    grad_x = dh.reshape(x.shape)
    summed = [None] * n_layers
    for l in range(1, n_layers):
        summed[l] = _reduce_finish(in_flight[l], my_chip, grad_x, f"grads_l{l}")
    summed[0] = _reduce_to_chips(_pack_big_grads([lg[0]]), my_core, "grads_l0")
    grads = _unpack_big_grads(summed, w)
    small_names = SMALL + ["ssd_conv_w"]
    small_parts = {n: [g[n] for g in lg] for n in small_names}
    small_like = {n: jax.ShapeDtypeStruct((n_layers,) + lg[0][n].shape, f32) for n in small_names}
    small_like.update({n: w[n] for n in SMALL})
    small_pack = _pack_small(small_parts, small_names)
    small_sum = _reduce_to_chips(jnp.broadcast_to(small_pack, (N_CHIPS,) + small_pack.shape), my_core, "small")
    small = _unpack_small(small_sum, small_like, small_names)
    k_me = 2 * lax.axis_index("x") + lax.axis_index("y")
    cw = w["ssd_conv_w"].shape[-1]
    small["ssd_conv_w"] = lax.dynamic_slice_in_dim(small["ssd_conv_w"], k_me * cw, cw, axis=2)
    grads.update(small)

    delta, new_m, new_v = {}, {}, {}
    for n in BIG + ["ssd_conv_w"]:
        delta[n], new_m[n], new_v[n] = _adamw(w[n], grads[n], mom[n], var[n], "adamw_" + n)
    pw, pm, pv = [_pack_small(t, SMALL) for t in (w, mom, var)]
    assert pw.shape[0] <= small_sum.shape[0]
    sd, sm_, sv = _adamw(pw, small_sum[:pw.shape[0]], pm, pv, "adamw_small")
    delta.update(_unpack_small(sd, w, SMALL))
    new_m.update(_unpack_small(sm_, w, SMALL))
    new_v.update(_unpack_small(sv, w, SMALL))
    for n in COL_SHARDED:
        for out in (grads, delta, new_m, new_v):
            out[n] = out[n].transpose(0, 2, 1)
    return (loss, grad_x, *[grads[n] for n in WEIGHTS], *[delta[n] for n in WEIGHTS],
            *[new_m[n] for n in WEIGHTS], *[new_v[n] for n in WEIGHTS])
```

```python
import functools

import numpy as np
import jax
import jax.numpy as jnp
from jax import lax
from jax.experimental import pallas as pl
from jax.experimental.pallas import tpu as pltpu

f32, bf16 = jnp.float32, jnp.bfloat16

SSD_N_GROUPS = 4
SSD_CHUNK = 128
RMS_EPS = 1e-6
S5_MAX_REAL = -1e-4
S5_SUPERBLOCK = 256
ADAM_LR, ADAM_B1, ADAM_B2, ADAM_EPS, ADAM_WD, ADAM_STEP = 0.001, 0.9, 0.999, 1e-08, 0.01, 10

LANES = 128
PACK_COLS = 1024
D2D_STREAMS = 16
PACK_ROW_MULT = 2 * D2D_STREAMS * 16
VMEM_LIMIT_BYTES = 48 * 1024 * 1024
N_CHIPS, N_CORES, N_DEV = 4, 2, 8
MESH = pl.DeviceIdType.MESH

BIG = ["ffn1_w_gate", "ffn1_w_up", "ffn1_w_down", "w_in", "w_branch_a", "s5_w_glu", "w_branch_b", "w_out",
       "ffn2_w_gate", "ffn2_w_up", "ffn2_w_down"]
COL_SHARDED = ["ffn1_w_gate", "ffn1_w_up", "w_in", "s5_w_glu", "ffn2_w_gate", "ffn2_w_up"]
SMALL = ["ffn1_pre_g", "ffn1_post_g", "mix_pre_g", "mix_post_g", "ssd_conv_b", "ssd_norm_g", "s5_lambda_re", "s5_lambda_im",
         "s5_b_re", "s5_b_im", "s5_c_re", "s5_c_im", "s5_d", "ffn2_pre_g", "ffn2_post_g", "s5_log_step", "ssd_dt_bias",
         "ssd_a_log", "ssd_d"]
WEIGHTS = ["ffn1_pre_g", "ffn1_post_g", "ffn1_w_gate", "ffn1_w_up", "ffn1_w_down", "mix_pre_g", "mix_post_g", "w_in",
           "ssd_conv_w", "ssd_conv_b", "ssd_dt_bias", "ssd_a_log", "ssd_d", "ssd_norm_g", "w_branch_a", "s5_lambda_re",
           "s5_lambda_im", "s5_b_re", "s5_b_im", "s5_c_re", "s5_c_im", "s5_log_step", "s5_d", "s5_w_glu", "w_branch_b",
           "w_out", "ffn2_pre_g", "ffn2_post_g", "ffn2_w_gate", "ffn2_w_up", "ffn2_w_down"]


def _params(sem=None):
    return pltpu.CompilerParams(dimension_semantics=sem, vmem_limit_bytes=VMEM_LIMIT_BYTES)


def _pick(n, target, mult=LANES):
    best = None
    for d in range(mult, min(n, target) + 1, mult):
        if n % d == 0:
            best = d
    return best if best is not None else n


_DIMS = {"nn": (((1,), (0,)), ((), ())), "nt": (((1,), (1,)), ((), ())), "tn": (((0,), (0,)), ((), ()))}


def _mm(a, b, mode, out_dtype, name, bm_t=1024, bn_t=1024, bk_t=2816):
    if mode == "nn":
        (M, K), (K2, N) = a.shape, b.shape
    elif mode == "nt":
        (M, K), (N, K2) = a.shape, b.shape
    else:
        (K, M), (K2, N) = a.shape, b.shape
    assert K == K2, (name, a.shape, b.shape)
    bm, bn, bk = _pick(M, bm_t), _pick(N, bn_t), _pick(K, bk_t)
    nk = K // bk
    dn = _DIMS[mode]

    def body(a_ref, b_ref, o_ref, *scratch):
        p = lax.dot_general(a_ref[...].astype(bf16), b_ref[...].astype(bf16), dn, preferred_element_type=f32)
        if nk == 1:
            o_ref[...] = p.astype(o_ref.dtype)
        else:
            acc = scratch[0]
            k = pl.program_id(2)

            @pl.when(k == 0)
            def _():
                acc[...] = p

            @pl.when(k > 0)
            def _():
                acc[...] += p

            @pl.when(k == nk - 1)
            def _():
                o_ref[...] = acc[...].astype(o_ref.dtype)

    if mode == "tn":
        a_spec = pl.BlockSpec((bk, bm), lambda i, j, k: (k, i))
    else:
        a_spec = pl.BlockSpec((bm, bk), lambda i, j, k: (i, k))
    if mode == "nt":
        b_spec = pl.BlockSpec((bn, bk), lambda i, j, k: (j, k))
    else:
        b_spec = pl.BlockSpec((bk, bn), lambda i, j, k: (k, j))
    return pl.pallas_call(
        body, name=name, grid=(M // bm, N // bn, nk), in_specs=[a_spec, b_spec],
        out_specs=pl.BlockSpec((bm, bn), lambda i, j, k: (i, j)), out_shape=jax.ShapeDtypeStruct((M, N), out_dtype),
        scratch_shapes=[pltpu.VMEM((bm, bn), f32)] if nk > 1 else [],
        compiler_params=_params(("parallel", "parallel", "arbitrary")),
    )(a, b)


def _bdmm(a, w, mode, nb, out_dtype, name, bt_t=512):
    if mode == "tn":
        T = a.shape[0]
        ka, nw = a.shape[1] // nb, w.shape[1] // nb
        bt = _pick(T, bt_t)
        nt = T // bt

        def body_tn(a_ref, b_ref, o_ref):
            p = lax.dot_general(a_ref[...].astype(bf16), b_ref[...].astype(bf16), _DIMS["tn"], preferred_element_type=f32)
            k = pl.program_id(1)

            @pl.when(k == 0)
            def _():
                o_ref[...] = p

            @pl.when(k > 0)
            def _():
                o_ref[...] += p

        return pl.pallas_call(
            body_tn, name=name, grid=(nb, nt),
            in_specs=[pl.BlockSpec((bt, ka), lambda j, k: (k, j)), pl.BlockSpec((bt, nw), lambda j, k: (k, j))],
            out_specs=pl.BlockSpec((ka, nw), lambda j, k: (j, 0)), out_shape=jax.ShapeDtypeStruct((nb * ka, nw), f32),
            compiler_params=_params(("parallel", "arbitrary")),
        )(a, w)
    T = a.shape[0]
    ka, nw = w.shape[0] // nb, w.shape[1]
    bt = _pick(T, bt_t)
    kin, kout = (ka, nw) if mode == "nn" else (nw, ka)
    dn = _DIMS[mode]

    def body(a_ref, w_ref, o_ref):
        o_ref[...] = lax.dot_general(a_ref[...].astype(bf16), w_ref[...].astype(bf16), dn,
                                     preferred_element_type=f32).astype(o_ref.dtype)

    return pl.pallas_call(
        body, name=name, grid=(nb, T // bt),
        in_specs=[pl.BlockSpec((bt, kin), lambda j, i: (i, j)), pl.BlockSpec((ka, nw), lambda j, i: (j, 0))],
        out_specs=pl.BlockSpec((bt, kout), lambda j, i: (i, j)), out_shape=jax.ShapeDtypeStruct((T, nb * kout), out_dtype),
        compiler_params=_params(("parallel", "parallel")),
    )(a, w)


def _row_index(i, cb):
    return (i, cb)


def _row_kernel(name, fn, rows, pars, row_outs, par_outs=(), block_rows=256):
    T = rows[0][0].shape[0]
    R = min(block_rows, T)
    assert T % R == 0
    nr, npar, nro = len(rows), len(pars), len(row_outs)

    def body(*refs):
        rv = [r[...] for r in refs[:nr]]
        pv = [r[...] for r in refs[nr:nr + npar]]
        ro, po = fn(rv, pv)
        for ref, v in zip(refs[nr + npar:nr + npar + nro], ro):
            ref[...] = v.astype(ref.dtype)
        if par_outs:
            i = pl.program_id(0)
            prefs = refs[nr + npar + nro:]

            @pl.when(i == 0)
            def _():
                for ref, v in zip(prefs, po):
                    ref[...] = v.astype(f32)

            @pl.when(i > 0)
            def _():
                for ref, v in zip(prefs, po):
                    ref[...] += v.astype(f32)

    in_specs = [pl.BlockSpec((R, nc), functools.partial(_row_index, cb=cb)) for (_, nc, cb) in rows]
    in_specs += [pl.BlockSpec(p.shape, lambda i: (0, 0)) for p in pars]
    out_specs = [pl.BlockSpec((R, nc), lambda i: (i, 0)) for (nc, _) in row_outs]
    out_specs += [pl.BlockSpec(s, lambda i: (0, 0)) for s in par_outs]
    out_shape = [jax.ShapeDtypeStruct((T, nc), dt) for (nc, dt) in row_outs]
    out_shape += [jax.ShapeDtypeStruct(s, f32) for s in par_outs]
    outs = pl.pallas_call(
        body, name=name, grid=(T // R,), in_specs=in_specs, out_specs=out_specs, out_shape=out_shape,
        compiler_params=_params(("arbitrary",) if par_outs else ("parallel",)),
    )(*[r[0] for r in rows], *pars)
    return list(outs)


def _fwd_of(f):
    def fn(rv, pv):
        return f([v.astype(f32) for v in rv], [v.astype(f32) for v in pv]), []
    return fn


def _vjp_of(f, n_x, n_cot, grad_idx, n_add=0):
    def fn(rv, pv):
        xs = [v.astype(f32) for v in rv[:n_x]]
        cots = [v.astype(f32) for v in rv[n_x:n_x + n_cot]]
        adds = rv[n_x + n_cot:n_x + n_cot + n_add]
        ps = [v.astype(f32) for v in pv]
        _, vjp = jax.vjp(lambda *a: f(list(a[:n_x]), list(a[n_x:])), *xs, *ps)
        g = vjp(cots)
        row_g = [g[i] for i in grad_idx]
        for k, a in enumerate(adds):
            row_g[k] = row_g[k] + a.astype(f32)
        return row_g, list(g[n_x:])
    return fn


def _rms(x, g):
    return x * lax.rsqrt(jnp.mean(x * x, axis=-1, keepdims=True) + RMS_EPS) * g


def _f_norm(xs, ps):
    return [_rms(xs[0], ps[0])]


def _f_post(scale):
    def f(xs, ps):
        return [scale * _rms(xs[0], ps[0])]
    return f


def _f_resnorm(scale):
    def f(xs, ps):
        return [xs[0] + scale * _rms(xs[1], ps[0])]
    return f


def _f_dt(xs, ps):
    dt = jax.nn.softplus(xs[0] + ps[0])
    return [dt, -jnp.exp(ps[1]) * dt]


def _f_ssdpost(n_groups):
    def f(xs, ps):
        y = xs[0] * jax.nn.silu(xs[1])
        width = y.shape[-1] // n_groups
        lane = lax.broadcasted_iota(jnp.int32, y.shape, 1)
        scale = jnp.zeros_like(y)
        for k in range(n_groups):
            m = ((lane >= k * width) & (lane < (k + 1) * width)).astype(f32)
            ms = jnp.sum(y * y * m, axis=-1, keepdims=True) / width
            scale = scale + lax.rsqrt(ms + RMS_EPS) * m
        return [y * scale * ps[0]]
    return f


def _f_s5post(xs, ps):
    return [jax.nn.gelu(xs[0] + ps[0] * xs[1])]


def _f_merge(xs, ps):
    return [jax.nn.sigmoid(xs[0]) * xs[1] + jax.nn.sigmoid(xs[2]) * xs[3]]


def _swiglu_fwd(rv, pv):
    ab = rv[0].astype(f32)
    h = ab.shape[1] // 2
    return [jax.nn.silu(ab[:, :h]) * ab[:, h:]], []


def _swiglu_bwd(rv, pv):
    ab, d = rv[0].astype(f32), rv[1].astype(f32)
    h = ab.shape[1] // 2
    a, b = ab[:, :h], ab[:, h:]
    s = jax.nn.sigmoid(a)
    return [jnp.concatenate([d * b * (s * (1.0 + a * (1.0 - s))), d * (a * s)], axis=1)], []


def _glu_fwd(rv, pv):
    vg = rv[0].astype(f32)
    h = vg.shape[1] // 2
    return [vg[:, :h] * jax.nn.sigmoid(vg[:, h:])], []


def _glu_bwd(rv, pv):
    vg, d = rv[0].astype(f32), rv[1].astype(f32)
    h = vg.shape[1] // 2
    s = jax.nn.sigmoid(vg[:, h:])
    return [jnp.concatenate([d * s, d * vg[:, :h] * s * (1.0 - s)], axis=1)], []


def _loss_fn(rv, pv):
    e = rv[0].astype(f32) - rv[1].astype(f32)
    per_tok = jnp.mean(e * e, axis=-1, keepdims=True)
    part = 0.5 * jnp.sum(per_tok, axis=0, keepdims=True)
    return [e / e.shape[-1]], [jnp.broadcast_to(part, (8, LANES))]


def _add_fn(rv, pv):
    return [rv[0].astype(f32) + rv[1].astype(f32)], []


def _adamw_fn(rv, pv):
    w, g, m, v = [x.astype(f32) for x in rv]
    m = ADAM_B1 * m + (1.0 - ADAM_B1) * g
    v = ADAM_B2 * v + (1.0 - ADAM_B2) * (g * g)
    m_hat = m / (1.0 - ADAM_B1 ** ADAM_STEP)
    v_hat = v / (1.0 - ADAM_B2 ** ADAM_STEP)
    return [-ADAM_LR * (m_hat / (jnp.sqrt(v_hat) + ADAM_EPS) + ADAM_WD * w), m, v], []


def _adamw(w, g, m, v, name):
    shape = w.shape
    cols = shape[-1] if (w.ndim >= 2 and shape[-1] >= LANES) else None
    if cols is None:
        n = int(np.prod(shape))
        cols = LANES if n % LANES == 0 else n
    n_rows = int(np.prod(shape)) // cols
    br, bc = _pick(n_rows, 256, 8), cols
    if br < 64 and cols % LANES == 0:
        br, bc = n_rows, LANES

    def body(w_ref, g_ref, m_ref, v_ref, d_ref, nm_ref, nv_ref):
        outs, _ = _adamw_fn([w_ref[...], g_ref[...], m_ref[...], v_ref[...]], [])
        d_ref[...], nm_ref[...], nv_ref[...] = outs

    spec = pl.BlockSpec((br, bc), lambda i, j: (i, j))
    outs = pl.pallas_call(
        body, name=name, grid=(n_rows // br, cols // bc), in_specs=[spec] * 4, out_specs=[spec] * 3,
        out_shape=[jax.ShapeDtypeStruct((n_rows, cols), f32)] * 3, compiler_params=_params(("parallel", "parallel")),
    )(*[t.reshape(n_rows, cols) for t in (w, g, m, v)])
    return [o.reshape(shape) for o in outs]


def _shift_down(x, s, row):
    if s == 0:
        return x
    return jnp.where(row >= s, pltpu.roll(x, s, 0), 0.0)


def _shift_up(x, s, row):
    if s == 0:
        return x
    n = x.shape[0]
    return jnp.where(row < n - s, pltpu.roll(x, n - s, 0), 0.0)


def _conv_pre(x, w, b, row):
    kw = w.shape[0]
    c = b
    for k in range(kw):
        c = c + w[k:k + 1, :] * _shift_down(x, kw - 1 - k, row)
    return c


def _conv_fwd(xsrc, col0, w, b, name, bc_t=512):
    T = xsrc.shape[0]
    kw, ncols = w.shape
    bc = _pick(ncols, bc_t)
    off = col0 // bc
    assert col0 % bc == 0

    def body(x_ref, w_ref, b_ref, o_ref):
        x = x_ref[...].astype(f32)
        row = lax.broadcasted_iota(jnp.int32, x.shape, 0)
        c = _conv_pre(x, w_ref[...], b_ref[...], row)
        o_ref[...] = c * jax.nn.sigmoid(c)

    return pl.pallas_call(
        body, name=name, grid=(ncols // bc,),
        in_specs=[pl.BlockSpec((T, bc), lambda j: (0, off + j)), pl.BlockSpec((kw, bc), lambda j: (0, j)),
                  pl.BlockSpec((1, bc), lambda j: (0, j))],
        out_specs=pl.BlockSpec((T, bc), lambda j: (0, j)), out_shape=jax.ShapeDtypeStruct((T, ncols), f32),
        compiler_params=_params(("parallel",)),
    )(xsrc, w, b)


def _conv_bwd(xsrc, col0, w, b, dact, name, bc_t=512):
    T = xsrc.shape[0]
    kw, ncols = w.shape
    bc = _pick(ncols, bc_t)
    off = col0 // bc
    assert col0 % bc == 0

    def body(x_ref, w_ref, b_ref, d_ref, dx_ref, dw_ref, db_ref):
        x = x_ref[...].astype(f32)
        w = w_ref[...]
        row = lax.broadcasted_iota(jnp.int32, x.shape, 0)
        c = _conv_pre(x, w, b_ref[...], row)
        s = jax.nn.sigmoid(c)
        dc = d_ref[...].astype(f32) * (s * (1.0 + c * (1.0 - s)))
        dx = jnp.zeros_like(x)
        dws = []
        for k in range(kw):
            dx = dx + w[k:k + 1, :] * _shift_up(dc, kw - 1 - k, row)
            dws.append(jnp.sum(dc * _shift_down(x, kw - 1 - k, row), axis=0, keepdims=True))
        dx_ref[...] = dx.astype(dx_ref.dtype)
        dw_ref[...] = jnp.concatenate(dws, axis=0)
        db_ref[...] = jnp.sum(dc, axis=0, keepdims=True)

    return pl.pallas_call(
        body, name=name, grid=(ncols // bc,),
        in_specs=[pl.BlockSpec((T, bc), lambda j: (0, off + j)), pl.BlockSpec((kw, bc), lambda j: (0, j)),
                  pl.BlockSpec((1, bc), lambda j: (0, j)), pl.BlockSpec((T, bc), lambda j: (0, j))],
        out_specs=[pl.BlockSpec((T, bc), lambda j: (0, j)), pl.BlockSpec((kw, bc), lambda j: (0, j)),
                   pl.BlockSpec((1, bc), lambda j: (0, j))],
        out_shape=[jax.ShapeDtypeStruct((T, ncols), bf16), jax.ShapeDtypeStruct((kw, ncols), f32),
                   jax.ShapeDtypeStruct((1, ncols), f32)],
        compiler_params=_params(("parallel",)),
    )(xsrc, w, b, dact)


_HI = lax.Precision.HIGHEST


def _dot(a, b, dims="nn", precision=None):
    return lax.dot_general(a, b, _DIMS[dims], preferred_element_type=f32, precision=precision)


def _dot01(a, b, dims="nn", ones="b"):
    x = a if ones == "b" else b
    hi = x.astype(bf16)
    rest = x - hi.astype(f32)
    mid = rest.astype(bf16)
    lo = (rest - mid.astype(f32)).astype(bf16)
    if ones == "b":
        e = b.astype(bf16)
        return _dot(hi, e, dims) + _dot(mid, e, dims) + _dot(lo, e, dims)
    e = a.astype(bf16)
    return _dot(e, hi, dims) + _dot(e, mid, dims) + _dot(e, lo, dims)


def _ssd_common(x_ref, b_ref, c_ref, dt_ref, adt_ref, d_ref, hpg, p):
    q = b_ref.shape[0]
    hp = hpg * p
    bb, cb = b_ref[...].astype(bf16), c_ref[...].astype(bf16)
    r = lax.broadcasted_iota(jnp.int32, (q, q), 0)
    s = lax.broadcasted_iota(jnp.int32, (q, q), 1)
    tril = r >= s
    trilf = tril.astype(f32)
    eh = lax.broadcasted_iota(jnp.int32, (LANES, hp), 0)
    ec = lax.broadcasted_iota(jnp.int32, (LANES, hp), 1)
    expand = ((ec >= eh * p) & (ec < (eh + 1) * p)).astype(f32)
    adt = adt_ref[...]
    cum = _dot01(trilf, adt, "nn", "a")
    cum_t = _dot01(adt, (r <= s).astype(f32), "tn")
    cum_e = _dot01(cum, expand)
    dt_e = _dot01(dt_ref[...], expand)
    d_e = _dot01(jnp.broadcast_to(d_ref[...], (8, LANES)), expand)[0:1, :]
    gmat = _dot(cb, bb, "nt")
    x = x_ref[...]
    xdt = x * dt_e
    e_all = jnp.exp(cum_e)
    dec = jnp.exp(cum_e[q - 1:q, :] - cum_e)
    lms, ms = [], []
    for h in range(hpg):
        lm = jnp.exp(jnp.where(tril, cum[:, h:h + 1] - cum_t[h:h + 1, :], -1e30))
        lms.append(lm)
        ms.append(gmat * lm)
    et = [jnp.exp(cum[q - 1:q, h:h + 1]) for h in range(hpg)]
    return dict(bb=bb, cb=cb, trilf=trilf, expand=expand, cum=cum, x=x, xdt=xdt, dt_e=dt_e, d_e=d_e, e=e_all, dec=dec,
                lms=lms, ms=ms, et=et)


def _ssd_specs(q, hp, n, g_n, nc, rev):
    def cidx(c):
        return (nc - 1 - c) if rev else c
    x_spec = pl.BlockSpec((q, hp), lambda g, c: (cidx(c), g))
    boff = (g_n * hp) // n
    b_spec = pl.BlockSpec((q, n), lambda g, c: (cidx(c), boff + g))
    c_spec = pl.BlockSpec((q, n), lambda g, c: (cidx(c), boff + g_n + g))
    dt_spec = pl.BlockSpec((q, LANES), lambda g, c: (cidx(c), g))
    d_spec = pl.BlockSpec((1, LANES), lambda g, c: (0, g))
    st_spec = pl.BlockSpec((1, 1, hp, n), lambda g, c: (cidx(c), g, 0, 0))
    return x_spec, b_spec, c_spec, dt_spec, d_spec, st_spec


def _ssd_fwd(act, dt, adt, dpad, hpg, p, n, name):
    T = act.shape[0]
    g_n, q = SSD_N_GROUPS, SSD_CHUNK
    nc, hp = T // q, hpg * p
    x_spec, b_spec, c_spec, dt_spec, d_spec, st_spec = _ssd_specs(q, hp, n, g_n, nc, False)

    def body(x_ref, b_ref, c_ref, dt_ref, adt_ref, d_ref, y_ref, st_ref, s_scr):
        @pl.when(pl.program_id(1) == 0)
        def _():
            s_scr[...] = jnp.zeros_like(s_scr)

        k = _ssd_common(x_ref, b_ref, c_ref, dt_ref, adt_ref, d_ref, hpg, p)
        s0 = s_scr[...]
        st_ref[0, 0] = s0
        xdtb = k["xdt"].astype(bf16)
        ydiag = [_dot(k["ms"][h].astype(bf16), xdtb[:, h * p:(h + 1) * p]) for h in range(hpg)]
        z = _dot(k["cb"], s0.astype(bf16), "nt")
        y_ref[...] = jnp.concatenate(ydiag, axis=1) + k["e"] * z + k["d_e"] * k["x"]
        upd = _dot((k["xdt"] * k["dec"]).astype(bf16), k["bb"], "tn")
        for h in range(hpg):
            s_scr[h * p:(h + 1) * p, :] = k["et"][h] * s0[h * p:(h + 1) * p, :] + upd[h * p:(h + 1) * p, :]

    return pl.pallas_call(
        body, name=name, grid=(g_n, nc),
        in_specs=[x_spec, b_spec, c_spec, dt_spec, dt_spec, d_spec],
        out_specs=[pl.BlockSpec((q, hp), lambda g, c: (c, g)), st_spec],
        out_shape=[jax.ShapeDtypeStruct((T, g_n * hp), f32), jax.ShapeDtypeStruct((nc, g_n, hp, n), f32)],
        scratch_shapes=[pltpu.VMEM((hp, n), f32)],
        compiler_params=_params(("parallel", "arbitrary")),
    )(act, act, act, dt, adt, dpad)


def _ssd_bwd(act, dt, adt, dpad, states, dy, hpg, p, n, name):
    T = act.shape[0]
    g_n, q = SSD_N_GROUPS, SSD_CHUNK
    nc, hp = T // q, hpg * p
    x_spec, b_spec, c_spec, dt_spec, d_spec, st_spec = _ssd_specs(q, hp, n, g_n, nc, True)

    def body(x_ref, b_ref, c_ref, dt_ref, adt_ref, d_ref, st_ref, dy_ref,
             dx_ref, db_ref, dc_ref, ddt_ref, dadt_ref, dd_ref, ds_scr):
        first = pl.program_id(1) == 0

        @pl.when(first)
        def _():
            ds_scr[...] = jnp.zeros_like(ds_scr)

        k = _ssd_common(x_ref, b_ref, c_ref, dt_ref, adt_ref, d_ref, hpg, p)
        bb, cb, expand, x, xdt, dec = k["bb"], k["cb"], k["expand"], k["x"], k["xdt"], k["dec"]
        heads = lambda t: _dot01(t, expand, "nt")
        s0 = st_ref[0, 0]
        ds1 = ds_scr[...]
        s0b, ds1b = s0.astype(bf16), ds1.astype(bf16)
        dy = dy_ref[...]
        dyb, xdtb = dy.astype(bf16), xdt.astype(bf16)
        lane = lax.broadcasted_iota(jnp.int32, (1, LANES), 1)
        dg = jnp.zeros((q, q), f32)
        w_rows = jnp.zeros((q, LANES), f32)
        w_cols, dxdt_parts = [], []
        for h in range(hpg):
            hs = slice(h * p, (h + 1) * p)
            dm = _dot(dyb[:, hs], xdtb[:, hs], "nt")
            dg = dg + dm * k["lms"][h]
            wm = dm * k["ms"][h]
            w_rows = w_rows + jnp.sum(wm, axis=1, keepdims=True) * (lane == h).astype(f32)
            w_cols.append(jnp.sum(wm, axis=0, keepdims=True))
            dxdt_parts.append(_dot(k["ms"][h].astype(bf16), dyb[:, hs], "tn"))
        dxdt_diag = jnp.concatenate(dxdt_parts, axis=1)
        w_cols = jnp.concatenate(w_cols + [jnp.zeros((LANES - hpg, q), f32)], axis=0).T
        dgb = dg.astype(bf16)
        z = _dot(cb, s0b, "nt")
        dz = dy * k["e"]
        dzb = dz.astype(bf16)
        dxd = _dot(bb, ds1b, "nt")
        ddec = dxd * xdt * dec
        db_ref[...] = _dot(dgb, cb, "tn") + _dot((xdt * dec).astype(bf16), ds1b)
        dc_ref[...] = _dot(dgb, bb) + _dot(dzb, s0b)
        ds0 = _dot(dzb, cb, "tn")
        for h in range(hpg):
            hs = slice(h * p, (h + 1) * p)
            ds_scr[hs, :] = ds0[hs, :] + k["et"][h] * ds1[hs, :]
        dxdt = dxdt_diag + dxd * dec
        ddec_h = heads(ddec)
        dcum = w_rows - w_cols + heads(dz * z) - ddec_h
        et_row = jnp.exp(k["cum"][q - 1:q, :])
        dsum = _dot01(jnp.ones((8, n), f32), _dot01(expand, ds1 * s0, "nn", "a"), "nt", "a")[0:1, :]
        dcl = dsum * et_row + jnp.sum(ddec_h, axis=0, keepdims=True)
        rowq = lax.broadcasted_iota(jnp.int32, (q, 1), 0)
        dcum = dcum + (rowq == q - 1).astype(f32) * dcl
        ddt_ref[...] = heads(dxdt * x)
        dadt_ref[...] = _dot01(k["trilf"], dcum, "tn", "a")
        dx_ref[...] = k["d_e"] * dy + dxdt * k["dt_e"]
        dd8 = heads(jnp.broadcast_to(jnp.sum(dy * x, axis=0, keepdims=True), (8, hp)))

        @pl.when(first)
        def _():
            dd_ref[...] = dd8

        @pl.when(jnp.logical_not(first))
        def _():
            dd_ref[...] += dd8

    rc = lambda g, c: (nc - 1 - c, g)
    return pl.pallas_call(
        body, name=name, grid=(g_n, nc),
        in_specs=[x_spec, b_spec, c_spec, dt_spec, dt_spec, d_spec, st_spec, pl.BlockSpec((q, hp), rc)],
        out_specs=[pl.BlockSpec((q, hp), rc), pl.BlockSpec((q, n), rc), pl.BlockSpec((q, n), rc),
                   pl.BlockSpec((q, LANES), rc), pl.BlockSpec((q, LANES), rc), pl.BlockSpec((8, LANES), lambda g, c: (g, 0))],
        out_shape=[jax.ShapeDtypeStruct((T, g_n * hp), f32), jax.ShapeDtypeStruct((T, g_n * n), f32),
                   jax.ShapeDtypeStruct((T, g_n * n), f32), jax.ShapeDtypeStruct((T, g_n * LANES), f32),
                   jax.ShapeDtypeStruct((T, g_n * LANES), f32), jax.ShapeDtypeStruct((g_n * 8, LANES), f32)],
        scratch_shapes=[pltpu.VMEM((hp, n), f32)],
        compiler_params=_params(("parallel", "arbitrary")),
    )(act, act, act, dt, adt, dpad, states, dy)


def _cmul(ar, ai, br, bi):
    return ar * br - ai * bi, ar * bi + ai * br


def _s5_tile_powers(lr, li):
    p = [(lr, li)]
    for _ in range(7):
        p.append(_cmul(p[-1][0], p[-1][1], lr, li))
    tile = (jnp.concatenate([q[0] for q in p], axis=0), jnp.concatenate([q[1] for q in p], axis=0))
    return tile, (p[0], p[1], p[3])


def _s5_tile_scan(xr, xi, steps, reverse):
    row = lax.broadcasted_iota(jnp.int32, xr.shape, 0)
    for d, (pr, pi) in zip((1, 2, 4), steps):
        if reverse:
            keep = row < 8 - d
            sr, si = pltpu.roll(xr, 8 - d, 0), pltpu.roll(xi, 8 - d, 0)
        else:
            keep = row >= d
            sr, si = pltpu.roll(xr, d, 0), pltpu.roll(xi, d, 0)
        sr, si = jnp.where(keep, sr, 0.0), jnp.where(keep, si, 0.0)
        ar, ai = _cmul(sr, si, pr, pi)
        xr, xi = xr + ar, xi + ai
    return xr, xi


def _s5_scan_fwd(bu, lam_re, lam_im, nsb, name, tc_t=512):
    T = bu.shape[0]
    w2 = bu.shape[1] // nsb
    w = w2 // 2
    tc = _pick(T, tc_t, 8)

    def body(bu_ref, lr_ref, li_ref, st_ref, carry):
        @pl.when(pl.program_id(1) == 0)
        def _():
            carry[...] = jnp.zeros_like(carry)

        (pr8, pi8), steps = _s5_tile_powers(lr_ref[0:1, :], li_ref[0:1, :])

        def tile(i, c):
            r = pl.ds(pl.multiple_of(i * 8, 8), 8)
            x = bu_ref[r, :]
            xr, xi = _s5_tile_scan(x[:, :w], x[:, w:], steps, False)
            ar, ai = _cmul(pr8, pi8, c[0], c[1])
            xr, xi = xr + ar, xi + ai
            st_ref[r, :] = jnp.concatenate([xr, xi], axis=1)
            return xr[7:8, :], xi[7:8, :]

        c = lax.fori_loop(0, tc // 8, tile, (carry[0:1, :], carry[1:2, :]), unroll=2)
        carry[0:1, :] = c[0]
        carry[1:2, :] = c[1]

    return pl.pallas_call(
        body, name=name, grid=(nsb, T // tc),
        in_specs=[pl.BlockSpec((tc, w2), lambda j, i: (i, j)), pl.BlockSpec((8, w), lambda j, i: (j, 0)),
                  pl.BlockSpec((8, w), lambda j, i: (j, 0))],
        out_specs=pl.BlockSpec((tc, w2), lambda j, i: (i, j)), out_shape=jax.ShapeDtypeStruct(bu.shape, f32),
        scratch_shapes=[pltpu.VMEM((8, w), f32)],
        compiler_params=_params(("parallel", "arbitrary")),
    )(bu, lam_re, lam_im)


def _s5_scan_bwd(gst, states, lam_re, lam_im, nsb, name, tc_t=512):
    T = gst.shape[0]
    w2 = gst.shape[1] // nsb
    w = w2 // 2
    tc = _pick(T, tc_t, 8)
    nt = T // tc
    n_tiles = tc // 8

    def body(g_ref, s_ref, sp_ref, lr_ref, li_ref, a_ref, dlr_ref, dli_ref, carry, acc):
        chunk = pl.program_id(1)

        @pl.when(chunk == 0)
        def _():
            carry[...] = jnp.zeros_like(carry)
            acc[...] = jnp.zeros_like(acc)

        (qr8, qi8), steps = _s5_tile_powers(lr_ref[0:1, :], -li_ref[0:1, :])
        row = lax.broadcasted_iota(jnp.int32, (8, w), 0)
        rev_r, rev_i = jnp.zeros((8, w), f32), jnp.zeros((8, w), f32)
        for r in range(8):
            rev_r = jnp.where(row == r, qr8[7 - r:8 - r, :], rev_r)
            rev_i = jnp.where(row == r, qi8[7 - r:8 - r, :], rev_i)
        row2 = lax.broadcasted_iota(jnp.int32, (8, w2), 0)

        def tile(k, c):
            ar_in, ai_in, dr, di = c
            i = n_tiles - 1 - k
            r = pl.ds(pl.multiple_of(i * 8, 8), 8)
            x = g_ref[r, :]
            xr, xi = _s5_tile_scan(x[:, :w], x[:, w:], steps, True)
            pr, pi = _cmul(rev_r, rev_i, ar_in, ai_in)
            xr, xi = xr + pr, xi + pi
            a_ref[r, :] = jnp.concatenate([xr, xi], axis=1)
            before = jnp.where(i > 0, s_ref[pl.ds(pl.multiple_of(jnp.maximum(i - 1, 0) * 8, 8), 8), :],
                               sp_ref[tc - 8:tc, :] * (chunk < nt - 1).astype(f32))
            prev = jnp.where(row2 == 0, pltpu.roll(before, 1, 0), pltpu.roll(s_ref[r, :], 1, 0))
            spr, spi = prev[:, :w], prev[:, w:]
            return xr[0:1, :], xi[0:1, :], dr + xr * spr + xi * spi, di - xr * spi + xi * spr

        c0 = (carry[0:1, :], carry[1:2, :], acc[0:8, :], acc[8:16, :])
        ar, ai, dr, di = lax.fori_loop(0, n_tiles, tile, c0, unroll=2)
        carry[0:1, :] = ar
        carry[1:2, :] = ai
        acc[0:8, :] = dr
        acc[8:16, :] = di
        dlr_ref[...] = jnp.broadcast_to(jnp.sum(dr, axis=0, keepdims=True), (8, w))
        dli_ref[...] = jnp.broadcast_to(jnp.sum(di, axis=0, keepdims=True), (8, w))

    cur = lambda j, i: (nt - 1 - i, j)
    prv = lambda j, i: (jnp.maximum(nt - 2 - i, 0), j)
    return pl.pallas_call(
        body, name=name, grid=(nsb, nt),
        in_specs=[pl.BlockSpec((tc, w2), cur), pl.BlockSpec((tc, w2), cur), pl.BlockSpec((tc, w2), prv),
                  pl.BlockSpec((8, w), lambda j, i: (j, 0)), pl.BlockSpec((8, w), lambda j, i: (j, 0))],
        out_specs=[pl.BlockSpec((tc, w2), cur), pl.BlockSpec((8, w), lambda j, i: (j, 0)),
                   pl.BlockSpec((8, w), lambda j, i: (j, 0))],
        out_shape=[jax.ShapeDtypeStruct(gst.shape, f32), jax.ShapeDtypeStruct((nsb * 8, w), f32),
                   jax.ShapeDtypeStruct((nsb * 8, w), f32)],
        scratch_shapes=[pltpu.VMEM((8, w), f32), pltpu.VMEM((16, w), f32)],
        compiler_params=_params(("parallel", "arbitrary")),
    )(gst, states, states, lam_re, lam_im)


def _s5_prep_fn(xs, ps):
    lam_re, lam_im, log_step, b_re, b_im, expand = ps
    lr = jnp.minimum(lam_re, S5_MAX_REAL)
    li = lam_im
    step = jnp.exp(log_step)
    er = jnp.exp(lr * step)
    ang = li * step
    lbr, lbi = er * jnp.cos(ang), er * jnp.sin(ang)
    nr, ni = lbr - 1.0, lbi
    den = lr * lr + li * li
    qr, qi = (nr * lr + ni * li) / den, (ni * lr - nr * li) / den
    qre, qie = _dot(qr, expand, "nn", _HI), _dot(qi, expand, "nn", _HI)
    return [lbr, lbi, qre * b_re - qie * b_im, qre * b_im + qie * b_re]


def _s5_prep(pars, name):
    def body(*refs):
        outs = _s5_prep_fn([], [r[...] for r in refs[:6]])
        for ref, v in zip(refs[6:], outs):
            ref[...] = v

    g, nst = pars[0].shape
    nc = pars[3].shape[1]
    return pl.pallas_call(
        body, name=name,
        out_shape=[jax.ShapeDtypeStruct((g, nst), f32)] * 2 + [jax.ShapeDtypeStruct((g, nc), f32)] * 2,
        compiler_params=_params(),
    )(*pars)


def _s5_prep_bwd(pars, cots, name):
    def body(*refs):
        ps = [r[...] for r in refs[:6]]
        ct = [r[...] for r in refs[6:10]]
        _, vjp = jax.vjp(lambda *a: _s5_prep_fn([], list(a)), *ps)
        g = vjp(ct)
        for ref, v in zip(refs[10:], g[:5]):
            ref[...] = v

    return pl.pallas_call(
        body, name=name, out_shape=[jax.ShapeDtypeStruct(p.shape, f32) for p in pars[:5]], compiler_params=_params(),
    )(*pars, *cots)


_ANY = pl.BlockSpec(memory_space=pl.ANY)


def _remote(src, dst, send_sem, recv_sem, device):
    return pltpu.make_async_remote_copy(src_ref=src, dst_ref=dst, send_sem=send_sem, recv_sem=recv_sem, device_id=device,
                                        device_id_type=MESH)


def _staged_copy(src, dst, buf, in_sems, out_sems):
    n = D2D_STREAMS
    piece = src.shape[0] // n
    assert src.shape[0] % n == 0

    def load(i):
        return pltpu.make_async_copy(src.at[pl.ds(i * piece, piece)], buf.at[i % 2], in_sems.at[i % 2])

    def store(i):
        return pltpu.make_async_copy(buf.at[i % 2], dst.at[pl.ds(i * piece, piece)], out_sems.at[i % 2])

    load(0).start()
    for i in range(n):
        if i + 1 < n:
            if i >= 1:
                store(i - 1).wait()
            load(i + 1).start()
        load(i).wait()
        store(i).start()
    store(n - 2).wait()
    store(n - 1).wait()


def _stage_scratch(rows, cols, dtype):
    return [pltpu.VMEM((2, rows // D2D_STREAMS, cols), dtype), pltpu.SemaphoreType.DMA((2,)), pltpu.SemaphoreType.DMA((2,))]


def _chip_all_gather(block, name):
    rows = block.shape[0]
    half = rows // 2
    piece = half // D2D_STREAMS
    assert rows % (2 * D2D_STREAMS * 16) == 0

    def body(src, out, ici_send, ici_recv, d2d_send, d2d_recv, *stage):
        x, y, c = lax.axis_index("x"), lax.axis_index("y"), lax.axis_index("c")
        me = 2 * x + y
        sibling = (x, y, 1 - c)
        chips = [(1 - x, y), (x, 1 - y), (1 - x, 1 - y)]
        mine = pl.ds(pl.multiple_of(c * half, 16), half)
        sends = []
        for j, (px, py) in enumerate(chips):
            cp = _remote(src.at[mine], out.at[me, mine], ici_send.at[j], ici_recv.at[j], (px, py, c))
            cp.start()
            sends.append(cp)
        _staged_copy(src, out.at[me], *stage)
        for j, (px, py) in enumerate(chips):
            slot = 2 * px + py
            _remote(src.at[mine], out.at[slot, mine], ici_send.at[j], ici_recv.at[j], (px, py, c)).wait_recv()
            for s in range(D2D_STREAMS):
                r = pl.ds(pl.multiple_of(c * half + s * piece, 16), piece)
                k = j * D2D_STREAMS + s
                cp = _remote(out.at[slot, r], out.at[slot, r], d2d_send.at[k], d2d_recv.at[k], sibling)
                cp.start()
                sends.append(cp)
        for j, (px, py) in enumerate(chips):
            slot = 2 * px + py
            for s in range(D2D_STREAMS):
                r = pl.ds(pl.multiple_of((1 - c) * half + s * piece, 16), piece)
                k = j * D2D_STREAMS + s
                _remote(out.at[slot, r], out.at[slot, r], d2d_send.at[k], d2d_recv.at[k], sibling).wait_recv()
        for cp in sends:
            cp.wait_send()

    n_d2d = 3 * D2D_STREAMS
    return pl.pallas_call(
        body, name=name, in_specs=[_ANY], out_specs=_ANY,
        out_shape=jax.ShapeDtypeStruct((N_CHIPS,) + block.shape, block.dtype),
        scratch_shapes=[pltpu.SemaphoreType.DMA((3,)), pltpu.SemaphoreType.DMA((3,)), pltpu.SemaphoreType.DMA((n_d2d,)),
                        pltpu.SemaphoreType.DMA((n_d2d,))] + _stage_scratch(rows, block.shape[1], block.dtype),
    )(block)


def _chip_scatter(parts, name):
    def body(src, out, send_sems, recv_sems, *stage):
        x, y, c = lax.axis_index("x"), lax.axis_index("y"), lax.axis_index("c")
        me = 2 * x + y
        chips = [(1 - x, y), (x, 1 - y), (1 - x, 1 - y)]
        sends = []
        for j, (px, py) in enumerate(chips):
            cp = pltpu.make_async_remote_copy(src_ref=src.at[2 * px + py], dst_ref=out.at[me], send_sem=send_sems.at[j],
                                              recv_sem=recv_sems.at[j], device_id=(px, py, c), device_id_type=MESH)
            cp.start()
            sends.append(cp)
        _staged_copy(src.at[me], out.at[me], *stage)
        for j, (px, py) in enumerate(chips):
            pltpu.make_async_remote_copy(src_ref=src.at[me], dst_ref=out.at[2 * px + py], send_sem=send_sems.at[j],
                                         recv_sem=recv_sems.at[j], device_id=(px, py, c), device_id_type=MESH).wait_recv()
        for cp in sends:
            cp.wait_send()

    return pl.pallas_call(
        body, name=name, in_specs=[_ANY], out_specs=_ANY, out_shape=jax.ShapeDtypeStruct(parts.shape, parts.dtype),
        scratch_shapes=[pltpu.SemaphoreType.DMA((3,)), pltpu.SemaphoreType.DMA((3,))]
        + _stage_scratch(parts.shape[1], parts.shape[2], parts.dtype),
    )(parts)


_HBM = pl.BlockSpec(memory_space=pltpu.HBM)
_SEM = pl.BlockSpec(memory_space=pltpu.SEMAPHORE)
_EFFECT = pltpu.SideEffectType.DATAFLOW_SIDE_EFFECTING


def _gather_peers():
    x, y, c = lax.axis_index("x"), lax.axis_index("y"), lax.axis_index("c")
    return x, y, c, 2 * x + y, [(1 - x, y), (x, 1 - y), (1 - x, 1 - y)]


def _chip_gather_start(block, name):
    half = block.shape[0] // 2

    def body(src, land, send_sems, recv_sems, src_out, land_out, token):
        x, y, c, me, chips = _gather_peers()
        mine = pl.ds(pl.multiple_of(c * half, 16), half)
        for j, (px, py) in enumerate(chips):
            _remote(src.at[mine], land.at[me, mine], send_sems.at[j], recv_sems.at[j], (px, py, c)).start()
        token[...] = jnp.zeros_like(token)

    land_shape = (N_CHIPS,) + block.shape
    return pl.pallas_call(
        body, name=name,
        out_shape=(pltpu.SemaphoreType.DMA((3,)), pltpu.SemaphoreType.DMA((3,)), pltpu.HBM(block.shape, block.dtype),
                   pltpu.HBM(land_shape, block.dtype), jax.ShapeDtypeStruct((8, LANES), f32)),
        in_specs=(_HBM, _HBM), out_specs=(_SEM, _SEM, _HBM, _HBM, pl.BlockSpec(memory_space=pltpu.VMEM)),
        input_output_aliases={0: 2, 1: 3}, compiler_params=pltpu.CompilerParams(has_side_effects=_EFFECT),
    )(pltpu.with_memory_space_constraint(block, pltpu.HBM),
      pltpu.with_memory_space_constraint(lax.empty(land_shape, block.dtype), pltpu.HBM))


def _chip_gather_wait(send_sems, recv_sems, block, land, after, name):
    half = block.shape[0] // 2

    def body(src, land_ref, send_ref, recv_ref, after_ref, src_dead, land_out):
        x, y, c, me, chips = _gather_peers()
        mine = pl.ds(pl.multiple_of(c * half, 16), half)
        for j, (px, py) in enumerate(chips):
            cp = _remote(src.at[mine], land_ref.at[2 * px + py, mine], send_ref.at[j], recv_ref.at[j], (px, py, c))
            cp.wait_send()
            cp.wait_recv()

    return pl.pallas_call(
        body, name=name, out_shape=(pltpu.HBM(block.shape, block.dtype), pltpu.HBM(land.shape, land.dtype)),
        in_specs=(_HBM, _HBM, _SEM, _SEM, _ANY), out_specs=(_HBM, _HBM), input_output_aliases={0: 0, 1: 1},
        compiler_params=pltpu.CompilerParams(has_side_effects=_EFFECT),
    )(block, land, send_sems, recv_sems, after)


def _chip_gather_finish(block, land, name):
    rows = block.shape[0]
    half = rows // 2
    piece = half // D2D_STREAMS

    def body(src, land_ref, out, d2d_send, d2d_recv, *stage):
        x, y, c, me, chips = _gather_peers()
        sibling = (x, y, 1 - c)
        sends = []
        for j, (px, py) in enumerate(chips):
            slot = 2 * px + py
            for s in range(D2D_STREAMS):
                r = pl.ds(pl.multiple_of(c * half + s * piece, 16), piece)
                k = j * D2D_STREAMS + s
                cp = _remote(land_ref.at[slot, r], out.at[slot, r], d2d_send.at[k], d2d_recv.at[k], sibling)
                cp.start()
                sends.append(cp)
        _staged_copy(src, out.at[me], *stage)
        for j, (px, py) in enumerate(chips):
            slot = 2 * px + py
            for s in range(D2D_STREAMS):
                r = pl.ds(pl.multiple_of((1 - c) * half + s * piece, 16), piece)
                k = j * D2D_STREAMS + s
                _remote(land_ref.at[slot, r], out.at[slot, r], d2d_send.at[k], d2d_recv.at[k], sibling).wait_recv()
        for cp in sends:
            cp.wait_send()

    n_d2d = 3 * D2D_STREAMS
    return pl.pallas_call(
        body, name=name, in_specs=[_ANY, _ANY], out_specs=_ANY, out_shape=jax.ShapeDtypeStruct(land.shape, land.dtype),
        input_output_aliases={1: 0},
        scratch_shapes=[pltpu.SemaphoreType.DMA((n_d2d,)), pltpu.SemaphoreType.DMA((n_d2d,))]
        + _stage_scratch(rows, block.shape[1], block.dtype),
    )(block, land)


def _chip_scatter_start(parts, name):
    def body(src, land, send_sems, recv_sems, src_out, land_out, token):
        x, y, c = lax.axis_index("x"), lax.axis_index("y"), lax.axis_index("c")
        me = 2 * x + y
        for j, (px, py) in enumerate([(1 - x, y), (x, 1 - y), (1 - x, 1 - y)]):
            _remote(src.at[2 * px + py], land.at[me], send_sems.at[j], recv_sems.at[j], (px, py, c)).start()
        token[...] = jnp.zeros_like(token)

    return pl.pallas_call(
        body, name=name,
        out_shape=(pltpu.SemaphoreType.DMA((3,)), pltpu.SemaphoreType.DMA((3,)), pltpu.HBM(parts.shape, parts.dtype),
                   pltpu.HBM(parts.shape, parts.dtype), jax.ShapeDtypeStruct((8, LANES), f32)),
        in_specs=(_HBM, _HBM), out_specs=(_SEM, _SEM, _HBM, _HBM, pl.BlockSpec(memory_space=pltpu.VMEM)),
        input_output_aliases={0: 2, 1: 3}, compiler_params=pltpu.CompilerParams(has_side_effects=_EFFECT),
    )(pltpu.with_memory_space_constraint(parts, pltpu.HBM),
      pltpu.with_memory_space_constraint(lax.empty(parts.shape, parts.dtype), pltpu.HBM))


def _chip_scatter_wait(send_sems, recv_sems, parts, land, after, name):
    def body(src, land_ref, send_ref, recv_ref, after_ref, src_dead, land_out):
        x, y, c = lax.axis_index("x"), lax.axis_index("y"), lax.axis_index("c")
        me = 2 * x + y
        for j, (px, py) in enumerate([(1 - x, y), (x, 1 - y), (1 - x, 1 - y)]):
            cp = _remote(src.at[2 * px + py], land_ref.at[2 * px + py], send_ref.at[j], recv_ref.at[j], (px, py, c))
            cp.wait_send()
            cp.wait_recv()

    return pl.pallas_call(
        body, name=name, out_shape=(pltpu.HBM(parts.shape, parts.dtype), pltpu.HBM(land.shape, land.dtype)),
        in_specs=(_HBM, _HBM, _SEM, _SEM, _ANY), out_specs=(_HBM, _HBM), input_output_aliases={0: 0, 1: 1},
        compiler_params=pltpu.CompilerParams(has_side_effects=_EFFECT),
    )(parts, land, send_sems, recv_sems, after)


def _sum_slots_own(landed, own, chip, name, block_rows=256):
    s_n, r_n, c_n = landed.shape
    br = _pick(r_n, block_rows, 8)

    def body(chip_ref, land_ref, own_ref, o_ref):
        acc = jnp.zeros((br, c_n), f32)
        for s in range(s_n):
            acc = acc + jnp.where(chip_ref[0] == s, own_ref[s], land_ref[s]).astype(f32)
        o_ref[...] = acc

    spec = pl.BlockSpec((s_n, br, c_n), lambda i, c: (0, i, 0))
    grid_spec = pltpu.PrefetchScalarGridSpec(num_scalar_prefetch=1, grid=(r_n // br,), in_specs=[spec, spec],
                                             out_specs=pl.BlockSpec((br, c_n), lambda i, c: (i, 0)))
    return pl.pallas_call(
        body, name=name, grid_spec=grid_spec, out_shape=jax.ShapeDtypeStruct((r_n, c_n), f32),
        compiler_params=_params(("parallel",)),
    )(chip, landed, own)


def _core_send_other_half(parts, name):
    n_slots, rows, cols = parts.shape
    half = rows // 2
    piece = half // D2D_STREAMS
    assert rows % (2 * D2D_STREAMS * 16) == 0

    def body(src, out, send_sems, recv_sems):
        x, y, c = lax.axis_index("x"), lax.axis_index("y"), lax.axis_index("c")
        sibling = (x, y, 1 - c)
        sends = []
        for k in range(n_slots):
            for s in range(D2D_STREAMS):
                theirs = pl.ds(pl.multiple_of((1 - c) * half + s * piece, 16), piece)
                i = k * D2D_STREAMS + s
                cp = _remote(src.at[k, theirs], out.at[k, pl.ds(s * piece, piece)], send_sems.at[i], recv_sems.at[i], sibling)
                cp.start()
                sends.append(cp)
        for cp in sends:
            cp.wait_recv()
        for cp in sends:
            cp.wait_send()

    n = n_slots * D2D_STREAMS
    return pl.pallas_call(
        body, name=name, in_specs=[_ANY], out_specs=_ANY, out_shape=jax.ShapeDtypeStruct((n_slots, half, cols), parts.dtype),
        scratch_shapes=[pltpu.SemaphoreType.DMA((n,)), pltpu.SemaphoreType.DMA((n,))],
    )(parts)


def _add_my_half(parts, other, core, name, block_rows=256):
    n_slots, rows, cols = parts.shape
    half = rows // 2
    br = _pick(half, block_rows, 16)
    nb = half // br

    def body(c_ref, a_ref, b_ref, o_ref):
        o_ref[...] = (a_ref[...].astype(f32) + b_ref[...].astype(f32)).astype(o_ref.dtype)

    grid_spec = pltpu.PrefetchScalarGridSpec(
        num_scalar_prefetch=1, grid=(n_slots, nb),
        in_specs=[pl.BlockSpec((1, br, cols), lambda k, i, c: (k, c[0] * nb + i, 0)),
                  pl.BlockSpec((1, br, cols), lambda k, i, c: (k, i, 0))],
        out_specs=pl.BlockSpec((1, br, cols), lambda k, i, c: (k, i, 0)))
    return pl.pallas_call(
        body, name=name, grid_spec=grid_spec, out_shape=jax.ShapeDtypeStruct((n_slots, half, cols), parts.dtype),
        compiler_params=_params(("parallel", "parallel")),
    )(core, parts, other)


def _core_join_halves(mine, name):
    half, cols = mine.shape
    piece = half // D2D_STREAMS
    assert half % (D2D_STREAMS * 16) == 0

    def body(src, out, send_sems, recv_sems, *stage):
        x, y, c = lax.axis_index("x"), lax.axis_index("y"), lax.axis_index("c")
        sibling = (x, y, 1 - c)
        sends = []
        for s in range(D2D_STREAMS):
            dst = out.at[pl.ds(pl.multiple_of(c * half + s * piece, 16), piece)]
            cp = _remote(src.at[pl.ds(s * piece, piece)], dst, send_sems.at[s], recv_sems.at[s], sibling)
            cp.start()
            sends.append(cp)
        _staged_copy(src, out.at[pl.ds(pl.multiple_of(c * half, 16), half)], *stage)
        for s in range(D2D_STREAMS):
            dst = out.at[pl.ds(pl.multiple_of((1 - c) * half + s * piece, 16), piece)]
            _remote(src.at[pl.ds(s * piece, piece)], dst, send_sems.at[s], recv_sems.at[s], sibling).wait_recv()
        for cp in sends:
            cp.wait_send()

    return pl.pallas_call(
        body, name=name, in_specs=[_ANY], out_specs=_ANY, out_shape=jax.ShapeDtypeStruct((2 * half, cols), mine.dtype),
        scratch_shapes=[pltpu.SemaphoreType.DMA((D2D_STREAMS,)), pltpu.SemaphoreType.DMA((D2D_STREAMS,))]
        + _stage_scratch(half, cols, mine.dtype),
    )(mine)


def _reduce_start(parts, core, tag):
    chip_part = _add_my_half(parts, _core_send_other_half(parts, f"exchange_core_halves_{tag}"), core, f"sum_core_halves_{tag}")
    return _chip_scatter_start(chip_part, f"scatter_start_{tag}")


def _reduce_finish(started, chip, after, tag):
    send_sems, recv_sems, chip_part, land, _ = started
    own, landed = _chip_scatter_wait(send_sems, recv_sems, chip_part, land, after, f"scatter_wait_{tag}")
    return _core_join_halves(_sum_slots_own(landed, own, chip, f"sum_chip_parts_{tag}"), f"join_core_halves_{tag}")


def _reduce_to_chips(parts, core, tag):
    chip_part = _add_my_half(parts, _core_send_other_half(parts, f"exchange_core_halves_{tag}"), core, f"sum_core_halves_{tag}")
    my_sum = _sum_slots(_chip_scatter(chip_part, f"scatter_{tag}"), f"sum_chip_parts_{tag}")
    return _core_join_halves(my_sum, f"join_core_halves_{tag}")


def _sum_slots(stack, name, block_rows=256):
    s_n, r_n, c_n = stack.shape
    br = _pick(r_n, block_rows, 8)

    def body(in_ref, o_ref):
        acc = in_ref[0].astype(f32)
        for s in range(1, s_n):
            acc = acc + in_ref[s].astype(f32)
        o_ref[...] = acc

    return pl.pallas_call(
        body, name=name, grid=(r_n // br,), in_specs=[pl.BlockSpec((s_n, br, c_n), lambda i: (0, i, 0))],
        out_specs=pl.BlockSpec((br, c_n), lambda i: (i, 0)), out_shape=jax.ShapeDtypeStruct((r_n, c_n), f32),
        compiler_params=_params(("parallel",)),
    )(stack)


def _concat_padded(parts, mult):
    rows = sum(p.shape[0] for p in parts)
    pad = (-rows) % mult
    if pad:
        parts = parts + [jnp.zeros((pad,) + parts[0].shape[1:], parts[0].dtype)]
    return jnp.concatenate(parts, axis=0)


def _pack_weights(w, l, conv_w=None):
    parts = [w[n][l].astype(bf16).reshape(-1, PACK_COLS) for n in BIG]
    if conv_w is not None:
        parts.append(_concat_padded([lax.bitcast_convert_type(conv_w, bf16).reshape(-1, PACK_COLS)], 16))
    return _concat_padded(parts, PACK_ROW_MULT)


def _unpack_weights(full, w, l, conv_w=None):
    start, r0 = {}, 0
    for n in BIG:
        start[n] = r0
        r0 += w[n][l].size // PACK_COLS

    def shards(n):
        rows = w[n][l].size // PACK_COLS
        return [full[k, start[n]:start[n] + rows].reshape(w[n].shape[1:]) for k in range(N_CHIPS)]

    if conv_w is None:
        return shards, None
    rows = conv_w.size * 2 // PACK_COLS
    pieces = lax.bitcast_convert_type(full[:, r0:r0 + rows].reshape((N_CHIPS,) + conv_w.shape + (2,)), f32)
    return shards, jnp.concatenate([pieces[k] for k in range(N_CHIPS)], axis=2)


def _pack_big_grads(layer_grads):
    parts, slot_rows = [], 0
    for k in range(N_CHIPS):
        slot = []
        for n in BIG:
            for g in layer_grads:
                width = g[n].shape[0] // N_CHIPS
                slot.append(g[n][k * width:(k + 1) * width].astype(bf16).reshape(-1, PACK_COLS))
        slot_rows = sum(p.shape[0] for p in slot)
        pad = (-slot_rows) % PACK_ROW_MULT
        if pad:
            slot.append(jnp.zeros((pad, PACK_COLS), bf16))
        slot_rows += pad
        parts += slot
    return jnp.concatenate(parts, axis=0).reshape(N_CHIPS, slot_rows, PACK_COLS)


def _unpack_big_grads(summed, w):
    out, r0 = {}, 0
    for n in BIG:
        rows = w[n][0].size // PACK_COLS
        out[n] = jnp.stack([s[r0:r0 + rows].reshape(w[n].shape[1:]) for s in summed])
        r0 += rows
    return out


_SMALL_TILE = 8 * LANES


def _pack_small(vals, names):
    parts = []
    for n in names:
        pieces = vals[n] if isinstance(vals[n], list) else [vals[n]]
        size = sum(p.size for p in pieces)
        if all(p.size % _SMALL_TILE == 0 for p in pieces):
            parts += [p.reshape(-1, LANES) for p in pieces]
        else:
            flat = [p.reshape(-1) for p in pieces] + [jnp.zeros(((-size) % _SMALL_TILE,), f32)]
            parts.append(jnp.concatenate(flat).reshape(-1, LANES))
    return _concat_padded(parts, PACK_ROW_MULT)


def _unpack_small(packed, like, names):
    out, r0 = {}, 0
    for n in names:
        size = like[n].size
        rows = -(-size // _SMALL_TILE) * 8
        out[n] = packed[r0:r0 + rows].reshape(-1)[:size].reshape(like[n].shape)
        r0 += rows
    return out


def _dims(w, x):
    d = {}
    d["D"] = x.shape[-1]
    d["T"] = x.shape[-2]
    d["DI"] = w["ssd_norm_g"].shape[-1]
    d["NH"] = w["ssd_dt_bias"].shape[-1]
    d["CD"] = w["ssd_conv_b"].shape[-1]
    d["G"] = SSD_N_GROUPS
    d["HPG"] = d["NH"] // d["G"]
    d["P"] = d["DI"] // d["NH"]
    d["N"] = (d["CD"] - d["DI"]) // (2 * d["G"])
    d["S5G"], d["S5N"] = w["s5_lambda_re"].shape[-2:]
    d["S5C"] = w["s5_b_re"].shape[-1]
    d["S5W"] = d["S5G"] * d["S5C"]
    d["NSB"] = d["S5W"] // S5_SUPERBLOCK
    d["GSB"] = d["S5G"] // d["NSB"]
    return d


def _head_pad(v, d):
    lead = v.shape[:-1]
    v = v.reshape(lead + (d["G"], d["HPG"]))
    v = jnp.concatenate([v, jnp.zeros(lead + (d["G"], LANES - d["HPG"]), v.dtype)], axis=-1)
    return v.reshape(lead + (d["G"] * LANES,))


def _head_unpad(v, d):
    lead = v.shape[:-1]
    return v.reshape(lead + (d["G"], LANES))[..., :d["HPG"]].reshape(lead + (d["NH"],))


def _w_in_perm(shards, d):
    o, nh = d["DI"] + d["CD"], d["NH"]
    r = shards[0].shape[0]

    def rows(lo, hi):
        out = []
        for k, s in enumerate(shards):
            a, b = max(lo, k * r), min(hi, (k + 1) * r)
            if a < b:
                out.append(s[a - k * r:b - k * r])
        return out

    dt = _head_pad(jnp.concatenate(rows(o, o + nh), axis=0).T, d).T
    return jnp.concatenate(rows(0, o) + rows(o + nh, len(shards) * r) + [dt], axis=0)


def _w_in_unperm(g, d):
    o = d["DI"] + d["CD"]
    rest = d["S5W"] + 2 * d["D"]
    return jnp.concatenate([g[:o], _head_unpad(g[o + rest:].T, d).T, g[o:o + rest]], axis=0)


def _s5_block_diag(v, d):
    gsb = d["GSB"]
    g, a, b = v.shape
    row_group = (lax.broadcasted_iota(jnp.int32, (g * a, gsb * b), 0) // a) % gsb
    col_group = lax.broadcasted_iota(jnp.int32, (g * a, gsb * b), 1) // b
    return jnp.where(row_group == col_group, jnp.tile(v.reshape(g * a, b), (1, gsb)), 0)


def _s5_diag_blocks(m, d, a, b):
    gsb = d["GSB"]
    rows = m.shape[0]
    m = m.reshape(rows, gsb, b)
    row_group = (lax.broadcasted_iota(jnp.int32, (rows, gsb, 1), 0) // a) % gsb
    col_group = lax.broadcasted_iota(jnp.int32, (rows, gsb, 1), 1)
    return jnp.sum(jnp.where(row_group == col_group, m, 0), axis=1).reshape(rows // a, a, b)


def _s5_lam_rows(v, d):
    v = v.reshape(d["NSB"], 1, d["GSB"] * d["S5N"])
    return jnp.broadcast_to(v, (d["NSB"], 8, v.shape[-1])).reshape(d["NSB"] * 8, -1)


def _ffn_fwd(h, pre_g, post_g, wgu, wd, tag):
    D = h.shape[1]
    H2 = wgu.shape[0]
    xn = _row_kernel(f"{tag}_norm", _fwd_of(_f_norm), [(h, D, 0)], [pre_g], [(D, bf16)])[0]
    ab = _mm(xn, wgu, "nt", f32, f"{tag}_mm_up")
    hid = _row_kernel(f"{tag}_swiglu", _swiglu_fwd, [(ab, H2, 0)], [], [(H2 // 2, bf16)])[0]
    f = _mm(hid, wd, "nn", f32, f"{tag}_mm_down")
    out = _row_kernel(f"{tag}_resnorm", _fwd_of(_f_resnorm(0.5)), [(h, D, 0), (f, D, 0)], [post_g], [(D, f32)])[0]
    return out, dict(h=h, xn=xn, ab=ab, hid=hid, f=f)


def _ffn_bwd(dh_out, s, pre_g, post_g, wgu, wd, tag):
    D = dh_out.shape[1]
    H2 = wgu.shape[0]
    df, dpost = _row_kernel(f"{tag}_resnorm_bwd", _vjp_of(_f_post(0.5), 1, 1, [0]), [(s["f"], D, 0), (dh_out, D, 0)],
                            [post_g], [(D, bf16)], [post_g.shape])
    dwd = _mm(s["hid"], df, "tn", bf16, f"{tag}_mm_dwd")
    dhid = _mm(df, wd, "nt", bf16, f"{tag}_mm_dhid")
    dab = _row_kernel(f"{tag}_swiglu_bwd", _swiglu_bwd, [(s["ab"], H2, 0), (dhid, H2 // 2, 0)], [], [(H2, bf16)])[0]
    dwgu = _mm(dab, s["xn"], "tn", bf16, f"{tag}_mm_dwgu")
    dxn = _mm(dab, wgu, "nn", f32, f"{tag}_mm_dxn")
    dh, dpre = _row_kernel(f"{tag}_norm_bwd", _vjp_of(_f_norm, 1, 1, [0], 1), [(s["h"], D, 0), (dxn, D, 0), (dh_out, D, 0)],
                           [pre_g], [(D, f32)], [pre_g.shape])
    return dh, dict(pre_g=dpre, post_g=dpost, wgu=dwgu, wd=dwd)


def _mixer_fwd(h, p, d):
    D, DI, CD, G, N = d["D"], d["DI"], d["CD"], d["G"], d["N"]
    gl = G * LANES
    c_u5, c_ga, c_gb, c_dt = DI + CD, DI + CD + d["S5W"], DI + CD + d["S5W"] + D, DI + CD + d["S5W"] + 2 * D
    u = _row_kernel("mix_norm", _fwd_of(_f_norm), [(h, D, 0)], [p["mix_pre_g"]], [(D, bf16)])[0]
    proj = _mm(u, p["w_in"], "nt", f32, "mix_mm_in", bn_t=512)
    act = _conv_fwd(proj, DI, p["conv_w"], p["conv_b"], "ssd_conv")
    dt, adt = _row_kernel("ssd_dt", _fwd_of(_f_dt), [(proj, gl, c_dt // gl)], [p["dt_bias"], p["a_log"]], [(gl, f32)] * 2)
    y_ssd, states = _ssd_fwd(act, dt, adt, p["d_skip"], d["HPG"], d["P"], N, "ssd_scan")
    nrm = _row_kernel("ssd_post", _fwd_of(_f_ssdpost(G)), [(y_ssd, DI, 0), (proj, DI, 0)], [p["norm_g"]], [(DI, bf16)])[0]
    y_a = _mm(nrm, p["w_a"], "nn", f32, "mix_mm_a")
    u5 =(proj, d["S5W"], c_u5 // d["S5W"])
    bu = _s5_in(proj, c_u5, p["bsb"], d)
    s5st = _s5_scan_fwd(bu, p["lam_re_rows"], p["lam_im_rows"], d["NSB"], "s5_scan")
    y5 = _bdmm(s5st, p["csb"], "nn", d["NSB"], f32, "s5_mm_c")
    gel = _row_kernel("s5_post", _fwd_of(_f_s5post), [(y5, d["S5W"], 0), u5], [p["s5_d"]], [(d["S5W"], bf16)])[0]
    vg = _mm(gel, p["w_glu"], "nt", f32, "mix_mm_glu")
    glu = _row_kernel("s5_glu", _glu_fwd, [(vg, vg.shape[1], 0)], [], [(vg.shape[1] // 2, bf16)])[0]
    y_b = _mm(glu, p["w_b"], "nn", f32, "mix_mm_b")
    merged = _row_kernel("mix_merge", _fwd_of(_f_merge), [(proj, D, c_ga // D), (y_a, D, 0), (proj, D, c_gb // D), (y_b, D, 0)],
                         [], [(D, bf16)])[0]
    m = _mm(merged, p["w_out"], "nn", f32, "mix_mm_out")
    out = _row_kernel("mix_resnorm", _fwd_of(_f_resnorm(1.0)), [(h, D, 0), (m, D, 0)], [p["mix_post_g"]], [(D, f32)])[0]
    return out, dict(h=h, u=u, proj=proj, act=act, dt=dt, adt=adt, states=states, y_ssd=y_ssd, nrm=nrm, y_a=y_a, s5st=s5st,
                     y5=y5, gel=gel, vg=vg, glu=glu, y_b=y_b, merged=merged, m=m)


def _s5_in(proj, c_u5, bsb, d):
    T = proj.shape[0]
    nsb = d["NSB"]
    ka, nw = S5_SUPERBLOCK, bsb.shape[1]
    off = c_u5 // ka
    assert c_u5 % ka == 0
    bt = _pick(T, 512)

    def body(a_ref, w_ref, o_ref):
        o_ref[...] = _dot(a_ref[...].astype(bf16), w_ref[...].astype(bf16))

    return pl.pallas_call(
        body, name="s5_mm_bu", grid=(nsb, T // bt),
        in_specs=[pl.BlockSpec((bt, ka), lambda j, i: (i, off + j)), pl.BlockSpec((ka, nw), lambda j, i: (j, 0))],
        out_specs=pl.BlockSpec((bt, nw), lambda j, i: (i, j)), out_shape=jax.ShapeDtypeStruct((T, nsb * nw), f32),
        compiler_params=_params(("parallel", "parallel")),
    )(proj, bsb)


def _s5_dbsb(proj, c_u5, a, d):
    T = proj.shape[0]
    nsb = d["NSB"]
    ka, nw = S5_SUPERBLOCK, a.shape[1] // nsb
    off = c_u5 // ka
    bt = _pick(T, 512)

    def body(u_ref, a_ref, o_ref):
        pr = _dot(u_ref[...].astype(bf16), a_ref[...].astype(bf16), "tn")
        k = pl.program_id(1)

        @pl.when(k == 0)
        def _():
            o_ref[...] = pr

        @pl.when(k > 0)
        def _():
            o_ref[...] += pr

    return pl.pallas_call(
        body, name="s5_mm_dbsb", grid=(nsb, T // bt),
        in_specs=[pl.BlockSpec((bt, ka), lambda j, k: (k, off + j)), pl.BlockSpec((bt, nw), lambda j, k: (k, j))],
        out_specs=pl.BlockSpec((ka, nw), lambda j, k: (j, 0)), out_shape=jax.ShapeDtypeStruct((nsb * ka, nw), f32),
        compiler_params=_params(("parallel", "arbitrary")),
    )(proj, a)


def _mixer_bwd(dh_out, s, p, d):
    D, DI, CD, G, N, S5W = d["D"], d["DI"], d["CD"], d["G"], d["N"], d["S5W"]
    gl = G * LANES
    gn = G * N
    c_u5, c_ga, c_gb, c_dt = DI + CD, DI + CD + S5W, DI + CD + S5W + D, DI + CD + S5W + 2 * D
    proj = s["proj"]
    g = {}
    dm, g["mix_post_g"] = _row_kernel("mix_resnorm_bwd", _vjp_of(_f_post(1.0), 1, 1, [0]), [(s["m"], D, 0), (dh_out, D, 0)],
                                      [p["mix_post_g"]], [(D, bf16)], [p["mix_post_g"].shape])
    g["w_out"] = _mm(s["merged"], dm, "tn", bf16, "mix_mm_dwout")
    dmerged = _mm(dm, p["w_out"], "nt", f32, "mix_mm_dmerged")
    dga, dya, dgb, dyb = _row_kernel(
        "mix_merge_bwd", _vjp_of(_f_merge, 4, 1, [0, 1, 2, 3]),
        [(proj, D, c_ga // D), (s["y_a"], D, 0), (proj, D, c_gb // D), (s["y_b"], D, 0), (dmerged, D, 0)], [],
        [(D, bf16), (D, bf16), (D, bf16), (D, bf16)])
    g["w_a"] = _mm(s["nrm"], dya, "tn", bf16, "mix_mm_dwa")
    dnrm = _mm(dya, p["w_a"], "nt", f32, "mix_mm_dnrm")
    dy_ssd, dz, g["norm_g"] = _row_kernel(
        "ssd_post_bwd", _vjp_of(_f_ssdpost(G), 2, 1, [0, 1]), [(s["y_ssd"], DI, 0), (proj, DI, 0), (dnrm, DI, 0)],
        [p["norm_g"]], [(DI, f32), (DI, bf16)], [p["norm_g"].shape])
    dxs, d_b, d_c, ddt, dadt, dd = _ssd_bwd(s["act"], s["dt"], s["adt"], p["d_skip"], s["states"], dy_ssd,
                                            d["HPG"], d["P"], N, "ssd_scan_bwd")
    g["d_skip"] = dd.reshape(G, 8, LANES)[:, 0, :].reshape(1, gl)
    ddt_raw, g["dt_bias"], g["a_log"] = _row_kernel(
        "ssd_dt_bwd", _vjp_of(_f_dt, 1, 2, [0]), [(proj, gl, c_dt // gl), (ddt, gl, 0), (dadt, gl, 0)],
        [p["dt_bias"], p["a_log"]], [(gl, bf16)], [p["dt_bias"].shape, p["a_log"].shape])
    cw, cb = p["conv_w"], p["conv_b"]
    dxc_x, dw_x, db_x = _conv_bwd(proj, DI, cw[:, :DI], cb[:, :DI], dxs, "ssd_conv_bwd_x")
    dxc_b, dw_b, db_b = _conv_bwd(proj, 2 * DI, cw[:, DI:DI + gn], cb[:, DI:DI + gn], d_b, "ssd_conv_bwd_b")
    dxc_c, dw_c, db_c = _conv_bwd(proj, 2 * DI + gn, cw[:, DI + gn:], cb[:, DI + gn:], d_c, "ssd_conv_bwd_c")
    g["conv_w"] = jnp.concatenate([dw_x, dw_b, dw_c], axis=1)
    g["conv_b"] = jnp.concatenate([db_x, db_b, db_c], axis=1)
    g["w_b"] = _mm(s["glu"], dyb, "tn", bf16, "mix_mm_dwb")
    dglu = _mm(dyb, p["w_b"], "nt", f32, "mix_mm_dglu")
    dvg = _row_kernel("s5_glu_bwd", _glu_bwd, [(s["vg"], s["vg"].shape[1], 0), (dglu, S5W, 0)], [], [(s["vg"].shape[1], bf16)])[0]
    g["w_glu"] = _mm(dvg, s["gel"], "tn", bf16, "mix_mm_dwglu")
    dgel = _mm(dvg, p["w_glu"], "nn", f32, "mix_mm_dgel")
    dy5, du5a, g["s5_d"] = _row_kernel(
        "s5_post_bwd", _vjp_of(_f_s5post, 2, 1, [0, 1]), [(s["y5"], S5W, 0), (proj, S5W, c_u5 // S5W), (dgel, S5W, 0)],
        [p["s5_d"]], [(S5W, bf16), (S5W, f32)], [p["s5_d"].shape])
    g["csb"] = _bdmm(s["s5st"], dy5, "tn", d["NSB"], f32, "s5_mm_dcsb")
    gst = _bdmm(dy5, p["csb"], "nt", d["NSB"], f32, "s5_mm_gst")
    a, g["lam_re_rows"], g["lam_im_rows"] = _s5_scan_bwd(gst, s["s5st"], p["lam_re_rows"], p["lam_im_rows"], d["NSB"], "s5_scan_bwd")
    g["bsb"] = _s5_dbsb(proj, c_u5, a, d)
    du5b = _bdmm(a, p["bsb"], "nt", d["NSB"], f32, "s5_mm_du5")
    du5 = _row_kernel("s5_du5", _add_fn, [(du5a, S5W, 0), (du5b, S5W, 0)], [], [(S5W, bf16)])[0]
    dproj = jnp.concatenate([dz, dxc_x, dxc_b, dxc_c, du5, dga, dgb, ddt_raw], axis=1)
    g["w_in"] = _mm(dproj, s["u"], "tn", bf16, "mix_mm_dwin")
    du = _mm(dproj, p["w_in"], "nn", f32, "mix_mm_du", bk_t=2176)
    dh, g["mix_pre_g"] = _row_kernel("mix_norm_bwd", _vjp_of(_f_norm, 1, 1, [0], 1), [(s["h"], D, 0), (du, D, 0), (dh_out, D, 0)],
                                     [p["mix_pre_g"]], [(D, f32)], [p["mix_pre_g"].shape])
    return dh, g


def _layer_params(l, w, wf, conv_w_full, d):
    r2 = lambda v: v[l].reshape(1, -1)
    p = {}
    for n in ["ffn1_pre_g", "ffn1_post_g", "mix_pre_g", "mix_post_g", "ffn2_pre_g", "ffn2_post_g", "s5_d"]:
        p[n] = r2(w[n])
    whole = lambda *names: jnp.concatenate([s for n in names for s in wf(n)], axis=0)
    p["wgu1"] = whole("ffn1_w_gate", "ffn1_w_up")
    p["wd1"] = whole("ffn1_w_down")
    p["wgu2"] = whole("ffn2_w_gate", "ffn2_w_up")
    p["wd2"] = whole("ffn2_w_down")
    p["w_in"] = _w_in_perm(wf("w_in"), d)
    p["w_a"], p["w_glu"], p["w_b"], p["w_out"] = whole("w_branch_a"), whole("s5_w_glu"), whole("w_branch_b"), whole("w_out")
    p["conv_w"] = conv_w_full[l]
    p["conv_b"] = r2(w["ssd_conv_b"])
    p["dt_bias"] = _head_pad(r2(w["ssd_dt_bias"]), d)
    p["a_log"] = _head_pad(r2(w["ssd_a_log"]), d)
    p["d_skip"] = _head_pad(r2(w["ssd_d"]), d)
    p["norm_g"] = r2(w["ssd_norm_g"])
    g5, n5, c5 = d["S5G"], d["S5N"], d["S5C"]
    expand = jnp.repeat(jnp.eye(n5, dtype=f32), c5, axis=1)
    prep_in = [w["s5_lambda_re"][l], w["s5_lambda_im"][l], w["s5_log_step"][l].reshape(g5, 1),
               w["s5_b_re"][l].reshape(g5, n5 * c5), w["s5_b_im"][l].reshape(g5, n5 * c5), expand]
    lbr, lbi, bbr, bbi = _s5_prep(prep_in, "s5_prep")
    p["s5_prep_in"] = prep_in
    p["lam_re_rows"], p["lam_im_rows"] = _s5_lam_rows(lbr, d), _s5_lam_rows(lbi, d)
    to_cn = lambda v: v.reshape(g5, n5, c5).transpose(0, 2, 1)
    p["bsb"] = jnp.concatenate([_s5_block_diag(to_cn(bbr), d), _s5_block_diag(to_cn(bbi), d)], axis=1).astype(bf16)
    c_re, c_im = w["s5_c_re"][l].transpose(0, 2, 1), w["s5_c_im"][l].transpose(0, 2, 1)
    nsb = d["NSB"]
    csb = jnp.stack([_s5_block_diag(c_re, d).reshape(nsb, -1, S5_SUPERBLOCK),
                     _s5_block_diag(-c_im, d).reshape(nsb, -1, S5_SUPERBLOCK)], axis=1)
    p["csb"] = csb.reshape(-1, S5_SUPERBLOCK).astype(bf16)
    return p


def _s5_param_grads(g, p, d, l):
    g5, n5, c5, nsb, gsb = d["S5G"], d["S5N"], d["S5C"], d["NSB"], d["GSB"]
    wst = gsb * n5
    dbsb = g["bsb"]
    from_cn = lambda v: v.transpose(0, 2, 1).reshape(g5, n5 * c5)
    dbbr = from_cn(_s5_diag_blocks(dbsb[:, :wst], d, c5, n5))
    dbbi = from_cn(_s5_diag_blocks(dbsb[:, wst:], d, c5, n5))
    rows = lambda v: v.reshape(nsb, 8, wst)[:, 0, :].reshape(g5, n5)
    cots = [rows(g["lam_re_rows"]), rows(g["lam_im_rows"]), dbbr, dbbi]
    dlr, dli, dls, dbr, dbi = _s5_prep_bwd(p["s5_prep_in"], cots, "s5_prep_bwd")
    dcsb = g["csb"].reshape(nsb, 2, wst, S5_SUPERBLOCK)
    dcr = _s5_diag_blocks(dcsb[:, 0].reshape(-1, S5_SUPERBLOCK), d, n5, c5).transpose(0, 2, 1)
    dci = -_s5_diag_blocks(dcsb[:, 1].reshape(-1, S5_SUPERBLOCK), d, n5, c5).transpose(0, 2, 1)
    return dict(s5_lambda_re=dlr, s5_lambda_im=dli, s5_log_step=dls.reshape(g5), s5_b_re=dbr.reshape(g5, n5, c5),
                s5_b_im=dbi.reshape(g5, n5, c5), s5_c_re=dcr, s5_c_im=dci)


def kernel(x, ffn1_pre_g, ffn1_post_g, ffn1_w_gate, ffn1_w_up, ffn1_w_down, mix_pre_g, mix_post_g, w_in, ssd_conv_w, ssd_conv_b, ssd_dt_bias, ssd_a_log, ssd_d, ssd_norm_g, w_branch_a, s5_lambda_re, s5_lambda_im, s5_b_re, s5_b_im, s5_c_re, s5_c_im, s5_log_step, s5_d, s5_w_glu, w_branch_b, w_out, ffn2_pre_g, ffn2_post_g, ffn2_w_gate, ffn2_w_up, ffn2_w_down, loss_target, m_ffn1_pre_g, m_ffn1_post_g, m_ffn1_w_gate, m_ffn1_w_up, m_ffn1_w_down, m_mix_pre_g, m_mix_post_g, m_w_in, m_ssd_conv_w, m_ssd_conv_b, m_ssd_dt_bias, m_ssd_a_log, m_ssd_d, m_ssd_norm_g, m_w_branch_a, m_s5_lambda_re, m_s5_lambda_im, m_s5_b_re, m_s5_b_im, m_s5_c_re, m_s5_c_im, m_s5_log_step, m_s5_d, m_s5_w_glu, m_w_branch_b, m_w_out, m_ffn2_pre_g, m_ffn2_post_g, m_ffn2_w_gate, m_ffn2_w_up, m_ffn2_w_down, v_ffn1_pre_g, v_ffn1_post_g, v_ffn1_w_gate, v_ffn1_w_up, v_ffn1_w_down, v_mix_pre_g, v_mix_post_g, v_w_in, v_ssd_conv_w, v_ssd_conv_b, v_ssd_dt_bias, v_ssd_a_log, v_ssd_d, v_ssd_norm_g, v_w_branch_a, v_s5_lambda_re, v_s5_lambda_im, v_s5_b_re, v_s5_b_im, v_s5_c_re, v_s5_c_im, v_s5_log_step, v_s5_d, v_s5_w_glu, v_w_branch_b, v_w_out, v_ffn2_pre_g, v_ffn2_post_g, v_ffn2_w_gate, v_ffn2_w_up, v_ffn2_w_down):
    given = dict(locals())
    for n in COL_SHARDED:
        for prefix in ("", "m_", "v_"):
            given[prefix + n] = given[prefix + n].transpose(0, 2, 1)
    w = {n: given[n] for n in WEIGHTS}
    mom = {n: given["m_" + n] for n in WEIGHTS}
    var = {n: given["v_" + n] for n in WEIGHTS}
    d = _dims(w, x)
    n_layers = w["ffn1_pre_g"].shape[0]
    T, D = d["T"], d["D"]

    conv_w = w["ssd_conv_w"]
    wf, conv_w_full = _unpack_weights(_chip_all_gather(_pack_weights(w, 0, conv_w), "gather_weights_l0"), w, 0, conv_w)
    coming = [None] + [_chip_gather_start(_pack_weights(w, l), f"gather_start_l{l}") for l in range(1, n_layers)]

    h = x.reshape(T, D)
    for started in coming[1:]:
        h = h + started[-1][0, 0]
    saved, layers = [], []
    for l in range(n_layers):
        if l > 0:
            send_sems, recv_sems, pack, land, _ = coming[l]
            pack, land = _chip_gather_wait(send_sems, recv_sems, pack, land, h, f"gather_wait_l{l}")
            wf, _ = _unpack_weights(_chip_gather_finish(pack, land, f"gather_finish_l{l}"), w, l)
        p = _layer_params(l, w, wf, conv_w_full, d)
        layers.append(p)
        h, s1 = _ffn_fwd(h, p["ffn1_pre_g"], p["ffn1_post_g"], p["wgu1"], p["wd1"], "ffn1")
        h, sm = _mixer_fwd(h, p, d)
        h, s2 = _ffn_fwd(h, p["ffn2_pre_g"], p["ffn2_post_g"], p["wgu2"], p["wd2"], "ffn2")
        saved.append((s1, sm, s2))
    dh, loss_part = _row_kernel("loss", _loss_fn, [(h, D, 0), (loss_target.reshape(T, D), D, 0)], [], [(D, f32)], [(8, LANES)])
    loss = lax.psum(loss_part[0, 0], ("x", "y", "c"))

    my_core = lax.axis_index("c").astype(jnp.int32).reshape(1)
    my_chip = (2 * lax.axis_index("x") + lax.axis_index("y")).astype(jnp.int32).reshape(1)
    lg, in_flight = [None] * n_layers, [None] * n_layers
    for l in reversed(range(n_layers)):
        p = layers[l]
        s1, sm, s2 = saved[l]
        dh, g2 = _ffn_bwd(dh, s2, p["ffn2_pre_g"], p["ffn2_post_g"], p["wgu2"], p["wd2"], "ffn2")
        dh, gm = _mixer_bwd(dh, sm, p, d)
        dh, g1 = _ffn_bwd(dh, s1, p["ffn1_pre_g"], p["ffn1_post_g"], p["wgu1"], p["wd1"], "ffn1")
        H = p["wd1"].shape[0]
        gl = dict(ffn1_pre_g=g1["pre_g"], ffn1_post_g=g1["post_g"], ffn1_w_gate=g1["wgu"][:H], ffn1_w_up=g1["wgu"][H:],
                  ffn1_w_down=g1["wd"], ffn2_pre_g=g2["pre_g"], ffn2_post_g=g2["post_g"], ffn2_w_gate=g2["wgu"][:H],
                  ffn2_w_up=g2["wgu"][H:], ffn2_w_down=g2["wd"], mix_pre_g=gm["mix_pre_g"], mix_post_g=gm["mix_post_g"],
                  w_in=_w_in_unperm(gm["w_in"], d), ssd_conv_w=gm["conv_w"], ssd_conv_b=gm["conv_b"],
                  ssd_dt_bias=_head_unpad(gm["dt_bias"], d), ssd_a_log=_head_unpad(gm["a_log"], d),
                  ssd_d=_head_unpad(gm["d_skip"], d), ssd_norm_g=gm["norm_g"], w_branch_a=gm["w_a"], s5_d=gm["s5_d"],
                  s5_w_glu=gm["w_glu"], w_branch_b=gm["w_b"], w_out=gm["w_out"])
        gl.update(_s5_param_grads(gm, p, d, l))
        lg[l] = gl
        if l > 0:
            in_flight[l] = _reduce_start(_pack_big_grads([gl]), my_core, f"grads_l{l}")
            dh = dh + in_flight[l][-1][0, 0]
    grad_x = dh.reshape(x.shape)
    in_flight[0] = _reduce_start(_pack_big_grads([lg[0]]), my_core, "grads_l0")
    summed = [None] * n_layers
    for l in range(1, n_layers):
        summed[l] = _reduce_finish(in_flight[l], my_chip, in_flight[0][-1], f"grads_l{l}")
    small_names = SMALL + ["ssd_conv_w"]
    small_parts = {n: [g[n] for g in lg] for n in small_names}
    small_like = {n: jax.ShapeDtypeStruct((n_layers,) + lg[0][n].shape, f32) for n in small_names}
    small_like.update({n: w[n] for n in SMALL})
    small_pack = _pack_small(small_parts, small_names)
    small_sum = _reduce_to_chips(jnp.broadcast_to(small_pack, (N_CHIPS,) + small_pack.shape), my_core, "small")
    small = _unpack_small(small_sum, small_like, small_names)
    summed[0] = _reduce_finish(in_flight[0], my_chip, small_sum, "grads_l0")
    grads = _unpack_big_grads(summed, w)
    k_me = 2 * lax.axis_index("x") + lax.axis_index("y")
    cw = w["ssd_conv_w"].shape[-1]
    small["ssd_conv_w"] = lax.dynamic_slice_in_dim(small["ssd_conv_w"], k_me * cw, cw, axis=2)
    grads.update(small)

    delta, new_m, new_v = {}, {}, {}
    for n in BIG + ["ssd_conv_w"]:
        delta[n], new_m[n], new_v[n] = _adamw(w[n], grads[n], mom[n], var[n], "adamw_" + n)
    pw, pm, pv = [_pack_small(t, SMALL) for t in (w, mom, var)]
    assert pw.shape[0] <= small_sum.shape[0]
    sd, sm_, sv = _adamw(pw, small_sum[:pw.shape[0]], pm, pv, "adamw_small")
    delta.update(_unpack_small(sd, w, SMALL))
    new_m.update(_unpack_small(sm_, w, SMALL))
    new_v.update(_unpack_small(sv, w, SMALL))
    for n in COL_SHARDED:
        for out in (grads, delta, new_m, new_v):
            out[n] = out[n].transpose(0, 2, 1)
    return (loss, grad_x, *[grads[n] for n in WEIGHTS], *[delta[n] for n in WEIGHTS],
            *[new_m[n] for n in WEIGHTS], *[new_v[n] for n in WEIGHTS])
```

```python
import functools

import numpy as np
import jax
import jax.numpy as jnp
from jax import lax
from jax.experimental import pallas as pl
from jax.experimental.pallas import tpu as pltpu

f32, bf16 = jnp.float32, jnp.bfloat16

SSD_N_GROUPS = 4
SSD_CHUNK = 128
RMS_EPS = 1e-6
S5_MAX_REAL = -1e-4
S5_SUPERBLOCK = 256
ADAM_LR, ADAM_B1, ADAM_B2, ADAM_EPS, ADAM_WD, ADAM_STEP = 0.001, 0.9, 0.999, 1e-08, 0.01, 10

LANES = 128
PACK_COLS = 1024
D2D_STREAMS = 16
PACK_ROW_MULT = 2 * D2D_STREAMS * 16
VMEM_LIMIT_BYTES = 48 * 1024 * 1024
N_CHIPS, N_CORES, N_DEV = 4, 2, 8
MESH = pl.DeviceIdType.MESH

BIG = ["ffn1_w_gate", "ffn1_w_up", "ffn1_w_down", "w_in", "w_branch_a", "s5_w_glu", "w_branch_b", "w_out",
       "ffn2_w_gate", "ffn2_w_up", "ffn2_w_down"]
COL_SHARDED = ["ffn1_w_gate", "ffn1_w_up", "w_in", "s5_w_glu", "ffn2_w_gate", "ffn2_w_up"]
SMALL = ["ffn1_pre_g", "ffn1_post_g", "mix_pre_g", "mix_post_g", "ssd_conv_b", "ssd_norm_g", "s5_lambda_re", "s5_lambda_im",
         "s5_b_re", "s5_b_im", "s5_c_re", "s5_c_im", "s5_d", "ffn2_pre_g", "ffn2_post_g", "s5_log_step", "ssd_dt_bias",
         "ssd_a_log", "ssd_d"]
WEIGHTS = ["ffn1_pre_g", "ffn1_post_g", "ffn1_w_gate", "ffn1_w_up", "ffn1_w_down", "mix_pre_g", "mix_post_g", "w_in",
           "ssd_conv_w", "ssd_conv_b", "ssd_dt_bias", "ssd_a_log", "ssd_d", "ssd_norm_g", "w_branch_a", "s5_lambda_re",
           "s5_lambda_im", "s5_b_re", "s5_b_im", "s5_c_re", "s5_c_im", "s5_log_step", "s5_d", "s5_w_glu", "w_branch_b",
           "w_out", "ffn2_pre_g", "ffn2_post_g", "ffn2_w_gate", "ffn2_w_up", "ffn2_w_down"]


def _params(sem=None):
    return pltpu.CompilerParams(dimension_semantics=sem, vmem_limit_bytes=VMEM_LIMIT_BYTES)


def _pick(n, target, mult=LANES):
    best = None
    for d in range(mult, min(n, target) + 1, mult):
        if n % d == 0:
            best = d
    return best if best is not None else n


_DIMS = {"nn": (((1,), (0,)), ((), ())), "nt": (((1,), (1,)), ((), ())), "tn": (((0,), (0,)), ((), ()))}


def _mm(a, b, mode, out_dtype, name, bm_t=1024, bn_t=1024, bk_t=2816):
    if mode == "nn":
        (M, K), (K2, N) = a.shape, b.shape
    elif mode == "nt":
        (M, K), (N, K2) = a.shape, b.shape
    else:
        (K, M), (K2, N) = a.shape, b.shape
    assert K == K2, (name, a.shape, b.shape)
    bm, bn, bk = _pick(M, bm_t), _pick(N, bn_t), _pick(K, bk_t)
    nk = K // bk
    dn = _DIMS[mode]

    def body(a_ref, b_ref, o_ref, *scratch):
        p = lax.dot_general(a_ref[...].astype(bf16), b_ref[...].astype(bf16), dn, preferred_element_type=f32)
        if nk == 1:
            o_ref[...] = p.astype(o_ref.dtype)
        else:
            acc = scratch[0]
            k = pl.program_id(2)

            @pl.when(k == 0)
            def _():
                acc[...] = p

            @pl.when(k > 0)
            def _():
                acc[...] += p

            @pl.when(k == nk - 1)
            def _():
                o_ref[...] = acc[...].astype(o_ref.dtype)

    if mode == "tn":
        a_spec = pl.BlockSpec((bk, bm), lambda i, j, k: (k, i))
    else:
        a_spec = pl.BlockSpec((bm, bk), lambda i, j, k: (i, k))
    if mode == "nt":
        b_spec = pl.BlockSpec((bn, bk), lambda i, j, k: (j, k))
    else:
        b_spec = pl.BlockSpec((bk, bn), lambda i, j, k: (k, j))
    return pl.pallas_call(
        body, name=name, grid=(M // bm, N // bn, nk), in_specs=[a_spec, b_spec],
        out_specs=pl.BlockSpec((bm, bn), lambda i, j, k: (i, j)), out_shape=jax.ShapeDtypeStruct((M, N), out_dtype),
        scratch_shapes=[pltpu.VMEM((bm, bn), f32)] if nk > 1 else [],
        compiler_params=_params(("parallel", "parallel", "arbitrary")),
    )(a, b)


def _bdmm(a, w, mode, nb, out_dtype, name, bt_t=512):
    if mode == "tn":
        T = a.shape[0]
        ka, nw = a.shape[1] // nb, w.shape[1] // nb
        bt = _pick(T, bt_t)
        nt = T // bt

        def body_tn(a_ref, b_ref, o_ref):
            p = lax.dot_general(a_ref[...].astype(bf16), b_ref[...].astype(bf16), _DIMS["tn"], preferred_element_type=f32)
            k = pl.program_id(1)

            @pl.when(k == 0)
            def _():
                o_ref[...] = p

            @pl.when(k > 0)
            def _():
                o_ref[...] += p

        return pl.pallas_call(
            body_tn, name=name, grid=(nb, nt),
            in_specs=[pl.BlockSpec((bt, ka), lambda j, k: (k, j)), pl.BlockSpec((bt, nw), lambda j, k: (k, j))],
            out_specs=pl.BlockSpec((ka, nw), lambda j, k: (j, 0)), out_shape=jax.ShapeDtypeStruct((nb * ka, nw), f32),
            compiler_params=_params(("parallel", "arbitrary")),
        )(a, w)
    T = a.shape[0]
    ka, nw = w.shape[0] // nb, w.shape[1]
    bt = _pick(T, bt_t)
    kin, kout = (ka, nw) if mode == "nn" else (nw, ka)
    dn = _DIMS[mode]

    def body(a_ref, w_ref, o_ref):
        o_ref[...] = lax.dot_general(a_ref[...].astype(bf16), w_ref[...].astype(bf16), dn,
                                     preferred_element_type=f32).astype(o_ref.dtype)

    return pl.pallas_call(
        body, name=name, grid=(nb, T // bt),
        in_specs=[pl.BlockSpec((bt, kin), lambda j, i: (i, j)), pl.BlockSpec((ka, nw), lambda j, i: (j, 0))],
        out_specs=pl.BlockSpec((bt, kout), lambda j, i: (i, j)), out_shape=jax.ShapeDtypeStruct((T, nb * kout), out_dtype),
        compiler_params=_params(("parallel", "parallel")),
    )(a, w)


def _row_index(i, cb):
    return (i, cb)


def _row_kernel(name, fn, rows, pars, row_outs, par_outs=(), block_rows=256):
    T = rows[0][0].shape[0]
    R = min(block_rows, T)
    assert T % R == 0
    nr, npar, nro = len(rows), len(pars), len(row_outs)

    def body(*refs):
        rv = [r[...] for r in refs[:nr]]
        pv = [r[...] for r in refs[nr:nr + npar]]
        ro, po = fn(rv, pv)
        for ref, v in zip(refs[nr + npar:nr + npar + nro], ro):
            ref[...] = v.astype(ref.dtype)
        if par_outs:
            i = pl.program_id(0)
            prefs = refs[nr + npar + nro:]

            @pl.when(i == 0)
            def _():
                for ref, v in zip(prefs, po):
                    ref[...] = v.astype(f32)

            @pl.when(i > 0)
            def _():
                for ref, v in zip(prefs, po):
                    ref[...] += v.astype(f32)

    in_specs = [pl.BlockSpec((R, nc), functools.partial(_row_index, cb=cb)) for (_, nc, cb) in rows]
    in_specs += [pl.BlockSpec(p.shape, lambda i: (0, 0)) for p in pars]
    out_specs = [pl.BlockSpec((R, nc), lambda i: (i, 0)) for (nc, _) in row_outs]
    out_specs += [pl.BlockSpec(s, lambda i: (0, 0)) for s in par_outs]
    out_shape = [jax.ShapeDtypeStruct((T, nc), dt) for (nc, dt) in row_outs]
    out_shape += [jax.ShapeDtypeStruct(s, f32) for s in par_outs]
    outs = pl.pallas_call(
        body, name=name, grid=(T // R,), in_specs=in_specs, out_specs=out_specs, out_shape=out_shape,
        compiler_params=_params(("arbitrary",) if par_outs else ("parallel",)),
    )(*[r[0] for r in rows], *pars)
    return list(outs)


def _fwd_of(f):
    def fn(rv, pv):
        return f([v.astype(f32) for v in rv], [v.astype(f32) for v in pv]), []
    return fn


def _vjp_of(f, n_x, n_cot, grad_idx, n_add=0):
    def fn(rv, pv):
        xs = [v.astype(f32) for v in rv[:n_x]]
        cots = [v.astype(f32) for v in rv[n_x:n_x + n_cot]]
        adds = rv[n_x + n_cot:n_x + n_cot + n_add]
        ps = [v.astype(f32) for v in pv]
        _, vjp = jax.vjp(lambda *a: f(list(a[:n_x]), list(a[n_x:])), *xs, *ps)
        g = vjp(cots)
        row_g = [g[i] for i in grad_idx]
        for k, a in enumerate(adds):
            row_g[k] = row_g[k] + a.astype(f32)
        return row_g, list(g[n_x:])
    return fn


def _rms(x, g):
    return x * lax.rsqrt(jnp.mean(x * x, axis=-1, keepdims=True) + RMS_EPS) * g


def _f_norm(xs, ps):
    return [_rms(xs[0], ps[0])]


def _f_post(scale):
    def f(xs, ps):
        return [scale * _rms(xs[0], ps[0])]
    return f


def _f_resnorm(scale):
    def f(xs, ps):
        return [xs[0] + scale * _rms(xs[1], ps[0])]
    return f


def _f_dt(xs, ps):
    dt = jax.nn.softplus(xs[0] + ps[0])
    return [dt, -jnp.exp(ps[1]) * dt]


def _f_ssdpost(n_groups):
    def f(xs, ps):
        y = xs[0] * jax.nn.silu(xs[1])
        width = y.shape[-1] // n_groups
        lane = lax.broadcasted_iota(jnp.int32, y.shape, 1)
        scale = jnp.zeros_like(y)
        for k in range(n_groups):
            m = ((lane >= k * width) & (lane < (k + 1) * width)).astype(f32)
            ms = jnp.sum(y * y * m, axis=-1, keepdims=True) / width
            scale = scale + lax.rsqrt(ms + RMS_EPS) * m
        return [y * scale * ps[0]]
    return f


def _f_s5post(xs, ps):
    return [jax.nn.gelu(xs[0] + ps[0] * xs[1])]


def _f_merge(xs, ps):
    return [jax.nn.sigmoid(xs[0]) * xs[1] + jax.nn.sigmoid(xs[2]) * xs[3]]


def _swiglu_fwd(rv, pv):
    ab = rv[0].astype(f32)
    h = ab.shape[1] // 2
    return [jax.nn.silu(ab[:, :h]) * ab[:, h:]], []


def _swiglu_bwd(rv, pv):
    ab, d = rv[0].astype(f32), rv[1].astype(f32)
    h = ab.shape[1] // 2
    a, b = ab[:, :h], ab[:, h:]
    s = jax.nn.sigmoid(a)
    return [jnp.concatenate([d * b * (s * (1.0 + a * (1.0 - s))), d * (a * s)], axis=1)], []


def _glu_fwd(rv, pv):
    vg = rv[0].astype(f32)
    h = vg.shape[1] // 2
    return [vg[:, :h] * jax.nn.sigmoid(vg[:, h:])], []


def _glu_bwd(rv, pv):
    vg, d = rv[0].astype(f32), rv[1].astype(f32)
    h = vg.shape[1] // 2
    s = jax.nn.sigmoid(vg[:, h:])
    return [jnp.concatenate([d * s, d * vg[:, :h] * s * (1.0 - s)], axis=1)], []


def _loss_fn(rv, pv):
    e = rv[0].astype(f32) - rv[1].astype(f32)
    per_tok = jnp.mean(e * e, axis=-1, keepdims=True)
    part = 0.5 * jnp.sum(per_tok, axis=0, keepdims=True)
    return [e / e.shape[-1]], [jnp.broadcast_to(part, (8, LANES))]


def _add_fn(rv, pv):
    return [rv[0].astype(f32) + rv[1].astype(f32)], []


def _adamw_fn(rv, pv):
    w, g, m, v = [x.astype(f32) for x in rv]
    m = ADAM_B1 * m + (1.0 - ADAM_B1) * g
    v = ADAM_B2 * v + (1.0 - ADAM_B2) * (g * g)
    m_hat = m / (1.0 - ADAM_B1 ** ADAM_STEP)
    v_hat = v / (1.0 - ADAM_B2 ** ADAM_STEP)
    return [-ADAM_LR * (m_hat / (jnp.sqrt(v_hat) + ADAM_EPS) + ADAM_WD * w), m, v], []


def _adamw(w, g, m, v, name):
    shape = w.shape
    cols = shape[-1] if (w.ndim >= 2 and shape[-1] >= LANES) else None
    if cols is None:
        n = int(np.prod(shape))
        cols = LANES if n % LANES == 0 else n
    n_rows = int(np.prod(shape)) // cols
    br, bc = _pick(n_rows, 256, 8), cols
    if br < 64 and cols % LANES == 0:
        br, bc = n_rows, LANES

    def body(w_ref, g_ref, m_ref, v_ref, d_ref, nm_ref, nv_ref):
        outs, _ = _adamw_fn([w_ref[...], g_ref[...], m_ref[...], v_ref[...]], [])
        d_ref[...], nm_ref[...], nv_ref[...] = outs

    spec = pl.BlockSpec((br, bc), lambda i, j: (i, j))
    outs = pl.pallas_call(
        body, name=name, grid=(n_rows // br, cols // bc), in_specs=[spec] * 4, out_specs=[spec] * 3,
        out_shape=[jax.ShapeDtypeStruct((n_rows, cols), f32)] * 3, compiler_params=_params(("parallel", "parallel")),
    )(*[t.reshape(n_rows, cols) for t in (w, g, m, v)])
    return [o.reshape(shape) for o in outs]


def _shift_down(x, s, row):
    if s == 0:
        return x
    return jnp.where(row >= s, pltpu.roll(x, s, 0), 0.0)


def _shift_up(x, s, row):
    if s == 0:
        return x
    n = x.shape[0]
    return jnp.where(row < n - s, pltpu.roll(x, n - s, 0), 0.0)


def _conv_pre(x, w, b, row):
    kw = w.shape[0]
    c = b
    for k in range(kw):
        c = c + w[k:k + 1, :] * _shift_down(x, kw - 1 - k, row)
    return c


def _conv_fwd(xsrc, col0, w, b, name, bc_t=512):
    T = xsrc.shape[0]
    kw, ncols = w.shape
    bc = _pick(ncols, bc_t)
    off = col0 // bc
    assert col0 % bc == 0

    def body(x_ref, w_ref, b_ref, o_ref):
        x = x_ref[...].astype(f32)
        row = lax.broadcasted_iota(jnp.int32, x.shape, 0)
        c = _conv_pre(x, w_ref[...], b_ref[...], row)
        o_ref[...] = c * jax.nn.sigmoid(c)

    return pl.pallas_call(
        body, name=name, grid=(ncols // bc,),
        in_specs=[pl.BlockSpec((T, bc), lambda j: (0, off + j)), pl.BlockSpec((kw, bc), lambda j: (0, j)),
                  pl.BlockSpec((1, bc), lambda j: (0, j))],
        out_specs=pl.BlockSpec((T, bc), lambda j: (0, j)), out_shape=jax.ShapeDtypeStruct((T, ncols), f32),
        compiler_params=_params(("parallel",)),
    )(xsrc, w, b)


def _conv_bwd(xsrc, col0, w, b, dact, name, bc_t=512):
    T = xsrc.shape[0]
    kw, ncols = w.shape
    bc = _pick(ncols, bc_t)
    off = col0 // bc
    assert col0 % bc == 0

    def body(x_ref, w_ref, b_ref, d_ref, dx_ref, dw_ref, db_ref):
        x = x_ref[...].astype(f32)
        w = w_ref[...]
        row = lax.broadcasted_iota(jnp.int32, x.shape, 0)
        c = _conv_pre(x, w, b_ref[...], row)
        s = jax.nn.sigmoid(c)
        dc = d_ref[...].astype(f32) * (s * (1.0 + c * (1.0 - s)))
        dx = jnp.zeros_like(x)
        dws = []
        for k in range(kw):
            dx = dx + w[k:k + 1, :] * _shift_up(dc, kw - 1 - k, row)
            dws.append(jnp.sum(dc * _shift_down(x, kw - 1 - k, row), axis=0, keepdims=True))
        dx_ref[...] = dx.astype(dx_ref.dtype)
        dw_ref[...] = jnp.concatenate(dws, axis=0)
        db_ref[...] = jnp.sum(dc, axis=0, keepdims=True)

    return pl.pallas_call(
        body, name=name, grid=(ncols // bc,),
        in_specs=[pl.BlockSpec((T, bc), lambda j: (0, off + j)), pl.BlockSpec((kw, bc), lambda j: (0, j)),
                  pl.BlockSpec((1, bc), lambda j: (0, j)), pl.BlockSpec((T, bc), lambda j: (0, j))],
        out_specs=[pl.BlockSpec((T, bc), lambda j: (0, j)), pl.BlockSpec((kw, bc), lambda j: (0, j)),
                   pl.BlockSpec((1, bc), lambda j: (0, j))],
        out_shape=[jax.ShapeDtypeStruct((T, ncols), bf16), jax.ShapeDtypeStruct((kw, ncols), f32),
                   jax.ShapeDtypeStruct((1, ncols), f32)],
        compiler_params=_params(("parallel",)),
    )(xsrc, w, b, dact)


_HI = lax.Precision.HIGHEST


def _dot(a, b, dims="nn", precision=None):
    return lax.dot_general(a, b, _DIMS[dims], preferred_element_type=f32, precision=precision)


def _dot01(a, b, dims="nn", ones="b"):
    x = a if ones == "b" else b
    hi = x.astype(bf16)
    rest = x - hi.astype(f32)
    mid = rest.astype(bf16)
    lo = (rest - mid.astype(f32)).astype(bf16)
    if ones == "b":
        e = b.astype(bf16)
        return _dot(hi, e, dims) + _dot(mid, e, dims) + _dot(lo, e, dims)
    e = a.astype(bf16)
    return _dot(e, hi, dims) + _dot(e, mid, dims) + _dot(e, lo, dims)


def _ssd_common(x_ref, b_ref, c_ref, dt_ref, adt_ref, d_ref, hpg, p):
    q = b_ref.shape[0]
    hp = hpg * p
    bb, cb = b_ref[...].astype(bf16), c_ref[...].astype(bf16)
    r = lax.broadcasted_iota(jnp.int32, (q, q), 0)
    s = lax.broadcasted_iota(jnp.int32, (q, q), 1)
    tril = r >= s
    trilf = tril.astype(f32)
    eh = lax.broadcasted_iota(jnp.int32, (LANES, hp), 0)
    ec = lax.broadcasted_iota(jnp.int32, (LANES, hp), 1)
    expand = ((ec >= eh * p) & (ec < (eh + 1) * p)).astype(f32)
    adt = adt_ref[...]
    cum = _dot01(trilf, adt, "nn", "a")
    cum_t = _dot01(adt, (r <= s).astype(f32), "tn")
    cum_e = _dot01(cum, expand)
    dt_e = _dot01(dt_ref[...], expand)
    d_e = _dot01(jnp.broadcast_to(d_ref[...], (8, LANES)), expand)[0:1, :]
    gmat = _dot(cb, bb, "nt")
    x = x_ref[...]
    xdt = x * dt_e
    e_all = jnp.exp(cum_e)
    dec = jnp.exp(cum_e[q - 1:q, :] - cum_e)
    lms, ms = [], []
    for h in range(hpg):
        lm = jnp.exp(jnp.where(tril, cum[:, h:h + 1] - cum_t[h:h + 1, :], -1e30))
        lms.append(lm)
        ms.append(gmat * lm)
    et = [jnp.exp(cum[q - 1:q, h:h + 1]) for h in range(hpg)]
    return dict(bb=bb, cb=cb, trilf=trilf, expand=expand, cum=cum, x=x, xdt=xdt, dt_e=dt_e, d_e=d_e, e=e_all, dec=dec,
                lms=lms, ms=ms, et=et)


def _ssd_specs(q, hp, n, g_n, nc, rev):
    def cidx(c):
        return (nc - 1 - c) if rev else c
    x_spec = pl.BlockSpec((q, hp), lambda g, c: (cidx(c), g))
    boff = (g_n * hp) // n
    b_spec = pl.BlockSpec((q, n), lambda g, c: (cidx(c), boff + g))
    c_spec = pl.BlockSpec((q, n), lambda g, c: (cidx(c), boff + g_n + g))
    dt_spec = pl.BlockSpec((q, LANES), lambda g, c: (cidx(c), g))
    d_spec = pl.BlockSpec((1, LANES), lambda g, c: (0, g))
    st_spec = pl.BlockSpec((1, 1, hp, n), lambda g, c: (cidx(c), g, 0, 0))
    return x_spec, b_spec, c_spec, dt_spec, d_spec, st_spec


def _ssd_fwd(act, dt, adt, dpad, hpg, p, n, name):
    T = act.shape[0]
    g_n, q = SSD_N_GROUPS, SSD_CHUNK
    nc, hp = T // q, hpg * p
    x_spec, b_spec, c_spec, dt_spec, d_spec, st_spec = _ssd_specs(q, hp, n, g_n, nc, False)

    def body(x_ref, b_ref, c_ref, dt_ref, adt_ref, d_ref, y_ref, st_ref, s_scr):
        @pl.when(pl.program_id(1) == 0)
        def _():
            s_scr[...] = jnp.zeros_like(s_scr)

        k = _ssd_common(x_ref, b_ref, c_ref, dt_ref, adt_ref, d_ref, hpg, p)
        s0 = s_scr[...]
        st_ref[0, 0] = s0
        xdtb = k["xdt"].astype(bf16)
        ydiag = [_dot(k["ms"][h].astype(bf16), xdtb[:, h * p:(h + 1) * p]) for h in range(hpg)]
        z = _dot(k["cb"], s0.astype(bf16), "nt")
        y_ref[...] = jnp.concatenate(ydiag, axis=1) + k["e"] * z + k["d_e"] * k["x"]
        upd = _dot((k["xdt"] * k["dec"]).astype(bf16), k["bb"], "tn")
        for h in range(hpg):
            s_scr[h * p:(h + 1) * p, :] = k["et"][h] * s0[h * p:(h + 1) * p, :] + upd[h * p:(h + 1) * p, :]

    return pl.pallas_call(
        body, name=name, grid=(g_n, nc),
        in_specs=[x_spec, b_spec, c_spec, dt_spec, dt_spec, d_spec],
        out_specs=[pl.BlockSpec((q, hp), lambda g, c: (c, g)), st_spec],
        out_shape=[jax.ShapeDtypeStruct((T, g_n * hp), f32), jax.ShapeDtypeStruct((nc, g_n, hp, n), f32)],
        scratch_shapes=[pltpu.VMEM((hp, n), f32)],
        compiler_params=_params(("parallel", "arbitrary")),
    )(act, act, act, dt, adt, dpad)


def _ssd_bwd(act, dt, adt, dpad, states, dy, hpg, p, n, name):
    T = act.shape[0]
    g_n, q = SSD_N_GROUPS, SSD_CHUNK
    nc, hp = T // q, hpg * p
    x_spec, b_spec, c_spec, dt_spec, d_spec, st_spec = _ssd_specs(q, hp, n, g_n, nc, True)

    def body(x_ref, b_ref, c_ref, dt_ref, adt_ref, d_ref, st_ref, dy_ref,
             dx_ref, db_ref, dc_ref, ddt_ref, dadt_ref, dd_ref, ds_scr):
        first = pl.program_id(1) == 0

        @pl.when(first)
        def _():
            ds_scr[...] = jnp.zeros_like(ds_scr)

        k = _ssd_common(x_ref, b_ref, c_ref, dt_ref, adt_ref, d_ref, hpg, p)
        bb, cb, expand, x, xdt, dec = k["bb"], k["cb"], k["expand"], k["x"], k["xdt"], k["dec"]
        heads = lambda t: _dot01(t, expand, "nt")
        s0 = st_ref[0, 0]
        ds1 = ds_scr[...]
        s0b, ds1b = s0.astype(bf16), ds1.astype(bf16)
        dy = dy_ref[...]
        dyb, xdtb = dy.astype(bf16), xdt.astype(bf16)
        lane = lax.broadcasted_iota(jnp.int32, (1, LANES), 1)
        dg = jnp.zeros((q, q), f32)
        w_rows = jnp.zeros((q, LANES), f32)
        w_cols, dxdt_parts = [], []
        for h in range(hpg):
            hs = slice(h * p, (h + 1) * p)
            dm = _dot(dyb[:, hs], xdtb[:, hs], "nt")
            dg = dg + dm * k["lms"][h]
            wm = dm * k["ms"][h]
            w_rows = w_rows + jnp.sum(wm, axis=1, keepdims=True) * (lane == h).astype(f32)
            w_cols.append(jnp.sum(wm, axis=0, keepdims=True))
            dxdt_parts.append(_dot(k["ms"][h].astype(bf16), dyb[:, hs], "tn"))
        dxdt_diag = jnp.concatenate(dxdt_parts, axis=1)
        w_cols = jnp.concatenate(w_cols + [jnp.zeros((LANES - hpg, q), f32)], axis=0).T
        dgb = dg.astype(bf16)
        z = _dot(cb, s0b, "nt")
        dz = dy * k["e"]
        dzb = dz.astype(bf16)
        dxd = _dot(bb, ds1b, "nt")
        ddec = dxd * xdt * dec
        db_ref[...] = _dot(dgb, cb, "tn") + _dot((xdt * dec).astype(bf16), ds1b)
        dc_ref[...] = _dot(dgb, bb) + _dot(dzb, s0b)
        ds0 = _dot(dzb, cb, "tn")
        for h in range(hpg):
            hs = slice(h * p, (h + 1) * p)
            ds_scr[hs, :] = ds0[hs, :] + k["et"][h] * ds1[hs, :]
        dxdt = dxdt_diag + dxd * dec
        ddec_h = heads(ddec)
        dcum = w_rows - w_cols + heads(dz * z) - ddec_h
        et_row = jnp.exp(k["cum"][q - 1:q, :])
        dsum = _dot01(jnp.ones((8, n), f32), _dot01(expand, ds1 * s0, "nn", "a"), "nt", "a")[0:1, :]
        dcl = dsum * et_row + jnp.sum(ddec_h, axis=0, keepdims=True)
        rowq = lax.broadcasted_iota(jnp.int32, (q, 1), 0)
        dcum = dcum + (rowq == q - 1).astype(f32) * dcl
        ddt_ref[...] = heads(dxdt * x)
        dadt_ref[...] = _dot01(k["trilf"], dcum, "tn", "a")
        dx_ref[...] = k["d_e"] * dy + dxdt * k["dt_e"]
        dd8 = heads(jnp.broadcast_to(jnp.sum(dy * x, axis=0, keepdims=True), (8, hp)))

        @pl.when(first)
        def _():
            dd_ref[...] = dd8

        @pl.when(jnp.logical_not(first))
        def _():
            dd_ref[...] += dd8

    rc = lambda g, c: (nc - 1 - c, g)
    return pl.pallas_call(
        body, name=name, grid=(g_n, nc),
        in_specs=[x_spec, b_spec, c_spec, dt_spec, dt_spec, d_spec, st_spec, pl.BlockSpec((q, hp), rc)],
        out_specs=[pl.BlockSpec((q, hp), rc), pl.BlockSpec((q, n), rc), pl.BlockSpec((q, n), rc),
                   pl.BlockSpec((q, LANES), rc), pl.BlockSpec((q, LANES), rc), pl.BlockSpec((8, LANES), lambda g, c: (g, 0))],
        out_shape=[jax.ShapeDtypeStruct((T, g_n * hp), f32), jax.ShapeDtypeStruct((T, g_n * n), f32),
                   jax.ShapeDtypeStruct((T, g_n * n), f32), jax.ShapeDtypeStruct((T, g_n * LANES), f32),
                   jax.ShapeDtypeStruct((T, g_n * LANES), f32), jax.ShapeDtypeStruct((g_n * 8, LANES), f32)],
        scratch_shapes=[pltpu.VMEM((hp, n), f32)],
        compiler_params=_params(("parallel", "arbitrary")),
    )(act, act, act, dt, adt, dpad, states, dy)


def _cmul(ar, ai, br, bi):
    return ar * br - ai * bi, ar * bi + ai * br


def _s5_tile_powers(lr, li):
    p = [(lr, li)]
    for _ in range(7):
        p.append(_cmul(p[-1][0], p[-1][1], lr, li))
    tile = (jnp.concatenate([q[0] for q in p], axis=0), jnp.concatenate([q[1] for q in p], axis=0))
    return tile, (p[0], p[1], p[3])


def _s5_tile_scan(xr, xi, steps, reverse):
    row = lax.broadcasted_iota(jnp.int32, xr.shape, 0)
    for d, (pr, pi) in zip((1, 2, 4), steps):
        if reverse:
            keep = row < 8 - d
            sr, si = pltpu.roll(xr, 8 - d, 0), pltpu.roll(xi, 8 - d, 0)
        else:
            keep = row >= d
            sr, si = pltpu.roll(xr, d, 0), pltpu.roll(xi, d, 0)
        sr, si = jnp.where(keep, sr, 0.0), jnp.where(keep, si, 0.0)
        ar, ai = _cmul(sr, si, pr, pi)
        xr, xi = xr + ar, xi + ai
    return xr, xi


def _s5_scan_fwd(bu, lam_re, lam_im, nsb, name, tc_t=512):
    T = bu.shape[0]
    w2 = bu.shape[1] // nsb
    w = w2 // 2
    tc = _pick(T, tc_t, 8)

    def body(bu_ref, lr_ref, li_ref, st_ref, carry):
        @pl.when(pl.program_id(1) == 0)
        def _():
            carry[...] = jnp.zeros_like(carry)

        (pr8, pi8), steps = _s5_tile_powers(lr_ref[0:1, :], li_ref[0:1, :])

        def tile(i, c):
            r = pl.ds(pl.multiple_of(i * 8, 8), 8)
            x = bu_ref[r, :]
            xr, xi = _s5_tile_scan(x[:, :w], x[:, w:], steps, False)
            ar, ai = _cmul(pr8, pi8, c[0], c[1])
            xr, xi = xr + ar, xi + ai
            st_ref[r, :] = jnp.concatenate([xr, xi], axis=1)
            return xr[7:8, :], xi[7:8, :]

        c = lax.fori_loop(0, tc // 8, tile, (carry[0:1, :], carry[1:2, :]), unroll=2)
        carry[0:1, :] = c[0]
        carry[1:2, :] = c[1]

    return pl.pallas_call(
        body, name=name, grid=(nsb, T // tc),
        in_specs=[pl.BlockSpec((tc, w2), lambda j, i: (i, j)), pl.BlockSpec((8, w), lambda j, i: (j, 0)),
                  pl.BlockSpec((8, w), lambda j, i: (j, 0))],
        out_specs=pl.BlockSpec((tc, w2), lambda j, i: (i, j)), out_shape=jax.ShapeDtypeStruct(bu.shape, f32),
        scratch_shapes=[pltpu.VMEM((8, w), f32)],
        compiler_params=_params(("parallel", "arbitrary")),
    )(bu, lam_re, lam_im)


def _s5_scan_bwd(gst, states, lam_re, lam_im, nsb, name, tc_t=512):
    T = gst.shape[0]
    w2 = gst.shape[1] // nsb
    w = w2 // 2
    tc = _pick(T, tc_t, 8)
    nt = T // tc
    n_tiles = tc // 8

    def body(g_ref, s_ref, sp_ref, lr_ref, li_ref, a_ref, dlr_ref, dli_ref, carry, acc):
        chunk = pl.program_id(1)

        @pl.when(chunk == 0)
        def _():
            carry[...] = jnp.zeros_like(carry)
            acc[...] = jnp.zeros_like(acc)

        (qr8, qi8), steps = _s5_tile_powers(lr_ref[0:1, :], -li_ref[0:1, :])
        row = lax.broadcasted_iota(jnp.int32, (8, w), 0)
        rev_r, rev_i = jnp.zeros((8, w), f32), jnp.zeros((8, w), f32)
        for r in range(8):
            rev_r = jnp.where(row == r, qr8[7 - r:8 - r, :], rev_r)
            rev_i = jnp.where(row == r, qi8[7 - r:8 - r, :], rev_i)
        row2 = lax.broadcasted_iota(jnp.int32, (8, w2), 0)

        def tile(k, c):
            ar_in, ai_in, dr, di = c
            i = n_tiles - 1 - k
            r = pl.ds(pl.multiple_of(i * 8, 8), 8)
            x = g_ref[r, :]
            xr, xi = _s5_tile_scan(x[:, :w], x[:, w:], steps, True)
            pr, pi = _cmul(rev_r, rev_i, ar_in, ai_in)
            xr, xi = xr + pr, xi + pi
            a_ref[r, :] = jnp.concatenate([xr, xi], axis=1)
            before = jnp.where(i > 0, s_ref[pl.ds(pl.multiple_of(jnp.maximum(i - 1, 0) * 8, 8), 8), :],
                               sp_ref[tc - 8:tc, :] * (chunk < nt - 1).astype(f32))
            prev = jnp.where(row2 == 0, pltpu.roll(before, 1, 0), pltpu.roll(s_ref[r, :], 1, 0))
            spr, spi = prev[:, :w], prev[:, w:]
            return xr[0:1, :], xi[0:1, :], dr + xr * spr + xi * spi, di - xr * spi + xi * spr

        c0 = (carry[0:1, :], carry[1:2, :], acc[0:8, :], acc[8:16, :])
        ar, ai, dr, di = lax.fori_loop(0, n_tiles, tile, c0, unroll=2)
        carry[0:1, :] = ar
        carry[1:2, :] = ai
        acc[0:8, :] = dr
        acc[8:16, :] = di
        dlr_ref[...] = jnp.broadcast_to(jnp.sum(dr, axis=0, keepdims=True), (8, w))
        dli_ref[...] = jnp.broadcast_to(jnp.sum(di, axis=0, keepdims=True), (8, w))

    cur = lambda j, i: (nt - 1 - i, j)
    prv = lambda j, i: (jnp.maximum(nt - 2 - i, 0), j)
    return pl.pallas_call(
        body, name=name, grid=(nsb, nt),
        in_specs=[pl.BlockSpec((tc, w2), cur), pl.BlockSpec((tc, w2), cur), pl.BlockSpec((tc, w2), prv),
                  pl.BlockSpec((8, w), lambda j, i: (j, 0)), pl.BlockSpec((8, w), lambda j, i: (j, 0))],
        out_specs=[pl.BlockSpec((tc, w2), cur), pl.BlockSpec((8, w), lambda j, i: (j, 0)),
                   pl.BlockSpec((8, w), lambda j, i: (j, 0))],
        out_shape=[jax.ShapeDtypeStruct(gst.shape, f32), jax.ShapeDtypeStruct((nsb * 8, w), f32),
                   jax.ShapeDtypeStruct((nsb * 8, w), f32)],
        scratch_shapes=[pltpu.VMEM((8, w), f32), pltpu.VMEM((16, w), f32)],
        compiler_params=_params(("parallel", "arbitrary")),
    )(gst, states, states, lam_re, lam_im)


def _s5_prep_fn(xs, ps):
    lam_re, lam_im, log_step, b_re, b_im, expand = ps
    lr = jnp.minimum(lam_re, S5_MAX_REAL)
    li = lam_im
    step = jnp.exp(log_step)
    er = jnp.exp(lr * step)
    ang = li * step
    lbr, lbi = er * jnp.cos(ang), er * jnp.sin(ang)
    nr, ni = lbr - 1.0, lbi
    den = lr * lr + li * li
    qr, qi = (nr * lr + ni * li) / den, (ni * lr - nr * li) / den
    qre, qie = _dot(qr, expand, "nn", _HI), _dot(qi, expand, "nn", _HI)
    return [lbr, lbi, qre * b_re - qie * b_im, qre * b_im + qie * b_re]


def _s5_prep(pars, name):
    def body(*refs):
        outs = _s5_prep_fn([], [r[...] for r in refs[:6]])
        for ref, v in zip(refs[6:], outs):
            ref[...] = v

    g, nst = pars[0].shape
    nc = pars[3].shape[1]
    return pl.pallas_call(
        body, name=name,
        out_shape=[jax.ShapeDtypeStruct((g, nst), f32)] * 2 + [jax.ShapeDtypeStruct((g, nc), f32)] * 2,
        compiler_params=_params(),
    )(*pars)


def _s5_prep_bwd(pars, cots, name):
    def body(*refs):
        ps = [r[...] for r in refs[:6]]
        ct = [r[...] for r in refs[6:10]]
        _, vjp = jax.vjp(lambda *a: _s5_prep_fn([], list(a)), *ps)
        g = vjp(ct)
        for ref, v in zip(refs[10:], g[:5]):
            ref[...] = v

    return pl.pallas_call(
        body, name=name, out_shape=[jax.ShapeDtypeStruct(p.shape, f32) for p in pars[:5]], compiler_params=_params(),
    )(*pars, *cots)


_ANY = pl.BlockSpec(memory_space=pl.ANY)


def _remote(src, dst, send_sem, recv_sem, device):
    return pltpu.make_async_remote_copy(src_ref=src, dst_ref=dst, send_sem=send_sem, recv_sem=recv_sem, device_id=device,
                                        device_id_type=MESH)


def _staged_copy(src, dst, buf, in_sems, out_sems):
    n = D2D_STREAMS
    piece = src.shape[0] // n
    assert src.shape[0] % n == 0

    def load(i):
        return pltpu.make_async_copy(src.at[pl.ds(i * piece, piece)], buf.at[i % 2], in_sems.at[i % 2])

    def store(i):
        return pltpu.make_async_copy(buf.at[i % 2], dst.at[pl.ds(i * piece, piece)], out_sems.at[i % 2])

    load(0).start()
    for i in range(n):
        if i + 1 < n:
            if i >= 1:
                store(i - 1).wait()
            load(i + 1).start()
        load(i).wait()
        store(i).start()
    store(n - 2).wait()
    store(n - 1).wait()


def _stage_scratch(rows, cols, dtype):
    return [pltpu.VMEM((2, rows // D2D_STREAMS, cols), dtype), pltpu.SemaphoreType.DMA((2,)), pltpu.SemaphoreType.DMA((2,))]


def _chip_all_gather(block, name):
    rows = block.shape[0]
    half = rows // 2
    piece = half // D2D_STREAMS
    assert rows % (2 * D2D_STREAMS * 16) == 0

    def body(src, out, ici_send, ici_recv, d2d_send, d2d_recv, *stage):
        x, y, c = lax.axis_index("x"), lax.axis_index("y"), lax.axis_index("c")
        me = 2 * x + y
        sibling = (x, y, 1 - c)
        chips = [(1 - x, y), (x, 1 - y), (1 - x, 1 - y)]
        mine = pl.ds(pl.multiple_of(c * half, 16), half)
        sends = []
        for j, (px, py) in enumerate(chips):
            cp = _remote(src.at[mine], out.at[me, mine], ici_send.at[j], ici_recv.at[j], (px, py, c))
            cp.start()
            sends.append(cp)
        _staged_copy(src, out.at[me], *stage)
        for j, (px, py) in enumerate(chips):
            slot = 2 * px + py
            _remote(src.at[mine], out.at[slot, mine], ici_send.at[j], ici_recv.at[j], (px, py, c)).wait_recv()
            for s in range(D2D_STREAMS):
                r = pl.ds(pl.multiple_of(c * half + s * piece, 16), piece)
                k = j * D2D_STREAMS + s
                cp = _remote(out.at[slot, r], out.at[slot, r], d2d_send.at[k], d2d_recv.at[k], sibling)
                cp.start()
                sends.append(cp)
        for j, (px, py) in enumerate(chips):
            slot = 2 * px + py
            for s in range(D2D_STREAMS):
                r = pl.ds(pl.multiple_of((1 - c) * half + s * piece, 16), piece)
                k = j * D2D_STREAMS + s
                _remote(out.at[slot, r], out.at[slot, r], d2d_send.at[k], d2d_recv.at[k], sibling).wait_recv()
        for cp in sends:
            cp.wait_send()

    n_d2d = 3 * D2D_STREAMS
    return pl.pallas_call(
        body, name=name, in_specs=[_ANY], out_specs=_ANY,
        out_shape=jax.ShapeDtypeStruct((N_CHIPS,) + block.shape, block.dtype),
        scratch_shapes=[pltpu.SemaphoreType.DMA((3,)), pltpu.SemaphoreType.DMA((3,)), pltpu.SemaphoreType.DMA((n_d2d,)),
                        pltpu.SemaphoreType.DMA((n_d2d,))] + _stage_scratch(rows, block.shape[1], block.dtype),
    )(block)


def _chip_scatter(parts, name):
    def body(src, out, send_sems, recv_sems, *stage):
        x, y, c = lax.axis_index("x"), lax.axis_index("y"), lax.axis_index("c")
        me = 2 * x + y
        chips = [(1 - x, y), (x, 1 - y), (1 - x, 1 - y)]
        sends = []
        for j, (px, py) in enumerate(chips):
            cp = pltpu.make_async_remote_copy(src_ref=src.at[2 * px + py], dst_ref=out.at[me], send_sem=send_sems.at[j],
                                              recv_sem=recv_sems.at[j], device_id=(px, py, c), device_id_type=MESH)
            cp.start()
            sends.append(cp)
        _staged_copy(src.at[me], out.at[me], *stage)
        for j, (px, py) in enumerate(chips):
            pltpu.make_async_remote_copy(src_ref=src.at[me], dst_ref=out.at[2 * px + py], send_sem=send_sems.at[j],
                                         recv_sem=recv_sems.at[j], device_id=(px, py, c), device_id_type=MESH).wait_recv()
        for cp in sends:
            cp.wait_send()

    return pl.pallas_call(
        body, name=name, in_specs=[_ANY], out_specs=_ANY, out_shape=jax.ShapeDtypeStruct(parts.shape, parts.dtype),
        scratch_shapes=[pltpu.SemaphoreType.DMA((3,)), pltpu.SemaphoreType.DMA((3,))]
        + _stage_scratch(parts.shape[1], parts.shape[2], parts.dtype),
    )(parts)


_HBM = pl.BlockSpec(memory_space=pltpu.HBM)
_SEM = pl.BlockSpec(memory_space=pltpu.SEMAPHORE)
_EFFECT = pltpu.SideEffectType.DATAFLOW_SIDE_EFFECTING


def _gather_peers():
    x, y, c = lax.axis_index("x"), lax.axis_index("y"), lax.axis_index("c")
    return x, y, c, 2 * x + y, [(1 - x, y), (x, 1 - y), (1 - x, 1 - y)]


def _chip_gather_start(block, name):
    half = block.shape[0] // 2

    def body(src, land, send_sems, recv_sems, src_out, land_out, token):
        x, y, c, me, chips = _gather_peers()
        mine = pl.ds(pl.multiple_of(c * half, 16), half)
        for j, (px, py) in enumerate(chips):
            _remote(src.at[mine], land.at[me, mine], send_sems.at[j], recv_sems.at[j], (px, py, c)).start()
        token[...] = jnp.zeros_like(token)

    land_shape = (N_CHIPS,) + block.shape
    return pl.pallas_call(
        body, name=name,
        out_shape=(pltpu.SemaphoreType.DMA((3,)), pltpu.SemaphoreType.DMA((3,)), pltpu.HBM(block.shape, block.dtype),
                   pltpu.HBM(land_shape, block.dtype), jax.ShapeDtypeStruct((8, LANES), f32)),
        in_specs=(_HBM, _HBM), out_specs=(_SEM, _SEM, _HBM, _HBM, pl.BlockSpec(memory_space=pltpu.VMEM)),
        input_output_aliases={0: 2, 1: 3}, compiler_params=pltpu.CompilerParams(has_side_effects=_EFFECT),
    )(pltpu.with_memory_space_constraint(block, pltpu.HBM),
      pltpu.with_memory_space_constraint(lax.empty(land_shape, block.dtype), pltpu.HBM))


def _chip_gather_wait(send_sems, recv_sems, block, land, after, name):
    half = block.shape[0] // 2

    def body(src, land_ref, send_ref, recv_ref, after_ref, src_dead, land_out):
        x, y, c, me, chips = _gather_peers()
        mine = pl.ds(pl.multiple_of(c * half, 16), half)
        for j, (px, py) in enumerate(chips):
            cp = _remote(src.at[mine], land_ref.at[2 * px + py, mine], send_ref.at[j], recv_ref.at[j], (px, py, c))
            cp.wait_send()
            cp.wait_recv()

    return pl.pallas_call(
        body, name=name, out_shape=(pltpu.HBM(block.shape, block.dtype), pltpu.HBM(land.shape, land.dtype)),
        in_specs=(_HBM, _HBM, _SEM, _SEM, _ANY), out_specs=(_HBM, _HBM), input_output_aliases={0: 0, 1: 1},
        compiler_params=pltpu.CompilerParams(has_side_effects=_EFFECT),
    )(block, land, send_sems, recv_sems, after)


def _chip_gather_finish(block, land, name):
    rows = block.shape[0]
    half = rows // 2
    piece = half // D2D_STREAMS

    def body(src, land_ref, out, d2d_send, d2d_recv, *stage):
        x, y, c, me, chips = _gather_peers()
        sibling = (x, y, 1 - c)
        sends = []
        for j, (px, py) in enumerate(chips):
            slot = 2 * px + py
            for s in range(D2D_STREAMS):
                r = pl.ds(pl.multiple_of(c * half + s * piece, 16), piece)
                k = j * D2D_STREAMS + s
                cp = _remote(land_ref.at[slot, r], out.at[slot, r], d2d_send.at[k], d2d_recv.at[k], sibling)
                cp.start()
                sends.append(cp)
        _staged_copy(src, out.at[me], *stage)
        for j, (px, py) in enumerate(chips):
            slot = 2 * px + py
            for s in range(D2D_STREAMS):
                r = pl.ds(pl.multiple_of((1 - c) * half + s * piece, 16), piece)
                k = j * D2D_STREAMS + s
                _remote(land_ref.at[slot, r], out.at[slot, r], d2d_send.at[k], d2d_recv.at[k], sibling).wait_recv()
        for cp in sends:
            cp.wait_send()

    n_d2d = 3 * D2D_STREAMS
    return pl.pallas_call(
        body, name=name, in_specs=[_ANY, _ANY], out_specs=_ANY, out_shape=jax.ShapeDtypeStruct(land.shape, land.dtype),
        input_output_aliases={1: 0},
        scratch_shapes=[pltpu.SemaphoreType.DMA((n_d2d,)), pltpu.SemaphoreType.DMA((n_d2d,))]
        + _stage_scratch(rows, block.shape[1], block.dtype),
    )(block, land)


def _chip_scatter_start(parts, name):
    def body(src, land, send_sems, recv_sems, src_out, land_out, token):
        x, y, c = lax.axis_index("x"), lax.axis_index("y"), lax.axis_index("c")
        me = 2 * x + y
        for j, (px, py) in enumerate([(1 - x, y), (x, 1 - y), (1 - x, 1 - y)]):
            _remote(src.at[2 * px + py], land.at[me], send_sems.at[j], recv_sems.at[j], (px, py, c)).start()
        token[...] = jnp.zeros_like(token)

    return pl.pallas_call(
        body, name=name,
        out_shape=(pltpu.SemaphoreType.DMA((3,)), pltpu.SemaphoreType.DMA((3,)), pltpu.HBM(parts.shape, parts.dtype),
                   pltpu.HBM(parts.shape, parts.dtype), jax.ShapeDtypeStruct((8, LANES), f32)),
        in_specs=(_HBM, _HBM), out_specs=(_SEM, _SEM, _HBM, _HBM, pl.BlockSpec(memory_space=pltpu.VMEM)),
        input_output_aliases={0: 2, 1: 3}, compiler_params=pltpu.CompilerParams(has_side_effects=_EFFECT),
    )(pltpu.with_memory_space_constraint(parts, pltpu.HBM),
      pltpu.with_memory_space_constraint(lax.empty(parts.shape, parts.dtype), pltpu.HBM))


def _chip_scatter_wait(send_sems, recv_sems, parts, land, after, name):
    def body(src, land_ref, send_ref, recv_ref, after_ref, src_dead, land_out):
        x, y, c = lax.axis_index("x"), lax.axis_index("y"), lax.axis_index("c")
        me = 2 * x + y
        for j, (px, py) in enumerate([(1 - x, y), (x, 1 - y), (1 - x, 1 - y)]):
            cp = _remote(src.at[2 * px + py], land_ref.at[2 * px + py], send_ref.at[j], recv_ref.at[j], (px, py, c))
            cp.wait_send()
            cp.wait_recv()

    return pl.pallas_call(
        body, name=name, out_shape=(pltpu.HBM(parts.shape, parts.dtype), pltpu.HBM(land.shape, land.dtype)),
        in_specs=(_HBM, _HBM, _SEM, _SEM, _ANY), out_specs=(_HBM, _HBM), input_output_aliases={0: 0, 1: 1},
        compiler_params=pltpu.CompilerParams(has_side_effects=_EFFECT),
    )(parts, land, send_sems, recv_sems, after)


def _sum_slots_own(landed, own, chip, name, block_rows=256):
    s_n, r_n, c_n = landed.shape
    br = _pick(r_n, block_rows, 8)

    def body(chip_ref, land_ref, own_ref, o_ref):
        acc = jnp.zeros((br, c_n), f32)
        for s in range(s_n):
            acc = acc + jnp.where(chip_ref[0] == s, own_ref[s], land_ref[s]).astype(f32)
        o_ref[...] = acc

    spec = pl.BlockSpec((s_n, br, c_n), lambda i, c: (0, i, 0))
    grid_spec = pltpu.PrefetchScalarGridSpec(num_scalar_prefetch=1, grid=(r_n // br,), in_specs=[spec, spec],
                                             out_specs=pl.BlockSpec((br, c_n), lambda i, c: (i, 0)))
    return pl.pallas_call(
        body, name=name, grid_spec=grid_spec, out_shape=jax.ShapeDtypeStruct((r_n, c_n), f32),
        compiler_params=_params(("parallel",)),
    )(chip, landed, own)


def _core_send_other_half(parts, name):
    n_slots, rows, cols = parts.shape
    half = rows // 2
    piece = half // D2D_STREAMS
    assert rows % (2 * D2D_STREAMS * 16) == 0

    def body(src, out, send_sems, recv_sems):
        x, y, c = lax.axis_index("x"), lax.axis_index("y"), lax.axis_index("c")
        sibling = (x, y, 1 - c)
        sends = []
        for k in range(n_slots):
            for s in range(D2D_STREAMS):
                theirs = pl.ds(pl.multiple_of((1 - c) * half + s * piece, 16), piece)
                i = k * D2D_STREAMS + s
                cp = _remote(src.at[k, theirs], out.at[k, pl.ds(s * piece, piece)], send_sems.at[i], recv_sems.at[i], sibling)
                cp.start()
                sends.append(cp)
        for cp in sends:
            cp.wait_recv()
        for cp in sends:
            cp.wait_send()

    n = n_slots * D2D_STREAMS
    return pl.pallas_call(
        body, name=name, in_specs=[_ANY], out_specs=_ANY, out_shape=jax.ShapeDtypeStruct((n_slots, half, cols), parts.dtype),
        scratch_shapes=[pltpu.SemaphoreType.DMA((n,)), pltpu.SemaphoreType.DMA((n,))],
    )(parts)


def _add_my_half(parts, other, core, name, block_rows=256):
    n_slots, rows, cols = parts.shape
    half = rows // 2
    br = _pick(half, block_rows, 16)
    nb = half // br

    def body(c_ref, a_ref, b_ref, o_ref):
        o_ref[...] = (a_ref[...].astype(f32) + b_ref[...].astype(f32)).astype(o_ref.dtype)

    grid_spec = pltpu.PrefetchScalarGridSpec(
        num_scalar_prefetch=1, grid=(n_slots, nb),
        in_specs=[pl.BlockSpec((1, br, cols), lambda k, i, c: (k, c[0] * nb + i, 0)),
                  pl.BlockSpec((1, br, cols), lambda k, i, c: (k, i, 0))],
        out_specs=pl.BlockSpec((1, br, cols), lambda k, i, c: (k, i, 0)))
    return pl.pallas_call(
        body, name=name, grid_spec=grid_spec, out_shape=jax.ShapeDtypeStruct((n_slots, half, cols), parts.dtype),
        compiler_params=_params(("parallel", "parallel")),
    )(core, parts, other)


def _core_join_halves(mine, name):
    half, cols = mine.shape
    piece = half // D2D_STREAMS
    assert half % (D2D_STREAMS * 16) == 0

    def body(src, out, send_sems, recv_sems, *stage):
        x, y, c = lax.axis_index("x"), lax.axis_index("y"), lax.axis_index("c")
        sibling = (x, y, 1 - c)
        sends = []
        for s in range(D2D_STREAMS):
            dst = out.at[pl.ds(pl.multiple_of(c * half + s * piece, 16), piece)]
            cp = _remote(src.at[pl.ds(s * piece, piece)], dst, send_sems.at[s], recv_sems.at[s], sibling)
            cp.start()
            sends.append(cp)
        _staged_copy(src, out.at[pl.ds(pl.multiple_of(c * half, 16), half)], *stage)
        for s in range(D2D_STREAMS):
            dst = out.at[pl.ds(pl.multiple_of((1 - c) * half + s * piece, 16), piece)]
            _remote(src.at[pl.ds(s * piece, piece)], dst, send_sems.at[s], recv_sems.at[s], sibling).wait_recv()
        for cp in sends:
            cp.wait_send()

    return pl.pallas_call(
        body, name=name, in_specs=[_ANY], out_specs=_ANY, out_shape=jax.ShapeDtypeStruct((2 * half, cols), mine.dtype),
        scratch_shapes=[pltpu.SemaphoreType.DMA((D2D_STREAMS,)), pltpu.SemaphoreType.DMA((D2D_STREAMS,))]
        + _stage_scratch(half, cols, mine.dtype),
    )(mine)


def _reduce_start(parts, core, tag):
    chip_part = _add_my_half(parts, _core_send_other_half(parts, f"exchange_core_halves_{tag}"), core, f"sum_core_halves_{tag}")
    return _chip_scatter_start(chip_part, f"scatter_start_{tag}")


def _reduce_finish(started, chip, after, tag):
    send_sems, recv_sems, chip_part, land, _ = started
    own, landed = _chip_scatter_wait(send_sems, recv_sems, chip_part, land, after, f"scatter_wait_{tag}")
    return _core_join_halves(_sum_slots_own(landed, own, chip, f"sum_chip_parts_{tag}"), f"join_core_halves_{tag}")


def _reduce_to_chips(parts, core, tag):
    chip_part = _add_my_half(parts, _core_send_other_half(parts, f"exchange_core_halves_{tag}"), core, f"sum_core_halves_{tag}")
    my_sum = _sum_slots(_chip_scatter(chip_part, f"scatter_{tag}"), f"sum_chip_parts_{tag}")
    return _core_join_halves(my_sum, f"join_core_halves_{tag}")


def _sum_slots(stack, name, block_rows=256):
    s_n, r_n, c_n = stack.shape
    br = _pick(r_n, block_rows, 8)

    def body(in_ref, o_ref):
        acc = in_ref[0].astype(f32)
        for s in range(1, s_n):
            acc = acc + in_ref[s].astype(f32)
        o_ref[...] = acc

    return pl.pallas_call(
        body, name=name, grid=(r_n // br,), in_specs=[pl.BlockSpec((s_n, br, c_n), lambda i: (0, i, 0))],
        out_specs=pl.BlockSpec((br, c_n), lambda i: (i, 0)), out_shape=jax.ShapeDtypeStruct((r_n, c_n), f32),
        compiler_params=_params(("parallel",)),
    )(stack)


def _concat_padded(parts, mult):
    rows = sum(p.shape[0] for p in parts)
    pad = (-rows) % mult
    if pad:
        parts = parts + [jnp.zeros((pad,) + parts[0].shape[1:], parts[0].dtype)]
    return jnp.concatenate(parts, axis=0)


def _pack_weights(w, l, names, conv_w=None):
    parts = [w[n][l].astype(bf16).reshape(-1, PACK_COLS) for n in names]
    if conv_w is not None:
        parts.append(_concat_padded([lax.bitcast_convert_type(conv_w, bf16).reshape(-1, PACK_COLS)], 16))
    return _concat_padded(parts, PACK_ROW_MULT)


def _unpack_weights(full, w, l, names, conv_w=None):
    start, r0 = {}, 0
    for n in names:
        start[n] = r0
        r0 += w[n][l].size // PACK_COLS

    def shards(n):
        rows = w[n][l].size // PACK_COLS
        return [full[k, start[n]:start[n] + rows].reshape(w[n].shape[1:]) for k in range(N_CHIPS)]

    if conv_w is None:
        return shards, None
    rows = conv_w.size * 2 // PACK_COLS
    pieces = lax.bitcast_convert_type(full[:, r0:r0 + rows].reshape((N_CHIPS,) + conv_w.shape + (2,)), f32)
    return shards, jnp.concatenate([pieces[k] for k in range(N_CHIPS)], axis=2)


def _pack_big_grads(layer_grads):
    parts, slot_rows = [], 0
    for k in range(N_CHIPS):
        slot = []
        for n in BIG:
            for g in layer_grads:
                width = g[n].shape[0] // N_CHIPS
                slot.append(g[n][k * width:(k + 1) * width].astype(bf16).reshape(-1, PACK_COLS))
        slot_rows = sum(p.shape[0] for p in slot)
        pad = (-slot_rows) % PACK_ROW_MULT
        if pad:
            slot.append(jnp.zeros((pad, PACK_COLS), bf16))
        slot_rows += pad
        parts += slot
    return jnp.concatenate(parts, axis=0).reshape(N_CHIPS, slot_rows, PACK_COLS)


def _unpack_big_grads(summed, w):
    out, r0 = {}, 0
    for n in BIG:
        rows = w[n][0].size // PACK_COLS
        out[n] = jnp.stack([s[r0:r0 + rows].reshape(w[n].shape[1:]) for s in summed])
        r0 += rows
    return out


_SMALL_TILE = 8 * LANES


def _pack_small(vals, names):
    parts = []
    for n in names:
        pieces = vals[n] if isinstance(vals[n], list) else [vals[n]]
        size = sum(p.size for p in pieces)
        if all(p.size % _SMALL_TILE == 0 for p in pieces):
            parts += [p.reshape(-1, LANES) for p in pieces]
        else:
            flat = [p.reshape(-1) for p in pieces] + [jnp.zeros(((-size) % _SMALL_TILE,), f32)]
            parts.append(jnp.concatenate(flat).reshape(-1, LANES))
    return _concat_padded(parts, PACK_ROW_MULT)


def _unpack_small(packed, like, names):
    out, r0 = {}, 0
    for n in names:
        size = like[n].size
        rows = -(-size // _SMALL_TILE) * 8
        out[n] = packed[r0:r0 + rows].reshape(-1)[:size].reshape(like[n].shape)
        r0 += rows
    return out


def _dims(w, x):
    d = {}
    d["D"] = x.shape[-1]
    d["T"] = x.shape[-2]
    d["DI"] = w["ssd_norm_g"].shape[-1]
    d["NH"] = w["ssd_dt_bias"].shape[-1]
    d["CD"] = w["ssd_conv_b"].shape[-1]
    d["G"] = SSD_N_GROUPS
    d["HPG"] = d["NH"] // d["G"]
    d["P"] = d["DI"] // d["NH"]
    d["N"] = (d["CD"] - d["DI"]) // (2 * d["G"])
    d["S5G"], d["S5N"] = w["s5_lambda_re"].shape[-2:]
    d["S5C"] = w["s5_b_re"].shape[-1]
    d["S5W"] = d["S5G"] * d["S5C"]
    d["NSB"] = d["S5W"] // S5_SUPERBLOCK
    d["GSB"] = d["S5G"] // d["NSB"]
    return d


def _head_pad(v, d):
    lead = v.shape[:-1]
    v = v.reshape(lead + (d["G"], d["HPG"]))
    v = jnp.concatenate([v, jnp.zeros(lead + (d["G"], LANES - d["HPG"]), v.dtype)], axis=-1)
    return v.reshape(lead + (d["G"] * LANES,))


def _head_unpad(v, d):
    lead = v.shape[:-1]
    return v.reshape(lead + (d["G"], LANES))[..., :d["HPG"]].reshape(lead + (d["NH"],))


def _w_in_perm(shards, d):
    o, nh = d["DI"] + d["CD"], d["NH"]
    r = shards[0].shape[0]

    def rows(lo, hi):
        out = []
        for k, s in enumerate(shards):
            a, b = max(lo, k * r), min(hi, (k + 1) * r)
            if a < b:
                out.append(s[a - k * r:b - k * r])
        return out

    dt = _head_pad(jnp.concatenate(rows(o, o + nh), axis=0).T, d).T
    return jnp.concatenate(rows(0, o) + rows(o + nh, len(shards) * r) + [dt], axis=0)


def _w_in_unperm(g, d):
    o = d["DI"] + d["CD"]
    rest = d["S5W"] + 2 * d["D"]
    return jnp.concatenate([g[:o], _head_unpad(g[o + rest:].T, d).T, g[o:o + rest]], axis=0)


def _s5_block_diag(v, d):
    gsb = d["GSB"]
    g, a, b = v.shape
    row_group = (lax.broadcasted_iota(jnp.int32, (g * a, gsb * b), 0) // a) % gsb
    col_group = lax.broadcasted_iota(jnp.int32, (g * a, gsb * b), 1) // b
    return jnp.where(row_group == col_group, jnp.tile(v.reshape(g * a, b), (1, gsb)), 0)


def _s5_diag_blocks(m, d, a, b):
    gsb = d["GSB"]
    rows = m.shape[0]
    m = m.reshape(rows, gsb, b)
    row_group = (lax.broadcasted_iota(jnp.int32, (rows, gsb, 1), 0) // a) % gsb
    col_group = lax.broadcasted_iota(jnp.int32, (rows, gsb, 1), 1)
    return jnp.sum(jnp.where(row_group == col_group, m, 0), axis=1).reshape(rows // a, a, b)


def _s5_lam_rows(v, d):
    v = v.reshape(d["NSB"], 1, d["GSB"] * d["S5N"])
    return jnp.broadcast_to(v, (d["NSB"], 8, v.shape[-1])).reshape(d["NSB"] * 8, -1)


def _ffn_fwd(h, pre_g, post_g, wgu, wd, tag):
    D = h.shape[1]
    H2 = wgu.shape[0]
    xn = _row_kernel(f"{tag}_norm", _fwd_of(_f_norm), [(h, D, 0)], [pre_g], [(D, bf16)])[0]
    ab = _mm(xn, wgu, "nt", bf16, f"{tag}_mm_up")
    hid = _row_kernel(f"{tag}_swiglu", _swiglu_fwd, [(ab, H2, 0)], [], [(H2 // 2, bf16)])[0]
    f = _mm(hid, wd, "nn", f32, f"{tag}_mm_down")
    out = _row_kernel(f"{tag}_resnorm", _fwd_of(_f_resnorm(0.5)), [(h, D, 0), (f, D, 0)], [post_g], [(D, f32)])[0]
    return out, dict(h=h, xn=xn, ab=ab, hid=hid, f=f)


def _ffn_bwd(dh_out, s, pre_g, post_g, wgu, wd, tag):
    D = dh_out.shape[1]
    H2 = wgu.shape[0]
    df, dpost = _row_kernel(f"{tag}_resnorm_bwd", _vjp_of(_f_post(0.5), 1, 1, [0]), [(s["f"], D, 0), (dh_out, D, 0)],
                            [post_g], [(D, bf16)], [post_g.shape])
    dwd = _mm(s["hid"], df, "tn", bf16, f"{tag}_mm_dwd")
    dhid = _mm(df, wd, "nt", bf16, f"{tag}_mm_dhid")
    dab = _row_kernel(f"{tag}_swiglu_bwd", _swiglu_bwd, [(s["ab"], H2, 0), (dhid, H2 // 2, 0)], [], [(H2, bf16)])[0]
    dwgu = _mm(dab, s["xn"], "tn", bf16, f"{tag}_mm_dwgu")
    dxn = _mm(dab, wgu, "nn", f32, f"{tag}_mm_dxn")
    dh, dpre = _row_kernel(f"{tag}_norm_bwd", _vjp_of(_f_norm, 1, 1, [0], 1), [(s["h"], D, 0), (dxn, D, 0), (dh_out, D, 0)],
                           [pre_g], [(D, f32)], [pre_g.shape])
    return dh, dict(pre_g=dpre, post_g=dpost, wgu=dwgu, wd=dwd)


def _mixer_fwd(h, p, d):
    D, DI, CD, G, N = d["D"], d["DI"], d["CD"], d["G"], d["N"]
    gl = G * LANES
    c_u5, c_ga, c_gb, c_dt = DI + CD, DI + CD + d["S5W"], DI + CD + d["S5W"] + D, DI + CD + d["S5W"] + 2 * D
    u = _row_kernel("mix_norm", _fwd_of(_f_norm), [(h, D, 0)], [p["mix_pre_g"]], [(D, bf16)])[0]
    proj = _mm(u, p["w_in"], "nt", f32, "mix_mm_in", bn_t=512)
    act = _conv_fwd(proj, DI, p["conv_w"], p["conv_b"], "ssd_conv")
    dt, adt = _row_kernel("ssd_dt", _fwd_of(_f_dt), [(proj, gl, c_dt // gl)], [p["dt_bias"], p["a_log"]], [(gl, f32)] * 2)
    y_ssd, states = _ssd_fwd(act, dt, adt, p["d_skip"], d["HPG"], d["P"], N, "ssd_scan")
    nrm = _row_kernel("ssd_post", _fwd_of(_f_ssdpost(G)), [(y_ssd, DI, 0), (proj, DI, 0)], [p["norm_g"]], [(DI, bf16)])[0]
    y_a = _mm(nrm, p["w_a"], "nn", f32, "mix_mm_a")
    u5 =(proj, d["S5W"], c_u5 // d["S5W"])
    bu = _s5_in(proj, c_u5, p["bsb"], d)
    s5st = _s5_scan_fwd(bu, p["lam_re_rows"], p["lam_im_rows"], d["NSB"], "s5_scan")
    y5 = _bdmm(s5st, p["csb"], "nn", d["NSB"], f32, "s5_mm_c")
    gel = _row_kernel("s5_post", _fwd_of(_f_s5post), [(y5, d["S5W"], 0), u5], [p["s5_d"]], [(d["S5W"], bf16)])[0]
    vg = _mm(gel, p["w_glu"], "nt", bf16, "mix_mm_glu")
    glu = _row_kernel("s5_glu", _glu_fwd, [(vg, vg.shape[1], 0)], [], [(vg.shape[1] // 2, bf16)])[0]
    y_b = _mm(glu, p["w_b"], "nn", f32, "mix_mm_b")
    merged = _row_kernel("mix_merge", _fwd_of(_f_merge), [(proj, D, c_ga // D), (y_a, D, 0), (proj, D, c_gb // D), (y_b, D, 0)],
                         [], [(D, bf16)])[0]
    m = _mm(merged, p["w_out"], "nn", f32, "mix_mm_out")
    out = _row_kernel("mix_resnorm", _fwd_of(_f_resnorm(1.0)), [(h, D, 0), (m, D, 0)], [p["mix_post_g"]], [(D, f32)])[0]
    return out, dict(h=h, u=u, proj=proj, act=act, dt=dt, adt=adt, states=states, y_ssd=y_ssd, nrm=nrm, y_a=y_a, s5st=s5st,
                     y5=y5, gel=gel, vg=vg, glu=glu, y_b=y_b, merged=merged, m=m)


def _s5_in(proj, c_u5, bsb, d):
    T = proj.shape[0]
    nsb = d["NSB"]
    ka, nw = S5_SUPERBLOCK, bsb.shape[1]
    off = c_u5 // ka
    assert c_u5 % ka == 0
    bt = _pick(T, 512)

    def body(a_ref, w_ref, o_ref):
        o_ref[...] = _dot(a_ref[...].astype(bf16), w_ref[...].astype(bf16))

    return pl.pallas_call(
        body, name="s5_mm_bu", grid=(nsb, T // bt),
        in_specs=[pl.BlockSpec((bt, ka), lambda j, i: (i, off + j)), pl.BlockSpec((ka, nw), lambda j, i: (j, 0))],
        out_specs=pl.BlockSpec((bt, nw), lambda j, i: (i, j)), out_shape=jax.ShapeDtypeStruct((T, nsb * nw), f32),
        compiler_params=_params(("parallel", "parallel")),
    )(proj, bsb)


def _s5_dbsb(proj, c_u5, a, d):
    T = proj.shape[0]
    nsb = d["NSB"]
    ka, nw = S5_SUPERBLOCK, a.shape[1] // nsb
    off = c_u5 // ka
    bt = _pick(T, 512)

    def body(u_ref, a_ref, o_ref):
        pr = _dot(u_ref[...].astype(bf16), a_ref[...].astype(bf16), "tn")
        k = pl.program_id(1)

        @pl.when(k == 0)
        def _():
            o_ref[...] = pr

        @pl.when(k > 0)
        def _():
            o_ref[...] += pr

    return pl.pallas_call(
        body, name="s5_mm_dbsb", grid=(nsb, T // bt),
        in_specs=[pl.BlockSpec((bt, ka), lambda j, k: (k, off + j)), pl.BlockSpec((bt, nw), lambda j, k: (k, j))],
        out_specs=pl.BlockSpec((ka, nw), lambda j, k: (j, 0)), out_shape=jax.ShapeDtypeStruct((nsb * ka, nw), f32),
        compiler_params=_params(("parallel", "arbitrary")),
    )(proj, a)


def _mixer_bwd(dh_out, s, p, d):
    D, DI, CD, G, N, S5W = d["D"], d["DI"], d["CD"], d["G"], d["N"], d["S5W"]
    gl = G * LANES
    gn = G * N
    c_u5, c_ga, c_gb, c_dt = DI + CD, DI + CD + S5W, DI + CD + S5W + D, DI + CD + S5W + 2 * D
    proj = s["proj"]
    g = {}
    dm, g["mix_post_g"] = _row_kernel("mix_resnorm_bwd", _vjp_of(_f_post(1.0), 1, 1, [0]), [(s["m"], D, 0), (dh_out, D, 0)],
                                      [p["mix_post_g"]], [(D, bf16)], [p["mix_post_g"].shape])
    g["w_out"] = _mm(s["merged"], dm, "tn", bf16, "mix_mm_dwout")
    dmerged = _mm(dm, p["w_out"], "nt", f32, "mix_mm_dmerged")
    dga, dya, dgb, dyb = _row_kernel(
        "mix_merge_bwd", _vjp_of(_f_merge, 4, 1, [0, 1, 2, 3]),
        [(proj, D, c_ga // D), (s["y_a"], D, 0), (proj, D, c_gb // D), (s["y_b"], D, 0), (dmerged, D, 0)], [],
        [(D, bf16), (D, bf16), (D, bf16), (D, bf16)])
    g["w_a"] = _mm(s["nrm"], dya, "tn", bf16, "mix_mm_dwa")
    dnrm = _mm(dya, p["w_a"], "nt", f32, "mix_mm_dnrm")
    dy_ssd, dz, g["norm_g"] = _row_kernel(
        "ssd_post_bwd", _vjp_of(_f_ssdpost(G), 2, 1, [0, 1]), [(s["y_ssd"], DI, 0), (proj, DI, 0), (dnrm, DI, 0)],
        [p["norm_g"]], [(DI, f32), (DI, bf16)], [p["norm_g"].shape])
    dxs, d_b, d_c, ddt, dadt, dd = _ssd_bwd(s["act"], s["dt"], s["adt"], p["d_skip"], s["states"], dy_ssd,
                                            d["HPG"], d["P"], N, "ssd_scan_bwd")
    g["d_skip"] = dd.reshape(G, 8, LANES)[:, 0, :].reshape(1, gl)
    ddt_raw, g["dt_bias"], g["a_log"] = _row_kernel(
        "ssd_dt_bwd", _vjp_of(_f_dt, 1, 2, [0]), [(proj, gl, c_dt // gl), (ddt, gl, 0), (dadt, gl, 0)],
        [p["dt_bias"], p["a_log"]], [(gl, bf16)], [p["dt_bias"].shape, p["a_log"].shape])
    cw, cb = p["conv_w"], p["conv_b"]
    dxc_x, dw_x, db_x = _conv_bwd(proj, DI, cw[:, :DI], cb[:, :DI], dxs, "ssd_conv_bwd_x")
    dxc_b, dw_b, db_b = _conv_bwd(proj, 2 * DI, cw[:, DI:DI + gn], cb[:, DI:DI + gn], d_b, "ssd_conv_bwd_b")
    dxc_c, dw_c, db_c = _conv_bwd(proj, 2 * DI + gn, cw[:, DI + gn:], cb[:, DI + gn:], d_c, "ssd_conv_bwd_c")
    g["conv_w"] = jnp.concatenate([dw_x, dw_b, dw_c], axis=1)
    g["conv_b"] = jnp.concatenate([db_x, db_b, db_c], axis=1)
    g["w_b"] = _mm(s["glu"], dyb, "tn", bf16, "mix_mm_dwb")
    dglu = _mm(dyb, p["w_b"], "nt", f32, "mix_mm_dglu")
    dvg = _row_kernel("s5_glu_bwd", _glu_bwd, [(s["vg"], s["vg"].shape[1], 0), (dglu, S5W, 0)], [], [(s["vg"].shape[1], bf16)])[0]
    g["w_glu"] = _mm(dvg, s["gel"], "tn", bf16, "mix_mm_dwglu")
    dgel = _mm(dvg, p["w_glu"], "nn", f32, "mix_mm_dgel")
    dy5, du5a, g["s5_d"] = _row_kernel(
        "s5_post_bwd", _vjp_of(_f_s5post, 2, 1, [0, 1]), [(s["y5"], S5W, 0), (proj, S5W, c_u5 // S5W), (dgel, S5W, 0)],
        [p["s5_d"]], [(S5W, bf16), (S5W, f32)], [p["s5_d"].shape])
    g["csb"] = _bdmm(s["s5st"], dy5, "tn", d["NSB"], f32, "s5_mm_dcsb")
    gst = _bdmm(dy5, p["csb"], "nt", d["NSB"], f32, "s5_mm_gst")
    a, g["lam_re_rows"], g["lam_im_rows"] = _s5_scan_bwd(gst, s["s5st"], p["lam_re_rows"], p["lam_im_rows"], d["NSB"], "s5_scan_bwd")
    g["bsb"] = _s5_dbsb(proj, c_u5, a, d)
    du5b = _bdmm(a, p["bsb"], "nt", d["NSB"], f32, "s5_mm_du5")
    du5 = _row_kernel("s5_du5", _add_fn, [(du5a, S5W, 0), (du5b, S5W, 0)], [], [(S5W, bf16)])[0]
    dproj = jnp.concatenate([dz, dxc_x, dxc_b, dxc_c, du5, dga, dgb, ddt_raw], axis=1)
    g["w_in"] = _mm(dproj, s["u"], "tn", bf16, "mix_mm_dwin")
    du = _mm(dproj, p["w_in"], "nn", f32, "mix_mm_du", bk_t=2176)
    dh, g["mix_pre_g"] = _row_kernel("mix_norm_bwd", _vjp_of(_f_norm, 1, 1, [0], 1), [(s["h"], D, 0), (du, D, 0), (dh_out, D, 0)],
                                     [p["mix_pre_g"]], [(D, f32)], [p["mix_pre_g"].shape])
    return dh, g


def _ffn1_params(l, w, wf):
    whole = lambda *names: jnp.concatenate([s for n in names for s in wf(n)], axis=0)
    return dict(ffn1_pre_g=w["ffn1_pre_g"][l].reshape(1, -1), ffn1_post_g=w["ffn1_post_g"][l].reshape(1, -1),
                wgu1=whole("ffn1_w_gate", "ffn1_w_up"), wd1=whole("ffn1_w_down"))


def _layer_params(l, w, wf, conv_w_full, d):
    r2 = lambda v: v[l].reshape(1, -1)
    p = {}
    for n in ["mix_pre_g", "mix_post_g", "ffn2_pre_g", "ffn2_post_g", "s5_d"]:
        p[n] = r2(w[n])
    whole = lambda *names: jnp.concatenate([s for n in names for s in wf(n)], axis=0)
    p["wgu2"] = whole("ffn2_w_gate", "ffn2_w_up")
    p["wd2"] = whole("ffn2_w_down")
    p["w_in"] = _w_in_perm(wf("w_in"), d)
    p["w_a"], p["w_glu"], p["w_b"], p["w_out"] = whole("w_branch_a"), whole("s5_w_glu"), whole("w_branch_b"), whole("w_out")
    p["conv_w"] = conv_w_full[l]
    p["conv_b"] = r2(w["ssd_conv_b"])
    p["dt_bias"] = _head_pad(r2(w["ssd_dt_bias"]), d)
    p["a_log"] = _head_pad(r2(w["ssd_a_log"]), d)
    p["d_skip"] = _head_pad(r2(w["ssd_d"]), d)
    p["norm_g"] = r2(w["ssd_norm_g"])
    g5, n5, c5 = d["S5G"], d["S5N"], d["S5C"]
    expand = jnp.repeat(jnp.eye(n5, dtype=f32), c5, axis=1)
    prep_in = [w["s5_lambda_re"][l], w["s5_lambda_im"][l], w["s5_log_step"][l].reshape(g5, 1),
               w["s5_b_re"][l].reshape(g5, n5 * c5), w["s5_b_im"][l].reshape(g5, n5 * c5), expand]
    lbr, lbi, bbr, bbi = _s5_prep(prep_in, "s5_prep")
    p["s5_prep_in"] = prep_in
    p["lam_re_rows"], p["lam_im_rows"] = _s5_lam_rows(lbr, d), _s5_lam_rows(lbi, d)
    to_cn = lambda v: v.astype(bf16).reshape(g5, n5, c5).transpose(0, 2, 1)
    p["bsb"] = jnp.concatenate([_s5_block_diag(to_cn(bbr), d), _s5_block_diag(to_cn(bbi), d)], axis=1)
    c_re = w["s5_c_re"][l].astype(bf16).transpose(0, 2, 1)
    c_im = w["s5_c_im"][l].astype(bf16).transpose(0, 2, 1)
    nsb = d["NSB"]
    csb = jnp.stack([_s5_block_diag(c_re, d).reshape(nsb, -1, S5_SUPERBLOCK),
                     _s5_block_diag(-c_im, d).reshape(nsb, -1, S5_SUPERBLOCK)], axis=1)
    p["csb"] = csb.reshape(-1, S5_SUPERBLOCK)
    return p


def _s5_param_grads(g, p, d, l):
    g5, n5, c5, nsb, gsb = d["S5G"], d["S5N"], d["S5C"], d["NSB"], d["GSB"]
    wst = gsb * n5
    dbsb = g["bsb"]
    from_cn = lambda v: v.transpose(0, 2, 1).reshape(g5, n5 * c5)
    dbbr = from_cn(_s5_diag_blocks(dbsb[:, :wst], d, c5, n5))
    dbbi = from_cn(_s5_diag_blocks(dbsb[:, wst:], d, c5, n5))
    rows = lambda v: v.reshape(nsb, 8, wst)[:, 0, :].reshape(g5, n5)
    cots = [rows(g["lam_re_rows"]), rows(g["lam_im_rows"]), dbbr, dbbi]
    dlr, dli, dls, dbr, dbi = _s5_prep_bwd(p["s5_prep_in"], cots, "s5_prep_bwd")
    dcsb = g["csb"].reshape(nsb, 2, wst, S5_SUPERBLOCK)
    dcr = _s5_diag_blocks(dcsb[:, 0].reshape(-1, S5_SUPERBLOCK), d, n5, c5).transpose(0, 2, 1)
    dci = -_s5_diag_blocks(dcsb[:, 1].reshape(-1, S5_SUPERBLOCK), d, n5, c5).transpose(0, 2, 1)
    return dict(s5_lambda_re=dlr, s5_lambda_im=dli, s5_log_step=dls.reshape(g5), s5_b_re=dbr.reshape(g5, n5, c5),
                s5_b_im=dbi.reshape(g5, n5, c5), s5_c_re=dcr, s5_c_im=dci)


def kernel(x, ffn1_pre_g, ffn1_post_g, ffn1_w_gate, ffn1_w_up, ffn1_w_down, mix_pre_g, mix_post_g, w_in, ssd_conv_w, ssd_conv_b, ssd_dt_bias, ssd_a_log, ssd_d, ssd_norm_g, w_branch_a, s5_lambda_re, s5_lambda_im, s5_b_re, s5_b_im, s5_c_re, s5_c_im, s5_log_step, s5_d, s5_w_glu, w_branch_b, w_out, ffn2_pre_g, ffn2_post_g, ffn2_w_gate, ffn2_w_up, ffn2_w_down, loss_target, m_ffn1_pre_g, m_ffn1_post_g, m_ffn1_w_gate, m_ffn1_w_up, m_ffn1_w_down, m_mix_pre_g, m_mix_post_g, m_w_in, m_ssd_conv_w, m_ssd_conv_b, m_ssd_dt_bias, m_ssd_a_log, m_ssd_d, m_ssd_norm_g, m_w_branch_a, m_s5_lambda_re, m_s5_lambda_im, m_s5_b_re, m_s5_b_im, m_s5_c_re, m_s5_c_im, m_s5_log_step, m_s5_d, m_s5_w_glu, m_w_branch_b, m_w_out, m_ffn2_pre_g, m_ffn2_post_g, m_ffn2_w_gate, m_ffn2_w_up, m_ffn2_w_down, v_ffn1_pre_g, v_ffn1_post_g, v_ffn1_w_gate, v_ffn1_w_up, v_ffn1_w_down, v_mix_pre_g, v_mix_post_g, v_w_in, v_ssd_conv_w, v_ssd_conv_b, v_ssd_dt_bias, v_ssd_a_log, v_ssd_d, v_ssd_norm_g, v_w_branch_a, v_s5_lambda_re, v_s5_lambda_im, v_s5_b_re, v_s5_b_im, v_s5_c_re, v_s5_c_im, v_s5_log_step, v_s5_d, v_s5_w_glu, v_w_branch_b, v_w_out, v_ffn2_pre_g, v_ffn2_post_g, v_ffn2_w_gate, v_ffn2_w_up, v_ffn2_w_down):
    given = dict(locals())
    for n in COL_SHARDED:
        for prefix in ("", "m_", "v_"):
            given[prefix + n] = given[prefix + n].transpose(0, 2, 1)
    w = {n: given[n] for n in WEIGHTS}
    mom = {n: given["m_" + n] for n in WEIGHTS}
    var = {n: given["v_" + n] for n in WEIGHTS}
    d = _dims(w, x)
    n_layers = w["ffn1_pre_g"].shape[0]
    T, D = d["T"], d["D"]

    conv_w = w["ssd_conv_w"]
    first, rest = BIG[:3], BIG[3:]
    wf, _ = _unpack_weights(_chip_all_gather(_pack_weights(w, 0, first), "gather_weights_first"), w, 0, first)
    coming = _chip_gather_start(_pack_weights(w, 0, rest, conv_w), "gather_start_l0")
    h = x.reshape(T, D) + coming[-1][0, 0]
    saved, layers = [], []
    for l in range(n_layers):
        if l > 0:
            pack, land = _chip_gather_wait(*coming[:4], h, f"gather_wait_l{l}")
            wf, _ = _unpack_weights(_chip_gather_finish(pack, land, f"gather_finish_l{l}"), w, l, BIG)
        p = _ffn1_params(l, w, wf)
        h, s1 = _ffn_fwd(h, p["ffn1_pre_g"], p["ffn1_post_g"], p["wgu1"], p["wd1"], "ffn1")
        if l == 0:
            pack, land = _chip_gather_wait(*coming[:4], h, "gather_wait_l0")
            wf, conv_w_full = _unpack_weights(_chip_gather_finish(pack, land, "gather_finish_l0"), w, 0, rest, conv_w)
        if l + 1 < n_layers:
            coming = _chip_gather_start(_pack_weights(w, l + 1, BIG), f"gather_start_l{l + 1}")
            h = h + coming[-1][0, 0]
        p.update(_layer_params(l, w, wf, conv_w_full, d))
        layers.append(p)
        h, sm = _mixer_fwd(h, p, d)
        h, s2 = _ffn_fwd(h, p["ffn2_pre_g"], p["ffn2_post_g"], p["wgu2"], p["wd2"], "ffn2")
        saved.append((s1, sm, s2))
    dh, loss_part = _row_kernel("loss", _loss_fn, [(h, D, 0), (loss_target.reshape(T, D), D, 0)], [], [(D, f32)], [(8, LANES)])
    loss = lax.psum(loss_part[0, 0], ("x", "y", "c"))

    my_core = lax.axis_index("c").astype(jnp.int32).reshape(1)
    my_chip = (2 * lax.axis_index("x") + lax.axis_index("y")).astype(jnp.int32).reshape(1)
    lg, in_flight = [None] * n_layers, [None] * n_layers
    for l in reversed(range(n_layers)):
        p = layers[l]
        s1, sm, s2 = saved[l]
        dh, g2 = _ffn_bwd(dh, s2, p["ffn2_pre_g"], p["ffn2_post_g"], p["wgu2"], p["wd2"], "ffn2")
        dh, gm = _mixer_bwd(dh, sm, p, d)
        dh, g1 = _ffn_bwd(dh, s1, p["ffn1_pre_g"], p["ffn1_post_g"], p["wgu1"], p["wd1"], "ffn1")
        H = p["wd1"].shape[0]
        gl = dict(ffn1_pre_g=g1["pre_g"], ffn1_post_g=g1["post_g"], ffn1_w_gate=g1["wgu"][:H], ffn1_w_up=g1["wgu"][H:],
                  ffn1_w_down=g1["wd"], ffn2_pre_g=g2["pre_g"], ffn2_post_g=g2["post_g"], ffn2_w_gate=g2["wgu"][:H],
                  ffn2_w_up=g2["wgu"][H:], ffn2_w_down=g2["wd"], mix_pre_g=gm["mix_pre_g"], mix_post_g=gm["mix_post_g"],
                  w_in=_w_in_unperm(gm["w_in"], d), ssd_conv_w=gm["conv_w"], ssd_conv_b=gm["conv_b"],
                  ssd_dt_bias=_head_unpad(gm["dt_bias"], d), ssd_a_log=_head_unpad(gm["a_log"], d),
                  ssd_d=_head_unpad(gm["d_skip"], d), ssd_norm_g=gm["norm_g"], w_branch_a=gm["w_a"], s5_d=gm["s5_d"],
                  s5_w_glu=gm["w_glu"], w_branch_b=gm["w_b"], w_out=gm["w_out"])
        gl.update(_s5_param_grads(gm, p, d, l))
        lg[l] = gl
        if l > 0:
            in_flight[l] = _reduce_start(_pack_big_grads([gl]), my_core, f"grads_l{l}")
            dh = dh + in_flight[l][-1][0, 0]
    grad_x = dh.reshape(x.shape)
    in_flight[0] = _reduce_start(_pack_big_grads([lg[0]]), my_core, "grads_l0")
    summed = [None] * n_layers
    for l in range(1, n_layers):
        summed[l] = _reduce_finish(in_flight[l], my_chip, in_flight[0][-1], f"grads_l{l}")
    small_names = SMALL + ["ssd_conv_w"]
    small_parts = {n: [g[n] for g in lg] for n in small_names}
    small_like = {n: jax.ShapeDtypeStruct((n_layers,) + lg[0][n].shape, f32) for n in small_names}
    small_like.update({n: w[n] for n in SMALL})
    small_pack = _pack_small(small_parts, small_names)
    small_sum = _reduce_to_chips(jnp.broadcast_to(small_pack, (N_CHIPS,) + small_pack.shape), my_core, "small")
    small = _unpack_small(small_sum, small_like, small_names)
    summed[0] = _reduce_finish(in_flight[0], my_chip, small_sum, "grads_l0")
    grads = _unpack_big_grads(summed, w)
    k_me = 2 * lax.axis_index("x") + lax.axis_index("y")
    cw = w["ssd_conv_w"].shape[-1]
    small["ssd_conv_w"] = lax.dynamic_slice_in_dim(small["ssd_conv_w"], k_me * cw, cw, axis=2)
    grads.update(small)

    delta, new_m, new_v = {}, {}, {}
    for n in BIG + ["ssd_conv_w"]:
        delta[n], new_m[n], new_v[n] = _adamw(w[n], grads[n], mom[n], var[n], "adamw_" + n)
    pw, pm, pv = [_pack_small(t, SMALL) for t in (w, mom, var)]
    assert pw.shape[0] <= small_sum.shape[0]
    sd, sm_, sv = _adamw(pw, small_sum[:pw.shape[0]], pm, pv, "adamw_small")
    delta.update(_unpack_small(sd, w, SMALL))
    new_m.update(_unpack_small(sm_, w, SMALL))
    new_v.update(_unpack_small(sv, w, SMALL))
    for n in COL_SHARDED:
        for out in (grads, delta, new_m, new_v):
            out[n] = out[n].transpose(0, 2, 1)
    return (loss, grad_x, *[grads[n] for n in WEIGHTS], *[delta[n] for n in WEIGHTS],
            *[new_m[n] for n in WEIGHTS], *[new_v[n] for n in WEIGHTS])
```

```python
import functools

import numpy as np
import jax
import jax.numpy as jnp
from jax import lax
from jax.experimental import pallas as pl
from jax.experimental.pallas import tpu as pltpu

f32, bf16 = jnp.float32, jnp.bfloat16

SSD_N_GROUPS = 4
SSD_CHUNK = 128
RMS_EPS = 1e-6
S5_MAX_REAL = -1e-4
S5_SUPERBLOCK = 256
ADAM_LR, ADAM_B1, ADAM_B2, ADAM_EPS, ADAM_WD, ADAM_STEP = 0.001, 0.9, 0.999, 1e-08, 0.01, 10

LANES = 128
PACK_COLS = 1024
D2D_STREAMS = 16
PACK_ROW_MULT = 2 * D2D_STREAMS * 16
VMEM_LIMIT_BYTES = 48 * 1024 * 1024
N_CHIPS, N_CORES, N_DEV = 4, 2, 8
MESH = pl.DeviceIdType.MESH

BIG = ["ffn1_w_gate", "ffn1_w_up", "ffn1_w_down", "w_in", "w_branch_a", "s5_w_glu", "w_branch_b", "w_out",
       "ffn2_w_gate", "ffn2_w_up", "ffn2_w_down"]
COL_SHARDED = ["ffn1_w_gate", "ffn1_w_up", "w_in", "s5_w_glu", "ffn2_w_gate", "ffn2_w_up"]
SMALL = ["ffn1_pre_g", "ffn1_post_g", "mix_pre_g", "mix_post_g", "ssd_conv_b", "ssd_norm_g", "s5_lambda_re", "s5_lambda_im",
         "s5_b_re", "s5_b_im", "s5_c_re", "s5_c_im", "s5_d", "ffn2_pre_g", "ffn2_post_g", "s5_log_step", "ssd_dt_bias",
         "ssd_a_log", "ssd_d"]
WEIGHTS = ["ffn1_pre_g", "ffn1_post_g", "ffn1_w_gate", "ffn1_w_up", "ffn1_w_down", "mix_pre_g", "mix_post_g", "w_in",
           "ssd_conv_w", "ssd_conv_b", "ssd_dt_bias", "ssd_a_log", "ssd_d", "ssd_norm_g", "w_branch_a", "s5_lambda_re",
           "s5_lambda_im", "s5_b_re", "s5_b_im", "s5_c_re", "s5_c_im", "s5_log_step", "s5_d", "s5_w_glu", "w_branch_b",
           "w_out", "ffn2_pre_g", "ffn2_post_g", "ffn2_w_gate", "ffn2_w_up", "ffn2_w_down"]


def _params(sem=None):
    return pltpu.CompilerParams(dimension_semantics=sem, vmem_limit_bytes=VMEM_LIMIT_BYTES)


def _pick(n, target, mult=LANES):
    best = None
    for d in range(mult, min(n, target) + 1, mult):
        if n % d == 0:
            best = d
    return best if best is not None else n


_DIMS = {"nn": (((1,), (0,)), ((), ())), "nt": (((1,), (1,)), ((), ())), "tn": (((0,), (0,)), ((), ()))}


def _mm(a, b, mode, out_dtype, name, bm_t=1024, bn_t=1024, bk_t=2816):
    if mode == "nn":
        (M, K), (K2, N) = a.shape, b.shape
    elif mode == "nt":
        (M, K), (N, K2) = a.shape, b.shape
    else:
        (K, M), (K2, N) = a.shape, b.shape
    assert K == K2, (name, a.shape, b.shape)
    bm, bn, bk = _pick(M, bm_t), _pick(N, bn_t), _pick(K, bk_t)
    nk = K // bk
    dn = _DIMS[mode]

    def body(a_ref, b_ref, o_ref, *scratch):
        p = lax.dot_general(a_ref[...].astype(bf16), b_ref[...].astype(bf16), dn, preferred_element_type=f32)
        if nk == 1:
            o_ref[...] = p.astype(o_ref.dtype)
        else:
            acc = scratch[0]
            k = pl.program_id(2)

            @pl.when(k == 0)
            def _():
                acc[...] = p

            @pl.when(k > 0)
            def _():
                acc[...] += p

            @pl.when(k == nk - 1)
            def _():
                o_ref[...] = acc[...].astype(o_ref.dtype)

    if mode == "tn":
        a_spec = pl.BlockSpec((bk, bm), lambda i, j, k: (k, i))
    else:
        a_spec = pl.BlockSpec((bm, bk), lambda i, j, k: (i, k))
    if mode == "nt":
        b_spec = pl.BlockSpec((bn, bk), lambda i, j, k: (j, k))
    else:
        b_spec = pl.BlockSpec((bk, bn), lambda i, j, k: (k, j))
    return pl.pallas_call(
        body, name=name, grid=(M // bm, N // bn, nk), in_specs=[a_spec, b_spec],
        out_specs=pl.BlockSpec((bm, bn), lambda i, j, k: (i, j)), out_shape=jax.ShapeDtypeStruct((M, N), out_dtype),
        scratch_shapes=[pltpu.VMEM((bm, bn), f32)] if nk > 1 else [],
        compiler_params=_params(("parallel", "parallel", "arbitrary")),
    )(a, b)


def _bdmm(a, w, mode, nb, out_dtype, name, bt_t=512):
    if mode == "tn":
        T = a.shape[0]
        ka, nw = a.shape[1] // nb, w.shape[1] // nb
        bt = _pick(T, bt_t)
        nt = T // bt

        def body_tn(a_ref, b_ref, o_ref):
            p = lax.dot_general(a_ref[...].astype(bf16), b_ref[...].astype(bf16), _DIMS["tn"], preferred_element_type=f32)
            k = pl.program_id(1)

            @pl.when(k == 0)
            def _():
                o_ref[...] = p

            @pl.when(k > 0)
            def _():
                o_ref[...] += p

        return pl.pallas_call(
            body_tn, name=name, grid=(nb, nt),
            in_specs=[pl.BlockSpec((bt, ka), lambda j, k: (k, j)), pl.BlockSpec((bt, nw), lambda j, k: (k, j))],
            out_specs=pl.BlockSpec((ka, nw), lambda j, k: (j, 0)), out_shape=jax.ShapeDtypeStruct((nb * ka, nw), f32),
            compiler_params=_params(("parallel", "arbitrary")),
        )(a, w)
    T = a.shape[0]
    ka, nw = w.shape[0] // nb, w.shape[1]
    bt = _pick(T, bt_t)
    kin, kout = (ka, nw) if mode == "nn" else (nw, ka)
    dn = _DIMS[mode]

    def body(a_ref, w_ref, o_ref):
        o_ref[...] = lax.dot_general(a_ref[...].astype(bf16), w_ref[...].astype(bf16), dn,
                                     preferred_element_type=f32).astype(o_ref.dtype)

    return pl.pallas_call(
        body, name=name, grid=(nb, T // bt),
        in_specs=[pl.BlockSpec((bt, kin), lambda j, i: (i, j)), pl.BlockSpec((ka, nw), lambda j, i: (j, 0))],
        out_specs=pl.BlockSpec((bt, kout), lambda j, i: (i, j)), out_shape=jax.ShapeDtypeStruct((T, nb * kout), out_dtype),
        compiler_params=_params(("parallel", "parallel")),
    )(a, w)


def _row_index(i, cb):
    return (i, cb)


def _row_kernel(name, fn, rows, pars, row_outs, par_outs=(), block_rows=256):
    T = rows[0][0].shape[0]
    R = min(block_rows, T)
    assert T % R == 0
    nr, npar, nro = len(rows), len(pars), len(row_outs)

    def body(*refs):
        rv = [r[...] for r in refs[:nr]]
        pv = [r[...] for r in refs[nr:nr + npar]]
        ro, po = fn(rv, pv)
        for ref, v in zip(refs[nr + npar:nr + npar + nro], ro):
            ref[...] = v.astype(ref.dtype)
        if par_outs:
            i = pl.program_id(0)
            prefs = refs[nr + npar + nro:]

            @pl.when(i == 0)
            def _():
                for ref, v in zip(prefs, po):
                    ref[...] = v.astype(f32)

            @pl.when(i > 0)
            def _():
                for ref, v in zip(prefs, po):
                    ref[...] += v.astype(f32)

    in_specs = [pl.BlockSpec((R, nc), functools.partial(_row_index, cb=cb)) for (_, nc, cb) in rows]
    in_specs += [pl.BlockSpec(p.shape, lambda i: (0, 0)) for p in pars]
    out_specs = [pl.BlockSpec((R, nc), lambda i: (i, 0)) for (nc, _) in row_outs]
    out_specs += [pl.BlockSpec(s, lambda i: (0, 0)) for s in par_outs]
    out_shape = [jax.ShapeDtypeStruct((T, nc), dt) for (nc, dt) in row_outs]
    out_shape += [jax.ShapeDtypeStruct(s, f32) for s in par_outs]
    outs = pl.pallas_call(
        body, name=name, grid=(T // R,), in_specs=in_specs, out_specs=out_specs, out_shape=out_shape,
        compiler_params=_params(("arbitrary",) if par_outs else ("parallel",)),
    )(*[r[0] for r in rows], *pars)
    return list(outs)


def _fwd_of(f):
    def fn(rv, pv):
        return f([v.astype(f32) for v in rv], [v.astype(f32) for v in pv]), []
    return fn


def _vjp_of(f, n_x, n_cot, grad_idx, n_add=0):
    def fn(rv, pv):
        xs = [v.astype(f32) for v in rv[:n_x]]
        cots = [v.astype(f32) for v in rv[n_x:n_x + n_cot]]
        adds = rv[n_x + n_cot:n_x + n_cot + n_add]
        ps = [v.astype(f32) for v in pv]
        _, vjp = jax.vjp(lambda *a: f(list(a[:n_x]), list(a[n_x:])), *xs, *ps)
        g = vjp(cots)
        row_g = [g[i] for i in grad_idx]
        for k, a in enumerate(adds):
            row_g[k] = row_g[k] + a.astype(f32)
        return row_g, list(g[n_x:])
    return fn


def _rms(x, g):
    return x * lax.rsqrt(jnp.mean(x * x, axis=-1, keepdims=True) + RMS_EPS) * g


def _f_norm(xs, ps):
    return [_rms(xs[0], ps[0])]


def _f_post(scale):
    def f(xs, ps):
        return [scale * _rms(xs[0], ps[0])]
    return f


def _f_resnorm(scale):
    def f(xs, ps):
        return [xs[0] + scale * _rms(xs[1], ps[0])]
    return f


def _f_dt(xs, ps):
    dt = jax.nn.softplus(xs[0] + ps[0])
    return [dt, -jnp.exp(ps[1]) * dt]


def _f_ssdpost(n_groups):
    def f(xs, ps):
        y = xs[0] * jax.nn.silu(xs[1])
        width = y.shape[-1] // n_groups
        lane = lax.broadcasted_iota(jnp.int32, y.shape, 1)
        scale = jnp.zeros_like(y)
        for k in range(n_groups):
            m = ((lane >= k * width) & (lane < (k + 1) * width)).astype(f32)
            ms = jnp.sum(y * y * m, axis=-1, keepdims=True) / width
            scale = scale + lax.rsqrt(ms + RMS_EPS) * m
        return [y * scale * ps[0]]
    return f


def _f_s5post(xs, ps):
    return [jax.nn.gelu(xs[0] + ps[0] * xs[1])]


def _f_merge(xs, ps):
    return [jax.nn.sigmoid(xs[0]) * xs[1] + jax.nn.sigmoid(xs[2]) * xs[3]]


def _swiglu_fwd(rv, pv):
    ab = rv[0].astype(f32)
    h = ab.shape[1] // 2
    return [jax.nn.silu(ab[:, :h]) * ab[:, h:]], []


def _swiglu_bwd(rv, pv):
    ab, d = rv[0].astype(f32), rv[1].astype(f32)
    h = ab.shape[1] // 2
    a, b = ab[:, :h], ab[:, h:]
    s = jax.nn.sigmoid(a)
    return [jnp.concatenate([d * b * (s * (1.0 + a * (1.0 - s))), d * (a * s)], axis=1)], []


def _glu_fwd(rv, pv):
    vg = rv[0].astype(f32)
    h = vg.shape[1] // 2
    return [vg[:, :h] * jax.nn.sigmoid(vg[:, h:])], []


def _glu_bwd(rv, pv):
    vg, d = rv[0].astype(f32), rv[1].astype(f32)
    h = vg.shape[1] // 2
    s = jax.nn.sigmoid(vg[:, h:])
    return [jnp.concatenate([d * s, d * vg[:, :h] * s * (1.0 - s)], axis=1)], []


def _loss_fn(rv, pv):
    e = rv[0].astype(f32) - rv[1].astype(f32)
    per_tok = jnp.mean(e * e, axis=-1, keepdims=True)
    part = 0.5 * jnp.sum(per_tok, axis=0, keepdims=True)
    return [e / e.shape[-1]], [jnp.broadcast_to(part, (8, LANES))]


def _add_fn(rv, pv):
    return [rv[0].astype(f32) + rv[1].astype(f32)], []


def _adamw_fn(rv, pv):
    w, g, m, v = [x.astype(f32) for x in rv]
    m = ADAM_B1 * m + (1.0 - ADAM_B1) * g
    v = ADAM_B2 * v + (1.0 - ADAM_B2) * (g * g)
    m_hat = m / (1.0 - ADAM_B1 ** ADAM_STEP)
    v_hat = v / (1.0 - ADAM_B2 ** ADAM_STEP)
    return [-ADAM_LR * (m_hat / (jnp.sqrt(v_hat) + ADAM_EPS) + ADAM_WD * w), m, v], []


def _adamw(w, g, m, v, name):
    shape = w.shape
    cols = shape[-1] if (w.ndim >= 2 and shape[-1] >= LANES) else None
    if cols is None:
        n = int(np.prod(shape))
        cols = LANES if n % LANES == 0 else n
    n_rows = int(np.prod(shape)) // cols
    br, bc = _pick(n_rows, 256, 8), cols
    if br < 64 and cols % LANES == 0:
        br, bc = n_rows, LANES

    def body(w_ref, g_ref, m_ref, v_ref, d_ref, nm_ref, nv_ref):
        outs, _ = _adamw_fn([w_ref[...], g_ref[...], m_ref[...], v_ref[...]], [])
        d_ref[...], nm_ref[...], nv_ref[...] = outs

    spec = pl.BlockSpec((br, bc), lambda i, j: (i, j))
    outs = pl.pallas_call(
        body, name=name, grid=(n_rows // br, cols // bc), in_specs=[spec] * 4, out_specs=[spec] * 3,
        out_shape=[jax.ShapeDtypeStruct((n_rows, cols), f32)] * 3, compiler_params=_params(("parallel", "parallel")),
    )(*[t.reshape(n_rows, cols) for t in (w, g, m, v)])
    return [o.reshape(shape) for o in outs]


def _shift_down(x, s, row):
    if s == 0:
        return x
    return jnp.where(row >= s, pltpu.roll(x, s, 0), 0.0)


def _shift_up(x, s, row):
    if s == 0:
        return x
    n = x.shape[0]
    return jnp.where(row < n - s, pltpu.roll(x, n - s, 0), 0.0)


def _conv_pre(x, w, b, row):
    kw = w.shape[0]
    c = b
    for k in range(kw):
        c = c + w[k:k + 1, :] * _shift_down(x, kw - 1 - k, row)
    return c


def _conv_fwd(xsrc, col0, w, b, name, bc_t=512):
    T = xsrc.shape[0]
    kw, ncols = w.shape
    bc = _pick(ncols, bc_t)
    off = col0 // bc
    assert col0 % bc == 0

    def body(x_ref, w_ref, b_ref, o_ref):
        x = x_ref[...].astype(f32)
        row = lax.broadcasted_iota(jnp.int32, x.shape, 0)
        c = _conv_pre(x, w_ref[...], b_ref[...], row)
        o_ref[...] = c * jax.nn.sigmoid(c)

    return pl.pallas_call(
        body, name=name, grid=(ncols // bc,),
        in_specs=[pl.BlockSpec((T, bc), lambda j: (0, off + j)), pl.BlockSpec((kw, bc), lambda j: (0, j)),
                  pl.BlockSpec((1, bc), lambda j: (0, j))],
        out_specs=pl.BlockSpec((T, bc), lambda j: (0, j)), out_shape=jax.ShapeDtypeStruct((T, ncols), f32),
        compiler_params=_params(("parallel",)),
    )(xsrc, w, b)


def _conv_bwd(xsrc, col0, w, b, dact, name, bc_t=512):
    T = xsrc.shape[0]
    kw, ncols = w.shape
    bc = _pick(ncols, bc_t)
    off = col0 // bc
    assert col0 % bc == 0

    def body(x_ref, w_ref, b_ref, d_ref, dx_ref, dw_ref, db_ref):
        x = x_ref[...].astype(f32)
        w = w_ref[...]
        row = lax.broadcasted_iota(jnp.int32, x.shape, 0)
        c = _conv_pre(x, w, b_ref[...], row)
        s = jax.nn.sigmoid(c)
        dc = d_ref[...].astype(f32) * (s * (1.0 + c * (1.0 - s)))
        dx = jnp.zeros_like(x)
        dws = []
        for k in range(kw):
            dx = dx + w[k:k + 1, :] * _shift_up(dc, kw - 1 - k, row)
            dws.append(jnp.sum(dc * _shift_down(x, kw - 1 - k, row), axis=0, keepdims=True))
        dx_ref[...] = dx.astype(dx_ref.dtype)
        dw_ref[...] = jnp.concatenate(dws, axis=0)
        db_ref[...] = jnp.sum(dc, axis=0, keepdims=True)

    return pl.pallas_call(
        body, name=name, grid=(ncols // bc,),
        in_specs=[pl.BlockSpec((T, bc), lambda j: (0, off + j)), pl.BlockSpec((kw, bc), lambda j: (0, j)),
                  pl.BlockSpec((1, bc), lambda j: (0, j)), pl.BlockSpec((T, bc), lambda j: (0, j))],
        out_specs=[pl.BlockSpec((T, bc), lambda j: (0, j)), pl.BlockSpec((kw, bc), lambda j: (0, j)),
                   pl.BlockSpec((1, bc), lambda j: (0, j))],
        out_shape=[jax.ShapeDtypeStruct((T, ncols), bf16), jax.ShapeDtypeStruct((kw, ncols), f32),
                   jax.ShapeDtypeStruct((1, ncols), f32)],
        compiler_params=_params(("parallel",)),
    )(xsrc, w, b, dact)


_HI = lax.Precision.HIGHEST


def _dot(a, b, dims="nn", precision=None):
    return lax.dot_general(a, b, _DIMS[dims], preferred_element_type=f32, precision=precision)


def _dot01(a, b, dims="nn", ones="b"):
    x = a if ones == "b" else b
    hi = x.astype(bf16)
    rest = x - hi.astype(f32)
    mid = rest.astype(bf16)
    lo = (rest - mid.astype(f32)).astype(bf16)
    if ones == "b":
        e = b.astype(bf16)
        return _dot(hi, e, dims) + _dot(mid, e, dims) + _dot(lo, e, dims)
    e = a.astype(bf16)
    return _dot(e, hi, dims) + _dot(e, mid, dims) + _dot(e, lo, dims)


def _ssd_common(x_ref, b_ref, c_ref, dt_ref, adt_ref, d_ref, hpg, p):
    q = b_ref.shape[0]
    hp = hpg * p
    bb, cb = b_ref[...].astype(bf16), c_ref[...].astype(bf16)
    r = lax.broadcasted_iota(jnp.int32, (q, q), 0)
    s = lax.broadcasted_iota(jnp.int32, (q, q), 1)
    tril = r >= s
    trilf = tril.astype(f32)
    eh = lax.broadcasted_iota(jnp.int32, (LANES, hp), 0)
    ec = lax.broadcasted_iota(jnp.int32, (LANES, hp), 1)
    expand = ((ec >= eh * p) & (ec < (eh + 1) * p)).astype(f32)
    adt = adt_ref[...]
    cum = _dot01(trilf, adt, "nn", "a")
    cum_t = _dot01(adt, (r <= s).astype(f32), "tn")
    cum_e = _dot01(cum, expand)
    dt_e = _dot01(dt_ref[...], expand)
    d_e = _dot01(jnp.broadcast_to(d_ref[...], (8, LANES)), expand)[0:1, :]
    gmat = _dot(cb, bb, "nt")
    x = x_ref[...]
    xdt = x * dt_e
    e_all = jnp.exp(cum_e)
    dec = jnp.exp(cum_e[q - 1:q, :] - cum_e)
    lms, ms = [], []
    for h in range(hpg):
        lm = jnp.exp(jnp.where(tril, cum[:, h:h + 1] - cum_t[h:h + 1, :], -1e30))
        lms.append(lm)
        ms.append(gmat * lm)
    et = [jnp.exp(cum[q - 1:q, h:h + 1]) for h in range(hpg)]
    return dict(bb=bb, cb=cb, trilf=trilf, expand=expand, cum=cum, x=x, xdt=xdt, dt_e=dt_e, d_e=d_e, e=e_all, dec=dec,
                lms=lms, ms=ms, et=et)


def _ssd_specs(q, hp, n, g_n, nc, rev):
    def cidx(c):
        return (nc - 1 - c) if rev else c
    x_spec = pl.BlockSpec((q, hp), lambda g, c: (cidx(c), g))
    boff = (g_n * hp) // n
    b_spec = pl.BlockSpec((q, n), lambda g, c: (cidx(c), boff + g))
    c_spec = pl.BlockSpec((q, n), lambda g, c: (cidx(c), boff + g_n + g))
    dt_spec = pl.BlockSpec((q, LANES), lambda g, c: (cidx(c), g))
    d_spec = pl.BlockSpec((1, LANES), lambda g, c: (0, g))
    st_spec = pl.BlockSpec((1, 1, hp, n), lambda g, c: (cidx(c), g, 0, 0))
    return x_spec, b_spec, c_spec, dt_spec, d_spec, st_spec


def _ssd_fwd(act, dt, adt, dpad, hpg, p, n, name):
    T = act.shape[0]
    g_n, q = SSD_N_GROUPS, SSD_CHUNK
    nc, hp = T // q, hpg * p
    x_spec, b_spec, c_spec, dt_spec, d_spec, st_spec = _ssd_specs(q, hp, n, g_n, nc, False)

    def body(x_ref, b_ref, c_ref, dt_ref, adt_ref, d_ref, y_ref, st_ref, s_scr):
        @pl.when(pl.program_id(1) == 0)
        def _():
            s_scr[...] = jnp.zeros_like(s_scr)

        k = _ssd_common(x_ref, b_ref, c_ref, dt_ref, adt_ref, d_ref, hpg, p)
        s0 = s_scr[...]
        st_ref[0, 0] = s0
        xdtb = k["xdt"].astype(bf16)
        ydiag = [_dot(k["ms"][h].astype(bf16), xdtb[:, h * p:(h + 1) * p]) for h in range(hpg)]
        z = _dot(k["cb"], s0.astype(bf16), "nt")
        y_ref[...] = jnp.concatenate(ydiag, axis=1) + k["e"] * z + k["d_e"] * k["x"]
        upd = _dot((k["xdt"] * k["dec"]).astype(bf16), k["bb"], "tn")
        for h in range(hpg):
            s_scr[h * p:(h + 1) * p, :] = k["et"][h] * s0[h * p:(h + 1) * p, :] + upd[h * p:(h + 1) * p, :]

    return pl.pallas_call(
        body, name=name, grid=(g_n, nc),
        in_specs=[x_spec, b_spec, c_spec, dt_spec, dt_spec, d_spec],
        out_specs=[pl.BlockSpec((q, hp), lambda g, c: (c, g)), st_spec],
        out_shape=[jax.ShapeDtypeStruct((T, g_n * hp), f32), jax.ShapeDtypeStruct((nc, g_n, hp, n), f32)],
        scratch_shapes=[pltpu.VMEM((hp, n), f32)],
        compiler_params=_params(("parallel", "arbitrary")),
    )(act, act, act, dt, adt, dpad)


def _ssd_bwd(act, dt, adt, dpad, states, dy, hpg, p, n, name):
    T = act.shape[0]
    g_n, q = SSD_N_GROUPS, SSD_CHUNK
    nc, hp = T // q, hpg * p
    x_spec, b_spec, c_spec, dt_spec, d_spec, st_spec = _ssd_specs(q, hp, n, g_n, nc, True)

    def body(x_ref, b_ref, c_ref, dt_ref, adt_ref, d_ref, st_ref, dy_ref,
             dx_ref, db_ref, dc_ref, ddt_ref, dadt_ref, dd_ref, ds_scr):
        first = pl.program_id(1) == 0

        @pl.when(first)
        def _():
            ds_scr[...] = jnp.zeros_like(ds_scr)

        k = _ssd_common(x_ref, b_ref, c_ref, dt_ref, adt_ref, d_ref, hpg, p)
        bb, cb, expand, x, xdt, dec = k["bb"], k["cb"], k["expand"], k["x"], k["xdt"], k["dec"]
        heads = lambda t: _dot01(t, expand, "nt")
        s0 = st_ref[0, 0]
        ds1 = ds_scr[...]
        s0b, ds1b = s0.astype(bf16), ds1.astype(bf16)
        dy = dy_ref[...]
        dyb, xdtb = dy.astype(bf16), xdt.astype(bf16)
        lane = lax.broadcasted_iota(jnp.int32, (1, LANES), 1)
        dg = jnp.zeros((q, q), f32)
        w_rows = jnp.zeros((q, LANES), f32)
        w_cols, dxdt_parts = [], []
        for h in range(hpg):
            hs = slice(h * p, (h + 1) * p)
            dm = _dot(dyb[:, hs], xdtb[:, hs], "nt")
            dg = dg + dm * k["lms"][h]
            wm = dm * k["ms"][h]
            w_rows = w_rows + jnp.sum(wm, axis=1, keepdims=True) * (lane == h).astype(f32)
            w_cols.append(jnp.sum(wm, axis=0, keepdims=True))
            dxdt_parts.append(_dot(k["ms"][h].astype(bf16), dyb[:, hs], "tn"))
        dxdt_diag = jnp.concatenate(dxdt_parts, axis=1)
        w_cols = jnp.concatenate(w_cols + [jnp.zeros((LANES - hpg, q), f32)], axis=0).T
        dgb = dg.astype(bf16)
        z = _dot(cb, s0b, "nt")
        dz = dy * k["e"]
        dzb = dz.astype(bf16)
        dxd = _dot(bb, ds1b, "nt")
        ddec = dxd * xdt * dec
        db_ref[...] = _dot(dgb, cb, "tn") + _dot((xdt * dec).astype(bf16), ds1b)
        dc_ref[...] = _dot(dgb, bb) + _dot(dzb, s0b)
        ds0 = _dot(dzb, cb, "tn")
        for h in range(hpg):
            hs = slice(h * p, (h + 1) * p)
            ds_scr[hs, :] = ds0[hs, :] + k["et"][h] * ds1[hs, :]
        dxdt = dxdt_diag + dxd * dec
        ddec_h = heads(ddec)
        dcum = w_rows - w_cols + heads(dz * z) - ddec_h
        et_row = jnp.exp(k["cum"][q - 1:q, :])
        dsum = _dot01(jnp.ones((8, n), f32), _dot01(expand, ds1 * s0, "nn", "a"), "nt", "a")[0:1, :]
        dcl = dsum * et_row + jnp.sum(ddec_h, axis=0, keepdims=True)
        rowq = lax.broadcasted_iota(jnp.int32, (q, 1), 0)
        dcum = dcum + (rowq == q - 1).astype(f32) * dcl
        ddt_ref[...] = heads(dxdt * x)
        dadt_ref[...] = _dot01(k["trilf"], dcum, "tn", "a")
        dx_ref[...] = k["d_e"] * dy + dxdt * k["dt_e"]
        dd8 = heads(jnp.broadcast_to(jnp.sum(dy * x, axis=0, keepdims=True), (8, hp)))

        @pl.when(first)
        def _():
            dd_ref[...] = dd8

        @pl.when(jnp.logical_not(first))
        def _():
            dd_ref[...] += dd8

    rc = lambda g, c: (nc - 1 - c, g)
    return pl.pallas_call(
        body, name=name, grid=(g_n, nc),
        in_specs=[x_spec, b_spec, c_spec, dt_spec, dt_spec, d_spec, st_spec, pl.BlockSpec((q, hp), rc)],
        out_specs=[pl.BlockSpec((q, hp), rc), pl.BlockSpec((q, n), rc), pl.BlockSpec((q, n), rc),
                   pl.BlockSpec((q, LANES), rc), pl.BlockSpec((q, LANES), rc), pl.BlockSpec((8, LANES), lambda g, c: (g, 0))],
        out_shape=[jax.ShapeDtypeStruct((T, g_n * hp), f32), jax.ShapeDtypeStruct((T, g_n * n), f32),
                   jax.ShapeDtypeStruct((T, g_n * n), f32), jax.ShapeDtypeStruct((T, g_n * LANES), f32),
                   jax.ShapeDtypeStruct((T, g_n * LANES), f32), jax.ShapeDtypeStruct((g_n * 8, LANES), f32)],
        scratch_shapes=[pltpu.VMEM((hp, n), f32)],
        compiler_params=_params(("parallel", "arbitrary")),
    )(act, act, act, dt, adt, dpad, states, dy)


def _cmul(ar, ai, br, bi):
    return ar * br - ai * bi, ar * bi + ai * br


def _s5_tile_powers(lr, li):
    p = [(lr, li)]
    for _ in range(7):
        p.append(_cmul(p[-1][0], p[-1][1], lr, li))
    tile = (jnp.concatenate([q[0] for q in p], axis=0), jnp.concatenate([q[1] for q in p], axis=0))
    return tile, (p[0], p[1], p[3])


def _s5_tile_scan(xr, xi, steps, reverse):
    row = lax.broadcasted_iota(jnp.int32, xr.shape, 0)
    for d, (pr, pi) in zip((1, 2, 4), steps):
        if reverse:
            keep = row < 8 - d
            sr, si = pltpu.roll(xr, 8 - d, 0), pltpu.roll(xi, 8 - d, 0)
        else:
            keep = row >= d
            sr, si = pltpu.roll(xr, d, 0), pltpu.roll(xi, d, 0)
        sr, si = jnp.where(keep, sr, 0.0), jnp.where(keep, si, 0.0)
        ar, ai = _cmul(sr, si, pr, pi)
        xr, xi = xr + ar, xi + ai
    return xr, xi


def _s5_scan_fwd(bu, lam_re, lam_im, nsb, name, tc_t=512):
    T = bu.shape[0]
    w2 = bu.shape[1] // nsb
    w = w2 // 2
    tc = _pick(T, tc_t, 8)

    def body(bu_ref, lr_ref, li_ref, st_ref, carry):
        @pl.when(pl.program_id(1) == 0)
        def _():
            carry[...] = jnp.zeros_like(carry)

        (pr8, pi8), steps = _s5_tile_powers(lr_ref[0:1, :], li_ref[0:1, :])

        def tile(i, c):
            r = pl.ds(pl.multiple_of(i * 8, 8), 8)
            x = bu_ref[r, :]
            xr, xi = _s5_tile_scan(x[:, :w], x[:, w:], steps, False)
            ar, ai = _cmul(pr8, pi8, c[0], c[1])
            xr, xi = xr + ar, xi + ai
            st_ref[r, :] = jnp.concatenate([xr, xi], axis=1)
            return xr[7:8, :], xi[7:8, :]

        c = lax.fori_loop(0, tc // 8, tile, (carry[0:1, :], carry[1:2, :]), unroll=2)
        carry[0:1, :] = c[0]
        carry[1:2, :] = c[1]

    return pl.pallas_call(
        body, name=name, grid=(nsb, T // tc),
        in_specs=[pl.BlockSpec((tc, w2), lambda j, i: (i, j)), pl.BlockSpec((8, w), lambda j, i: (j, 0)),
                  pl.BlockSpec((8, w), lambda j, i: (j, 0))],
        out_specs=pl.BlockSpec((tc, w2), lambda j, i: (i, j)), out_shape=jax.ShapeDtypeStruct(bu.shape, f32),
        scratch_shapes=[pltpu.VMEM((8, w), f32)],
        compiler_params=_params(("parallel", "arbitrary")),
    )(bu, lam_re, lam_im)


def _s5_scan_bwd(gst, states, lam_re, lam_im, nsb, name, tc_t=512):
    T = gst.shape[0]
    w2 = gst.shape[1] // nsb
    w = w2 // 2
    tc = _pick(T, tc_t, 8)
    nt = T // tc
    n_tiles = tc // 8

    def body(g_ref, s_ref, sp_ref, lr_ref, li_ref, a_ref, dlr_ref, dli_ref, carry, acc):
        chunk = pl.program_id(1)

        @pl.when(chunk == 0)
        def _():
            carry[...] = jnp.zeros_like(carry)
            acc[...] = jnp.zeros_like(acc)

        (qr8, qi8), steps = _s5_tile_powers(lr_ref[0:1, :], -li_ref[0:1, :])
        row = lax.broadcasted_iota(jnp.int32, (8, w), 0)
        rev_r, rev_i = jnp.zeros((8, w), f32), jnp.zeros((8, w), f32)
        for r in range(8):
            rev_r = jnp.where(row == r, qr8[7 - r:8 - r, :], rev_r)
            rev_i = jnp.where(row == r, qi8[7 - r:8 - r, :], rev_i)
        row2 = lax.broadcasted_iota(jnp.int32, (8, w2), 0)

        def tile(k, c):
            ar_in, ai_in, dr, di = c
            i = n_tiles - 1 - k
            r = pl.ds(pl.multiple_of(i * 8, 8), 8)
            x = g_ref[r, :]
            xr, xi = _s5_tile_scan(x[:, :w], x[:, w:], steps, True)
            pr, pi = _cmul(rev_r, rev_i, ar_in, ai_in)
            xr, xi = xr + pr, xi + pi
            a_ref[r, :] = jnp.concatenate([xr, xi], axis=1)
            before = jnp.where(i > 0, s_ref[pl.ds(pl.multiple_of(jnp.maximum(i - 1, 0) * 8, 8), 8), :],
                               sp_ref[tc - 8:tc, :] * (chunk < nt - 1).astype(f32))
            prev = jnp.where(row2 == 0, pltpu.roll(before, 1, 0), pltpu.roll(s_ref[r, :], 1, 0))
            spr, spi = prev[:, :w], prev[:, w:]
            return xr[0:1, :], xi[0:1, :], dr + xr * spr + xi * spi, di - xr * spi + xi * spr

        c0 = (carry[0:1, :], carry[1:2, :], acc[0:8, :], acc[8:16, :])
        ar, ai, dr, di = lax.fori_loop(0, n_tiles, tile, c0, unroll=2)
        carry[0:1, :] = ar
        carry[1:2, :] = ai
        acc[0:8, :] = dr
        acc[8:16, :] = di
        dlr_ref[...] = jnp.broadcast_to(jnp.sum(dr, axis=0, keepdims=True), (8, w))
        dli_ref[...] = jnp.broadcast_to(jnp.sum(di, axis=0, keepdims=True), (8, w))

    cur = lambda j, i: (nt - 1 - i, j)
    prv = lambda j, i: (jnp.maximum(nt - 2 - i, 0), j)
    return pl.pallas_call(
        body, name=name, grid=(nsb, nt),
        in_specs=[pl.BlockSpec((tc, w2), cur), pl.BlockSpec((tc, w2), cur), pl.BlockSpec((tc, w2), prv),
                  pl.BlockSpec((8, w), lambda j, i: (j, 0)), pl.BlockSpec((8, w), lambda j, i: (j, 0))],
        out_specs=[pl.BlockSpec((tc, w2), cur), pl.BlockSpec((8, w), lambda j, i: (j, 0)),
                   pl.BlockSpec((8, w), lambda j, i: (j, 0))],
        out_shape=[jax.ShapeDtypeStruct(gst.shape, f32), jax.ShapeDtypeStruct((nsb * 8, w), f32),
                   jax.ShapeDtypeStruct((nsb * 8, w), f32)],
        scratch_shapes=[pltpu.VMEM((8, w), f32), pltpu.VMEM((16, w), f32)],
        compiler_params=_params(("parallel", "arbitrary")),
    )(gst, states, states, lam_re, lam_im)


def _s5_prep_fn(xs, ps):
    lam_re, lam_im, log_step, b_re, b_im, expand = ps
    lr = jnp.minimum(lam_re, S5_MAX_REAL)
    li = lam_im
    step = jnp.exp(log_step)
    er = jnp.exp(lr * step)
    ang = li * step
    lbr, lbi = er * jnp.cos(ang), er * jnp.sin(ang)
    nr, ni = lbr - 1.0, lbi
    den = lr * lr + li * li
    qr, qi = (nr * lr + ni * li) / den, (ni * lr - nr * li) / den
    qre, qie = _dot(qr, expand, "nn", _HI), _dot(qi, expand, "nn", _HI)
    return [lbr, lbi, qre * b_re - qie * b_im, qre * b_im + qie * b_re]


def _s5_prep(pars, name):
    def body(*refs):
        outs = _s5_prep_fn([], [r[...] for r in refs[:6]])
        for ref, v in zip(refs[6:], outs):
            ref[...] = v

    g, nst = pars[0].shape
    nc = pars[3].shape[1]
    return pl.pallas_call(
        body, name=name,
        out_shape=[jax.ShapeDtypeStruct((g, nst), f32)] * 2 + [jax.ShapeDtypeStruct((g, nc), f32)] * 2,
        compiler_params=_params(),
    )(*pars)


def _s5_prep_bwd(pars, cots, name):
    def body(*refs):
        ps = [r[...] for r in refs[:6]]
        ct = [r[...] for r in refs[6:10]]
        _, vjp = jax.vjp(lambda *a: _s5_prep_fn([], list(a)), *ps)
        g = vjp(ct)
        for ref, v in zip(refs[10:], g[:5]):
            ref[...] = v

    return pl.pallas_call(
        body, name=name, out_shape=[jax.ShapeDtypeStruct(p.shape, f32) for p in pars[:5]], compiler_params=_params(),
    )(*pars, *cots)


_ANY = pl.BlockSpec(memory_space=pl.ANY)


def _remote(src, dst, send_sem, recv_sem, device):
    return pltpu.make_async_remote_copy(src_ref=src, dst_ref=dst, send_sem=send_sem, recv_sem=recv_sem, device_id=device,
                                        device_id_type=MESH)


def _staged_copy(src, dst, buf, in_sems, out_sems):
    n = D2D_STREAMS
    piece = src.shape[0] // n
    assert src.shape[0] % n == 0

    def load(i):
        return pltpu.make_async_copy(src.at[pl.ds(i * piece, piece)], buf.at[i % 2], in_sems.at[i % 2])

    def store(i):
        return pltpu.make_async_copy(buf.at[i % 2], dst.at[pl.ds(i * piece, piece)], out_sems.at[i % 2])

    load(0).start()
    for i in range(n):
        if i + 1 < n:
            if i >= 1:
                store(i - 1).wait()
            load(i + 1).start()
        load(i).wait()
        store(i).start()
    store(n - 2).wait()
    store(n - 1).wait()


def _stage_scratch(rows, cols, dtype):
    return [pltpu.VMEM((2, rows // D2D_STREAMS, cols), dtype), pltpu.SemaphoreType.DMA((2,)), pltpu.SemaphoreType.DMA((2,))]


def _chip_all_gather(block, name):
    rows = block.shape[0]
    half = rows // 2
    piece = half // D2D_STREAMS
    assert rows % (2 * D2D_STREAMS * 16) == 0

    def body(src, out, ici_send, ici_recv, d2d_send, d2d_recv, *stage):
        x, y, c = lax.axis_index("x"), lax.axis_index("y"), lax.axis_index("c")
        me = 2 * x + y
        sibling = (x, y, 1 - c)
        chips = [(1 - x, y), (x, 1 - y), (1 - x, 1 - y)]
        mine = pl.ds(pl.multiple_of(c * half, 16), half)
        sends = []
        for j, (px, py) in enumerate(chips):
            cp = _remote(src.at[mine], out.at[me, mine], ici_send.at[j], ici_recv.at[j], (px, py, c))
            cp.start()
            sends.append(cp)
        _staged_copy(src, out.at[me], *stage)
        for j, (px, py) in enumerate(chips):
            slot = 2 * px + py
            _remote(src.at[mine], out.at[slot, mine], ici_send.at[j], ici_recv.at[j], (px, py, c)).wait_recv()
            for s in range(D2D_STREAMS):
                r = pl.ds(pl.multiple_of(c * half + s * piece, 16), piece)
                k = j * D2D_STREAMS + s
                cp = _remote(out.at[slot, r], out.at[slot, r], d2d_send.at[k], d2d_recv.at[k], sibling)
                cp.start()
                sends.append(cp)
        for j, (px, py) in enumerate(chips):
            slot = 2 * px + py
            for s in range(D2D_STREAMS):
                r = pl.ds(pl.multiple_of((1 - c) * half + s * piece, 16), piece)
                k = j * D2D_STREAMS + s
                _remote(out.at[slot, r], out.at[slot, r], d2d_send.at[k], d2d_recv.at[k], sibling).wait_recv()
        for cp in sends:
            cp.wait_send()

    n_d2d = 3 * D2D_STREAMS
    return pl.pallas_call(
        body, name=name, in_specs=[_ANY], out_specs=_ANY,
        out_shape=jax.ShapeDtypeStruct((N_CHIPS,) + block.shape, block.dtype),
        scratch_shapes=[pltpu.SemaphoreType.DMA((3,)), pltpu.SemaphoreType.DMA((3,)), pltpu.SemaphoreType.DMA((n_d2d,)),
                        pltpu.SemaphoreType.DMA((n_d2d,))] + _stage_scratch(rows, block.shape[1], block.dtype),
    )(block)


def _chip_scatter(parts, name):
    def body(src, out, send_sems, recv_sems, *stage):
        x, y, c = lax.axis_index("x"), lax.axis_index("y"), lax.axis_index("c")
        me = 2 * x + y
        chips = [(1 - x, y), (x, 1 - y), (1 - x, 1 - y)]
        sends = []
        for j, (px, py) in enumerate(chips):
            cp = pltpu.make_async_remote_copy(src_ref=src.at[2 * px + py], dst_ref=out.at[me], send_sem=send_sems.at[j],
                                              recv_sem=recv_sems.at[j], device_id=(px, py, c), device_id_type=MESH)
            cp.start()
            sends.append(cp)
        _staged_copy(src.at[me], out.at[me], *stage)
        for j, (px, py) in enumerate(chips):
            pltpu.make_async_remote_copy(src_ref=src.at[me], dst_ref=out.at[2 * px + py], send_sem=send_sems.at[j],
                                         recv_sem=recv_sems.at[j], device_id=(px, py, c), device_id_type=MESH).wait_recv()
        for cp in sends:
            cp.wait_send()

    return pl.pallas_call(
        body, name=name, in_specs=[_ANY], out_specs=_ANY, out_shape=jax.ShapeDtypeStruct(parts.shape, parts.dtype),
        scratch_shapes=[pltpu.SemaphoreType.DMA((3,)), pltpu.SemaphoreType.DMA((3,))]
        + _stage_scratch(parts.shape[1], parts.shape[2], parts.dtype),
    )(parts)


_HBM = pl.BlockSpec(memory_space=pltpu.HBM)
_SEM = pl.BlockSpec(memory_space=pltpu.SEMAPHORE)
_EFFECT = pltpu.SideEffectType.DATAFLOW_SIDE_EFFECTING


def _gather_peers():
    x, y, c = lax.axis_index("x"), lax.axis_index("y"), lax.axis_index("c")
    return x, y, c, 2 * x + y, [(1 - x, y), (x, 1 - y), (1 - x, 1 - y)]


def _chip_gather_start(block, name):
    half = block.shape[0] // 2

    def body(src, land, send_sems, recv_sems, src_out, land_out, token):
        x, y, c, me, chips = _gather_peers()
        mine = pl.ds(pl.multiple_of(c * half, 16), half)
        for j, (px, py) in enumerate(chips):
            _remote(src.at[mine], land.at[me, mine], send_sems.at[j], recv_sems.at[j], (px, py, c)).start()
        token[...] = jnp.zeros_like(token)

    land_shape = (N_CHIPS,) + block.shape
    return pl.pallas_call(
        body, name=name,
        out_shape=(pltpu.SemaphoreType.DMA((3,)), pltpu.SemaphoreType.DMA((3,)), pltpu.HBM(block.shape, block.dtype),
                   pltpu.HBM(land_shape, block.dtype), jax.ShapeDtypeStruct((8, LANES), f32)),
        in_specs=(_HBM, _HBM), out_specs=(_SEM, _SEM, _HBM, _HBM, pl.BlockSpec(memory_space=pltpu.VMEM)),
        input_output_aliases={0: 2, 1: 3}, compiler_params=pltpu.CompilerParams(has_side_effects=_EFFECT),
    )(pltpu.with_memory_space_constraint(block, pltpu.HBM),
      pltpu.with_memory_space_constraint(lax.empty(land_shape, block.dtype), pltpu.HBM))


def _chip_gather_wait(send_sems, recv_sems, block, land, after, name):
    half = block.shape[0] // 2

    def body(src, land_ref, send_ref, recv_ref, after_ref, src_dead, land_out):
        x, y, c, me, chips = _gather_peers()
        mine = pl.ds(pl.multiple_of(c * half, 16), half)
        for j, (px, py) in enumerate(chips):
            cp = _remote(src.at[mine], land_ref.at[2 * px + py, mine], send_ref.at[j], recv_ref.at[j], (px, py, c))
            cp.wait_send()
            cp.wait_recv()

    return pl.pallas_call(
        body, name=name, out_shape=(pltpu.HBM(block.shape, block.dtype), pltpu.HBM(land.shape, land.dtype)),
        in_specs=(_HBM, _HBM, _SEM, _SEM, _ANY), out_specs=(_HBM, _HBM), input_output_aliases={0: 0, 1: 1},
        compiler_params=pltpu.CompilerParams(has_side_effects=_EFFECT),
    )(block, land, send_sems, recv_sems, after)


def _chip_gather_finish(block, land, name):
    rows = block.shape[0]
    half = rows // 2
    piece = half // D2D_STREAMS

    def body(src, land_ref, out, d2d_send, d2d_recv, *stage):
        x, y, c, me, chips = _gather_peers()
        sibling = (x, y, 1 - c)
        sends = []
        for j, (px, py) in enumerate(chips):
            slot = 2 * px + py
            for s in range(D2D_STREAMS):
                r = pl.ds(pl.multiple_of(c * half + s * piece, 16), piece)
                k = j * D2D_STREAMS + s
                cp = _remote(land_ref.at[slot, r], out.at[slot, r], d2d_send.at[k], d2d_recv.at[k], sibling)
                cp.start()
                sends.append(cp)
        _staged_copy(src, out.at[me], *stage)
        for j, (px, py) in enumerate(chips):
            slot = 2 * px + py
            for s in range(D2D_STREAMS):
                r = pl.ds(pl.multiple_of((1 - c) * half + s * piece, 16), piece)
                k = j * D2D_STREAMS + s
                _remote(land_ref.at[slot, r], out.at[slot, r], d2d_send.at[k], d2d_recv.at[k], sibling).wait_recv()
        for cp in sends:
            cp.wait_send()

    n_d2d = 3 * D2D_STREAMS
    return pl.pallas_call(
        body, name=name, in_specs=[_ANY, _ANY], out_specs=_ANY, out_shape=jax.ShapeDtypeStruct(land.shape, land.dtype),
        input_output_aliases={1: 0},
        scratch_shapes=[pltpu.SemaphoreType.DMA((n_d2d,)), pltpu.SemaphoreType.DMA((n_d2d,))]
        + _stage_scratch(rows, block.shape[1], block.dtype),
    )(block, land)


def _chip_scatter_start(parts, name):
    def body(src, land, send_sems, recv_sems, src_out, land_out, token):
        x, y, c = lax.axis_index("x"), lax.axis_index("y"), lax.axis_index("c")
        me = 2 * x + y
        for j, (px, py) in enumerate([(1 - x, y), (x, 1 - y), (1 - x, 1 - y)]):
            _remote(src.at[2 * px + py], land.at[me], send_sems.at[j], recv_sems.at[j], (px, py, c)).start()
        token[...] = jnp.zeros_like(token)

    return pl.pallas_call(
        body, name=name,
        out_shape=(pltpu.SemaphoreType.DMA((3,)), pltpu.SemaphoreType.DMA((3,)), pltpu.HBM(parts.shape, parts.dtype),
                   pltpu.HBM(parts.shape, parts.dtype), jax.ShapeDtypeStruct((8, LANES), f32)),
        in_specs=(_HBM, _HBM), out_specs=(_SEM, _SEM, _HBM, _HBM, pl.BlockSpec(memory_space=pltpu.VMEM)),
        input_output_aliases={0: 2, 1: 3}, compiler_params=pltpu.CompilerParams(has_side_effects=_EFFECT),
    )(pltpu.with_memory_space_constraint(parts, pltpu.HBM),
      pltpu.with_memory_space_constraint(lax.empty(parts.shape, parts.dtype), pltpu.HBM))


def _chip_scatter_wait(send_sems, recv_sems, parts, land, after, name):
    def body(src, land_ref, send_ref, recv_ref, after_ref, src_dead, land_out):
        x, y, c = lax.axis_index("x"), lax.axis_index("y"), lax.axis_index("c")
        me = 2 * x + y
        for j, (px, py) in enumerate([(1 - x, y), (x, 1 - y), (1 - x, 1 - y)]):
            cp = _remote(src.at[2 * px + py], land_ref.at[2 * px + py], send_ref.at[j], recv_ref.at[j], (px, py, c))
            cp.wait_send()
            cp.wait_recv()

    return pl.pallas_call(
        body, name=name, out_shape=(pltpu.HBM(parts.shape, parts.dtype), pltpu.HBM(land.shape, land.dtype)),
        in_specs=(_HBM, _HBM, _SEM, _SEM, _ANY), out_specs=(_HBM, _HBM), input_output_aliases={0: 0, 1: 1},
        compiler_params=pltpu.CompilerParams(has_side_effects=_EFFECT),
    )(parts, land, send_sems, recv_sems, after)


def _sum_slots_own(landed, own, chip, name, block_rows=256):
    s_n, r_n, c_n = landed.shape
    br = _pick(r_n, block_rows, 8)

    def body(chip_ref, land_ref, own_ref, o_ref):
        acc = jnp.zeros((br, c_n), f32)
        for s in range(s_n):
            acc = acc + jnp.where(chip_ref[0] == s, own_ref[s], land_ref[s]).astype(f32)
        o_ref[...] = acc

    spec = pl.BlockSpec((s_n, br, c_n), lambda i, c: (0, i, 0))
    grid_spec = pltpu.PrefetchScalarGridSpec(num_scalar_prefetch=1, grid=(r_n // br,), in_specs=[spec, spec],
                                             out_specs=pl.BlockSpec((br, c_n), lambda i, c: (i, 0)))
    return pl.pallas_call(
        body, name=name, grid_spec=grid_spec, out_shape=jax.ShapeDtypeStruct((r_n, c_n), f32),
        compiler_params=_params(("parallel",)),
    )(chip, landed, own)


def _core_send_other_half(parts, name):
    n_slots, rows, cols = parts.shape
    half = rows // 2
    piece = half // D2D_STREAMS
    assert rows % (2 * D2D_STREAMS * 16) == 0

    def body(src, out, send_sems, recv_sems):
        x, y, c = lax.axis_index("x"), lax.axis_index("y"), lax.axis_index("c")
        sibling = (x, y, 1 - c)
        sends = []
        for k in range(n_slots):
            for s in range(D2D_STREAMS):
                theirs = pl.ds(pl.multiple_of((1 - c) * half + s * piece, 16), piece)
                i = k * D2D_STREAMS + s
                cp = _remote(src.at[k, theirs], out.at[k, pl.ds(s * piece, piece)], send_sems.at[i], recv_sems.at[i], sibling)
                cp.start()
                sends.append(cp)
        for cp in sends:
            cp.wait_recv()
        for cp in sends:
            cp.wait_send()

    n = n_slots * D2D_STREAMS
    return pl.pallas_call(
        body, name=name, in_specs=[_ANY], out_specs=_ANY, out_shape=jax.ShapeDtypeStruct((n_slots, half, cols), parts.dtype),
        scratch_shapes=[pltpu.SemaphoreType.DMA((n,)), pltpu.SemaphoreType.DMA((n,))],
    )(parts)


def _other_half_copies(src, land, send_sems, recv_sems):
    n_slots, rows, _ = src.shape
    half = rows // 2
    piece = half // D2D_STREAMS
    x, y, c = lax.axis_index("x"), lax.axis_index("y"), lax.axis_index("c")
    copies = []
    for k in range(n_slots):
        for s in range(D2D_STREAMS):
            theirs = pl.ds(pl.multiple_of((1 - c) * half + s * piece, 16), piece)
            i = k * D2D_STREAMS + s
            copies.append(_remote(src.at[k, theirs], land.at[k, pl.ds(s * piece, piece)], send_sems.at[i], recv_sems.at[i],
                                  (x, y, 1 - c)))
    return copies


def _core_send_other_half_start(parts, name):
    n_slots, rows, cols = parts.shape
    assert rows % (2 * D2D_STREAMS * 16) == 0
    n = n_slots * D2D_STREAMS
    land_shape = (n_slots, rows // 2, cols)

    def body(src, land, send_sems, recv_sems, src_out, land_out, token):
        for cp in _other_half_copies(src, land, send_sems, recv_sems):
            cp.start()
        token[...] = jnp.zeros_like(token)

    return pl.pallas_call(
        body, name=name,
        out_shape=(pltpu.SemaphoreType.DMA((n,)), pltpu.SemaphoreType.DMA((n,)), pltpu.HBM(parts.shape, parts.dtype),
                   pltpu.HBM(land_shape, parts.dtype), jax.ShapeDtypeStruct((8, LANES), f32)),
        in_specs=(_HBM, _HBM), out_specs=(_SEM, _SEM, _HBM, _HBM, pl.BlockSpec(memory_space=pltpu.VMEM)),
        input_output_aliases={0: 2, 1: 3}, compiler_params=pltpu.CompilerParams(has_side_effects=_EFFECT),
    )(pltpu.with_memory_space_constraint(parts, pltpu.HBM),
      pltpu.with_memory_space_constraint(lax.empty(land_shape, parts.dtype), pltpu.HBM))


def _core_send_other_half_wait(send_sems, recv_sems, parts, land, after, name):
    def body(src, land_ref, send_ref, recv_ref, after_ref, src_dead, land_out):
        for cp in _other_half_copies(src, land_ref, send_ref, recv_ref):
            cp.wait_send()
            cp.wait_recv()

    return pl.pallas_call(
        body, name=name, out_shape=(pltpu.HBM(parts.shape, parts.dtype), pltpu.HBM(land.shape, land.dtype)),
        in_specs=(_HBM, _HBM, _SEM, _SEM, _ANY), out_specs=(_HBM, _HBM), input_output_aliases={0: 0, 1: 1},
        compiler_params=pltpu.CompilerParams(has_side_effects=_EFFECT),
    )(parts, land, send_sems, recv_sems, after)


def _add_my_half(parts, other, core, name, block_rows=256):
    n_slots, rows, cols = parts.shape
    half = rows // 2
    br = _pick(half, block_rows, 16)
    nb = half // br

    def body(c_ref, a_ref, b_ref, o_ref):
        o_ref[...] = (a_ref[...].astype(f32) + b_ref[...].astype(f32)).astype(o_ref.dtype)

    grid_spec = pltpu.PrefetchScalarGridSpec(
        num_scalar_prefetch=1, grid=(n_slots, nb),
        in_specs=[pl.BlockSpec((1, br, cols), lambda k, i, c: (k, c[0] * nb + i, 0)),
                  pl.BlockSpec((1, br, cols), lambda k, i, c: (k, i, 0))],
        out_specs=pl.BlockSpec((1, br, cols), lambda k, i, c: (k, i, 0)))
    return pl.pallas_call(
        body, name=name, grid_spec=grid_spec, out_shape=jax.ShapeDtypeStruct((n_slots, half, cols), parts.dtype),
        compiler_params=_params(("parallel", "parallel")),
    )(core, parts, other)


def _core_join_halves(mine, name):
    half, cols = mine.shape
    piece = half // D2D_STREAMS
    assert half % (D2D_STREAMS * 16) == 0

    def body(src, out, send_sems, recv_sems, *stage):
        x, y, c = lax.axis_index("x"), lax.axis_index("y"), lax.axis_index("c")
        sibling = (x, y, 1 - c)
        sends = []
        for s in range(D2D_STREAMS):
            dst = out.at[pl.ds(pl.multiple_of(c * half + s * piece, 16), piece)]
            cp = _remote(src.at[pl.ds(s * piece, piece)], dst, send_sems.at[s], recv_sems.at[s], sibling)
            cp.start()
            sends.append(cp)
        _staged_copy(src, out.at[pl.ds(pl.multiple_of(c * half, 16), half)], *stage)
        for s in range(D2D_STREAMS):
            dst = out.at[pl.ds(pl.multiple_of((1 - c) * half + s * piece, 16), piece)]
            _remote(src.at[pl.ds(s * piece, piece)], dst, send_sems.at[s], recv_sems.at[s], sibling).wait_recv()
        for cp in sends:
            cp.wait_send()

    return pl.pallas_call(
        body, name=name, in_specs=[_ANY], out_specs=_ANY, out_shape=jax.ShapeDtypeStruct((2 * half, cols), mine.dtype),
        scratch_shapes=[pltpu.SemaphoreType.DMA((D2D_STREAMS,)), pltpu.SemaphoreType.DMA((D2D_STREAMS,))]
        + _stage_scratch(half, cols, mine.dtype),
    )(mine)


def _reduce_start(exchanging, core, after, tag):
    parts, other = _core_send_other_half_wait(*exchanging[:4], after, f"exchange_wait_{tag}")
    return _chip_scatter_start(_add_my_half(parts, other, core, f"sum_core_halves_{tag}"), f"scatter_start_{tag}")


def _reduce_finish(started, chip, after, tag):
    send_sems, recv_sems, chip_part, land, _ = started
    own, landed = _chip_scatter_wait(send_sems, recv_sems, chip_part, land, after, f"scatter_wait_{tag}")
    return _core_join_halves(_sum_slots_own(landed, own, chip, f"sum_chip_parts_{tag}"), f"join_core_halves_{tag}")


def _reduce_to_chips(parts, core, tag):
    chip_part = _add_my_half(parts, _core_send_other_half(parts, f"exchange_core_halves_{tag}"), core, f"sum_core_halves_{tag}")
    my_sum = _sum_slots(_chip_scatter(chip_part, f"scatter_{tag}"), f"sum_chip_parts_{tag}")
    return _core_join_halves(my_sum, f"join_core_halves_{tag}")


def _sum_slots(stack, name, block_rows=256):
    s_n, r_n, c_n = stack.shape
    br = _pick(r_n, block_rows, 8)

    def body(in_ref, o_ref):
        acc = in_ref[0].astype(f32)
        for s in range(1, s_n):
            acc = acc + in_ref[s].astype(f32)
        o_ref[...] = acc

    return pl.pallas_call(
        body, name=name, grid=(r_n // br,), in_specs=[pl.BlockSpec((s_n, br, c_n), lambda i: (0, i, 0))],
        out_specs=pl.BlockSpec((br, c_n), lambda i: (i, 0)), out_shape=jax.ShapeDtypeStruct((r_n, c_n), f32),
        compiler_params=_params(("parallel",)),
    )(stack)


def _concat_padded(parts, mult):
    rows = sum(p.shape[0] for p in parts)
    pad = (-rows) % mult
    if pad:
        parts = parts + [jnp.zeros((pad,) + parts[0].shape[1:], parts[0].dtype)]
    return jnp.concatenate(parts, axis=0)


def _pack_weights(w, l, names, conv_w=None):
    parts = [w[n][l].astype(bf16).reshape(-1, PACK_COLS) for n in names]
    if conv_w is not None:
        parts.append(_concat_padded([lax.bitcast_convert_type(conv_w, bf16).reshape(-1, PACK_COLS)], 16))
    return _concat_padded(parts, PACK_ROW_MULT)


def _unpack_weights(full, w, l, names, conv_w=None):
    start, r0 = {}, 0
    for n in names:
        start[n] = r0
        r0 += w[n][l].size // PACK_COLS

    def shards(n):
        rows = w[n][l].size // PACK_COLS
        return [full[k, start[n]:start[n] + rows].reshape(w[n].shape[1:]) for k in range(N_CHIPS)]

    if conv_w is None:
        return shards, None
    rows = conv_w.size * 2 // PACK_COLS
    pieces = lax.bitcast_convert_type(full[:, r0:r0 + rows].reshape((N_CHIPS,) + conv_w.shape + (2,)), f32)
    return shards, jnp.concatenate([pieces[k] for k in range(N_CHIPS)], axis=2)


def _pack_big_grads(layer_grads):
    parts, slot_rows = [], 0
    for k in range(N_CHIPS):
        slot = []
        for n in BIG:
            for g in layer_grads:
                width = g[n].shape[0] // N_CHIPS
                slot.append(g[n][k * width:(k + 1) * width].astype(bf16).reshape(-1, PACK_COLS))
        slot_rows = sum(p.shape[0] for p in slot)
        pad = (-slot_rows) % PACK_ROW_MULT
        if pad:
            slot.append(jnp.zeros((pad, PACK_COLS), bf16))
        slot_rows += pad
        parts += slot
    return jnp.concatenate(parts, axis=0).reshape(N_CHIPS, slot_rows, PACK_COLS)


def _unpack_big_grads(summed, w):
    out, r0 = {}, 0
    for n in BIG:
        rows = w[n][0].size // PACK_COLS
        out[n] = jnp.stack([s[r0:r0 + rows].reshape(w[n].shape[1:]) for s in summed])
        r0 += rows
    return out


_SMALL_TILE = 8 * LANES


def _pack_small(vals, names):
    parts = []
    for n in names:
        pieces = vals[n] if isinstance(vals[n], list) else [vals[n]]
        size = sum(p.size for p in pieces)
        if all(p.size % _SMALL_TILE == 0 for p in pieces):
            parts += [p.reshape(-1, LANES) for p in pieces]
        else:
            flat = [p.reshape(-1) for p in pieces] + [jnp.zeros(((-size) % _SMALL_TILE,), f32)]
            parts.append(jnp.concatenate(flat).reshape(-1, LANES))
    return _concat_padded(parts, PACK_ROW_MULT)


def _unpack_small(packed, like, names):
    out, r0 = {}, 0
    for n in names:
        size = like[n].size
        rows = -(-size // _SMALL_TILE) * 8
        out[n] = packed[r0:r0 + rows].reshape(-1)[:size].reshape(like[n].shape)
        r0 += rows
    return out


def _dims(w, x):
    d = {}
    d["D"] = x.shape[-1]
    d["T"] = x.shape[-2]
    d["DI"] = w["ssd_norm_g"].shape[-1]
    d["NH"] = w["ssd_dt_bias"].shape[-1]
    d["CD"] = w["ssd_conv_b"].shape[-1]
    d["G"] = SSD_N_GROUPS
    d["HPG"] = d["NH"] // d["G"]
    d["P"] = d["DI"] // d["NH"]
    d["N"] = (d["CD"] - d["DI"]) // (2 * d["G"])
    d["S5G"], d["S5N"] = w["s5_lambda_re"].shape[-2:]
    d["S5C"] = w["s5_b_re"].shape[-1]
    d["S5W"] = d["S5G"] * d["S5C"]
    d["NSB"] = d["S5W"] // S5_SUPERBLOCK
    d["GSB"] = d["S5G"] // d["NSB"]
    return d


def _head_pad(v, d):
    lead = v.shape[:-1]
    v = v.reshape(lead + (d["G"], d["HPG"]))
    v = jnp.concatenate([v, jnp.zeros(lead + (d["G"], LANES - d["HPG"]), v.dtype)], axis=-1)
    return v.reshape(lead + (d["G"] * LANES,))


def _head_unpad(v, d):
    lead = v.shape[:-1]
    return v.reshape(lead + (d["G"], LANES))[..., :d["HPG"]].reshape(lead + (d["NH"],))


def _w_in_perm(shards, d):
    o, nh = d["DI"] + d["CD"], d["NH"]
    r = shards[0].shape[0]

    def rows(lo, hi):
        out = []
        for k, s in enumerate(shards):
            a, b = max(lo, k * r), min(hi, (k + 1) * r)
            if a < b:
                out.append(s[a - k * r:b - k * r])
        return out

    dt = _head_pad(jnp.concatenate(rows(o, o + nh), axis=0).T, d).T
    return jnp.concatenate(rows(0, o) + rows(o + nh, len(shards) * r) + [dt], axis=0)


def _w_in_unperm(g, d):
    o = d["DI"] + d["CD"]
    rest = d["S5W"] + 2 * d["D"]
    return jnp.concatenate([g[:o], _head_unpad(g[o + rest:].T, d).T, g[o:o + rest]], axis=0)


def _s5_block_diag(v, d):
    gsb = d["GSB"]
    g, a, b = v.shape
    row_group = (lax.broadcasted_iota(jnp.int32, (g * a, gsb * b), 0) // a) % gsb
    col_group = lax.broadcasted_iota(jnp.int32, (g * a, gsb * b), 1) // b
    return jnp.where(row_group == col_group, jnp.tile(v.reshape(g * a, b), (1, gsb)), 0)


def _s5_diag_blocks(m, d, a, b):
    gsb = d["GSB"]
    rows = m.shape[0]
    m = m.reshape(rows, gsb, b)
    row_group = (lax.broadcasted_iota(jnp.int32, (rows, gsb, 1), 0) // a) % gsb
    col_group = lax.broadcasted_iota(jnp.int32, (rows, gsb, 1), 1)
    return jnp.sum(jnp.where(row_group == col_group, m, 0), axis=1).reshape(rows // a, a, b)


def _s5_lam_rows(v, d):
    v = v.reshape(d["NSB"], 1, d["GSB"] * d["S5N"])
    return jnp.broadcast_to(v, (d["NSB"], 8, v.shape[-1])).reshape(d["NSB"] * 8, -1)


def _ffn_fwd(h, pre_g, post_g, wgu, wd, tag):
    D = h.shape[1]
    H2 = wgu.shape[0]
    xn = _row_kernel(f"{tag}_norm", _fwd_of(_f_norm), [(h, D, 0)], [pre_g], [(D, bf16)])[0]
    ab = _mm(xn, wgu, "nt", bf16, f"{tag}_mm_up")
    hid = _row_kernel(f"{tag}_swiglu", _swiglu_fwd, [(ab, H2, 0)], [], [(H2 // 2, bf16)])[0]
    f = _mm(hid, wd, "nn", f32, f"{tag}_mm_down")
    out = _row_kernel(f"{tag}_resnorm", _fwd_of(_f_resnorm(0.5)), [(h, D, 0), (f, D, 0)], [post_g], [(D, f32)])[0]
    return out, dict(h=h, xn=xn, ab=ab, hid=hid, f=f)


def _ffn_bwd(dh_out, s, pre_g, post_g, wgu, wd, tag):
    D = dh_out.shape[1]
    H2 = wgu.shape[0]
    df, dpost = _row_kernel(f"{tag}_resnorm_bwd", _vjp_of(_f_post(0.5), 1, 1, [0]), [(s["f"], D, 0), (dh_out, D, 0)],
                            [post_g], [(D, bf16)], [post_g.shape])
    dwd = _mm(s["hid"], df, "tn", bf16, f"{tag}_mm_dwd")
    dhid = _mm(df, wd, "nt", bf16, f"{tag}_mm_dhid")
    dab = _row_kernel(f"{tag}_swiglu_bwd", _swiglu_bwd, [(s["ab"], H2, 0), (dhid, H2 // 2, 0)], [], [(H2, bf16)])[0]
    dwgu = _mm(dab, s["xn"], "tn", bf16, f"{tag}_mm_dwgu")
    dxn = _mm(dab, wgu, "nn", f32, f"{tag}_mm_dxn")
    dh, dpre = _row_kernel(f"{tag}_norm_bwd", _vjp_of(_f_norm, 1, 1, [0], 1), [(s["h"], D, 0), (dxn, D, 0), (dh_out, D, 0)],
                           [pre_g], [(D, f32)], [pre_g.shape])
    return dh, dict(pre_g=dpre, post_g=dpost, wgu=dwgu, wd=dwd)


def _mixer_fwd(h, p, d):
    D, DI, CD, G, N = d["D"], d["DI"], d["CD"], d["G"], d["N"]
    gl = G * LANES
    c_u5, c_ga, c_gb, c_dt = DI + CD, DI + CD + d["S5W"], DI + CD + d["S5W"] + D, DI + CD + d["S5W"] + 2 * D
    u = _row_kernel("mix_norm", _fwd_of(_f_norm), [(h, D, 0)], [p["mix_pre_g"]], [(D, bf16)])[0]
    proj = _mm(u, p["w_in"], "nt", f32, "mix_mm_in", bn_t=512)
    act = _conv_fwd(proj, DI, p["conv_w"], p["conv_b"], "ssd_conv")
    dt, adt = _row_kernel("ssd_dt", _fwd_of(_f_dt), [(proj, gl, c_dt // gl)], [p["dt_bias"], p["a_log"]], [(gl, f32)] * 2)
    y_ssd, states = _ssd_fwd(act, dt, adt, p["d_skip"], d["HPG"], d["P"], N, "ssd_scan")
    nrm = _row_kernel("ssd_post", _fwd_of(_f_ssdpost(G)), [(y_ssd, DI, 0), (proj, DI, 0)], [p["norm_g"]], [(DI, bf16)])[0]
    y_a = _mm(nrm, p["w_a"], "nn", f32, "mix_mm_a")
    u5 =(proj, d["S5W"], c_u5 // d["S5W"])
    bu = _s5_in(proj, c_u5, p["bsb"], d)
    s5st = _s5_scan_fwd(bu, p["lam_re_rows"], p["lam_im_rows"], d["NSB"], "s5_scan")
    y5 = _bdmm(s5st, p["csb"], "nn", d["NSB"], f32, "s5_mm_c")
    gel = _row_kernel("s5_post", _fwd_of(_f_s5post), [(y5, d["S5W"], 0), u5], [p["s5_d"]], [(d["S5W"], bf16)])[0]
    vg = _mm(gel, p["w_glu"], "nt", bf16, "mix_mm_glu")
    glu = _row_kernel("s5_glu", _glu_fwd, [(vg, vg.shape[1], 0)], [], [(vg.shape[1] // 2, bf16)])[0]
    y_b = _mm(glu, p["w_b"], "nn", f32, "mix_mm_b")
    merged = _row_kernel("mix_merge", _fwd_of(_f_merge), [(proj, D, c_ga // D), (y_a, D, 0), (proj, D, c_gb // D), (y_b, D, 0)],
                         [], [(D, bf16)])[0]
    m = _mm(merged, p["w_out"], "nn", f32, "mix_mm_out")
    out = _row_kernel("mix_resnorm", _fwd_of(_f_resnorm(1.0)), [(h, D, 0), (m, D, 0)], [p["mix_post_g"]], [(D, f32)])[0]
    return out, dict(h=h, u=u, proj=proj, act=act, dt=dt, adt=adt, states=states, y_ssd=y_ssd, nrm=nrm, y_a=y_a, s5st=s5st,
                     y5=y5, gel=gel, vg=vg, glu=glu, y_b=y_b, merged=merged, m=m)


def _s5_in(proj, c_u5, bsb, d):
    T = proj.shape[0]
    nsb = d["NSB"]
    ka, nw = S5_SUPERBLOCK, bsb.shape[1]
    off = c_u5 // ka
    assert c_u5 % ka == 0
    bt = _pick(T, 512)

    def body(a_ref, w_ref, o_ref):
        o_ref[...] = _dot(a_ref[...].astype(bf16), w_ref[...].astype(bf16))

    return pl.pallas_call(
        body, name="s5_mm_bu", grid=(nsb, T // bt),
        in_specs=[pl.BlockSpec((bt, ka), lambda j, i: (i, off + j)), pl.BlockSpec((ka, nw), lambda j, i: (j, 0))],
        out_specs=pl.BlockSpec((bt, nw), lambda j, i: (i, j)), out_shape=jax.ShapeDtypeStruct((T, nsb * nw), f32),
        compiler_params=_params(("parallel", "parallel")),
    )(proj, bsb)


def _s5_dbsb(proj, c_u5, a, d):
    T = proj.shape[0]
    nsb = d["NSB"]
    ka, nw = S5_SUPERBLOCK, a.shape[1] // nsb
    off = c_u5 // ka
    bt = _pick(T, 512)

    def body(u_ref, a_ref, o_ref):
        pr = _dot(u_ref[...].astype(bf16), a_ref[...].astype(bf16), "tn")
        k = pl.program_id(1)

        @pl.when(k == 0)
        def _():
            o_ref[...] = pr

        @pl.when(k > 0)
        def _():
            o_ref[...] += pr

    return pl.pallas_call(
        body, name="s5_mm_dbsb", grid=(nsb, T // bt),
        in_specs=[pl.BlockSpec((bt, ka), lambda j, k: (k, off + j)), pl.BlockSpec((bt, nw), lambda j, k: (k, j))],
        out_specs=pl.BlockSpec((ka, nw), lambda j, k: (j, 0)), out_shape=jax.ShapeDtypeStruct((nsb * ka, nw), f32),
        compiler_params=_params(("parallel", "arbitrary")),
    )(proj, a)


def _mixer_bwd(dh_out, s, p, d):
    D, DI, CD, G, N, S5W = d["D"], d["DI"], d["CD"], d["G"], d["N"], d["S5W"]
    gl = G * LANES
    gn = G * N
    c_u5, c_ga, c_gb, c_dt = DI + CD, DI + CD + S5W, DI + CD + S5W + D, DI + CD + S5W + 2 * D
    proj = s["proj"]
    g = {}
    dm, g["mix_post_g"] = _row_kernel("mix_resnorm_bwd", _vjp_of(_f_post(1.0), 1, 1, [0]), [(s["m"], D, 0), (dh_out, D, 0)],
                                      [p["mix_post_g"]], [(D, bf16)], [p["mix_post_g"].shape])
    g["w_out"] = _mm(s["merged"], dm, "tn", bf16, "mix_mm_dwout")
    dmerged = _mm(dm, p["w_out"], "nt", f32, "mix_mm_dmerged")
    dga, dya, dgb, dyb = _row_kernel(
        "mix_merge_bwd", _vjp_of(_f_merge, 4, 1, [0, 1, 2, 3]),
        [(proj, D, c_ga // D), (s["y_a"], D, 0), (proj, D, c_gb // D), (s["y_b"], D, 0), (dmerged, D, 0)], [],
        [(D, bf16), (D, bf16), (D, bf16), (D, bf16)])
    g["w_a"] = _mm(s["nrm"], dya, "tn", bf16, "mix_mm_dwa")
    dnrm = _mm(dya, p["w_a"], "nt", f32, "mix_mm_dnrm")
    dy_ssd, dz, g["norm_g"] = _row_kernel(
        "ssd_post_bwd", _vjp_of(_f_ssdpost(G), 2, 1, [0, 1]), [(s["y_ssd"], DI, 0), (proj, DI, 0), (dnrm, DI, 0)],
        [p["norm_g"]], [(DI, f32), (DI, bf16)], [p["norm_g"].shape])
    dxs, d_b, d_c, ddt, dadt, dd = _ssd_bwd(s["act"], s["dt"], s["adt"], p["d_skip"], s["states"], dy_ssd,
                                            d["HPG"], d["P"], N, "ssd_scan_bwd")
    g["d_skip"] = dd.reshape(G, 8, LANES)[:, 0, :].reshape(1, gl)
    ddt_raw, g["dt_bias"], g["a_log"] = _row_kernel(
        "ssd_dt_bwd", _vjp_of(_f_dt, 1, 2, [0]), [(proj, gl, c_dt // gl), (ddt, gl, 0), (dadt, gl, 0)],
        [p["dt_bias"], p["a_log"]], [(gl, bf16)], [p["dt_bias"].shape, p["a_log"].shape])
    cw, cb = p["conv_w"], p["conv_b"]
    dxc_x, dw_x, db_x = _conv_bwd(proj, DI, cw[:, :DI], cb[:, :DI], dxs, "ssd_conv_bwd_x")
    dxc_b, dw_b, db_b = _conv_bwd(proj, 2 * DI, cw[:, DI:DI + gn], cb[:, DI:DI + gn], d_b, "ssd_conv_bwd_b")
    dxc_c, dw_c, db_c = _conv_bwd(proj, 2 * DI + gn, cw[:, DI + gn:], cb[:, DI + gn:], d_c, "ssd_conv_bwd_c")
    g["conv_w"] = jnp.concatenate([dw_x, dw_b, dw_c], axis=1)
    g["conv_b"] = jnp.concatenate([db_x, db_b, db_c], axis=1)
    g["w_b"] = _mm(s["glu"], dyb, "tn", bf16, "mix_mm_dwb")
    dglu = _mm(dyb, p["w_b"], "nt", f32, "mix_mm_dglu")
    dvg = _row_kernel("s5_glu_bwd", _glu_bwd, [(s["vg"], s["vg"].shape[1], 0), (dglu, S5W, 0)], [], [(s["vg"].shape[1], bf16)])[0]
    g["w_glu"] = _mm(dvg, s["gel"], "tn", bf16, "mix_mm_dwglu")
    dgel = _mm(dvg, p["w_glu"], "nn", f32, "mix_mm_dgel")
    dy5, du5a, g["s5_d"] = _row_kernel(
        "s5_post_bwd", _vjp_of(_f_s5post, 2, 1, [0, 1]), [(s["y5"], S5W, 0), (proj, S5W, c_u5 // S5W), (dgel, S5W, 0)],
        [p["s5_d"]], [(S5W, bf16), (S5W, f32)], [p["s5_d"].shape])
    g["csb"] = _bdmm(s["s5st"], dy5, "tn", d["NSB"], f32, "s5_mm_dcsb")
    gst = _bdmm(dy5, p["csb"], "nt", d["NSB"], f32, "s5_mm_gst")
    a, g["lam_re_rows"], g["lam_im_rows"] = _s5_scan_bwd(gst, s["s5st"], p["lam_re_rows"], p["lam_im_rows"], d["NSB"], "s5_scan_bwd")
    g["bsb"] = _s5_dbsb(proj, c_u5, a, d)
    du5b = _bdmm(a, p["bsb"], "nt", d["NSB"], f32, "s5_mm_du5")
    du5 = _row_kernel("s5_du5", _add_fn, [(du5a, S5W, 0), (du5b, S5W, 0)], [], [(S5W, bf16)])[0]
    dproj = jnp.concatenate([dz, dxc_x, dxc_b, dxc_c, du5, dga, dgb, ddt_raw], axis=1)
    g["w_in"] = _mm(dproj, s["u"], "tn", bf16, "mix_mm_dwin")
    du = _mm(dproj, p["w_in"], "nn", f32, "mix_mm_du", bk_t=2176)
    dh, g["mix_pre_g"] = _row_kernel("mix_norm_bwd", _vjp_of(_f_norm, 1, 1, [0], 1), [(s["h"], D, 0), (du, D, 0), (dh_out, D, 0)],
                                     [p["mix_pre_g"]], [(D, f32)], [p["mix_pre_g"].shape])
    return dh, g


def _ffn1_params(l, w, wf):
    whole = lambda *names: jnp.concatenate([s for n in names for s in wf(n)], axis=0)
    return dict(ffn1_pre_g=w["ffn1_pre_g"][l].reshape(1, -1), ffn1_post_g=w["ffn1_post_g"][l].reshape(1, -1),
                wgu1=whole("ffn1_w_gate", "ffn1_w_up"), wd1=whole("ffn1_w_down"))


def _layer_params(l, w, wf, conv_w_full, d):
    r2 = lambda v: v[l].reshape(1, -1)
    p = {}
    for n in ["mix_pre_g", "mix_post_g", "ffn2_pre_g", "ffn2_post_g", "s5_d"]:
        p[n] = r2(w[n])
    whole = lambda *names: jnp.concatenate([s for n in names for s in wf(n)], axis=0)
    p["wgu2"] = whole("ffn2_w_gate", "ffn2_w_up")
    p["wd2"] = whole("ffn2_w_down")
    p["w_in"] = _w_in_perm(wf("w_in"), d)
    p["w_a"], p["w_glu"], p["w_b"], p["w_out"] = whole("w_branch_a"), whole("s5_w_glu"), whole("w_branch_b"), whole("w_out")
    p["conv_w"] = conv_w_full[l]
    p["conv_b"] = r2(w["ssd_conv_b"])
    p["dt_bias"] = _head_pad(r2(w["ssd_dt_bias"]), d)
    p["a_log"] = _head_pad(r2(w["ssd_a_log"]), d)
    p["d_skip"] = _head_pad(r2(w["ssd_d"]), d)
    p["norm_g"] = r2(w["ssd_norm_g"])
    g5, n5, c5 = d["S5G"], d["S5N"], d["S5C"]
    expand = jnp.repeat(jnp.eye(n5, dtype=f32), c5, axis=1)
    prep_in = [w["s5_lambda_re"][l], w["s5_lambda_im"][l], w["s5_log_step"][l].reshape(g5, 1),
               w["s5_b_re"][l].reshape(g5, n5 * c5), w["s5_b_im"][l].reshape(g5, n5 * c5), expand]
    lbr, lbi, bbr, bbi = _s5_prep(prep_in, "s5_prep")
    p["s5_prep_in"] = prep_in
    p["lam_re_rows"], p["lam_im_rows"] = _s5_lam_rows(lbr, d), _s5_lam_rows(lbi, d)
    to_cn = lambda v: v.astype(bf16).reshape(g5, n5, c5).transpose(0, 2, 1)
    p["bsb"] = jnp.concatenate([_s5_block_diag(to_cn(bbr), d), _s5_block_diag(to_cn(bbi), d)], axis=1)
    c_re = w["s5_c_re"][l].astype(bf16).transpose(0, 2, 1)
    c_im = w["s5_c_im"][l].astype(bf16).transpose(0, 2, 1)
    nsb = d["NSB"]
    csb = jnp.stack([_s5_block_diag(c_re, d).reshape(nsb, -1, S5_SUPERBLOCK),
                     _s5_block_diag(-c_im, d).reshape(nsb, -1, S5_SUPERBLOCK)], axis=1)
    p["csb"] = csb.reshape(-1, S5_SUPERBLOCK)
    return p


def _s5_param_grads(g, p, d, l):
    g5, n5, c5, nsb, gsb = d["S5G"], d["S5N"], d["S5C"], d["NSB"], d["GSB"]
    wst = gsb * n5
    dbsb = g["bsb"]
    from_cn = lambda v: v.transpose(0, 2, 1).reshape(g5, n5 * c5)
    dbbr = from_cn(_s5_diag_blocks(dbsb[:, :wst], d, c5, n5))
    dbbi = from_cn(_s5_diag_blocks(dbsb[:, wst:], d, c5, n5))
    rows = lambda v: v.reshape(nsb, 8, wst)[:, 0, :].reshape(g5, n5)
    cots = [rows(g["lam_re_rows"]), rows(g["lam_im_rows"]), dbbr, dbbi]
    dlr, dli, dls, dbr, dbi = _s5_prep_bwd(p["s5_prep_in"], cots, "s5_prep_bwd")
    dcsb = g["csb"].reshape(nsb, 2, wst, S5_SUPERBLOCK)
    dcr = _s5_diag_blocks(dcsb[:, 0].reshape(-1, S5_SUPERBLOCK), d, n5, c5).transpose(0, 2, 1)
    dci = -_s5_diag_blocks(dcsb[:, 1].reshape(-1, S5_SUPERBLOCK), d, n5, c5).transpose(0, 2, 1)
    return dict(s5_lambda_re=dlr, s5_lambda_im=dli, s5_log_step=dls.reshape(g5), s5_b_re=dbr.reshape(g5, n5, c5),
                s5_b_im=dbi.reshape(g5, n5, c5), s5_c_re=dcr, s5_c_im=dci)


def kernel(x, ffn1_pre_g, ffn1_post_g, ffn1_w_gate, ffn1_w_up, ffn1_w_down, mix_pre_g, mix_post_g, w_in, ssd_conv_w, ssd_conv_b, ssd_dt_bias, ssd_a_log, ssd_d, ssd_norm_g, w_branch_a, s5_lambda_re, s5_lambda_im, s5_b_re, s5_b_im, s5_c_re, s5_c_im, s5_log_step, s5_d, s5_w_glu, w_branch_b, w_out, ffn2_pre_g, ffn2_post_g, ffn2_w_gate, ffn2_w_up, ffn2_w_down, loss_target, m_ffn1_pre_g, m_ffn1_post_g, m_ffn1_w_gate, m_ffn1_w_up, m_ffn1_w_down, m_mix_pre_g, m_mix_post_g, m_w_in, m_ssd_conv_w, m_ssd_conv_b, m_ssd_dt_bias, m_ssd_a_log, m_ssd_d, m_ssd_norm_g, m_w_branch_a, m_s5_lambda_re, m_s5_lambda_im, m_s5_b_re, m_s5_b_im, m_s5_c_re, m_s5_c_im, m_s5_log_step, m_s5_d, m_s5_w_glu, m_w_branch_b, m_w_out, m_ffn2_pre_g, m_ffn2_post_g, m_ffn2_w_gate, m_ffn2_w_up, m_ffn2_w_down, v_ffn1_pre_g, v_ffn1_post_g, v_ffn1_w_gate, v_ffn1_w_up, v_ffn1_w_down, v_mix_pre_g, v_mix_post_g, v_w_in, v_ssd_conv_w, v_ssd_conv_b, v_ssd_dt_bias, v_ssd_a_log, v_ssd_d, v_ssd_norm_g, v_w_branch_a, v_s5_lambda_re, v_s5_lambda_im, v_s5_b_re, v_s5_b_im, v_s5_c_re, v_s5_c_im, v_s5_log_step, v_s5_d, v_s5_w_glu, v_w_branch_b, v_w_out, v_ffn2_pre_g, v_ffn2_post_g, v_ffn2_w_gate, v_ffn2_w_up, v_ffn2_w_down):
    given = dict(locals())
    for n in COL_SHARDED:
        for prefix in ("", "m_", "v_"):
            given[prefix + n] = given[prefix + n].transpose(0, 2, 1)
    w = {n: given[n] for n in WEIGHTS}
    mom = {n: given["m_" + n] for n in WEIGHTS}
    var = {n: given["v_" + n] for n in WEIGHTS}
    d = _dims(w, x)
    n_layers = w["ffn1_pre_g"].shape[0]
    T, D = d["T"], d["D"]

    conv_w = w["ssd_conv_w"]
    first, rest = BIG[:3], BIG[3:]
    wf, _ = _unpack_weights(_chip_all_gather(_pack_weights(w, 0, first), "gather_weights_first"), w, 0, first)
    coming = _chip_gather_start(_pack_weights(w, 0, rest, conv_w), "gather_start_l0")
    h = x.reshape(T, D) + coming[-1][0, 0]
    saved, layers = [], []
    for l in range(n_layers):
        if l > 0:
            pack, land = _chip_gather_wait(*coming[:4], h, f"gather_wait_l{l}")
            wf, _ = _unpack_weights(_chip_gather_finish(pack, land, f"gather_finish_l{l}"), w, l, BIG)
        p = _ffn1_params(l, w, wf)
        h, s1 = _ffn_fwd(h, p["ffn1_pre_g"], p["ffn1_post_g"], p["wgu1"], p["wd1"], "ffn1")
        if l == 0:
            pack, land = _chip_gather_wait(*coming[:4], h, "gather_wait_l0")
            wf, conv_w_full = _unpack_weights(_chip_gather_finish(pack, land, "gather_finish_l0"), w, 0, rest, conv_w)
        if l + 1 < n_layers:
            coming = _chip_gather_start(_pack_weights(w, l + 1, BIG), f"gather_start_l{l + 1}")
            h = h + coming[-1][0, 0]
        p.update(_layer_params(l, w, wf, conv_w_full, d))
        layers.append(p)
        h, sm = _mixer_fwd(h, p, d)
        h, s2 = _ffn_fwd(h, p["ffn2_pre_g"], p["ffn2_post_g"], p["wgu2"], p["wd2"], "ffn2")
        saved.append((s1, sm, s2))
    dh, loss_part = _row_kernel("loss", _loss_fn, [(h, D, 0), (loss_target.reshape(T, D), D, 0)], [], [(D, f32)], [(8, LANES)])
    loss = lax.psum(loss_part[0, 0], ("x", "y", "c"))

    my_core = lax.axis_index("c").astype(jnp.int32).reshape(1)
    my_chip = (2 * lax.axis_index("x") + lax.axis_index("y")).astype(jnp.int32).reshape(1)
    lg, exchanging, in_flight = [None] * n_layers, [None] * n_layers, [None] * n_layers
    for l in reversed(range(n_layers)):
        p = layers[l]
        s1, sm, s2 = saved[l]
        dh, g2 = _ffn_bwd(dh, s2, p["ffn2_pre_g"], p["ffn2_post_g"], p["wgu2"], p["wd2"], "ffn2")
        if l + 1 < n_layers:
            in_flight[l + 1] = _reduce_start(exchanging[l + 1], my_core, dh, f"grads_l{l + 1}")
            dh = dh + in_flight[l + 1][-1][0, 0]
        dh, gm = _mixer_bwd(dh, sm, p, d)
        dh, g1 = _ffn_bwd(dh, s1, p["ffn1_pre_g"], p["ffn1_post_g"], p["wgu1"], p["wd1"], "ffn1")
        H = p["wd1"].shape[0]
        gl = dict(ffn1_pre_g=g1["pre_g"], ffn1_post_g=g1["post_g"], ffn1_w_gate=g1["wgu"][:H], ffn1_w_up=g1["wgu"][H:],
                  ffn1_w_down=g1["wd"], ffn2_pre_g=g2["pre_g"], ffn2_post_g=g2["post_g"], ffn2_w_gate=g2["wgu"][:H],
                  ffn2_w_up=g2["wgu"][H:], ffn2_w_down=g2["wd"], mix_pre_g=gm["mix_pre_g"], mix_post_g=gm["mix_post_g"],
                  w_in=_w_in_unperm(gm["w_in"], d), ssd_conv_w=gm["conv_w"], ssd_conv_b=gm["conv_b"],
                  ssd_dt_bias=_head_unpad(gm["dt_bias"], d), ssd_a_log=_head_unpad(gm["a_log"], d),
                  ssd_d=_head_unpad(gm["d_skip"], d), ssd_norm_g=gm["norm_g"], w_branch_a=gm["w_a"], s5_d=gm["s5_d"],
                  s5_w_glu=gm["w_glu"], w_branch_b=gm["w_b"], w_out=gm["w_out"])
        gl.update(_s5_param_grads(gm, p, d, l))
        lg[l] = gl
        exchanging[l] = _core_send_other_half_start(_pack_big_grads([gl]), f"exchange_start_grads_l{l}")
        if l > 0:
            dh = dh + exchanging[l][-1][0, 0]
    grad_x = dh.reshape(x.shape)
    summed = [None] * n_layers
    for l in range(1, n_layers):
        summed[l] = _reduce_finish(in_flight[l], my_chip, exchanging[0][-1], f"grads_l{l}")
    in_flight[0] = _reduce_start(exchanging[0], my_core, summed[-1] if n_layers > 1 else grad_x, "grads_l0")
    small_names = SMALL + ["ssd_conv_w"]
    small_parts = {n: [g[n] for g in lg] for n in small_names}
    small_like = {n: jax.ShapeDtypeStruct((n_layers,) + lg[0][n].shape, f32) for n in small_names}
    small_like.update({n: w[n] for n in SMALL})
    small_pack = _pack_small(small_parts, small_names)
    small_sum = _reduce_to_chips(jnp.broadcast_to(small_pack, (N_CHIPS,) + small_pack.shape), my_core, "small")
    small = _unpack_small(small_sum, small_like, small_names)
    summed[0] = _reduce_finish(in_flight[0], my_chip, small_sum, "grads_l0")
    grads = _unpack_big_grads(summed, w)
    k_me = 2 * lax.axis_index("x") + lax.axis_index("y")
    cw = w["ssd_conv_w"].shape[-1]
    small["ssd_conv_w"] = lax.dynamic_slice_in_dim(small["ssd_conv_w"], k_me * cw, cw, axis=2)
    grads.update(small)

    delta, new_m, new_v = {}, {}, {}
    for n in BIG + ["ssd_conv_w"]:
        delta[n], new_m[n], new_v[n] = _adamw(w[n], grads[n], mom[n], var[n], "adamw_" + n)
    pw, pm, pv = [_pack_small(t, SMALL) for t in (w, mom, var)]
    assert pw.shape[0] <= small_sum.shape[0]
    sd, sm_, sv = _adamw(pw, small_sum[:pw.shape[0]], pm, pv, "adamw_small")
    delta.update(_unpack_small(sd, w, SMALL))
    new_m.update(_unpack_small(sm_, w, SMALL))
    new_v.update(_unpack_small(sv, w, SMALL))
    for n in COL_SHARDED:
        for out in (grads, delta, new_m, new_v):
            out[n] = out[n].transpose(0, 2, 1)
    return (loss, grad_x, *[grads[n] for n in WEIGHTS], *[delta[n] for n in WEIGHTS],
            *[new_m[n] for n in WEIGHTS], *[new_v[n] for n in WEIGHTS])
```

```python
import functools

import numpy as np
import jax
import jax.numpy as jnp
from jax import lax
from jax.experimental import pallas as pl
from jax.experimental.pallas import tpu as pltpu

f32, bf16 = jnp.float32, jnp.bfloat16

SSD_N_GROUPS = 4
SSD_CHUNK = 128
RMS_EPS = 1e-6
S5_MAX_REAL = -1e-4
S5_SUPERBLOCK = 256
ADAM_LR, ADAM_B1, ADAM_B2, ADAM_EPS, ADAM_WD, ADAM_STEP = 0.001, 0.9, 0.999, 1e-08, 0.01, 10

LANES = 128
PACK_COLS = 1024
D2D_STREAMS = 16
PACK_ROW_MULT = 2 * D2D_STREAMS * 16
VMEM_LIMIT_BYTES = 48 * 1024 * 1024
N_CHIPS, N_CORES, N_DEV = 4, 2, 8
MESH = pl.DeviceIdType.MESH

BIG = ["ffn1_w_gate", "ffn1_w_up", "ffn1_w_down", "w_in", "w_branch_a", "s5_w_glu", "w_branch_b", "w_out",
       "ffn2_w_gate", "ffn2_w_up", "ffn2_w_down"]
COL_SHARDED = ["ffn1_w_gate", "ffn1_w_up", "w_in", "s5_w_glu", "ffn2_w_gate", "ffn2_w_up"]
SMALL = ["ffn1_pre_g", "ffn1_post_g", "mix_pre_g", "mix_post_g", "ssd_conv_b", "ssd_norm_g", "s5_lambda_re", "s5_lambda_im",
         "s5_b_re", "s5_b_im", "s5_c_re", "s5_c_im", "s5_d", "ffn2_pre_g", "ffn2_post_g", "s5_log_step", "ssd_dt_bias",
         "ssd_a_log", "ssd_d"]
WEIGHTS = ["ffn1_pre_g", "ffn1_post_g", "ffn1_w_gate", "ffn1_w_up", "ffn1_w_down", "mix_pre_g", "mix_post_g", "w_in",
           "ssd_conv_w", "ssd_conv_b", "ssd_dt_bias", "ssd_a_log", "ssd_d", "ssd_norm_g", "w_branch_a", "s5_lambda_re",
           "s5_lambda_im", "s5_b_re", "s5_b_im", "s5_c_re", "s5_c_im", "s5_log_step", "s5_d", "s5_w_glu", "w_branch_b",
           "w_out", "ffn2_pre_g", "ffn2_post_g", "ffn2_w_gate", "ffn2_w_up", "ffn2_w_down"]


def _params(sem=None):
    return pltpu.CompilerParams(dimension_semantics=sem, vmem_limit_bytes=VMEM_LIMIT_BYTES)


def _pick(n, target, mult=LANES):
    best = None
    for d in range(mult, min(n, target) + 1, mult):
        if n % d == 0:
            best = d
    return best if best is not None else n


_DIMS = {"nn": (((1,), (0,)), ((), ())), "nt": (((1,), (1,)), ((), ())), "tn": (((0,), (0,)), ((), ()))}


def _mm(a, b, mode, out_dtype, name, bm_t=1024, bn_t=1024, bk_t=2816):
    if mode == "nn":
        (M, K), (K2, N) = a.shape, b.shape
    elif mode == "nt":
        (M, K), (N, K2) = a.shape, b.shape
    else:
        (K, M), (K2, N) = a.shape, b.shape
    assert K == K2, (name, a.shape, b.shape)
    bm, bn, bk = _pick(M, bm_t), _pick(N, bn_t), _pick(K, bk_t)
    nk = K // bk
    dn = _DIMS[mode]

    def body(a_ref, b_ref, o_ref, *scratch):
        p = lax.dot_general(a_ref[...].astype(bf16), b_ref[...].astype(bf16), dn, preferred_element_type=f32)
        if nk == 1:
            o_ref[...] = p.astype(o_ref.dtype)
        else:
            acc = scratch[0]
            k = pl.program_id(2)

            @pl.when(k == 0)
            def _():
                acc[...] = p

            @pl.when(k > 0)
            def _():
                acc[...] += p

            @pl.when(k == nk - 1)
            def _():
                o_ref[...] = acc[...].astype(o_ref.dtype)

    if mode == "tn":
        a_spec = pl.BlockSpec((bk, bm), lambda i, j, k: (k, i))
    else:
        a_spec = pl.BlockSpec((bm, bk), lambda i, j, k: (i, k))
    if mode == "nt":
        b_spec = pl.BlockSpec((bn, bk), lambda i, j, k: (j, k))
    else:
        b_spec = pl.BlockSpec((bk, bn), lambda i, j, k: (k, j))
    return pl.pallas_call(
        body, name=name, grid=(M // bm, N // bn, nk), in_specs=[a_spec, b_spec],
        out_specs=pl.BlockSpec((bm, bn), lambda i, j, k: (i, j)), out_shape=jax.ShapeDtypeStruct((M, N), out_dtype),
        scratch_shapes=[pltpu.VMEM((bm, bn), f32)] if nk > 1 else [],
        compiler_params=_params(("parallel", "parallel", "arbitrary")),
    )(a, b)


def _bdmm(a, w, mode, nb, out_dtype, name, bt_t=512):
    if mode == "tn":
        T = a.shape[0]
        ka, nw = a.shape[1] // nb, w.shape[1] // nb
        bt = _pick(T, bt_t)
        nt = T // bt

        def body_tn(a_ref, b_ref, o_ref):
            p = lax.dot_general(a_ref[...].astype(bf16), b_ref[...].astype(bf16), _DIMS["tn"], preferred_element_type=f32)
            k = pl.program_id(1)

            @pl.when(k == 0)
            def _():
                o_ref[...] = p

            @pl.when(k > 0)
            def _():
                o_ref[...] += p

        return pl.pallas_call(
            body_tn, name=name, grid=(nb, nt),
            in_specs=[pl.BlockSpec((bt, ka), lambda j, k: (k, j)), pl.BlockSpec((bt, nw), lambda j, k: (k, j))],
            out_specs=pl.BlockSpec((ka, nw), lambda j, k: (j, 0)), out_shape=jax.ShapeDtypeStruct((nb * ka, nw), f32),
            compiler_params=_params(("parallel", "arbitrary")),
        )(a, w)
    T = a.shape[0]
    ka, nw = w.shape[0] // nb, w.shape[1]
    bt = _pick(T, bt_t)
    kin, kout = (ka, nw) if mode == "nn" else (nw, ka)
    dn = _DIMS[mode]

    def body(a_ref, w_ref, o_ref):
        o_ref[...] = lax.dot_general(a_ref[...].astype(bf16), w_ref[...].astype(bf16), dn,
                                     preferred_element_type=f32).astype(o_ref.dtype)

    return pl.pallas_call(
        body, name=name, grid=(nb, T // bt),
        in_specs=[pl.BlockSpec((bt, kin), lambda j, i: (i, j)), pl.BlockSpec((ka, nw), lambda j, i: (j, 0))],
        out_specs=pl.BlockSpec((bt, kout), lambda j, i: (i, j)), out_shape=jax.ShapeDtypeStruct((T, nb * kout), out_dtype),
        compiler_params=_params(("parallel", "parallel")),
    )(a, w)


def _row_index(i, cb):
    return (i, cb)


def _row_kernel(name, fn, rows, pars, row_outs, par_outs=(), block_rows=256):
    T = rows[0][0].shape[0]
    R = min(block_rows, T)
    assert T % R == 0
    nr, npar, nro = len(rows), len(pars), len(row_outs)

    def body(*refs):
        rv = [r[...] for r in refs[:nr]]
        pv = [r[...] for r in refs[nr:nr + npar]]
        ro, po = fn(rv, pv)
        for ref, v in zip(refs[nr + npar:nr + npar + nro], ro):
            ref[...] = v.astype(ref.dtype)
        if par_outs:
            i = pl.program_id(0)
            prefs = refs[nr + npar + nro:]

            @pl.when(i == 0)
            def _():
                for ref, v in zip(prefs, po):
                    ref[...] = v.astype(f32)

            @pl.when(i > 0)
            def _():
                for ref, v in zip(prefs, po):
                    ref[...] += v.astype(f32)

    in_specs = [pl.BlockSpec((R, nc), functools.partial(_row_index, cb=cb)) for (_, nc, cb) in rows]
    in_specs += [pl.BlockSpec(p.shape, lambda i: (0, 0)) for p in pars]
    out_specs = [pl.BlockSpec((R, nc), lambda i: (i, 0)) for (nc, _) in row_outs]
    out_specs += [pl.BlockSpec(s, lambda i: (0, 0)) for s in par_outs]
    out_shape = [jax.ShapeDtypeStruct((T, nc), dt) for (nc, dt) in row_outs]
    out_shape += [jax.ShapeDtypeStruct(s, f32) for s in par_outs]
    outs = pl.pallas_call(
        body, name=name, grid=(T // R,), in_specs=in_specs, out_specs=out_specs, out_shape=out_shape,
        compiler_params=_params(("arbitrary",) if par_outs else ("parallel",)),
    )(*[r[0] for r in rows], *pars)
    return list(outs)


def _fwd_of(f):
    def fn(rv, pv):
        return f([v.astype(f32) for v in rv], [v.astype(f32) for v in pv]), []
    return fn


def _vjp_of(f, n_x, n_cot, grad_idx, n_add=0):
    def fn(rv, pv):
        xs = [v.astype(f32) for v in rv[:n_x]]
        cots = [v.astype(f32) for v in rv[n_x:n_x + n_cot]]
        adds = rv[n_x + n_cot:n_x + n_cot + n_add]
        ps = [v.astype(f32) for v in pv]
        _, vjp = jax.vjp(lambda *a: f(list(a[:n_x]), list(a[n_x:])), *xs, *ps)
        g = vjp(cots)
        row_g = [g[i] for i in grad_idx]
        for k, a in enumerate(adds):
            row_g[k] = row_g[k] + a.astype(f32)
        return row_g, list(g[n_x:])
    return fn


def _rms(x, g):
    return x * lax.rsqrt(jnp.mean(x * x, axis=-1, keepdims=True) + RMS_EPS) * g


def _f_norm(xs, ps):
    return [_rms(xs[0], ps[0])]


def _f_post(scale):
    def f(xs, ps):
        return [scale * _rms(xs[0], ps[0])]
    return f


def _f_resnorm(scale):
    def f(xs, ps):
        return [xs[0] + scale * _rms(xs[1], ps[0])]
    return f


def _f_dt(xs, ps):
    dt = jax.nn.softplus(xs[0] + ps[0])
    return [dt, -jnp.exp(ps[1]) * dt]


def _f_ssdpost(n_groups):
    def f(xs, ps):
        y = xs[0] * jax.nn.silu(xs[1])
        width = y.shape[-1] // n_groups
        lane = lax.broadcasted_iota(jnp.int32, y.shape, 1)
        scale = jnp.zeros_like(y)
        for k in range(n_groups):
            m = ((lane >= k * width) & (lane < (k + 1) * width)).astype(f32)
            ms = jnp.sum(y * y * m, axis=-1, keepdims=True) / width
            scale = scale + lax.rsqrt(ms + RMS_EPS) * m
        return [y * scale * ps[0]]
    return f


def _f_s5post(xs, ps):
    return [jax.nn.gelu(xs[0] + ps[0] * xs[1])]


def _f_merge(xs, ps):
    return [jax.nn.sigmoid(xs[0]) * xs[1] + jax.nn.sigmoid(xs[2]) * xs[3]]


def _swiglu_fwd(rv, pv):
    ab = rv[0].astype(f32)
    h = ab.shape[1] // 2
    return [jax.nn.silu(ab[:, :h]) * ab[:, h:]], []


def _swiglu_bwd(rv, pv):
    ab, d = rv[0].astype(f32), rv[1].astype(f32)
    h = ab.shape[1] // 2
    a, b = ab[:, :h], ab[:, h:]
    s = jax.nn.sigmoid(a)
    return [jnp.concatenate([d * b * (s * (1.0 + a * (1.0 - s))), d * (a * s)], axis=1)], []


def _glu_fwd(rv, pv):
    vg = rv[0].astype(f32)
    h = vg.shape[1] // 2
    return [vg[:, :h] * jax.nn.sigmoid(vg[:, h:])], []


def _glu_bwd(rv, pv):
    vg, d = rv[0].astype(f32), rv[1].astype(f32)
    h = vg.shape[1] // 2
    s = jax.nn.sigmoid(vg[:, h:])
    return [jnp.concatenate([d * s, d * vg[:, :h] * s * (1.0 - s)], axis=1)], []


def _loss_fn(rv, pv):
    e = rv[0].astype(f32) - rv[1].astype(f32)
    per_tok = jnp.mean(e * e, axis=-1, keepdims=True)
    part = 0.5 * jnp.sum(per_tok, axis=0, keepdims=True)
    return [e / e.shape[-1]], [jnp.broadcast_to(part, (8, LANES))]


def _add_fn(rv, pv):
    return [rv[0].astype(f32) + rv[1].astype(f32)], []


def _adamw_fn(rv, pv):
    w, g, m, v = [x.astype(f32) for x in rv]
    m = ADAM_B1 * m + (1.0 - ADAM_B1) * g
    v = ADAM_B2 * v + (1.0 - ADAM_B2) * (g * g)
    m_hat = m / (1.0 - ADAM_B1 ** ADAM_STEP)
    v_hat = v / (1.0 - ADAM_B2 ** ADAM_STEP)
    return [-ADAM_LR * (m_hat / (jnp.sqrt(v_hat) + ADAM_EPS) + ADAM_WD * w), m, v], []


def _adamw_layers(w, gs, m, v, name):
    n_layers, r, cols = w.shape
    br, bc = _pick(r, 256, 8), cols
    if br < 64 and cols % LANES == 0:
        br, bc = r, LANES

    def body(w_ref, *refs):
        g_refs, (m_ref, v_ref, g_out, d_ref, nm_ref, nv_ref) = refs[:n_layers], refs[n_layers:]
        layer = pl.program_id(0)
        g = g_refs[0][...]
        for k in range(1, n_layers):
            g = jnp.where(layer == k, g_refs[k][...], g)
        outs, _ = _adamw_fn([w_ref[0], g, m_ref[0], v_ref[0]], [])
        g_out[0] = g
        d_ref[0], nm_ref[0], nv_ref[0] = outs

    stacked = pl.BlockSpec((1, br, bc), lambda l, i, j: (l, i, j))
    single = pl.BlockSpec((br, bc), lambda l, i, j: (i, j))
    return pl.pallas_call(
        body, name=name, grid=(n_layers, r // br, cols // bc), in_specs=[stacked] + [single] * n_layers + [stacked] * 2,
        out_specs=[stacked] * 4, out_shape=[jax.ShapeDtypeStruct(w.shape, f32)] * 4,
        compiler_params=_params(("parallel", "parallel", "parallel")),
    )(w, *gs, m, v)


def _adamw(w, g, m, v, name):
    shape = w.shape
    cols = shape[-1] if (w.ndim >= 2 and shape[-1] >= LANES) else None
    if cols is None:
        n = int(np.prod(shape))
        cols = LANES if n % LANES == 0 else n
    n_rows = int(np.prod(shape)) // cols
    br, bc = _pick(n_rows, 256, 8), cols
    if br < 64 and cols % LANES == 0:
        br, bc = n_rows, LANES

    def body(w_ref, g_ref, m_ref, v_ref, d_ref, nm_ref, nv_ref):
        outs, _ = _adamw_fn([w_ref[...], g_ref[...], m_ref[...], v_ref[...]], [])
        d_ref[...], nm_ref[...], nv_ref[...] = outs

    spec = pl.BlockSpec((br, bc), lambda i, j: (i, j))
    outs = pl.pallas_call(
        body, name=name, grid=(n_rows // br, cols // bc), in_specs=[spec] * 4, out_specs=[spec] * 3,
        out_shape=[jax.ShapeDtypeStruct((n_rows, cols), f32)] * 3, compiler_params=_params(("parallel", "parallel")),
    )(*[t.reshape(n_rows, cols) for t in (w, g, m, v)])
    return [o.reshape(shape) for o in outs]


def _shift_down(x, s, row):
    if s == 0:
        return x
    return jnp.where(row >= s, pltpu.roll(x, s, 0), 0.0)


def _shift_up(x, s, row):
    if s == 0:
        return x
    n = x.shape[0]
    return jnp.where(row < n - s, pltpu.roll(x, n - s, 0), 0.0)


def _conv_pre(x, w, b, row):
    kw = w.shape[0]
    c = b
    for k in range(kw):
        c = c + w[k:k + 1, :] * _shift_down(x, kw - 1 - k, row)
    return c


def _conv_fwd(xsrc, col0, w, b, name, bc_t=512):
    T = xsrc.shape[0]
    kw, ncols = w.shape
    bc = _pick(ncols, bc_t)
    off = col0 // bc
    assert col0 % bc == 0

    def body(x_ref, w_ref, b_ref, o_ref):
        x = x_ref[...].astype(f32)
        row = lax.broadcasted_iota(jnp.int32, x.shape, 0)
        c = _conv_pre(x, w_ref[...], b_ref[...], row)
        o_ref[...] = c * jax.nn.sigmoid(c)

    return pl.pallas_call(
        body, name=name, grid=(ncols // bc,),
        in_specs=[pl.BlockSpec((T, bc), lambda j: (0, off + j)), pl.BlockSpec((kw, bc), lambda j: (0, j)),
                  pl.BlockSpec((1, bc), lambda j: (0, j))],
        out_specs=pl.BlockSpec((T, bc), lambda j: (0, j)), out_shape=jax.ShapeDtypeStruct((T, ncols), f32),
        compiler_params=_params(("parallel",)),
    )(xsrc, w, b)


def _conv_bwd(xsrc, col0, w, b, dact, name, bc_t=512):
    T = xsrc.shape[0]
    kw, ncols = w.shape
    bc = _pick(ncols, bc_t)
    off = col0 // bc
    assert col0 % bc == 0

    def body(x_ref, w_ref, b_ref, d_ref, dx_ref, dw_ref, db_ref):
        x = x_ref[...].astype(f32)
        w = w_ref[...]
        row = lax.broadcasted_iota(jnp.int32, x.shape, 0)
        c = _conv_pre(x, w, b_ref[...], row)
        s = jax.nn.sigmoid(c)
        dc = d_ref[...].astype(f32) * (s * (1.0 + c * (1.0 - s)))
        dx = jnp.zeros_like(x)
        dws = []
        for k in range(kw):
            dx = dx + w[k:k + 1, :] * _shift_up(dc, kw - 1 - k, row)
            dws.append(jnp.sum(dc * _shift_down(x, kw - 1 - k, row), axis=0, keepdims=True))
        dx_ref[...] = dx.astype(dx_ref.dtype)
        dw_ref[...] = jnp.concatenate(dws, axis=0)
        db_ref[...] = jnp.sum(dc, axis=0, keepdims=True)

    return pl.pallas_call(
        body, name=name, grid=(ncols // bc,),
        in_specs=[pl.BlockSpec((T, bc), lambda j: (0, off + j)), pl.BlockSpec((kw, bc), lambda j: (0, j)),
                  pl.BlockSpec((1, bc), lambda j: (0, j)), pl.BlockSpec((T, bc), lambda j: (0, j))],
        out_specs=[pl.BlockSpec((T, bc), lambda j: (0, j)), pl.BlockSpec((kw, bc), lambda j: (0, j)),
                   pl.BlockSpec((1, bc), lambda j: (0, j))],
        out_shape=[jax.ShapeDtypeStruct((T, ncols), bf16), jax.ShapeDtypeStruct((kw, ncols), f32),
                   jax.ShapeDtypeStruct((1, ncols), f32)],
        compiler_params=_params(("parallel",)),
    )(xsrc, w, b, dact)


_HI = lax.Precision.HIGHEST


def _dot(a, b, dims="nn", precision=None):
    return lax.dot_general(a, b, _DIMS[dims], preferred_element_type=f32, precision=precision)


def _dot01(a, b, dims="nn", ones="b"):
    x = a if ones == "b" else b
    hi = x.astype(bf16)
    rest = x - hi.astype(f32)
    mid = rest.astype(bf16)
    lo = (rest - mid.astype(f32)).astype(bf16)
    if ones == "b":
        e = b.astype(bf16)
        return _dot(hi, e, dims) + _dot(mid, e, dims) + _dot(lo, e, dims)
    e = a.astype(bf16)
    return _dot(e, hi, dims) + _dot(e, mid, dims) + _dot(e, lo, dims)


def _ssd_common(x_ref, b_ref, c_ref, dt_ref, adt_ref, d_ref, hpg, p):
    q = b_ref.shape[0]
    hp = hpg * p
    bb, cb = b_ref[...].astype(bf16), c_ref[...].astype(bf16)
    r = lax.broadcasted_iota(jnp.int32, (q, q), 0)
    s = lax.broadcasted_iota(jnp.int32, (q, q), 1)
    tril = r >= s
    trilf = tril.astype(f32)
    eh = lax.broadcasted_iota(jnp.int32, (LANES, hp), 0)
    ec = lax.broadcasted_iota(jnp.int32, (LANES, hp), 1)
    expand = ((ec >= eh * p) & (ec < (eh + 1) * p)).astype(f32)
    adt = adt_ref[...]
    cum = _dot01(trilf, adt, "nn", "a")
    cum_t = _dot01(adt, (r <= s).astype(f32), "tn")
    cum_e = _dot01(cum, expand)
    dt_e = _dot01(dt_ref[...], expand)
    d_e = _dot01(jnp.broadcast_to(d_ref[...], (8, LANES)), expand)[0:1, :]
    gmat = _dot(cb, bb, "nt")
    x = x_ref[...]
    xdt = x * dt_e
    e_all = jnp.exp(cum_e)
    dec = jnp.exp(cum_e[q - 1:q, :] - cum_e)
    lms, ms = [], []
    for h in range(hpg):
        lm = jnp.exp(jnp.where(tril, cum[:, h:h + 1] - cum_t[h:h + 1, :], -1e30))
        lms.append(lm)
        ms.append(gmat * lm)
    et = [jnp.exp(cum[q - 1:q, h:h + 1]) for h in range(hpg)]
    return dict(bb=bb, cb=cb, trilf=trilf, expand=expand, cum=cum, x=x, xdt=xdt, dt_e=dt_e, d_e=d_e, e=e_all, dec=dec,
                lms=lms, ms=ms, et=et)


def _ssd_specs(q, hp, n, g_n, nc, rev):
    def cidx(c):
        return (nc - 1 - c) if rev else c
    x_spec = pl.BlockSpec((q, hp), lambda g, c: (cidx(c), g))
    boff = (g_n * hp) // n
    b_spec = pl.BlockSpec((q, n), lambda g, c: (cidx(c), boff + g))
    c_spec = pl.BlockSpec((q, n), lambda g, c: (cidx(c), boff + g_n + g))
    dt_spec = pl.BlockSpec((q, LANES), lambda g, c: (cidx(c), g))
    d_spec = pl.BlockSpec((1, LANES), lambda g, c: (0, g))
    st_spec = pl.BlockSpec((1, 1, hp, n), lambda g, c: (cidx(c), g, 0, 0))
    return x_spec, b_spec, c_spec, dt_spec, d_spec, st_spec


def _ssd_fwd(act, dt, adt, dpad, hpg, p, n, name):
    T = act.shape[0]
    g_n, q = SSD_N_GROUPS, SSD_CHUNK
    nc, hp = T // q, hpg * p
    x_spec, b_spec, c_spec, dt_spec, d_spec, st_spec = _ssd_specs(q, hp, n, g_n, nc, False)

    def body(x_ref, b_ref, c_ref, dt_ref, adt_ref, d_ref, y_ref, st_ref, s_scr):
        @pl.when(pl.program_id(1) == 0)
        def _():
            s_scr[...] = jnp.zeros_like(s_scr)

        k = _ssd_common(x_ref, b_ref, c_ref, dt_ref, adt_ref, d_ref, hpg, p)
        s0 = s_scr[...]
        st_ref[0, 0] = s0
        xdtb = k["xdt"].astype(bf16)
        ydiag = [_dot(k["ms"][h].astype(bf16), xdtb[:, h * p:(h + 1) * p]) for h in range(hpg)]
        z = _dot(k["cb"], s0.astype(bf16), "nt")
        y_ref[...] = jnp.concatenate(ydiag, axis=1) + k["e"] * z + k["d_e"] * k["x"]
        upd = _dot((k["xdt"] * k["dec"]).astype(bf16), k["bb"], "tn")
        for h in range(hpg):
            s_scr[h * p:(h + 1) * p, :] = k["et"][h] * s0[h * p:(h + 1) * p, :] + upd[h * p:(h + 1) * p, :]

    return pl.pallas_call(
        body, name=name, grid=(g_n, nc),
        in_specs=[x_spec, b_spec, c_spec, dt_spec, dt_spec, d_spec],
        out_specs=[pl.BlockSpec((q, hp), lambda g, c: (c, g)), st_spec],
        out_shape=[jax.ShapeDtypeStruct((T, g_n * hp), f32), jax.ShapeDtypeStruct((nc, g_n, hp, n), f32)],
        scratch_shapes=[pltpu.VMEM((hp, n), f32)],
        compiler_params=_params(("parallel", "arbitrary")),
    )(act, act, act, dt, adt, dpad)


def _ssd_bwd(act, dt, adt, dpad, states, dy, hpg, p, n, name):
    T = act.shape[0]
    g_n, q = SSD_N_GROUPS, SSD_CHUNK
    nc, hp = T // q, hpg * p
    x_spec, b_spec, c_spec, dt_spec, d_spec, st_spec = _ssd_specs(q, hp, n, g_n, nc, True)

    def body(x_ref, b_ref, c_ref, dt_ref, adt_ref, d_ref, st_ref, dy_ref,
             dx_ref, db_ref, dc_ref, ddt_ref, dadt_ref, dd_ref, ds_scr):
        first = pl.program_id(1) == 0

        @pl.when(first)
        def _():
            ds_scr[...] = jnp.zeros_like(ds_scr)

        k = _ssd_common(x_ref, b_ref, c_ref, dt_ref, adt_ref, d_ref, hpg, p)
        bb, cb, expand, x, xdt, dec = k["bb"], k["cb"], k["expand"], k["x"], k["xdt"], k["dec"]
        heads = lambda t: _dot01(t, expand, "nt")
        s0 = st_ref[0, 0]
        ds1 = ds_scr[...]
        s0b, ds1b = s0.astype(bf16), ds1.astype(bf16)
        dy = dy_ref[...]
        dyb, xdtb = dy.astype(bf16), xdt.astype(bf16)
        lane = lax.broadcasted_iota(jnp.int32, (1, LANES), 1)
        dg = jnp.zeros((q, q), f32)
        w_rows = jnp.zeros((q, LANES), f32)
        w_cols, dxdt_parts = [], []
        for h in range(hpg):
            hs = slice(h * p, (h + 1) * p)
            dm = _dot(dyb[:, hs], xdtb[:, hs], "nt")
            dg = dg + dm * k["lms"][h]
            wm = dm * k["ms"][h]
            w_rows = w_rows + jnp.sum(wm, axis=1, keepdims=True) * (lane == h).astype(f32)
            w_cols.append(jnp.sum(wm, axis=0, keepdims=True))
            dxdt_parts.append(_dot(k["ms"][h].astype(bf16), dyb[:, hs], "tn"))
        dxdt_diag = jnp.concatenate(dxdt_parts, axis=1)
        w_cols = jnp.concatenate(w_cols + [jnp.zeros((LANES - hpg, q), f32)], axis=0).T
        dgb = dg.astype(bf16)
        z = _dot(cb, s0b, "nt")
        dz = dy * k["e"]
        dzb = dz.astype(bf16)
        dxd = _dot(bb, ds1b, "nt")
        ddec = dxd * xdt * dec
        db_ref[...] = _dot(dgb, cb, "tn") + _dot((xdt * dec).astype(bf16), ds1b)
        dc_ref[...] = _dot(dgb, bb) + _dot(dzb, s0b)
        ds0 = _dot(dzb, cb, "tn")
        for h in range(hpg):
            hs = slice(h * p, (h + 1) * p)
            ds_scr[hs, :] = ds0[hs, :] + k["et"][h] * ds1[hs, :]
        dxdt = dxdt_diag + dxd * dec
        ddec_h = heads(ddec)
        dcum = w_rows - w_cols + heads(dz * z) - ddec_h
        et_row = jnp.exp(k["cum"][q - 1:q, :])
        dsum = _dot01(jnp.ones((8, n), f32), _dot01(expand, ds1 * s0, "nn", "a"), "nt", "a")[0:1, :]
        dcl = dsum * et_row + jnp.sum(ddec_h, axis=0, keepdims=True)
        rowq = lax.broadcasted_iota(jnp.int32, (q, 1), 0)
        dcum = dcum + (rowq == q - 1).astype(f32) * dcl
        ddt_ref[...] = heads(dxdt * x)
        dadt_ref[...] = _dot01(k["trilf"], dcum, "tn", "a")
        dx_ref[...] = k["d_e"] * dy + dxdt * k["dt_e"]
        dd8 = heads(jnp.broadcast_to(jnp.sum(dy * x, axis=0, keepdims=True), (8, hp)))

        @pl.when(first)
        def _():
            dd_ref[...] = dd8

        @pl.when(jnp.logical_not(first))
        def _():
            dd_ref[...] += dd8

    rc = lambda g, c: (nc - 1 - c, g)
    return pl.pallas_call(
        body, name=name, grid=(g_n, nc),
        in_specs=[x_spec, b_spec, c_spec, dt_spec, dt_spec, d_spec, st_spec, pl.BlockSpec((q, hp), rc)],
        out_specs=[pl.BlockSpec((q, hp), rc), pl.BlockSpec((q, n), rc), pl.BlockSpec((q, n), rc),
                   pl.BlockSpec((q, LANES), rc), pl.BlockSpec((q, LANES), rc), pl.BlockSpec((8, LANES), lambda g, c: (g, 0))],
        out_shape=[jax.ShapeDtypeStruct((T, g_n * hp), f32), jax.ShapeDtypeStruct((T, g_n * n), f32),
                   jax.ShapeDtypeStruct((T, g_n * n), f32), jax.ShapeDtypeStruct((T, g_n * LANES), f32),
                   jax.ShapeDtypeStruct((T, g_n * LANES), f32), jax.ShapeDtypeStruct((g_n * 8, LANES), f32)],
        scratch_shapes=[pltpu.VMEM((hp, n), f32)],
        compiler_params=_params(("parallel", "arbitrary")),
    )(act, act, act, dt, adt, dpad, states, dy)


def _cmul(ar, ai, br, bi):
    return ar * br - ai * bi, ar * bi + ai * br


def _s5_tile_powers(lr, li):
    p = [(lr, li)]
    for _ in range(7):
        p.append(_cmul(p[-1][0], p[-1][1], lr, li))
    tile = (jnp.concatenate([q[0] for q in p], axis=0), jnp.concatenate([q[1] for q in p], axis=0))
    return tile, (p[0], p[1], p[3])


def _s5_tile_scan(xr, xi, steps, reverse):
    row = lax.broadcasted_iota(jnp.int32, xr.shape, 0)
    for d, (pr, pi) in zip((1, 2, 4), steps):
        if reverse:
            keep = row < 8 - d
            sr, si = pltpu.roll(xr, 8 - d, 0), pltpu.roll(xi, 8 - d, 0)
        else:
            keep = row >= d
            sr, si = pltpu.roll(xr, d, 0), pltpu.roll(xi, d, 0)
        sr, si = jnp.where(keep, sr, 0.0), jnp.where(keep, si, 0.0)
        ar, ai = _cmul(sr, si, pr, pi)
        xr, xi = xr + ar, xi + ai
    return xr, xi


def _s5_scan_fwd(bu, lam_re, lam_im, nsb, name, tc_t=512):
    T = bu.shape[0]
    w2 = bu.shape[1] // nsb
    w = w2 // 2
    tc = _pick(T, tc_t, 8)

    def body(bu_ref, lr_ref, li_ref, st_ref, carry):
        @pl.when(pl.program_id(1) == 0)
        def _():
            carry[...] = jnp.zeros_like(carry)

        (pr8, pi8), steps = _s5_tile_powers(lr_ref[0:1, :], li_ref[0:1, :])

        def tile(i, c):
            r = pl.ds(pl.multiple_of(i * 8, 8), 8)
            x = bu_ref[r, :]
            xr, xi = _s5_tile_scan(x[:, :w], x[:, w:], steps, False)
            ar, ai = _cmul(pr8, pi8, c[0], c[1])
            xr, xi = xr + ar, xi + ai
            st_ref[r, :] = jnp.concatenate([xr, xi], axis=1)
            return xr[7:8, :], xi[7:8, :]

        c = lax.fori_loop(0, tc // 8, tile, (carry[0:1, :], carry[1:2, :]), unroll=2)
        carry[0:1, :] = c[0]
        carry[1:2, :] = c[1]

    return pl.pallas_call(
        body, name=name, grid=(nsb, T // tc),
        in_specs=[pl.BlockSpec((tc, w2), lambda j, i: (i, j)), pl.BlockSpec((8, w), lambda j, i: (j, 0)),
                  pl.BlockSpec((8, w), lambda j, i: (j, 0))],
        out_specs=pl.BlockSpec((tc, w2), lambda j, i: (i, j)), out_shape=jax.ShapeDtypeStruct(bu.shape, f32),
        scratch_shapes=[pltpu.VMEM((8, w), f32)],
        compiler_params=_params(("parallel", "arbitrary")),
    )(bu, lam_re, lam_im)


def _s5_scan_bwd(gst, states, lam_re, lam_im, nsb, name, tc_t=512):
    T = gst.shape[0]
    w2 = gst.shape[1] // nsb
    w = w2 // 2
    tc = _pick(T, tc_t, 8)
    nt = T // tc
    n_tiles = tc // 8

    def body(g_ref, s_ref, sp_ref, lr_ref, li_ref, a_ref, dlr_ref, dli_ref, carry, acc):
        chunk = pl.program_id(1)

        @pl.when(chunk == 0)
        def _():
            carry[...] = jnp.zeros_like(carry)
            acc[...] = jnp.zeros_like(acc)

        (qr8, qi8), steps = _s5_tile_powers(lr_ref[0:1, :], -li_ref[0:1, :])
        row = lax.broadcasted_iota(jnp.int32, (8, w), 0)
        rev_r, rev_i = jnp.zeros((8, w), f32), jnp.zeros((8, w), f32)
        for r in range(8):
            rev_r = jnp.where(row == r, qr8[7 - r:8 - r, :], rev_r)
            rev_i = jnp.where(row == r, qi8[7 - r:8 - r, :], rev_i)
        row2 = lax.broadcasted_iota(jnp.int32, (8, w2), 0)

        def tile(k, c):
            ar_in, ai_in, dr, di = c
            i = n_tiles - 1 - k
            r = pl.ds(pl.multiple_of(i * 8, 8), 8)
            x = g_ref[r, :]
            xr, xi = _s5_tile_scan(x[:, :w], x[:, w:], steps, True)
            pr, pi = _cmul(rev_r, rev_i, ar_in, ai_in)
            xr, xi = xr + pr, xi + pi
            a_ref[r, :] = jnp.concatenate([xr, xi], axis=1)
            before = jnp.where(i > 0, s_ref[pl.ds(pl.multiple_of(jnp.maximum(i - 1, 0) * 8, 8), 8), :],
                               sp_ref[tc - 8:tc, :] * (chunk < nt - 1).astype(f32))
            prev = jnp.where(row2 == 0, pltpu.roll(before, 1, 0), pltpu.roll(s_ref[r, :], 1, 0))
            spr, spi = prev[:, :w], prev[:, w:]
            return xr[0:1, :], xi[0:1, :], dr + xr * spr + xi * spi, di - xr * spi + xi * spr

        c0 = (carry[0:1, :], carry[1:2, :], acc[0:8, :], acc[8:16, :])
        ar, ai, dr, di = lax.fori_loop(0, n_tiles, tile, c0, unroll=2)
        carry[0:1, :] = ar
        carry[1:2, :] = ai
        acc[0:8, :] = dr
        acc[8:16, :] = di
        dlr_ref[...] = jnp.broadcast_to(jnp.sum(dr, axis=0, keepdims=True), (8, w))
        dli_ref[...] = jnp.broadcast_to(jnp.sum(di, axis=0, keepdims=True), (8, w))

    cur = lambda j, i: (nt - 1 - i, j)
    prv = lambda j, i: (jnp.maximum(nt - 2 - i, 0), j)
    return pl.pallas_call(
        body, name=name, grid=(nsb, nt),
        in_specs=[pl.BlockSpec((tc, w2), cur), pl.BlockSpec((tc, w2), cur), pl.BlockSpec((tc, w2), prv),
                  pl.BlockSpec((8, w), lambda j, i: (j, 0)), pl.BlockSpec((8, w), lambda j, i: (j, 0))],
        out_specs=[pl.BlockSpec((tc, w2), cur), pl.BlockSpec((8, w), lambda j, i: (j, 0)),
                   pl.BlockSpec((8, w), lambda j, i: (j, 0))],
        out_shape=[jax.ShapeDtypeStruct(gst.shape, f32), jax.ShapeDtypeStruct((nsb * 8, w), f32),
                   jax.ShapeDtypeStruct((nsb * 8, w), f32)],
        scratch_shapes=[pltpu.VMEM((8, w), f32), pltpu.VMEM((16, w), f32)],
        compiler_params=_params(("parallel", "arbitrary")),
    )(gst, states, states, lam_re, lam_im)


def _s5_prep_fn(xs, ps):
    lam_re, lam_im, log_step, b_re, b_im, expand = ps
    lr = jnp.minimum(lam_re, S5_MAX_REAL)
    li = lam_im
    step = jnp.exp(log_step)
    er = jnp.exp(lr * step)
    ang = li * step
    lbr, lbi = er * jnp.cos(ang), er * jnp.sin(ang)
    nr, ni = lbr - 1.0, lbi
    den = lr * lr + li * li
    qr, qi = (nr * lr + ni * li) / den, (ni * lr - nr * li) / den
    qre, qie = _dot(qr, expand, "nn", _HI), _dot(qi, expand, "nn", _HI)
    return [lbr, lbi, qre * b_re - qie * b_im, qre * b_im + qie * b_re]


def _s5_prep(pars, name):
    def body(*refs):
        outs = _s5_prep_fn([], [r[...] for r in refs[:6]])
        for ref, v in zip(refs[6:], outs):
            ref[...] = v

    g, nst = pars[0].shape
    nc = pars[3].shape[1]
    return pl.pallas_call(
        body, name=name,
        out_shape=[jax.ShapeDtypeStruct((g, nst), f32)] * 2 + [jax.ShapeDtypeStruct((g, nc), f32)] * 2,
        compiler_params=_params(),
    )(*pars)


def _s5_prep_bwd(pars, cots, name):
    def body(*refs):
        ps = [r[...] for r in refs[:6]]
        ct = [r[...] for r in refs[6:10]]
        _, vjp = jax.vjp(lambda *a: _s5_prep_fn([], list(a)), *ps)
        g = vjp(ct)
        for ref, v in zip(refs[10:], g[:5]):
            ref[...] = v

    return pl.pallas_call(
        body, name=name, out_shape=[jax.ShapeDtypeStruct(p.shape, f32) for p in pars[:5]], compiler_params=_params(),
    )(*pars, *cots)


_ANY = pl.BlockSpec(memory_space=pl.ANY)


def _remote(src, dst, send_sem, recv_sem, device):
    return pltpu.make_async_remote_copy(src_ref=src, dst_ref=dst, send_sem=send_sem, recv_sem=recv_sem, device_id=device,
                                        device_id_type=MESH)


def _staged_copy(src, dst, buf, in_sems, out_sems):
    n = D2D_STREAMS
    piece = src.shape[0] // n
    assert src.shape[0] % n == 0

    def load(i):
        return pltpu.make_async_copy(src.at[pl.ds(i * piece, piece)], buf.at[i % 2], in_sems.at[i % 2])

    def store(i):
        return pltpu.make_async_copy(buf.at[i % 2], dst.at[pl.ds(i * piece, piece)], out_sems.at[i % 2])

    load(0).start()
    for i in range(n):
        if i + 1 < n:
            if i >= 1:
                store(i - 1).wait()
            load(i + 1).start()
        load(i).wait()
        store(i).start()
    store(n - 2).wait()
    store(n - 1).wait()


def _stage_scratch(rows, cols, dtype):
    return [pltpu.VMEM((2, rows // D2D_STREAMS, cols), dtype), pltpu.SemaphoreType.DMA((2,)), pltpu.SemaphoreType.DMA((2,))]


def _chip_all_gather(block, name):
    rows = block.shape[0]
    half = rows // 2
    piece = half // D2D_STREAMS
    assert rows % (2 * D2D_STREAMS * 16) == 0

    def body(src, out, ici_send, ici_recv, d2d_send, d2d_recv, *stage):
        x, y, c = lax.axis_index("x"), lax.axis_index("y"), lax.axis_index("c")
        me = 2 * x + y
        sibling = (x, y, 1 - c)
        chips = [(1 - x, y), (x, 1 - y), (1 - x, 1 - y)]
        mine = pl.ds(pl.multiple_of(c * half, 16), half)
        sends = []
        for j, (px, py) in enumerate(chips):
            cp = _remote(src.at[mine], out.at[me, mine], ici_send.at[j], ici_recv.at[j], (px, py, c))
            cp.start()
            sends.append(cp)
        _staged_copy(src, out.at[me], *stage)
        for j, (px, py) in enumerate(chips):
            slot = 2 * px + py
            _remote(src.at[mine], out.at[slot, mine], ici_send.at[j], ici_recv.at[j], (px, py, c)).wait_recv()
            for s in range(D2D_STREAMS):
                r = pl.ds(pl.multiple_of(c * half + s * piece, 16), piece)
                k = j * D2D_STREAMS + s
                cp = _remote(out.at[slot, r], out.at[slot, r], d2d_send.at[k], d2d_recv.at[k], sibling)
                cp.start()
                sends.append(cp)
        for j, (px, py) in enumerate(chips):
            slot = 2 * px + py
            for s in range(D2D_STREAMS):
                r = pl.ds(pl.multiple_of((1 - c) * half + s * piece, 16), piece)
                k = j * D2D_STREAMS + s
                _remote(out.at[slot, r], out.at[slot, r], d2d_send.at[k], d2d_recv.at[k], sibling).wait_recv()
        for cp in sends:
            cp.wait_send()

    n_d2d = 3 * D2D_STREAMS
    return pl.pallas_call(
        body, name=name, in_specs=[_ANY], out_specs=_ANY,
        out_shape=jax.ShapeDtypeStruct((N_CHIPS,) + block.shape, block.dtype),
        scratch_shapes=[pltpu.SemaphoreType.DMA((3,)), pltpu.SemaphoreType.DMA((3,)), pltpu.SemaphoreType.DMA((n_d2d,)),
                        pltpu.SemaphoreType.DMA((n_d2d,))] + _stage_scratch(rows, block.shape[1], block.dtype),
    )(block)


def _chip_scatter(parts, name):
    def body(src, out, send_sems, recv_sems, *stage):
        x, y, c = lax.axis_index("x"), lax.axis_index("y"), lax.axis_index("c")
        me = 2 * x + y
        chips = [(1 - x, y), (x, 1 - y), (1 - x, 1 - y)]
        sends = []
        for j, (px, py) in enumerate(chips):
            cp = pltpu.make_async_remote_copy(src_ref=src.at[2 * px + py], dst_ref=out.at[me], send_sem=send_sems.at[j],
                                              recv_sem=recv_sems.at[j], device_id=(px, py, c), device_id_type=MESH)
            cp.start()
            sends.append(cp)
        _staged_copy(src.at[me], out.at[me], *stage)
        for j, (px, py) in enumerate(chips):
            pltpu.make_async_remote_copy(src_ref=src.at[me], dst_ref=out.at[2 * px + py], send_sem=send_sems.at[j],
                                         recv_sem=recv_sems.at[j], device_id=(px, py, c), device_id_type=MESH).wait_recv()
        for cp in sends:
            cp.wait_send()

    return pl.pallas_call(
        body, name=name, in_specs=[_ANY], out_specs=_ANY, out_shape=jax.ShapeDtypeStruct(parts.shape, parts.dtype),
        scratch_shapes=[pltpu.SemaphoreType.DMA((3,)), pltpu.SemaphoreType.DMA((3,))]
        + _stage_scratch(parts.shape[1], parts.shape[2], parts.dtype),
    )(parts)


_HBM = pl.BlockSpec(memory_space=pltpu.HBM)
_SEM = pl.BlockSpec(memory_space=pltpu.SEMAPHORE)
_EFFECT = pltpu.SideEffectType.DATAFLOW_SIDE_EFFECTING


def _gather_peers():
    x, y, c = lax.axis_index("x"), lax.axis_index("y"), lax.axis_index("c")
    return x, y, c, 2 * x + y, [(1 - x, y), (x, 1 - y), (1 - x, 1 - y)]


def _chip_gather_start(block, name):
    half = block.shape[0] // 2

    def body(src, land, send_sems, recv_sems, src_out, land_out, token):
        x, y, c, me, chips = _gather_peers()
        mine = pl.ds(pl.multiple_of(c * half, 16), half)
        for j, (px, py) in enumerate(chips):
            _remote(src.at[mine], land.at[me, mine], send_sems.at[j], recv_sems.at[j], (px, py, c)).start()
        token[...] = jnp.zeros_like(token)

    land_shape = (N_CHIPS,) + block.shape
    return pl.pallas_call(
        body, name=name,
        out_shape=(pltpu.SemaphoreType.DMA((3,)), pltpu.SemaphoreType.DMA((3,)), pltpu.HBM(block.shape, block.dtype),
                   pltpu.HBM(land_shape, block.dtype), jax.ShapeDtypeStruct((8, LANES), f32)),
        in_specs=(_HBM, _HBM), out_specs=(_SEM, _SEM, _HBM, _HBM, pl.BlockSpec(memory_space=pltpu.VMEM)),
        input_output_aliases={0: 2, 1: 3}, compiler_params=pltpu.CompilerParams(has_side_effects=_EFFECT),
    )(pltpu.with_memory_space_constraint(block, pltpu.HBM),
      pltpu.with_memory_space_constraint(lax.empty(land_shape, block.dtype), pltpu.HBM))


def _chip_gather_wait(send_sems, recv_sems, block, land, after, name):
    half = block.shape[0] // 2

    def body(src, land_ref, send_ref, recv_ref, after_ref, src_dead, land_out):
        x, y, c, me, chips = _gather_peers()
        mine = pl.ds(pl.multiple_of(c * half, 16), half)
        for j, (px, py) in enumerate(chips):
            cp = _remote(src.at[mine], land_ref.at[2 * px + py, mine], send_ref.at[j], recv_ref.at[j], (px, py, c))
            cp.wait_send()
            cp.wait_recv()

    return pl.pallas_call(
        body, name=name, out_shape=(pltpu.HBM(block.shape, block.dtype), pltpu.HBM(land.shape, land.dtype)),
        in_specs=(_HBM, _HBM, _SEM, _SEM, _ANY), out_specs=(_HBM, _HBM), input_output_aliases={0: 0, 1: 1},
        compiler_params=pltpu.CompilerParams(has_side_effects=_EFFECT),
    )(block, land, send_sems, recv_sems, after)


def _chip_gather_finish(block, land, name):
    rows = block.shape[0]
    half = rows // 2
    piece = half // D2D_STREAMS

    def body(src, land_ref, out, d2d_send, d2d_recv, *stage):
        x, y, c, me, chips = _gather_peers()
        sibling = (x, y, 1 - c)
        sends = []
        for j, (px, py) in enumerate(chips):
            slot = 2 * px + py
            for s in range(D2D_STREAMS):
                r = pl.ds(pl.multiple_of(c * half + s * piece, 16), piece)
                k = j * D2D_STREAMS + s
                cp = _remote(land_ref.at[slot, r], out.at[slot, r], d2d_send.at[k], d2d_recv.at[k], sibling)
                cp.start()
                sends.append(cp)
        _staged_copy(src, out.at[me], *stage)
        for j, (px, py) in enumerate(chips):
            slot = 2 * px + py
            for s in range(D2D_STREAMS):
                r = pl.ds(pl.multiple_of((1 - c) * half + s * piece, 16), piece)
                k = j * D2D_STREAMS + s
                _remote(land_ref.at[slot, r], out.at[slot, r], d2d_send.at[k], d2d_recv.at[k], sibling).wait_recv()
        for cp in sends:
            cp.wait_send()

    n_d2d = 3 * D2D_STREAMS
    return pl.pallas_call(
        body, name=name, in_specs=[_ANY, _ANY], out_specs=_ANY, out_shape=jax.ShapeDtypeStruct(land.shape, land.dtype),
        input_output_aliases={1: 0},
        scratch_shapes=[pltpu.SemaphoreType.DMA((n_d2d,)), pltpu.SemaphoreType.DMA((n_d2d,))]
        + _stage_scratch(rows, block.shape[1], block.dtype),
    )(block, land)


def _chip_scatter_start(parts, name):
    def body(src, land, send_sems, recv_sems, src_out, land_out, token):
        x, y, c = lax.axis_index("x"), lax.axis_index("y"), lax.axis_index("c")
        me = 2 * x + y
        for j, (px, py) in enumerate([(1 - x, y), (x, 1 - y), (1 - x, 1 - y)]):
            _remote(src.at[2 * px + py], land.at[me], send_sems.at[j], recv_sems.at[j], (px, py, c)).start()
        token[...] = jnp.zeros_like(token)

    return pl.pallas_call(
        body, name=name,
        out_shape=(pltpu.SemaphoreType.DMA((3,)), pltpu.SemaphoreType.DMA((3,)), pltpu.HBM(parts.shape, parts.dtype),
                   pltpu.HBM(parts.shape, parts.dtype), jax.ShapeDtypeStruct((8, LANES), f32)),
        in_specs=(_HBM, _HBM), out_specs=(_SEM, _SEM, _HBM, _HBM, pl.BlockSpec(memory_space=pltpu.VMEM)),
        input_output_aliases={0: 2, 1: 3}, compiler_params=pltpu.CompilerParams(has_side_effects=_EFFECT),
    )(pltpu.with_memory_space_constraint(parts, pltpu.HBM),
      pltpu.with_memory_space_constraint(lax.empty(parts.shape, parts.dtype), pltpu.HBM))


def _chip_scatter_wait(send_sems, recv_sems, parts, land, after, name):
    def body(src, land_ref, send_ref, recv_ref, after_ref, src_dead, land_out):
        x, y, c = lax.axis_index("x"), lax.axis_index("y"), lax.axis_index("c")
        me = 2 * x + y
        for j, (px, py) in enumerate([(1 - x, y), (x, 1 - y), (1 - x, 1 - y)]):
            cp = _remote(src.at[2 * px + py], land_ref.at[2 * px + py], send_ref.at[j], recv_ref.at[j], (px, py, c))
            cp.wait_send()
            cp.wait_recv()

    return pl.pallas_call(
        body, name=name, out_shape=(pltpu.HBM(parts.shape, parts.dtype), pltpu.HBM(land.shape, land.dtype)),
        in_specs=(_HBM, _HBM, _SEM, _SEM, _ANY), out_specs=(_HBM, _HBM), input_output_aliases={0: 0, 1: 1},
        compiler_params=pltpu.CompilerParams(has_side_effects=_EFFECT),
    )(parts, land, send_sems, recv_sems, after)


def _sum_slots_own(landed, own, chip, name, block_rows=256):
    s_n, r_n, c_n = landed.shape
    br = _pick(r_n, block_rows, 8)

    def body(chip_ref, land_ref, own_ref, o_ref):
        acc = jnp.zeros((br, c_n), f32)
        for s in range(s_n):
            acc = acc + jnp.where(chip_ref[0] == s, own_ref[s], land_ref[s]).astype(f32)
        o_ref[...] = acc

    spec = pl.BlockSpec((s_n, br, c_n), lambda i, c: (0, i, 0))
    grid_spec = pltpu.PrefetchScalarGridSpec(num_scalar_prefetch=1, grid=(r_n // br,), in_specs=[spec, spec],
                                             out_specs=pl.BlockSpec((br, c_n), lambda i, c: (i, 0)))
    return pl.pallas_call(
        body, name=name, grid_spec=grid_spec, out_shape=jax.ShapeDtypeStruct((r_n, c_n), f32),
        compiler_params=_params(("parallel",)),
    )(chip, landed, own)


def _core_send_other_half(parts, name):
    n_slots, rows, cols = parts.shape
    half = rows // 2
    piece = half // D2D_STREAMS
    assert rows % (2 * D2D_STREAMS * 16) == 0

    def body(src, out, send_sems, recv_sems):
        x, y, c = lax.axis_index("x"), lax.axis_index("y"), lax.axis_index("c")
        sibling = (x, y, 1 - c)
        sends = []
        for k in range(n_slots):
            for s in range(D2D_STREAMS):
                theirs = pl.ds(pl.multiple_of((1 - c) * half + s * piece, 16), piece)
                i = k * D2D_STREAMS + s
                cp = _remote(src.at[k, theirs], out.at[k, pl.ds(s * piece, piece)], send_sems.at[i], recv_sems.at[i], sibling)
                cp.start()
                sends.append(cp)
        for cp in sends:
            cp.wait_recv()
        for cp in sends:
            cp.wait_send()

    n = n_slots * D2D_STREAMS
    return pl.pallas_call(
        body, name=name, in_specs=[_ANY], out_specs=_ANY, out_shape=jax.ShapeDtypeStruct((n_slots, half, cols), parts.dtype),
        scratch_shapes=[pltpu.SemaphoreType.DMA((n,)), pltpu.SemaphoreType.DMA((n,))],
    )(parts)


def _other_half_copies(src, land, send_sems, recv_sems):
    n_slots, rows, _ = src.shape
    half = rows // 2
    piece = half // D2D_STREAMS
    x, y, c = lax.axis_index("x"), lax.axis_index("y"), lax.axis_index("c")
    copies = []
    for k in range(n_slots):
        for s in range(D2D_STREAMS):
            theirs = pl.ds(pl.multiple_of((1 - c) * half + s * piece, 16), piece)
            i = k * D2D_STREAMS + s
            copies.append(_remote(src.at[k, theirs], land.at[k, pl.ds(s * piece, piece)], send_sems.at[i], recv_sems.at[i],
                                  (x, y, 1 - c)))
    return copies


def _core_send_other_half_start(parts, name):
    n_slots, rows, cols = parts.shape
    assert rows % (2 * D2D_STREAMS * 16) == 0
    n = n_slots * D2D_STREAMS
    land_shape = (n_slots, rows // 2, cols)

    def body(src, land, send_sems, recv_sems, src_out, land_out, token):
        for cp in _other_half_copies(src, land, send_sems, recv_sems):
            cp.start()
        token[...] = jnp.zeros_like(token)

    return pl.pallas_call(
        body, name=name,
        out_shape=(pltpu.SemaphoreType.DMA((n,)), pltpu.SemaphoreType.DMA((n,)), pltpu.HBM(parts.shape, parts.dtype),
                   pltpu.HBM(land_shape, parts.dtype), jax.ShapeDtypeStruct((8, LANES), f32)),
        in_specs=(_HBM, _HBM), out_specs=(_SEM, _SEM, _HBM, _HBM, pl.BlockSpec(memory_space=pltpu.VMEM)),
        input_output_aliases={0: 2, 1: 3}, compiler_params=pltpu.CompilerParams(has_side_effects=_EFFECT),
    )(pltpu.with_memory_space_constraint(parts, pltpu.HBM),
      pltpu.with_memory_space_constraint(lax.empty(land_shape, parts.dtype), pltpu.HBM))


def _core_send_other_half_wait(send_sems, recv_sems, parts, land, after, name):
    def body(src, land_ref, send_ref, recv_ref, after_ref, src_dead, land_out):
        for cp in _other_half_copies(src, land_ref, send_ref, recv_ref):
            cp.wait_send()
            cp.wait_recv()

    return pl.pallas_call(
        body, name=name, out_shape=(pltpu.HBM(parts.shape, parts.dtype), pltpu.HBM(land.shape, land.dtype)),
        in_specs=(_HBM, _HBM, _SEM, _SEM, _ANY), out_specs=(_HBM, _HBM), input_output_aliases={0: 0, 1: 1},
        compiler_params=pltpu.CompilerParams(has_side_effects=_EFFECT),
    )(parts, land, send_sems, recv_sems, after)


def _add_my_half(parts, other, core, name, block_rows=256):
    n_slots, rows, cols = parts.shape
    half = rows // 2
    br = _pick(half, block_rows, 16)
    nb = half // br

    def body(c_ref, a_ref, b_ref, o_ref):
        o_ref[...] = (a_ref[...].astype(f32) + b_ref[...].astype(f32)).astype(o_ref.dtype)

    grid_spec = pltpu.PrefetchScalarGridSpec(
        num_scalar_prefetch=1, grid=(n_slots, nb),
        in_specs=[pl.BlockSpec((1, br, cols), lambda k, i, c: (k, c[0] * nb + i, 0)),
                  pl.BlockSpec((1, br, cols), lambda k, i, c: (k, i, 0))],
        out_specs=pl.BlockSpec((1, br, cols), lambda k, i, c: (k, i, 0)))
    return pl.pallas_call(
        body, name=name, grid_spec=grid_spec, out_shape=jax.ShapeDtypeStruct((n_slots, half, cols), parts.dtype),
        compiler_params=_params(("parallel", "parallel")),
    )(core, parts, other)


def _core_join_halves(mine, name):
    half, cols = mine.shape
    piece = half // D2D_STREAMS
    assert half % (D2D_STREAMS * 16) == 0

    def body(src, out, send_sems, recv_sems, *stage):
        x, y, c = lax.axis_index("x"), lax.axis_index("y"), lax.axis_index("c")
        sibling = (x, y, 1 - c)
        sends = []
        for s in range(D2D_STREAMS):
            dst = out.at[pl.ds(pl.multiple_of(c * half + s * piece, 16), piece)]
            cp = _remote(src.at[pl.ds(s * piece, piece)], dst, send_sems.at[s], recv_sems.at[s], sibling)
            cp.start()
            sends.append(cp)
        _staged_copy(src, out.at[pl.ds(pl.multiple_of(c * half, 16), half)], *stage)
        for s in range(D2D_STREAMS):
            dst = out.at[pl.ds(pl.multiple_of((1 - c) * half + s * piece, 16), piece)]
            _remote(src.at[pl.ds(s * piece, piece)], dst, send_sems.at[s], recv_sems.at[s], sibling).wait_recv()
        for cp in sends:
            cp.wait_send()

    return pl.pallas_call(
        body, name=name, in_specs=[_ANY], out_specs=_ANY, out_shape=jax.ShapeDtypeStruct((2 * half, cols), mine.dtype),
        scratch_shapes=[pltpu.SemaphoreType.DMA((D2D_STREAMS,)), pltpu.SemaphoreType.DMA((D2D_STREAMS,))]
        + _stage_scratch(half, cols, mine.dtype),
    )(mine)


def _reduce_start(exchanging, core, after, tag):
    parts, other = _core_send_other_half_wait(*exchanging[:4], after, f"exchange_wait_{tag}")
    return _chip_scatter_start(_add_my_half(parts, other, core, f"sum_core_halves_{tag}"), f"scatter_start_{tag}")


def _reduce_finish(started, chip, after, tag):
    send_sems, recv_sems, chip_part, land, _ = started
    own, landed = _chip_scatter_wait(send_sems, recv_sems, chip_part, land, after, f"scatter_wait_{tag}")
    return _core_join_halves(_sum_slots_own(landed, own, chip, f"sum_chip_parts_{tag}"), f"join_core_halves_{tag}")


def _reduce_to_chips(parts, core, tag):
    chip_part = _add_my_half(parts, _core_send_other_half(parts, f"exchange_core_halves_{tag}"), core, f"sum_core_halves_{tag}")
    my_sum = _sum_slots(_chip_scatter(chip_part, f"scatter_{tag}"), f"sum_chip_parts_{tag}")
    return _core_join_halves(my_sum, f"join_core_halves_{tag}")


def _sum_slots(stack, name, block_rows=256):
    s_n, r_n, c_n = stack.shape
    br = _pick(r_n, block_rows, 8)

    def body(in_ref, o_ref):
        acc = in_ref[0].astype(f32)
        for s in range(1, s_n):
            acc = acc + in_ref[s].astype(f32)
        o_ref[...] = acc

    return pl.pallas_call(
        body, name=name, grid=(r_n // br,), in_specs=[pl.BlockSpec((s_n, br, c_n), lambda i: (0, i, 0))],
        out_specs=pl.BlockSpec((br, c_n), lambda i: (i, 0)), out_shape=jax.ShapeDtypeStruct((r_n, c_n), f32),
        compiler_params=_params(("parallel",)),
    )(stack)


def _concat_padded(parts, mult):
    rows = sum(p.shape[0] for p in parts)
    pad = (-rows) % mult
    if pad:
        parts = parts + [jnp.zeros((pad,) + parts[0].shape[1:], parts[0].dtype)]
    return jnp.concatenate(parts, axis=0)


def _pack_weights(w, l, names, conv_w=None):
    parts = [w[n][l].astype(bf16).reshape(-1, PACK_COLS) for n in names]
    if conv_w is not None:
        parts.append(_concat_padded([lax.bitcast_convert_type(conv_w, bf16).reshape(-1, PACK_COLS)], 16))
    return _concat_padded(parts, PACK_ROW_MULT)


def _unpack_weights(full, w, l, names, conv_w=None):
    start, r0 = {}, 0
    for n in names:
        start[n] = r0
        r0 += w[n][l].size // PACK_COLS

    def shards(n):
        rows = w[n][l].size // PACK_COLS
        return [full[k, start[n]:start[n] + rows].reshape(w[n].shape[1:]) for k in range(N_CHIPS)]

    if conv_w is None:
        return shards, None
    rows = conv_w.size * 2 // PACK_COLS
    pieces = lax.bitcast_convert_type(full[:, r0:r0 + rows].reshape((N_CHIPS,) + conv_w.shape + (2,)), f32)
    return shards, jnp.concatenate([pieces[k] for k in range(N_CHIPS)], axis=2)


def _pack_big_grads(layer_grads):
    parts, slot_rows = [], 0
    for k in range(N_CHIPS):
        slot = []
        for n in BIG:
            for g in layer_grads:
                width = g[n].shape[0] // N_CHIPS
                slot.append(g[n][k * width:(k + 1) * width].astype(bf16).reshape(-1, PACK_COLS))
        slot_rows = sum(p.shape[0] for p in slot)
        pad = (-slot_rows) % PACK_ROW_MULT
        if pad:
            slot.append(jnp.zeros((pad, PACK_COLS), bf16))
        slot_rows += pad
        parts += slot
    return jnp.concatenate(parts, axis=0).reshape(N_CHIPS, slot_rows, PACK_COLS)


def _unpack_big_grads(summed, w):
    out, r0 = {}, 0
    for n in BIG:
        rows = w[n][0].size // PACK_COLS
        out[n] = [s[r0:r0 + rows].reshape(w[n].shape[1:]) for s in summed]
        r0 += rows
    return out


_SMALL_TILE = 8 * LANES


def _pack_small(vals, names):
    parts = []
    for n in names:
        pieces = vals[n] if isinstance(vals[n], list) else [vals[n]]
        size = sum(p.size for p in pieces)
        if all(p.size % _SMALL_TILE == 0 for p in pieces):
            parts += [p.reshape(-1, LANES) for p in pieces]
        else:
            flat = [p.reshape(-1) for p in pieces] + [jnp.zeros(((-size) % _SMALL_TILE,), f32)]
            parts.append(jnp.concatenate(flat).reshape(-1, LANES))
    return _concat_padded(parts, PACK_ROW_MULT)


def _unpack_small(packed, like, names):
    out, r0 = {}, 0
    for n in names:
        size = like[n].size
        rows = -(-size // _SMALL_TILE) * 8
        out[n] = packed[r0:r0 + rows].reshape(-1)[:size].reshape(like[n].shape)
        r0 += rows
    return out


def _dims(w, x):
    d = {}
    d["D"] = x.shape[-1]
    d["T"] = x.shape[-2]
    d["DI"] = w["ssd_norm_g"].shape[-1]
    d["NH"] = w["ssd_dt_bias"].shape[-1]
    d["CD"] = w["ssd_conv_b"].shape[-1]
    d["G"] = SSD_N_GROUPS
    d["HPG"] = d["NH"] // d["G"]
    d["P"] = d["DI"] // d["NH"]
    d["N"] = (d["CD"] - d["DI"]) // (2 * d["G"])
    d["S5G"], d["S5N"] = w["s5_lambda_re"].shape[-2:]
    d["S5C"] = w["s5_b_re"].shape[-1]
    d["S5W"] = d["S5G"] * d["S5C"]
    d["NSB"] = d["S5W"] // S5_SUPERBLOCK
    d["GSB"] = d["S5G"] // d["NSB"]
    return d


def _head_pad(v, d):
    lead = v.shape[:-1]
    v = v.reshape(lead + (d["G"], d["HPG"]))
    v = jnp.concatenate([v, jnp.zeros(lead + (d["G"], LANES - d["HPG"]), v.dtype)], axis=-1)
    return v.reshape(lead + (d["G"] * LANES,))


def _head_unpad(v, d):
    lead = v.shape[:-1]
    return v.reshape(lead + (d["G"], LANES))[..., :d["HPG"]].reshape(lead + (d["NH"],))


def _w_in_perm(shards, d):
    o, nh = d["DI"] + d["CD"], d["NH"]
    r = shards[0].shape[0]

    def rows(lo, hi):
        out = []
        for k, s in enumerate(shards):
            a, b = max(lo, k * r), min(hi, (k + 1) * r)
            if a < b:
                out.append(s[a - k * r:b - k * r])
        return out

    dt = _head_pad(jnp.concatenate(rows(o, o + nh), axis=0).T, d).T
    return jnp.concatenate(rows(0, o) + rows(o + nh, len(shards) * r) + [dt], axis=0)


def _w_in_unperm(g, d):
    o = d["DI"] + d["CD"]
    rest = d["S5W"] + 2 * d["D"]
    return jnp.concatenate([g[:o], _head_unpad(g[o + rest:].T, d).T, g[o:o + rest]], axis=0)


def _s5_block_diag(v, d):
    gsb = d["GSB"]
    g, a, b = v.shape
    row_group = (lax.broadcasted_iota(jnp.int32, (g * a, gsb * b), 0) // a) % gsb
    col_group = lax.broadcasted_iota(jnp.int32, (g * a, gsb * b), 1) // b
    return jnp.where(row_group == col_group, jnp.tile(v.reshape(g * a, b), (1, gsb)), 0)


def _s5_diag_blocks(m, d, a, b):
    gsb = d["GSB"]
    rows = m.shape[0]
    m = m.reshape(rows, gsb, b)
    row_group = (lax.broadcasted_iota(jnp.int32, (rows, gsb, 1), 0) // a) % gsb
    col_group = lax.broadcasted_iota(jnp.int32, (rows, gsb, 1), 1)
    return jnp.sum(jnp.where(row_group == col_group, m, 0), axis=1).reshape(rows // a, a, b)


def _s5_lam_rows(v, d):
    v = v.reshape(d["NSB"], 1, d["GSB"] * d["S5N"])
    return jnp.broadcast_to(v, (d["NSB"], 8, v.shape[-1])).reshape(d["NSB"] * 8, -1)


def _ffn_fwd(h, pre_g, post_g, wgu, wd, tag):
    D = h.shape[1]
    H2 = wgu.shape[0]
    xn = _row_kernel(f"{tag}_norm", _fwd_of(_f_norm), [(h, D, 0)], [pre_g], [(D, bf16)])[0]
    ab = _mm(xn, wgu, "nt", bf16, f"{tag}_mm_up")
    hid = _row_kernel(f"{tag}_swiglu", _swiglu_fwd, [(ab, H2, 0)], [], [(H2 // 2, bf16)])[0]
    f = _mm(hid, wd, "nn", f32, f"{tag}_mm_down")
    out = _row_kernel(f"{tag}_resnorm", _fwd_of(_f_resnorm(0.5)), [(h, D, 0), (f, D, 0)], [post_g], [(D, f32)])[0]
    return out, dict(h=h, xn=xn, ab=ab, hid=hid, f=f)


def _ffn_bwd(dh_out, s, pre_g, post_g, wgu, wd, tag):
    D = dh_out.shape[1]
    H2 = wgu.shape[0]
    df, dpost = _row_kernel(f"{tag}_resnorm_bwd", _vjp_of(_f_post(0.5), 1, 1, [0]), [(s["f"], D, 0), (dh_out, D, 0)],
                            [post_g], [(D, bf16)], [post_g.shape])
    dwd = _mm(s["hid"], df, "tn", bf16, f"{tag}_mm_dwd")
    dhid = _mm(df, wd, "nt", bf16, f"{tag}_mm_dhid")
    dab = _row_kernel(f"{tag}_swiglu_bwd", _swiglu_bwd, [(s["ab"], H2, 0), (dhid, H2 // 2, 0)], [], [(H2, bf16)])[0]
    dwgu = _mm(dab, s["xn"], "tn", bf16, f"{tag}_mm_dwgu")
    dxn = _mm(dab, wgu, "nn", f32, f"{tag}_mm_dxn")
    dh, dpre = _row_kernel(f"{tag}_norm_bwd", _vjp_of(_f_norm, 1, 1, [0], 1), [(s["h"], D, 0), (dxn, D, 0), (dh_out, D, 0)],
                           [pre_g], [(D, f32)], [pre_g.shape])
    return dh, dict(pre_g=dpre, post_g=dpost, wgu=dwgu, wd=dwd)


def _mixer_fwd(h, p, d):
    D, DI, CD, G, N = d["D"], d["DI"], d["CD"], d["G"], d["N"]
    gl = G * LANES
    c_u5, c_ga, c_gb, c_dt = DI + CD, DI + CD + d["S5W"], DI + CD + d["S5W"] + D, DI + CD + d["S5W"] + 2 * D
    u = _row_kernel("mix_norm", _fwd_of(_f_norm), [(h, D, 0)], [p["mix_pre_g"]], [(D, bf16)])[0]
    proj = _mm(u, p["w_in"], "nt", f32, "mix_mm_in", bn_t=512)
    act = _conv_fwd(proj, DI, p["conv_w"], p["conv_b"], "ssd_conv")
    dt, adt = _row_kernel("ssd_dt", _fwd_of(_f_dt), [(proj, gl, c_dt // gl)], [p["dt_bias"], p["a_log"]], [(gl, f32)] * 2)
    y_ssd, states = _ssd_fwd(act, dt, adt, p["d_skip"], d["HPG"], d["P"], N, "ssd_scan")
    nrm = _row_kernel("ssd_post", _fwd_of(_f_ssdpost(G)), [(y_ssd, DI, 0), (proj, DI, 0)], [p["norm_g"]], [(DI, bf16)])[0]
    y_a = _mm(nrm, p["w_a"], "nn", f32, "mix_mm_a")
    u5 =(proj, d["S5W"], c_u5 // d["S5W"])
    bu = _s5_in(proj, c_u5, p["bsb"], d)
    s5st = _s5_scan_fwd(bu, p["lam_re_rows"], p["lam_im_rows"], d["NSB"], "s5_scan")
    y5 = _bdmm(s5st, p["csb"], "nn", d["NSB"], f32, "s5_mm_c")
    gel = _row_kernel("s5_post", _fwd_of(_f_s5post), [(y5, d["S5W"], 0), u5], [p["s5_d"]], [(d["S5W"], bf16)])[0]
    vg = _mm(gel, p["w_glu"], "nt", bf16, "mix_mm_glu")
    glu = _row_kernel("s5_glu", _glu_fwd, [(vg, vg.shape[1], 0)], [], [(vg.shape[1] // 2, bf16)])[0]
    y_b = _mm(glu, p["w_b"], "nn", f32, "mix_mm_b")
    merged = _row_kernel("mix_merge", _fwd_of(_f_merge), [(proj, D, c_ga // D), (y_a, D, 0), (proj, D, c_gb // D), (y_b, D, 0)],
                         [], [(D, bf16)])[0]
    m = _mm(merged, p["w_out"], "nn", f32, "mix_mm_out")
    out = _row_kernel("mix_resnorm", _fwd_of(_f_resnorm(1.0)), [(h, D, 0), (m, D, 0)], [p["mix_post_g"]], [(D, f32)])[0]
    return out, dict(h=h, u=u, proj=proj, act=act, dt=dt, adt=adt, states=states, y_ssd=y_ssd, nrm=nrm, y_a=y_a, s5st=s5st,
                     y5=y5, gel=gel, vg=vg, glu=glu, y_b=y_b, merged=merged, m=m)


def _s5_in(proj, c_u5, bsb, d):
    T = proj.shape[0]
    nsb = d["NSB"]
    ka, nw = S5_SUPERBLOCK, bsb.shape[1]
    off = c_u5 // ka
    assert c_u5 % ka == 0
    bt = _pick(T, 512)

    def body(a_ref, w_ref, o_ref):
        o_ref[...] = _dot(a_ref[...].astype(bf16), w_ref[...].astype(bf16))

    return pl.pallas_call(
        body, name="s5_mm_bu", grid=(nsb, T // bt),
        in_specs=[pl.BlockSpec((bt, ka), lambda j, i: (i, off + j)), pl.BlockSpec((ka, nw), lambda j, i: (j, 0))],
        out_specs=pl.BlockSpec((bt, nw), lambda j, i: (i, j)), out_shape=jax.ShapeDtypeStruct((T, nsb * nw), f32),
        compiler_params=_params(("parallel", "parallel")),
    )(proj, bsb)


def _s5_dbsb(proj, c_u5, a, d):
    T = proj.shape[0]
    nsb = d["NSB"]
    ka, nw = S5_SUPERBLOCK, a.shape[1] // nsb
    off = c_u5 // ka
    bt = _pick(T, 512)

    def body(u_ref, a_ref, o_ref):
        pr = _dot(u_ref[...].astype(bf16), a_ref[...].astype(bf16), "tn")
        k = pl.program_id(1)

        @pl.when(k == 0)
        def _():
            o_ref[...] = pr

        @pl.when(k > 0)
        def _():
            o_ref[...] += pr

    return pl.pallas_call(
        body, name="s5_mm_dbsb", grid=(nsb, T // bt),
        in_specs=[pl.BlockSpec((bt, ka), lambda j, k: (k, off + j)), pl.BlockSpec((bt, nw), lambda j, k: (k, j))],
        out_specs=pl.BlockSpec((ka, nw), lambda j, k: (j, 0)), out_shape=jax.ShapeDtypeStruct((nsb * ka, nw), f32),
        compiler_params=_params(("parallel", "arbitrary")),
    )(proj, a)


def _mixer_bwd(dh_out, s, p, d):
    D, DI, CD, G, N, S5W = d["D"], d["DI"], d["CD"], d["G"], d["N"], d["S5W"]
    gl = G * LANES
    gn = G * N
    c_u5, c_ga, c_gb, c_dt = DI + CD, DI + CD + S5W, DI + CD + S5W + D, DI + CD + S5W + 2 * D
    proj = s["proj"]
    g = {}
    dm, g["mix_post_g"] = _row_kernel("mix_resnorm_bwd", _vjp_of(_f_post(1.0), 1, 1, [0]), [(s["m"], D, 0), (dh_out, D, 0)],
                                      [p["mix_post_g"]], [(D, bf16)], [p["mix_post_g"].shape])
    g["w_out"] = _mm(s["merged"], dm, "tn", bf16, "mix_mm_dwout")
    dmerged = _mm(dm, p["w_out"], "nt", f32, "mix_mm_dmerged")
    dga, dya, dgb, dyb = _row_kernel(
        "mix_merge_bwd", _vjp_of(_f_merge, 4, 1, [0, 1, 2, 3]),
        [(proj, D, c_ga // D), (s["y_a"], D, 0), (proj, D, c_gb // D), (s["y_b"], D, 0), (dmerged, D, 0)], [],
        [(D, bf16), (D, bf16), (D, bf16), (D, bf16)])
    g["w_a"] = _mm(s["nrm"], dya, "tn", bf16, "mix_mm_dwa")
    dnrm = _mm(dya, p["w_a"], "nt", f32, "mix_mm_dnrm")
    dy_ssd, dz, g["norm_g"] = _row_kernel(
        "ssd_post_bwd", _vjp_of(_f_ssdpost(G), 2, 1, [0, 1]), [(s["y_ssd"], DI, 0), (proj, DI, 0), (dnrm, DI, 0)],
        [p["norm_g"]], [(DI, f32), (DI, bf16)], [p["norm_g"].shape])
    dxs, d_b, d_c, ddt, dadt, dd = _ssd_bwd(s["act"], s["dt"], s["adt"], p["d_skip"], s["states"], dy_ssd,
                                            d["HPG"], d["P"], N, "ssd_scan_bwd")
    g["d_skip"] = dd.reshape(G, 8, LANES)[:, 0, :].reshape(1, gl)
    ddt_raw, g["dt_bias"], g["a_log"] = _row_kernel(
        "ssd_dt_bwd", _vjp_of(_f_dt, 1, 2, [0]), [(proj, gl, c_dt // gl), (ddt, gl, 0), (dadt, gl, 0)],
        [p["dt_bias"], p["a_log"]], [(gl, bf16)], [p["dt_bias"].shape, p["a_log"].shape])
    cw, cb = p["conv_w"], p["conv_b"]
    dxc_x, dw_x, db_x = _conv_bwd(proj, DI, cw[:, :DI], cb[:, :DI], dxs, "ssd_conv_bwd_x")
    dxc_b, dw_b, db_b = _conv_bwd(proj, 2 * DI, cw[:, DI:DI + gn], cb[:, DI:DI + gn], d_b, "ssd_conv_bwd_b")
    dxc_c, dw_c, db_c = _conv_bwd(proj, 2 * DI + gn, cw[:, DI + gn:], cb[:, DI + gn:], d_c, "ssd_conv_bwd_c")
    g["conv_w"] = jnp.concatenate([dw_x, dw_b, dw_c], axis=1)
    g["conv_b"] = jnp.concatenate([db_x, db_b, db_c], axis=1)
    g["w_b"] = _mm(s["glu"], dyb, "tn", bf16, "mix_mm_dwb")
    dglu = _mm(dyb, p["w_b"], "nt", f32, "mix_mm_dglu")
    dvg = _row_kernel("s5_glu_bwd", _glu_bwd, [(s["vg"], s["vg"].shape[1], 0), (dglu, S5W, 0)], [], [(s["vg"].shape[1], bf16)])[0]
    g["w_glu"] = _mm(dvg, s["gel"], "tn", bf16, "mix_mm_dwglu")
    dgel = _mm(dvg, p["w_glu"], "nn", f32, "mix_mm_dgel")
    dy5, du5a, g["s5_d"] = _row_kernel(
        "s5_post_bwd", _vjp_of(_f_s5post, 2, 1, [0, 1]), [(s["y5"], S5W, 0), (proj, S5W, c_u5 // S5W), (dgel, S5W, 0)],
        [p["s5_d"]], [(S5W, bf16), (S5W, f32)], [p["s5_d"].shape])
    g["csb"] = _bdmm(s["s5st"], dy5, "tn", d["NSB"], f32, "s5_mm_dcsb")
    gst = _bdmm(dy5, p["csb"], "nt", d["NSB"], f32, "s5_mm_gst")
    a, g["lam_re_rows"], g["lam_im_rows"] = _s5_scan_bwd(gst, s["s5st"], p["lam_re_rows"], p["lam_im_rows"], d["NSB"], "s5_scan_bwd")
    g["bsb"] = _s5_dbsb(proj, c_u5, a, d)
    du5b = _bdmm(a, p["bsb"], "nt", d["NSB"], f32, "s5_mm_du5")
    du5 = _row_kernel("s5_du5", _add_fn, [(du5a, S5W, 0), (du5b, S5W, 0)], [], [(S5W, bf16)])[0]
    dproj = jnp.concatenate([dz, dxc_x, dxc_b, dxc_c, du5, dga, dgb, ddt_raw], axis=1)
    g["w_in"] = _mm(dproj, s["u"], "tn", bf16, "mix_mm_dwin")
    du = _mm(dproj, p["w_in"], "nn", f32, "mix_mm_du", bk_t=2176)
    dh, g["mix_pre_g"] = _row_kernel("mix_norm_bwd", _vjp_of(_f_norm, 1, 1, [0], 1), [(s["h"], D, 0), (du, D, 0), (dh_out, D, 0)],
                                     [p["mix_pre_g"]], [(D, f32)], [p["mix_pre_g"].shape])
    return dh, g


def _ffn1_params(l, w, wf):
    whole = lambda *names: jnp.concatenate([s for n in names for s in wf(n)], axis=0)
    return dict(ffn1_pre_g=w["ffn1_pre_g"][l].reshape(1, -1), ffn1_post_g=w["ffn1_post_g"][l].reshape(1, -1),
                wgu1=whole("ffn1_w_gate", "ffn1_w_up"), wd1=whole("ffn1_w_down"))


def _layer_params(l, w, wf, conv_w_full, d):
    r2 = lambda v: v[l].reshape(1, -1)
    p = {}
    for n in ["mix_pre_g", "mix_post_g", "ffn2_pre_g", "ffn2_post_g", "s5_d"]:
        p[n] = r2(w[n])
    whole = lambda *names: jnp.concatenate([s for n in names for s in wf(n)], axis=0)
    p["wgu2"] = whole("ffn2_w_gate", "ffn2_w_up")
    p["wd2"] = whole("ffn2_w_down")
    p["w_in"] = _w_in_perm(wf("w_in"), d)
    p["w_a"], p["w_glu"], p["w_b"], p["w_out"] = whole("w_branch_a"), whole("s5_w_glu"), whole("w_branch_b"), whole("w_out")
    p["conv_w"] = conv_w_full[l]
    p["conv_b"] = r2(w["ssd_conv_b"])
    p["dt_bias"] = _head_pad(r2(w["ssd_dt_bias"]), d)
    p["a_log"] = _head_pad(r2(w["ssd_a_log"]), d)
    p["d_skip"] = _head_pad(r2(w["ssd_d"]), d)
    p["norm_g"] = r2(w["ssd_norm_g"])
    g5, n5, c5 = d["S5G"], d["S5N"], d["S5C"]
    expand = jnp.repeat(jnp.eye(n5, dtype=f32), c5, axis=1)
    prep_in = [w["s5_lambda_re"][l], w["s5_lambda_im"][l], w["s5_log_step"][l].reshape(g5, 1),
               w["s5_b_re"][l].reshape(g5, n5 * c5), w["s5_b_im"][l].reshape(g5, n5 * c5), expand]
    lbr, lbi, bbr, bbi = _s5_prep(prep_in, "s5_prep")
    p["s5_prep_in"] = prep_in
    p["lam_re_rows"], p["lam_im_rows"] = _s5_lam_rows(lbr, d), _s5_lam_rows(lbi, d)
    to_cn = lambda v: v.astype(bf16).reshape(g5, n5, c5).transpose(0, 2, 1)
    p["bsb"] = jnp.concatenate([_s5_block_diag(to_cn(bbr), d), _s5_block_diag(to_cn(bbi), d)], axis=1)
    c_re = w["s5_c_re"][l].astype(bf16).transpose(0, 2, 1)
    c_im = w["s5_c_im"][l].astype(bf16).transpose(0, 2, 1)
    nsb = d["NSB"]
    csb = jnp.stack([_s5_block_diag(c_re, d).reshape(nsb, -1, S5_SUPERBLOCK),
                     _s5_block_diag(-c_im, d).reshape(nsb, -1, S5_SUPERBLOCK)], axis=1)
    p["csb"] = csb.reshape(-1, S5_SUPERBLOCK)
    return p


def _s5_param_grads(g, p, d, l):
    g5, n5, c5, nsb, gsb = d["S5G"], d["S5N"], d["S5C"], d["NSB"], d["GSB"]
    wst = gsb * n5
    dbsb = g["bsb"]
    from_cn = lambda v: v.transpose(0, 2, 1).reshape(g5, n5 * c5)
    dbbr = from_cn(_s5_diag_blocks(dbsb[:, :wst], d, c5, n5))
    dbbi = from_cn(_s5_diag_blocks(dbsb[:, wst:], d, c5, n5))
    rows = lambda v: v.reshape(nsb, 8, wst)[:, 0, :].reshape(g5, n5)
    cots = [rows(g["lam_re_rows"]), rows(g["lam_im_rows"]), dbbr, dbbi]
    dlr, dli, dls, dbr, dbi = _s5_prep_bwd(p["s5_prep_in"], cots, "s5_prep_bwd")
    dcsb = g["csb"].reshape(nsb, 2, wst, S5_SUPERBLOCK)
    dcr = _s5_diag_blocks(dcsb[:, 0].reshape(-1, S5_SUPERBLOCK), d, n5, c5).transpose(0, 2, 1)
    dci = -_s5_diag_blocks(dcsb[:, 1].reshape(-1, S5_SUPERBLOCK), d, n5, c5).transpose(0, 2, 1)
    return dict(s5_lambda_re=dlr, s5_lambda_im=dli, s5_log_step=dls.reshape(g5), s5_b_re=dbr.reshape(g5, n5, c5),
                s5_b_im=dbi.reshape(g5, n5, c5), s5_c_re=dcr, s5_c_im=dci)


def kernel(x, ffn1_pre_g, ffn1_post_g, ffn1_w_gate, ffn1_w_up, ffn1_w_down, mix_pre_g, mix_post_g, w_in, ssd_conv_w, ssd_conv_b, ssd_dt_bias, ssd_a_log, ssd_d, ssd_norm_g, w_branch_a, s5_lambda_re, s5_lambda_im, s5_b_re, s5_b_im, s5_c_re, s5_c_im, s5_log_step, s5_d, s5_w_glu, w_branch_b, w_out, ffn2_pre_g, ffn2_post_g, ffn2_w_gate, ffn2_w_up, ffn2_w_down, loss_target, m_ffn1_pre_g, m_ffn1_post_g, m_ffn1_w_gate, m_ffn1_w_up, m_ffn1_w_down, m_mix_pre_g, m_mix_post_g, m_w_in, m_ssd_conv_w, m_ssd_conv_b, m_ssd_dt_bias, m_ssd_a_log, m_ssd_d, m_ssd_norm_g, m_w_branch_a, m_s5_lambda_re, m_s5_lambda_im, m_s5_b_re, m_s5_b_im, m_s5_c_re, m_s5_c_im, m_s5_log_step, m_s5_d, m_s5_w_glu, m_w_branch_b, m_w_out, m_ffn2_pre_g, m_ffn2_post_g, m_ffn2_w_gate, m_ffn2_w_up, m_ffn2_w_down, v_ffn1_pre_g, v_ffn1_post_g, v_ffn1_w_gate, v_ffn1_w_up, v_ffn1_w_down, v_mix_pre_g, v_mix_post_g, v_w_in, v_ssd_conv_w, v_ssd_conv_b, v_ssd_dt_bias, v_ssd_a_log, v_ssd_d, v_ssd_norm_g, v_w_branch_a, v_s5_lambda_re, v_s5_lambda_im, v_s5_b_re, v_s5_b_im, v_s5_c_re, v_s5_c_im, v_s5_log_step, v_s5_d, v_s5_w_glu, v_w_branch_b, v_w_out, v_ffn2_pre_g, v_ffn2_post_g, v_ffn2_w_gate, v_ffn2_w_up, v_ffn2_w_down):
    given = dict(locals())
    for n in COL_SHARDED:
        for prefix in ("", "m_", "v_"):
            given[prefix + n] = given[prefix + n].transpose(0, 2, 1)
    w = {n: given[n] for n in WEIGHTS}
    mom = {n: given["m_" + n] for n in WEIGHTS}
    var = {n: given["v_" + n] for n in WEIGHTS}
    d = _dims(w, x)
    n_layers = w["ffn1_pre_g"].shape[0]
    T, D = d["T"], d["D"]

    conv_w = w["ssd_conv_w"]
    first, rest = BIG[:3], BIG[3:]
    wf, _ = _unpack_weights(_chip_all_gather(_pack_weights(w, 0, first), "gather_weights_first"), w, 0, first)
    coming = _chip_gather_start(_pack_weights(w, 0, rest, conv_w), "gather_start_l0")
    h = x.reshape(T, D) + coming[-1][0, 0]
    saved, layers = [], []
    for l in range(n_layers):
        if l > 0:
            pack, land = _chip_gather_wait(*coming[:4], h, f"gather_wait_l{l}")
            wf, _ = _unpack_weights(_chip_gather_finish(pack, land, f"gather_finish_l{l}"), w, l, BIG)
        p = _ffn1_params(l, w, wf)
        h, s1 = _ffn_fwd(h, p["ffn1_pre_g"], p["ffn1_post_g"], p["wgu1"], p["wd1"], "ffn1")
        if l == 0:
            pack, land = _chip_gather_wait(*coming[:4], h, "gather_wait_l0")
            wf, conv_w_full = _unpack_weights(_chip_gather_finish(pack, land, "gather_finish_l0"), w, 0, rest, conv_w)
        if l + 1 < n_layers:
            coming = _chip_gather_start(_pack_weights(w, l + 1, BIG), f"gather_start_l{l + 1}")
            h = h + coming[-1][0, 0]
        p.update(_layer_params(l, w, wf, conv_w_full, d))
        layers.append(p)
        h, sm = _mixer_fwd(h, p, d)
        h, s2 = _ffn_fwd(h, p["ffn2_pre_g"], p["ffn2_post_g"], p["wgu2"], p["wd2"], "ffn2")
        saved.append((s1, sm, s2))
    dh, loss_part = _row_kernel("loss", _loss_fn, [(h, D, 0), (loss_target.reshape(T, D), D, 0)], [], [(D, f32)], [(8, LANES)])
    loss = lax.psum(loss_part[0, 0], ("x", "y", "c"))

    my_core = lax.axis_index("c").astype(jnp.int32).reshape(1)
    my_chip = (2 * lax.axis_index("x") + lax.axis_index("y")).astype(jnp.int32).reshape(1)
    lg, exchanging, in_flight = [None] * n_layers, [None] * n_layers, [None] * n_layers
    for l in reversed(range(n_layers)):
        p = layers[l]
        s1, sm, s2 = saved[l]
        dh, g2 = _ffn_bwd(dh, s2, p["ffn2_pre_g"], p["ffn2_post_g"], p["wgu2"], p["wd2"], "ffn2")
        if l + 1 < n_layers:
            in_flight[l + 1] = _reduce_start(exchanging[l + 1], my_core, dh, f"grads_l{l + 1}")
            dh = dh + in_flight[l + 1][-1][0, 0]
        dh, gm = _mixer_bwd(dh, sm, p, d)
        dh, g1 = _ffn_bwd(dh, s1, p["ffn1_pre_g"], p["ffn1_post_g"], p["wgu1"], p["wd1"], "ffn1")
        H = p["wd1"].shape[0]
        gl = dict(ffn1_pre_g=g1["pre_g"], ffn1_post_g=g1["post_g"], ffn1_w_gate=g1["wgu"][:H], ffn1_w_up=g1["wgu"][H:],
                  ffn1_w_down=g1["wd"], ffn2_pre_g=g2["pre_g"], ffn2_post_g=g2["post_g"], ffn2_w_gate=g2["wgu"][:H],
                  ffn2_w_up=g2["wgu"][H:], ffn2_w_down=g2["wd"], mix_pre_g=gm["mix_pre_g"], mix_post_g=gm["mix_post_g"],
                  w_in=_w_in_unperm(gm["w_in"], d), ssd_conv_w=gm["conv_w"], ssd_conv_b=gm["conv_b"],
                  ssd_dt_bias=_head_unpad(gm["dt_bias"], d), ssd_a_log=_head_unpad(gm["a_log"], d),
                  ssd_d=_head_unpad(gm["d_skip"], d), ssd_norm_g=gm["norm_g"], w_branch_a=gm["w_a"], s5_d=gm["s5_d"],
                  s5_w_glu=gm["w_glu"], w_branch_b=gm["w_b"], w_out=gm["w_out"])
        gl.update(_s5_param_grads(gm, p, d, l))
        lg[l] = gl
        exchanging[l] = _core_send_other_half_start(_pack_big_grads([gl]), f"exchange_start_grads_l{l}")
        if l > 0:
            dh = dh + exchanging[l][-1][0, 0]
    grad_x = dh.reshape(x.shape)
    summed = [None] * n_layers
    for l in range(1, n_layers):
        summed[l] = _reduce_finish(in_flight[l], my_chip, exchanging[0][-1], f"grads_l{l}")
    in_flight[0] = _reduce_start(exchanging[0], my_core, summed[-1] if n_layers > 1 else grad_x, "grads_l0")
    small_names = SMALL + ["ssd_conv_w"]
    small_parts = {n: [g[n] for g in lg] for n in small_names}
    small_like = {n: jax.ShapeDtypeStruct((n_layers,) + lg[0][n].shape, f32) for n in small_names}
    small_like.update({n: w[n] for n in SMALL})
    small_pack = _pack_small(small_parts, small_names)
    small_sum = _reduce_to_chips(jnp.broadcast_to(small_pack, (N_CHIPS,) + small_pack.shape), my_core, "small")
    small = _unpack_small(small_sum, small_like, small_names)
    summed[0] = _reduce_finish(in_flight[0], my_chip, small_sum, "grads_l0")
    big_grads = _unpack_big_grads(summed, w)
    grads = {}
    k_me = 2 * lax.axis_index("x") + lax.axis_index("y")
    cw = w["ssd_conv_w"].shape[-1]
    small["ssd_conv_w"] = lax.dynamic_slice_in_dim(small["ssd_conv_w"], k_me * cw, cw, axis=2)
    grads.update(small)

    delta, new_m, new_v = {}, {}, {}
    for n in BIG:
        grads[n], delta[n], new_m[n], new_v[n] = _adamw_layers(w[n], big_grads[n], mom[n], var[n], "adamw_" + n)
    n = "ssd_conv_w"
    delta[n], new_m[n], new_v[n] = _adamw(w[n], grads[n], mom[n], var[n], "adamw_" + n)
    pw, pm, pv = [_pack_small(t, SMALL) for t in (w, mom, var)]
    assert pw.shape[0] <= small_sum.shape[0]
    sd, sm_, sv = _adamw(pw, small_sum[:pw.shape[0]], pm, pv, "adamw_small")
    delta.update(_unpack_small(sd, w, SMALL))
    new_m.update(_unpack_small(sm_, w, SMALL))
    new_v.update(_unpack_small(sv, w, SMALL))
    for n in COL_SHARDED:
        for out in (grads, delta, new_m, new_v):
            out[n] = out[n].transpose(0, 2, 1)
    return (loss, grad_x, *[grads[n] for n in WEIGHTS], *[delta[n] for n in WEIGHTS],
            *[new_m[n] for n in WEIGHTS], *[new_v[n] for n in WEIGHTS])
```

```python
import functools

import numpy as np
import jax
import jax.numpy as jnp
from jax import lax
from jax.experimental import pallas as pl
from jax.experimental.pallas import tpu as pltpu

f32, bf16 = jnp.float32, jnp.bfloat16

SSD_N_GROUPS = 4
SSD_CHUNK = 128
RMS_EPS = 1e-6
S5_MAX_REAL = -1e-4
S5_SUPERBLOCK = 256
ADAM_LR, ADAM_B1, ADAM_B2, ADAM_EPS, ADAM_WD, ADAM_STEP = 0.001, 0.9, 0.999, 1e-08, 0.01, 10

LANES = 128
PACK_COLS = 1024
D2D_STREAMS = 16
PACK_ROW_MULT = 2 * D2D_STREAMS * 16
VMEM_LIMIT_BYTES = 48 * 1024 * 1024
N_CHIPS, N_CORES, N_DEV = 4, 2, 8
MESH = pl.DeviceIdType.MESH

BIG = ["ffn1_w_gate", "ffn1_w_up", "ffn1_w_down", "w_in", "w_branch_a", "s5_w_glu", "w_branch_b", "w_out",
       "ffn2_w_gate", "ffn2_w_up", "ffn2_w_down"]
COL_SHARDED = ["ffn1_w_gate", "ffn1_w_up", "w_in", "s5_w_glu", "ffn2_w_gate", "ffn2_w_up"]
SMALL = ["ffn1_pre_g", "ffn1_post_g", "mix_pre_g", "mix_post_g", "ssd_conv_b", "ssd_norm_g", "s5_lambda_re", "s5_lambda_im",
         "s5_b_re", "s5_b_im", "s5_c_re", "s5_c_im", "s5_d", "ffn2_pre_g", "ffn2_post_g", "s5_log_step", "ssd_dt_bias",
         "ssd_a_log", "ssd_d"]
WEIGHTS = ["ffn1_pre_g", "ffn1_post_g", "ffn1_w_gate", "ffn1_w_up", "ffn1_w_down", "mix_pre_g", "mix_post_g", "w_in",
           "ssd_conv_w", "ssd_conv_b", "ssd_dt_bias", "ssd_a_log", "ssd_d", "ssd_norm_g", "w_branch_a", "s5_lambda_re",
           "s5_lambda_im", "s5_b_re", "s5_b_im", "s5_c_re", "s5_c_im", "s5_log_step", "s5_d", "s5_w_glu", "w_branch_b",
           "w_out", "ffn2_pre_g", "ffn2_post_g", "ffn2_w_gate", "ffn2_w_up", "ffn2_w_down"]


def _params(sem=None):
    return pltpu.CompilerParams(dimension_semantics=sem, vmem_limit_bytes=VMEM_LIMIT_BYTES)


def _pick(n, target, mult=LANES):
    best = None
    for d in range(mult, min(n, target) + 1, mult):
        if n % d == 0:
            best = d
    return best if best is not None else n


_DIMS = {"nn": (((1,), (0,)), ((), ())), "nt": (((1,), (1,)), ((), ())), "tn": (((0,), (0,)), ((), ()))}


def _mm(a, b, mode, out_dtype, name, bm_t=1024, bn_t=1024, bk_t=2816):
    if mode == "nn":
        (M, K), (K2, N) = a.shape, b.shape
    elif mode == "nt":
        (M, K), (N, K2) = a.shape, b.shape
    else:
        (K, M), (K2, N) = a.shape, b.shape
    assert K == K2, (name, a.shape, b.shape)
    bm, bn, bk = _pick(M, bm_t), _pick(N, bn_t), _pick(K, bk_t)
    nk = K // bk
    dn = _DIMS[mode]

    def body(a_ref, b_ref, o_ref, *scratch):
        p = lax.dot_general(a_ref[...].astype(bf16), b_ref[...].astype(bf16), dn, preferred_element_type=f32)
        if nk == 1:
            o_ref[...] = p.astype(o_ref.dtype)
        else:
            acc = scratch[0]
            k = pl.program_id(2)

            @pl.when(k == 0)
            def _():
                acc[...] = p

            @pl.when(k > 0)
            def _():
                acc[...] += p

            @pl.when(k == nk - 1)
            def _():
                o_ref[...] = acc[...].astype(o_ref.dtype)

    if mode == "tn":
        a_spec = pl.BlockSpec((bk, bm), lambda i, j, k: (k, i))
    else:
        a_spec = pl.BlockSpec((bm, bk), lambda i, j, k: (i, k))
    if mode == "nt":
        b_spec = pl.BlockSpec((bn, bk), lambda i, j, k: (j, k))
    else:
        b_spec = pl.BlockSpec((bk, bn), lambda i, j, k: (k, j))
    return pl.pallas_call(
        body, name=name, grid=(M // bm, N // bn, nk), in_specs=[a_spec, b_spec],
        out_specs=pl.BlockSpec((bm, bn), lambda i, j, k: (i, j)), out_shape=jax.ShapeDtypeStruct((M, N), out_dtype),
        scratch_shapes=[pltpu.VMEM((bm, bn), f32)] if nk > 1 else [],
        compiler_params=_params(("parallel", "parallel", "arbitrary")),
    )(a, b)


def _bdmm(a, w, mode, nb, out_dtype, name, bt_t=512):
    if mode == "tn":
        T = a.shape[0]
        ka, nw = a.shape[1] // nb, w.shape[1] // nb
        bt = _pick(T, bt_t)
        nt = T // bt

        def body_tn(a_ref, b_ref, o_ref):
            p = lax.dot_general(a_ref[...].astype(bf16), b_ref[...].astype(bf16), _DIMS["tn"], preferred_element_type=f32)
            k = pl.program_id(1)

            @pl.when(k == 0)
            def _():
                o_ref[...] = p

            @pl.when(k > 0)
            def _():
                o_ref[...] += p

        return pl.pallas_call(
            body_tn, name=name, grid=(nb, nt),
            in_specs=[pl.BlockSpec((bt, ka), lambda j, k: (k, j)), pl.BlockSpec((bt, nw), lambda j, k: (k, j))],
            out_specs=pl.BlockSpec((ka, nw), lambda j, k: (j, 0)), out_shape=jax.ShapeDtypeStruct((nb * ka, nw), f32),
            compiler_params=_params(("parallel", "arbitrary")),
        )(a, w)
    T = a.shape[0]
    ka, nw = w.shape[0] // nb, w.shape[1]
    bt = _pick(T, bt_t)
    kin, kout = (ka, nw) if mode == "nn" else (nw, ka)
    dn = _DIMS[mode]

    def body(a_ref, w_ref, o_ref):
        o_ref[...] = lax.dot_general(a_ref[...].astype(bf16), w_ref[...].astype(bf16), dn,
                                     preferred_element_type=f32).astype(o_ref.dtype)

    return pl.pallas_call(
        body, name=name, grid=(nb, T // bt),
        in_specs=[pl.BlockSpec((bt, kin), lambda j, i: (i, j)), pl.BlockSpec((ka, nw), lambda j, i: (j, 0))],
        out_specs=pl.BlockSpec((bt, kout), lambda j, i: (i, j)), out_shape=jax.ShapeDtypeStruct((T, nb * kout), out_dtype),
        compiler_params=_params(("parallel", "parallel")),
    )(a, w)


def _row_index(i, cb):
    return (i, cb)


def _row_kernel(name, fn, rows, pars, row_outs, par_outs=(), block_rows=256):
    T = rows[0][0].shape[0]
    R = min(block_rows, T)
    assert T % R == 0
    nr, npar, nro = len(rows), len(pars), len(row_outs)

    def body(*refs):
        rv = [r[...] for r in refs[:nr]]
        pv = [r[...] for r in refs[nr:nr + npar]]
        ro, po = fn(rv, pv)
        for ref, v in zip(refs[nr + npar:nr + npar + nro], ro):
            ref[...] = v.astype(ref.dtype)
        if par_outs:
            i = pl.program_id(0)
            prefs = refs[nr + npar + nro:]

            @pl.when(i == 0)
            def _():
                for ref, v in zip(prefs, po):
                    ref[...] = v.astype(f32)

            @pl.when(i > 0)
            def _():
                for ref, v in zip(prefs, po):
                    ref[...] += v.astype(f32)

    in_specs = [pl.BlockSpec((R, nc), functools.partial(_row_index, cb=cb)) for (_, nc, cb) in rows]
    in_specs += [pl.BlockSpec(p.shape, lambda i: (0, 0)) for p in pars]
    out_specs = [pl.BlockSpec((R, nc), lambda i: (i, 0)) for (nc, _) in row_outs]
    out_specs += [pl.BlockSpec(s, lambda i: (0, 0)) for s in par_outs]
    out_shape = [jax.ShapeDtypeStruct((T, nc), dt) for (nc, dt) in row_outs]
    out_shape += [jax.ShapeDtypeStruct(s, f32) for s in par_outs]
    outs = pl.pallas_call(
        body, name=name, grid=(T // R,), in_specs=in_specs, out_specs=out_specs, out_shape=out_shape,
        compiler_params=_params(("arbitrary",) if par_outs else ("parallel",)),
    )(*[r[0] for r in rows], *pars)
    return list(outs)


def _fwd_of(f):
    def fn(rv, pv):
        return f([v.astype(f32) for v in rv], [v.astype(f32) for v in pv]), []
    return fn


def _vjp_of(f, n_x, n_cot, grad_idx, n_add=0):
    def fn(rv, pv):
        xs = [v.astype(f32) for v in rv[:n_x]]
        cots = [v.astype(f32) for v in rv[n_x:n_x + n_cot]]
        adds = rv[n_x + n_cot:n_x + n_cot + n_add]
        ps = [v.astype(f32) for v in pv]
        _, vjp = jax.vjp(lambda *a: f(list(a[:n_x]), list(a[n_x:])), *xs, *ps)
        g = vjp(cots)
        row_g = [g[i] for i in grad_idx]
        for k, a in enumerate(adds):
            row_g[k] = row_g[k] + a.astype(f32)
        return row_g, list(g[n_x:])
    return fn


def _rms(x, g):
    return x * lax.rsqrt(jnp.mean(x * x, axis=-1, keepdims=True) + RMS_EPS) * g


def _f_norm(xs, ps):
    return [_rms(xs[0], ps[0])]


def _f_post(scale):
    def f(xs, ps):
        return [scale * _rms(xs[0], ps[0])]
    return f


def _f_resnorm(scale):
    def f(xs, ps):
        return [xs[0] + scale * _rms(xs[1], ps[0])]
    return f


def _f_dt(xs, ps):
    dt = jax.nn.softplus(xs[0] + ps[0])
    return [dt, -jnp.exp(ps[1]) * dt]


def _f_ssdpost(n_groups):
    def f(xs, ps):
        y = xs[0] * jax.nn.silu(xs[1])
        width = y.shape[-1] // n_groups
        lane = lax.broadcasted_iota(jnp.int32, y.shape, 1)
        scale = jnp.zeros_like(y)
        for k in range(n_groups):
            m = ((lane >= k * width) & (lane < (k + 1) * width)).astype(f32)
            ms = jnp.sum(y * y * m, axis=-1, keepdims=True) / width
            scale = scale + lax.rsqrt(ms + RMS_EPS) * m
        return [y * scale * ps[0]]
    return f


def _f_s5post(xs, ps):
    return [jax.nn.gelu(xs[0] + ps[0] * xs[1])]


def _f_merge(xs, ps):
    return [jax.nn.sigmoid(xs[0]) * xs[1] + jax.nn.sigmoid(xs[2]) * xs[3]]


def _swiglu_fwd(rv, pv):
    ab = rv[0].astype(f32)
    h = ab.shape[1] // 2
    return [jax.nn.silu(ab[:, :h]) * ab[:, h:]], []


def _swiglu_bwd(rv, pv):
    ab, d = rv[0].astype(f32), rv[1].astype(f32)
    h = ab.shape[1] // 2
    a, b = ab[:, :h], ab[:, h:]
    s = jax.nn.sigmoid(a)
    return [jnp.concatenate([d * b * (s * (1.0 + a * (1.0 - s))), d * (a * s)], axis=1)], []


def _glu_fwd(rv, pv):
    vg = rv[0].astype(f32)
    h = vg.shape[1] // 2
    return [vg[:, :h] * jax.nn.sigmoid(vg[:, h:])], []


def _glu_bwd(rv, pv):
    vg, d = rv[0].astype(f32), rv[1].astype(f32)
    h = vg.shape[1] // 2
    s = jax.nn.sigmoid(vg[:, h:])
    return [jnp.concatenate([d * s, d * vg[:, :h] * s * (1.0 - s)], axis=1)], []


def _loss_fn(rv, pv):
    e = rv[0].astype(f32) - rv[1].astype(f32)
    per_tok = jnp.mean(e * e, axis=-1, keepdims=True)
    part = 0.5 * jnp.sum(per_tok, axis=0, keepdims=True)
    return [e / e.shape[-1]], [jnp.broadcast_to(part, (8, LANES))]


def _add_fn(rv, pv):
    return [rv[0].astype(f32) + rv[1].astype(f32)], []


def _adamw_fn(rv, pv):
    w, g, m, v = [x.astype(f32) for x in rv]
    m = ADAM_B1 * m + (1.0 - ADAM_B1) * g
    v = ADAM_B2 * v + (1.0 - ADAM_B2) * (g * g)
    m_hat = m / (1.0 - ADAM_B1 ** ADAM_STEP)
    v_hat = v / (1.0 - ADAM_B2 ** ADAM_STEP)
    return [-ADAM_LR * (m_hat / (jnp.sqrt(v_hat) + ADAM_EPS) + ADAM_WD * w), m, v], []


def _adamw_layers(w, gs, m, v, name):
    n_layers, r, cols = w.shape
    br, bc = _pick(r, 256, 8), cols
    if br < 64 and cols % LANES == 0:
        br, bc = r, LANES

    def body(w_ref, *refs):
        g_refs, (m_ref, v_ref, g_out, d_ref, nm_ref, nv_ref) = refs[:n_layers], refs[n_layers:]
        layer = pl.program_id(0)
        g = g_refs[0][...]
        for k in range(1, n_layers):
            g = jnp.where(layer == k, g_refs[k][...], g)
        outs, _ = _adamw_fn([w_ref[0], g, m_ref[0], v_ref[0]], [])
        g_out[0] = g
        d_ref[0], nm_ref[0], nv_ref[0] = outs

    stacked = pl.BlockSpec((1, br, bc), lambda l, i, j: (l, i, j))
    single = pl.BlockSpec((br, bc), lambda l, i, j: (i, j))
    return pl.pallas_call(
        body, name=name, grid=(n_layers, r // br, cols // bc), in_specs=[stacked] + [single] * n_layers + [stacked] * 2,
        out_specs=[stacked] * 4, out_shape=[jax.ShapeDtypeStruct(w.shape, f32)] * 4,
        compiler_params=_params(("parallel", "parallel", "parallel")),
    )(w, *gs, m, v)


def _adamw(w, g, m, v, name):
    shape = w.shape
    cols = shape[-1] if (w.ndim >= 2 and shape[-1] >= LANES) else None
    if cols is None:
        n = int(np.prod(shape))
        cols = LANES if n % LANES == 0 else n
    n_rows = int(np.prod(shape)) // cols
    br, bc = _pick(n_rows, 256, 8), cols
    if br < 64 and cols % LANES == 0:
        br, bc = n_rows, LANES

    def body(w_ref, g_ref, m_ref, v_ref, d_ref, nm_ref, nv_ref):
        outs, _ = _adamw_fn([w_ref[...], g_ref[...], m_ref[...], v_ref[...]], [])
        d_ref[...], nm_ref[...], nv_ref[...] = outs

    spec = pl.BlockSpec((br, bc), lambda i, j: (i, j))
    outs = pl.pallas_call(
        body, name=name, grid=(n_rows // br, cols // bc), in_specs=[spec] * 4, out_specs=[spec] * 3,
        out_shape=[jax.ShapeDtypeStruct((n_rows, cols), f32)] * 3, compiler_params=_params(("parallel", "parallel")),
    )(*[t.reshape(n_rows, cols) for t in (w, g, m, v)])
    return [o.reshape(shape) for o in outs]


def _shift_down(x, s, row):
    if s == 0:
        return x
    return jnp.where(row >= s, pltpu.roll(x, s, 0), 0.0)


def _shift_up(x, s, row):
    if s == 0:
        return x
    n = x.shape[0]
    return jnp.where(row < n - s, pltpu.roll(x, n - s, 0), 0.0)


def _conv_pre(x, w, b, row):
    kw = w.shape[0]
    c = b
    for k in range(kw):
        c = c + w[k:k + 1, :] * _shift_down(x, kw - 1 - k, row)
    return c


def _conv_fwd(xsrc, col0, w, b, name, bc_t=512):
    T = xsrc.shape[0]
    kw, ncols = w.shape
    bc = _pick(ncols, bc_t)
    off = col0 // bc
    assert col0 % bc == 0

    def body(x_ref, w_ref, b_ref, o_ref):
        x = x_ref[...].astype(f32)
        row = lax.broadcasted_iota(jnp.int32, x.shape, 0)
        c = _conv_pre(x, w_ref[...], b_ref[...], row)
        o_ref[...] = c * jax.nn.sigmoid(c)

    return pl.pallas_call(
        body, name=name, grid=(ncols // bc,),
        in_specs=[pl.BlockSpec((T, bc), lambda j: (0, off + j)), pl.BlockSpec((kw, bc), lambda j: (0, j)),
                  pl.BlockSpec((1, bc), lambda j: (0, j))],
        out_specs=pl.BlockSpec((T, bc), lambda j: (0, j)), out_shape=jax.ShapeDtypeStruct((T, ncols), f32),
        compiler_params=_params(("parallel",)),
    )(xsrc, w, b)


def _conv_bwd(xsrc, col0, w, b, dact, name, bc_t=512):
    T = xsrc.shape[0]
    kw, ncols = w.shape
    bc = _pick(ncols, bc_t)
    off = col0 // bc
    assert col0 % bc == 0

    def body(x_ref, w_ref, b_ref, d_ref, dx_ref, dw_ref, db_ref):
        x = x_ref[...].astype(f32)
        w = w_ref[...]
        row = lax.broadcasted_iota(jnp.int32, x.shape, 0)
        c = _conv_pre(x, w, b_ref[...], row)
        s = jax.nn.sigmoid(c)
        dc = d_ref[...].astype(f32) * (s * (1.0 + c * (1.0 - s)))
        dx = jnp.zeros_like(x)
        dws = []
        for k in range(kw):
            dx = dx + w[k:k + 1, :] * _shift_up(dc, kw - 1 - k, row)
            dws.append(jnp.sum(dc * _shift_down(x, kw - 1 - k, row), axis=0, keepdims=True))
        dx_ref[...] = dx.astype(dx_ref.dtype)
        dw_ref[...] = jnp.concatenate(dws, axis=0)
        db_ref[...] = jnp.sum(dc, axis=0, keepdims=True)

    return pl.pallas_call(
        body, name=name, grid=(ncols // bc,),
        in_specs=[pl.BlockSpec((T, bc), lambda j: (0, off + j)), pl.BlockSpec((kw, bc), lambda j: (0, j)),
                  pl.BlockSpec((1, bc), lambda j: (0, j)), pl.BlockSpec((T, bc), lambda j: (0, j))],
        out_specs=[pl.BlockSpec((T, bc), lambda j: (0, j)), pl.BlockSpec((kw, bc), lambda j: (0, j)),
                   pl.BlockSpec((1, bc), lambda j: (0, j))],
        out_shape=[jax.ShapeDtypeStruct((T, ncols), bf16), jax.ShapeDtypeStruct((kw, ncols), f32),
                   jax.ShapeDtypeStruct((1, ncols), f32)],
        compiler_params=_params(("parallel",)),
    )(xsrc, w, b, dact)


_HI = lax.Precision.HIGHEST


def _dot(a, b, dims="nn", precision=None):
    return lax.dot_general(a, b, _DIMS[dims], preferred_element_type=f32, precision=precision)


def _dot01(a, b, dims="nn", ones="b"):
    x = a if ones == "b" else b
    hi = x.astype(bf16)
    rest = x - hi.astype(f32)
    mid = rest.astype(bf16)
    lo = (rest - mid.astype(f32)).astype(bf16)
    if ones == "b":
        e = b.astype(bf16)
        return _dot(hi, e, dims) + _dot(mid, e, dims) + _dot(lo, e, dims)
    e = a.astype(bf16)
    return _dot(e, hi, dims) + _dot(e, mid, dims) + _dot(e, lo, dims)


def _ssd_common(x_ref, b_ref, c_ref, dt_ref, adt_ref, d_ref, hpg, p):
    q = b_ref.shape[0]
    hp = hpg * p
    bb, cb = b_ref[...].astype(bf16), c_ref[...].astype(bf16)
    r = lax.broadcasted_iota(jnp.int32, (q, q), 0)
    s = lax.broadcasted_iota(jnp.int32, (q, q), 1)
    tril = r >= s
    trilf = tril.astype(f32)
    eh = lax.broadcasted_iota(jnp.int32, (LANES, hp), 0)
    ec = lax.broadcasted_iota(jnp.int32, (LANES, hp), 1)
    expand = ((ec >= eh * p) & (ec < (eh + 1) * p)).astype(f32)
    adt = adt_ref[...]
    cum = _dot01(trilf, adt, "nn", "a")
    cum_t = _dot01(adt, (r <= s).astype(f32), "tn")
    cum_e = _dot01(cum, expand)
    dt_e = _dot01(dt_ref[...], expand)
    d_e = _dot01(jnp.broadcast_to(d_ref[...], (8, LANES)), expand)[0:1, :]
    gmat = _dot(cb, bb, "nt")
    x = x_ref[...]
    xdt = x * dt_e
    e_all = jnp.exp(cum_e)
    dec = jnp.exp(cum_e[q - 1:q, :] - cum_e)
    lms, ms = [], []
    for h in range(hpg):
        lm = jnp.exp(jnp.where(tril, cum[:, h:h + 1] - cum_t[h:h + 1, :], -1e30))
        lms.append(lm)
        ms.append(gmat * lm)
    et = [jnp.exp(cum[q - 1:q, h:h + 1]) for h in range(hpg)]
    return dict(bb=bb, cb=cb, trilf=trilf, expand=expand, cum=cum, x=x, xdt=xdt, dt_e=dt_e, d_e=d_e, e=e_all, dec=dec,
                lms=lms, ms=ms, et=et)


def _ssd_specs(q, hp, n, g_n, nc, rev):
    def cidx(c):
        return (nc - 1 - c) if rev else c
    x_spec = pl.BlockSpec((q, hp), lambda g, c: (cidx(c), g))
    boff = (g_n * hp) // n
    b_spec = pl.BlockSpec((q, n), lambda g, c: (cidx(c), boff + g))
    c_spec = pl.BlockSpec((q, n), lambda g, c: (cidx(c), boff + g_n + g))
    dt_spec = pl.BlockSpec((q, LANES), lambda g, c: (cidx(c), g))
    d_spec = pl.BlockSpec((1, LANES), lambda g, c: (0, g))
    st_spec = pl.BlockSpec((1, 1, hp, n), lambda g, c: (cidx(c), g, 0, 0))
    return x_spec, b_spec, c_spec, dt_spec, d_spec, st_spec


def _ssd_fwd(act, dt, adt, dpad, hpg, p, n, name):
    T = act.shape[0]
    g_n, q = SSD_N_GROUPS, SSD_CHUNK
    nc, hp = T // q, hpg * p
    x_spec, b_spec, c_spec, dt_spec, d_spec, st_spec = _ssd_specs(q, hp, n, g_n, nc, False)

    def body(x_ref, b_ref, c_ref, dt_ref, adt_ref, d_ref, y_ref, st_ref, s_scr):
        @pl.when(pl.program_id(1) == 0)
        def _():
            s_scr[...] = jnp.zeros_like(s_scr)

        k = _ssd_common(x_ref, b_ref, c_ref, dt_ref, adt_ref, d_ref, hpg, p)
        s0 = s_scr[...]
        st_ref[0, 0] = s0
        xdtb = k["xdt"].astype(bf16)
        ydiag = [_dot(k["ms"][h].astype(bf16), xdtb[:, h * p:(h + 1) * p]) for h in range(hpg)]
        z = _dot(k["cb"], s0.astype(bf16), "nt")
        y_ref[...] = jnp.concatenate(ydiag, axis=1) + k["e"] * z + k["d_e"] * k["x"]
        upd = _dot((k["xdt"] * k["dec"]).astype(bf16), k["bb"], "tn")
        for h in range(hpg):
            s_scr[h * p:(h + 1) * p, :] = k["et"][h] * s0[h * p:(h + 1) * p, :] + upd[h * p:(h + 1) * p, :]

    return pl.pallas_call(
        body, name=name, grid=(g_n, nc),
        in_specs=[x_spec, b_spec, c_spec, dt_spec, dt_spec, d_spec],
        out_specs=[pl.BlockSpec((q, hp), lambda g, c: (c, g)), st_spec],
        out_shape=[jax.ShapeDtypeStruct((T, g_n * hp), f32), jax.ShapeDtypeStruct((nc, g_n, hp, n), f32)],
        scratch_shapes=[pltpu.VMEM((hp, n), f32)],
        compiler_params=_params(("parallel", "arbitrary")),
    )(act, act, act, dt, adt, dpad)


def _ssd_bwd(act, dt, adt, dpad, states, dy, hpg, p, n, name):
    T = act.shape[0]
    g_n, q = SSD_N_GROUPS, SSD_CHUNK
    nc, hp = T // q, hpg * p
    x_spec, b_spec, c_spec, dt_spec, d_spec, st_spec = _ssd_specs(q, hp, n, g_n, nc, True)

    def body(x_ref, b_ref, c_ref, dt_ref, adt_ref, d_ref, st_ref, dy_ref,
             dx_ref, db_ref, dc_ref, ddt_ref, dadt_ref, dd_ref, ds_scr):
        first = pl.program_id(1) == 0

        @pl.when(first)
        def _():
            ds_scr[...] = jnp.zeros_like(ds_scr)

        k = _ssd_common(x_ref, b_ref, c_ref, dt_ref, adt_ref, d_ref, hpg, p)
        bb, cb, expand, x, xdt, dec = k["bb"], k["cb"], k["expand"], k["x"], k["xdt"], k["dec"]
        heads = lambda t: _dot01(t, expand, "nt")
        s0 = st_ref[0, 0]
        ds1 = ds_scr[...]
        s0b, ds1b = s0.astype(bf16), ds1.astype(bf16)
        dy = dy_ref[...]
        dyb, xdtb = dy.astype(bf16), xdt.astype(bf16)
        lane = lax.broadcasted_iota(jnp.int32, (1, LANES), 1)
        dg = jnp.zeros((q, q), f32)
        w_rows = jnp.zeros((q, LANES), f32)
        w_cols, dxdt_parts = [], []
        for h in range(hpg):
            hs = slice(h * p, (h + 1) * p)
            dm = _dot(dyb[:, hs], xdtb[:, hs], "nt")
            dg = dg + dm * k["lms"][h]
            wm = dm * k["ms"][h]
            w_rows = w_rows + jnp.sum(wm, axis=1, keepdims=True) * (lane == h).astype(f32)
            w_cols.append(jnp.sum(wm, axis=0, keepdims=True))
            dxdt_parts.append(_dot(k["ms"][h].astype(bf16), dyb[:, hs], "tn"))
        dxdt_diag = jnp.concatenate(dxdt_parts, axis=1)
        w_cols = jnp.concatenate(w_cols + [jnp.zeros((LANES - hpg, q), f32)], axis=0).T
        dgb = dg.astype(bf16)
        z = _dot(cb, s0b, "nt")
        dz = dy * k["e"]
        dzb = dz.astype(bf16)
        dxd = _dot(bb, ds1b, "nt")
        ddec = dxd * xdt * dec
        db_ref[...] = _dot(dgb, cb, "tn") + _dot((xdt * dec).astype(bf16), ds1b)
        dc_ref[...] = _dot(dgb, bb) + _dot(dzb, s0b)
        ds0 = _dot(dzb, cb, "tn")
        for h in range(hpg):
            hs = slice(h * p, (h + 1) * p)
            ds_scr[hs, :] = ds0[hs, :] + k["et"][h] * ds1[hs, :]
        dxdt = dxdt_diag + dxd * dec
        ddec_h = heads(ddec)
        dcum = w_rows - w_cols + heads(dz * z) - ddec_h
        et_row = jnp.exp(k["cum"][q - 1:q, :])
        dsum = _dot01(jnp.ones((8, n), f32), _dot01(expand, ds1 * s0, "nn", "a"), "nt", "a")[0:1, :]
        dcl = dsum * et_row + jnp.sum(ddec_h, axis=0, keepdims=True)
        rowq = lax.broadcasted_iota(jnp.int32, (q, 1), 0)
        dcum = dcum + (rowq == q - 1).astype(f32) * dcl
        ddt_ref[...] = heads(dxdt * x)
        dadt_ref[...] = _dot01(k["trilf"], dcum, "tn", "a")
        dx_ref[...] = k["d_e"] * dy + dxdt * k["dt_e"]
        dd8 = heads(jnp.broadcast_to(jnp.sum(dy * x, axis=0, keepdims=True), (8, hp)))

        @pl.when(first)
        def _():
            dd_ref[...] = dd8

        @pl.when(jnp.logical_not(first))
        def _():
            dd_ref[...] += dd8

    rc = lambda g, c: (nc - 1 - c, g)
    return pl.pallas_call(
        body, name=name, grid=(g_n, nc),
        in_specs=[x_spec, b_spec, c_spec, dt_spec, dt_spec, d_spec, st_spec, pl.BlockSpec((q, hp), rc)],
        out_specs=[pl.BlockSpec((q, hp), rc), pl.BlockSpec((q, n), rc), pl.BlockSpec((q, n), rc),
                   pl.BlockSpec((q, LANES), rc), pl.BlockSpec((q, LANES), rc), pl.BlockSpec((8, LANES), lambda g, c: (g, 0))],
        out_shape=[jax.ShapeDtypeStruct((T, g_n * hp), f32), jax.ShapeDtypeStruct((T, g_n * n), f32),
                   jax.ShapeDtypeStruct((T, g_n * n), f32), jax.ShapeDtypeStruct((T, g_n * LANES), f32),
                   jax.ShapeDtypeStruct((T, g_n * LANES), f32), jax.ShapeDtypeStruct((g_n * 8, LANES), f32)],
        scratch_shapes=[pltpu.VMEM((hp, n), f32)],
        compiler_params=_params(("parallel", "arbitrary")),
    )(act, act, act, dt, adt, dpad, states, dy)


def _cmul(ar, ai, br, bi):
    return ar * br - ai * bi, ar * bi + ai * br


def _s5_tile_powers(lr, li):
    p = [(lr, li)]
    for _ in range(7):
        p.append(_cmul(p[-1][0], p[-1][1], lr, li))
    tile = (jnp.concatenate([q[0] for q in p], axis=0), jnp.concatenate([q[1] for q in p], axis=0))
    return tile, (p[0], p[1], p[3])


def _s5_tile_scan(xr, xi, steps, reverse):
    row = lax.broadcasted_iota(jnp.int32, xr.shape, 0)
    for d, (pr, pi) in zip((1, 2, 4), steps):
        if reverse:
            keep = row < 8 - d
            sr, si = pltpu.roll(xr, 8 - d, 0), pltpu.roll(xi, 8 - d, 0)
        else:
            keep = row >= d
            sr, si = pltpu.roll(xr, d, 0), pltpu.roll(xi, d, 0)
        sr, si = jnp.where(keep, sr, 0.0), jnp.where(keep, si, 0.0)
        ar, ai = _cmul(sr, si, pr, pi)
        xr, xi = xr + ar, xi + ai
    return xr, xi


def _s5_scan_fwd(bu, lam_re, lam_im, nsb, name, tc_t=512):
    T = bu.shape[0]
    w2 = bu.shape[1] // nsb
    w = w2 // 2
    tc = _pick(T, tc_t, 8)

    def body(bu_ref, lr_ref, li_ref, st_ref, carry):
        @pl.when(pl.program_id(1) == 0)
        def _():
            carry[...] = jnp.zeros_like(carry)

        (pr8, pi8), steps = _s5_tile_powers(lr_ref[0:1, :], li_ref[0:1, :])

        def tile(i, c):
            r = pl.ds(pl.multiple_of(i * 8, 8), 8)
            x = bu_ref[r, :]
            xr, xi = _s5_tile_scan(x[:, :w], x[:, w:], steps, False)
            ar, ai = _cmul(pr8, pi8, c[0], c[1])
            xr, xi = xr + ar, xi + ai
            st_ref[r, :] = jnp.concatenate([xr, xi], axis=1)
            return xr[7:8, :], xi[7:8, :]

        c = lax.fori_loop(0, tc // 8, tile, (carry[0:1, :], carry[1:2, :]), unroll=2)
        carry[0:1, :] = c[0]
        carry[1:2, :] = c[1]

    return pl.pallas_call(
        body, name=name, grid=(nsb, T // tc),
        in_specs=[pl.BlockSpec((tc, w2), lambda j, i: (i, j)), pl.BlockSpec((8, w), lambda j, i: (j, 0)),
                  pl.BlockSpec((8, w), lambda j, i: (j, 0))],
        out_specs=pl.BlockSpec((tc, w2), lambda j, i: (i, j)), out_shape=jax.ShapeDtypeStruct(bu.shape, f32),
        scratch_shapes=[pltpu.VMEM((8, w), f32)],
        compiler_params=_params(("parallel", "arbitrary")),
    )(bu, lam_re, lam_im)


def _s5_scan_bwd(gst, states, lam_re, lam_im, nsb, name, tc_t=512):
    T = gst.shape[0]
    w2 = gst.shape[1] // nsb
    w = w2 // 2
    tc = _pick(T, tc_t, 8)
    nt = T // tc
    n_tiles = tc // 8

    def body(g_ref, s_ref, sp_ref, lr_ref, li_ref, a_ref, dlr_ref, dli_ref, carry, acc):
        chunk = pl.program_id(1)

        @pl.when(chunk == 0)
        def _():
            carry[...] = jnp.zeros_like(carry)
            acc[...] = jnp.zeros_like(acc)

        (qr8, qi8), steps = _s5_tile_powers(lr_ref[0:1, :], -li_ref[0:1, :])
        row = lax.broadcasted_iota(jnp.int32, (8, w), 0)
        rev_r, rev_i = jnp.zeros((8, w), f32), jnp.zeros((8, w), f32)
        for r in range(8):
            rev_r = jnp.where(row == r, qr8[7 - r:8 - r, :], rev_r)
            rev_i = jnp.where(row == r, qi8[7 - r:8 - r, :], rev_i)
        row2 = lax.broadcasted_iota(jnp.int32, (8, w2), 0)

        def tile(k, c):
            ar_in, ai_in, dr, di = c
            i = n_tiles - 1 - k
            r = pl.ds(pl.multiple_of(i * 8, 8), 8)
            x = g_ref[r, :]
            xr, xi = _s5_tile_scan(x[:, :w], x[:, w:], steps, True)
            pr, pi = _cmul(rev_r, rev_i, ar_in, ai_in)
            xr, xi = xr + pr, xi + pi
            a_ref[r, :] = jnp.concatenate([xr, xi], axis=1)
            before = jnp.where(i > 0, s_ref[pl.ds(pl.multiple_of(jnp.maximum(i - 1, 0) * 8, 8), 8), :],
                               sp_ref[tc - 8:tc, :] * (chunk < nt - 1).astype(f32))
            prev = jnp.where(row2 == 0, pltpu.roll(before, 1, 0), pltpu.roll(s_ref[r, :], 1, 0))
            spr, spi = prev[:, :w], prev[:, w:]
            return xr[0:1, :], xi[0:1, :], dr + xr * spr + xi * spi, di - xr * spi + xi * spr

        c0 = (carry[0:1, :], carry[1:2, :], acc[0:8, :], acc[8:16, :])
        ar, ai, dr, di = lax.fori_loop(0, n_tiles, tile, c0, unroll=2)
        carry[0:1, :] = ar
        carry[1:2, :] = ai
        acc[0:8, :] = dr
        acc[8:16, :] = di
        dlr_ref[...] = jnp.broadcast_to(jnp.sum(dr, axis=0, keepdims=True), (8, w))
        dli_ref[...] = jnp.broadcast_to(jnp.sum(di, axis=0, keepdims=True), (8, w))

    cur = lambda j, i: (nt - 1 - i, j)
    prv = lambda j, i: (jnp.maximum(nt - 2 - i, 0), j)
    return pl.pallas_call(
        body, name=name, grid=(nsb, nt),
        in_specs=[pl.BlockSpec((tc, w2), cur), pl.BlockSpec((tc, w2), cur), pl.BlockSpec((tc, w2), prv),
                  pl.BlockSpec((8, w), lambda j, i: (j, 0)), pl.BlockSpec((8, w), lambda j, i: (j, 0))],
        out_specs=[pl.BlockSpec((tc, w2), cur), pl.BlockSpec((8, w), lambda j, i: (j, 0)),
                   pl.BlockSpec((8, w), lambda j, i: (j, 0))],
        out_shape=[jax.ShapeDtypeStruct(gst.shape, f32), jax.ShapeDtypeStruct((nsb * 8, w), f32),
                   jax.ShapeDtypeStruct((nsb * 8, w), f32)],
        scratch_shapes=[pltpu.VMEM((8, w), f32), pltpu.VMEM((16, w), f32)],
        compiler_params=_params(("parallel", "arbitrary")),
    )(gst, states, states, lam_re, lam_im)


def _s5_prep_fn(xs, ps):
    lam_re, lam_im, log_step, b_re, b_im, expand = ps
    lr = jnp.minimum(lam_re, S5_MAX_REAL)
    li = lam_im
    step = jnp.exp(log_step)
    er = jnp.exp(lr * step)
    ang = li * step
    lbr, lbi = er * jnp.cos(ang), er * jnp.sin(ang)
    nr, ni = lbr - 1.0, lbi
    den = lr * lr + li * li
    qr, qi = (nr * lr + ni * li) / den, (ni * lr - nr * li) / den
    qre, qie = _dot(qr, expand, "nn", _HI), _dot(qi, expand, "nn", _HI)
    return [lbr, lbi, qre * b_re - qie * b_im, qre * b_im + qie * b_re]


def _s5_prep(pars, name):
    def body(*refs):
        outs = _s5_prep_fn([], [r[...] for r in refs[:6]])
        for ref, v in zip(refs[6:], outs):
            ref[...] = v

    g, nst = pars[0].shape
    nc = pars[3].shape[1]
    return pl.pallas_call(
        body, name=name,
        out_shape=[jax.ShapeDtypeStruct((g, nst), f32)] * 2 + [jax.ShapeDtypeStruct((g, nc), f32)] * 2,
        compiler_params=_params(),
    )(*pars)


def _s5_prep_bwd(pars, cots, name):
    def body(*refs):
        ps = [r[...] for r in refs[:6]]
        ct = [r[...] for r in refs[6:10]]
        _, vjp = jax.vjp(lambda *a: _s5_prep_fn([], list(a)), *ps)
        g = vjp(ct)
        for ref, v in zip(refs[10:], g[:5]):
            ref[...] = v

    return pl.pallas_call(
        body, name=name, out_shape=[jax.ShapeDtypeStruct(p.shape, f32) for p in pars[:5]], compiler_params=_params(),
    )(*pars, *cots)


_ANY = pl.BlockSpec(memory_space=pl.ANY)


def _remote(src, dst, send_sem, recv_sem, device):
    return pltpu.make_async_remote_copy(src_ref=src, dst_ref=dst, send_sem=send_sem, recv_sem=recv_sem, device_id=device,
                                        device_id_type=MESH)


def _staged_copy(src, dst, buf, in_sems, out_sems):
    n = D2D_STREAMS
    piece = src.shape[0] // n
    assert src.shape[0] % n == 0

    def load(i):
        return pltpu.make_async_copy(src.at[pl.ds(i * piece, piece)], buf.at[i % 2], in_sems.at[i % 2])

    def store(i):
        return pltpu.make_async_copy(buf.at[i % 2], dst.at[pl.ds(i * piece, piece)], out_sems.at[i % 2])

    load(0).start()
    for i in range(n):
        if i + 1 < n:
            if i >= 1:
                store(i - 1).wait()
            load(i + 1).start()
        load(i).wait()
        store(i).start()
    store(n - 2).wait()
    store(n - 1).wait()


def _stage_scratch(rows, cols, dtype):
    return [pltpu.VMEM((2, rows // D2D_STREAMS, cols), dtype), pltpu.SemaphoreType.DMA((2,)), pltpu.SemaphoreType.DMA((2,))]


def _chip_all_gather(block, name):
    rows = block.shape[0]
    half = rows // 2
    piece = half // D2D_STREAMS
    assert rows % (2 * D2D_STREAMS * 16) == 0

    def body(src, out, ici_send, ici_recv, d2d_send, d2d_recv, *stage):
        x, y, c = lax.axis_index("x"), lax.axis_index("y"), lax.axis_index("c")
        me = 2 * x + y
        sibling = (x, y, 1 - c)
        chips = [(1 - x, y), (x, 1 - y), (1 - x, 1 - y)]
        mine = pl.ds(pl.multiple_of(c * half, 16), half)
        sends = []
        for j, (px, py) in enumerate(chips):
            cp = _remote(src.at[mine], out.at[me, mine], ici_send.at[j], ici_recv.at[j], (px, py, c))
            cp.start()
            sends.append(cp)
        _staged_copy(src, out.at[me], *stage)
        for j, (px, py) in enumerate(chips):
            slot = 2 * px + py
            _remote(src.at[mine], out.at[slot, mine], ici_send.at[j], ici_recv.at[j], (px, py, c)).wait_recv()
            for s in range(D2D_STREAMS):
                r = pl.ds(pl.multiple_of(c * half + s * piece, 16), piece)
                k = j * D2D_STREAMS + s
                cp = _remote(out.at[slot, r], out.at[slot, r], d2d_send.at[k], d2d_recv.at[k], sibling)
                cp.start()
                sends.append(cp)
        for j, (px, py) in enumerate(chips):
            slot = 2 * px + py
            for s in range(D2D_STREAMS):
                r = pl.ds(pl.multiple_of((1 - c) * half + s * piece, 16), piece)
                k = j * D2D_STREAMS + s
                _remote(out.at[slot, r], out.at[slot, r], d2d_send.at[k], d2d_recv.at[k], sibling).wait_recv()
        for cp in sends:
            cp.wait_send()

    n_d2d = 3 * D2D_STREAMS
    return pl.pallas_call(
        body, name=name, in_specs=[_ANY], out_specs=_ANY,
        out_shape=jax.ShapeDtypeStruct((N_CHIPS,) + block.shape, block.dtype),
        scratch_shapes=[pltpu.SemaphoreType.DMA((3,)), pltpu.SemaphoreType.DMA((3,)), pltpu.SemaphoreType.DMA((n_d2d,)),
                        pltpu.SemaphoreType.DMA((n_d2d,))] + _stage_scratch(rows, block.shape[1], block.dtype),
    )(block)


def _chip_scatter(parts, name):
    def body(src, out, send_sems, recv_sems, *stage):
        x, y, c = lax.axis_index("x"), lax.axis_index("y"), lax.axis_index("c")
        me = 2 * x + y
        chips = [(1 - x, y), (x, 1 - y), (1 - x, 1 - y)]
        sends = []
        for j, (px, py) in enumerate(chips):
            cp = pltpu.make_async_remote_copy(src_ref=src.at[2 * px + py], dst_ref=out.at[me], send_sem=send_sems.at[j],
                                              recv_sem=recv_sems.at[j], device_id=(px, py, c), device_id_type=MESH)
            cp.start()
            sends.append(cp)
        _staged_copy(src.at[me], out.at[me], *stage)
        for j, (px, py) in enumerate(chips):
            pltpu.make_async_remote_copy(src_ref=src.at[me], dst_ref=out.at[2 * px + py], send_sem=send_sems.at[j],
                                         recv_sem=recv_sems.at[j], device_id=(px, py, c), device_id_type=MESH).wait_recv()
        for cp in sends:
            cp.wait_send()

    return pl.pallas_call(
        body, name=name, in_specs=[_ANY], out_specs=_ANY, out_shape=jax.ShapeDtypeStruct(parts.shape, parts.dtype),
        scratch_shapes=[pltpu.SemaphoreType.DMA((3,)), pltpu.SemaphoreType.DMA((3,))]
        + _stage_scratch(parts.shape[1], parts.shape[2], parts.dtype),
    )(parts)


_HBM = pl.BlockSpec(memory_space=pltpu.HBM)
_SEM = pl.BlockSpec(memory_space=pltpu.SEMAPHORE)
_EFFECT = pltpu.SideEffectType.DATAFLOW_SIDE_EFFECTING


def _gather_peers():
    x, y, c = lax.axis_index("x"), lax.axis_index("y"), lax.axis_index("c")
    return x, y, c, 2 * x + y, [(1 - x, y), (x, 1 - y), (1 - x, 1 - y)]


def _chip_gather_start(block, name):
    half = block.shape[0] // 2

    def body(src, land, send_sems, recv_sems, src_out, land_out, token):
        x, y, c, me, chips = _gather_peers()
        mine = pl.ds(pl.multiple_of(c * half, 16), half)
        for j, (px, py) in enumerate(chips):
            _remote(src.at[mine], land.at[me, mine], send_sems.at[j], recv_sems.at[j], (px, py, c)).start()
        token[...] = jnp.zeros_like(token)

    land_shape = (N_CHIPS,) + block.shape
    return pl.pallas_call(
        body, name=name,
        out_shape=(pltpu.SemaphoreType.DMA((3,)), pltpu.SemaphoreType.DMA((3,)), pltpu.HBM(block.shape, block.dtype),
                   pltpu.HBM(land_shape, block.dtype), jax.ShapeDtypeStruct((8, LANES), f32)),
        in_specs=(_HBM, _HBM), out_specs=(_SEM, _SEM, _HBM, _HBM, pl.BlockSpec(memory_space=pltpu.VMEM)),
        input_output_aliases={0: 2, 1: 3}, compiler_params=pltpu.CompilerParams(has_side_effects=_EFFECT),
    )(pltpu.with_memory_space_constraint(block, pltpu.HBM),
      pltpu.with_memory_space_constraint(lax.empty(land_shape, block.dtype), pltpu.HBM))


def _chip_gather_wait(send_sems, recv_sems, block, land, after, name):
    half = block.shape[0] // 2

    def body(src, land_ref, send_ref, recv_ref, after_ref, src_dead, land_out):
        x, y, c, me, chips = _gather_peers()
        mine = pl.ds(pl.multiple_of(c * half, 16), half)
        for j, (px, py) in enumerate(chips):
            cp = _remote(src.at[mine], land_ref.at[2 * px + py, mine], send_ref.at[j], recv_ref.at[j], (px, py, c))
            cp.wait_send()
            cp.wait_recv()

    return pl.pallas_call(
        body, name=name, out_shape=(pltpu.HBM(block.shape, block.dtype), pltpu.HBM(land.shape, land.dtype)),
        in_specs=(_HBM, _HBM, _SEM, _SEM, _ANY), out_specs=(_HBM, _HBM), input_output_aliases={0: 0, 1: 1},
        compiler_params=pltpu.CompilerParams(has_side_effects=_EFFECT),
    )(block, land, send_sems, recv_sems, after)


def _chip_gather_finish(block, land, name):
    rows = block.shape[0]
    half = rows // 2
    piece = half // D2D_STREAMS

    def body(src, land_ref, out, d2d_send, d2d_recv, *stage):
        x, y, c, me, chips = _gather_peers()
        sibling = (x, y, 1 - c)
        sends = []
        for j, (px, py) in enumerate(chips):
            slot = 2 * px + py
            for s in range(D2D_STREAMS):
                r = pl.ds(pl.multiple_of(c * half + s * piece, 16), piece)
                k = j * D2D_STREAMS + s
                cp = _remote(land_ref.at[slot, r], out.at[slot, r], d2d_send.at[k], d2d_recv.at[k], sibling)
                cp.start()
                sends.append(cp)
        _staged_copy(src, out.at[me], *stage)
        for j, (px, py) in enumerate(chips):
            slot = 2 * px + py
            for s in range(D2D_STREAMS):
                r = pl.ds(pl.multiple_of((1 - c) * half + s * piece, 16), piece)
                k = j * D2D_STREAMS + s
                _remote(land_ref.at[slot, r], out.at[slot, r], d2d_send.at[k], d2d_recv.at[k], sibling).wait_recv()
        for cp in sends:
            cp.wait_send()

    n_d2d = 3 * D2D_STREAMS
    return pl.pallas_call(
        body, name=name, in_specs=[_ANY, _ANY], out_specs=_ANY, out_shape=jax.ShapeDtypeStruct(land.shape, land.dtype),
        input_output_aliases={1: 0},
        scratch_shapes=[pltpu.SemaphoreType.DMA((n_d2d,)), pltpu.SemaphoreType.DMA((n_d2d,))]
        + _stage_scratch(rows, block.shape[1], block.dtype),
    )(block, land)


def _chip_scatter_start(parts, name):
    def body(src, land, send_sems, recv_sems, src_out, land_out, token):
        x, y, c = lax.axis_index("x"), lax.axis_index("y"), lax.axis_index("c")
        me = 2 * x + y
        for j, (px, py) in enumerate([(1 - x, y), (x, 1 - y), (1 - x, 1 - y)]):
            _remote(src.at[2 * px + py], land.at[me], send_sems.at[j], recv_sems.at[j], (px, py, c)).start()
        token[...] = jnp.zeros_like(token)

    return pl.pallas_call(
        body, name=name,
        out_shape=(pltpu.SemaphoreType.DMA((3,)), pltpu.SemaphoreType.DMA((3,)), pltpu.HBM(parts.shape, parts.dtype),
                   pltpu.HBM(parts.shape, parts.dtype), jax.ShapeDtypeStruct((8, LANES), f32)),
        in_specs=(_HBM, _HBM), out_specs=(_SEM, _SEM, _HBM, _HBM, pl.BlockSpec(memory_space=pltpu.VMEM)),
        input_output_aliases={0: 2, 1: 3}, compiler_params=pltpu.CompilerParams(has_side_effects=_EFFECT),
    )(pltpu.with_memory_space_constraint(parts, pltpu.HBM),
      pltpu.with_memory_space_constraint(lax.empty(parts.shape, parts.dtype), pltpu.HBM))


def _chip_scatter_wait(send_sems, recv_sems, parts, land, after, name):
    def body(src, land_ref, send_ref, recv_ref, after_ref, src_dead, land_out):
        x, y, c = lax.axis_index("x"), lax.axis_index("y"), lax.axis_index("c")
        me = 2 * x + y
        for j, (px, py) in enumerate([(1 - x, y), (x, 1 - y), (1 - x, 1 - y)]):
            cp = _remote(src.at[2 * px + py], land_ref.at[2 * px + py], send_ref.at[j], recv_ref.at[j], (px, py, c))
            cp.wait_send()
            cp.wait_recv()

    return pl.pallas_call(
        body, name=name, out_shape=(pltpu.HBM(parts.shape, parts.dtype), pltpu.HBM(land.shape, land.dtype)),
        in_specs=(_HBM, _HBM, _SEM, _SEM, _ANY), out_specs=(_HBM, _HBM), input_output_aliases={0: 0, 1: 1},
        compiler_params=pltpu.CompilerParams(has_side_effects=_EFFECT),
    )(parts, land, send_sems, recv_sems, after)


def _sum_slots_own(landed, own, chip, name, block_rows=256):
    s_n, r_n, c_n = landed.shape
    br = _pick(r_n, block_rows, 8)

    def body(chip_ref, land_ref, own_ref, o_ref):
        acc = jnp.zeros((br, c_n), f32)
        for s in range(s_n):
            acc = acc + jnp.where(chip_ref[0] == s, own_ref[s], land_ref[s]).astype(f32)
        o_ref[...] = acc

    spec = pl.BlockSpec((s_n, br, c_n), lambda i, c: (0, i, 0))
    grid_spec = pltpu.PrefetchScalarGridSpec(num_scalar_prefetch=1, grid=(r_n // br,), in_specs=[spec, spec],
                                             out_specs=pl.BlockSpec((br, c_n), lambda i, c: (i, 0)))
    return pl.pallas_call(
        body, name=name, grid_spec=grid_spec, out_shape=jax.ShapeDtypeStruct((r_n, c_n), f32),
        compiler_params=_params(("parallel",)),
    )(chip, landed, own)


def _core_send_other_half(parts, name):
    n_slots, rows, cols = parts.shape
    half = rows // 2
    piece = half // D2D_STREAMS
    assert rows % (2 * D2D_STREAMS * 16) == 0

    def body(src, out, send_sems, recv_sems):
        x, y, c = lax.axis_index("x"), lax.axis_index("y"), lax.axis_index("c")
        sibling = (x, y, 1 - c)
        sends = []
        for k in range(n_slots):
            for s in range(D2D_STREAMS):
                theirs = pl.ds(pl.multiple_of((1 - c) * half + s * piece, 16), piece)
                i = k * D2D_STREAMS + s
                cp = _remote(src.at[k, theirs], out.at[k, pl.ds(s * piece, piece)], send_sems.at[i], recv_sems.at[i], sibling)
                cp.start()
                sends.append(cp)
        for cp in sends:
            cp.wait_recv()
        for cp in sends:
            cp.wait_send()

    n = n_slots * D2D_STREAMS
    return pl.pallas_call(
        body, name=name, in_specs=[_ANY], out_specs=_ANY, out_shape=jax.ShapeDtypeStruct((n_slots, half, cols), parts.dtype),
        scratch_shapes=[pltpu.SemaphoreType.DMA((n,)), pltpu.SemaphoreType.DMA((n,))],
    )(parts)


def _other_half_copies(src, land, send_sems, recv_sems):
    n_slots, rows, _ = src.shape
    half = rows // 2
    piece = half // D2D_STREAMS
    x, y, c = lax.axis_index("x"), lax.axis_index("y"), lax.axis_index("c")
    copies = []
    for k in range(n_slots):
        for s in range(D2D_STREAMS):
            theirs = pl.ds(pl.multiple_of((1 - c) * half + s * piece, 16), piece)
            i = k * D2D_STREAMS + s
            copies.append(_remote(src.at[k, theirs], land.at[k, pl.ds(s * piece, piece)], send_sems.at[i], recv_sems.at[i],
                                  (x, y, 1 - c)))
    return copies


def _core_send_other_half_start(parts, name):
    n_slots, rows, cols = parts.shape
    assert rows % (2 * D2D_STREAMS * 16) == 0
    n = n_slots * D2D_STREAMS
    land_shape = (n_slots, rows // 2, cols)

    def body(src, land, send_sems, recv_sems, src_out, land_out, token):
        for cp in _other_half_copies(src, land, send_sems, recv_sems):
            cp.start()
        token[...] = jnp.zeros_like(token)

    return pl.pallas_call(
        body, name=name,
        out_shape=(pltpu.SemaphoreType.DMA((n,)), pltpu.SemaphoreType.DMA((n,)), pltpu.HBM(parts.shape, parts.dtype),
                   pltpu.HBM(land_shape, parts.dtype), jax.ShapeDtypeStruct((8, LANES), f32)),
        in_specs=(_HBM, _HBM), out_specs=(_SEM, _SEM, _HBM, _HBM, pl.BlockSpec(memory_space=pltpu.VMEM)),
        input_output_aliases={0: 2, 1: 3}, compiler_params=pltpu.CompilerParams(has_side_effects=_EFFECT),
    )(pltpu.with_memory_space_constraint(parts, pltpu.HBM),
      pltpu.with_memory_space_constraint(lax.empty(land_shape, parts.dtype), pltpu.HBM))


def _core_send_other_half_wait(send_sems, recv_sems, parts, land, after, name):
    def body(src, land_ref, send_ref, recv_ref, after_ref, src_dead, land_out):
        for cp in _other_half_copies(src, land_ref, send_ref, recv_ref):
            cp.wait_send()
            cp.wait_recv()

    return pl.pallas_call(
        body, name=name, out_shape=(pltpu.HBM(parts.shape, parts.dtype), pltpu.HBM(land.shape, land.dtype)),
        in_specs=(_HBM, _HBM, _SEM, _SEM, _ANY), out_specs=(_HBM, _HBM), input_output_aliases={0: 0, 1: 1},
        compiler_params=pltpu.CompilerParams(has_side_effects=_EFFECT),
    )(parts, land, send_sems, recv_sems, after)


def _add_my_half(parts, other, core, name, block_rows=256):
    n_slots, rows, cols = parts.shape
    half = rows // 2
    br = _pick(half, block_rows, 16)
    nb = half // br

    def body(c_ref, a_ref, b_ref, o_ref):
        o_ref[...] = (a_ref[...].astype(f32) + b_ref[...].astype(f32)).astype(o_ref.dtype)

    grid_spec = pltpu.PrefetchScalarGridSpec(
        num_scalar_prefetch=1, grid=(n_slots, nb),
        in_specs=[pl.BlockSpec((1, br, cols), lambda k, i, c: (k, c[0] * nb + i, 0)),
                  pl.BlockSpec((1, br, cols), lambda k, i, c: (k, i, 0))],
        out_specs=pl.BlockSpec((1, br, cols), lambda k, i, c: (k, i, 0)))
    return pl.pallas_call(
        body, name=name, grid_spec=grid_spec, out_shape=jax.ShapeDtypeStruct((n_slots, half, cols), parts.dtype),
        compiler_params=_params(("parallel", "parallel")),
    )(core, parts, other)


def _core_join_halves(mine, name):
    half, cols = mine.shape
    piece = half // D2D_STREAMS
    assert half % (D2D_STREAMS * 16) == 0

    def body(src, out, send_sems, recv_sems, *stage):
        x, y, c = lax.axis_index("x"), lax.axis_index("y"), lax.axis_index("c")
        sibling = (x, y, 1 - c)
        sends = []
        for s in range(D2D_STREAMS):
            dst = out.at[pl.ds(pl.multiple_of(c * half + s * piece, 16), piece)]
            cp = _remote(src.at[pl.ds(s * piece, piece)], dst, send_sems.at[s], recv_sems.at[s], sibling)
            cp.start()
            sends.append(cp)
        _staged_copy(src, out.at[pl.ds(pl.multiple_of(c * half, 16), half)], *stage)
        for s in range(D2D_STREAMS):
            dst = out.at[pl.ds(pl.multiple_of((1 - c) * half + s * piece, 16), piece)]
            _remote(src.at[pl.ds(s * piece, piece)], dst, send_sems.at[s], recv_sems.at[s], sibling).wait_recv()
        for cp in sends:
            cp.wait_send()

    return pl.pallas_call(
        body, name=name, in_specs=[_ANY], out_specs=_ANY, out_shape=jax.ShapeDtypeStruct((2 * half, cols), mine.dtype),
        scratch_shapes=[pltpu.SemaphoreType.DMA((D2D_STREAMS,)), pltpu.SemaphoreType.DMA((D2D_STREAMS,))]
        + _stage_scratch(half, cols, mine.dtype),
    )(mine)


def _reduce_start(exchanging, core, after, tag):
    parts, other = _core_send_other_half_wait(*exchanging[:4], after, f"exchange_wait_{tag}")
    return _chip_scatter_start(_add_my_half(parts, other, core, f"sum_core_halves_{tag}"), f"scatter_start_{tag}")


def _reduce_finish(started, chip, after, tag):
    send_sems, recv_sems, chip_part, land, _ = started
    own, landed = _chip_scatter_wait(send_sems, recv_sems, chip_part, land, after, f"scatter_wait_{tag}")
    return _core_join_halves(_sum_slots_own(landed, own, chip, f"sum_chip_parts_{tag}"), f"join_core_halves_{tag}")


def _reduce_to_chips(parts, core, tag):
    chip_part = _add_my_half(parts, _core_send_other_half(parts, f"exchange_core_halves_{tag}"), core, f"sum_core_halves_{tag}")
    my_sum = _sum_slots(_chip_scatter(chip_part, f"scatter_{tag}"), f"sum_chip_parts_{tag}")
    return _core_join_halves(my_sum, f"join_core_halves_{tag}")


def _sum_slots(stack, name, block_rows=256):
    s_n, r_n, c_n = stack.shape
    br = _pick(r_n, block_rows, 8)

    def body(in_ref, o_ref):
        acc = in_ref[0].astype(f32)
        for s in range(1, s_n):
            acc = acc + in_ref[s].astype(f32)
        o_ref[...] = acc

    return pl.pallas_call(
        body, name=name, grid=(r_n // br,), in_specs=[pl.BlockSpec((s_n, br, c_n), lambda i: (0, i, 0))],
        out_specs=pl.BlockSpec((br, c_n), lambda i: (i, 0)), out_shape=jax.ShapeDtypeStruct((r_n, c_n), f32),
        compiler_params=_params(("parallel",)),
    )(stack)


def _concat_padded(parts, mult):
    rows = sum(p.shape[0] for p in parts)
    pad = (-rows) % mult
    if pad:
        parts = parts + [jnp.zeros((pad,) + parts[0].shape[1:], parts[0].dtype)]
    return jnp.concatenate(parts, axis=0)


def _pack_weights(w, l, names, conv_w=None):
    parts = [w[n][l].astype(bf16).reshape(-1, PACK_COLS) for n in names]
    if conv_w is not None:
        parts.append(_concat_padded([lax.bitcast_convert_type(conv_w, bf16).reshape(-1, PACK_COLS)], 16))
    return _concat_padded(parts, PACK_ROW_MULT)


def _unpack_weights(full, w, l, names, conv_w=None):
    start, r0 = {}, 0
    for n in names:
        start[n] = r0
        r0 += w[n][l].size // PACK_COLS

    def shards(n):
        rows = w[n][l].size // PACK_COLS
        return [full[k, start[n]:start[n] + rows].reshape(w[n].shape[1:]) for k in range(N_CHIPS)]

    if conv_w is None:
        return shards, None
    rows = conv_w.size * 2 // PACK_COLS
    pieces = lax.bitcast_convert_type(full[:, r0:r0 + rows].reshape((N_CHIPS,) + conv_w.shape + (2,)), f32)
    return shards, jnp.concatenate([pieces[k] for k in range(N_CHIPS)], axis=2)


def _pack_big_grads(layer_grads):
    parts, slot_rows = [], 0
    for k in range(N_CHIPS):
        slot = []
        for n in BIG:
            for g in layer_grads:
                width = g[n].shape[0] // N_CHIPS
                slot.append(g[n][k * width:(k + 1) * width].astype(bf16).reshape(-1, PACK_COLS))
        slot_rows = sum(p.shape[0] for p in slot)
        pad = (-slot_rows) % PACK_ROW_MULT
        if pad:
            slot.append(jnp.zeros((pad, PACK_COLS), bf16))
        slot_rows += pad
        parts += slot
    return jnp.concatenate(parts, axis=0).reshape(N_CHIPS, slot_rows, PACK_COLS)


def _unpack_big_grads(summed, w):
    out, r0 = {}, 0
    for n in BIG:
        rows = w[n][0].size // PACK_COLS
        out[n] = [s[r0:r0 + rows].reshape(w[n].shape[1:]) for s in summed]
        r0 += rows
    return out


_SMALL_TILE = 8 * LANES


def _pack_small(vals, names):
    parts = []
    for n in names:
        pieces = vals[n] if isinstance(vals[n], list) else [vals[n]]
        size = sum(p.size for p in pieces)
        if all(p.size % _SMALL_TILE == 0 for p in pieces):
            parts += [p.reshape(-1, LANES) for p in pieces]
        else:
            flat = [p.reshape(-1) for p in pieces] + [jnp.zeros(((-size) % _SMALL_TILE,), f32)]
            parts.append(jnp.concatenate(flat).reshape(-1, LANES))
    return _concat_padded(parts, PACK_ROW_MULT)


def _unpack_small(packed, like, names):
    out, r0 = {}, 0
    for n in names:
        size = like[n].size
        rows = -(-size // _SMALL_TILE) * 8
        out[n] = packed[r0:r0 + rows].reshape(-1)[:size].reshape(like[n].shape)
        r0 += rows
    return out


def _dims(w, x):
    d = {}
    d["D"] = x.shape[-1]
    d["T"] = x.shape[-2]
    d["DI"] = w["ssd_norm_g"].shape[-1]
    d["NH"] = w["ssd_dt_bias"].shape[-1]
    d["CD"] = w["ssd_conv_b"].shape[-1]
    d["G"] = SSD_N_GROUPS
    d["HPG"] = d["NH"] // d["G"]
    d["P"] = d["DI"] // d["NH"]
    d["N"] = (d["CD"] - d["DI"]) // (2 * d["G"])
    d["S5G"], d["S5N"] = w["s5_lambda_re"].shape[-2:]
    d["S5C"] = w["s5_b_re"].shape[-1]
    d["S5W"] = d["S5G"] * d["S5C"]
    d["NSB"] = d["S5W"] // S5_SUPERBLOCK
    d["GSB"] = d["S5G"] // d["NSB"]
    return d


def _head_pad(v, d):
    lead = v.shape[:-1]
    v = v.reshape(lead + (d["G"], d["HPG"]))
    v = jnp.concatenate([v, jnp.zeros(lead + (d["G"], LANES - d["HPG"]), v.dtype)], axis=-1)
    return v.reshape(lead + (d["G"] * LANES,))


def _head_unpad(v, d):
    lead = v.shape[:-1]
    return v.reshape(lead + (d["G"], LANES))[..., :d["HPG"]].reshape(lead + (d["NH"],))


def _w_in_perm(shards, d):
    o, nh = d["DI"] + d["CD"], d["NH"]
    r = shards[0].shape[0]

    def rows(lo, hi):
        out = []
        for k, s in enumerate(shards):
            a, b = max(lo, k * r), min(hi, (k + 1) * r)
            if a < b:
                out.append(s[a - k * r:b - k * r])
        return out

    dt = _head_pad(jnp.concatenate(rows(o, o + nh), axis=0).T, d).T
    return jnp.concatenate(rows(0, o) + rows(o + nh, len(shards) * r) + [dt], axis=0)


def _w_in_unperm(g, d):
    o = d["DI"] + d["CD"]
    rest = d["S5W"] + 2 * d["D"]
    return jnp.concatenate([g[:o], _head_unpad(g[o + rest:].T, d).T, g[o:o + rest]], axis=0)


def _s5_block_diag(v, d):
    gsb = d["GSB"]
    g, a, b = v.shape
    row_group = (lax.broadcasted_iota(jnp.int32, (g * a, gsb * b), 0) // a) % gsb
    col_group = lax.broadcasted_iota(jnp.int32, (g * a, gsb * b), 1) // b
    return jnp.where(row_group == col_group, jnp.tile(v.reshape(g * a, b), (1, gsb)), 0)


def _s5_diag_blocks(m, d, a, b):
    gsb = d["GSB"]
    rows = m.shape[0]
    m = m.reshape(rows, gsb, b)
    row_group = (lax.broadcasted_iota(jnp.int32, (rows, gsb, 1), 0) // a) % gsb
    col_group = lax.broadcasted_iota(jnp.int32, (rows, gsb, 1), 1)
    return jnp.sum(jnp.where(row_group == col_group, m, 0), axis=1).reshape(rows // a, a, b)


def _s5_lam_rows(v, d):
    v = v.reshape(d["NSB"], 1, d["GSB"] * d["S5N"])
    return jnp.broadcast_to(v, (d["NSB"], 8, v.shape[-1])).reshape(d["NSB"] * 8, -1)


def _ffn_fwd(h, pre_g, post_g, wgu, wd, tag):
    D = h.shape[1]
    H2 = wgu.shape[0]
    xn = _row_kernel(f"{tag}_norm", _fwd_of(_f_norm), [(h, D, 0)], [pre_g], [(D, bf16)])[0]
    ab = _mm(xn, wgu, "nt", bf16, f"{tag}_mm_up")
    hid = _row_kernel(f"{tag}_swiglu", _swiglu_fwd, [(ab, H2, 0)], [], [(H2 // 2, bf16)])[0]
    f = _mm(hid, wd, "nn", f32, f"{tag}_mm_down")
    out = _row_kernel(f"{tag}_resnorm", _fwd_of(_f_resnorm(0.5)), [(h, D, 0), (f, D, 0)], [post_g], [(D, f32)])[0]
    return out, dict(h=h, xn=xn, ab=ab, hid=hid, f=f)


def _ffn_bwd(dh_out, s, pre_g, post_g, wgu, wd, tag):
    D = dh_out.shape[1]
    H2 = wgu.shape[0]
    df, dpost = _row_kernel(f"{tag}_resnorm_bwd", _vjp_of(_f_post(0.5), 1, 1, [0]), [(s["f"], D, 0), (dh_out, D, 0)],
                            [post_g], [(D, bf16)], [post_g.shape])
    dwd = _mm(s["hid"], df, "tn", bf16, f"{tag}_mm_dwd")
    dhid = _mm(df, wd, "nt", bf16, f"{tag}_mm_dhid")
    dab = _row_kernel(f"{tag}_swiglu_bwd", _swiglu_bwd, [(s["ab"], H2, 0), (dhid, H2 // 2, 0)], [], [(H2, bf16)])[0]
    dwgu = _mm(dab, s["xn"], "tn", bf16, f"{tag}_mm_dwgu")
    dxn = _mm(dab, wgu, "nn", f32, f"{tag}_mm_dxn")
    dh, dpre = _row_kernel(f"{tag}_norm_bwd", _vjp_of(_f_norm, 1, 1, [0], 1), [(s["h"], D, 0), (dxn, D, 0), (dh_out, D, 0)],
                           [pre_g], [(D, f32)], [pre_g.shape])
    return dh, dict(pre_g=dpre, post_g=dpost, wgu=dwgu, wd=dwd)


def _mixer_fwd(h, p, d):
    D, DI, CD, G, N = d["D"], d["DI"], d["CD"], d["G"], d["N"]
    gl = G * LANES
    c_u5, c_ga, c_gb, c_dt = DI + CD, DI + CD + d["S5W"], DI + CD + d["S5W"] + D, DI + CD + d["S5W"] + 2 * D
    u = _row_kernel("mix_norm", _fwd_of(_f_norm), [(h, D, 0)], [p["mix_pre_g"]], [(D, bf16)])[0]
    proj = _mm(u, p["w_in"], "nt", f32, "mix_mm_in", bn_t=512)
    act = _conv_fwd(proj, DI, p["conv_w"], p["conv_b"], "ssd_conv")
    dt, adt = _row_kernel("ssd_dt", _fwd_of(_f_dt), [(proj, gl, c_dt // gl)], [p["dt_bias"], p["a_log"]], [(gl, f32)] * 2)
    y_ssd, states = _ssd_fwd(act, dt, adt, p["d_skip"], d["HPG"], d["P"], N, "ssd_scan")
    nrm = _row_kernel("ssd_post", _fwd_of(_f_ssdpost(G)), [(y_ssd, DI, 0), (proj, DI, 0)], [p["norm_g"]], [(DI, bf16)])[0]
    y_a = _mm(nrm, p["w_a"], "nn", f32, "mix_mm_a")
    u5 =(proj, d["S5W"], c_u5 // d["S5W"])
    bu = _s5_in(proj, c_u5, p["bsb"], d)
    s5st = _s5_scan_fwd(bu, p["lam_re_rows"], p["lam_im_rows"], d["NSB"], "s5_scan")
    y5 = _bdmm(s5st, p["csb"], "nn", d["NSB"], f32, "s5_mm_c")
    gel = _row_kernel("s5_post", _fwd_of(_f_s5post), [(y5, d["S5W"], 0), u5], [p["s5_d"]], [(d["S5W"], bf16)])[0]
    vg = _mm(gel, p["w_glu"], "nt", bf16, "mix_mm_glu")
    glu = _row_kernel("s5_glu", _glu_fwd, [(vg, vg.shape[1], 0)], [], [(vg.shape[1] // 2, bf16)])[0]
    y_b = _mm(glu, p["w_b"], "nn", f32, "mix_mm_b")
    merged = _row_kernel("mix_merge", _fwd_of(_f_merge), [(proj, D, c_ga // D), (y_a, D, 0), (proj, D, c_gb // D), (y_b, D, 0)],
                         [], [(D, bf16)])[0]
    m = _mm(merged, p["w_out"], "nn", f32, "mix_mm_out")
    out = _row_kernel("mix_resnorm", _fwd_of(_f_resnorm(1.0)), [(h, D, 0), (m, D, 0)], [p["mix_post_g"]], [(D, f32)])[0]
    return out, dict(h=h, u=u, proj=proj, act=act, dt=dt, adt=adt, states=states, y_ssd=y_ssd, nrm=nrm, y_a=y_a, s5st=s5st,
                     y5=y5, gel=gel, vg=vg, glu=glu, y_b=y_b, merged=merged, m=m)


def _s5_in(proj, c_u5, bsb, d):
    T = proj.shape[0]
    nsb = d["NSB"]
    ka, nw = S5_SUPERBLOCK, bsb.shape[1]
    off = c_u5 // ka
    assert c_u5 % ka == 0
    bt = _pick(T, 512)

    def body(a_ref, w_ref, o_ref):
        o_ref[...] = _dot(a_ref[...].astype(bf16), w_ref[...].astype(bf16))

    return pl.pallas_call(
        body, name="s5_mm_bu", grid=(nsb, T // bt),
        in_specs=[pl.BlockSpec((bt, ka), lambda j, i: (i, off + j)), pl.BlockSpec((ka, nw), lambda j, i: (j, 0))],
        out_specs=pl.BlockSpec((bt, nw), lambda j, i: (i, j)), out_shape=jax.ShapeDtypeStruct((T, nsb * nw), f32),
        compiler_params=_params(("parallel", "parallel")),
    )(proj, bsb)


def _s5_dbsb(proj, c_u5, a, d):
    T = proj.shape[0]
    nsb = d["NSB"]
    ka, nw = S5_SUPERBLOCK, a.shape[1] // nsb
    off = c_u5 // ka
    bt = _pick(T, 512)

    def body(u_ref, a_ref, o_ref):
        pr = _dot(u_ref[...].astype(bf16), a_ref[...].astype(bf16), "tn")
        k = pl.program_id(1)

        @pl.when(k == 0)
        def _():
            o_ref[...] = pr

        @pl.when(k > 0)
        def _():
            o_ref[...] += pr

    return pl.pallas_call(
        body, name="s5_mm_dbsb", grid=(nsb, T // bt),
        in_specs=[pl.BlockSpec((bt, ka), lambda j, k: (k, off + j)), pl.BlockSpec((bt, nw), lambda j, k: (k, j))],
        out_specs=pl.BlockSpec((ka, nw), lambda j, k: (j, 0)), out_shape=jax.ShapeDtypeStruct((nsb * ka, nw), f32),
        compiler_params=_params(("parallel", "arbitrary")),
    )(proj, a)


def _mixer_bwd(dh_out, s, p, d):
    D, DI, CD, G, N, S5W = d["D"], d["DI"], d["CD"], d["G"], d["N"], d["S5W"]
    gl = G * LANES
    gn = G * N
    c_u5, c_ga, c_gb, c_dt = DI + CD, DI + CD + S5W, DI + CD + S5W + D, DI + CD + S5W + 2 * D
    proj = s["proj"]
    g = {}
    dm, g["mix_post_g"] = _row_kernel("mix_resnorm_bwd", _vjp_of(_f_post(1.0), 1, 1, [0]), [(s["m"], D, 0), (dh_out, D, 0)],
                                      [p["mix_post_g"]], [(D, bf16)], [p["mix_post_g"].shape])
    g["w_out"] = _mm(s["merged"], dm, "tn", bf16, "mix_mm_dwout")
    dmerged = _mm(dm, p["w_out"], "nt", f32, "mix_mm_dmerged")
    dga, dya, dgb, dyb = _row_kernel(
        "mix_merge_bwd", _vjp_of(_f_merge, 4, 1, [0, 1, 2, 3]),
        [(proj, D, c_ga // D), (s["y_a"], D, 0), (proj, D, c_gb // D), (s["y_b"], D, 0), (dmerged, D, 0)], [],
        [(D, bf16), (D, bf16), (D, bf16), (D, bf16)])
    g["w_a"] = _mm(s["nrm"], dya, "tn", bf16, "mix_mm_dwa")
    dnrm = _mm(dya, p["w_a"], "nt", f32, "mix_mm_dnrm")
    dy_ssd, dz, g["norm_g"] = _row_kernel(
        "ssd_post_bwd", _vjp_of(_f_ssdpost(G), 2, 1, [0, 1]), [(s["y_ssd"], DI, 0), (proj, DI, 0), (dnrm, DI, 0)],
        [p["norm_g"]], [(DI, f32), (DI, bf16)], [p["norm_g"].shape])
    dxs, d_b, d_c, ddt, dadt, dd = _ssd_bwd(s["act"], s["dt"], s["adt"], p["d_skip"], s["states"], dy_ssd,
                                            d["HPG"], d["P"], N, "ssd_scan_bwd")
    g["d_skip"] = dd.reshape(G, 8, LANES)[:, 0, :].reshape(1, gl)
    ddt_raw, g["dt_bias"], g["a_log"] = _row_kernel(
        "ssd_dt_bwd", _vjp_of(_f_dt, 1, 2, [0]), [(proj, gl, c_dt // gl), (ddt, gl, 0), (dadt, gl, 0)],
        [p["dt_bias"], p["a_log"]], [(gl, bf16)], [p["dt_bias"].shape, p["a_log"].shape])
    cw, cb = p["conv_w"], p["conv_b"]
    dxc_x, dw_x, db_x = _conv_bwd(proj, DI, cw[:, :DI], cb[:, :DI], dxs, "ssd_conv_bwd_x")
    dxc_b, dw_b, db_b = _conv_bwd(proj, 2 * DI, cw[:, DI:DI + gn], cb[:, DI:DI + gn], d_b, "ssd_conv_bwd_b")
    dxc_c, dw_c, db_c = _conv_bwd(proj, 2 * DI + gn, cw[:, DI + gn:], cb[:, DI + gn:], d_c, "ssd_conv_bwd_c")
    g["conv_w"] = jnp.concatenate([dw_x, dw_b, dw_c], axis=1)
    g["conv_b"] = jnp.concatenate([db_x, db_b, db_c], axis=1)
    g["w_b"] = _mm(s["glu"], dyb, "tn", bf16, "mix_mm_dwb")
    dglu = _mm(dyb, p["w_b"], "nt", f32, "mix_mm_dglu")
    dvg = _row_kernel("s5_glu_bwd", _glu_bwd, [(s["vg"], s["vg"].shape[1], 0), (dglu, S5W, 0)], [], [(s["vg"].shape[1], bf16)])[0]
    g["w_glu"] = _mm(dvg, s["gel"], "tn", bf16, "mix_mm_dwglu")
    dgel = _mm(dvg, p["w_glu"], "nn", f32, "mix_mm_dgel")
    dy5, du5a, g["s5_d"] = _row_kernel(
        "s5_post_bwd", _vjp_of(_f_s5post, 2, 1, [0, 1]), [(s["y5"], S5W, 0), (proj, S5W, c_u5 // S5W), (dgel, S5W, 0)],
        [p["s5_d"]], [(S5W, bf16), (S5W, f32)], [p["s5_d"].shape])
    g["csb"] = _bdmm(s["s5st"], dy5, "tn", d["NSB"], f32, "s5_mm_dcsb")
    gst = _bdmm(dy5, p["csb"], "nt", d["NSB"], f32, "s5_mm_gst")
    a, g["lam_re_rows"], g["lam_im_rows"] = _s5_scan_bwd(gst, s["s5st"], p["lam_re_rows"], p["lam_im_rows"], d["NSB"], "s5_scan_bwd")
    g["bsb"] = _s5_dbsb(proj, c_u5, a, d)
    du5b = _bdmm(a, p["bsb"], "nt", d["NSB"], f32, "s5_mm_du5")
    du5 = _row_kernel("s5_du5", _add_fn, [(du5a, S5W, 0), (du5b, S5W, 0)], [], [(S5W, bf16)])[0]
    dproj = jnp.concatenate([dz, dxc_x, dxc_b, dxc_c, du5, dga, dgb, ddt_raw], axis=1)
    g["w_in"] = _mm(dproj, s["u"], "tn", bf16, "mix_mm_dwin")
    du = _mm(dproj, p["w_in"], "nn", f32, "mix_mm_du", bk_t=2176)
    dh, g["mix_pre_g"] = _row_kernel("mix_norm_bwd", _vjp_of(_f_norm, 1, 1, [0], 1), [(s["h"], D, 0), (du, D, 0), (dh_out, D, 0)],
                                     [p["mix_pre_g"]], [(D, f32)], [p["mix_pre_g"].shape])
    return dh, g


def _ffn1_params(l, w, wf):
    whole = lambda *names: jnp.concatenate([s for n in names for s in wf(n)], axis=0)
    return dict(ffn1_pre_g=w["ffn1_pre_g"][l].reshape(1, -1), ffn1_post_g=w["ffn1_post_g"][l].reshape(1, -1),
                wgu1=whole("ffn1_w_gate", "ffn1_w_up"), wd1=whole("ffn1_w_down"))


def _ffn2_params(wf):
    whole = lambda *names: jnp.concatenate([s for n in names for s in wf(n)], axis=0)
    return dict(wgu2=whole("ffn2_w_gate", "ffn2_w_up"), wd2=whole("ffn2_w_down"))


def _layer_params(l, w, wf, conv_w_full, d):
    r2 = lambda v: v[l].reshape(1, -1)
    p = {}
    for n in ["mix_pre_g", "mix_post_g", "ffn2_pre_g", "ffn2_post_g", "s5_d"]:
        p[n] = r2(w[n])
    whole = lambda *names: jnp.concatenate([s for n in names for s in wf(n)], axis=0)
    p["w_in"] = _w_in_perm(wf("w_in"), d)
    p["w_a"], p["w_glu"], p["w_b"], p["w_out"] = whole("w_branch_a"), whole("s5_w_glu"), whole("w_branch_b"), whole("w_out")
    p["conv_w"] = conv_w_full[l]
    p["conv_b"] = r2(w["ssd_conv_b"])
    p["dt_bias"] = _head_pad(r2(w["ssd_dt_bias"]), d)
    p["a_log"] = _head_pad(r2(w["ssd_a_log"]), d)
    p["d_skip"] = _head_pad(r2(w["ssd_d"]), d)
    p["norm_g"] = r2(w["ssd_norm_g"])
    g5, n5, c5 = d["S5G"], d["S5N"], d["S5C"]
    expand = jnp.repeat(jnp.eye(n5, dtype=f32), c5, axis=1)
    prep_in = [w["s5_lambda_re"][l], w["s5_lambda_im"][l], w["s5_log_step"][l].reshape(g5, 1),
               w["s5_b_re"][l].reshape(g5, n5 * c5), w["s5_b_im"][l].reshape(g5, n5 * c5), expand]
    lbr, lbi, bbr, bbi = _s5_prep(prep_in, "s5_prep")
    p["s5_prep_in"] = prep_in
    p["lam_re_rows"], p["lam_im_rows"] = _s5_lam_rows(lbr, d), _s5_lam_rows(lbi, d)
    to_cn = lambda v: v.astype(bf16).reshape(g5, n5, c5).transpose(0, 2, 1)
    p["bsb"] = jnp.concatenate([_s5_block_diag(to_cn(bbr), d), _s5_block_diag(to_cn(bbi), d)], axis=1)
    c_re = w["s5_c_re"][l].astype(bf16).transpose(0, 2, 1)
    c_im = w["s5_c_im"][l].astype(bf16).transpose(0, 2, 1)
    nsb = d["NSB"]
    csb = jnp.stack([_s5_block_diag(c_re, d).reshape(nsb, -1, S5_SUPERBLOCK),
                     _s5_block_diag(-c_im, d).reshape(nsb, -1, S5_SUPERBLOCK)], axis=1)
    p["csb"] = csb.reshape(-1, S5_SUPERBLOCK)
    return p


def _s5_param_grads(g, p, d, l):
    g5, n5, c5, nsb, gsb = d["S5G"], d["S5N"], d["S5C"], d["NSB"], d["GSB"]
    wst = gsb * n5
    dbsb = g["bsb"]
    from_cn = lambda v: v.transpose(0, 2, 1).reshape(g5, n5 * c5)
    dbbr = from_cn(_s5_diag_blocks(dbsb[:, :wst], d, c5, n5))
    dbbi = from_cn(_s5_diag_blocks(dbsb[:, wst:], d, c5, n5))
    rows = lambda v: v.reshape(nsb, 8, wst)[:, 0, :].reshape(g5, n5)
    cots = [rows(g["lam_re_rows"]), rows(g["lam_im_rows"]), dbbr, dbbi]
    dlr, dli, dls, dbr, dbi = _s5_prep_bwd(p["s5_prep_in"], cots, "s5_prep_bwd")
    dcsb = g["csb"].reshape(nsb, 2, wst, S5_SUPERBLOCK)
    dcr = _s5_diag_blocks(dcsb[:, 0].reshape(-1, S5_SUPERBLOCK), d, n5, c5).transpose(0, 2, 1)
    dci = -_s5_diag_blocks(dcsb[:, 1].reshape(-1, S5_SUPERBLOCK), d, n5, c5).transpose(0, 2, 1)
    return dict(s5_lambda_re=dlr, s5_lambda_im=dli, s5_log_step=dls.reshape(g5), s5_b_re=dbr.reshape(g5, n5, c5),
                s5_b_im=dbi.reshape(g5, n5, c5), s5_c_re=dcr, s5_c_im=dci)


def kernel(x, ffn1_pre_g, ffn1_post_g, ffn1_w_gate, ffn1_w_up, ffn1_w_down, mix_pre_g, mix_post_g, w_in, ssd_conv_w, ssd_conv_b, ssd_dt_bias, ssd_a_log, ssd_d, ssd_norm_g, w_branch_a, s5_lambda_re, s5_lambda_im, s5_b_re, s5_b_im, s5_c_re, s5_c_im, s5_log_step, s5_d, s5_w_glu, w_branch_b, w_out, ffn2_pre_g, ffn2_post_g, ffn2_w_gate, ffn2_w_up, ffn2_w_down, loss_target, m_ffn1_pre_g, m_ffn1_post_g, m_ffn1_w_gate, m_ffn1_w_up, m_ffn1_w_down, m_mix_pre_g, m_mix_post_g, m_w_in, m_ssd_conv_w, m_ssd_conv_b, m_ssd_dt_bias, m_ssd_a_log, m_ssd_d, m_ssd_norm_g, m_w_branch_a, m_s5_lambda_re, m_s5_lambda_im, m_s5_b_re, m_s5_b_im, m_s5_c_re, m_s5_c_im, m_s5_log_step, m_s5_d, m_s5_w_glu, m_w_branch_b, m_w_out, m_ffn2_pre_g, m_ffn2_post_g, m_ffn2_w_gate, m_ffn2_w_up, m_ffn2_w_down, v_ffn1_pre_g, v_ffn1_post_g, v_ffn1_w_gate, v_ffn1_w_up, v_ffn1_w_down, v_mix_pre_g, v_mix_post_g, v_w_in, v_ssd_conv_w, v_ssd_conv_b, v_ssd_dt_bias, v_ssd_a_log, v_ssd_d, v_ssd_norm_g, v_w_branch_a, v_s5_lambda_re, v_s5_lambda_im, v_s5_b_re, v_s5_b_im, v_s5_c_re, v_s5_c_im, v_s5_log_step, v_s5_d, v_s5_w_glu, v_w_branch_b, v_w_out, v_ffn2_pre_g, v_ffn2_post_g, v_ffn2_w_gate, v_ffn2_w_up, v_ffn2_w_down):
    given = dict(locals())
    for n in COL_SHARDED:
        for prefix in ("", "m_", "v_"):
            given[prefix + n] = given[prefix + n].transpose(0, 2, 1)
    w = {n: given[n] for n in WEIGHTS}
    mom = {n: given["m_" + n] for n in WEIGHTS}
    var = {n: given["v_" + n] for n in WEIGHTS}
    d = _dims(w, x)
    n_layers = w["ffn1_pre_g"].shape[0]
    T, D = d["T"], d["D"]

    conv_w = w["ssd_conv_w"]
    first, middle, last = BIG[:3], BIG[3:8], BIG[8:]
    wf, _ = _unpack_weights(_chip_all_gather(_pack_weights(w, 0, first), "gather_weights_first"), w, 0, first)
    coming = _chip_gather_start(_pack_weights(w, 0, middle, conv_w), "gather_start_l0")
    h = x.reshape(T, D) + coming[-1][0, 0]
    saved, layers = [], []
    for l in range(n_layers):
        if l > 0:
            pack, land = _chip_gather_wait(*coming[:4], h, f"gather_wait_l{l}")
            wf, _ = _unpack_weights(_chip_gather_finish(pack, land, f"gather_finish_l{l}"), w, l, BIG)
        p = _ffn1_params(l, w, wf)
        h, s1 = _ffn_fwd(h, p["ffn1_pre_g"], p["ffn1_post_g"], p["wgu1"], p["wd1"], "ffn1")
        if l == 0:
            pack, land = _chip_gather_wait(*coming[:4], h, "gather_wait_l0")
            wf, conv_w_full = _unpack_weights(_chip_gather_finish(pack, land, "gather_finish_l0"), w, 0, middle, conv_w)
            ffn2_coming = _chip_gather_start(_pack_weights(w, 0, last), "gather_start_l0_ffn2")
            h = h + ffn2_coming[-1][0, 0]
        if l + 1 < n_layers:
            coming = _chip_gather_start(_pack_weights(w, l + 1, BIG), f"gather_start_l{l + 1}")
            h = h + coming[-1][0, 0]
        p.update(_layer_params(l, w, wf, conv_w_full, d))
        layers.append(p)
        h, sm = _mixer_fwd(h, p, d)
        if l == 0:
            pack, land = _chip_gather_wait(*ffn2_coming[:4], h, "gather_wait_l0_ffn2")
            wf, _ = _unpack_weights(_chip_gather_finish(pack, land, "gather_finish_l0_ffn2"), w, 0, last)
        p.update(_ffn2_params(wf))
        h, s2 = _ffn_fwd(h, p["ffn2_pre_g"], p["ffn2_post_g"], p["wgu2"], p["wd2"], "ffn2")
        saved.append((s1, sm, s2))
    dh, loss_part = _row_kernel("loss", _loss_fn, [(h, D, 0), (loss_target.reshape(T, D), D, 0)], [], [(D, f32)], [(8, LANES)])
    loss = lax.psum(loss_part[0, 0], ("x", "y", "c"))

    my_core = lax.axis_index("c").astype(jnp.int32).reshape(1)
    my_chip = (2 * lax.axis_index("x") + lax.axis_index("y")).astype(jnp.int32).reshape(1)
    lg, exchanging, in_flight = [None] * n_layers, [None] * n_layers, [None] * n_layers
    for l in reversed(range(n_layers)):
        p = layers[l]
        s1, sm, s2 = saved[l]
        dh, g2 = _ffn_bwd(dh, s2, p["ffn2_pre_g"], p["ffn2_post_g"], p["wgu2"], p["wd2"], "ffn2")
        if l + 1 < n_layers:
            in_flight[l + 1] = _reduce_start(exchanging[l + 1], my_core, dh, f"grads_l{l + 1}")
            dh = dh + in_flight[l + 1][-1][0, 0]
        dh, gm = _mixer_bwd(dh, sm, p, d)
        dh, g1 = _ffn_bwd(dh, s1, p["ffn1_pre_g"], p["ffn1_post_g"], p["wgu1"], p["wd1"], "ffn1")
        H = p["wd1"].shape[0]
        gl = dict(ffn1_pre_g=g1["pre_g"], ffn1_post_g=g1["post_g"], ffn1_w_gate=g1["wgu"][:H], ffn1_w_up=g1["wgu"][H:],
                  ffn1_w_down=g1["wd"], ffn2_pre_g=g2["pre_g"], ffn2_post_g=g2["post_g"], ffn2_w_gate=g2["wgu"][:H],
                  ffn2_w_up=g2["wgu"][H:], ffn2_w_down=g2["wd"], mix_pre_g=gm["mix_pre_g"], mix_post_g=gm["mix_post_g"],
                  w_in=_w_in_unperm(gm["w_in"], d), ssd_conv_w=gm["conv_w"], ssd_conv_b=gm["conv_b"],
                  ssd_dt_bias=_head_unpad(gm["dt_bias"], d), ssd_a_log=_head_unpad(gm["a_log"], d),
                  ssd_d=_head_unpad(gm["d_skip"], d), ssd_norm_g=gm["norm_g"], w_branch_a=gm["w_a"], s5_d=gm["s5_d"],
                  s5_w_glu=gm["w_glu"], w_branch_b=gm["w_b"], w_out=gm["w_out"])
        gl.update(_s5_param_grads(gm, p, d, l))
        lg[l] = gl
        exchanging[l] = _core_send_other_half_start(_pack_big_grads([gl]), f"exchange_start_grads_l{l}")
        if l > 0:
            dh = dh + exchanging[l][-1][0, 0]
    grad_x = dh.reshape(x.shape)
    summed = [None] * n_layers
    for l in range(1, n_layers):
        summed[l] = _reduce_finish(in_flight[l], my_chip, exchanging[0][-1], f"grads_l{l}")
    in_flight[0] = _reduce_start(exchanging[0], my_core, summed[-1] if n_layers > 1 else grad_x, "grads_l0")
    small_names = SMALL + ["ssd_conv_w"]
    small_parts = {n: [g[n] for g in lg] for n in small_names}
    small_like = {n: jax.ShapeDtypeStruct((n_layers,) + lg[0][n].shape, f32) for n in small_names}
    small_like.update({n: w[n] for n in SMALL})
    small_pack = _pack_small(small_parts, small_names)
    small_sum = _reduce_to_chips(jnp.broadcast_to(small_pack, (N_CHIPS,) + small_pack.shape), my_core, "small")
    small = _unpack_small(small_sum, small_like, small_names)
    summed[0] = _reduce_finish(in_flight[0], my_chip, small_sum, "grads_l0")
    big_grads = _unpack_big_grads(summed, w)
    grads = {}
    k_me = 2 * lax.axis_index("x") + lax.axis_index("y")
    cw = w["ssd_conv_w"].shape[-1]
    small["ssd_conv_w"] = lax.dynamic_slice_in_dim(small["ssd_conv_w"], k_me * cw, cw, axis=2)
    grads.update(small)

    delta, new_m, new_v = {}, {}, {}
    for n in BIG:
        grads[n], delta[n], new_m[n], new_v[n] = _adamw_layers(w[n], big_grads[n], mom[n], var[n], "adamw_" + n)
    n = "ssd_conv_w"
    delta[n], new_m[n], new_v[n] = _adamw(w[n], grads[n], mom[n], var[n], "adamw_" + n)
    pw, pm, pv = [_pack_small(t, SMALL) for t in (w, mom, var)]
    assert pw.shape[0] <= small_sum.shape[0]
    sd, sm_, sv = _adamw(pw, small_sum[:pw.shape[0]], pm, pv, "adamw_small")
    delta.update(_unpack_small(sd, w, SMALL))
    new_m.update(_unpack_small(sm_, w, SMALL))
    new_v.update(_unpack_small(sv, w, SMALL))
    for n in COL_SHARDED:
        for out in (grads, delta, new_m, new_v):
            out[n] = out[n].transpose(0, 2, 1)
    return (loss, grad_x, *[grads[n] for n in WEIGHTS], *[delta[n] for n in WEIGHTS],
            *[new_m[n] for n in WEIGHTS], *[new_v[n] for n in WEIGHTS])
```

```python
import functools

import numpy as np
import jax
import jax.numpy as jnp
from jax import lax
from jax.experimental import pallas as pl
from jax.experimental.pallas import tpu as pltpu

f32, bf16 = jnp.float32, jnp.bfloat16

SSD_N_GROUPS = 4
SSD_CHUNK = 128
RMS_EPS = 1e-6
S5_MAX_REAL = -1e-4
S5_SUPERBLOCK = 256
ADAM_LR, ADAM_B1, ADAM_B2, ADAM_EPS, ADAM_WD, ADAM_STEP = 0.001, 0.9, 0.999, 1e-08, 0.01, 10

LANES = 128
PACK_COLS = 1024
D2D_STREAMS = 16
PACK_ROW_MULT = 2 * D2D_STREAMS * 16
VMEM_LIMIT_BYTES = 48 * 1024 * 1024
N_CHIPS, N_CORES, N_DEV = 4, 2, 8
MESH = pl.DeviceIdType.MESH

BIG = ["ffn1_w_gate", "ffn1_w_up", "ffn1_w_down", "w_in", "w_branch_a", "s5_w_glu", "w_branch_b", "w_out",
       "ffn2_w_gate", "ffn2_w_up", "ffn2_w_down"]
COL_SHARDED = ["ffn1_w_gate", "ffn1_w_up", "w_in", "s5_w_glu", "ffn2_w_gate", "ffn2_w_up"]
SMALL = ["ffn1_pre_g", "ffn1_post_g", "mix_pre_g", "mix_post_g", "ssd_conv_b", "ssd_norm_g", "s5_lambda_re", "s5_lambda_im",
         "s5_b_re", "s5_b_im", "s5_c_re", "s5_c_im", "s5_d", "ffn2_pre_g", "ffn2_post_g", "s5_log_step", "ssd_dt_bias",
         "ssd_a_log", "ssd_d"]
WEIGHTS = ["ffn1_pre_g", "ffn1_post_g", "ffn1_w_gate", "ffn1_w_up", "ffn1_w_down", "mix_pre_g", "mix_post_g", "w_in",
           "ssd_conv_w", "ssd_conv_b", "ssd_dt_bias", "ssd_a_log", "ssd_d", "ssd_norm_g", "w_branch_a", "s5_lambda_re",
           "s5_lambda_im", "s5_b_re", "s5_b_im", "s5_c_re", "s5_c_im", "s5_log_step", "s5_d", "s5_w_glu", "w_branch_b",
           "w_out", "ffn2_pre_g", "ffn2_post_g", "ffn2_w_gate", "ffn2_w_up", "ffn2_w_down"]


def _params(sem=None):
    return pltpu.CompilerParams(dimension_semantics=sem, vmem_limit_bytes=VMEM_LIMIT_BYTES)


def _pick(n, target, mult=LANES):
    best = None
    for d in range(mult, min(n, target) + 1, mult):
        if n % d == 0:
            best = d
    return best if best is not None else n


_DIMS = {"nn": (((1,), (0,)), ((), ())), "nt": (((1,), (1,)), ((), ())), "tn": (((0,), (0,)), ((), ()))}


def _mm(a, b, mode, out_dtype, name, bm_t=1024, bn_t=1024, bk_t=2816):
    if mode == "nn":
        (M, K), (K2, N) = a.shape, b.shape
    elif mode == "nt":
        (M, K), (N, K2) = a.shape, b.shape
    else:
        (K, M), (K2, N) = a.shape, b.shape
    assert K == K2, (name, a.shape, b.shape)
    bm, bn, bk = _pick(M, bm_t), _pick(N, bn_t), _pick(K, bk_t)
    nk = K // bk
    dn = _DIMS[mode]

    def body(a_ref, b_ref, o_ref, *scratch):
        p = lax.dot_general(a_ref[...].astype(bf16), b_ref[...].astype(bf16), dn, preferred_element_type=f32)
        if nk == 1:
            o_ref[...] = p.astype(o_ref.dtype)
        else:
            acc = scratch[0]
            k = pl.program_id(2)

            @pl.when(k == 0)
            def _():
                acc[...] = p

            @pl.when(k > 0)
            def _():
                acc[...] += p

            @pl.when(k == nk - 1)
            def _():
                o_ref[...] = acc[...].astype(o_ref.dtype)

    if mode == "tn":
        a_spec = pl.BlockSpec((bk, bm), lambda i, j, k: (k, i))
    else:
        a_spec = pl.BlockSpec((bm, bk), lambda i, j, k: (i, k))
    if mode == "nt":
        b_spec = pl.BlockSpec((bn, bk), lambda i, j, k: (j, k))
    else:
        b_spec = pl.BlockSpec((bk, bn), lambda i, j, k: (k, j))
    return pl.pallas_call(
        body, name=name, grid=(M // bm, N // bn, nk), in_specs=[a_spec, b_spec],
        out_specs=pl.BlockSpec((bm, bn), lambda i, j, k: (i, j)), out_shape=jax.ShapeDtypeStruct((M, N), out_dtype),
        scratch_shapes=[pltpu.VMEM((bm, bn), f32)] if nk > 1 else [],
        compiler_params=_params(("parallel", "parallel", "arbitrary")),
    )(a, b)


def _bdmm(a, w, mode, nb, out_dtype, name, bt_t=512):
    if mode == "tn":
        T = a.shape[0]
        ka, nw = a.shape[1] // nb, w.shape[1] // nb
        bt = _pick(T, bt_t)
        nt = T // bt

        def body_tn(a_ref, b_ref, o_ref):
            p = lax.dot_general(a_ref[...].astype(bf16), b_ref[...].astype(bf16), _DIMS["tn"], preferred_element_type=f32)
            k = pl.program_id(1)

            @pl.when(k == 0)
            def _():
                o_ref[...] = p

            @pl.when(k > 0)
            def _():
                o_ref[...] += p

        return pl.pallas_call(
            body_tn, name=name, grid=(nb, nt),
            in_specs=[pl.BlockSpec((bt, ka), lambda j, k: (k, j)), pl.BlockSpec((bt, nw), lambda j, k: (k, j))],
            out_specs=pl.BlockSpec((ka, nw), lambda j, k: (j, 0)), out_shape=jax.ShapeDtypeStruct((nb * ka, nw), f32),
            compiler_params=_params(("parallel", "arbitrary")),
        )(a, w)
    T = a.shape[0]
    ka, nw = w.shape[0] // nb, w.shape[1]
    bt = _pick(T, bt_t)
    kin, kout = (ka, nw) if mode == "nn" else (nw, ka)
    dn = _DIMS[mode]

    def body(a_ref, w_ref, o_ref):
        o_ref[...] = lax.dot_general(a_ref[...].astype(bf16), w_ref[...].astype(bf16), dn,
                                     preferred_element_type=f32).astype(o_ref.dtype)

    return pl.pallas_call(
        body, name=name, grid=(nb, T // bt),
        in_specs=[pl.BlockSpec((bt, kin), lambda j, i: (i, j)), pl.BlockSpec((ka, nw), lambda j, i: (j, 0))],
        out_specs=pl.BlockSpec((bt, kout), lambda j, i: (i, j)), out_shape=jax.ShapeDtypeStruct((T, nb * kout), out_dtype),
        compiler_params=_params(("parallel", "parallel")),
    )(a, w)


def _row_index(i, cb):
    return (i, cb)


def _row_kernel(name, fn, rows, pars, row_outs, par_outs=(), block_rows=256):
    T = rows[0][0].shape[0]
    R = min(block_rows, T)
    assert T % R == 0
    nr, npar, nro = len(rows), len(pars), len(row_outs)

    def body(*refs):
        rv = [r[...] for r in refs[:nr]]
        pv = [r[...] for r in refs[nr:nr + npar]]
        ro, po = fn(rv, pv)
        for ref, v in zip(refs[nr + npar:nr + npar + nro], ro):
            ref[...] = v.astype(ref.dtype)
        if par_outs:
            i = pl.program_id(0)
            prefs = refs[nr + npar + nro:]

            @pl.when(i == 0)
            def _():
                for ref, v in zip(prefs, po):
                    ref[...] = v.astype(f32)

            @pl.when(i > 0)
            def _():
                for ref, v in zip(prefs, po):
                    ref[...] += v.astype(f32)

    in_specs = [pl.BlockSpec((R, nc), functools.partial(_row_index, cb=cb)) for (_, nc, cb) in rows]
    in_specs += [pl.BlockSpec(p.shape, lambda i: (0, 0)) for p in pars]
    out_specs = [pl.BlockSpec((R, nc), lambda i: (i, 0)) for (nc, _) in row_outs]
    out_specs += [pl.BlockSpec(s, lambda i: (0, 0)) for s in par_outs]
    out_shape = [jax.ShapeDtypeStruct((T, nc), dt) for (nc, dt) in row_outs]
    out_shape += [jax.ShapeDtypeStruct(s, f32) for s in par_outs]
    outs = pl.pallas_call(
        body, name=name, grid=(T // R,), in_specs=in_specs, out_specs=out_specs, out_shape=out_shape,
        compiler_params=_params(("arbitrary",) if par_outs else ("parallel",)),
    )(*[r[0] for r in rows], *pars)
    return list(outs)


def _fwd_of(f):
    def fn(rv, pv):
        return f([v.astype(f32) for v in rv], [v.astype(f32) for v in pv]), []
    return fn


def _vjp_of(f, n_x, n_cot, grad_idx, n_add=0):
    def fn(rv, pv):
        xs = [v.astype(f32) for v in rv[:n_x]]
        cots = [v.astype(f32) for v in rv[n_x:n_x + n_cot]]
        adds = rv[n_x + n_cot:n_x + n_cot + n_add]
        ps = [v.astype(f32) for v in pv]
        _, vjp = jax.vjp(lambda *a: f(list(a[:n_x]), list(a[n_x:])), *xs, *ps)
        g = vjp(cots)
        row_g = [g[i] for i in grad_idx]
        for k, a in enumerate(adds):
            row_g[k] = row_g[k] + a.astype(f32)
        return row_g, list(g[n_x:])
    return fn


def _rms(x, g):
    return x * lax.rsqrt(jnp.mean(x * x, axis=-1, keepdims=True) + RMS_EPS) * g


def _f_norm(xs, ps):
    return [_rms(xs[0], ps[0])]


def _f_post(scale):
    def f(xs, ps):
        return [scale * _rms(xs[0], ps[0])]
    return f


def _f_resnorm(scale):
    def f(xs, ps):
        return [xs[0] + scale * _rms(xs[1], ps[0])]
    return f


def _f_dt(xs, ps):
    dt = jax.nn.softplus(xs[0] + ps[0])
    return [dt, -jnp.exp(ps[1]) * dt]


def _f_ssdpost(n_groups):
    def f(xs, ps):
        y = xs[0] * jax.nn.silu(xs[1])
        width = y.shape[-1] // n_groups
        lane = lax.broadcasted_iota(jnp.int32, y.shape, 1)
        scale = jnp.zeros_like(y)
        for k in range(n_groups):
            m = ((lane >= k * width) & (lane < (k + 1) * width)).astype(f32)
            ms = jnp.sum(y * y * m, axis=-1, keepdims=True) / width
            scale = scale + lax.rsqrt(ms + RMS_EPS) * m
        return [y * scale * ps[0]]
    return f


def _f_s5post(xs, ps):
    return [jax.nn.gelu(xs[0] + ps[0] * xs[1])]


def _f_merge(xs, ps):
    return [jax.nn.sigmoid(xs[0]) * xs[1] + jax.nn.sigmoid(xs[2]) * xs[3]]


def _swiglu_fwd(rv, pv):
    ab = rv[0].astype(f32)
    h = ab.shape[1] // 2
    return [jax.nn.silu(ab[:, :h]) * ab[:, h:]], []


def _swiglu_bwd(rv, pv):
    ab, d = rv[0].astype(f32), rv[1].astype(f32)
    h = ab.shape[1] // 2
    a, b = ab[:, :h], ab[:, h:]
    s = jax.nn.sigmoid(a)
    return [jnp.concatenate([d * b * (s * (1.0 + a * (1.0 - s))), d * (a * s)], axis=1)], []


def _glu_fwd(rv, pv):
    vg = rv[0].astype(f32)
    h = vg.shape[1] // 2
    return [vg[:, :h] * jax.nn.sigmoid(vg[:, h:])], []


def _glu_bwd(rv, pv):
    vg, d = rv[0].astype(f32), rv[1].astype(f32)
    h = vg.shape[1] // 2
    s = jax.nn.sigmoid(vg[:, h:])
    return [jnp.concatenate([d * s, d * vg[:, :h] * s * (1.0 - s)], axis=1)], []


def _loss_fn(rv, pv):
    e = rv[0].astype(f32) - rv[1].astype(f32)
    per_tok = jnp.mean(e * e, axis=-1, keepdims=True)
    part = 0.5 * jnp.sum(per_tok, axis=0, keepdims=True)
    return [e / e.shape[-1]], [jnp.broadcast_to(part, (8, LANES))]


def _add_fn(rv, pv):
    return [rv[0].astype(f32) + rv[1].astype(f32)], []


def _adamw_fn(rv, pv):
    w, g, m, v = [x.astype(f32) for x in rv]
    m = ADAM_B1 * m + (1.0 - ADAM_B1) * g
    v = ADAM_B2 * v + (1.0 - ADAM_B2) * (g * g)
    m_hat = m / (1.0 - ADAM_B1 ** ADAM_STEP)
    v_hat = v / (1.0 - ADAM_B2 ** ADAM_STEP)
    return [-ADAM_LR * (m_hat / (jnp.sqrt(v_hat) + ADAM_EPS) + ADAM_WD * w), m, v], []


def _adamw_layers(w, gs, m, v, name):
    n_layers, r, cols = w.shape
    br, bc = _pick(r, 256, 8), cols
    if br < 64 and cols % LANES == 0:
        br, bc = r, LANES

    def body(w_ref, *refs):
        g_refs, (m_ref, v_ref, g_out, d_ref, nm_ref, nv_ref) = refs[:n_layers], refs[n_layers:]
        layer = pl.program_id(0)
        g = g_refs[0][...]
        for k in range(1, n_layers):
            g = jnp.where(layer == k, g_refs[k][...], g)
        outs, _ = _adamw_fn([w_ref[0], g, m_ref[0], v_ref[0]], [])
        g_out[0] = g
        d_ref[0], nm_ref[0], nv_ref[0] = outs

    stacked = pl.BlockSpec((1, br, bc), lambda l, i, j: (l, i, j))
    single = pl.BlockSpec((br, bc), lambda l, i, j: (i, j))
    return pl.pallas_call(
        body, name=name, grid=(n_layers, r // br, cols // bc), in_specs=[stacked] + [single] * n_layers + [stacked] * 2,
        out_specs=[stacked] * 4, out_shape=[jax.ShapeDtypeStruct(w.shape, f32)] * 4,
        compiler_params=_params(("parallel", "parallel", "parallel")),
    )(w, *gs, m, v)


def _adamw(w, g, m, v, name):
    shape = w.shape
    cols = shape[-1] if (w.ndim >= 2 and shape[-1] >= LANES) else None
    if cols is None:
        n = int(np.prod(shape))
        cols = LANES if n % LANES == 0 else n
    n_rows = int(np.prod(shape)) // cols
    br, bc = _pick(n_rows, 256, 8), cols
    if br < 64 and cols % LANES == 0:
        br, bc = n_rows, LANES

    def body(w_ref, g_ref, m_ref, v_ref, d_ref, nm_ref, nv_ref):
        outs, _ = _adamw_fn([w_ref[...], g_ref[...], m_ref[...], v_ref[...]], [])
        d_ref[...], nm_ref[...], nv_ref[...] = outs

    spec = pl.BlockSpec((br, bc), lambda i, j: (i, j))
    outs = pl.pallas_call(
        body, name=name, grid=(n_rows // br, cols // bc), in_specs=[spec] * 4, out_specs=[spec] * 3,
        out_shape=[jax.ShapeDtypeStruct((n_rows, cols), f32)] * 3, compiler_params=_params(("parallel", "parallel")),
    )(*[t.reshape(n_rows, cols) for t in (w, g, m, v)])
    return [o.reshape(shape) for o in outs]


def _shift_down(x, s, row):
    if s == 0:
        return x
    return jnp.where(row >= s, pltpu.roll(x, s, 0), 0.0)


def _shift_up(x, s, row):
    if s == 0:
        return x
    n = x.shape[0]
    return jnp.where(row < n - s, pltpu.roll(x, n - s, 0), 0.0)


def _conv_pre(x, w, b, row):
    kw = w.shape[0]
    c = b
    for k in range(kw):
        c = c + w[k:k + 1, :] * _shift_down(x, kw - 1 - k, row)
    return c


def _conv_fwd(xsrc, col0, w, b, name, bc_t=512):
    T = xsrc.shape[0]
    kw, ncols = w.shape
    bc = _pick(ncols, bc_t)
    off = col0 // bc
    assert col0 % bc == 0

    def body(x_ref, w_ref, b_ref, o_ref):
        x = x_ref[...].astype(f32)
        row = lax.broadcasted_iota(jnp.int32, x.shape, 0)
        c = _conv_pre(x, w_ref[...], b_ref[...], row)
        o_ref[...] = c * jax.nn.sigmoid(c)

    return pl.pallas_call(
        body, name=name, grid=(ncols // bc,),
        in_specs=[pl.BlockSpec((T, bc), lambda j: (0, off + j)), pl.BlockSpec((kw, bc), lambda j: (0, j)),
                  pl.BlockSpec((1, bc), lambda j: (0, j))],
        out_specs=pl.BlockSpec((T, bc), lambda j: (0, j)), out_shape=jax.ShapeDtypeStruct((T, ncols), f32),
        compiler_params=_params(("parallel",)),
    )(xsrc, w, b)


def _conv_bwd(xsrc, col0, w, b, dact, name, bc_t=512):
    T = xsrc.shape[0]
    kw, ncols = w.shape
    bc = _pick(ncols, bc_t)
    off = col0 // bc
    assert col0 % bc == 0

    def body(x_ref, w_ref, b_ref, d_ref, dx_ref, dw_ref, db_ref):
        x = x_ref[...].astype(f32)
        w = w_ref[...]
        row = lax.broadcasted_iota(jnp.int32, x.shape, 0)
        c = _conv_pre(x, w, b_ref[...], row)
        s = jax.nn.sigmoid(c)
        dc = d_ref[...].astype(f32) * (s * (1.0 + c * (1.0 - s)))
        dx = jnp.zeros_like(x)
        dws = []
        for k in range(kw):
            dx = dx + w[k:k + 1, :] * _shift_up(dc, kw - 1 - k, row)
            dws.append(jnp.sum(dc * _shift_down(x, kw - 1 - k, row), axis=0, keepdims=True))
        dx_ref[...] = dx.astype(dx_ref.dtype)
        dw_ref[...] = jnp.concatenate(dws, axis=0)
        db_ref[...] = jnp.sum(dc, axis=0, keepdims=True)

    return pl.pallas_call(
        body, name=name, grid=(ncols // bc,),
        in_specs=[pl.BlockSpec((T, bc), lambda j: (0, off + j)), pl.BlockSpec((kw, bc), lambda j: (0, j)),
                  pl.BlockSpec((1, bc), lambda j: (0, j)), pl.BlockSpec((T, bc), lambda j: (0, j))],
        out_specs=[pl.BlockSpec((T, bc), lambda j: (0, j)), pl.BlockSpec((kw, bc), lambda j: (0, j)),
                   pl.BlockSpec((1, bc), lambda j: (0, j))],
        out_shape=[jax.ShapeDtypeStruct((T, ncols), bf16), jax.ShapeDtypeStruct((kw, ncols), f32),
                   jax.ShapeDtypeStruct((1, ncols), f32)],
        compiler_params=_params(("parallel",)),
    )(xsrc, w, b, dact)


_HI = lax.Precision.HIGHEST


def _dot(a, b, dims="nn", precision=None):
    return lax.dot_general(a, b, _DIMS[dims], preferred_element_type=f32, precision=precision)


def _dot01(a, b, dims="nn", ones="b"):
    x = a if ones == "b" else b
    hi = x.astype(bf16)
    rest = x - hi.astype(f32)
    mid = rest.astype(bf16)
    lo = (rest - mid.astype(f32)).astype(bf16)
    if ones == "b":
        e = b.astype(bf16)
        return _dot(hi, e, dims) + _dot(mid, e, dims) + _dot(lo, e, dims)
    e = a.astype(bf16)
    return _dot(e, hi, dims) + _dot(e, mid, dims) + _dot(e, lo, dims)


def _ssd_common(x_ref, b_ref, c_ref, dt_ref, adt_ref, d_ref, hpg, p):
    q = b_ref.shape[0]
    hp = hpg * p
    bb, cb = b_ref[...].astype(bf16), c_ref[...].astype(bf16)
    r = lax.broadcasted_iota(jnp.int32, (q, q), 0)
    s = lax.broadcasted_iota(jnp.int32, (q, q), 1)
    tril = r >= s
    trilf = tril.astype(f32)
    eh = lax.broadcasted_iota(jnp.int32, (LANES, hp), 0)
    ec = lax.broadcasted_iota(jnp.int32, (LANES, hp), 1)
    expand = ((ec >= eh * p) & (ec < (eh + 1) * p)).astype(f32)
    adt = adt_ref[...]
    cum = _dot01(trilf, adt, "nn", "a")
    cum_t = _dot01(adt, (r <= s).astype(f32), "tn")
    cum_e = _dot01(cum, expand)
    dt_e = _dot01(dt_ref[...], expand)
    d_e = _dot01(jnp.broadcast_to(d_ref[...], (8, LANES)), expand)[0:1, :]
    gmat = _dot(cb, bb, "nt")
    x = x_ref[...]
    xdt = x * dt_e
    e_all = jnp.exp(cum_e)
    dec = jnp.exp(cum_e[q - 1:q, :] - cum_e)
    lms, ms = [], []
    for h in range(hpg):
        lm = jnp.exp(jnp.where(tril, cum[:, h:h + 1] - cum_t[h:h + 1, :], -1e30))
        lms.append(lm)
        ms.append(gmat * lm)
    et = [jnp.exp(cum[q - 1:q, h:h + 1]) for h in range(hpg)]
    return dict(bb=bb, cb=cb, trilf=trilf, expand=expand, cum=cum, x=x, xdt=xdt, dt_e=dt_e, d_e=d_e, e=e_all, dec=dec,
                lms=lms, ms=ms, et=et)


def _ssd_specs(q, hp, n, g_n, nc, rev):
    def cidx(c):
        return (nc - 1 - c) if rev else c
    x_spec = pl.BlockSpec((q, hp), lambda g, c: (cidx(c), g))
    boff = (g_n * hp) // n
    b_spec = pl.BlockSpec((q, n), lambda g, c: (cidx(c), boff + g))
    c_spec = pl.BlockSpec((q, n), lambda g, c: (cidx(c), boff + g_n + g))
    dt_spec = pl.BlockSpec((q, LANES), lambda g, c: (cidx(c), g))
    d_spec = pl.BlockSpec((1, LANES), lambda g, c: (0, g))
    st_spec = pl.BlockSpec((1, 1, hp, n), lambda g, c: (cidx(c), g, 0, 0))
    return x_spec, b_spec, c_spec, dt_spec, d_spec, st_spec


def _ssd_fwd(act, dt, adt, dpad, hpg, p, n, name):
    T = act.shape[0]
    g_n, q = SSD_N_GROUPS, SSD_CHUNK
    nc, hp = T // q, hpg * p
    x_spec, b_spec, c_spec, dt_spec, d_spec, st_spec = _ssd_specs(q, hp, n, g_n, nc, False)

    def body(x_ref, b_ref, c_ref, dt_ref, adt_ref, d_ref, y_ref, st_ref, s_scr):
        @pl.when(pl.program_id(1) == 0)
        def _():
            s_scr[...] = jnp.zeros_like(s_scr)

        k = _ssd_common(x_ref, b_ref, c_ref, dt_ref, adt_ref, d_ref, hpg, p)
        s0 = s_scr[...]
        st_ref[0, 0] = s0
        xdtb = k["xdt"].astype(bf16)
        ydiag = [_dot(k["ms"][h].astype(bf16), xdtb[:, h * p:(h + 1) * p]) for h in range(hpg)]
        z = _dot(k["cb"], s0.astype(bf16), "nt")
        y_ref[...] = jnp.concatenate(ydiag, axis=1) + k["e"] * z + k["d_e"] * k["x"]
        upd = _dot((k["xdt"] * k["dec"]).astype(bf16), k["bb"], "tn")
        for h in range(hpg):
            s_scr[h * p:(h + 1) * p, :] = k["et"][h] * s0[h * p:(h + 1) * p, :] + upd[h * p:(h + 1) * p, :]

    return pl.pallas_call(
        body, name=name, grid=(g_n, nc),
        in_specs=[x_spec, b_spec, c_spec, dt_spec, dt_spec, d_spec],
        out_specs=[pl.BlockSpec((q, hp), lambda g, c: (c, g)), st_spec],
        out_shape=[jax.ShapeDtypeStruct((T, g_n * hp), f32), jax.ShapeDtypeStruct((nc, g_n, hp, n), f32)],
        scratch_shapes=[pltpu.VMEM((hp, n), f32)],
        compiler_params=_params(("parallel", "arbitrary")),
    )(act, act, act, dt, adt, dpad)


def _ssd_bwd(act, dt, adt, dpad, states, dy, hpg, p, n, name):
    T = act.shape[0]
    g_n, q = SSD_N_GROUPS, SSD_CHUNK
    nc, hp = T // q, hpg * p
    x_spec, b_spec, c_spec, dt_spec, d_spec, st_spec = _ssd_specs(q, hp, n, g_n, nc, True)

    def body(x_ref, b_ref, c_ref, dt_ref, adt_ref, d_ref, st_ref, dy_ref,
             dx_ref, db_ref, dc_ref, ddt_ref, dadt_ref, dd_ref, ds_scr):
        first = pl.program_id(1) == 0

        @pl.when(first)
        def _():
            ds_scr[...] = jnp.zeros_like(ds_scr)

        k = _ssd_common(x_ref, b_ref, c_ref, dt_ref, adt_ref, d_ref, hpg, p)
        bb, cb, expand, x, xdt, dec = k["bb"], k["cb"], k["expand"], k["x"], k["xdt"], k["dec"]
        heads = lambda t: _dot01(t, expand, "nt")
        s0 = st_ref[0, 0]
        ds1 = ds_scr[...]
        s0b, ds1b = s0.astype(bf16), ds1.astype(bf16)
        dy = dy_ref[...]
        dyb, xdtb = dy.astype(bf16), xdt.astype(bf16)
        lane = lax.broadcasted_iota(jnp.int32, (1, LANES), 1)
        dg = jnp.zeros((q, q), f32)
        w_rows = jnp.zeros((q, LANES), f32)
        w_cols, dxdt_parts = [], []
        for h in range(hpg):
            hs = slice(h * p, (h + 1) * p)
            dm = _dot(dyb[:, hs], xdtb[:, hs], "nt")
            dg = dg + dm * k["lms"][h]
            wm = dm * k["ms"][h]
            w_rows = w_rows + jnp.sum(wm, axis=1, keepdims=True) * (lane == h).astype(f32)
            w_cols.append(jnp.sum(wm, axis=0, keepdims=True))
            dxdt_parts.append(_dot(k["ms"][h].astype(bf16), dyb[:, hs], "tn"))
        dxdt_diag = jnp.concatenate(dxdt_parts, axis=1)
        w_cols = jnp.concatenate(w_cols + [jnp.zeros((LANES - hpg, q), f32)], axis=0).T
        dgb = dg.astype(bf16)
        z = _dot(cb, s0b, "nt")
        dz = dy * k["e"]
        dzb = dz.astype(bf16)
        dxd = _dot(bb, ds1b, "nt")
        ddec = dxd * xdt * dec
        db_ref[...] = _dot(dgb, cb, "tn") + _dot((xdt * dec).astype(bf16), ds1b)
        dc_ref[...] = _dot(dgb, bb) + _dot(dzb, s0b)
        ds0 = _dot(dzb, cb, "tn")
        for h in range(hpg):
            hs = slice(h * p, (h + 1) * p)
            ds_scr[hs, :] = ds0[hs, :] + k["et"][h] * ds1[hs, :]
        dxdt = dxdt_diag + dxd * dec
        ddec_h = heads(ddec)
        dcum = w_rows - w_cols + heads(dz * z) - ddec_h
        et_row = jnp.exp(k["cum"][q - 1:q, :])
        dsum = _dot01(jnp.ones((8, n), f32), _dot01(expand, ds1 * s0, "nn", "a"), "nt", "a")[0:1, :]
        dcl = dsum * et_row + jnp.sum(ddec_h, axis=0, keepdims=True)
        rowq = lax.broadcasted_iota(jnp.int32, (q, 1), 0)
        dcum = dcum + (rowq == q - 1).astype(f32) * dcl
        ddt_ref[...] = heads(dxdt * x)
        dadt_ref[...] = _dot01(k["trilf"], dcum, "tn", "a")
        dx_ref[...] = k["d_e"] * dy + dxdt * k["dt_e"]
        dd8 = heads(jnp.broadcast_to(jnp.sum(dy * x, axis=0, keepdims=True), (8, hp)))

        @pl.when(first)
        def _():
            dd_ref[...] = dd8

        @pl.when(jnp.logical_not(first))
        def _():
            dd_ref[...] += dd8

    rc = lambda g, c: (nc - 1 - c, g)
    return pl.pallas_call(
        body, name=name, grid=(g_n, nc),
        in_specs=[x_spec, b_spec, c_spec, dt_spec, dt_spec, d_spec, st_spec, pl.BlockSpec((q, hp), rc)],
        out_specs=[pl.BlockSpec((q, hp), rc), pl.BlockSpec((q, n), rc), pl.BlockSpec((q, n), rc),
                   pl.BlockSpec((q, LANES), rc), pl.BlockSpec((q, LANES), rc), pl.BlockSpec((8, LANES), lambda g, c: (g, 0))],
        out_shape=[jax.ShapeDtypeStruct((T, g_n * hp), f32), jax.ShapeDtypeStruct((T, g_n * n), f32),
                   jax.ShapeDtypeStruct((T, g_n * n), f32), jax.ShapeDtypeStruct((T, g_n * LANES), f32),
                   jax.ShapeDtypeStruct((T, g_n * LANES), f32), jax.ShapeDtypeStruct((g_n * 8, LANES), f32)],
        scratch_shapes=[pltpu.VMEM((hp, n), f32)],
        compiler_params=_params(("parallel", "arbitrary")),
    )(act, act, act, dt, adt, dpad, states, dy)


def _cmul(ar, ai, br, bi):
    return ar * br - ai * bi, ar * bi + ai * br


def _s5_tile_powers(lr, li):
    p = [(lr, li)]
    for _ in range(7):
        p.append(_cmul(p[-1][0], p[-1][1], lr, li))
    tile = (jnp.concatenate([q[0] for q in p], axis=0), jnp.concatenate([q[1] for q in p], axis=0))
    return tile, (p[0], p[1], p[3])


def _s5_tile_scan(xr, xi, steps, reverse):
    row = lax.broadcasted_iota(jnp.int32, xr.shape, 0)
    for d, (pr, pi) in zip((1, 2, 4), steps):
        if reverse:
            keep = row < 8 - d
            sr, si = pltpu.roll(xr, 8 - d, 0), pltpu.roll(xi, 8 - d, 0)
        else:
            keep = row >= d
            sr, si = pltpu.roll(xr, d, 0), pltpu.roll(xi, d, 0)
        sr, si = jnp.where(keep, sr, 0.0), jnp.where(keep, si, 0.0)
        ar, ai = _cmul(sr, si, pr, pi)
        xr, xi = xr + ar, xi + ai
    return xr, xi


def _s5_scan_fwd(bu, lam_re, lam_im, nsb, name, tc_t=512):
    T = bu.shape[0]
    w2 = bu.shape[1] // nsb
    w = w2 // 2
    tc = _pick(T, tc_t, 8)

    def body(bu_ref, lr_ref, li_ref, st_ref, carry):
        @pl.when(pl.program_id(1) == 0)
        def _():
            carry[...] = jnp.zeros_like(carry)

        (pr8, pi8), steps = _s5_tile_powers(lr_ref[0:1, :], li_ref[0:1, :])

        def tile(i, c):
            r = pl.ds(pl.multiple_of(i * 8, 8), 8)
            x = bu_ref[r, :]
            xr, xi = _s5_tile_scan(x[:, :w], x[:, w:], steps, False)
            ar, ai = _cmul(pr8, pi8, c[0], c[1])
            xr, xi = xr + ar, xi + ai
            st_ref[r, :] = jnp.concatenate([xr, xi], axis=1)
            return xr[7:8, :], xi[7:8, :]

        c = lax.fori_loop(0, tc // 8, tile, (carry[0:1, :], carry[1:2, :]), unroll=2)
        carry[0:1, :] = c[0]
        carry[1:2, :] = c[1]

    return pl.pallas_call(
        body, name=name, grid=(nsb, T // tc),
        in_specs=[pl.BlockSpec((tc, w2), lambda j, i: (i, j)), pl.BlockSpec((8, w), lambda j, i: (j, 0)),
                  pl.BlockSpec((8, w), lambda j, i: (j, 0))],
        out_specs=pl.BlockSpec((tc, w2), lambda j, i: (i, j)), out_shape=jax.ShapeDtypeStruct(bu.shape, f32),
        scratch_shapes=[pltpu.VMEM((8, w), f32)],
        compiler_params=_params(("parallel", "arbitrary")),
    )(bu, lam_re, lam_im)


def _s5_scan_bwd(gst, states, lam_re, lam_im, nsb, name, tc_t=512):
    T = gst.shape[0]
    w2 = gst.shape[1] // nsb
    w = w2 // 2
    tc = _pick(T, tc_t, 8)
    nt = T // tc
    n_tiles = tc // 8

    def body(g_ref, s_ref, sp_ref, lr_ref, li_ref, a_ref, dlr_ref, dli_ref, carry, acc):
        chunk = pl.program_id(1)

        @pl.when(chunk == 0)
        def _():
            carry[...] = jnp.zeros_like(carry)
            acc[...] = jnp.zeros_like(acc)

        (qr8, qi8), steps = _s5_tile_powers(lr_ref[0:1, :], -li_ref[0:1, :])
        row = lax.broadcasted_iota(jnp.int32, (8, w), 0)
        rev_r, rev_i = jnp.zeros((8, w), f32), jnp.zeros((8, w), f32)
        for r in range(8):
            rev_r = jnp.where(row == r, qr8[7 - r:8 - r, :], rev_r)
            rev_i = jnp.where(row == r, qi8[7 - r:8 - r, :], rev_i)
        row2 = lax.broadcasted_iota(jnp.int32, (8, w2), 0)

        def tile(k, c):
            ar_in, ai_in, dr, di = c
            i = n_tiles - 1 - k
            r = pl.ds(pl.multiple_of(i * 8, 8), 8)
            x = g_ref[r, :]
            xr, xi = _s5_tile_scan(x[:, :w], x[:, w:], steps, True)
            pr, pi = _cmul(rev_r, rev_i, ar_in, ai_in)
            xr, xi = xr + pr, xi + pi
            a_ref[r, :] = jnp.concatenate([xr, xi], axis=1)
            before = jnp.where(i > 0, s_ref[pl.ds(pl.multiple_of(jnp.maximum(i - 1, 0) * 8, 8), 8), :],
                               sp_ref[tc - 8:tc, :] * (chunk < nt - 1).astype(f32))
            prev = jnp.where(row2 == 0, pltpu.roll(before, 1, 0), pltpu.roll(s_ref[r, :], 1, 0))
            spr, spi = prev[:, :w], prev[:, w:]
            return xr[0:1, :], xi[0:1, :], dr + xr * spr + xi * spi, di - xr * spi + xi * spr

        c0 = (carry[0:1, :], carry[1:2, :], acc[0:8, :], acc[8:16, :])
        ar, ai, dr, di = lax.fori_loop(0, n_tiles, tile, c0, unroll=2)
        carry[0:1, :] = ar
        carry[1:2, :] = ai
        acc[0:8, :] = dr
        acc[8:16, :] = di
        dlr_ref[...] = jnp.broadcast_to(jnp.sum(dr, axis=0, keepdims=True), (8, w))
        dli_ref[...] = jnp.broadcast_to(jnp.sum(di, axis=0, keepdims=True), (8, w))

    cur = lambda j, i: (nt - 1 - i, j)
    prv = lambda j, i: (jnp.maximum(nt - 2 - i, 0), j)
    return pl.pallas_call(
        body, name=name, grid=(nsb, nt),
        in_specs=[pl.BlockSpec((tc, w2), cur), pl.BlockSpec((tc, w2), cur), pl.BlockSpec((tc, w2), prv),
                  pl.BlockSpec((8, w), lambda j, i: (j, 0)), pl.BlockSpec((8, w), lambda j, i: (j, 0))],
        out_specs=[pl.BlockSpec((tc, w2), cur), pl.BlockSpec((8, w), lambda j, i: (j, 0)),
                   pl.BlockSpec((8, w), lambda j, i: (j, 0))],
        out_shape=[jax.ShapeDtypeStruct(gst.shape, f32), jax.ShapeDtypeStruct((nsb * 8, w), f32),
                   jax.ShapeDtypeStruct((nsb * 8, w), f32)],
        scratch_shapes=[pltpu.VMEM((8, w), f32), pltpu.VMEM((16, w), f32)],
        compiler_params=_params(("parallel", "arbitrary")),
    )(gst, states, states, lam_re, lam_im)


def _s5_prep_fn(xs, ps):
    lam_re, lam_im, log_step, b_re, b_im, expand = ps
    lr = jnp.minimum(lam_re, S5_MAX_REAL)
    li = lam_im
    step = jnp.exp(log_step)
    er = jnp.exp(lr * step)
    ang = li * step
    lbr, lbi = er * jnp.cos(ang), er * jnp.sin(ang)
    nr, ni = lbr - 1.0, lbi
    den = lr * lr + li * li
    qr, qi = (nr * lr + ni * li) / den, (ni * lr - nr * li) / den
    qre, qie = _dot(qr, expand, "nn", _HI), _dot(qi, expand, "nn", _HI)
    return [lbr, lbi, qre * b_re - qie * b_im, qre * b_im + qie * b_re]


def _s5_prep(pars, name):
    def body(*refs):
        outs = _s5_prep_fn([], [r[...] for r in refs[:6]])
        for ref, v in zip(refs[6:], outs):
            ref[...] = v

    g, nst = pars[0].shape
    nc = pars[3].shape[1]
    return pl.pallas_call(
        body, name=name,
        out_shape=[jax.ShapeDtypeStruct((g, nst), f32)] * 2 + [jax.ShapeDtypeStruct((g, nc), f32)] * 2,
        compiler_params=_params(),
    )(*pars)


def _s5_prep_bwd(pars, cots, name):
    def body(*refs):
        ps = [r[...] for r in refs[:6]]
        ct = [r[...] for r in refs[6:10]]
        _, vjp = jax.vjp(lambda *a: _s5_prep_fn([], list(a)), *ps)
        g = vjp(ct)
        for ref, v in zip(refs[10:], g[:5]):
            ref[...] = v

    return pl.pallas_call(
        body, name=name, out_shape=[jax.ShapeDtypeStruct(p.shape, f32) for p in pars[:5]], compiler_params=_params(),
    )(*pars, *cots)


_ANY = pl.BlockSpec(memory_space=pl.ANY)


def _remote(src, dst, send_sem, recv_sem, device):
    return pltpu.make_async_remote_copy(src_ref=src, dst_ref=dst, send_sem=send_sem, recv_sem=recv_sem, device_id=device,
                                        device_id_type=MESH)


def _staged_copy(src, dst, buf, in_sems, out_sems):
    n = D2D_STREAMS
    piece = src.shape[0] // n
    assert src.shape[0] % n == 0

    def load(i):
        return pltpu.make_async_copy(src.at[pl.ds(i * piece, piece)], buf.at[i % 2], in_sems.at[i % 2])

    def store(i):
        return pltpu.make_async_copy(buf.at[i % 2], dst.at[pl.ds(i * piece, piece)], out_sems.at[i % 2])

    load(0).start()
    for i in range(n):
        if i + 1 < n:
            if i >= 1:
                store(i - 1).wait()
            load(i + 1).start()
        load(i).wait()
        store(i).start()
    store(n - 2).wait()
    store(n - 1).wait()


def _stage_scratch(rows, cols, dtype):
    return [pltpu.VMEM((2, rows // D2D_STREAMS, cols), dtype), pltpu.SemaphoreType.DMA((2,)), pltpu.SemaphoreType.DMA((2,))]


def _chip_all_gather(block, name):
    rows = block.shape[0]
    half = rows // 2
    piece = half // D2D_STREAMS
    assert rows % (2 * D2D_STREAMS * 16) == 0

    def body(src, out, ici_send, ici_recv, d2d_send, d2d_recv, *stage):
        x, y, c = lax.axis_index("x"), lax.axis_index("y"), lax.axis_index("c")
        me = 2 * x + y
        sibling = (x, y, 1 - c)
        chips = [(1 - x, y), (x, 1 - y), (1 - x, 1 - y)]
        mine = pl.ds(pl.multiple_of(c * half, 16), half)
        sends = []
        for j, (px, py) in enumerate(chips):
            cp = _remote(src.at[mine], out.at[me, mine], ici_send.at[j], ici_recv.at[j], (px, py, c))
            cp.start()
            sends.append(cp)
        _staged_copy(src, out.at[me], *stage)
        for j, (px, py) in enumerate(chips):
            slot = 2 * px + py
            _remote(src.at[mine], out.at[slot, mine], ici_send.at[j], ici_recv.at[j], (px, py, c)).wait_recv()
            for s in range(D2D_STREAMS):
                r = pl.ds(pl.multiple_of(c * half + s * piece, 16), piece)
                k = j * D2D_STREAMS + s
                cp = _remote(out.at[slot, r], out.at[slot, r], d2d_send.at[k], d2d_recv.at[k], sibling)
                cp.start()
                sends.append(cp)
        for j, (px, py) in enumerate(chips):
            slot = 2 * px + py
            for s in range(D2D_STREAMS):
                r = pl.ds(pl.multiple_of((1 - c) * half + s * piece, 16), piece)
                k = j * D2D_STREAMS + s
                _remote(out.at[slot, r], out.at[slot, r], d2d_send.at[k], d2d_recv.at[k], sibling).wait_recv()
        for cp in sends:
            cp.wait_send()

    n_d2d = 3 * D2D_STREAMS
    return pl.pallas_call(
        body, name=name, in_specs=[_ANY], out_specs=_ANY,
        out_shape=jax.ShapeDtypeStruct((N_CHIPS,) + block.shape, block.dtype),
        scratch_shapes=[pltpu.SemaphoreType.DMA((3,)), pltpu.SemaphoreType.DMA((3,)), pltpu.SemaphoreType.DMA((n_d2d,)),
                        pltpu.SemaphoreType.DMA((n_d2d,))] + _stage_scratch(rows, block.shape[1], block.dtype),
    )(block)


def _chip_scatter(parts, name):
    def body(src, out, send_sems, recv_sems, *stage):
        x, y, c = lax.axis_index("x"), lax.axis_index("y"), lax.axis_index("c")
        me = 2 * x + y
        chips = [(1 - x, y), (x, 1 - y), (1 - x, 1 - y)]
        sends = []
        for j, (px, py) in enumerate(chips):
            cp = pltpu.make_async_remote_copy(src_ref=src.at[2 * px + py], dst_ref=out.at[me], send_sem=send_sems.at[j],
                                              recv_sem=recv_sems.at[j], device_id=(px, py, c), device_id_type=MESH)
            cp.start()
            sends.append(cp)
        _staged_copy(src.at[me], out.at[me], *stage)
        for j, (px, py) in enumerate(chips):
            pltpu.make_async_remote_copy(src_ref=src.at[me], dst_ref=out.at[2 * px + py], send_sem=send_sems.at[j],
                                         recv_sem=recv_sems.at[j], device_id=(px, py, c), device_id_type=MESH).wait_recv()
        for cp in sends:
            cp.wait_send()

    return pl.pallas_call(
        body, name=name, in_specs=[_ANY], out_specs=_ANY, out_shape=jax.ShapeDtypeStruct(parts.shape, parts.dtype),
        scratch_shapes=[pltpu.SemaphoreType.DMA((3,)), pltpu.SemaphoreType.DMA((3,))]
        + _stage_scratch(parts.shape[1], parts.shape[2], parts.dtype),
    )(parts)


_HBM = pl.BlockSpec(memory_space=pltpu.HBM)
_SEM = pl.BlockSpec(memory_space=pltpu.SEMAPHORE)
_EFFECT = pltpu.SideEffectType.DATAFLOW_SIDE_EFFECTING


def _gather_peers():
    x, y, c = lax.axis_index("x"), lax.axis_index("y"), lax.axis_index("c")
    return x, y, c, 2 * x + y, [(1 - x, y), (x, 1 - y), (1 - x, 1 - y)]


def _chip_gather_start(block, name):
    half = block.shape[0] // 2

    def body(src, land, send_sems, recv_sems, src_out, land_out, token):
        x, y, c, me, chips = _gather_peers()
        mine = pl.ds(pl.multiple_of(c * half, 16), half)
        for j, (px, py) in enumerate(chips):
            _remote(src.at[mine], land.at[me, mine], send_sems.at[j], recv_sems.at[j], (px, py, c)).start()
        token[...] = jnp.zeros_like(token)

    land_shape = (N_CHIPS,) + block.shape
    return pl.pallas_call(
        body, name=name,
        out_shape=(pltpu.SemaphoreType.DMA((3,)), pltpu.SemaphoreType.DMA((3,)), pltpu.HBM(block.shape, block.dtype),
                   pltpu.HBM(land_shape, block.dtype), jax.ShapeDtypeStruct((8, LANES), f32)),
        in_specs=(_HBM, _HBM), out_specs=(_SEM, _SEM, _HBM, _HBM, pl.BlockSpec(memory_space=pltpu.VMEM)),
        input_output_aliases={0: 2, 1: 3}, compiler_params=pltpu.CompilerParams(has_side_effects=_EFFECT),
    )(pltpu.with_memory_space_constraint(block, pltpu.HBM),
      pltpu.with_memory_space_constraint(lax.empty(land_shape, block.dtype), pltpu.HBM))


def _chip_gather_wait(send_sems, recv_sems, block, land, after, name):
    half = block.shape[0] // 2

    def body(src, land_ref, send_ref, recv_ref, after_ref, src_dead, land_out):
        x, y, c, me, chips = _gather_peers()
        mine = pl.ds(pl.multiple_of(c * half, 16), half)
        for j, (px, py) in enumerate(chips):
            cp = _remote(src.at[mine], land_ref.at[2 * px + py, mine], send_ref.at[j], recv_ref.at[j], (px, py, c))
            cp.wait_send()
            cp.wait_recv()

    return pl.pallas_call(
        body, name=name, out_shape=(pltpu.HBM(block.shape, block.dtype), pltpu.HBM(land.shape, land.dtype)),
        in_specs=(_HBM, _HBM, _SEM, _SEM, _ANY), out_specs=(_HBM, _HBM), input_output_aliases={0: 0, 1: 1},
        compiler_params=pltpu.CompilerParams(has_side_effects=_EFFECT),
    )(block, land, send_sems, recv_sems, after)


def _chip_gather_finish(block, land, name):
    rows = block.shape[0]
    half = rows // 2
    piece = half // D2D_STREAMS

    def body(src, land_ref, out, d2d_send, d2d_recv, *stage):
        x, y, c, me, chips = _gather_peers()
        sibling = (x, y, 1 - c)
        sends = []
        for j, (px, py) in enumerate(chips):
            slot = 2 * px + py
            for s in range(D2D_STREAMS):
                r = pl.ds(pl.multiple_of(c * half + s * piece, 16), piece)
                k = j * D2D_STREAMS + s
                cp = _remote(land_ref.at[slot, r], out.at[slot, r], d2d_send.at[k], d2d_recv.at[k], sibling)
                cp.start()
                sends.append(cp)
        _staged_copy(src, out.at[me], *stage)
        for j, (px, py) in enumerate(chips):
            slot = 2 * px + py
            for s in range(D2D_STREAMS):
                r = pl.ds(pl.multiple_of((1 - c) * half + s * piece, 16), piece)
                k = j * D2D_STREAMS + s
                _remote(land_ref.at[slot, r], out.at[slot, r], d2d_send.at[k], d2d_recv.at[k], sibling).wait_recv()
        for cp in sends:
            cp.wait_send()

    n_d2d = 3 * D2D_STREAMS
    return pl.pallas_call(
        body, name=name, in_specs=[_ANY, _ANY], out_specs=_ANY, out_shape=jax.ShapeDtypeStruct(land.shape, land.dtype),
        input_output_aliases={1: 0},
        scratch_shapes=[pltpu.SemaphoreType.DMA((n_d2d,)), pltpu.SemaphoreType.DMA((n_d2d,))]
        + _stage_scratch(rows, block.shape[1], block.dtype),
    )(block, land)


def _chip_scatter_start(parts, name):
    def body(src, land, send_sems, recv_sems, src_out, land_out, token):
        x, y, c = lax.axis_index("x"), lax.axis_index("y"), lax.axis_index("c")
        me = 2 * x + y
        for j, (px, py) in enumerate([(1 - x, y), (x, 1 - y), (1 - x, 1 - y)]):
            _remote(src.at[2 * px + py], land.at[me], send_sems.at[j], recv_sems.at[j], (px, py, c)).start()
        token[...] = jnp.zeros_like(token)

    return pl.pallas_call(
        body, name=name,
        out_shape=(pltpu.SemaphoreType.DMA((3,)), pltpu.SemaphoreType.DMA((3,)), pltpu.HBM(parts.shape, parts.dtype),
                   pltpu.HBM(parts.shape, parts.dtype), jax.ShapeDtypeStruct((8, LANES), f32)),
        in_specs=(_HBM, _HBM), out_specs=(_SEM, _SEM, _HBM, _HBM, pl.BlockSpec(memory_space=pltpu.VMEM)),
        input_output_aliases={0: 2, 1: 3}, compiler_params=pltpu.CompilerParams(has_side_effects=_EFFECT),
    )(pltpu.with_memory_space_constraint(parts, pltpu.HBM),
      pltpu.with_memory_space_constraint(lax.empty(parts.shape, parts.dtype), pltpu.HBM))


def _chip_scatter_wait(send_sems, recv_sems, parts, land, after, name):
    def body(src, land_ref, send_ref, recv_ref, after_ref, src_dead, land_out):
        x, y, c = lax.axis_index("x"), lax.axis_index("y"), lax.axis_index("c")
        me = 2 * x + y
        for j, (px, py) in enumerate([(1 - x, y), (x, 1 - y), (1 - x, 1 - y)]):
            cp = _remote(src.at[2 * px + py], land_ref.at[2 * px + py], send_ref.at[j], recv_ref.at[j], (px, py, c))
            cp.wait_send()
            cp.wait_recv()

    return pl.pallas_call(
        body, name=name, out_shape=(pltpu.HBM(parts.shape, parts.dtype), pltpu.HBM(land.shape, land.dtype)),
        in_specs=(_HBM, _HBM, _SEM, _SEM, _ANY), out_specs=(_HBM, _HBM), input_output_aliases={0: 0, 1: 1},
        compiler_params=pltpu.CompilerParams(has_side_effects=_EFFECT),
    )(parts, land, send_sems, recv_sems, after)


def _sum_slots_own(landed, own, chip, name, block_rows=512):
    s_n, r_n, c_n = landed.shape
    br = _pick(r_n, block_rows, 8)

    def body(chip_ref, land_ref, own_ref, o_ref):
        acc = jnp.zeros((br, c_n), f32)
        for s in range(s_n):
            acc = acc + jnp.where(chip_ref[0] == s, own_ref[s], land_ref[s]).astype(f32)
        o_ref[...] = acc

    spec = pl.BlockSpec((s_n, br, c_n), lambda i, c: (0, i, 0))
    grid_spec = pltpu.PrefetchScalarGridSpec(num_scalar_prefetch=1, grid=(r_n // br,), in_specs=[spec, spec],
                                             out_specs=pl.BlockSpec((br, c_n), lambda i, c: (i, 0)))
    return pl.pallas_call(
        body, name=name, grid_spec=grid_spec, out_shape=jax.ShapeDtypeStruct((r_n, c_n), f32),
        compiler_params=_params(("parallel",)),
    )(chip, landed, own)


def _core_send_other_half(parts, name):
    n_slots, rows, cols = parts.shape
    half = rows // 2
    piece = half // D2D_STREAMS
    assert rows % (2 * D2D_STREAMS * 16) == 0

    def body(src, out, send_sems, recv_sems):
        x, y, c = lax.axis_index("x"), lax.axis_index("y"), lax.axis_index("c")
        sibling = (x, y, 1 - c)
        sends = []
        for k in range(n_slots):
            for s in range(D2D_STREAMS):
                theirs = pl.ds(pl.multiple_of((1 - c) * half + s * piece, 16), piece)
                i = k * D2D_STREAMS + s
                cp = _remote(src.at[k, theirs], out.at[k, pl.ds(s * piece, piece)], send_sems.at[i], recv_sems.at[i], sibling)
                cp.start()
                sends.append(cp)
        for cp in sends:
            cp.wait_recv()
        for cp in sends:
            cp.wait_send()

    n = n_slots * D2D_STREAMS
    return pl.pallas_call(
        body, name=name, in_specs=[_ANY], out_specs=_ANY, out_shape=jax.ShapeDtypeStruct((n_slots, half, cols), parts.dtype),
        scratch_shapes=[pltpu.SemaphoreType.DMA((n,)), pltpu.SemaphoreType.DMA((n,))],
    )(parts)


def _other_half_copies(src, land, send_sems, recv_sems):
    n_slots, rows, _ = src.shape
    half = rows // 2
    piece = half // D2D_STREAMS
    x, y, c = lax.axis_index("x"), lax.axis_index("y"), lax.axis_index("c")
    copies = []
    for k in range(n_slots):
        for s in range(D2D_STREAMS):
            theirs = pl.ds(pl.multiple_of((1 - c) * half + s * piece, 16), piece)
            i = k * D2D_STREAMS + s
            copies.append(_remote(src.at[k, theirs], land.at[k, pl.ds(s * piece, piece)], send_sems.at[i], recv_sems.at[i],
                                  (x, y, 1 - c)))
    return copies


def _core_send_other_half_start(parts, name):
    n_slots, rows, cols = parts.shape
    assert rows % (2 * D2D_STREAMS * 16) == 0
    n = n_slots * D2D_STREAMS
    land_shape = (n_slots, rows // 2, cols)

    def body(src, land, send_sems, recv_sems, src_out, land_out, token):
        for cp in _other_half_copies(src, land, send_sems, recv_sems):
            cp.start()
        token[...] = jnp.zeros_like(token)

    return pl.pallas_call(
        body, name=name,
        out_shape=(pltpu.SemaphoreType.DMA((n,)), pltpu.SemaphoreType.DMA((n,)), pltpu.HBM(parts.shape, parts.dtype),
                   pltpu.HBM(land_shape, parts.dtype), jax.ShapeDtypeStruct((8, LANES), f32)),
        in_specs=(_HBM, _HBM), out_specs=(_SEM, _SEM, _HBM, _HBM, pl.BlockSpec(memory_space=pltpu.VMEM)),
        input_output_aliases={0: 2, 1: 3}, compiler_params=pltpu.CompilerParams(has_side_effects=_EFFECT),
    )(pltpu.with_memory_space_constraint(parts, pltpu.HBM),
      pltpu.with_memory_space_constraint(lax.empty(land_shape, parts.dtype), pltpu.HBM))


def _core_send_other_half_wait(send_sems, recv_sems, parts, land, after, name):
    def body(src, land_ref, send_ref, recv_ref, after_ref, src_dead, land_out):
        for cp in _other_half_copies(src, land_ref, send_ref, recv_ref):
            cp.wait_send()
            cp.wait_recv()

    return pl.pallas_call(
        body, name=name, out_shape=(pltpu.HBM(parts.shape, parts.dtype), pltpu.HBM(land.shape, land.dtype)),
        in_specs=(_HBM, _HBM, _SEM, _SEM, _ANY), out_specs=(_HBM, _HBM), input_output_aliases={0: 0, 1: 1},
        compiler_params=pltpu.CompilerParams(has_side_effects=_EFFECT),
    )(parts, land, send_sems, recv_sems, after)


def _add_my_half(parts, other, core, name, block_rows=1024):
    n_slots, rows, cols = parts.shape
    half = rows // 2
    br = _pick(half, block_rows, 16)
    nb = half // br

    def body(c_ref, a_ref, b_ref, o_ref):
        o_ref[...] = (a_ref[...].astype(f32) + b_ref[...].astype(f32)).astype(o_ref.dtype)

    grid_spec = pltpu.PrefetchScalarGridSpec(
        num_scalar_prefetch=1, grid=(n_slots, nb),
        in_specs=[pl.BlockSpec((1, br, cols), lambda k, i, c: (k, c[0] * nb + i, 0)),
                  pl.BlockSpec((1, br, cols), lambda k, i, c: (k, i, 0))],
        out_specs=pl.BlockSpec((1, br, cols), lambda k, i, c: (k, i, 0)))
    return pl.pallas_call(
        body, name=name, grid_spec=grid_spec, out_shape=jax.ShapeDtypeStruct((n_slots, half, cols), parts.dtype),
        compiler_params=_params(("parallel", "parallel")),
    )(core, parts, other)


def _core_join_halves(mine, name):
    half, cols = mine.shape
    piece = half // D2D_STREAMS
    assert half % (D2D_STREAMS * 16) == 0

    def body(src, out, send_sems, recv_sems, *stage):
        x, y, c = lax.axis_index("x"), lax.axis_index("y"), lax.axis_index("c")
        sibling = (x, y, 1 - c)
        sends = []
        for s in range(D2D_STREAMS):
            dst = out.at[pl.ds(pl.multiple_of(c * half + s * piece, 16), piece)]
            cp = _remote(src.at[pl.ds(s * piece, piece)], dst, send_sems.at[s], recv_sems.at[s], sibling)
            cp.start()
            sends.append(cp)
        _staged_copy(src, out.at[pl.ds(pl.multiple_of(c * half, 16), half)], *stage)
        for s in range(D2D_STREAMS):
            dst = out.at[pl.ds(pl.multiple_of((1 - c) * half + s * piece, 16), piece)]
            _remote(src.at[pl.ds(s * piece, piece)], dst, send_sems.at[s], recv_sems.at[s], sibling).wait_recv()
        for cp in sends:
            cp.wait_send()

    return pl.pallas_call(
        body, name=name, in_specs=[_ANY], out_specs=_ANY, out_shape=jax.ShapeDtypeStruct((2 * half, cols), mine.dtype),
        scratch_shapes=[pltpu.SemaphoreType.DMA((D2D_STREAMS,)), pltpu.SemaphoreType.DMA((D2D_STREAMS,))]
        + _stage_scratch(half, cols, mine.dtype),
    )(mine)


def _reduce_start(exchanging, core, after, tag):
    parts, other = _core_send_other_half_wait(*exchanging[:4], after, f"exchange_wait_{tag}")
    return _chip_scatter_start(_add_my_half(parts, other, core, f"sum_core_halves_{tag}"), f"scatter_start_{tag}")


def _reduce_finish(started, chip, after, tag):
    send_sems, recv_sems, chip_part, land, _ = started
    own, landed = _chip_scatter_wait(send_sems, recv_sems, chip_part, land, after, f"scatter_wait_{tag}")
    return _core_join_halves(_sum_slots_own(landed, own, chip, f"sum_chip_parts_{tag}"), f"join_core_halves_{tag}")


def _reduce_to_chips(parts, core, tag):
    chip_part = _add_my_half(parts, _core_send_other_half(parts, f"exchange_core_halves_{tag}"), core, f"sum_core_halves_{tag}")
    my_sum = _sum_slots(_chip_scatter(chip_part, f"scatter_{tag}"), f"sum_chip_parts_{tag}")
    return _core_join_halves(my_sum, f"join_core_halves_{tag}")


def _sum_slots(stack, name, block_rows=512):
    s_n, r_n, c_n = stack.shape
    br = _pick(r_n, block_rows, 8)

    def body(in_ref, o_ref):
        acc = in_ref[0].astype(f32)
        for s in range(1, s_n):
            acc = acc + in_ref[s].astype(f32)
        o_ref[...] = acc

    return pl.pallas_call(
        body, name=name, grid=(r_n // br,), in_specs=[pl.BlockSpec((s_n, br, c_n), lambda i: (0, i, 0))],
        out_specs=pl.BlockSpec((br, c_n), lambda i: (i, 0)), out_shape=jax.ShapeDtypeStruct((r_n, c_n), f32),
        compiler_params=_params(("parallel",)),
    )(stack)


def _concat_padded(parts, mult):
    rows = sum(p.shape[0] for p in parts)
    pad = (-rows) % mult
    if pad:
        parts = parts + [jnp.zeros((pad,) + parts[0].shape[1:], parts[0].dtype)]
    return jnp.concatenate(parts, axis=0)


def _pack_weights(w, l, names, conv_w=None):
    parts = [w[n][l].astype(bf16).reshape(-1, PACK_COLS) for n in names]
    if conv_w is not None:
        parts.append(_concat_padded([lax.bitcast_convert_type(conv_w, bf16).reshape(-1, PACK_COLS)], 16))
    return _concat_padded(parts, PACK_ROW_MULT)


def _unpack_weights(full, w, l, names, conv_w=None):
    start, r0 = {}, 0
    for n in names:
        start[n] = r0
        r0 += w[n][l].size // PACK_COLS

    def shards(n):
        rows = w[n][l].size // PACK_COLS
        return [full[k, start[n]:start[n] + rows].reshape(w[n].shape[1:]) for k in range(N_CHIPS)]

    if conv_w is None:
        return shards, None
    rows = conv_w.size * 2 // PACK_COLS
    pieces = lax.bitcast_convert_type(full[:, r0:r0 + rows].reshape((N_CHIPS,) + conv_w.shape + (2,)), f32)
    return shards, jnp.concatenate([pieces[k] for k in range(N_CHIPS)], axis=2)


def _pack_big_grads(layer_grads):
    parts, slot_rows = [], 0
    for k in range(N_CHIPS):
        slot = []
        for n in BIG:
            for g in layer_grads:
                width = g[n].shape[0] // N_CHIPS
                slot.append(g[n][k * width:(k + 1) * width].astype(bf16).reshape(-1, PACK_COLS))
        slot_rows = sum(p.shape[0] for p in slot)
        pad = (-slot_rows) % PACK_ROW_MULT
        if pad:
            slot.append(jnp.zeros((pad, PACK_COLS), bf16))
        slot_rows += pad
        parts += slot
    return jnp.concatenate(parts, axis=0).reshape(N_CHIPS, slot_rows, PACK_COLS)


def _unpack_big_grads(summed, w):
    out, r0 = {}, 0
    for n in BIG:
        rows = w[n][0].size // PACK_COLS
        out[n] = [s[r0:r0 + rows].reshape(w[n].shape[1:]) for s in summed]
        r0 += rows
    return out


_SMALL_TILE = 8 * LANES


def _pack_small(vals, names):
    parts = []
    for n in names:
        pieces = vals[n] if isinstance(vals[n], list) else [vals[n]]
        size = sum(p.size for p in pieces)
        if all(p.size % _SMALL_TILE == 0 for p in pieces):
            parts += [p.reshape(-1, LANES) for p in pieces]
        else:
            flat = [p.reshape(-1) for p in pieces] + [jnp.zeros(((-size) % _SMALL_TILE,), f32)]
            parts.append(jnp.concatenate(flat).reshape(-1, LANES))
    return _concat_padded(parts, PACK_ROW_MULT)


def _unpack_small(packed, like, names):
    out, r0 = {}, 0
    for n in names:
        size = like[n].size
        rows = -(-size // _SMALL_TILE) * 8
        out[n] = packed[r0:r0 + rows].reshape(-1)[:size].reshape(like[n].shape)
        r0 += rows
    return out


def _dims(w, x):
    d = {}
    d["D"] = x.shape[-1]
    d["T"] = x.shape[-2]
    d["DI"] = w["ssd_norm_g"].shape[-1]
    d["NH"] = w["ssd_dt_bias"].shape[-1]
    d["CD"] = w["ssd_conv_b"].shape[-1]
    d["G"] = SSD_N_GROUPS
    d["HPG"] = d["NH"] // d["G"]
    d["P"] = d["DI"] // d["NH"]
    d["N"] = (d["CD"] - d["DI"]) // (2 * d["G"])
    d["S5G"], d["S5N"] = w["s5_lambda_re"].shape[-2:]
    d["S5C"] = w["s5_b_re"].shape[-1]
    d["S5W"] = d["S5G"] * d["S5C"]
    d["NSB"] = d["S5W"] // S5_SUPERBLOCK
    d["GSB"] = d["S5G"] // d["NSB"]
    return d


def _head_pad(v, d):
    lead = v.shape[:-1]
    v = v.reshape(lead + (d["G"], d["HPG"]))
    v = jnp.concatenate([v, jnp.zeros(lead + (d["G"], LANES - d["HPG"]), v.dtype)], axis=-1)
    return v.reshape(lead + (d["G"] * LANES,))


def _head_unpad(v, d):
    lead = v.shape[:-1]
    return v.reshape(lead + (d["G"], LANES))[..., :d["HPG"]].reshape(lead + (d["NH"],))


def _w_in_perm(shards, d):
    o, nh = d["DI"] + d["CD"], d["NH"]
    r = shards[0].shape[0]

    def rows(lo, hi):
        out = []
        for k, s in enumerate(shards):
            a, b = max(lo, k * r), min(hi, (k + 1) * r)
            if a < b:
                out.append(s[a - k * r:b - k * r])
        return out

    dt = _head_pad(jnp.concatenate(rows(o, o + nh), axis=0).T, d).T
    return jnp.concatenate(rows(0, o) + rows(o + nh, len(shards) * r) + [dt], axis=0)


def _w_in_unperm(g, d):
    o = d["DI"] + d["CD"]
    rest = d["S5W"] + 2 * d["D"]
    return jnp.concatenate([g[:o], _head_unpad(g[o + rest:].T, d).T, g[o:o + rest]], axis=0)


def _s5_block_diag(v, d):
    gsb = d["GSB"]
    g, a, b = v.shape
    row_group = (lax.broadcasted_iota(jnp.int32, (g * a, gsb * b), 0) // a) % gsb
    col_group = lax.broadcasted_iota(jnp.int32, (g * a, gsb * b), 1) // b
    return jnp.where(row_group == col_group, jnp.tile(v.reshape(g * a, b), (1, gsb)), 0)


def _s5_diag_blocks(m, d, a, b):
    gsb = d["GSB"]
    rows = m.shape[0]
    m = m.reshape(rows, gsb, b)
    row_group = (lax.broadcasted_iota(jnp.int32, (rows, gsb, 1), 0) // a) % gsb
    col_group = lax.broadcasted_iota(jnp.int32, (rows, gsb, 1), 1)
    return jnp.sum(jnp.where(row_group == col_group, m, 0), axis=1).reshape(rows // a, a, b)


def _s5_lam_rows(v, d):
    v = v.reshape(d["NSB"], 1, d["GSB"] * d["S5N"])
    return jnp.broadcast_to(v, (d["NSB"], 8, v.shape[-1])).reshape(d["NSB"] * 8, -1)


def _ffn_fwd(h, pre_g, post_g, wgu, wd, tag):
    D = h.shape[1]
    H2 = wgu.shape[0]
    xn = _row_kernel(f"{tag}_norm", _fwd_of(_f_norm), [(h, D, 0)], [pre_g], [(D, bf16)])[0]
    ab = _mm(xn, wgu, "nt", bf16, f"{tag}_mm_up")
    hid = _row_kernel(f"{tag}_swiglu", _swiglu_fwd, [(ab, H2, 0)], [], [(H2 // 2, bf16)])[0]
    f = _mm(hid, wd, "nn", f32, f"{tag}_mm_down")
    out = _row_kernel(f"{tag}_resnorm", _fwd_of(_f_resnorm(0.5)), [(h, D, 0), (f, D, 0)], [post_g], [(D, f32)])[0]
    return out, dict(h=h, xn=xn, ab=ab, hid=hid, f=f)


def _ffn_bwd(dh_out, s, pre_g, post_g, wgu, wd, tag):
    D = dh_out.shape[1]
    H2 = wgu.shape[0]
    df, dpost = _row_kernel(f"{tag}_resnorm_bwd", _vjp_of(_f_post(0.5), 1, 1, [0]), [(s["f"], D, 0), (dh_out, D, 0)],
                            [post_g], [(D, bf16)], [post_g.shape])
    dwd = _mm(s["hid"], df, "tn", bf16, f"{tag}_mm_dwd")
    dhid = _mm(df, wd, "nt", bf16, f"{tag}_mm_dhid")
    dab = _row_kernel(f"{tag}_swiglu_bwd", _swiglu_bwd, [(s["ab"], H2, 0), (dhid, H2 // 2, 0)], [], [(H2, bf16)])[0]
    dwgu = _mm(dab, s["xn"], "tn", bf16, f"{tag}_mm_dwgu")
    dxn = _mm(dab, wgu, "nn", f32, f"{tag}_mm_dxn")
    dh, dpre = _row_kernel(f"{tag}_norm_bwd", _vjp_of(_f_norm, 1, 1, [0], 1), [(s["h"], D, 0), (dxn, D, 0), (dh_out, D, 0)],
                           [pre_g], [(D, f32)], [pre_g.shape])
    return dh, dict(pre_g=dpre, post_g=dpost, wgu=dwgu, wd=dwd)


def _mixer_fwd(h, p, d):
    D, DI, CD, G, N = d["D"], d["DI"], d["CD"], d["G"], d["N"]
    gl = G * LANES
    c_u5, c_ga, c_gb, c_dt = DI + CD, DI + CD + d["S5W"], DI + CD + d["S5W"] + D, DI + CD + d["S5W"] + 2 * D
    u = _row_kernel("mix_norm", _fwd_of(_f_norm), [(h, D, 0)], [p["mix_pre_g"]], [(D, bf16)])[0]
    proj = _mm(u, p["w_in"], "nt", f32, "mix_mm_in", bn_t=512)
    act = _conv_fwd(proj, DI, p["conv_w"], p["conv_b"], "ssd_conv")
    dt, adt = _row_kernel("ssd_dt", _fwd_of(_f_dt), [(proj, gl, c_dt // gl)], [p["dt_bias"], p["a_log"]], [(gl, f32)] * 2)
    y_ssd, states = _ssd_fwd(act, dt, adt, p["d_skip"], d["HPG"], d["P"], N, "ssd_scan")
    nrm = _row_kernel("ssd_post", _fwd_of(_f_ssdpost(G)), [(y_ssd, DI, 0), (proj, DI, 0)], [p["norm_g"]], [(DI, bf16)])[0]
    y_a = _mm(nrm, p["w_a"], "nn", f32, "mix_mm_a")
    u5 =(proj, d["S5W"], c_u5 // d["S5W"])
    bu = _s5_in(proj, c_u5, p["bsb"], d)
    s5st = _s5_scan_fwd(bu, p["lam_re_rows"], p["lam_im_rows"], d["NSB"], "s5_scan")
    y5 = _bdmm(s5st, p["csb"], "nn", d["NSB"], f32, "s5_mm_c")
    gel = _row_kernel("s5_post", _fwd_of(_f_s5post), [(y5, d["S5W"], 0), u5], [p["s5_d"]], [(d["S5W"], bf16)])[0]
    vg = _mm(gel, p["w_glu"], "nt", bf16, "mix_mm_glu")
    glu = _row_kernel("s5_glu", _glu_fwd, [(vg, vg.shape[1], 0)], [], [(vg.shape[1] // 2, bf16)])[0]
    y_b = _mm(glu, p["w_b"], "nn", f32, "mix_mm_b")
    merged = _row_kernel("mix_merge", _fwd_of(_f_merge), [(proj, D, c_ga // D), (y_a, D, 0), (proj, D, c_gb // D), (y_b, D, 0)],
                         [], [(D, bf16)])[0]
    m = _mm(merged, p["w_out"], "nn", f32, "mix_mm_out")
    out = _row_kernel("mix_resnorm", _fwd_of(_f_resnorm(1.0)), [(h, D, 0), (m, D, 0)], [p["mix_post_g"]], [(D, f32)])[0]
    return out, dict(h=h, u=u, proj=proj, act=act, dt=dt, adt=adt, states=states, y_ssd=y_ssd, nrm=nrm, y_a=y_a, s5st=s5st,
                     y5=y5, gel=gel, vg=vg, glu=glu, y_b=y_b, merged=merged, m=m)


def _s5_in(proj, c_u5, bsb, d):
    T = proj.shape[0]
    nsb = d["NSB"]
    ka, nw = S5_SUPERBLOCK, bsb.shape[1]
    off = c_u5 // ka
    assert c_u5 % ka == 0
    bt = _pick(T, 512)

    def body(a_ref, w_ref, o_ref):
        o_ref[...] = _dot(a_ref[...].astype(bf16), w_ref[...].astype(bf16))

    return pl.pallas_call(
        body, name="s5_mm_bu", grid=(nsb, T // bt),
        in_specs=[pl.BlockSpec((bt, ka), lambda j, i: (i, off + j)), pl.BlockSpec((ka, nw), lambda j, i: (j, 0))],
        out_specs=pl.BlockSpec((bt, nw), lambda j, i: (i, j)), out_shape=jax.ShapeDtypeStruct((T, nsb * nw), f32),
        compiler_params=_params(("parallel", "parallel")),
    )(proj, bsb)


def _s5_dbsb(proj, c_u5, a, d):
    T = proj.shape[0]
    nsb = d["NSB"]
    ka, nw = S5_SUPERBLOCK, a.shape[1] // nsb
    off = c_u5 // ka
    bt = _pick(T, 512)

    def body(u_ref, a_ref, o_ref):
        pr = _dot(u_ref[...].astype(bf16), a_ref[...].astype(bf16), "tn")
        k = pl.program_id(1)

        @pl.when(k == 0)
        def _():
            o_ref[...] = pr

        @pl.when(k > 0)
        def _():
            o_ref[...] += pr

    return pl.pallas_call(
        body, name="s5_mm_dbsb", grid=(nsb, T // bt),
        in_specs=[pl.BlockSpec((bt, ka), lambda j, k: (k, off + j)), pl.BlockSpec((bt, nw), lambda j, k: (k, j))],
        out_specs=pl.BlockSpec((ka, nw), lambda j, k: (j, 0)), out_shape=jax.ShapeDtypeStruct((nsb * ka, nw), f32),
        compiler_params=_params(("parallel", "arbitrary")),
    )(proj, a)


def _mixer_bwd(dh_out, s, p, d):
    D, DI, CD, G, N, S5W = d["D"], d["DI"], d["CD"], d["G"], d["N"], d["S5W"]
    gl = G * LANES
    gn = G * N
    c_u5, c_ga, c_gb, c_dt = DI + CD, DI + CD + S5W, DI + CD + S5W + D, DI + CD + S5W + 2 * D
    proj = s["proj"]
    g = {}
    dm, g["mix_post_g"] = _row_kernel("mix_resnorm_bwd", _vjp_of(_f_post(1.0), 1, 1, [0]), [(s["m"], D, 0), (dh_out, D, 0)],
                                      [p["mix_post_g"]], [(D, bf16)], [p["mix_post_g"].shape])
    g["w_out"] = _mm(s["merged"], dm, "tn", bf16, "mix_mm_dwout")
    dmerged = _mm(dm, p["w_out"], "nt", f32, "mix_mm_dmerged")
    dga, dya, dgb, dyb = _row_kernel(
        "mix_merge_bwd", _vjp_of(_f_merge, 4, 1, [0, 1, 2, 3]),
        [(proj, D, c_ga // D), (s["y_a"], D, 0), (proj, D, c_gb // D), (s["y_b"], D, 0), (dmerged, D, 0)], [],
        [(D, bf16), (D, bf16), (D, bf16), (D, bf16)])
    g["w_a"] = _mm(s["nrm"], dya, "tn", bf16, "mix_mm_dwa")
    dnrm = _mm(dya, p["w_a"], "nt", f32, "mix_mm_dnrm")
    dy_ssd, dz, g["norm_g"] = _row_kernel(
        "ssd_post_bwd", _vjp_of(_f_ssdpost(G), 2, 1, [0, 1]), [(s["y_ssd"], DI, 0), (proj, DI, 0), (dnrm, DI, 0)],
        [p["norm_g"]], [(DI, f32), (DI, bf16)], [p["norm_g"].shape])
    dxs, d_b, d_c, ddt, dadt, dd = _ssd_bwd(s["act"], s["dt"], s["adt"], p["d_skip"], s["states"], dy_ssd,
                                            d["HPG"], d["P"], N, "ssd_scan_bwd")
    g["d_skip"] = dd.reshape(G, 8, LANES)[:, 0, :].reshape(1, gl)
    ddt_raw, g["dt_bias"], g["a_log"] = _row_kernel(
        "ssd_dt_bwd", _vjp_of(_f_dt, 1, 2, [0]), [(proj, gl, c_dt // gl), (ddt, gl, 0), (dadt, gl, 0)],
        [p["dt_bias"], p["a_log"]], [(gl, bf16)], [p["dt_bias"].shape, p["a_log"].shape])
    cw, cb = p["conv_w"], p["conv_b"]
    dxc_x, dw_x, db_x = _conv_bwd(proj, DI, cw[:, :DI], cb[:, :DI], dxs, "ssd_conv_bwd_x")
    dxc_b, dw_b, db_b = _conv_bwd(proj, 2 * DI, cw[:, DI:DI + gn], cb[:, DI:DI + gn], d_b, "ssd_conv_bwd_b")
    dxc_c, dw_c, db_c = _conv_bwd(proj, 2 * DI + gn, cw[:, DI + gn:], cb[:, DI + gn:], d_c, "ssd_conv_bwd_c")
    g["conv_w"] = jnp.concatenate([dw_x, dw_b, dw_c], axis=1)
    g["conv_b"] = jnp.concatenate([db_x, db_b, db_c], axis=1)
    g["w_b"] = _mm(s["glu"], dyb, "tn", bf16, "mix_mm_dwb")
    dglu = _mm(dyb, p["w_b"], "nt", f32, "mix_mm_dglu")
    dvg = _row_kernel("s5_glu_bwd", _glu_bwd, [(s["vg"], s["vg"].shape[1], 0), (dglu, S5W, 0)], [], [(s["vg"].shape[1], bf16)])[0]
    g["w_glu"] = _mm(dvg, s["gel"], "tn", bf16, "mix_mm_dwglu")
    dgel = _mm(dvg, p["w_glu"], "nn", f32, "mix_mm_dgel")
    dy5, du5a, g["s5_d"] = _row_kernel(
        "s5_post_bwd", _vjp_of(_f_s5post, 2, 1, [0, 1]), [(s["y5"], S5W, 0), (proj, S5W, c_u5 // S5W), (dgel, S5W, 0)],
        [p["s5_d"]], [(S5W, bf16), (S5W, f32)], [p["s5_d"].shape])
    g["csb"] = _bdmm(s["s5st"], dy5, "tn", d["NSB"], f32, "s5_mm_dcsb")
    gst = _bdmm(dy5, p["csb"], "nt", d["NSB"], f32, "s5_mm_gst")
    a, g["lam_re_rows"], g["lam_im_rows"] = _s5_scan_bwd(gst, s["s5st"], p["lam_re_rows"], p["lam_im_rows"], d["NSB"], "s5_scan_bwd")
    g["bsb"] = _s5_dbsb(proj, c_u5, a, d)
    du5b = _bdmm(a, p["bsb"], "nt", d["NSB"], f32, "s5_mm_du5")
    du5 = _row_kernel("s5_du5", _add_fn, [(du5a, S5W, 0), (du5b, S5W, 0)], [], [(S5W, bf16)])[0]
    dproj = jnp.concatenate([dz, dxc_x, dxc_b, dxc_c, du5, dga, dgb, ddt_raw], axis=1)
    g["w_in"] = _mm(dproj, s["u"], "tn", bf16, "mix_mm_dwin")
    du = _mm(dproj, p["w_in"], "nn", f32, "mix_mm_du", bk_t=2176)
    dh, g["mix_pre_g"] = _row_kernel("mix_norm_bwd", _vjp_of(_f_norm, 1, 1, [0], 1), [(s["h"], D, 0), (du, D, 0), (dh_out, D, 0)],
                                     [p["mix_pre_g"]], [(D, f32)], [p["mix_pre_g"].shape])
    return dh, g


def _ffn1_params(l, w, wf):
    whole = lambda *names: jnp.concatenate([s for n in names for s in wf(n)], axis=0)
    return dict(ffn1_pre_g=w["ffn1_pre_g"][l].reshape(1, -1), ffn1_post_g=w["ffn1_post_g"][l].reshape(1, -1),
                wgu1=whole("ffn1_w_gate", "ffn1_w_up"), wd1=whole("ffn1_w_down"))


def _ffn2_params(wf):
    whole = lambda *names: jnp.concatenate([s for n in names for s in wf(n)], axis=0)
    return dict(wgu2=whole("ffn2_w_gate", "ffn2_w_up"), wd2=whole("ffn2_w_down"))


def _layer_params(l, w, wf, conv_w_full, d):
    r2 = lambda v: v[l].reshape(1, -1)
    p = {}
    for n in ["mix_pre_g", "mix_post_g", "ffn2_pre_g", "ffn2_post_g", "s5_d"]:
        p[n] = r2(w[n])
    whole = lambda *names: jnp.concatenate([s for n in names for s in wf(n)], axis=0)
    p["w_in"] = _w_in_perm(wf("w_in"), d)
    p["w_a"], p["w_glu"], p["w_b"], p["w_out"] = whole("w_branch_a"), whole("s5_w_glu"), whole("w_branch_b"), whole("w_out")
    p["conv_w"] = conv_w_full[l]
    p["conv_b"] = r2(w["ssd_conv_b"])
    p["dt_bias"] = _head_pad(r2(w["ssd_dt_bias"]), d)
    p["a_log"] = _head_pad(r2(w["ssd_a_log"]), d)
    p["d_skip"] = _head_pad(r2(w["ssd_d"]), d)
    p["norm_g"] = r2(w["ssd_norm_g"])
    g5, n5, c5 = d["S5G"], d["S5N"], d["S5C"]
    expand = jnp.repeat(jnp.eye(n5, dtype=f32), c5, axis=1)
    prep_in = [w["s5_lambda_re"][l], w["s5_lambda_im"][l], w["s5_log_step"][l].reshape(g5, 1),
               w["s5_b_re"][l].reshape(g5, n5 * c5), w["s5_b_im"][l].reshape(g5, n5 * c5), expand]
    lbr, lbi, bbr, bbi = _s5_prep(prep_in, "s5_prep")
    p["s5_prep_in"] = prep_in
    p["lam_re_rows"], p["lam_im_rows"] = _s5_lam_rows(lbr, d), _s5_lam_rows(lbi, d)
    to_cn = lambda v: v.astype(bf16).reshape(g5, n5, c5).transpose(0, 2, 1)
    p["bsb"] = jnp.concatenate([_s5_block_diag(to_cn(bbr), d), _s5_block_diag(to_cn(bbi), d)], axis=1)
    c_re = w["s5_c_re"][l].astype(bf16).transpose(0, 2, 1)
    c_im = w["s5_c_im"][l].astype(bf16).transpose(0, 2, 1)
    nsb = d["NSB"]
    csb = jnp.stack([_s5_block_diag(c_re, d).reshape(nsb, -1, S5_SUPERBLOCK),
                     _s5_block_diag(-c_im, d).reshape(nsb, -1, S5_SUPERBLOCK)], axis=1)
    p["csb"] = csb.reshape(-1, S5_SUPERBLOCK)
    return p


def _s5_param_grads(g, p, d, l):
    g5, n5, c5, nsb, gsb = d["S5G"], d["S5N"], d["S5C"], d["NSB"], d["GSB"]
    wst = gsb * n5
    dbsb = g["bsb"]
    from_cn = lambda v: v.transpose(0, 2, 1).reshape(g5, n5 * c5)
    dbbr = from_cn(_s5_diag_blocks(dbsb[:, :wst], d, c5, n5))
    dbbi = from_cn(_s5_diag_blocks(dbsb[:, wst:], d, c5, n5))
    rows = lambda v: v.reshape(nsb, 8, wst)[:, 0, :].reshape(g5, n5)
    cots = [rows(g["lam_re_rows"]), rows(g["lam_im_rows"]), dbbr, dbbi]
    dlr, dli, dls, dbr, dbi = _s5_prep_bwd(p["s5_prep_in"], cots, "s5_prep_bwd")
    dcsb = g["csb"].reshape(nsb, 2, wst, S5_SUPERBLOCK)
    dcr = _s5_diag_blocks(dcsb[:, 0].reshape(-1, S5_SUPERBLOCK), d, n5, c5).transpose(0, 2, 1)
    dci = -_s5_diag_blocks(dcsb[:, 1].reshape(-1, S5_SUPERBLOCK), d, n5, c5).transpose(0, 2, 1)
    return dict(s5_lambda_re=dlr, s5_lambda_im=dli, s5_log_step=dls.reshape(g5), s5_b_re=dbr.reshape(g5, n5, c5),
                s5_b_im=dbi.reshape(g5, n5, c5), s5_c_re=dcr, s5_c_im=dci)


def kernel(x, ffn1_pre_g, ffn1_post_g, ffn1_w_gate, ffn1_w_up, ffn1_w_down, mix_pre_g, mix_post_g, w_in, ssd_conv_w, ssd_conv_b, ssd_dt_bias, ssd_a_log, ssd_d, ssd_norm_g, w_branch_a, s5_lambda_re, s5_lambda_im, s5_b_re, s5_b_im, s5_c_re, s5_c_im, s5_log_step, s5_d, s5_w_glu, w_branch_b, w_out, ffn2_pre_g, ffn2_post_g, ffn2_w_gate, ffn2_w_up, ffn2_w_down, loss_target, m_ffn1_pre_g, m_ffn1_post_g, m_ffn1_w_gate, m_ffn1_w_up, m_ffn1_w_down, m_mix_pre_g, m_mix_post_g, m_w_in, m_ssd_conv_w, m_ssd_conv_b, m_ssd_dt_bias, m_ssd_a_log, m_ssd_d, m_ssd_norm_g, m_w_branch_a, m_s5_lambda_re, m_s5_lambda_im, m_s5_b_re, m_s5_b_im, m_s5_c_re, m_s5_c_im, m_s5_log_step, m_s5_d, m_s5_w_glu, m_w_branch_b, m_w_out, m_ffn2_pre_g, m_ffn2_post_g, m_ffn2_w_gate, m_ffn2_w_up, m_ffn2_w_down, v_ffn1_pre_g, v_ffn1_post_g, v_ffn1_w_gate, v_ffn1_w_up, v_ffn1_w_down, v_mix_pre_g, v_mix_post_g, v_w_in, v_ssd_conv_w, v_ssd_conv_b, v_ssd_dt_bias, v_ssd_a_log, v_ssd_d, v_ssd_norm_g, v_w_branch_a, v_s5_lambda_re, v_s5_lambda_im, v_s5_b_re, v_s5_b_im, v_s5_c_re, v_s5_c_im, v_s5_log_step, v_s5_d, v_s5_w_glu, v_w_branch_b, v_w_out, v_ffn2_pre_g, v_ffn2_post_g, v_ffn2_w_gate, v_ffn2_w_up, v_ffn2_w_down):
    given = dict(locals())
    for n in COL_SHARDED:
        for prefix in ("", "m_", "v_"):
            given[prefix + n] = given[prefix + n].transpose(0, 2, 1)
    w = {n: given[n] for n in WEIGHTS}
    mom = {n: given["m_" + n] for n in WEIGHTS}
    var = {n: given["v_" + n] for n in WEIGHTS}
    d = _dims(w, x)
    n_layers = w["ffn1_pre_g"].shape[0]
    T, D = d["T"], d["D"]

    conv_w = w["ssd_conv_w"]
    first, middle, last = BIG[:3], BIG[3:8], BIG[8:]
    wf, _ = _unpack_weights(_chip_all_gather(_pack_weights(w, 0, first), "gather_weights_first"), w, 0, first)
    coming = _chip_gather_start(_pack_weights(w, 0, middle, conv_w), "gather_start_l0")
    h = x.reshape(T, D) + coming[-1][0, 0]
    saved, layers = [], []
    for l in range(n_layers):
        if l > 0:
            pack, land = _chip_gather_wait(*coming[:4], h, f"gather_wait_l{l}")
            wf, _ = _unpack_weights(_chip_gather_finish(pack, land, f"gather_finish_l{l}"), w, l, BIG)
        p = _ffn1_params(l, w, wf)
        h, s1 = _ffn_fwd(h, p["ffn1_pre_g"], p["ffn1_post_g"], p["wgu1"], p["wd1"], "ffn1")
        if l == 0:
            pack, land = _chip_gather_wait(*coming[:4], h, "gather_wait_l0")
            wf, conv_w_full = _unpack_weights(_chip_gather_finish(pack, land, "gather_finish_l0"), w, 0, middle, conv_w)
            ffn2_coming = _chip_gather_start(_pack_weights(w, 0, last), "gather_start_l0_ffn2")
            h = h + ffn2_coming[-1][0, 0]
        if l + 1 < n_layers:
            coming = _chip_gather_start(_pack_weights(w, l + 1, BIG), f"gather_start_l{l + 1}")
            h = h + coming[-1][0, 0]
        p.update(_layer_params(l, w, wf, conv_w_full, d))
        layers.append(p)
        h, sm = _mixer_fwd(h, p, d)
        if l == 0:
            pack, land = _chip_gather_wait(*ffn2_coming[:4], h, "gather_wait_l0_ffn2")
            wf, _ = _unpack_weights(_chip_gather_finish(pack, land, "gather_finish_l0_ffn2"), w, 0, last)
        p.update(_ffn2_params(wf))
        h, s2 = _ffn_fwd(h, p["ffn2_pre_g"], p["ffn2_post_g"], p["wgu2"], p["wd2"], "ffn2")
        saved.append((s1, sm, s2))
    dh, loss_part = _row_kernel("loss", _loss_fn, [(h, D, 0), (loss_target.reshape(T, D), D, 0)], [], [(D, f32)], [(8, LANES)])
    loss = lax.psum(loss_part[0, 0], ("x", "y", "c"))

    my_core = lax.axis_index("c").astype(jnp.int32).reshape(1)
    my_chip = (2 * lax.axis_index("x") + lax.axis_index("y")).astype(jnp.int32).reshape(1)
    lg, exchanging, in_flight = [None] * n_layers, [None] * n_layers, [None] * n_layers
    for l in reversed(range(n_layers)):
        p = layers[l]
        s1, sm, s2 = saved[l]
        dh, g2 = _ffn_bwd(dh, s2, p["ffn2_pre_g"], p["ffn2_post_g"], p["wgu2"], p["wd2"], "ffn2")
        if l + 1 < n_layers:
            in_flight[l + 1] = _reduce_start(exchanging[l + 1], my_core, dh, f"grads_l{l + 1}")
            dh = dh + in_flight[l + 1][-1][0, 0]
        dh, gm = _mixer_bwd(dh, sm, p, d)
        dh, g1 = _ffn_bwd(dh, s1, p["ffn1_pre_g"], p["ffn1_post_g"], p["wgu1"], p["wd1"], "ffn1")
        H = p["wd1"].shape[0]
        gl = dict(ffn1_pre_g=g1["pre_g"], ffn1_post_g=g1["post_g"], ffn1_w_gate=g1["wgu"][:H], ffn1_w_up=g1["wgu"][H:],
                  ffn1_w_down=g1["wd"], ffn2_pre_g=g2["pre_g"], ffn2_post_g=g2["post_g"], ffn2_w_gate=g2["wgu"][:H],
                  ffn2_w_up=g2["wgu"][H:], ffn2_w_down=g2["wd"], mix_pre_g=gm["mix_pre_g"], mix_post_g=gm["mix_post_g"],
                  w_in=_w_in_unperm(gm["w_in"], d), ssd_conv_w=gm["conv_w"], ssd_conv_b=gm["conv_b"],
                  ssd_dt_bias=_head_unpad(gm["dt_bias"], d), ssd_a_log=_head_unpad(gm["a_log"], d),
                  ssd_d=_head_unpad(gm["d_skip"], d), ssd_norm_g=gm["norm_g"], w_branch_a=gm["w_a"], s5_d=gm["s5_d"],
                  s5_w_glu=gm["w_glu"], w_branch_b=gm["w_b"], w_out=gm["w_out"])
        gl.update(_s5_param_grads(gm, p, d, l))
        lg[l] = gl
        exchanging[l] = _core_send_other_half_start(_pack_big_grads([gl]), f"exchange_start_grads_l{l}")
        if l > 0:
            dh = dh + exchanging[l][-1][0, 0]
    grad_x = dh.reshape(x.shape)
    summed = [None] * n_layers
    for l in range(1, n_layers):
        summed[l] = _reduce_finish(in_flight[l], my_chip, exchanging[0][-1], f"grads_l{l}")
    in_flight[0] = _reduce_start(exchanging[0], my_core, summed[-1] if n_layers > 1 else grad_x, "grads_l0")
    small_names = SMALL + ["ssd_conv_w"]
    small_parts = {n: [g[n] for g in lg] for n in small_names}
    small_like = {n: jax.ShapeDtypeStruct((n_layers,) + lg[0][n].shape, f32) for n in small_names}
    small_like.update({n: w[n] for n in SMALL})
    small_pack = _pack_small(small_parts, small_names)
    small_sum = _reduce_to_chips(jnp.broadcast_to(small_pack, (N_CHIPS,) + small_pack.shape), my_core, "small")
    small = _unpack_small(small_sum, small_like, small_names)
    summed[0] = _reduce_finish(in_flight[0], my_chip, small_sum, "grads_l0")
    big_grads = _unpack_big_grads(summed, w)
    grads = {}
    k_me = 2 * lax.axis_index("x") + lax.axis_index("y")
    cw = w["ssd_conv_w"].shape[-1]
    small["ssd_conv_w"] = lax.dynamic_slice_in_dim(small["ssd_conv_w"], k_me * cw, cw, axis=2)
    grads.update(small)

    delta, new_m, new_v = {}, {}, {}
    for n in BIG:
        grads[n], delta[n], new_m[n], new_v[n] = _adamw_layers(w[n], big_grads[n], mom[n], var[n], "adamw_" + n)
    n = "ssd_conv_w"
    delta[n], new_m[n], new_v[n] = _adamw(w[n], grads[n], mom[n], var[n], "adamw_" + n)
    pw, pm, pv = [_pack_small(t, SMALL) for t in (w, mom, var)]
    assert pw.shape[0] <= small_sum.shape[0]
    sd, sm_, sv = _adamw(pw, small_sum[:pw.shape[0]], pm, pv, "adamw_small")
    delta.update(_unpack_small(sd, w, SMALL))
    new_m.update(_unpack_small(sm_, w, SMALL))
    new_v.update(_unpack_small(sv, w, SMALL))
    for n in COL_SHARDED:
        for out in (grads, delta, new_m, new_v):
            out[n] = out[n].transpose(0, 2, 1)
    return (loss, grad_x, *[grads[n] for n in WEIGHTS], *[delta[n] for n in WEIGHTS],
            *[new_m[n] for n in WEIGHTS], *[new_v[n] for n in WEIGHTS])
```

```python
import functools

import numpy as np
import jax
import jax.numpy as jnp
from jax import lax
from jax.experimental import pallas as pl
from jax.experimental.pallas import tpu as pltpu

f32, bf16 = jnp.float32, jnp.bfloat16

SSD_N_GROUPS = 4
SSD_CHUNK = 128
RMS_EPS = 1e-6
S5_MAX_REAL = -1e-4
S5_SUPERBLOCK = 256
ADAM_LR, ADAM_B1, ADAM_B2, ADAM_EPS, ADAM_WD, ADAM_STEP = 0.001, 0.9, 0.999, 1e-08, 0.01, 10

LANES = 128
PACK_COLS = 1024
D2D_STREAMS = 16
PACK_ROW_MULT = 2 * D2D_STREAMS * 16
VMEM_LIMIT_BYTES = 48 * 1024 * 1024
N_CHIPS, N_CORES, N_DEV = 4, 2, 8
MESH = pl.DeviceIdType.MESH

BIG = ["ffn1_w_gate", "ffn1_w_up", "ffn1_w_down", "w_in", "w_branch_a", "s5_w_glu", "w_branch_b", "w_out",
       "ffn2_w_gate", "ffn2_w_up", "ffn2_w_down"]
COL_SHARDED = ["ffn1_w_gate", "ffn1_w_up", "w_in", "s5_w_glu", "ffn2_w_gate", "ffn2_w_up"]
SMALL = ["ffn1_pre_g", "ffn1_post_g", "mix_pre_g", "mix_post_g", "ssd_conv_b", "ssd_norm_g", "s5_lambda_re", "s5_lambda_im",
         "s5_b_re", "s5_b_im", "s5_c_re", "s5_c_im", "s5_d", "ffn2_pre_g", "ffn2_post_g", "s5_log_step", "ssd_dt_bias",
         "ssd_a_log", "ssd_d"]
WEIGHTS = ["ffn1_pre_g", "ffn1_post_g", "ffn1_w_gate", "ffn1_w_up", "ffn1_w_down", "mix_pre_g", "mix_post_g", "w_in",
           "ssd_conv_w", "ssd_conv_b", "ssd_dt_bias", "ssd_a_log", "ssd_d", "ssd_norm_g", "w_branch_a", "s5_lambda_re",
           "s5_lambda_im", "s5_b_re", "s5_b_im", "s5_c_re", "s5_c_im", "s5_log_step", "s5_d", "s5_w_glu", "w_branch_b",
           "w_out", "ffn2_pre_g", "ffn2_post_g", "ffn2_w_gate", "ffn2_w_up", "ffn2_w_down"]


def _params(sem=None):
    return pltpu.CompilerParams(dimension_semantics=sem, vmem_limit_bytes=VMEM_LIMIT_BYTES)


def _pick(n, target, mult=LANES):
    best = None
    for d in range(mult, min(n, target) + 1, mult):
        if n % d == 0:
            best = d
    return best if best is not None else n


_DIMS = {"nn": (((1,), (0,)), ((), ())), "nt": (((1,), (1,)), ((), ())), "tn": (((0,), (0,)), ((), ()))}


def _mm(a, b, mode, out_dtype, name, bm_t=1024, bn_t=1024, bk_t=2816):
    if mode == "nn":
        (M, K), (K2, N) = a.shape, b.shape
    elif mode == "nt":
        (M, K), (N, K2) = a.shape, b.shape
    else:
        (K, M), (K2, N) = a.shape, b.shape
    assert K == K2, (name, a.shape, b.shape)
    bm, bn, bk = _pick(M, bm_t), _pick(N, bn_t), _pick(K, bk_t)
    nk = K // bk
    dn = _DIMS[mode]

    def body(a_ref, b_ref, o_ref, *scratch):
        p = lax.dot_general(a_ref[...].astype(bf16), b_ref[...].astype(bf16), dn, preferred_element_type=f32)
        if nk == 1:
            o_ref[...] = p.astype(o_ref.dtype)
        else:
            acc = scratch[0]
            k = pl.program_id(2)

            @pl.when(k == 0)
            def _():
                acc[...] = p

            @pl.when(k > 0)
            def _():
                acc[...] += p

            @pl.when(k == nk - 1)
            def _():
                o_ref[...] = acc[...].astype(o_ref.dtype)

    if mode == "tn":
        a_spec = pl.BlockSpec((bk, bm), lambda i, j, k: (k, i))
    else:
        a_spec = pl.BlockSpec((bm, bk), lambda i, j, k: (i, k))
    if mode == "nt":
        b_spec = pl.BlockSpec((bn, bk), lambda i, j, k: (j, k))
    else:
        b_spec = pl.BlockSpec((bk, bn), lambda i, j, k: (k, j))
    return pl.pallas_call(
        body, name=name, grid=(M // bm, N // bn, nk), in_specs=[a_spec, b_spec],
        out_specs=pl.BlockSpec((bm, bn), lambda i, j, k: (i, j)), out_shape=jax.ShapeDtypeStruct((M, N), out_dtype),
        scratch_shapes=[pltpu.VMEM((bm, bn), f32)] if nk > 1 else [],
        compiler_params=_params(("parallel", "parallel", "arbitrary")),
    )(a, b)


def _bdmm(a, w, mode, nb, out_dtype, name, bt_t=512):
    if mode == "tn":
        T = a.shape[0]
        ka, nw = a.shape[1] // nb, w.shape[1] // nb
        bt = _pick(T, bt_t)
        nt = T // bt

        def body_tn(a_ref, b_ref, o_ref):
            p = lax.dot_general(a_ref[...].astype(bf16), b_ref[...].astype(bf16), _DIMS["tn"], preferred_element_type=f32)
            k = pl.program_id(1)

            @pl.when(k == 0)
            def _():
                o_ref[...] = p

            @pl.when(k > 0)
            def _():
                o_ref[...] += p

        return pl.pallas_call(
            body_tn, name=name, grid=(nb, nt),
            in_specs=[pl.BlockSpec((bt, ka), lambda j, k: (k, j)), pl.BlockSpec((bt, nw), lambda j, k: (k, j))],
            out_specs=pl.BlockSpec((ka, nw), lambda j, k: (j, 0)), out_shape=jax.ShapeDtypeStruct((nb * ka, nw), f32),
            compiler_params=_params(("parallel", "arbitrary")),
        )(a, w)
    T = a.shape[0]
    ka, nw = w.shape[0] // nb, w.shape[1]
    bt = _pick(T, bt_t)
    kin, kout = (ka, nw) if mode == "nn" else (nw, ka)
    dn = _DIMS[mode]

    def body(a_ref, w_ref, o_ref):
        o_ref[...] = lax.dot_general(a_ref[...].astype(bf16), w_ref[...].astype(bf16), dn,
                                     preferred_element_type=f32).astype(o_ref.dtype)

    return pl.pallas_call(
        body, name=name, grid=(nb, T // bt),
        in_specs=[pl.BlockSpec((bt, kin), lambda j, i: (i, j)), pl.BlockSpec((ka, nw), lambda j, i: (j, 0))],
        out_specs=pl.BlockSpec((bt, kout), lambda j, i: (i, j)), out_shape=jax.ShapeDtypeStruct((T, nb * kout), out_dtype),
        compiler_params=_params(("parallel", "parallel")),
    )(a, w)


def _row_index(i, cb):
    return (i, cb)


def _row_kernel(name, fn, rows, pars, row_outs, par_outs=(), block_rows=256):
    T = rows[0][0].shape[0]
    widest = max([nc for (_, nc, _) in rows] + [nc for (nc, _) in row_outs])
    R = min(block_rows if widest > 1024 else 2 * block_rows, T)
    assert T % R == 0
    nr, npar, nro = len(rows), len(pars), len(row_outs)

    def body(*refs):
        rv = [r[...] for r in refs[:nr]]
        pv = [r[...] for r in refs[nr:nr + npar]]
        ro, po = fn(rv, pv)
        for ref, v in zip(refs[nr + npar:nr + npar + nro], ro):
            ref[...] = v.astype(ref.dtype)
        if par_outs:
            i = pl.program_id(0)
            prefs = refs[nr + npar + nro:]

            @pl.when(i == 0)
            def _():
                for ref, v in zip(prefs, po):
                    ref[...] = v.astype(f32)

            @pl.when(i > 0)
            def _():
                for ref, v in zip(prefs, po):
                    ref[...] += v.astype(f32)

    in_specs = [pl.BlockSpec((R, nc), functools.partial(_row_index, cb=cb)) for (_, nc, cb) in rows]
    in_specs += [pl.BlockSpec(p.shape, lambda i: (0, 0)) for p in pars]
    out_specs = [pl.BlockSpec((R, nc), lambda i: (i, 0)) for (nc, _) in row_outs]
    out_specs += [pl.BlockSpec(s, lambda i: (0, 0)) for s in par_outs]
    out_shape = [jax.ShapeDtypeStruct((T, nc), dt) for (nc, dt) in row_outs]
    out_shape += [jax.ShapeDtypeStruct(s, f32) for s in par_outs]
    outs = pl.pallas_call(
        body, name=name, grid=(T // R,), in_specs=in_specs, out_specs=out_specs, out_shape=out_shape,
        compiler_params=_params(("arbitrary",) if par_outs else ("parallel",)),
    )(*[r[0] for r in rows], *pars)
    return list(outs)


def _fwd_of(f):
    def fn(rv, pv):
        return f([v.astype(f32) for v in rv], [v.astype(f32) for v in pv]), []
    return fn


def _vjp_of(f, n_x, n_cot, grad_idx, n_add=0):
    def fn(rv, pv):
        xs = [v.astype(f32) for v in rv[:n_x]]
        cots = [v.astype(f32) for v in rv[n_x:n_x + n_cot]]
        adds = rv[n_x + n_cot:n_x + n_cot + n_add]
        ps = [v.astype(f32) for v in pv]
        _, vjp = jax.vjp(lambda *a: f(list(a[:n_x]), list(a[n_x:])), *xs, *ps)
        g = vjp(cots)
        row_g = [g[i] for i in grad_idx]
        for k, a in enumerate(adds):
            row_g[k] = row_g[k] + a.astype(f32)
        return row_g, list(g[n_x:])
    return fn


def _rms(x, g):
    return x * lax.rsqrt(jnp.mean(x * x, axis=-1, keepdims=True) + RMS_EPS) * g


def _f_norm(xs, ps):
    return [_rms(xs[0], ps[0])]


def _f_post(scale):
    def f(xs, ps):
        return [scale * _rms(xs[0], ps[0])]
    return f


def _f_resnorm(scale):
    def f(xs, ps):
        return [xs[0] + scale * _rms(xs[1], ps[0])]
    return f


def _f_dt(xs, ps):
    dt = jax.nn.softplus(xs[0] + ps[0])
    return [dt, -jnp.exp(ps[1]) * dt]


def _f_ssdpost(n_groups):
    def f(xs, ps):
        y = xs[0] * jax.nn.silu(xs[1])
        width = y.shape[-1] // n_groups
        lane = lax.broadcasted_iota(jnp.int32, y.shape, 1)
        scale = jnp.zeros_like(y)
        for k in range(n_groups):
            m = ((lane >= k * width) & (lane < (k + 1) * width)).astype(f32)
            ms = jnp.sum(y * y * m, axis=-1, keepdims=True) / width
            scale = scale + lax.rsqrt(ms + RMS_EPS) * m
        return [y * scale * ps[0]]
    return f


def _f_s5post(xs, ps):
    return [jax.nn.gelu(xs[0] + ps[0] * xs[1])]


def _f_merge(xs, ps):
    return [jax.nn.sigmoid(xs[0]) * xs[1] + jax.nn.sigmoid(xs[2]) * xs[3]]


def _swiglu_fwd(rv, pv):
    ab = rv[0].astype(f32)
    h = ab.shape[1] // 2
    return [jax.nn.silu(ab[:, :h]) * ab[:, h:]], []


def _swiglu_bwd(rv, pv):
    ab, d = rv[0].astype(f32), rv[1].astype(f32)
    h = ab.shape[1] // 2
    a, b = ab[:, :h], ab[:, h:]
    s = jax.nn.sigmoid(a)
    return [jnp.concatenate([d * b * (s * (1.0 + a * (1.0 - s))), d * (a * s)], axis=1)], []


def _glu_fwd(rv, pv):
    vg = rv[0].astype(f32)
    h = vg.shape[1] // 2
    return [vg[:, :h] * jax.nn.sigmoid(vg[:, h:])], []


def _glu_bwd(rv, pv):
    vg, d = rv[0].astype(f32), rv[1].astype(f32)
    h = vg.shape[1] // 2
    s = jax.nn.sigmoid(vg[:, h:])
    return [jnp.concatenate([d * s, d * vg[:, :h] * s * (1.0 - s)], axis=1)], []


def _loss_fn(rv, pv):
    e = rv[0].astype(f32) - rv[1].astype(f32)
    per_tok = jnp.mean(e * e, axis=-1, keepdims=True)
    part = 0.5 * jnp.sum(per_tok, axis=0, keepdims=True)
    return [e / e.shape[-1]], [jnp.broadcast_to(part, (8, LANES))]


def _add_fn(rv, pv):
    return [rv[0].astype(f32) + rv[1].astype(f32)], []


def _adamw_fn(rv, pv):
    w, g, m, v = [x.astype(f32) for x in rv]
    m = ADAM_B1 * m + (1.0 - ADAM_B1) * g
    v = ADAM_B2 * v + (1.0 - ADAM_B2) * (g * g)
    m_hat = m / (1.0 - ADAM_B1 ** ADAM_STEP)
    v_hat = v / (1.0 - ADAM_B2 ** ADAM_STEP)
    return [-ADAM_LR * (m_hat / (jnp.sqrt(v_hat) + ADAM_EPS) + ADAM_WD * w), m, v], []


def _adamw_layers(w, gs, m, v, name):
    n_layers, r, cols = w.shape
    br, bc = _pick(r, 256, 8), cols
    if br < 64 and cols % LANES == 0:
        br, bc = r, LANES

    def body(w_ref, *refs):
        g_refs, (m_ref, v_ref, g_out, d_ref, nm_ref, nv_ref) = refs[:n_layers], refs[n_layers:]
        layer = pl.program_id(0)
        g = g_refs[0][...]
        for k in range(1, n_layers):
            g = jnp.where(layer == k, g_refs[k][...], g)
        outs, _ = _adamw_fn([w_ref[0], g, m_ref[0], v_ref[0]], [])
        g_out[0] = g
        d_ref[0], nm_ref[0], nv_ref[0] = outs

    stacked = pl.BlockSpec((1, br, bc), lambda l, i, j: (l, i, j))
    single = pl.BlockSpec((br, bc), lambda l, i, j: (i, j))
    return pl.pallas_call(
        body, name=name, grid=(n_layers, r // br, cols // bc), in_specs=[stacked] + [single] * n_layers + [stacked] * 2,
        out_specs=[stacked] * 4, out_shape=[jax.ShapeDtypeStruct(w.shape, f32)] * 4,
        compiler_params=_params(("parallel", "parallel", "parallel")),
    )(w, *gs, m, v)


def _adamw(w, g, m, v, name):
    shape = w.shape
    cols = shape[-1] if (w.ndim >= 2 and shape[-1] >= LANES) else None
    if cols is None:
        n = int(np.prod(shape))
        cols = LANES if n % LANES == 0 else n
    n_rows = int(np.prod(shape)) // cols
    br, bc = _pick(n_rows, 256, 8), cols
    if br < 64 and cols % LANES == 0:
        br, bc = n_rows, LANES

    def body(w_ref, g_ref, m_ref, v_ref, d_ref, nm_ref, nv_ref):
        outs, _ = _adamw_fn([w_ref[...], g_ref[...], m_ref[...], v_ref[...]], [])
        d_ref[...], nm_ref[...], nv_ref[...] = outs

    spec = pl.BlockSpec((br, bc), lambda i, j: (i, j))
    outs = pl.pallas_call(
        body, name=name, grid=(n_rows // br, cols // bc), in_specs=[spec] * 4, out_specs=[spec] * 3,
        out_shape=[jax.ShapeDtypeStruct((n_rows, cols), f32)] * 3, compiler_params=_params(("parallel", "parallel")),
    )(*[t.reshape(n_rows, cols) for t in (w, g, m, v)])
    return [o.reshape(shape) for o in outs]


def _shift_down(x, s, row):
    if s == 0:
        return x
    return jnp.where(row >= s, pltpu.roll(x, s, 0), 0.0)


def _shift_up(x, s, row):
    if s == 0:
        return x
    n = x.shape[0]
    return jnp.where(row < n - s, pltpu.roll(x, n - s, 0), 0.0)


def _conv_pre(x, w, b, row):
    kw = w.shape[0]
    c = b
    for k in range(kw):
        c = c + w[k:k + 1, :] * _shift_down(x, kw - 1 - k, row)
    return c


def _conv_fwd(xsrc, col0, w, b, name, bc_t=512):
    T = xsrc.shape[0]
    kw, ncols = w.shape
    bc = _pick(ncols, bc_t)
    off = col0 // bc
    assert col0 % bc == 0

    def body(x_ref, w_ref, b_ref, o_ref):
        x = x_ref[...].astype(f32)
        row = lax.broadcasted_iota(jnp.int32, x.shape, 0)
        c = _conv_pre(x, w_ref[...], b_ref[...], row)
        o_ref[...] = c * jax.nn.sigmoid(c)

    return pl.pallas_call(
        body, name=name, grid=(ncols // bc,),
        in_specs=[pl.BlockSpec((T, bc), lambda j: (0, off + j)), pl.BlockSpec((kw, bc), lambda j: (0, j)),
                  pl.BlockSpec((1, bc), lambda j: (0, j))],
        out_specs=pl.BlockSpec((T, bc), lambda j: (0, j)), out_shape=jax.ShapeDtypeStruct((T, ncols), f32),
        compiler_params=_params(("parallel",)),
    )(xsrc, w, b)


def _conv_bwd(xsrc, col0, w, b, dact, name, bc_t=512):
    T = xsrc.shape[0]
    kw, ncols = w.shape
    bc = _pick(ncols, bc_t)
    off = col0 // bc
    assert col0 % bc == 0

    def body(x_ref, w_ref, b_ref, d_ref, dx_ref, dw_ref, db_ref):
        x = x_ref[...].astype(f32)
        w = w_ref[...]
        row = lax.broadcasted_iota(jnp.int32, x.shape, 0)
        c = _conv_pre(x, w, b_ref[...], row)
        s = jax.nn.sigmoid(c)
        dc = d_ref[...].astype(f32) * (s * (1.0 + c * (1.0 - s)))
        dx = jnp.zeros_like(x)
        dws = []
        for k in range(kw):
            dx = dx + w[k:k + 1, :] * _shift_up(dc, kw - 1 - k, row)
            dws.append(jnp.sum(dc * _shift_down(x, kw - 1 - k, row), axis=0, keepdims=True))
        dx_ref[...] = dx.astype(dx_ref.dtype)
        dw_ref[...] = jnp.concatenate(dws, axis=0)
        db_ref[...] = jnp.sum(dc, axis=0, keepdims=True)

    return pl.pallas_call(
        body, name=name, grid=(ncols // bc,),
        in_specs=[pl.BlockSpec((T, bc), lambda j: (0, off + j)), pl.BlockSpec((kw, bc), lambda j: (0, j)),
                  pl.BlockSpec((1, bc), lambda j: (0, j)), pl.BlockSpec((T, bc), lambda j: (0, j))],
        out_specs=[pl.BlockSpec((T, bc), lambda j: (0, j)), pl.BlockSpec((kw, bc), lambda j: (0, j)),
                   pl.BlockSpec((1, bc), lambda j: (0, j))],
        out_shape=[jax.ShapeDtypeStruct((T, ncols), bf16), jax.ShapeDtypeStruct((kw, ncols), f32),
                   jax.ShapeDtypeStruct((1, ncols), f32)],
        compiler_params=_params(("parallel",)),
    )(xsrc, w, b, dact)


_HI = lax.Precision.HIGHEST


def _dot(a, b, dims="nn", precision=None):
    return lax.dot_general(a, b, _DIMS[dims], preferred_element_type=f32, precision=precision)


def _dot01(a, b, dims="nn", ones="b"):
    x = a if ones == "b" else b
    hi = x.astype(bf16)
    rest = x - hi.astype(f32)
    mid = rest.astype(bf16)
    lo = (rest - mid.astype(f32)).astype(bf16)
    if ones == "b":
        e = b.astype(bf16)
        return _dot(hi, e, dims) + _dot(mid, e, dims) + _dot(lo, e, dims)
    e = a.astype(bf16)
    return _dot(e, hi, dims) + _dot(e, mid, dims) + _dot(e, lo, dims)


def _ssd_common(x_ref, b_ref, c_ref, dt_ref, adt_ref, d_ref, hpg, p):
    q = b_ref.shape[0]
    hp = hpg * p
    bb, cb = b_ref[...].astype(bf16), c_ref[...].astype(bf16)
    r = lax.broadcasted_iota(jnp.int32, (q, q), 0)
    s = lax.broadcasted_iota(jnp.int32, (q, q), 1)
    tril = r >= s
    trilf = tril.astype(f32)
    eh = lax.broadcasted_iota(jnp.int32, (LANES, hp), 0)
    ec = lax.broadcasted_iota(jnp.int32, (LANES, hp), 1)
    expand = ((ec >= eh * p) & (ec < (eh + 1) * p)).astype(f32)
    adt = adt_ref[...]
    cum = _dot01(trilf, adt, "nn", "a")
    cum_t = _dot01(adt, (r <= s).astype(f32), "tn")
    cum_e = _dot01(cum, expand)
    dt_e = _dot01(dt_ref[...], expand)
    d_e = _dot01(jnp.broadcast_to(d_ref[...], (8, LANES)), expand)[0:1, :]
    gmat = _dot(cb, bb, "nt")
    x = x_ref[...]
    xdt = x * dt_e
    e_all = jnp.exp(cum_e)
    dec = jnp.exp(cum_e[q - 1:q, :] - cum_e)
    lms, ms = [], []
    for h in range(hpg):
        lm = jnp.exp(jnp.where(tril, cum[:, h:h + 1] - cum_t[h:h + 1, :], -1e30))
        lms.append(lm)
        ms.append(gmat * lm)
    et = [jnp.exp(cum[q - 1:q, h:h + 1]) for h in range(hpg)]
    return dict(bb=bb, cb=cb, trilf=trilf, expand=expand, cum=cum, x=x, xdt=xdt, dt_e=dt_e, d_e=d_e, e=e_all, dec=dec,
                lms=lms, ms=ms, et=et)


def _ssd_specs(q, hp, n, g_n, nc, rev):
    def cidx(c):
        return (nc - 1 - c) if rev else c
    x_spec = pl.BlockSpec((q, hp), lambda g, c: (cidx(c), g))
    boff = (g_n * hp) // n
    b_spec = pl.BlockSpec((q, n), lambda g, c: (cidx(c), boff + g))
    c_spec = pl.BlockSpec((q, n), lambda g, c: (cidx(c), boff + g_n + g))
    dt_spec = pl.BlockSpec((q, LANES), lambda g, c: (cidx(c), g))
    d_spec = pl.BlockSpec((1, LANES), lambda g, c: (0, g))
    st_spec = pl.BlockSpec((1, 1, hp, n), lambda g, c: (cidx(c), g, 0, 0))
    return x_spec, b_spec, c_spec, dt_spec, d_spec, st_spec


def _ssd_fwd(act, dt, adt, dpad, hpg, p, n, name):
    T = act.shape[0]
    g_n, q = SSD_N_GROUPS, SSD_CHUNK
    nc, hp = T // q, hpg * p
    x_spec, b_spec, c_spec, dt_spec, d_spec, st_spec = _ssd_specs(q, hp, n, g_n, nc, False)

    def body(x_ref, b_ref, c_ref, dt_ref, adt_ref, d_ref, y_ref, st_ref, s_scr):
        @pl.when(pl.program_id(1) == 0)
        def _():
            s_scr[...] = jnp.zeros_like(s_scr)

        k = _ssd_common(x_ref, b_ref, c_ref, dt_ref, adt_ref, d_ref, hpg, p)
        s0 = s_scr[...]
        st_ref[0, 0] = s0
        xdtb = k["xdt"].astype(bf16)
        ydiag = [_dot(k["ms"][h].astype(bf16), xdtb[:, h * p:(h + 1) * p]) for h in range(hpg)]
        z = _dot(k["cb"], s0.astype(bf16), "nt")
        y_ref[...] = jnp.concatenate(ydiag, axis=1) + k["e"] * z + k["d_e"] * k["x"]
        upd = _dot((k["xdt"] * k["dec"]).astype(bf16), k["bb"], "tn")
        for h in range(hpg):
            s_scr[h * p:(h + 1) * p, :] = k["et"][h] * s0[h * p:(h + 1) * p, :] + upd[h * p:(h + 1) * p, :]

    return pl.pallas_call(
        body, name=name, grid=(g_n, nc),
        in_specs=[x_spec, b_spec, c_spec, dt_spec, dt_spec, d_spec],
        out_specs=[pl.BlockSpec((q, hp), lambda g, c: (c, g)), st_spec],
        out_shape=[jax.ShapeDtypeStruct((T, g_n * hp), f32), jax.ShapeDtypeStruct((nc, g_n, hp, n), f32)],
        scratch_shapes=[pltpu.VMEM((hp, n), f32)],
        compiler_params=_params(("parallel", "arbitrary")),
    )(act, act, act, dt, adt, dpad)


def _ssd_bwd(act, dt, adt, dpad, states, dy, hpg, p, n, name):
    T = act.shape[0]
    g_n, q = SSD_N_GROUPS, SSD_CHUNK
    nc, hp = T // q, hpg * p
    x_spec, b_spec, c_spec, dt_spec, d_spec, st_spec = _ssd_specs(q, hp, n, g_n, nc, True)

    def body(x_ref, b_ref, c_ref, dt_ref, adt_ref, d_ref, st_ref, dy_ref,
             dx_ref, db_ref, dc_ref, ddt_ref, dadt_ref, dd_ref, ds_scr):
        first = pl.program_id(1) == 0

        @pl.when(first)
        def _():
            ds_scr[...] = jnp.zeros_like(ds_scr)

        k = _ssd_common(x_ref, b_ref, c_ref, dt_ref, adt_ref, d_ref, hpg, p)
        bb, cb, expand, x, xdt, dec = k["bb"], k["cb"], k["expand"], k["x"], k["xdt"], k["dec"]
        heads = lambda t: _dot01(t, expand, "nt")
        s0 = st_ref[0, 0]
        ds1 = ds_scr[...]
        s0b, ds1b = s0.astype(bf16), ds1.astype(bf16)
        dy = dy_ref[...]
        dyb, xdtb = dy.astype(bf16), xdt.astype(bf16)
        lane = lax.broadcasted_iota(jnp.int32, (1, LANES), 1)
        dg = jnp.zeros((q, q), f32)
        w_rows = jnp.zeros((q, LANES), f32)
        w_cols, dxdt_parts = [], []
        for h in range(hpg):
            hs = slice(h * p, (h + 1) * p)
            dm = _dot(dyb[:, hs], xdtb[:, hs], "nt")
            dg = dg + dm * k["lms"][h]
            wm = dm * k["ms"][h]
            w_rows = w_rows + jnp.sum(wm, axis=1, keepdims=True) * (lane == h).astype(f32)
            w_cols.append(jnp.sum(wm, axis=0, keepdims=True))
            dxdt_parts.append(_dot(k["ms"][h].astype(bf16), dyb[:, hs], "tn"))
        dxdt_diag = jnp.concatenate(dxdt_parts, axis=1)
        w_cols = jnp.concatenate(w_cols + [jnp.zeros((LANES - hpg, q), f32)], axis=0).T
        dgb = dg.astype(bf16)
        z = _dot(cb, s0b, "nt")
        dz = dy * k["e"]
        dzb = dz.astype(bf16)
        dxd = _dot(bb, ds1b, "nt")
        ddec = dxd * xdt * dec
        db_ref[...] = _dot(dgb, cb, "tn") + _dot((xdt * dec).astype(bf16), ds1b)
        dc_ref[...] = _dot(dgb, bb) + _dot(dzb, s0b)
        ds0 = _dot(dzb, cb, "tn")
        for h in range(hpg):
            hs = slice(h * p, (h + 1) * p)
            ds_scr[hs, :] = ds0[hs, :] + k["et"][h] * ds1[hs, :]
        dxdt = dxdt_diag + dxd * dec
        ddec_h = heads(ddec)
        dcum = w_rows - w_cols + heads(dz * z) - ddec_h
        et_row = jnp.exp(k["cum"][q - 1:q, :])
        dsum = _dot01(jnp.ones((8, n), f32), _dot01(expand, ds1 * s0, "nn", "a"), "nt", "a")[0:1, :]
        dcl = dsum * et_row + jnp.sum(ddec_h, axis=0, keepdims=True)
        rowq = lax.broadcasted_iota(jnp.int32, (q, 1), 0)
        dcum = dcum + (rowq == q - 1).astype(f32) * dcl
        ddt_ref[...] = heads(dxdt * x)
        dadt_ref[...] = _dot01(k["trilf"], dcum, "tn", "a")
        dx_ref[...] = k["d_e"] * dy + dxdt * k["dt_e"]
        dd8 = heads(jnp.broadcast_to(jnp.sum(dy * x, axis=0, keepdims=True), (8, hp)))

        @pl.when(first)
        def _():
            dd_ref[...] = dd8

        @pl.when(jnp.logical_not(first))
        def _():
            dd_ref[...] += dd8

    rc = lambda g, c: (nc - 1 - c, g)
    return pl.pallas_call(
        body, name=name, grid=(g_n, nc),
        in_specs=[x_spec, b_spec, c_spec, dt_spec, dt_spec, d_spec, st_spec, pl.BlockSpec((q, hp), rc)],
        out_specs=[pl.BlockSpec((q, hp), rc), pl.BlockSpec((q, n), rc), pl.BlockSpec((q, n), rc),
                   pl.BlockSpec((q, LANES), rc), pl.BlockSpec((q, LANES), rc), pl.BlockSpec((8, LANES), lambda g, c: (g, 0))],
        out_shape=[jax.ShapeDtypeStruct((T, g_n * hp), f32), jax.ShapeDtypeStruct((T, g_n * n), f32),
                   jax.ShapeDtypeStruct((T, g_n * n), f32), jax.ShapeDtypeStruct((T, g_n * LANES), f32),
                   jax.ShapeDtypeStruct((T, g_n * LANES), f32), jax.ShapeDtypeStruct((g_n * 8, LANES), f32)],
        scratch_shapes=[pltpu.VMEM((hp, n), f32)],
        compiler_params=_params(("parallel", "arbitrary")),
    )(act, act, act, dt, adt, dpad, states, dy)


def _cmul(ar, ai, br, bi):
    return ar * br - ai * bi, ar * bi + ai * br


def _s5_tile_powers(lr, li):
    p = [(lr, li)]
    for _ in range(7):
        p.append(_cmul(p[-1][0], p[-1][1], lr, li))
    tile = (jnp.concatenate([q[0] for q in p], axis=0), jnp.concatenate([q[1] for q in p], axis=0))
    return tile, (p[0], p[1], p[3])


def _s5_tile_scan(xr, xi, steps, reverse):
    row = lax.broadcasted_iota(jnp.int32, xr.shape, 0)
    for d, (pr, pi) in zip((1, 2, 4), steps):
        if reverse:
            keep = row < 8 - d
            sr, si = pltpu.roll(xr, 8 - d, 0), pltpu.roll(xi, 8 - d, 0)
        else:
            keep = row >= d
            sr, si = pltpu.roll(xr, d, 0), pltpu.roll(xi, d, 0)
        sr, si = jnp.where(keep, sr, 0.0), jnp.where(keep, si, 0.0)
        ar, ai = _cmul(sr, si, pr, pi)
        xr, xi = xr + ar, xi + ai
    return xr, xi


def _s5_scan_fwd(bu, lam_re, lam_im, nsb, name, tc_t=512):
    T = bu.shape[0]
    w2 = bu.shape[1] // nsb
    w = w2 // 2
    tc = _pick(T, tc_t, 8)

    def body(bu_ref, lr_ref, li_ref, st_ref, carry):
        @pl.when(pl.program_id(1) == 0)
        def _():
            carry[...] = jnp.zeros_like(carry)

        (pr8, pi8), steps = _s5_tile_powers(lr_ref[0:1, :], li_ref[0:1, :])

        def tile(i, c):
            r = pl.ds(pl.multiple_of(i * 8, 8), 8)
            x = bu_ref[r, :]
            xr, xi = _s5_tile_scan(x[:, :w], x[:, w:], steps, False)
            ar, ai = _cmul(pr8, pi8, c[0], c[1])
            xr, xi = xr + ar, xi + ai
            st_ref[r, :] = jnp.concatenate([xr, xi], axis=1)
            return xr[7:8, :], xi[7:8, :]

        c = lax.fori_loop(0, tc // 8, tile, (carry[0:1, :], carry[1:2, :]), unroll=2)
        carry[0:1, :] = c[0]
        carry[1:2, :] = c[1]

    return pl.pallas_call(
        body, name=name, grid=(nsb, T // tc),
        in_specs=[pl.BlockSpec((tc, w2), lambda j, i: (i, j)), pl.BlockSpec((8, w), lambda j, i: (j, 0)),
                  pl.BlockSpec((8, w), lambda j, i: (j, 0))],
        out_specs=pl.BlockSpec((tc, w2), lambda j, i: (i, j)), out_shape=jax.ShapeDtypeStruct(bu.shape, f32),
        scratch_shapes=[pltpu.VMEM((8, w), f32)],
        compiler_params=_params(("parallel", "arbitrary")),
    )(bu, lam_re, lam_im)


def _s5_scan_bwd(gst, states, lam_re, lam_im, nsb, name, tc_t=512):
    T = gst.shape[0]
    w2 = gst.shape[1] // nsb
    w = w2 // 2
    tc = _pick(T, tc_t, 8)
    nt = T // tc
    n_tiles = tc // 8

    def body(g_ref, s_ref, sp_ref, lr_ref, li_ref, a_ref, dlr_ref, dli_ref, carry, acc):
        chunk = pl.program_id(1)

        @pl.when(chunk == 0)
        def _():
            carry[...] = jnp.zeros_like(carry)
            acc[...] = jnp.zeros_like(acc)

        (qr8, qi8), steps = _s5_tile_powers(lr_ref[0:1, :], -li_ref[0:1, :])
        row = lax.broadcasted_iota(jnp.int32, (8, w), 0)
        rev_r, rev_i = jnp.zeros((8, w), f32), jnp.zeros((8, w), f32)
        for r in range(8):
            rev_r = jnp.where(row == r, qr8[7 - r:8 - r, :], rev_r)
            rev_i = jnp.where(row == r, qi8[7 - r:8 - r, :], rev_i)
        row2 = lax.broadcasted_iota(jnp.int32, (8, w2), 0)

        def tile(k, c):
            ar_in, ai_in, dr, di = c
            i = n_tiles - 1 - k
            r = pl.ds(pl.multiple_of(i * 8, 8), 8)
            x = g_ref[r, :]
            xr, xi = _s5_tile_scan(x[:, :w], x[:, w:], steps, True)
            pr, pi = _cmul(rev_r, rev_i, ar_in, ai_in)
            xr, xi = xr + pr, xi + pi
            a_ref[r, :] = jnp.concatenate([xr, xi], axis=1)
            before = jnp.where(i > 0, s_ref[pl.ds(pl.multiple_of(jnp.maximum(i - 1, 0) * 8, 8), 8), :],
                               sp_ref[tc - 8:tc, :] * (chunk < nt - 1).astype(f32))
            prev = jnp.where(row2 == 0, pltpu.roll(before, 1, 0), pltpu.roll(s_ref[r, :], 1, 0))
            spr, spi = prev[:, :w], prev[:, w:]
            return xr[0:1, :], xi[0:1, :], dr + xr * spr + xi * spi, di - xr * spi + xi * spr

        c0 = (carry[0:1, :], carry[1:2, :], acc[0:8, :], acc[8:16, :])
        ar, ai, dr, di = lax.fori_loop(0, n_tiles, tile, c0, unroll=2)
        carry[0:1, :] = ar
        carry[1:2, :] = ai
        acc[0:8, :] = dr
        acc[8:16, :] = di
        dlr_ref[...] = jnp.broadcast_to(jnp.sum(dr, axis=0, keepdims=True), (8, w))
        dli_ref[...] = jnp.broadcast_to(jnp.sum(di, axis=0, keepdims=True), (8, w))

    cur = lambda j, i: (nt - 1 - i, j)
    prv = lambda j, i: (jnp.maximum(nt - 2 - i, 0), j)
    return pl.pallas_call(
        body, name=name, grid=(nsb, nt),
        in_specs=[pl.BlockSpec((tc, w2), cur), pl.BlockSpec((tc, w2), cur), pl.BlockSpec((tc, w2), prv),
                  pl.BlockSpec((8, w), lambda j, i: (j, 0)), pl.BlockSpec((8, w), lambda j, i: (j, 0))],
        out_specs=[pl.BlockSpec((tc, w2), cur), pl.BlockSpec((8, w), lambda j, i: (j, 0)),
                   pl.BlockSpec((8, w), lambda j, i: (j, 0))],
        out_shape=[jax.ShapeDtypeStruct(gst.shape, f32), jax.ShapeDtypeStruct((nsb * 8, w), f32),
                   jax.ShapeDtypeStruct((nsb * 8, w), f32)],
        scratch_shapes=[pltpu.VMEM((8, w), f32), pltpu.VMEM((16, w), f32)],
        compiler_params=_params(("parallel", "arbitrary")),
    )(gst, states, states, lam_re, lam_im)


def _s5_prep_fn(xs, ps):
    lam_re, lam_im, log_step, b_re, b_im, expand = ps
    lr = jnp.minimum(lam_re, S5_MAX_REAL)
    li = lam_im
    step = jnp.exp(log_step)
    er = jnp.exp(lr * step)
    ang = li * step
    lbr, lbi = er * jnp.cos(ang), er * jnp.sin(ang)
    nr, ni = lbr - 1.0, lbi
    den = lr * lr + li * li
    qr, qi = (nr * lr + ni * li) / den, (ni * lr - nr * li) / den
    qre, qie = _dot(qr, expand, "nn", _HI), _dot(qi, expand, "nn", _HI)
    return [lbr, lbi, qre * b_re - qie * b_im, qre * b_im + qie * b_re]


def _s5_prep(pars, name):
    def body(*refs):
        outs = _s5_prep_fn([], [r[...] for r in refs[:6]])
        for ref, v in zip(refs[6:], outs):
            ref[...] = v

    g, nst = pars[0].shape
    nc = pars[3].shape[1]
    return pl.pallas_call(
        body, name=name,
        out_shape=[jax.ShapeDtypeStruct((g, nst), f32)] * 2 + [jax.ShapeDtypeStruct((g, nc), f32)] * 2,
        compiler_params=_params(),
    )(*pars)


def _s5_prep_bwd(pars, cots, name):
    def body(*refs):
        ps = [r[...] for r in refs[:6]]
        ct = [r[...] for r in refs[6:10]]
        _, vjp = jax.vjp(lambda *a: _s5_prep_fn([], list(a)), *ps)
        g = vjp(ct)
        for ref, v in zip(refs[10:], g[:5]):
            ref[...] = v

    return pl.pallas_call(
        body, name=name, out_shape=[jax.ShapeDtypeStruct(p.shape, f32) for p in pars[:5]], compiler_params=_params(),
    )(*pars, *cots)


_ANY = pl.BlockSpec(memory_space=pl.ANY)


def _remote(src, dst, send_sem, recv_sem, device):
    return pltpu.make_async_remote_copy(src_ref=src, dst_ref=dst, send_sem=send_sem, recv_sem=recv_sem, device_id=device,
                                        device_id_type=MESH)


def _staged_copy(src, dst, buf, in_sems, out_sems):
    n = D2D_STREAMS
    piece = src.shape[0] // n
    assert src.shape[0] % n == 0

    def load(i):
        return pltpu.make_async_copy(src.at[pl.ds(i * piece, piece)], buf.at[i % 2], in_sems.at[i % 2])

    def store(i):
        return pltpu.make_async_copy(buf.at[i % 2], dst.at[pl.ds(i * piece, piece)], out_sems.at[i % 2])

    load(0).start()
    for i in range(n):
        if i + 1 < n:
            if i >= 1:
                store(i - 1).wait()
            load(i + 1).start()
        load(i).wait()
        store(i).start()
    store(n - 2).wait()
    store(n - 1).wait()


def _stage_scratch(rows, cols, dtype):
    return [pltpu.VMEM((2, rows // D2D_STREAMS, cols), dtype), pltpu.SemaphoreType.DMA((2,)), pltpu.SemaphoreType.DMA((2,))]


def _chip_all_gather(block, name):
    rows = block.shape[0]
    half = rows // 2
    piece = half // D2D_STREAMS
    assert rows % (2 * D2D_STREAMS * 16) == 0

    def body(src, out, ici_send, ici_recv, d2d_send, d2d_recv, *stage):
        x, y, c = lax.axis_index("x"), lax.axis_index("y"), lax.axis_index("c")
        me = 2 * x + y
        sibling = (x, y, 1 - c)
        chips = [(1 - x, y), (x, 1 - y), (1 - x, 1 - y)]
        mine = pl.ds(pl.multiple_of(c * half, 16), half)
        sends = []
        for j, (px, py) in enumerate(chips):
            cp = _remote(src.at[mine], out.at[me, mine], ici_send.at[j], ici_recv.at[j], (px, py, c))
            cp.start()
            sends.append(cp)
        _staged_copy(src, out.at[me], *stage)
        for j, (px, py) in enumerate(chips):
            slot = 2 * px + py
            _remote(src.at[mine], out.at[slot, mine], ici_send.at[j], ici_recv.at[j], (px, py, c)).wait_recv()
            for s in range(D2D_STREAMS):
                r = pl.ds(pl.multiple_of(c * half + s * piece, 16), piece)
                k = j * D2D_STREAMS + s
                cp = _remote(out.at[slot, r], out.at[slot, r], d2d_send.at[k], d2d_recv.at[k], sibling)
                cp.start()
                sends.append(cp)
        for j, (px, py) in enumerate(chips):
            slot = 2 * px + py
            for s in range(D2D_STREAMS):
                r = pl.ds(pl.multiple_of((1 - c) * half + s * piece, 16), piece)
                k = j * D2D_STREAMS + s
                _remote(out.at[slot, r], out.at[slot, r], d2d_send.at[k], d2d_recv.at[k], sibling).wait_recv()
        for cp in sends:
            cp.wait_send()

    n_d2d = 3 * D2D_STREAMS
    return pl.pallas_call(
        body, name=name, in_specs=[_ANY], out_specs=_ANY,
        out_shape=jax.ShapeDtypeStruct((N_CHIPS,) + block.shape, block.dtype),
        scratch_shapes=[pltpu.SemaphoreType.DMA((3,)), pltpu.SemaphoreType.DMA((3,)), pltpu.SemaphoreType.DMA((n_d2d,)),
                        pltpu.SemaphoreType.DMA((n_d2d,))] + _stage_scratch(rows, block.shape[1], block.dtype),
    )(block)


def _chip_scatter(parts, name):
    def body(src, out, send_sems, recv_sems, *stage):
        x, y, c = lax.axis_index("x"), lax.axis_index("y"), lax.axis_index("c")
        me = 2 * x + y
        chips = [(1 - x, y), (x, 1 - y), (1 - x, 1 - y)]
        sends = []
        for j, (px, py) in enumerate(chips):
            cp = pltpu.make_async_remote_copy(src_ref=src.at[2 * px + py], dst_ref=out.at[me], send_sem=send_sems.at[j],
                                              recv_sem=recv_sems.at[j], device_id=(px, py, c), device_id_type=MESH)
            cp.start()
            sends.append(cp)
        _staged_copy(src.at[me], out.at[me], *stage)
        for j, (px, py) in enumerate(chips):
            pltpu.make_async_remote_copy(src_ref=src.at[me], dst_ref=out.at[2 * px + py], send_sem=send_sems.at[j],
                                         recv_sem=recv_sems.at[j], device_id=(px, py, c), device_id_type=MESH).wait_recv()
        for cp in sends:
            cp.wait_send()

    return pl.pallas_call(
        body, name=name, in_specs=[_ANY], out_specs=_ANY, out_shape=jax.ShapeDtypeStruct(parts.shape, parts.dtype),
        scratch_shapes=[pltpu.SemaphoreType.DMA((3,)), pltpu.SemaphoreType.DMA((3,))]
        + _stage_scratch(parts.shape[1], parts.shape[2], parts.dtype),
    )(parts)


_HBM = pl.BlockSpec(memory_space=pltpu.HBM)
_SEM = pl.BlockSpec(memory_space=pltpu.SEMAPHORE)
_EFFECT = pltpu.SideEffectType.DATAFLOW_SIDE_EFFECTING


def _gather_peers():
    x, y, c = lax.axis_index("x"), lax.axis_index("y"), lax.axis_index("c")
    return x, y, c, 2 * x + y, [(1 - x, y), (x, 1 - y), (1 - x, 1 - y)]


def _chip_gather_start(block, name):
    half = block.shape[0] // 2

    def body(src, land, send_sems, recv_sems, src_out, land_out, token):
        x, y, c, me, chips = _gather_peers()
        mine = pl.ds(pl.multiple_of(c * half, 16), half)
        for j, (px, py) in enumerate(chips):
            _remote(src.at[mine], land.at[me, mine], send_sems.at[j], recv_sems.at[j], (px, py, c)).start()
        token[...] = jnp.zeros_like(token)

    land_shape = (N_CHIPS,) + block.shape
    return pl.pallas_call(
        body, name=name,
        out_shape=(pltpu.SemaphoreType.DMA((3,)), pltpu.SemaphoreType.DMA((3,)), pltpu.HBM(block.shape, block.dtype),
                   pltpu.HBM(land_shape, block.dtype), jax.ShapeDtypeStruct((8, LANES), f32)),
        in_specs=(_HBM, _HBM), out_specs=(_SEM, _SEM, _HBM, _HBM, pl.BlockSpec(memory_space=pltpu.VMEM)),
        input_output_aliases={0: 2, 1: 3}, compiler_params=pltpu.CompilerParams(has_side_effects=_EFFECT),
    )(pltpu.with_memory_space_constraint(block, pltpu.HBM),
      pltpu.with_memory_space_constraint(lax.empty(land_shape, block.dtype), pltpu.HBM))


def _chip_gather_wait(send_sems, recv_sems, block, land, after, name):
    half = block.shape[0] // 2

    def body(src, land_ref, send_ref, recv_ref, after_ref, src_dead, land_out):
        x, y, c, me, chips = _gather_peers()
        mine = pl.ds(pl.multiple_of(c * half, 16), half)
        for j, (px, py) in enumerate(chips):
            cp = _remote(src.at[mine], land_ref.at[2 * px + py, mine], send_ref.at[j], recv_ref.at[j], (px, py, c))
            cp.wait_send()
            cp.wait_recv()

    return pl.pallas_call(
        body, name=name, out_shape=(pltpu.HBM(block.shape, block.dtype), pltpu.HBM(land.shape, land.dtype)),
        in_specs=(_HBM, _HBM, _SEM, _SEM, _ANY), out_specs=(_HBM, _HBM), input_output_aliases={0: 0, 1: 1},
        compiler_params=pltpu.CompilerParams(has_side_effects=_EFFECT),
    )(block, land, send_sems, recv_sems, after)


def _chip_gather_finish(block, land, name):
    rows = block.shape[0]
    half = rows // 2
    piece = half // D2D_STREAMS

    def body(src, land_ref, out, d2d_send, d2d_recv, *stage):
        x, y, c, me, chips = _gather_peers()
        sibling = (x, y, 1 - c)
        sends = []
        for j, (px, py) in enumerate(chips):
            slot = 2 * px + py
            for s in range(D2D_STREAMS):
                r = pl.ds(pl.multiple_of(c * half + s * piece, 16), piece)
                k = j * D2D_STREAMS + s
                cp = _remote(land_ref.at[slot, r], out.at[slot, r], d2d_send.at[k], d2d_recv.at[k], sibling)
                cp.start()
                sends.append(cp)
        _staged_copy(src, out.at[me], *stage)
        for j, (px, py) in enumerate(chips):
            slot = 2 * px + py
            for s in range(D2D_STREAMS):
                r = pl.ds(pl.multiple_of((1 - c) * half + s * piece, 16), piece)
                k = j * D2D_STREAMS + s
                _remote(land_ref.at[slot, r], out.at[slot, r], d2d_send.at[k], d2d_recv.at[k], sibling).wait_recv()
        for cp in sends:
            cp.wait_send()

    n_d2d = 3 * D2D_STREAMS
    return pl.pallas_call(
        body, name=name, in_specs=[_ANY, _ANY], out_specs=_ANY, out_shape=jax.ShapeDtypeStruct(land.shape, land.dtype),
        input_output_aliases={1: 0},
        scratch_shapes=[pltpu.SemaphoreType.DMA((n_d2d,)), pltpu.SemaphoreType.DMA((n_d2d,))]
        + _stage_scratch(rows, block.shape[1], block.dtype),
    )(block, land)


def _chip_scatter_start(parts, name):
    def body(src, land, send_sems, recv_sems, src_out, land_out, token):
        x, y, c = lax.axis_index("x"), lax.axis_index("y"), lax.axis_index("c")
        me = 2 * x + y
        for j, (px, py) in enumerate([(1 - x, y), (x, 1 - y), (1 - x, 1 - y)]):
            _remote(src.at[2 * px + py], land.at[me], send_sems.at[j], recv_sems.at[j], (px, py, c)).start()
        token[...] = jnp.zeros_like(token)

    return pl.pallas_call(
        body, name=name,
        out_shape=(pltpu.SemaphoreType.DMA((3,)), pltpu.SemaphoreType.DMA((3,)), pltpu.HBM(parts.shape, parts.dtype),
                   pltpu.HBM(parts.shape, parts.dtype), jax.ShapeDtypeStruct((8, LANES), f32)),
        in_specs=(_HBM, _HBM), out_specs=(_SEM, _SEM, _HBM, _HBM, pl.BlockSpec(memory_space=pltpu.VMEM)),
        input_output_aliases={0: 2, 1: 3}, compiler_params=pltpu.CompilerParams(has_side_effects=_EFFECT),
    )(pltpu.with_memory_space_constraint(parts, pltpu.HBM),
      pltpu.with_memory_space_constraint(lax.empty(parts.shape, parts.dtype), pltpu.HBM))


def _chip_scatter_wait(send_sems, recv_sems, parts, land, after, name):
    def body(src, land_ref, send_ref, recv_ref, after_ref, src_dead, land_out):
        x, y, c = lax.axis_index("x"), lax.axis_index("y"), lax.axis_index("c")
        me = 2 * x + y
        for j, (px, py) in enumerate([(1 - x, y), (x, 1 - y), (1 - x, 1 - y)]):
            cp = _remote(src.at[2 * px + py], land_ref.at[2 * px + py], send_ref.at[j], recv_ref.at[j], (px, py, c))
            cp.wait_send()
            cp.wait_recv()

    return pl.pallas_call(
        body, name=name, out_shape=(pltpu.HBM(parts.shape, parts.dtype), pltpu.HBM(land.shape, land.dtype)),
        in_specs=(_HBM, _HBM, _SEM, _SEM, _ANY), out_specs=(_HBM, _HBM), input_output_aliases={0: 0, 1: 1},
        compiler_params=pltpu.CompilerParams(has_side_effects=_EFFECT),
    )(parts, land, send_sems, recv_sems, after)


def _sum_slots_own(landed, own, chip, name, block_rows=512):
    s_n, r_n, c_n = landed.shape
    br = _pick(r_n, block_rows, 8)

    def body(chip_ref, land_ref, own_ref, o_ref):
        acc = jnp.zeros((br, c_n), f32)
        for s in range(s_n):
            acc = acc + jnp.where(chip_ref[0] == s, own_ref[s], land_ref[s]).astype(f32)
        o_ref[...] = acc

    spec = pl.BlockSpec((s_n, br, c_n), lambda i, c: (0, i, 0))
    grid_spec = pltpu.PrefetchScalarGridSpec(num_scalar_prefetch=1, grid=(r_n // br,), in_specs=[spec, spec],
                                             out_specs=pl.BlockSpec((br, c_n), lambda i, c: (i, 0)))
    return pl.pallas_call(
        body, name=name, grid_spec=grid_spec, out_shape=jax.ShapeDtypeStruct((r_n, c_n), f32),
        compiler_params=_params(("parallel",)),
    )(chip, landed, own)


def _core_send_other_half(parts, name):
    n_slots, rows, cols = parts.shape
    half = rows // 2
    piece = half // D2D_STREAMS
    assert rows % (2 * D2D_STREAMS * 16) == 0

    def body(src, out, send_sems, recv_sems):
        x, y, c = lax.axis_index("x"), lax.axis_index("y"), lax.axis_index("c")
        sibling = (x, y, 1 - c)
        sends = []
        for k in range(n_slots):
            for s in range(D2D_STREAMS):
                theirs = pl.ds(pl.multiple_of((1 - c) * half + s * piece, 16), piece)
                i = k * D2D_STREAMS + s
                cp = _remote(src.at[k, theirs], out.at[k, pl.ds(s * piece, piece)], send_sems.at[i], recv_sems.at[i], sibling)
                cp.start()
                sends.append(cp)
        for cp in sends:
            cp.wait_recv()
        for cp in sends:
            cp.wait_send()

    n = n_slots * D2D_STREAMS
    return pl.pallas_call(
        body, name=name, in_specs=[_ANY], out_specs=_ANY, out_shape=jax.ShapeDtypeStruct((n_slots, half, cols), parts.dtype),
        scratch_shapes=[pltpu.SemaphoreType.DMA((n,)), pltpu.SemaphoreType.DMA((n,))],
    )(parts)


def _other_half_copies(src, land, send_sems, recv_sems):
    n_slots, rows, _ = src.shape
    half = rows // 2
    piece = half // D2D_STREAMS
    x, y, c = lax.axis_index("x"), lax.axis_index("y"), lax.axis_index("c")
    copies = []
    for k in range(n_slots):
        for s in range(D2D_STREAMS):
            theirs = pl.ds(pl.multiple_of((1 - c) * half + s * piece, 16), piece)
            i = k * D2D_STREAMS + s
            copies.append(_remote(src.at[k, theirs], land.at[k, pl.ds(s * piece, piece)], send_sems.at[i], recv_sems.at[i],
                                  (x, y, 1 - c)))
    return copies


def _core_send_other_half_start(parts, name):
    n_slots, rows, cols = parts.shape
    assert rows % (2 * D2D_STREAMS * 16) == 0
    n = n_slots * D2D_STREAMS
    land_shape = (n_slots, rows // 2, cols)

    def body(src, land, send_sems, recv_sems, src_out, land_out, token):
        for cp in _other_half_copies(src, land, send_sems, recv_sems):
            cp.start()
        token[...] = jnp.zeros_like(token)

    return pl.pallas_call(
        body, name=name,
        out_shape=(pltpu.SemaphoreType.DMA((n,)), pltpu.SemaphoreType.DMA((n,)), pltpu.HBM(parts.shape, parts.dtype),
                   pltpu.HBM(land_shape, parts.dtype), jax.ShapeDtypeStruct((8, LANES), f32)),
        in_specs=(_HBM, _HBM), out_specs=(_SEM, _SEM, _HBM, _HBM, pl.BlockSpec(memory_space=pltpu.VMEM)),
        input_output_aliases={0: 2, 1: 3}, compiler_params=pltpu.CompilerParams(has_side_effects=_EFFECT),
    )(pltpu.with_memory_space_constraint(parts, pltpu.HBM),
      pltpu.with_memory_space_constraint(lax.empty(land_shape, parts.dtype), pltpu.HBM))


def _core_send_other_half_wait(send_sems, recv_sems, parts, land, after, name):
    def body(src, land_ref, send_ref, recv_ref, after_ref, src_dead, land_out):
        for cp in _other_half_copies(src, land_ref, send_ref, recv_ref):
            cp.wait_send()
            cp.wait_recv()

    return pl.pallas_call(
        body, name=name, out_shape=(pltpu.HBM(parts.shape, parts.dtype), pltpu.HBM(land.shape, land.dtype)),
        in_specs=(_HBM, _HBM, _SEM, _SEM, _ANY), out_specs=(_HBM, _HBM), input_output_aliases={0: 0, 1: 1},
        compiler_params=pltpu.CompilerParams(has_side_effects=_EFFECT),
    )(parts, land, send_sems, recv_sems, after)


def _add_my_half(parts, other, core, name, block_rows=1024):
    n_slots, rows, cols = parts.shape
    half = rows // 2
    br = _pick(half, block_rows, 16)
    nb = half // br

    def body(c_ref, a_ref, b_ref, o_ref):
        o_ref[...] = (a_ref[...].astype(f32) + b_ref[...].astype(f32)).astype(o_ref.dtype)

    grid_spec = pltpu.PrefetchScalarGridSpec(
        num_scalar_prefetch=1, grid=(n_slots, nb),
        in_specs=[pl.BlockSpec((1, br, cols), lambda k, i, c: (k, c[0] * nb + i, 0)),
                  pl.BlockSpec((1, br, cols), lambda k, i, c: (k, i, 0))],
        out_specs=pl.BlockSpec((1, br, cols), lambda k, i, c: (k, i, 0)))
    return pl.pallas_call(
        body, name=name, grid_spec=grid_spec, out_shape=jax.ShapeDtypeStruct((n_slots, half, cols), parts.dtype),
        compiler_params=_params(("parallel", "parallel")),
    )(core, parts, other)


def _core_join_halves(mine, name):
    half, cols = mine.shape
    piece = half // D2D_STREAMS
    assert half % (D2D_STREAMS * 16) == 0

    def body(src, out, send_sems, recv_sems, *stage):
        x, y, c = lax.axis_index("x"), lax.axis_index("y"), lax.axis_index("c")
        sibling = (x, y, 1 - c)
        sends = []
        for s in range(D2D_STREAMS):
            dst = out.at[pl.ds(pl.multiple_of(c * half + s * piece, 16), piece)]
            cp = _remote(src.at[pl.ds(s * piece, piece)], dst, send_sems.at[s], recv_sems.at[s], sibling)
            cp.start()
            sends.append(cp)
        _staged_copy(src, out.at[pl.ds(pl.multiple_of(c * half, 16), half)], *stage)
        for s in range(D2D_STREAMS):
            dst = out.at[pl.ds(pl.multiple_of((1 - c) * half + s * piece, 16), piece)]
            _remote(src.at[pl.ds(s * piece, piece)], dst, send_sems.at[s], recv_sems.at[s], sibling).wait_recv()
        for cp in sends:
            cp.wait_send()

    return pl.pallas_call(
        body, name=name, in_specs=[_ANY], out_specs=_ANY, out_shape=jax.ShapeDtypeStruct((2 * half, cols), mine.dtype),
        scratch_shapes=[pltpu.SemaphoreType.DMA((D2D_STREAMS,)), pltpu.SemaphoreType.DMA((D2D_STREAMS,))]
        + _stage_scratch(half, cols, mine.dtype),
    )(mine)


def _reduce_start(exchanging, core, after, tag):
    parts, other = _core_send_other_half_wait(*exchanging[:4], after, f"exchange_wait_{tag}")
    return _chip_scatter_start(_add_my_half(parts, other, core, f"sum_core_halves_{tag}"), f"scatter_start_{tag}")


def _reduce_finish(started, chip, after, tag):
    send_sems, recv_sems, chip_part, land, _ = started
    own, landed = _chip_scatter_wait(send_sems, recv_sems, chip_part, land, after, f"scatter_wait_{tag}")
    return _core_join_halves(_sum_slots_own(landed, own, chip, f"sum_chip_parts_{tag}"), f"join_core_halves_{tag}")


def _reduce_to_chips(parts, core, tag):
    chip_part = _add_my_half(parts, _core_send_other_half(parts, f"exchange_core_halves_{tag}"), core, f"sum_core_halves_{tag}")
    my_sum = _sum_slots(_chip_scatter(chip_part, f"scatter_{tag}"), f"sum_chip_parts_{tag}")
    return _core_join_halves(my_sum, f"join_core_halves_{tag}")


def _sum_slots(stack, name, block_rows=512):
    s_n, r_n, c_n = stack.shape
    br = _pick(r_n, block_rows, 8)

    def body(in_ref, o_ref):
        acc = in_ref[0].astype(f32)
        for s in range(1, s_n):
            acc = acc + in_ref[s].astype(f32)
        o_ref[...] = acc

    return pl.pallas_call(
        body, name=name, grid=(r_n // br,), in_specs=[pl.BlockSpec((s_n, br, c_n), lambda i: (0, i, 0))],
        out_specs=pl.BlockSpec((br, c_n), lambda i: (i, 0)), out_shape=jax.ShapeDtypeStruct((r_n, c_n), f32),
        compiler_params=_params(("parallel",)),
    )(stack)


def _concat_padded(parts, mult):
    rows = sum(p.shape[0] for p in parts)
    pad = (-rows) % mult
    if pad:
        parts = parts + [jnp.zeros((pad,) + parts[0].shape[1:], parts[0].dtype)]
    return jnp.concatenate(parts, axis=0)


def _pack_weights(w, l, names, conv_w=None):
    parts = [w[n][l].astype(bf16).reshape(-1, PACK_COLS) for n in names]
    if conv_w is not None:
        parts.append(_concat_padded([lax.bitcast_convert_type(conv_w, bf16).reshape(-1, PACK_COLS)], 16))
    return _concat_padded(parts, PACK_ROW_MULT)


def _unpack_weights(full, w, l, names, conv_w=None):
    start, r0 = {}, 0
    for n in names:
        start[n] = r0
        r0 += w[n][l].size // PACK_COLS

    def shards(n):
        rows = w[n][l].size // PACK_COLS
        return [full[k, start[n]:start[n] + rows].reshape(w[n].shape[1:]) for k in range(N_CHIPS)]

    if conv_w is None:
        return shards, None
    rows = conv_w.size * 2 // PACK_COLS
    pieces = lax.bitcast_convert_type(full[:, r0:r0 + rows].reshape((N_CHIPS,) + conv_w.shape + (2,)), f32)
    return shards, jnp.concatenate([pieces[k] for k in range(N_CHIPS)], axis=2)


def _pack_big_grads(layer_grads):
    parts, slot_rows = [], 0
    for k in range(N_CHIPS):
        slot = []
        for n in BIG:
            for g in layer_grads:
                width = g[n].shape[0] // N_CHIPS
                slot.append(g[n][k * width:(k + 1) * width].astype(bf16).reshape(-1, PACK_COLS))
        slot_rows = sum(p.shape[0] for p in slot)
        pad = (-slot_rows) % PACK_ROW_MULT
        if pad:
            slot.append(jnp.zeros((pad, PACK_COLS), bf16))
        slot_rows += pad
        parts += slot
    return jnp.concatenate(parts, axis=0).reshape(N_CHIPS, slot_rows, PACK_COLS)


def _unpack_big_grads(summed, w):
    out, r0 = {}, 0
    for n in BIG:
        rows = w[n][0].size // PACK_COLS
        out[n] = [s[r0:r0 + rows].reshape(w[n].shape[1:]) for s in summed]
        r0 += rows
    return out


_SMALL_TILE = 8 * LANES


def _pack_small(vals, names):
    parts = []
    for n in names:
        pieces = vals[n] if isinstance(vals[n], list) else [vals[n]]
        size = sum(p.size for p in pieces)
        if all(p.size % _SMALL_TILE == 0 for p in pieces):
            parts += [p.reshape(-1, LANES) for p in pieces]
        else:
            flat = [p.reshape(-1) for p in pieces] + [jnp.zeros(((-size) % _SMALL_TILE,), f32)]
            parts.append(jnp.concatenate(flat).reshape(-1, LANES))
    return _concat_padded(parts, PACK_ROW_MULT)


def _unpack_small(packed, like, names):
    out, r0 = {}, 0
    for n in names:
        size = like[n].size
        rows = -(-size // _SMALL_TILE) * 8
        out[n] = packed[r0:r0 + rows].reshape(-1)[:size].reshape(like[n].shape)
        r0 += rows
    return out


def _dims(w, x):
    d = {}
    d["D"] = x.shape[-1]
    d["T"] = x.shape[-2]
    d["DI"] = w["ssd_norm_g"].shape[-1]
    d["NH"] = w["ssd_dt_bias"].shape[-1]
    d["CD"] = w["ssd_conv_b"].shape[-1]
    d["G"] = SSD_N_GROUPS
    d["HPG"] = d["NH"] // d["G"]
    d["P"] = d["DI"] // d["NH"]
    d["N"] = (d["CD"] - d["DI"]) // (2 * d["G"])
    d["S5G"], d["S5N"] = w["s5_lambda_re"].shape[-2:]
    d["S5C"] = w["s5_b_re"].shape[-1]
    d["S5W"] = d["S5G"] * d["S5C"]
    d["NSB"] = d["S5W"] // S5_SUPERBLOCK
    d["GSB"] = d["S5G"] // d["NSB"]
    return d


def _head_pad(v, d):
    lead = v.shape[:-1]
    v = v.reshape(lead + (d["G"], d["HPG"]))
    v = jnp.concatenate([v, jnp.zeros(lead + (d["G"], LANES - d["HPG"]), v.dtype)], axis=-1)
    return v.reshape(lead + (d["G"] * LANES,))


def _head_unpad(v, d):
    lead = v.shape[:-1]
    return v.reshape(lead + (d["G"], LANES))[..., :d["HPG"]].reshape(lead + (d["NH"],))


def _w_in_perm(shards, d):
    o, nh = d["DI"] + d["CD"], d["NH"]
    r = shards[0].shape[0]

    def rows(lo, hi):
        out = []
        for k, s in enumerate(shards):
            a, b = max(lo, k * r), min(hi, (k + 1) * r)
            if a < b:
                out.append(s[a - k * r:b - k * r])
        return out

    dt = _head_pad(jnp.concatenate(rows(o, o + nh), axis=0).T, d).T
    return jnp.concatenate(rows(0, o) + rows(o + nh, len(shards) * r) + [dt], axis=0)


def _w_in_unperm(g, d):
    o = d["DI"] + d["CD"]
    rest = d["S5W"] + 2 * d["D"]
    return jnp.concatenate([g[:o], _head_unpad(g[o + rest:].T, d).T, g[o:o + rest]], axis=0)


def _s5_block_diag(v, d):
    gsb = d["GSB"]
    g, a, b = v.shape
    row_group = (lax.broadcasted_iota(jnp.int32, (g * a, gsb * b), 0) // a) % gsb
    col_group = lax.broadcasted_iota(jnp.int32, (g * a, gsb * b), 1) // b
    return jnp.where(row_group == col_group, jnp.tile(v.reshape(g * a, b), (1, gsb)), 0)


def _s5_diag_blocks(m, d, a, b):
    gsb = d["GSB"]
    rows = m.shape[0]
    m = m.reshape(rows, gsb, b)
    row_group = (lax.broadcasted_iota(jnp.int32, (rows, gsb, 1), 0) // a) % gsb
    col_group = lax.broadcasted_iota(jnp.int32, (rows, gsb, 1), 1)
    return jnp.sum(jnp.where(row_group == col_group, m, 0), axis=1).reshape(rows // a, a, b)


def _s5_lam_rows(v, d):
    v = v.reshape(d["NSB"], 1, d["GSB"] * d["S5N"])
    return jnp.broadcast_to(v, (d["NSB"], 8, v.shape[-1])).reshape(d["NSB"] * 8, -1)


def _ffn_fwd(h, pre_g, post_g, wgu, wd, tag):
    D = h.shape[1]
    H2 = wgu.shape[0]
    xn = _row_kernel(f"{tag}_norm", _fwd_of(_f_norm), [(h, D, 0)], [pre_g], [(D, bf16)])[0]
    ab = _mm(xn, wgu, "nt", bf16, f"{tag}_mm_up")
    hid = _row_kernel(f"{tag}_swiglu", _swiglu_fwd, [(ab, H2, 0)], [], [(H2 // 2, bf16)])[0]
    f = _mm(hid, wd, "nn", f32, f"{tag}_mm_down")
    out = _row_kernel(f"{tag}_resnorm", _fwd_of(_f_resnorm(0.5)), [(h, D, 0), (f, D, 0)], [post_g], [(D, f32)])[0]
    return out, dict(h=h, xn=xn, ab=ab, hid=hid, f=f)


def _ffn_bwd(dh_out, s, pre_g, post_g, wgu, wd, tag):
    D = dh_out.shape[1]
    H2 = wgu.shape[0]
    df, dpost = _row_kernel(f"{tag}_resnorm_bwd", _vjp_of(_f_post(0.5), 1, 1, [0]), [(s["f"], D, 0), (dh_out, D, 0)],
                            [post_g], [(D, bf16)], [post_g.shape])
    dwd = _mm(s["hid"], df, "tn", bf16, f"{tag}_mm_dwd")
    dhid = _mm(df, wd, "nt", bf16, f"{tag}_mm_dhid")
    dab = _row_kernel(f"{tag}_swiglu_bwd", _swiglu_bwd, [(s["ab"], H2, 0), (dhid, H2 // 2, 0)], [], [(H2, bf16)])[0]
    dwgu = _mm(dab, s["xn"], "tn", bf16, f"{tag}_mm_dwgu")
    dxn = _mm(dab, wgu, "nn", f32, f"{tag}_mm_dxn")
    dh, dpre = _row_kernel(f"{tag}_norm_bwd", _vjp_of(_f_norm, 1, 1, [0], 1), [(s["h"], D, 0), (dxn, D, 0), (dh_out, D, 0)],
                           [pre_g], [(D, f32)], [pre_g.shape])
    return dh, dict(pre_g=dpre, post_g=dpost, wgu=dwgu, wd=dwd)


def _mixer_fwd(h, p, d):
    D, DI, CD, G, N = d["D"], d["DI"], d["CD"], d["G"], d["N"]
    gl = G * LANES
    c_u5, c_ga, c_gb, c_dt = DI + CD, DI + CD + d["S5W"], DI + CD + d["S5W"] + D, DI + CD + d["S5W"] + 2 * D
    u = _row_kernel("mix_norm", _fwd_of(_f_norm), [(h, D, 0)], [p["mix_pre_g"]], [(D, bf16)])[0]
    proj = _mm(u, p["w_in"], "nt", f32, "mix_mm_in", bn_t=512)
    act = _conv_fwd(proj, DI, p["conv_w"], p["conv_b"], "ssd_conv")
    dt, adt = _row_kernel("ssd_dt", _fwd_of(_f_dt), [(proj, gl, c_dt // gl)], [p["dt_bias"], p["a_log"]], [(gl, f32)] * 2)
    y_ssd, states = _ssd_fwd(act, dt, adt, p["d_skip"], d["HPG"], d["P"], N, "ssd_scan")
    nrm = _row_kernel("ssd_post", _fwd_of(_f_ssdpost(G)), [(y_ssd, DI, 0), (proj, DI, 0)], [p["norm_g"]], [(DI, bf16)])[0]
    y_a = _mm(nrm, p["w_a"], "nn", f32, "mix_mm_a")
    u5 =(proj, d["S5W"], c_u5 // d["S5W"])
    bu = _s5_in(proj, c_u5, p["bsb"], d)
    s5st = _s5_scan_fwd(bu, p["lam_re_rows"], p["lam_im_rows"], d["NSB"], "s5_scan")
    y5 = _bdmm(s5st, p["csb"], "nn", d["NSB"], f32, "s5_mm_c")
    gel = _row_kernel("s5_post", _fwd_of(_f_s5post), [(y5, d["S5W"], 0), u5], [p["s5_d"]], [(d["S5W"], bf16)])[0]
    vg = _mm(gel, p["w_glu"], "nt", bf16, "mix_mm_glu")
    glu = _row_kernel("s5_glu", _glu_fwd, [(vg, vg.shape[1], 0)], [], [(vg.shape[1] // 2, bf16)])[0]
    y_b = _mm(glu, p["w_b"], "nn", f32, "mix_mm_b")
    merged = _row_kernel("mix_merge", _fwd_of(_f_merge), [(proj, D, c_ga // D), (y_a, D, 0), (proj, D, c_gb // D), (y_b, D, 0)],
                         [], [(D, bf16)])[0]
    m = _mm(merged, p["w_out"], "nn", f32, "mix_mm_out")
    out = _row_kernel("mix_resnorm", _fwd_of(_f_resnorm(1.0)), [(h, D, 0), (m, D, 0)], [p["mix_post_g"]], [(D, f32)])[0]
    return out, dict(h=h, u=u, proj=proj, act=act, dt=dt, adt=adt, states=states, y_ssd=y_ssd, nrm=nrm, y_a=y_a, s5st=s5st,
                     y5=y5, gel=gel, vg=vg, glu=glu, y_b=y_b, merged=merged, m=m)


def _s5_in(proj, c_u5, bsb, d):
    T = proj.shape[0]
    nsb = d["NSB"]
    ka, nw = S5_SUPERBLOCK, bsb.shape[1]
    off = c_u5 // ka
    assert c_u5 % ka == 0
    bt = _pick(T, 512)

    def body(a_ref, w_ref, o_ref):
        o_ref[...] = _dot(a_ref[...].astype(bf16), w_ref[...].astype(bf16))

    return pl.pallas_call(
        body, name="s5_mm_bu", grid=(nsb, T // bt),
        in_specs=[pl.BlockSpec((bt, ka), lambda j, i: (i, off + j)), pl.BlockSpec((ka, nw), lambda j, i: (j, 0))],
        out_specs=pl.BlockSpec((bt, nw), lambda j, i: (i, j)), out_shape=jax.ShapeDtypeStruct((T, nsb * nw), f32),
        compiler_params=_params(("parallel", "parallel")),
    )(proj, bsb)


def _s5_dbsb(proj, c_u5, a, d):
    T = proj.shape[0]
    nsb = d["NSB"]
    ka, nw = S5_SUPERBLOCK, a.shape[1] // nsb
    off = c_u5 // ka
    bt = _pick(T, 512)

    def body(u_ref, a_ref, o_ref):
        pr = _dot(u_ref[...].astype(bf16), a_ref[...].astype(bf16), "tn")
        k = pl.program_id(1)

        @pl.when(k == 0)
        def _():
            o_ref[...] = pr

        @pl.when(k > 0)
        def _():
            o_ref[...] += pr

    return pl.pallas_call(
        body, name="s5_mm_dbsb", grid=(nsb, T // bt),
        in_specs=[pl.BlockSpec((bt, ka), lambda j, k: (k, off + j)), pl.BlockSpec((bt, nw), lambda j, k: (k, j))],
        out_specs=pl.BlockSpec((ka, nw), lambda j, k: (j, 0)), out_shape=jax.ShapeDtypeStruct((nsb * ka, nw), f32),
        compiler_params=_params(("parallel", "arbitrary")),
    )(proj, a)


def _mixer_bwd(dh_out, s, p, d):
    D, DI, CD, G, N, S5W = d["D"], d["DI"], d["CD"], d["G"], d["N"], d["S5W"]
    gl = G * LANES
    gn = G * N
    c_u5, c_ga, c_gb, c_dt = DI + CD, DI + CD + S5W, DI + CD + S5W + D, DI + CD + S5W + 2 * D
    proj = s["proj"]
    g = {}
    dm, g["mix_post_g"] = _row_kernel("mix_resnorm_bwd", _vjp_of(_f_post(1.0), 1, 1, [0]), [(s["m"], D, 0), (dh_out, D, 0)],
                                      [p["mix_post_g"]], [(D, bf16)], [p["mix_post_g"].shape])
    g["w_out"] = _mm(s["merged"], dm, "tn", bf16, "mix_mm_dwout")
    dmerged = _mm(dm, p["w_out"], "nt", f32, "mix_mm_dmerged")
    dga, dya, dgb, dyb = _row_kernel(
        "mix_merge_bwd", _vjp_of(_f_merge, 4, 1, [0, 1, 2, 3]),
        [(proj, D, c_ga // D), (s["y_a"], D, 0), (proj, D, c_gb // D), (s["y_b"], D, 0), (dmerged, D, 0)], [],
        [(D, bf16), (D, bf16), (D, bf16), (D, bf16)])
    g["w_a"] = _mm(s["nrm"], dya, "tn", bf16, "mix_mm_dwa")
    dnrm = _mm(dya, p["w_a"], "nt", f32, "mix_mm_dnrm")
    dy_ssd, dz, g["norm_g"] = _row_kernel(
        "ssd_post_bwd", _vjp_of(_f_ssdpost(G), 2, 1, [0, 1]), [(s["y_ssd"], DI, 0), (proj, DI, 0), (dnrm, DI, 0)],
        [p["norm_g"]], [(DI, f32), (DI, bf16)], [p["norm_g"].shape])
    dxs, d_b, d_c, ddt, dadt, dd = _ssd_bwd(s["act"], s["dt"], s["adt"], p["d_skip"], s["states"], dy_ssd,
                                            d["HPG"], d["P"], N, "ssd_scan_bwd")
    g["d_skip"] = dd.reshape(G, 8, LANES)[:, 0, :].reshape(1, gl)
    ddt_raw, g["dt_bias"], g["a_log"] = _row_kernel(
        "ssd_dt_bwd", _vjp_of(_f_dt, 1, 2, [0]), [(proj, gl, c_dt // gl), (ddt, gl, 0), (dadt, gl, 0)],
        [p["dt_bias"], p["a_log"]], [(gl, bf16)], [p["dt_bias"].shape, p["a_log"].shape])
    cw, cb = p["conv_w"], p["conv_b"]
    dxc_x, dw_x, db_x = _conv_bwd(proj, DI, cw[:, :DI], cb[:, :DI], dxs, "ssd_conv_bwd_x")
    dxc_b, dw_b, db_b = _conv_bwd(proj, 2 * DI, cw[:, DI:DI + gn], cb[:, DI:DI + gn], d_b, "ssd_conv_bwd_b")
    dxc_c, dw_c, db_c = _conv_bwd(proj, 2 * DI + gn, cw[:, DI + gn:], cb[:, DI + gn:], d_c, "ssd_conv_bwd_c")
    g["conv_w"] = jnp.concatenate([dw_x, dw_b, dw_c], axis=1)
    g["conv_b"] = jnp.concatenate([db_x, db_b, db_c], axis=1)
    g["w_b"] = _mm(s["glu"], dyb, "tn", bf16, "mix_mm_dwb")
    dglu = _mm(dyb, p["w_b"], "nt", f32, "mix_mm_dglu")
    dvg = _row_kernel("s5_glu_bwd", _glu_bwd, [(s["vg"], s["vg"].shape[1], 0), (dglu, S5W, 0)], [], [(s["vg"].shape[1], bf16)])[0]
    g["w_glu"] = _mm(dvg, s["gel"], "tn", bf16, "mix_mm_dwglu")
    dgel = _mm(dvg, p["w_glu"], "nn", f32, "mix_mm_dgel")
    dy5, du5a, g["s5_d"] = _row_kernel(
        "s5_post_bwd", _vjp_of(_f_s5post, 2, 1, [0, 1]), [(s["y5"], S5W, 0), (proj, S5W, c_u5 // S5W), (dgel, S5W, 0)],
        [p["s5_d"]], [(S5W, bf16), (S5W, f32)], [p["s5_d"].shape])
    g["csb"] = _bdmm(s["s5st"], dy5, "tn", d["NSB"], f32, "s5_mm_dcsb")
    gst = _bdmm(dy5, p["csb"], "nt", d["NSB"], f32, "s5_mm_gst")
    a, g["lam_re_rows"], g["lam_im_rows"] = _s5_scan_bwd(gst, s["s5st"], p["lam_re_rows"], p["lam_im_rows"], d["NSB"], "s5_scan_bwd")
    g["bsb"] = _s5_dbsb(proj, c_u5, a, d)
    du5b = _bdmm(a, p["bsb"], "nt", d["NSB"], f32, "s5_mm_du5")
    du5 = _row_kernel("s5_du5", _add_fn, [(du5a, S5W, 0), (du5b, S5W, 0)], [], [(S5W, bf16)])[0]
    dproj = jnp.concatenate([dz, dxc_x, dxc_b, dxc_c, du5, dga, dgb, ddt_raw], axis=1)
    g["w_in"] = _mm(dproj, s["u"], "tn", bf16, "mix_mm_dwin")
    du = _mm(dproj, p["w_in"], "nn", f32, "mix_mm_du", bk_t=2176)
    dh, g["mix_pre_g"] = _row_kernel("mix_norm_bwd", _vjp_of(_f_norm, 1, 1, [0], 1), [(s["h"], D, 0), (du, D, 0), (dh_out, D, 0)],
                                     [p["mix_pre_g"]], [(D, f32)], [p["mix_pre_g"].shape])
    return dh, g


def _ffn1_params(l, w, wf):
    whole = lambda *names: jnp.concatenate([s for n in names for s in wf(n)], axis=0)
    return dict(ffn1_pre_g=w["ffn1_pre_g"][l].reshape(1, -1), ffn1_post_g=w["ffn1_post_g"][l].reshape(1, -1),
                wgu1=whole("ffn1_w_gate", "ffn1_w_up"), wd1=whole("ffn1_w_down"))


def _ffn2_params(wf):
    whole = lambda *names: jnp.concatenate([s for n in names for s in wf(n)], axis=0)
    return dict(wgu2=whole("ffn2_w_gate", "ffn2_w_up"), wd2=whole("ffn2_w_down"))


def _layer_params(l, w, wf, conv_w_full, d):
    r2 = lambda v: v[l].reshape(1, -1)
    p = {}
    for n in ["mix_pre_g", "mix_post_g", "ffn2_pre_g", "ffn2_post_g", "s5_d"]:
        p[n] = r2(w[n])
    whole = lambda *names: jnp.concatenate([s for n in names for s in wf(n)], axis=0)
    p["w_in"] = _w_in_perm(wf("w_in"), d)
    p["w_a"], p["w_glu"], p["w_b"], p["w_out"] = whole("w_branch_a"), whole("s5_w_glu"), whole("w_branch_b"), whole("w_out")
    p["conv_w"] = conv_w_full[l]
    p["conv_b"] = r2(w["ssd_conv_b"])
    p["dt_bias"] = _head_pad(r2(w["ssd_dt_bias"]), d)
    p["a_log"] = _head_pad(r2(w["ssd_a_log"]), d)
    p["d_skip"] = _head_pad(r2(w["ssd_d"]), d)
    p["norm_g"] = r2(w["ssd_norm_g"])
    g5, n5, c5 = d["S5G"], d["S5N"], d["S5C"]
    expand = jnp.repeat(jnp.eye(n5, dtype=f32), c5, axis=1)
    prep_in = [w["s5_lambda_re"][l], w["s5_lambda_im"][l], w["s5_log_step"][l].reshape(g5, 1),
               w["s5_b_re"][l].reshape(g5, n5 * c5), w["s5_b_im"][l].reshape(g5, n5 * c5), expand]
    lbr, lbi, bbr, bbi = _s5_prep(prep_in, "s5_prep")
    p["s5_prep_in"] = prep_in
    p["lam_re_rows"], p["lam_im_rows"] = _s5_lam_rows(lbr, d), _s5_lam_rows(lbi, d)
    to_cn = lambda v: v.astype(bf16).reshape(g5, n5, c5).transpose(0, 2, 1)
    p["bsb"] = jnp.concatenate([_s5_block_diag(to_cn(bbr), d), _s5_block_diag(to_cn(bbi), d)], axis=1)
    c_re = w["s5_c_re"][l].astype(bf16).transpose(0, 2, 1)
    c_im = w["s5_c_im"][l].astype(bf16).transpose(0, 2, 1)
    nsb = d["NSB"]
    csb = jnp.stack([_s5_block_diag(c_re, d).reshape(nsb, -1, S5_SUPERBLOCK),
                     _s5_block_diag(-c_im, d).reshape(nsb, -1, S5_SUPERBLOCK)], axis=1)
    p["csb"] = csb.reshape(-1, S5_SUPERBLOCK)
    return p


def _s5_param_grads(g, p, d, l):
    g5, n5, c5, nsb, gsb = d["S5G"], d["S5N"], d["S5C"], d["NSB"], d["GSB"]
    wst = gsb * n5
    dbsb = g["bsb"]
    from_cn = lambda v: v.transpose(0, 2, 1).reshape(g5, n5 * c5)
    dbbr = from_cn(_s5_diag_blocks(dbsb[:, :wst], d, c5, n5))
    dbbi = from_cn(_s5_diag_blocks(dbsb[:, wst:], d, c5, n5))
    rows = lambda v: v.reshape(nsb, 8, wst)[:, 0, :].reshape(g5, n5)
    cots = [rows(g["lam_re_rows"]), rows(g["lam_im_rows"]), dbbr, dbbi]
    dlr, dli, dls, dbr, dbi = _s5_prep_bwd(p["s5_prep_in"], cots, "s5_prep_bwd")
    dcsb = g["csb"].reshape(nsb, 2, wst, S5_SUPERBLOCK)
    dcr = _s5_diag_blocks(dcsb[:, 0].reshape(-1, S5_SUPERBLOCK), d, n5, c5).transpose(0, 2, 1)
    dci = -_s5_diag_blocks(dcsb[:, 1].reshape(-1, S5_SUPERBLOCK), d, n5, c5).transpose(0, 2, 1)
    return dict(s5_lambda_re=dlr, s5_lambda_im=dli, s5_log_step=dls.reshape(g5), s5_b_re=dbr.reshape(g5, n5, c5),
                s5_b_im=dbi.reshape(g5, n5, c5), s5_c_re=dcr, s5_c_im=dci)


def kernel(x, ffn1_pre_g, ffn1_post_g, ffn1_w_gate, ffn1_w_up, ffn1_w_down, mix_pre_g, mix_post_g, w_in, ssd_conv_w, ssd_conv_b, ssd_dt_bias, ssd_a_log, ssd_d, ssd_norm_g, w_branch_a, s5_lambda_re, s5_lambda_im, s5_b_re, s5_b_im, s5_c_re, s5_c_im, s5_log_step, s5_d, s5_w_glu, w_branch_b, w_out, ffn2_pre_g, ffn2_post_g, ffn2_w_gate, ffn2_w_up, ffn2_w_down, loss_target, m_ffn1_pre_g, m_ffn1_post_g, m_ffn1_w_gate, m_ffn1_w_up, m_ffn1_w_down, m_mix_pre_g, m_mix_post_g, m_w_in, m_ssd_conv_w, m_ssd_conv_b, m_ssd_dt_bias, m_ssd_a_log, m_ssd_d, m_ssd_norm_g, m_w_branch_a, m_s5_lambda_re, m_s5_lambda_im, m_s5_b_re, m_s5_b_im, m_s5_c_re, m_s5_c_im, m_s5_log_step, m_s5_d, m_s5_w_glu, m_w_branch_b, m_w_out, m_ffn2_pre_g, m_ffn2_post_g, m_ffn2_w_gate, m_ffn2_w_up, m_ffn2_w_down, v_ffn1_pre_g, v_ffn1_post_g, v_ffn1_w_gate, v_ffn1_w_up, v_ffn1_w_down, v_mix_pre_g, v_mix_post_g, v_w_in, v_ssd_conv_w, v_ssd_conv_b, v_ssd_dt_bias, v_ssd_a_log, v_ssd_d, v_ssd_norm_g, v_w_branch_a, v_s5_lambda_re, v_s5_lambda_im, v_s5_b_re, v_s5_b_im, v_s5_c_re, v_s5_c_im, v_s5_log_step, v_s5_d, v_s5_w_glu, v_w_branch_b, v_w_out, v_ffn2_pre_g, v_ffn2_post_g, v_ffn2_w_gate, v_ffn2_w_up, v_ffn2_w_down):
    given = dict(locals())
    for n in COL_SHARDED:
        for prefix in ("", "m_", "v_"):
            given[prefix + n] = given[prefix + n].transpose(0, 2, 1)
    w = {n: given[n] for n in WEIGHTS}
    mom = {n: given["m_" + n] for n in WEIGHTS}
    var = {n: given["v_" + n] for n in WEIGHTS}
    d = _dims(w, x)
    n_layers = w["ffn1_pre_g"].shape[0]
    T, D = d["T"], d["D"]

    conv_w = w["ssd_conv_w"]
    first, middle, last = BIG[:3], BIG[3:8], BIG[8:]
    wf, _ = _unpack_weights(_chip_all_gather(_pack_weights(w, 0, first), "gather_weights_first"), w, 0, first)
    coming = _chip_gather_start(_pack_weights(w, 0, middle, conv_w), "gather_start_l0")
    h = x.reshape(T, D) + coming[-1][0, 0]
    saved, layers = [], []
    for l in range(n_layers):
        if l > 0:
            pack, land = _chip_gather_wait(*coming[:4], h, f"gather_wait_l{l}")
            wf, _ = _unpack_weights(_chip_gather_finish(pack, land, f"gather_finish_l{l}"), w, l, BIG)
        p = _ffn1_params(l, w, wf)
        h, s1 = _ffn_fwd(h, p["ffn1_pre_g"], p["ffn1_post_g"], p["wgu1"], p["wd1"], "ffn1")
        if l == 0:
            pack, land = _chip_gather_wait(*coming[:4], h, "gather_wait_l0")
            wf, conv_w_full = _unpack_weights(_chip_gather_finish(pack, land, "gather_finish_l0"), w, 0, middle, conv_w)
            ffn2_coming = _chip_gather_start(_pack_weights(w, 0, last), "gather_start_l0_ffn2")
            h = h + ffn2_coming[-1][0, 0]
        if l + 1 < n_layers:
            coming = _chip_gather_start(_pack_weights(w, l + 1, BIG), f"gather_start_l{l + 1}")
            h = h + coming[-1][0, 0]
        p.update(_layer_params(l, w, wf, conv_w_full, d))
        layers.append(p)
        h, sm = _mixer_fwd(h, p, d)
        if l == 0:
            pack, land = _chip_gather_wait(*ffn2_coming[:4], h, "gather_wait_l0_ffn2")
            wf, _ = _unpack_weights(_chip_gather_finish(pack, land, "gather_finish_l0_ffn2"), w, 0, last)
        p.update(_ffn2_params(wf))
        h, s2 = _ffn_fwd(h, p["ffn2_pre_g"], p["ffn2_post_g"], p["wgu2"], p["wd2"], "ffn2")
        saved.append((s1, sm, s2))
    dh, loss_part = _row_kernel("loss", _loss_fn, [(h, D, 0), (loss_target.reshape(T, D), D, 0)], [], [(D, f32)], [(8, LANES)])
    loss = lax.psum(loss_part[0, 0], ("x", "y", "c"))

    my_core = lax.axis_index("c").astype(jnp.int32).reshape(1)
    my_chip = (2 * lax.axis_index("x") + lax.axis_index("y")).astype(jnp.int32).reshape(1)
    lg, exchanging, in_flight = [None] * n_layers, [None] * n_layers, [None] * n_layers
    for l in reversed(range(n_layers)):
        p = layers[l]
        s1, sm, s2 = saved[l]
        dh, g2 = _ffn_bwd(dh, s2, p["ffn2_pre_g"], p["ffn2_post_g"], p["wgu2"], p["wd2"], "ffn2")
        if l + 1 < n_layers:
            in_flight[l + 1] = _reduce_start(exchanging[l + 1], my_core, dh, f"grads_l{l + 1}")
            dh = dh + in_flight[l + 1][-1][0, 0]
        dh, gm = _mixer_bwd(dh, sm, p, d)
        dh, g1 = _ffn_bwd(dh, s1, p["ffn1_pre_g"], p["ffn1_post_g"], p["wgu1"], p["wd1"], "ffn1")
        H = p["wd1"].shape[0]
        gl = dict(ffn1_pre_g=g1["pre_g"], ffn1_post_g=g1["post_g"], ffn1_w_gate=g1["wgu"][:H], ffn1_w_up=g1["wgu"][H:],
                  ffn1_w_down=g1["wd"], ffn2_pre_g=g2["pre_g"], ffn2_post_g=g2["post_g"], ffn2_w_gate=g2["wgu"][:H],
                  ffn2_w_up=g2["wgu"][H:], ffn2_w_down=g2["wd"], mix_pre_g=gm["mix_pre_g"], mix_post_g=gm["mix_post_g"],
                  w_in=_w_in_unperm(gm["w_in"], d), ssd_conv_w=gm["conv_w"], ssd_conv_b=gm["conv_b"],
                  ssd_dt_bias=_head_unpad(gm["dt_bias"], d), ssd_a_log=_head_unpad(gm["a_log"], d),
                  ssd_d=_head_unpad(gm["d_skip"], d), ssd_norm_g=gm["norm_g"], w_branch_a=gm["w_a"], s5_d=gm["s5_d"],
                  s5_w_glu=gm["w_glu"], w_branch_b=gm["w_b"], w_out=gm["w_out"])
        gl.update(_s5_param_grads(gm, p, d, l))
        lg[l] = gl
        exchanging[l] = _core_send_other_half_start(_pack_big_grads([gl]), f"exchange_start_grads_l{l}")
        if l > 0:
            dh = dh + exchanging[l][-1][0, 0]
    grad_x = dh.reshape(x.shape)
    summed = [None] * n_layers
    for l in range(1, n_layers):
        summed[l] = _reduce_finish(in_flight[l], my_chip, exchanging[0][-1], f"grads_l{l}")
    in_flight[0] = _reduce_start(exchanging[0], my_core, summed[-1] if n_layers > 1 else grad_x, "grads_l0")
    small_names = SMALL + ["ssd_conv_w"]
    small_parts = {n: [g[n] for g in lg] for n in small_names}
    small_like = {n: jax.ShapeDtypeStruct((n_layers,) + lg[0][n].shape, f32) for n in small_names}
    small_like.update({n: w[n] for n in SMALL})
    small_pack = _pack_small(small_parts, small_names)
    small_sum = _reduce_to_chips(jnp.broadcast_to(small_pack, (N_CHIPS,) + small_pack.shape), my_core, "small")
    small = _unpack_small(small_sum, small_like, small_names)
    summed[0] = _reduce_finish(in_flight[0], my_chip, small_sum, "grads_l0")
    big_grads = _unpack_big_grads(summed, w)
    grads = {}
    k_me = 2 * lax.axis_index("x") + lax.axis_index("y")
    cw = w["ssd_conv_w"].shape[-1]
    small["ssd_conv_w"] = lax.dynamic_slice_in_dim(small["ssd_conv_w"], k_me * cw, cw, axis=2)
    grads.update(small)

    delta, new_m, new_v = {}, {}, {}
    for n in BIG:
        grads[n], delta[n], new_m[n], new_v[n] = _adamw_layers(w[n], big_grads[n], mom[n], var[n], "adamw_" + n)
    n = "ssd_conv_w"
    delta[n], new_m[n], new_v[n] = _adamw(w[n], grads[n], mom[n], var[n], "adamw_" + n)
    pw, pm, pv = [_pack_small(t, SMALL) for t in (w, mom, var)]
    assert pw.shape[0] <= small_sum.shape[0]
    sd, sm_, sv = _adamw(pw, small_sum[:pw.shape[0]], pm, pv, "adamw_small")
    delta.update(_unpack_small(sd, w, SMALL))
    new_m.update(_unpack_small(sm_, w, SMALL))
    new_v.update(_unpack_small(sv, w, SMALL))
    for n in COL_SHARDED:
        for out in (grads, delta, new_m, new_v):
            out[n] = out[n].transpose(0, 2, 1)
    return (loss, grad_x, *[grads[n] for n in WEIGHTS], *[delta[n] for n in WEIGHTS],
            *[new_m[n] for n in WEIGHTS], *[new_v[n] for n in WEIGHTS])
```

```python
import functools

import numpy as np
import jax
import jax.numpy as jnp
from jax import lax
from jax.experimental import pallas as pl
from jax.experimental.pallas import tpu as pltpu

f32, bf16 = jnp.float32, jnp.bfloat16

SSD_N_GROUPS = 4
SSD_CHUNK = 128
RMS_EPS = 1e-6
S5_MAX_REAL = -1e-4
S5_SUPERBLOCK = 256
ADAM_LR, ADAM_B1, ADAM_B2, ADAM_EPS, ADAM_WD, ADAM_STEP = 0.001, 0.9, 0.999, 1e-08, 0.01, 10

LANES = 128
PACK_COLS = 1024
D2D_STREAMS = 16
PACK_ROW_MULT = 2 * D2D_STREAMS * 16
VMEM_LIMIT_BYTES = 48 * 1024 * 1024
N_CHIPS, N_CORES, N_DEV = 4, 2, 8
MESH = pl.DeviceIdType.MESH

BIG = ["ffn1_w_gate", "ffn1_w_up", "ffn1_w_down", "w_in", "w_branch_a", "s5_w_glu", "w_branch_b", "w_out",
       "ffn2_w_gate", "ffn2_w_up", "ffn2_w_down"]
COL_SHARDED = ["ffn1_w_gate", "ffn1_w_up", "w_in", "s5_w_glu", "ffn2_w_gate", "ffn2_w_up"]
SMALL = ["ffn1_pre_g", "ffn1_post_g", "mix_pre_g", "mix_post_g", "ssd_conv_b", "ssd_norm_g", "s5_lambda_re", "s5_lambda_im",
         "s5_b_re", "s5_b_im", "s5_c_re", "s5_c_im", "s5_d", "ffn2_pre_g", "ffn2_post_g", "s5_log_step", "ssd_dt_bias",
         "ssd_a_log", "ssd_d"]
WEIGHTS = ["ffn1_pre_g", "ffn1_post_g", "ffn1_w_gate", "ffn1_w_up", "ffn1_w_down", "mix_pre_g", "mix_post_g", "w_in",
           "ssd_conv_w", "ssd_conv_b", "ssd_dt_bias", "ssd_a_log", "ssd_d", "ssd_norm_g", "w_branch_a", "s5_lambda_re",
           "s5_lambda_im", "s5_b_re", "s5_b_im", "s5_c_re", "s5_c_im", "s5_log_step", "s5_d", "s5_w_glu", "w_branch_b",
           "w_out", "ffn2_pre_g", "ffn2_post_g", "ffn2_w_gate", "ffn2_w_up", "ffn2_w_down"]


def _params(sem=None):
    return pltpu.CompilerParams(dimension_semantics=sem, vmem_limit_bytes=VMEM_LIMIT_BYTES)


def _pick(n, target, mult=LANES):
    best = None
    for d in range(mult, min(n, target) + 1, mult):
        if n % d == 0:
            best = d
    return best if best is not None else n


_DIMS = {"nn": (((1,), (0,)), ((), ())), "nt": (((1,), (1,)), ((), ())), "tn": (((0,), (0,)), ((), ()))}


def _mm(a, b, mode, out_dtype, name, bm_t=1024, bn_t=1024, bk_t=2816):
    if mode == "nn":
        (M, K), (K2, N) = a.shape, b.shape
    elif mode == "nt":
        (M, K), (N, K2) = a.shape, b.shape
    else:
        (K, M), (K2, N) = a.shape, b.shape
    assert K == K2, (name, a.shape, b.shape)
    bm, bn, bk = _pick(M, bm_t), _pick(N, bn_t), _pick(K, bk_t)
    nk = K // bk
    dn = _DIMS[mode]

    def body(a_ref, b_ref, o_ref, *scratch):
        p = lax.dot_general(a_ref[...].astype(bf16), b_ref[...].astype(bf16), dn, preferred_element_type=f32)
        if nk == 1:
            o_ref[...] = p.astype(o_ref.dtype)
        else:
            acc = scratch[0]
            k = pl.program_id(2)

            @pl.when(k == 0)
            def _():
                acc[...] = p

            @pl.when(k > 0)
            def _():
                acc[...] += p

            @pl.when(k == nk - 1)
            def _():
                o_ref[...] = acc[...].astype(o_ref.dtype)

    if mode == "tn":
        a_spec = pl.BlockSpec((bk, bm), lambda i, j, k: (k, i))
    else:
        a_spec = pl.BlockSpec((bm, bk), lambda i, j, k: (i, k))
    if mode == "nt":
        b_spec = pl.BlockSpec((bn, bk), lambda i, j, k: (j, k))
    else:
        b_spec = pl.BlockSpec((bk, bn), lambda i, j, k: (k, j))
    return pl.pallas_call(
        body, name=name, grid=(M // bm, N // bn, nk), in_specs=[a_spec, b_spec],
        out_specs=pl.BlockSpec((bm, bn), lambda i, j, k: (i, j)), out_shape=jax.ShapeDtypeStruct((M, N), out_dtype),
        scratch_shapes=[pltpu.VMEM((bm, bn), f32)] if nk > 1 else [],
        compiler_params=_params(("parallel", "parallel", "arbitrary")),
    )(a, b)


def _bdmm(a, w, mode, nb, out_dtype, name, bt_t=1024):
    if mode == "tn":
        T = a.shape[0]
        ka, nw = a.shape[1] // nb, w.shape[1] // nb
        bt = _pick(T, bt_t)
        nt = T // bt

        def body_tn(a_ref, b_ref, o_ref):
            p = lax.dot_general(a_ref[...].astype(bf16), b_ref[...].astype(bf16), _DIMS["tn"], preferred_element_type=f32)
            k = pl.program_id(1)

            @pl.when(k == 0)
            def _():
                o_ref[...] = p

            @pl.when(k > 0)
            def _():
                o_ref[...] += p

        return pl.pallas_call(
            body_tn, name=name, grid=(nb, nt),
            in_specs=[pl.BlockSpec((bt, ka), lambda j, k: (k, j)), pl.BlockSpec((bt, nw), lambda j, k: (k, j))],
            out_specs=pl.BlockSpec((ka, nw), lambda j, k: (j, 0)), out_shape=jax.ShapeDtypeStruct((nb * ka, nw), f32),
            compiler_params=_params(("parallel", "arbitrary")),
        )(a, w)
    T = a.shape[0]
    ka, nw = w.shape[0] // nb, w.shape[1]
    bt = _pick(T, bt_t)
    kin, kout = (ka, nw) if mode == "nn" else (nw, ka)
    dn = _DIMS[mode]

    def body(a_ref, w_ref, o_ref):
        o_ref[...] = lax.dot_general(a_ref[...].astype(bf16), w_ref[...].astype(bf16), dn,
                                     preferred_element_type=f32).astype(o_ref.dtype)

    return pl.pallas_call(
        body, name=name, grid=(nb, T // bt),
        in_specs=[pl.BlockSpec((bt, kin), lambda j, i: (i, j)), pl.BlockSpec((ka, nw), lambda j, i: (j, 0))],
        out_specs=pl.BlockSpec((bt, kout), lambda j, i: (i, j)), out_shape=jax.ShapeDtypeStruct((T, nb * kout), out_dtype),
        compiler_params=_params(("parallel", "parallel")),
    )(a, w)


def _row_index(i, cb):
    return (i, cb)


def _row_kernel(name, fn, rows, pars, row_outs, par_outs=(), block_rows=256):
    T = rows[0][0].shape[0]
    widest = max([nc for (_, nc, _) in rows] + [nc for (nc, _) in row_outs])
    R = min(block_rows if widest > 1024 else 2 * block_rows, T)
    assert T % R == 0
    nr, npar, nro = len(rows), len(pars), len(row_outs)

    def body(*refs):
        rv = [r[...] for r in refs[:nr]]
        pv = [r[...] for r in refs[nr:nr + npar]]
        ro, po = fn(rv, pv)
        for ref, v in zip(refs[nr + npar:nr + npar + nro], ro):
            ref[...] = v.astype(ref.dtype)
        if par_outs:
            i = pl.program_id(0)
            prefs = refs[nr + npar + nro:]

            @pl.when(i == 0)
            def _():
                for ref, v in zip(prefs, po):
                    ref[...] = v.astype(f32)

            @pl.when(i > 0)
            def _():
                for ref, v in zip(prefs, po):
                    ref[...] += v.astype(f32)

    in_specs = [pl.BlockSpec((R, nc), functools.partial(_row_index, cb=cb)) for (_, nc, cb) in rows]
    in_specs += [pl.BlockSpec(p.shape, lambda i: (0, 0)) for p in pars]
    out_specs = [pl.BlockSpec((R, nc), lambda i: (i, 0)) for (nc, _) in row_outs]
    out_specs += [pl.BlockSpec(s, lambda i: (0, 0)) for s in par_outs]
    out_shape = [jax.ShapeDtypeStruct((T, nc), dt) for (nc, dt) in row_outs]
    out_shape += [jax.ShapeDtypeStruct(s, f32) for s in par_outs]
    outs = pl.pallas_call(
        body, name=name, grid=(T // R,), in_specs=in_specs, out_specs=out_specs, out_shape=out_shape,
        compiler_params=_params(("arbitrary",) if par_outs else ("parallel",)),
    )(*[r[0] for r in rows], *pars)
    return list(outs)


def _fwd_of(f):
    def fn(rv, pv):
        return f([v.astype(f32) for v in rv], [v.astype(f32) for v in pv]), []
    return fn


def _vjp_of(f, n_x, n_cot, grad_idx, n_add=0):
    def fn(rv, pv):
        xs = [v.astype(f32) for v in rv[:n_x]]
        cots = [v.astype(f32) for v in rv[n_x:n_x + n_cot]]
        adds = rv[n_x + n_cot:n_x + n_cot + n_add]
        ps = [v.astype(f32) for v in pv]
        _, vjp = jax.vjp(lambda *a: f(list(a[:n_x]), list(a[n_x:])), *xs, *ps)
        g = vjp(cots)
        row_g = [g[i] for i in grad_idx]
        for k, a in enumerate(adds):
            row_g[k] = row_g[k] + a.astype(f32)
        return row_g, list(g[n_x:])
    return fn


def _rms(x, g):
    return x * lax.rsqrt(jnp.mean(x * x, axis=-1, keepdims=True) + RMS_EPS) * g


def _f_norm(xs, ps):
    return [_rms(xs[0], ps[0])]


def _f_post(scale):
    def f(xs, ps):
        return [scale * _rms(xs[0], ps[0])]
    return f


def _f_resnorm(scale):
    def f(xs, ps):
        return [xs[0] + scale * _rms(xs[1], ps[0])]
    return f


def _f_dt(xs, ps):
    dt = jax.nn.softplus(xs[0] + ps[0])
    return [dt, -jnp.exp(ps[1]) * dt]


def _f_ssdpost(n_groups):
    def f(xs, ps):
        y = xs[0] * jax.nn.silu(xs[1])
        width = y.shape[-1] // n_groups
        lane = lax.broadcasted_iota(jnp.int32, y.shape, 1)
        scale = jnp.zeros_like(y)
        for k in range(n_groups):
            m = ((lane >= k * width) & (lane < (k + 1) * width)).astype(f32)
            ms = jnp.sum(y * y * m, axis=-1, keepdims=True) / width
            scale = scale + lax.rsqrt(ms + RMS_EPS) * m
        return [y * scale * ps[0]]
    return f


def _f_s5post(xs, ps):
    return [jax.nn.gelu(xs[0] + ps[0] * xs[1])]


def _f_merge(xs, ps):
    return [jax.nn.sigmoid(xs[0]) * xs[1] + jax.nn.sigmoid(xs[2]) * xs[3]]


def _swiglu_fwd(rv, pv):
    ab = rv[0].astype(f32)
    h = ab.shape[1] // 2
    return [jax.nn.silu(ab[:, :h]) * ab[:, h:]], []


def _swiglu_bwd(rv, pv):
    ab, d = rv[0].astype(f32), rv[1].astype(f32)
    h = ab.shape[1] // 2
    a, b = ab[:, :h], ab[:, h:]
    s = jax.nn.sigmoid(a)
    return [jnp.concatenate([d * b * (s * (1.0 + a * (1.0 - s))), d * (a * s)], axis=1)], []


def _glu_fwd(rv, pv):
    vg = rv[0].astype(f32)
    h = vg.shape[1] // 2
    return [vg[:, :h] * jax.nn.sigmoid(vg[:, h:])], []


def _glu_bwd(rv, pv):
    vg, d = rv[0].astype(f32), rv[1].astype(f32)
    h = vg.shape[1] // 2
    s = jax.nn.sigmoid(vg[:, h:])
    return [jnp.concatenate([d * s, d * vg[:, :h] * s * (1.0 - s)], axis=1)], []


def _loss_fn(rv, pv):
    e = rv[0].astype(f32) - rv[1].astype(f32)
    per_tok = jnp.mean(e * e, axis=-1, keepdims=True)
    part = 0.5 * jnp.sum(per_tok, axis=0, keepdims=True)
    return [e / e.shape[-1]], [jnp.broadcast_to(part, (8, LANES))]


def _add_fn(rv, pv):
    return [rv[0].astype(f32) + rv[1].astype(f32)], []


def _adamw_fn(rv, pv):
    w, g, m, v = [x.astype(f32) for x in rv]
    m = ADAM_B1 * m + (1.0 - ADAM_B1) * g
    v = ADAM_B2 * v + (1.0 - ADAM_B2) * (g * g)
    m_hat = m / (1.0 - ADAM_B1 ** ADAM_STEP)
    v_hat = v / (1.0 - ADAM_B2 ** ADAM_STEP)
    return [-ADAM_LR * (m_hat / (jnp.sqrt(v_hat) + ADAM_EPS) + ADAM_WD * w), m, v], []


def _adamw_layers(w, gs, m, v, name):
    n_layers, r, cols = w.shape
    br, bc = _pick(r, 256, 8), cols
    if br < 64 and cols % LANES == 0:
        br, bc = r, LANES

    def body(w_ref, *refs):
        g_refs, (m_ref, v_ref, g_out, d_ref, nm_ref, nv_ref) = refs[:n_layers], refs[n_layers:]
        layer = pl.program_id(0)
        g = g_refs[0][...]
        for k in range(1, n_layers):
            g = jnp.where(layer == k, g_refs[k][...], g)
        outs, _ = _adamw_fn([w_ref[0], g, m_ref[0], v_ref[0]], [])
        g_out[0] = g
        d_ref[0], nm_ref[0], nv_ref[0] = outs

    stacked = pl.BlockSpec((1, br, bc), lambda l, i, j: (l, i, j))
    single = pl.BlockSpec((br, bc), lambda l, i, j: (i, j))
    return pl.pallas_call(
        body, name=name, grid=(n_layers, r // br, cols // bc), in_specs=[stacked] + [single] * n_layers + [stacked] * 2,
        out_specs=[stacked] * 4, out_shape=[jax.ShapeDtypeStruct(w.shape, f32)] * 4,
        compiler_params=_params(("parallel", "parallel", "parallel")),
    )(w, *gs, m, v)


def _adamw(w, g, m, v, name):
    shape = w.shape
    cols = shape[-1] if (w.ndim >= 2 and shape[-1] >= LANES) else None
    if cols is None:
        n = int(np.prod(shape))
        cols = LANES if n % LANES == 0 else n
    n_rows = int(np.prod(shape)) // cols
    br, bc = _pick(n_rows, 256, 8), cols
    if br < 64 and cols % LANES == 0:
        br, bc = n_rows, LANES

    def body(w_ref, g_ref, m_ref, v_ref, d_ref, nm_ref, nv_ref):
        outs, _ = _adamw_fn([w_ref[...], g_ref[...], m_ref[...], v_ref[...]], [])
        d_ref[...], nm_ref[...], nv_ref[...] = outs

    spec = pl.BlockSpec((br, bc), lambda i, j: (i, j))
    outs = pl.pallas_call(
        body, name=name, grid=(n_rows // br, cols // bc), in_specs=[spec] * 4, out_specs=[spec] * 3,
        out_shape=[jax.ShapeDtypeStruct((n_rows, cols), f32)] * 3, compiler_params=_params(("parallel", "parallel")),
    )(*[t.reshape(n_rows, cols) for t in (w, g, m, v)])
    return [o.reshape(shape) for o in outs]


def _shift_down(x, s, row):
    if s == 0:
        return x
    return jnp.where(row >= s, pltpu.roll(x, s, 0), 0.0)


def _shift_up(x, s, row):
    if s == 0:
        return x
    n = x.shape[0]
    return jnp.where(row < n - s, pltpu.roll(x, n - s, 0), 0.0)


def _conv_pre(x, w, b, row):
    kw = w.shape[0]
    c = b
    for k in range(kw):
        c = c + w[k:k + 1, :] * _shift_down(x, kw - 1 - k, row)
    return c


def _conv_fwd(xsrc, col0, w, b, name, bc_t=512):
    T = xsrc.shape[0]
    kw, ncols = w.shape
    bc = _pick(ncols, bc_t)
    off = col0 // bc
    assert col0 % bc == 0

    def body(x_ref, w_ref, b_ref, o_ref):
        x = x_ref[...].astype(f32)
        row = lax.broadcasted_iota(jnp.int32, x.shape, 0)
        c = _conv_pre(x, w_ref[...], b_ref[...], row)
        o_ref[...] = c * jax.nn.sigmoid(c)

    return pl.pallas_call(
        body, name=name, grid=(ncols // bc,),
        in_specs=[pl.BlockSpec((T, bc), lambda j: (0, off + j)), pl.BlockSpec((kw, bc), lambda j: (0, j)),
                  pl.BlockSpec((1, bc), lambda j: (0, j))],
        out_specs=pl.BlockSpec((T, bc), lambda j: (0, j)), out_shape=jax.ShapeDtypeStruct((T, ncols), f32),
        compiler_params=_params(("parallel",)),
    )(xsrc, w, b)


def _conv_bwd(xsrc, col0, w, b, dact, name, bc_t=512):
    T = xsrc.shape[0]
    kw, ncols = w.shape
    bc = _pick(ncols, bc_t)
    off = col0 // bc
    assert col0 % bc == 0

    def body(x_ref, w_ref, b_ref, d_ref, dx_ref, dw_ref, db_ref):
        x = x_ref[...].astype(f32)
        w = w_ref[...]
        row = lax.broadcasted_iota(jnp.int32, x.shape, 0)
        c = _conv_pre(x, w, b_ref[...], row)
        s = jax.nn.sigmoid(c)
        dc = d_ref[...].astype(f32) * (s * (1.0 + c * (1.0 - s)))
        dx = jnp.zeros_like(x)
        dws = []
        for k in range(kw):
            dx = dx + w[k:k + 1, :] * _shift_up(dc, kw - 1 - k, row)
            dws.append(jnp.sum(dc * _shift_down(x, kw - 1 - k, row), axis=0, keepdims=True))
        dx_ref[...] = dx.astype(dx_ref.dtype)
        dw_ref[...] = jnp.concatenate(dws, axis=0)
        db_ref[...] = jnp.sum(dc, axis=0, keepdims=True)

    return pl.pallas_call(
        body, name=name, grid=(ncols // bc,),
        in_specs=[pl.BlockSpec((T, bc), lambda j: (0, off + j)), pl.BlockSpec((kw, bc), lambda j: (0, j)),
                  pl.BlockSpec((1, bc), lambda j: (0, j)), pl.BlockSpec((T, bc), lambda j: (0, j))],
        out_specs=[pl.BlockSpec((T, bc), lambda j: (0, j)), pl.BlockSpec((kw, bc), lambda j: (0, j)),
                   pl.BlockSpec((1, bc), lambda j: (0, j))],
        out_shape=[jax.ShapeDtypeStruct((T, ncols), bf16), jax.ShapeDtypeStruct((kw, ncols), f32),
                   jax.ShapeDtypeStruct((1, ncols), f32)],
        compiler_params=_params(("parallel",)),
    )(xsrc, w, b, dact)


_HI = lax.Precision.HIGHEST


def _dot(a, b, dims="nn", precision=None):
    return lax.dot_general(a, b, _DIMS[dims], preferred_element_type=f32, precision=precision)


def _dot01(a, b, dims="nn", ones="b"):
    x = a if ones == "b" else b
    hi = x.astype(bf16)
    rest = x - hi.astype(f32)
    mid = rest.astype(bf16)
    lo = (rest - mid.astype(f32)).astype(bf16)
    if ones == "b":
        e = b.astype(bf16)
        return _dot(hi, e, dims) + _dot(mid, e, dims) + _dot(lo, e, dims)
    e = a.astype(bf16)
    return _dot(e, hi, dims) + _dot(e, mid, dims) + _dot(e, lo, dims)


def _ssd_common(x_ref, b_ref, c_ref, dt_ref, adt_ref, d_ref, hpg, p):
    q = b_ref.shape[0]
    hp = hpg * p
    bb, cb = b_ref[...].astype(bf16), c_ref[...].astype(bf16)
    r = lax.broadcasted_iota(jnp.int32, (q, q), 0)
    s = lax.broadcasted_iota(jnp.int32, (q, q), 1)
    tril = r >= s
    trilf = tril.astype(f32)
    eh = lax.broadcasted_iota(jnp.int32, (LANES, hp), 0)
    ec = lax.broadcasted_iota(jnp.int32, (LANES, hp), 1)
    expand = ((ec >= eh * p) & (ec < (eh + 1) * p)).astype(f32)
    adt = adt_ref[...]
    cum = _dot01(trilf, adt, "nn", "a")
    cum_t = _dot01(adt, (r <= s).astype(f32), "tn")
    cum_e = _dot01(cum, expand)
    dt_e = _dot01(dt_ref[...], expand)
    d_e = _dot01(jnp.broadcast_to(d_ref[...], (8, LANES)), expand)[0:1, :]
    gmat = _dot(cb, bb, "nt")
    x = x_ref[...]
    xdt = x * dt_e
    e_all = jnp.exp(cum_e)
    dec = jnp.exp(cum_e[q - 1:q, :] - cum_e)
    lms, ms = [], []
    for h in range(hpg):
        lm = jnp.exp(jnp.where(tril, cum[:, h:h + 1] - cum_t[h:h + 1, :], -1e30))
        lms.append(lm)
        ms.append(gmat * lm)
    et = [jnp.exp(cum[q - 1:q, h:h + 1]) for h in range(hpg)]
    return dict(bb=bb, cb=cb, trilf=trilf, expand=expand, cum=cum, x=x, xdt=xdt, dt_e=dt_e, d_e=d_e, e=e_all, dec=dec,
                lms=lms, ms=ms, et=et)


def _ssd_specs(q, hp, n, g_n, nc, rev):
    def cidx(c):
        return (nc - 1 - c) if rev else c
    x_spec = pl.BlockSpec((q, hp), lambda g, c: (cidx(c), g))
    boff = (g_n * hp) // n
    b_spec = pl.BlockSpec((q, n), lambda g, c: (cidx(c), boff + g))
    c_spec = pl.BlockSpec((q, n), lambda g, c: (cidx(c), boff + g_n + g))
    dt_spec = pl.BlockSpec((q, LANES), lambda g, c: (cidx(c), g))
    d_spec = pl.BlockSpec((1, LANES), lambda g, c: (0, g))
    st_spec = pl.BlockSpec((1, 1, hp, n), lambda g, c: (cidx(c), g, 0, 0))
    return x_spec, b_spec, c_spec, dt_spec, d_spec, st_spec


def _ssd_fwd(act, dt, adt, dpad, hpg, p, n, name):
    T = act.shape[0]
    g_n, q = SSD_N_GROUPS, SSD_CHUNK
    nc, hp = T // q, hpg * p
    x_spec, b_spec, c_spec, dt_spec, d_spec, st_spec = _ssd_specs(q, hp, n, g_n, nc, False)

    def body(x_ref, b_ref, c_ref, dt_ref, adt_ref, d_ref, y_ref, st_ref, s_scr):
        @pl.when(pl.program_id(1) == 0)
        def _():
            s_scr[...] = jnp.zeros_like(s_scr)

        k = _ssd_common(x_ref, b_ref, c_ref, dt_ref, adt_ref, d_ref, hpg, p)
        s0 = s_scr[...]
        st_ref[0, 0] = s0
        xdtb = k["xdt"].astype(bf16)
        ydiag = [_dot(k["ms"][h].astype(bf16), xdtb[:, h * p:(h + 1) * p]) for h in range(hpg)]
        z = _dot(k["cb"], s0.astype(bf16), "nt")
        y_ref[...] = jnp.concatenate(ydiag, axis=1) + k["e"] * z + k["d_e"] * k["x"]
        upd = _dot((k["xdt"] * k["dec"]).astype(bf16), k["bb"], "tn")
        for h in range(hpg):
            s_scr[h * p:(h + 1) * p, :] = k["et"][h] * s0[h * p:(h + 1) * p, :] + upd[h * p:(h + 1) * p, :]

    return pl.pallas_call(
        body, name=name, grid=(g_n, nc),
        in_specs=[x_spec, b_spec, c_spec, dt_spec, dt_spec, d_spec],
        out_specs=[pl.BlockSpec((q, hp), lambda g, c: (c, g)), st_spec],
        out_shape=[jax.ShapeDtypeStruct((T, g_n * hp), f32), jax.ShapeDtypeStruct((nc, g_n, hp, n), f32)],
        scratch_shapes=[pltpu.VMEM((hp, n), f32)],
        compiler_params=_params(("parallel", "arbitrary")),
    )(act, act, act, dt, adt, dpad)


def _ssd_bwd(act, dt, adt, dpad, states, dy, hpg, p, n, name):
    T = act.shape[0]
    g_n, q = SSD_N_GROUPS, SSD_CHUNK
    nc, hp = T // q, hpg * p
    x_spec, b_spec, c_spec, dt_spec, d_spec, st_spec = _ssd_specs(q, hp, n, g_n, nc, True)

    def body(x_ref, b_ref, c_ref, dt_ref, adt_ref, d_ref, st_ref, dy_ref,
             dx_ref, db_ref, dc_ref, ddt_ref, dadt_ref, dd_ref, ds_scr):
        first = pl.program_id(1) == 0

        @pl.when(first)
        def _():
            ds_scr[...] = jnp.zeros_like(ds_scr)

        k = _ssd_common(x_ref, b_ref, c_ref, dt_ref, adt_ref, d_ref, hpg, p)
        bb, cb, expand, x, xdt, dec = k["bb"], k["cb"], k["expand"], k["x"], k["xdt"], k["dec"]
        heads = lambda t: _dot01(t, expand, "nt")
        s0 = st_ref[0, 0]
        ds1 = ds_scr[...]
        s0b, ds1b = s0.astype(bf16), ds1.astype(bf16)
        dy = dy_ref[...]
        dyb, xdtb = dy.astype(bf16), xdt.astype(bf16)
        lane = lax.broadcasted_iota(jnp.int32, (1, LANES), 1)
        dg = jnp.zeros((q, q), f32)
        w_rows = jnp.zeros((q, LANES), f32)
        w_cols, dxdt_parts = [], []
        for h in range(hpg):
            hs = slice(h * p, (h + 1) * p)
            dm = _dot(dyb[:, hs], xdtb[:, hs], "nt")
            dg = dg + dm * k["lms"][h]
            wm = dm * k["ms"][h]
            w_rows = w_rows + jnp.sum(wm, axis=1, keepdims=True) * (lane == h).astype(f32)
            w_cols.append(jnp.sum(wm, axis=0, keepdims=True))
            dxdt_parts.append(_dot(k["ms"][h].astype(bf16), dyb[:, hs], "tn"))
        dxdt_diag = jnp.concatenate(dxdt_parts, axis=1)
        w_cols = jnp.concatenate(w_cols + [jnp.zeros((LANES - hpg, q), f32)], axis=0).T
        dgb = dg.astype(bf16)
        z = _dot(cb, s0b, "nt")
        dz = dy * k["e"]
        dzb = dz.astype(bf16)
        dxd = _dot(bb, ds1b, "nt")
        ddec = dxd * xdt * dec
        db_ref[...] = _dot(dgb, cb, "tn") + _dot((xdt * dec).astype(bf16), ds1b)
        dc_ref[...] = _dot(dgb, bb) + _dot(dzb, s0b)
        ds0 = _dot(dzb, cb, "tn")
        for h in range(hpg):
            hs = slice(h * p, (h + 1) * p)
            ds_scr[hs, :] = ds0[hs, :] + k["et"][h] * ds1[hs, :]
        dxdt = dxdt_diag + dxd * dec
        ddec_h = heads(ddec)
        dcum = w_rows - w_cols + heads(dz * z) - ddec_h
        et_row = jnp.exp(k["cum"][q - 1:q, :])
        dsum = _dot01(jnp.ones((8, n), f32), _dot01(expand, ds1 * s0, "nn", "a"), "nt", "a")[0:1, :]
        dcl = dsum * et_row + jnp.sum(ddec_h, axis=0, keepdims=True)
        rowq = lax.broadcasted_iota(jnp.int32, (q, 1), 0)
        dcum = dcum + (rowq == q - 1).astype(f32) * dcl
        ddt_ref[...] = heads(dxdt * x)
        dadt_ref[...] = _dot01(k["trilf"], dcum, "tn", "a")
        dx_ref[...] = k["d_e"] * dy + dxdt * k["dt_e"]
        dd8 = heads(jnp.broadcast_to(jnp.sum(dy * x, axis=0, keepdims=True), (8, hp)))

        @pl.when(first)
        def _():
            dd_ref[...] = dd8

        @pl.when(jnp.logical_not(first))
        def _():
            dd_ref[...] += dd8

    rc = lambda g, c: (nc - 1 - c, g)
    return pl.pallas_call(
        body, name=name, grid=(g_n, nc),
        in_specs=[x_spec, b_spec, c_spec, dt_spec, dt_spec, d_spec, st_spec, pl.BlockSpec((q, hp), rc)],
        out_specs=[pl.BlockSpec((q, hp), rc), pl.BlockSpec((q, n), rc), pl.BlockSpec((q, n), rc),
                   pl.BlockSpec((q, LANES), rc), pl.BlockSpec((q, LANES), rc), pl.BlockSpec((8, LANES), lambda g, c: (g, 0))],
        out_shape=[jax.ShapeDtypeStruct((T, g_n * hp), f32), jax.ShapeDtypeStruct((T, g_n * n), f32),
                   jax.ShapeDtypeStruct((T, g_n * n), f32), jax.ShapeDtypeStruct((T, g_n * LANES), f32),
                   jax.ShapeDtypeStruct((T, g_n * LANES), f32), jax.ShapeDtypeStruct((g_n * 8, LANES), f32)],
        scratch_shapes=[pltpu.VMEM((hp, n), f32)],
        compiler_params=_params(("parallel", "arbitrary")),
    )(act, act, act, dt, adt, dpad, states, dy)


def _cmul(ar, ai, br, bi):
    return ar * br - ai * bi, ar * bi + ai * br


def _s5_tile_powers(lr, li):
    p = [(lr, li)]
    for _ in range(7):
        p.append(_cmul(p[-1][0], p[-1][1], lr, li))
    tile = (jnp.concatenate([q[0] for q in p], axis=0), jnp.concatenate([q[1] for q in p], axis=0))
    return tile, (p[0], p[1], p[3])


def _s5_tile_scan(xr, xi, steps, reverse):
    row = lax.broadcasted_iota(jnp.int32, xr.shape, 0)
    for d, (pr, pi) in zip((1, 2, 4), steps):
        if reverse:
            keep = row < 8 - d
            sr, si = pltpu.roll(xr, 8 - d, 0), pltpu.roll(xi, 8 - d, 0)
        else:
            keep = row >= d
            sr, si = pltpu.roll(xr, d, 0), pltpu.roll(xi, d, 0)
        sr, si = jnp.where(keep, sr, 0.0), jnp.where(keep, si, 0.0)
        ar, ai = _cmul(sr, si, pr, pi)
        xr, xi = xr + ar, xi + ai
    return xr, xi


def _s5_scan_fwd(bu, lam_re, lam_im, nsb, name, tc_t=512):
    T = bu.shape[0]
    w2 = bu.shape[1] // nsb
    w = w2 // 2
    tc = _pick(T, tc_t, 8)

    def body(bu_ref, lr_ref, li_ref, st_ref, carry):
        @pl.when(pl.program_id(1) == 0)
        def _():
            carry[...] = jnp.zeros_like(carry)

        (pr8, pi8), steps = _s5_tile_powers(lr_ref[0:1, :], li_ref[0:1, :])

        def tile(i, c):
            r = pl.ds(pl.multiple_of(i * 8, 8), 8)
            x = bu_ref[r, :]
            xr, xi = _s5_tile_scan(x[:, :w], x[:, w:], steps, False)
            ar, ai = _cmul(pr8, pi8, c[0], c[1])
            xr, xi = xr + ar, xi + ai
            st_ref[r, :] = jnp.concatenate([xr, xi], axis=1)
            return xr[7:8, :], xi[7:8, :]

        c = lax.fori_loop(0, tc // 8, tile, (carry[0:1, :], carry[1:2, :]), unroll=2)
        carry[0:1, :] = c[0]
        carry[1:2, :] = c[1]

    return pl.pallas_call(
        body, name=name, grid=(nsb, T // tc),
        in_specs=[pl.BlockSpec((tc, w2), lambda j, i: (i, j)), pl.BlockSpec((8, w), lambda j, i: (j, 0)),
                  pl.BlockSpec((8, w), lambda j, i: (j, 0))],
        out_specs=pl.BlockSpec((tc, w2), lambda j, i: (i, j)), out_shape=jax.ShapeDtypeStruct(bu.shape, f32),
        scratch_shapes=[pltpu.VMEM((8, w), f32)],
        compiler_params=_params(("parallel", "arbitrary")),
    )(bu, lam_re, lam_im)


def _s5_scan_bwd(gst, states, lam_re, lam_im, nsb, name, tc_t=512):
    T = gst.shape[0]
    w2 = gst.shape[1] // nsb
    w = w2 // 2
    tc = _pick(T, tc_t, 8)
    nt = T // tc
    n_tiles = tc // 8

    def body(g_ref, s_ref, sp_ref, lr_ref, li_ref, a_ref, dlr_ref, dli_ref, carry, acc):
        chunk = pl.program_id(1)

        @pl.when(chunk == 0)
        def _():
            carry[...] = jnp.zeros_like(carry)
            acc[...] = jnp.zeros_like(acc)

        (qr8, qi8), steps = _s5_tile_powers(lr_ref[0:1, :], -li_ref[0:1, :])
        row = lax.broadcasted_iota(jnp.int32, (8, w), 0)
        rev_r, rev_i = jnp.zeros((8, w), f32), jnp.zeros((8, w), f32)
        for r in range(8):
            rev_r = jnp.where(row == r, qr8[7 - r:8 - r, :], rev_r)
            rev_i = jnp.where(row == r, qi8[7 - r:8 - r, :], rev_i)
        row2 = lax.broadcasted_iota(jnp.int32, (8, w2), 0)

        def tile(k, c):
            ar_in, ai_in, dr, di = c
            i = n_tiles - 1 - k
            r = pl.ds(pl.multiple_of(i * 8, 8), 8)
            x = g_ref[r, :]
            xr, xi = _s5_tile_scan(x[:, :w], x[:, w:], steps, True)
            pr, pi = _cmul(rev_r, rev_i, ar_in, ai_in)
            xr, xi = xr + pr, xi + pi
            a_ref[r, :] = jnp.concatenate([xr, xi], axis=1)
            before = jnp.where(i > 0, s_ref[pl.ds(pl.multiple_of(jnp.maximum(i - 1, 0) * 8, 8), 8), :],
                               sp_ref[tc - 8:tc, :] * (chunk < nt - 1).astype(f32))
            prev = jnp.where(row2 == 0, pltpu.roll(before, 1, 0), pltpu.roll(s_ref[r, :], 1, 0))
            spr, spi = prev[:, :w], prev[:, w:]
            return xr[0:1, :], xi[0:1, :], dr + xr * spr + xi * spi, di - xr * spi + xi * spr

        c0 = (carry[0:1, :], carry[1:2, :], acc[0:8, :], acc[8:16, :])
        ar, ai, dr, di = lax.fori_loop(0, n_tiles, tile, c0, unroll=2)
        carry[0:1, :] = ar
        carry[1:2, :] = ai
        acc[0:8, :] = dr
        acc[8:16, :] = di
        dlr_ref[...] = jnp.broadcast_to(jnp.sum(dr, axis=0, keepdims=True), (8, w))
        dli_ref[...] = jnp.broadcast_to(jnp.sum(di, axis=0, keepdims=True), (8, w))

    cur = lambda j, i: (nt - 1 - i, j)
    prv = lambda j, i: (jnp.maximum(nt - 2 - i, 0), j)
    return pl.pallas_call(
        body, name=name, grid=(nsb, nt),
        in_specs=[pl.BlockSpec((tc, w2), cur), pl.BlockSpec((tc, w2), cur), pl.BlockSpec((tc, w2), prv),
                  pl.BlockSpec((8, w), lambda j, i: (j, 0)), pl.BlockSpec((8, w), lambda j, i: (j, 0))],
        out_specs=[pl.BlockSpec((tc, w2), cur), pl.BlockSpec((8, w), lambda j, i: (j, 0)),
                   pl.BlockSpec((8, w), lambda j, i: (j, 0))],
        out_shape=[jax.ShapeDtypeStruct(gst.shape, f32), jax.ShapeDtypeStruct((nsb * 8, w), f32),
                   jax.ShapeDtypeStruct((nsb * 8, w), f32)],
        scratch_shapes=[pltpu.VMEM((8, w), f32), pltpu.VMEM((16, w), f32)],
        compiler_params=_params(("parallel", "arbitrary")),
    )(gst, states, states, lam_re, lam_im)


def _s5_prep_fn(xs, ps):
    lam_re, lam_im, log_step, b_re, b_im, expand = ps
    lr = jnp.minimum(lam_re, S5_MAX_REAL)
    li = lam_im
    step = jnp.exp(log_step)
    er = jnp.exp(lr * step)
    ang = li * step
    lbr, lbi = er * jnp.cos(ang), er * jnp.sin(ang)
    nr, ni = lbr - 1.0, lbi
    den = lr * lr + li * li
    qr, qi = (nr * lr + ni * li) / den, (ni * lr - nr * li) / den
    qre, qie = _dot(qr, expand, "nn", _HI), _dot(qi, expand, "nn", _HI)
    return [lbr, lbi, qre * b_re - qie * b_im, qre * b_im + qie * b_re]


def _s5_prep(pars, name):
    def body(*refs):
        outs = _s5_prep_fn([], [r[...] for r in refs[:6]])
        for ref, v in zip(refs[6:], outs):
            ref[...] = v

    g, nst = pars[0].shape
    nc = pars[3].shape[1]
    return pl.pallas_call(
        body, name=name,
        out_shape=[jax.ShapeDtypeStruct((g, nst), f32)] * 2 + [jax.ShapeDtypeStruct((g, nc), f32)] * 2,
        compiler_params=_params(),
    )(*pars)


def _s5_prep_bwd(pars, cots, name):
    def body(*refs):
        ps = [r[...] for r in refs[:6]]
        ct = [r[...] for r in refs[6:10]]
        _, vjp = jax.vjp(lambda *a: _s5_prep_fn([], list(a)), *ps)
        g = vjp(ct)
        for ref, v in zip(refs[10:], g[:5]):
            ref[...] = v

    return pl.pallas_call(
        body, name=name, out_shape=[jax.ShapeDtypeStruct(p.shape, f32) for p in pars[:5]], compiler_params=_params(),
    )(*pars, *cots)


_ANY = pl.BlockSpec(memory_space=pl.ANY)


def _remote(src, dst, send_sem, recv_sem, device):
    return pltpu.make_async_remote_copy(src_ref=src, dst_ref=dst, send_sem=send_sem, recv_sem=recv_sem, device_id=device,
                                        device_id_type=MESH)


def _staged_copy(src, dst, buf, in_sems, out_sems):
    n = D2D_STREAMS
    piece = src.shape[0] // n
    assert src.shape[0] % n == 0

    def load(i):
        return pltpu.make_async_copy(src.at[pl.ds(i * piece, piece)], buf.at[i % 2], in_sems.at[i % 2])

    def store(i):
        return pltpu.make_async_copy(buf.at[i % 2], dst.at[pl.ds(i * piece, piece)], out_sems.at[i % 2])

    load(0).start()
    for i in range(n):
        if i + 1 < n:
            if i >= 1:
                store(i - 1).wait()
            load(i + 1).start()
        load(i).wait()
        store(i).start()
    store(n - 2).wait()
    store(n - 1).wait()


def _stage_scratch(rows, cols, dtype):
    return [pltpu.VMEM((2, rows // D2D_STREAMS, cols), dtype), pltpu.SemaphoreType.DMA((2,)), pltpu.SemaphoreType.DMA((2,))]


def _chip_all_gather(block, name):
    rows = block.shape[0]
    half = rows // 2
    piece = half // D2D_STREAMS
    assert rows % (2 * D2D_STREAMS * 16) == 0

    def body(src, out, ici_send, ici_recv, d2d_send, d2d_recv, *stage):
        x, y, c = lax.axis_index("x"), lax.axis_index("y"), lax.axis_index("c")
        me = 2 * x + y
        sibling = (x, y, 1 - c)
        chips = [(1 - x, y), (x, 1 - y), (1 - x, 1 - y)]
        mine = pl.ds(pl.multiple_of(c * half, 16), half)
        sends = []
        for j, (px, py) in enumerate(chips):
            cp = _remote(src.at[mine], out.at[me, mine], ici_send.at[j], ici_recv.at[j], (px, py, c))
            cp.start()
            sends.append(cp)
        _staged_copy(src, out.at[me], *stage)
        for j, (px, py) in enumerate(chips):
            slot = 2 * px + py
            _remote(src.at[mine], out.at[slot, mine], ici_send.at[j], ici_recv.at[j], (px, py, c)).wait_recv()
            for s in range(D2D_STREAMS):
                r = pl.ds(pl.multiple_of(c * half + s * piece, 16), piece)
                k = j * D2D_STREAMS + s
                cp = _remote(out.at[slot, r], out.at[slot, r], d2d_send.at[k], d2d_recv.at[k], sibling)
                cp.start()
                sends.append(cp)
        for j, (px, py) in enumerate(chips):
            slot = 2 * px + py
            for s in range(D2D_STREAMS):
                r = pl.ds(pl.multiple_of((1 - c) * half + s * piece, 16), piece)
                k = j * D2D_STREAMS + s
                _remote(out.at[slot, r], out.at[slot, r], d2d_send.at[k], d2d_recv.at[k], sibling).wait_recv()
        for cp in sends:
            cp.wait_send()

    n_d2d = 3 * D2D_STREAMS
    return pl.pallas_call(
        body, name=name, in_specs=[_ANY], out_specs=_ANY,
        out_shape=jax.ShapeDtypeStruct((N_CHIPS,) + block.shape, block.dtype),
        scratch_shapes=[pltpu.SemaphoreType.DMA((3,)), pltpu.SemaphoreType.DMA((3,)), pltpu.SemaphoreType.DMA((n_d2d,)),
                        pltpu.SemaphoreType.DMA((n_d2d,))] + _stage_scratch(rows, block.shape[1], block.dtype),
    )(block)


def _chip_scatter(parts, name):
    def body(src, out, send_sems, recv_sems, *stage):
        x, y, c = lax.axis_index("x"), lax.axis_index("y"), lax.axis_index("c")
        me = 2 * x + y
        chips = [(1 - x, y), (x, 1 - y), (1 - x, 1 - y)]
        sends = []
        for j, (px, py) in enumerate(chips):
            cp = pltpu.make_async_remote_copy(src_ref=src.at[2 * px + py], dst_ref=out.at[me], send_sem=send_sems.at[j],
                                              recv_sem=recv_sems.at[j], device_id=(px, py, c), device_id_type=MESH)
            cp.start()
            sends.append(cp)
        _staged_copy(src.at[me], out.at[me], *stage)
        for j, (px, py) in enumerate(chips):
            pltpu.make_async_remote_copy(src_ref=src.at[me], dst_ref=out.at[2 * px + py], send_sem=send_sems.at[j],
                                         recv_sem=recv_sems.at[j], device_id=(px, py, c), device_id_type=MESH).wait_recv()
        for cp in sends:
            cp.wait_send()

    return pl.pallas_call(
        body, name=name, in_specs=[_ANY], out_specs=_ANY, out_shape=jax.ShapeDtypeStruct(parts.shape, parts.dtype),
        scratch_shapes=[pltpu.SemaphoreType.DMA((3,)), pltpu.SemaphoreType.DMA((3,))]
        + _stage_scratch(parts.shape[1], parts.shape[2], parts.dtype),
    )(parts)


_HBM = pl.BlockSpec(memory_space=pltpu.HBM)
_SEM = pl.BlockSpec(memory_space=pltpu.SEMAPHORE)
_EFFECT = pltpu.SideEffectType.DATAFLOW_SIDE_EFFECTING


def _gather_peers():
    x, y, c = lax.axis_index("x"), lax.axis_index("y"), lax.axis_index("c")
    return x, y, c, 2 * x + y, [(1 - x, y), (x, 1 - y), (1 - x, 1 - y)]


def _chip_gather_start(block, name):
    half = block.shape[0] // 2

    def body(src, land, send_sems, recv_sems, src_out, land_out, token):
        x, y, c, me, chips = _gather_peers()
        mine = pl.ds(pl.multiple_of(c * half, 16), half)
        for j, (px, py) in enumerate(chips):
            _remote(src.at[mine], land.at[me, mine], send_sems.at[j], recv_sems.at[j], (px, py, c)).start()
        token[...] = jnp.zeros_like(token)

    land_shape = (N_CHIPS,) + block.shape
    return pl.pallas_call(
        body, name=name,
        out_shape=(pltpu.SemaphoreType.DMA((3,)), pltpu.SemaphoreType.DMA((3,)), pltpu.HBM(block.shape, block.dtype),
                   pltpu.HBM(land_shape, block.dtype), jax.ShapeDtypeStruct((8, LANES), f32)),
        in_specs=(_HBM, _HBM), out_specs=(_SEM, _SEM, _HBM, _HBM, pl.BlockSpec(memory_space=pltpu.VMEM)),
        input_output_aliases={0: 2, 1: 3}, compiler_params=pltpu.CompilerParams(has_side_effects=_EFFECT),
    )(pltpu.with_memory_space_constraint(block, pltpu.HBM),
      pltpu.with_memory_space_constraint(lax.empty(land_shape, block.dtype), pltpu.HBM))


def _chip_gather_wait(send_sems, recv_sems, block, land, after, name):
    half = block.shape[0] // 2

    def body(src, land_ref, send_ref, recv_ref, after_ref, src_dead, land_out):
        x, y, c, me, chips = _gather_peers()
        mine = pl.ds(pl.multiple_of(c * half, 16), half)
        for j, (px, py) in enumerate(chips):
            cp = _remote(src.at[mine], land_ref.at[2 * px + py, mine], send_ref.at[j], recv_ref.at[j], (px, py, c))
            cp.wait_send()
            cp.wait_recv()

    return pl.pallas_call(
        body, name=name, out_shape=(pltpu.HBM(block.shape, block.dtype), pltpu.HBM(land.shape, land.dtype)),
        in_specs=(_HBM, _HBM, _SEM, _SEM, _ANY), out_specs=(_HBM, _HBM), input_output_aliases={0: 0, 1: 1},
        compiler_params=pltpu.CompilerParams(has_side_effects=_EFFECT),
    )(block, land, send_sems, recv_sems, after)


def _chip_gather_finish(block, land, name):
    rows = block.shape[0]
    half = rows // 2
    piece = half // D2D_STREAMS

    def body(src, land_ref, out, d2d_send, d2d_recv, *stage):
        x, y, c, me, chips = _gather_peers()
        sibling = (x, y, 1 - c)
        sends = []
        for j, (px, py) in enumerate(chips):
            slot = 2 * px + py
            for s in range(D2D_STREAMS):
                r = pl.ds(pl.multiple_of(c * half + s * piece, 16), piece)
                k = j * D2D_STREAMS + s
                cp = _remote(land_ref.at[slot, r], out.at[slot, r], d2d_send.at[k], d2d_recv.at[k], sibling)
                cp.start()
                sends.append(cp)
        _staged_copy(src, out.at[me], *stage)
        for j, (px, py) in enumerate(chips):
            slot = 2 * px + py
            for s in range(D2D_STREAMS):
                r = pl.ds(pl.multiple_of((1 - c) * half + s * piece, 16), piece)
                k = j * D2D_STREAMS + s
                _remote(land_ref.at[slot, r], out.at[slot, r], d2d_send.at[k], d2d_recv.at[k], sibling).wait_recv()
        for cp in sends:
            cp.wait_send()

    n_d2d = 3 * D2D_STREAMS
    return pl.pallas_call(
        body, name=name, in_specs=[_ANY, _ANY], out_specs=_ANY, out_shape=jax.ShapeDtypeStruct(land.shape, land.dtype),
        input_output_aliases={1: 0},
        scratch_shapes=[pltpu.SemaphoreType.DMA((n_d2d,)), pltpu.SemaphoreType.DMA((n_d2d,))]
        + _stage_scratch(rows, block.shape[1], block.dtype),
    )(block, land)


def _chip_scatter_start(parts, name):
    def body(src, land, send_sems, recv_sems, src_out, land_out, token):
        x, y, c = lax.axis_index("x"), lax.axis_index("y"), lax.axis_index("c")
        me = 2 * x + y
        for j, (px, py) in enumerate([(1 - x, y), (x, 1 - y), (1 - x, 1 - y)]):
            _remote(src.at[2 * px + py], land.at[me], send_sems.at[j], recv_sems.at[j], (px, py, c)).start()
        token[...] = jnp.zeros_like(token)

    return pl.pallas_call(
        body, name=name,
        out_shape=(pltpu.SemaphoreType.DMA((3,)), pltpu.SemaphoreType.DMA((3,)), pltpu.HBM(parts.shape, parts.dtype),
                   pltpu.HBM(parts.shape, parts.dtype), jax.ShapeDtypeStruct((8, LANES), f32)),
        in_specs=(_HBM, _HBM), out_specs=(_SEM, _SEM, _HBM, _HBM, pl.BlockSpec(memory_space=pltpu.VMEM)),
        input_output_aliases={0: 2, 1: 3}, compiler_params=pltpu.CompilerParams(has_side_effects=_EFFECT),
    )(pltpu.with_memory_space_constraint(parts, pltpu.HBM),
      pltpu.with_memory_space_constraint(lax.empty(parts.shape, parts.dtype), pltpu.HBM))


def _chip_scatter_wait(send_sems, recv_sems, parts, land, after, name):
    def body(src, land_ref, send_ref, recv_ref, after_ref, src_dead, land_out):
        x, y, c = lax.axis_index("x"), lax.axis_index("y"), lax.axis_index("c")
        me = 2 * x + y
        for j, (px, py) in enumerate([(1 - x, y), (x, 1 - y), (1 - x, 1 - y)]):
            cp = _remote(src.at[2 * px + py], land_ref.at[2 * px + py], send_ref.at[j], recv_ref.at[j], (px, py, c))
            cp.wait_send()
            cp.wait_recv()

    return pl.pallas_call(
        body, name=name, out_shape=(pltpu.HBM(parts.shape, parts.dtype), pltpu.HBM(land.shape, land.dtype)),
        in_specs=(_HBM, _HBM, _SEM, _SEM, _ANY), out_specs=(_HBM, _HBM), input_output_aliases={0: 0, 1: 1},
        compiler_params=pltpu.CompilerParams(has_side_effects=_EFFECT),
    )(parts, land, send_sems, recv_sems, after)


def _sum_slots_own(landed, own, chip, name, block_rows=512):
    s_n, r_n, c_n = landed.shape
    br = _pick(r_n, block_rows, 8)

    def body(chip_ref, land_ref, own_ref, o_ref):
        acc = jnp.zeros((br, c_n), f32)
        for s in range(s_n):
            acc = acc + jnp.where(chip_ref[0] == s, own_ref[s], land_ref[s]).astype(f32)
        o_ref[...] = acc

    spec = pl.BlockSpec((s_n, br, c_n), lambda i, c: (0, i, 0))
    grid_spec = pltpu.PrefetchScalarGridSpec(num_scalar_prefetch=1, grid=(r_n // br,), in_specs=[spec, spec],
                                             out_specs=pl.BlockSpec((br, c_n), lambda i, c: (i, 0)))
    return pl.pallas_call(
        body, name=name, grid_spec=grid_spec, out_shape=jax.ShapeDtypeStruct((r_n, c_n), f32),
        compiler_params=_params(("parallel",)),
    )(chip, landed, own)


def _core_send_other_half(parts, name):
    n_slots, rows, cols = parts.shape
    half = rows // 2
    piece = half // D2D_STREAMS
    assert rows % (2 * D2D_STREAMS * 16) == 0

    def body(src, out, send_sems, recv_sems):
        x, y, c = lax.axis_index("x"), lax.axis_index("y"), lax.axis_index("c")
        sibling = (x, y, 1 - c)
        sends = []
        for k in range(n_slots):
            for s in range(D2D_STREAMS):
                theirs = pl.ds(pl.multiple_of((1 - c) * half + s * piece, 16), piece)
                i = k * D2D_STREAMS + s
                cp = _remote(src.at[k, theirs], out.at[k, pl.ds(s * piece, piece)], send_sems.at[i], recv_sems.at[i], sibling)
                cp.start()
                sends.append(cp)
        for cp in sends:
            cp.wait_recv()
        for cp in sends:
            cp.wait_send()

    n = n_slots * D2D_STREAMS
    return pl.pallas_call(
        body, name=name, in_specs=[_ANY], out_specs=_ANY, out_shape=jax.ShapeDtypeStruct((n_slots, half, cols), parts.dtype),
        scratch_shapes=[pltpu.SemaphoreType.DMA((n,)), pltpu.SemaphoreType.DMA((n,))],
    )(parts)


def _other_half_copies(src, land, send_sems, recv_sems):
    n_slots, rows, _ = src.shape
    half = rows // 2
    piece = half // D2D_STREAMS
    x, y, c = lax.axis_index("x"), lax.axis_index("y"), lax.axis_index("c")
    copies = []
    for k in range(n_slots):
        for s in range(D2D_STREAMS):
            theirs = pl.ds(pl.multiple_of((1 - c) * half + s * piece, 16), piece)
            i = k * D2D_STREAMS + s
            copies.append(_remote(src.at[k, theirs], land.at[k, pl.ds(s * piece, piece)], send_sems.at[i], recv_sems.at[i],
                                  (x, y, 1 - c)))
    return copies


def _core_send_other_half_start(parts, name):
    n_slots, rows, cols = parts.shape
    assert rows % (2 * D2D_STREAMS * 16) == 0
    n = n_slots * D2D_STREAMS
    land_shape = (n_slots, rows // 2, cols)

    def body(src, land, send_sems, recv_sems, src_out, land_out, token):
        for cp in _other_half_copies(src, land, send_sems, recv_sems):
            cp.start()
        token[...] = jnp.zeros_like(token)

    return pl.pallas_call(
        body, name=name,
        out_shape=(pltpu.SemaphoreType.DMA((n,)), pltpu.SemaphoreType.DMA((n,)), pltpu.HBM(parts.shape, parts.dtype),
                   pltpu.HBM(land_shape, parts.dtype), jax.ShapeDtypeStruct((8, LANES), f32)),
        in_specs=(_HBM, _HBM), out_specs=(_SEM, _SEM, _HBM, _HBM, pl.BlockSpec(memory_space=pltpu.VMEM)),
        input_output_aliases={0: 2, 1: 3}, compiler_params=pltpu.CompilerParams(has_side_effects=_EFFECT),
    )(pltpu.with_memory_space_constraint(parts, pltpu.HBM),
      pltpu.with_memory_space_constraint(lax.empty(land_shape, parts.dtype), pltpu.HBM))


def _core_send_other_half_wait(send_sems, recv_sems, parts, land, after, name):
    def body(src, land_ref, send_ref, recv_ref, after_ref, src_dead, land_out):
        for cp in _other_half_copies(src, land_ref, send_ref, recv_ref):
            cp.wait_send()
            cp.wait_recv()

    return pl.pallas_call(
        body, name=name, out_shape=(pltpu.HBM(parts.shape, parts.dtype), pltpu.HBM(land.shape, land.dtype)),
        in_specs=(_HBM, _HBM, _SEM, _SEM, _ANY), out_specs=(_HBM, _HBM), input_output_aliases={0: 0, 1: 1},
        compiler_params=pltpu.CompilerParams(has_side_effects=_EFFECT),
    )(parts, land, send_sems, recv_sems, after)


def _add_my_half(parts, other, core, name, block_rows=1024):
    n_slots, rows, cols = parts.shape
    half = rows // 2
    br = _pick(half, block_rows, 16)
    nb = half // br

    def body(c_ref, a_ref, b_ref, o_ref):
        o_ref[...] = (a_ref[...].astype(f32) + b_ref[...].astype(f32)).astype(o_ref.dtype)

    grid_spec = pltpu.PrefetchScalarGridSpec(
        num_scalar_prefetch=1, grid=(n_slots, nb),
        in_specs=[pl.BlockSpec((1, br, cols), lambda k, i, c: (k, c[0] * nb + i, 0)),
                  pl.BlockSpec((1, br, cols), lambda k, i, c: (k, i, 0))],
        out_specs=pl.BlockSpec((1, br, cols), lambda k, i, c: (k, i, 0)))
    return pl.pallas_call(
        body, name=name, grid_spec=grid_spec, out_shape=jax.ShapeDtypeStruct((n_slots, half, cols), parts.dtype),
        compiler_params=_params(("parallel", "parallel")),
    )(core, parts, other)


def _core_join_halves(mine, name):
    half, cols = mine.shape
    piece = half // D2D_STREAMS
    assert half % (D2D_STREAMS * 16) == 0

    def body(src, out, send_sems, recv_sems, *stage):
        x, y, c = lax.axis_index("x"), lax.axis_index("y"), lax.axis_index("c")
        sibling = (x, y, 1 - c)
        sends = []
        for s in range(D2D_STREAMS):
            dst = out.at[pl.ds(pl.multiple_of(c * half + s * piece, 16), piece)]
            cp = _remote(src.at[pl.ds(s * piece, piece)], dst, send_sems.at[s], recv_sems.at[s], sibling)
            cp.start()
            sends.append(cp)
        _staged_copy(src, out.at[pl.ds(pl.multiple_of(c * half, 16), half)], *stage)
        for s in range(D2D_STREAMS):
            dst = out.at[pl.ds(pl.multiple_of((1 - c) * half + s * piece, 16), piece)]
            _remote(src.at[pl.ds(s * piece, piece)], dst, send_sems.at[s], recv_sems.at[s], sibling).wait_recv()
        for cp in sends:
            cp.wait_send()

    return pl.pallas_call(
        body, name=name, in_specs=[_ANY], out_specs=_ANY, out_shape=jax.ShapeDtypeStruct((2 * half, cols), mine.dtype),
        scratch_shapes=[pltpu.SemaphoreType.DMA((D2D_STREAMS,)), pltpu.SemaphoreType.DMA((D2D_STREAMS,))]
        + _stage_scratch(half, cols, mine.dtype),
    )(mine)


def _reduce_start(exchanging, core, after, tag):
    parts, other = _core_send_other_half_wait(*exchanging[:4], after, f"exchange_wait_{tag}")
    return _chip_scatter_start(_add_my_half(parts, other, core, f"sum_core_halves_{tag}"), f"scatter_start_{tag}")


def _reduce_finish(started, chip, after, tag):
    send_sems, recv_sems, chip_part, land, _ = started
    own, landed = _chip_scatter_wait(send_sems, recv_sems, chip_part, land, after, f"scatter_wait_{tag}")
    return _core_join_halves(_sum_slots_own(landed, own, chip, f"sum_chip_parts_{tag}"), f"join_core_halves_{tag}")


def _reduce_to_chips(parts, core, tag):
    chip_part = _add_my_half(parts, _core_send_other_half(parts, f"exchange_core_halves_{tag}"), core, f"sum_core_halves_{tag}")
    my_sum = _sum_slots(_chip_scatter(chip_part, f"scatter_{tag}"), f"sum_chip_parts_{tag}")
    return _core_join_halves(my_sum, f"join_core_halves_{tag}")


def _sum_slots(stack, name, block_rows=512):
    s_n, r_n, c_n = stack.shape
    br = _pick(r_n, block_rows, 8)

    def body(in_ref, o_ref):
        acc = in_ref[0].astype(f32)
        for s in range(1, s_n):
            acc = acc + in_ref[s].astype(f32)
        o_ref[...] = acc

    return pl.pallas_call(
        body, name=name, grid=(r_n // br,), in_specs=[pl.BlockSpec((s_n, br, c_n), lambda i: (0, i, 0))],
        out_specs=pl.BlockSpec((br, c_n), lambda i: (i, 0)), out_shape=jax.ShapeDtypeStruct((r_n, c_n), f32),
        compiler_params=_params(("parallel",)),
    )(stack)


def _concat_padded(parts, mult):
    rows = sum(p.shape[0] for p in parts)
    pad = (-rows) % mult
    if pad:
        parts = parts + [jnp.zeros((pad,) + parts[0].shape[1:], parts[0].dtype)]
    return jnp.concatenate(parts, axis=0)


def _pack_weights(w, l, names, conv_w=None):
    parts = [w[n][l].astype(bf16).reshape(-1, PACK_COLS) for n in names]
    if conv_w is not None:
        parts.append(_concat_padded([lax.bitcast_convert_type(conv_w, bf16).reshape(-1, PACK_COLS)], 16))
    return _concat_padded(parts, PACK_ROW_MULT)


def _unpack_weights(full, w, l, names, conv_w=None):
    start, r0 = {}, 0
    for n in names:
        start[n] = r0
        r0 += w[n][l].size // PACK_COLS

    def shards(n):
        rows = w[n][l].size // PACK_COLS
        return [full[k, start[n]:start[n] + rows].reshape(w[n].shape[1:]) for k in range(N_CHIPS)]

    if conv_w is None:
        return shards, None
    rows = conv_w.size * 2 // PACK_COLS
    pieces = lax.bitcast_convert_type(full[:, r0:r0 + rows].reshape((N_CHIPS,) + conv_w.shape + (2,)), f32)
    return shards, jnp.concatenate([pieces[k] for k in range(N_CHIPS)], axis=2)


def _pack_big_grads(layer_grads):
    parts, slot_rows = [], 0
    for k in range(N_CHIPS):
        slot = []
        for n in BIG:
            for g in layer_grads:
                width = g[n].shape[0] // N_CHIPS
                slot.append(g[n][k * width:(k + 1) * width].astype(bf16).reshape(-1, PACK_COLS))
        slot_rows = sum(p.shape[0] for p in slot)
        pad = (-slot_rows) % PACK_ROW_MULT
        if pad:
            slot.append(jnp.zeros((pad, PACK_COLS), bf16))
        slot_rows += pad
        parts += slot
    return jnp.concatenate(parts, axis=0).reshape(N_CHIPS, slot_rows, PACK_COLS)


def _unpack_big_grads(summed, w):
    out, r0 = {}, 0
    for n in BIG:
        rows = w[n][0].size // PACK_COLS
        out[n] = [s[r0:r0 + rows].reshape(w[n].shape[1:]) for s in summed]
        r0 += rows
    return out


_SMALL_TILE = 8 * LANES


def _pack_small(vals, names):
    parts = []
    for n in names:
        pieces = vals[n] if isinstance(vals[n], list) else [vals[n]]
        size = sum(p.size for p in pieces)
        if all(p.size % _SMALL_TILE == 0 for p in pieces):
            parts += [p.reshape(-1, LANES) for p in pieces]
        else:
            flat = [p.reshape(-1) for p in pieces] + [jnp.zeros(((-size) % _SMALL_TILE,), f32)]
            parts.append(jnp.concatenate(flat).reshape(-1, LANES))
    return _concat_padded(parts, PACK_ROW_MULT)


def _unpack_small(packed, like, names):
    out, r0 = {}, 0
    for n in names:
        size = like[n].size
        rows = -(-size // _SMALL_TILE) * 8
        out[n] = packed[r0:r0 + rows].reshape(-1)[:size].reshape(like[n].shape)
        r0 += rows
    return out


def _dims(w, x):
    d = {}
    d["D"] = x.shape[-1]
    d["T"] = x.shape[-2]
    d["DI"] = w["ssd_norm_g"].shape[-1]
    d["NH"] = w["ssd_dt_bias"].shape[-1]
    d["CD"] = w["ssd_conv_b"].shape[-1]
    d["G"] = SSD_N_GROUPS
    d["HPG"] = d["NH"] // d["G"]
    d["P"] = d["DI"] // d["NH"]
    d["N"] = (d["CD"] - d["DI"]) // (2 * d["G"])
    d["S5G"], d["S5N"] = w["s5_lambda_re"].shape[-2:]
    d["S5C"] = w["s5_b_re"].shape[-1]
    d["S5W"] = d["S5G"] * d["S5C"]
    d["NSB"] = d["S5W"] // S5_SUPERBLOCK
    d["GSB"] = d["S5G"] // d["NSB"]
    return d


def _head_pad(v, d):
    lead = v.shape[:-1]
    v = v.reshape(lead + (d["G"], d["HPG"]))
    v = jnp.concatenate([v, jnp.zeros(lead + (d["G"], LANES - d["HPG"]), v.dtype)], axis=-1)
    return v.reshape(lead + (d["G"] * LANES,))


def _head_unpad(v, d):
    lead = v.shape[:-1]
    return v.reshape(lead + (d["G"], LANES))[..., :d["HPG"]].reshape(lead + (d["NH"],))


def _w_in_perm(shards, d):
    o, nh = d["DI"] + d["CD"], d["NH"]
    r = shards[0].shape[0]

    def rows(lo, hi):
        out = []
        for k, s in enumerate(shards):
            a, b = max(lo, k * r), min(hi, (k + 1) * r)
            if a < b:
                out.append(s[a - k * r:b - k * r])
        return out

    dt = _head_pad(jnp.concatenate(rows(o, o + nh), axis=0).T, d).T
    return jnp.concatenate(rows(0, o) + rows(o + nh, len(shards) * r) + [dt], axis=0)


def _w_in_unperm(g, d):
    o = d["DI"] + d["CD"]
    rest = d["S5W"] + 2 * d["D"]
    return jnp.concatenate([g[:o], _head_unpad(g[o + rest:].T, d).T, g[o:o + rest]], axis=0)


def _s5_block_diag(v, d):
    gsb = d["GSB"]
    g, a, b = v.shape
    row_group = (lax.broadcasted_iota(jnp.int32, (g * a, gsb * b), 0) // a) % gsb
    col_group = lax.broadcasted_iota(jnp.int32, (g * a, gsb * b), 1) // b
    return jnp.where(row_group == col_group, jnp.tile(v.reshape(g * a, b), (1, gsb)), 0)


def _s5_diag_blocks(m, d, a, b):
    gsb = d["GSB"]
    rows = m.shape[0]
    m = m.reshape(rows, gsb, b)
    row_group = (lax.broadcasted_iota(jnp.int32, (rows, gsb, 1), 0) // a) % gsb
    col_group = lax.broadcasted_iota(jnp.int32, (rows, gsb, 1), 1)
    return jnp.sum(jnp.where(row_group == col_group, m, 0), axis=1).reshape(rows // a, a, b)


def _s5_lam_rows(v, d):
    v = v.reshape(d["NSB"], 1, d["GSB"] * d["S5N"])
    return jnp.broadcast_to(v, (d["NSB"], 8, v.shape[-1])).reshape(d["NSB"] * 8, -1)


def _ffn_fwd(h, pre_g, post_g, wgu, wd, tag):
    D = h.shape[1]
    H2 = wgu.shape[0]
    xn = _row_kernel(f"{tag}_norm", _fwd_of(_f_norm), [(h, D, 0)], [pre_g], [(D, bf16)])[0]
    ab = _mm(xn, wgu, "nt", bf16, f"{tag}_mm_up")
    hid = _row_kernel(f"{tag}_swiglu", _swiglu_fwd, [(ab, H2, 0)], [], [(H2 // 2, bf16)])[0]
    f = _mm(hid, wd, "nn", f32, f"{tag}_mm_down")
    out = _row_kernel(f"{tag}_resnorm", _fwd_of(_f_resnorm(0.5)), [(h, D, 0), (f, D, 0)], [post_g], [(D, f32)])[0]
    return out, dict(h=h, xn=xn, ab=ab, hid=hid, f=f)


def _ffn_bwd(dh_out, s, pre_g, post_g, wgu, wd, tag):
    D = dh_out.shape[1]
    H2 = wgu.shape[0]
    df, dpost = _row_kernel(f"{tag}_resnorm_bwd", _vjp_of(_f_post(0.5), 1, 1, [0]), [(s["f"], D, 0), (dh_out, D, 0)],
                            [post_g], [(D, bf16)], [post_g.shape])
    dwd = _mm(s["hid"], df, "tn", bf16, f"{tag}_mm_dwd")
    dhid = _mm(df, wd, "nt", bf16, f"{tag}_mm_dhid")
    dab = _row_kernel(f"{tag}_swiglu_bwd", _swiglu_bwd, [(s["ab"], H2, 0), (dhid, H2 // 2, 0)], [], [(H2, bf16)])[0]
    dwgu = _mm(dab, s["xn"], "tn", bf16, f"{tag}_mm_dwgu")
    dxn = _mm(dab, wgu, "nn", f32, f"{tag}_mm_dxn")
    dh, dpre = _row_kernel(f"{tag}_norm_bwd", _vjp_of(_f_norm, 1, 1, [0], 1), [(s["h"], D, 0), (dxn, D, 0), (dh_out, D, 0)],
                           [pre_g], [(D, f32)], [pre_g.shape])
    return dh, dict(pre_g=dpre, post_g=dpost, wgu=dwgu, wd=dwd)


def _mixer_fwd(h, p, d):
    D, DI, CD, G, N = d["D"], d["DI"], d["CD"], d["G"], d["N"]
    gl = G * LANES
    c_u5, c_ga, c_gb, c_dt = DI + CD, DI + CD + d["S5W"], DI + CD + d["S5W"] + D, DI + CD + d["S5W"] + 2 * D
    u = _row_kernel("mix_norm", _fwd_of(_f_norm), [(h, D, 0)], [p["mix_pre_g"]], [(D, bf16)])[0]
    proj = _mm(u, p["w_in"], "nt", f32, "mix_mm_in", bn_t=512)
    act = _conv_fwd(proj, DI, p["conv_w"], p["conv_b"], "ssd_conv")
    dt, adt = _row_kernel("ssd_dt", _fwd_of(_f_dt), [(proj, gl, c_dt // gl)], [p["dt_bias"], p["a_log"]], [(gl, f32)] * 2)
    y_ssd, states = _ssd_fwd(act, dt, adt, p["d_skip"], d["HPG"], d["P"], N, "ssd_scan")
    nrm = _row_kernel("ssd_post", _fwd_of(_f_ssdpost(G)), [(y_ssd, DI, 0), (proj, DI, 0)], [p["norm_g"]], [(DI, bf16)])[0]
    y_a = _mm(nrm, p["w_a"], "nn", f32, "mix_mm_a")
    u5 =(proj, d["S5W"], c_u5 // d["S5W"])
    bu = _s5_in(proj, c_u5, p["bsb"], d)
    s5st = _s5_scan_fwd(bu, p["lam_re_rows"], p["lam_im_rows"], d["NSB"], "s5_scan")
    y5 = _bdmm(s5st, p["csb"], "nn", d["NSB"], f32, "s5_mm_c")
    gel = _row_kernel("s5_post", _fwd_of(_f_s5post), [(y5, d["S5W"], 0), u5], [p["s5_d"]], [(d["S5W"], bf16)])[0]
    vg = _mm(gel, p["w_glu"], "nt", bf16, "mix_mm_glu")
    glu = _row_kernel("s5_glu", _glu_fwd, [(vg, vg.shape[1], 0)], [], [(vg.shape[1] // 2, bf16)])[0]
    y_b = _mm(glu, p["w_b"], "nn", f32, "mix_mm_b")
    merged = _row_kernel("mix_merge", _fwd_of(_f_merge), [(proj, D, c_ga // D), (y_a, D, 0), (proj, D, c_gb // D), (y_b, D, 0)],
                         [], [(D, bf16)])[0]
    m = _mm(merged, p["w_out"], "nn", f32, "mix_mm_out")
    out = _row_kernel("mix_resnorm", _fwd_of(_f_resnorm(1.0)), [(h, D, 0), (m, D, 0)], [p["mix_post_g"]], [(D, f32)])[0]
    return out, dict(h=h, u=u, proj=proj, act=act, dt=dt, adt=adt, states=states, y_ssd=y_ssd, nrm=nrm, y_a=y_a, s5st=s5st,
                     y5=y5, gel=gel, vg=vg, glu=glu, y_b=y_b, merged=merged, m=m)


def _s5_in(proj, c_u5, bsb, d):
    T = proj.shape[0]
    nsb = d["NSB"]
    ka, nw = S5_SUPERBLOCK, bsb.shape[1]
    off = c_u5 // ka
    assert c_u5 % ka == 0
    bt = _pick(T, 1024)

    def body(a_ref, w_ref, o_ref):
        o_ref[...] = _dot(a_ref[...].astype(bf16), w_ref[...].astype(bf16))

    return pl.pallas_call(
        body, name="s5_mm_bu", grid=(nsb, T // bt),
        in_specs=[pl.BlockSpec((bt, ka), lambda j, i: (i, off + j)), pl.BlockSpec((ka, nw), lambda j, i: (j, 0))],
        out_specs=pl.BlockSpec((bt, nw), lambda j, i: (i, j)), out_shape=jax.ShapeDtypeStruct((T, nsb * nw), f32),
        compiler_params=_params(("parallel", "parallel")),
    )(proj, bsb)


def _s5_dbsb(proj, c_u5, a, d):
    T = proj.shape[0]
    nsb = d["NSB"]
    ka, nw = S5_SUPERBLOCK, a.shape[1] // nsb
    off = c_u5 // ka
    bt = _pick(T, 1024)

    def body(u_ref, a_ref, o_ref):
        pr = _dot(u_ref[...].astype(bf16), a_ref[...].astype(bf16), "tn")
        k = pl.program_id(1)

        @pl.when(k == 0)
        def _():
            o_ref[...] = pr

        @pl.when(k > 0)
        def _():
            o_ref[...] += pr

    return pl.pallas_call(
        body, name="s5_mm_dbsb", grid=(nsb, T // bt),
        in_specs=[pl.BlockSpec((bt, ka), lambda j, k: (k, off + j)), pl.BlockSpec((bt, nw), lambda j, k: (k, j))],
        out_specs=pl.BlockSpec((ka, nw), lambda j, k: (j, 0)), out_shape=jax.ShapeDtypeStruct((nsb * ka, nw), f32),
        compiler_params=_params(("parallel", "arbitrary")),
    )(proj, a)


def _mixer_bwd(dh_out, s, p, d):
    D, DI, CD, G, N, S5W = d["D"], d["DI"], d["CD"], d["G"], d["N"], d["S5W"]
    gl = G * LANES
    gn = G * N
    c_u5, c_ga, c_gb, c_dt = DI + CD, DI + CD + S5W, DI + CD + S5W + D, DI + CD + S5W + 2 * D
    proj = s["proj"]
    g = {}
    dm, g["mix_post_g"] = _row_kernel("mix_resnorm_bwd", _vjp_of(_f_post(1.0), 1, 1, [0]), [(s["m"], D, 0), (dh_out, D, 0)],
                                      [p["mix_post_g"]], [(D, bf16)], [p["mix_post_g"].shape])
    g["w_out"] = _mm(s["merged"], dm, "tn", bf16, "mix_mm_dwout")
    dmerged = _mm(dm, p["w_out"], "nt", f32, "mix_mm_dmerged")
    dga, dya, dgb, dyb = _row_kernel(
        "mix_merge_bwd", _vjp_of(_f_merge, 4, 1, [0, 1, 2, 3]),
        [(proj, D, c_ga // D), (s["y_a"], D, 0), (proj, D, c_gb // D), (s["y_b"], D, 0), (dmerged, D, 0)], [],
        [(D, bf16), (D, bf16), (D, bf16), (D, bf16)])
    g["w_a"] = _mm(s["nrm"], dya, "tn", bf16, "mix_mm_dwa")
    dnrm = _mm(dya, p["w_a"], "nt", f32, "mix_mm_dnrm")
    dy_ssd, dz, g["norm_g"] = _row_kernel(
        "ssd_post_bwd", _vjp_of(_f_ssdpost(G), 2, 1, [0, 1]), [(s["y_ssd"], DI, 0), (proj, DI, 0), (dnrm, DI, 0)],
        [p["norm_g"]], [(DI, f32), (DI, bf16)], [p["norm_g"].shape])
    dxs, d_b, d_c, ddt, dadt, dd = _ssd_bwd(s["act"], s["dt"], s["adt"], p["d_skip"], s["states"], dy_ssd,
                                            d["HPG"], d["P"], N, "ssd_scan_bwd")
    g["d_skip"] = dd.reshape(G, 8, LANES)[:, 0, :].reshape(1, gl)
    ddt_raw, g["dt_bias"], g["a_log"] = _row_kernel(
        "ssd_dt_bwd", _vjp_of(_f_dt, 1, 2, [0]), [(proj, gl, c_dt // gl), (ddt, gl, 0), (dadt, gl, 0)],
        [p["dt_bias"], p["a_log"]], [(gl, bf16)], [p["dt_bias"].shape, p["a_log"].shape])
    cw, cb = p["conv_w"], p["conv_b"]
    dxc_x, dw_x, db_x = _conv_bwd(proj, DI, cw[:, :DI], cb[:, :DI], dxs, "ssd_conv_bwd_x")
    dxc_b, dw_b, db_b = _conv_bwd(proj, 2 * DI, cw[:, DI:DI + gn], cb[:, DI:DI + gn], d_b, "ssd_conv_bwd_b")
    dxc_c, dw_c, db_c = _conv_bwd(proj, 2 * DI + gn, cw[:, DI + gn:], cb[:, DI + gn:], d_c, "ssd_conv_bwd_c")
    g["conv_w"] = jnp.concatenate([dw_x, dw_b, dw_c], axis=1)
    g["conv_b"] = jnp.concatenate([db_x, db_b, db_c], axis=1)
    g["w_b"] = _mm(s["glu"], dyb, "tn", bf16, "mix_mm_dwb")
    dglu = _mm(dyb, p["w_b"], "nt", f32, "mix_mm_dglu")
    dvg = _row_kernel("s5_glu_bwd", _glu_bwd, [(s["vg"], s["vg"].shape[1], 0), (dglu, S5W, 0)], [], [(s["vg"].shape[1], bf16)])[0]
    g["w_glu"] = _mm(dvg, s["gel"], "tn", bf16, "mix_mm_dwglu")
    dgel = _mm(dvg, p["w_glu"], "nn", f32, "mix_mm_dgel")
    dy5, du5a, g["s5_d"] = _row_kernel(
        "s5_post_bwd", _vjp_of(_f_s5post, 2, 1, [0, 1]), [(s["y5"], S5W, 0), (proj, S5W, c_u5 // S5W), (dgel, S5W, 0)],
        [p["s5_d"]], [(S5W, bf16), (S5W, f32)], [p["s5_d"].shape])
    g["csb"] = _bdmm(s["s5st"], dy5, "tn", d["NSB"], f32, "s5_mm_dcsb")
    gst = _bdmm(dy5, p["csb"], "nt", d["NSB"], f32, "s5_mm_gst")
    a, g["lam_re_rows"], g["lam_im_rows"] = _s5_scan_bwd(gst, s["s5st"], p["lam_re_rows"], p["lam_im_rows"], d["NSB"], "s5_scan_bwd")
    g["bsb"] = _s5_dbsb(proj, c_u5, a, d)
    du5b = _bdmm(a, p["bsb"], "nt", d["NSB"], f32, "s5_mm_du5")
    du5 = _row_kernel("s5_du5", _add_fn, [(du5a, S5W, 0), (du5b, S5W, 0)], [], [(S5W, bf16)])[0]
    dproj = jnp.concatenate([dz, dxc_x, dxc_b, dxc_c, du5, dga, dgb, ddt_raw], axis=1)
    g["w_in"] = _mm(dproj, s["u"], "tn", bf16, "mix_mm_dwin")
    du = _mm(dproj, p["w_in"], "nn", f32, "mix_mm_du", bk_t=2176)
    dh, g["mix_pre_g"] = _row_kernel("mix_norm_bwd", _vjp_of(_f_norm, 1, 1, [0], 1), [(s["h"], D, 0), (du, D, 0), (dh_out, D, 0)],
                                     [p["mix_pre_g"]], [(D, f32)], [p["mix_pre_g"].shape])
    return dh, g


def _ffn1_params(l, w, wf):
    whole = lambda *names: jnp.concatenate([s for n in names for s in wf(n)], axis=0)
    return dict(ffn1_pre_g=w["ffn1_pre_g"][l].reshape(1, -1), ffn1_post_g=w["ffn1_post_g"][l].reshape(1, -1),
                wgu1=whole("ffn1_w_gate", "ffn1_w_up"), wd1=whole("ffn1_w_down"))


def _ffn2_params(wf):
    whole = lambda *names: jnp.concatenate([s for n in names for s in wf(n)], axis=0)
    return dict(wgu2=whole("ffn2_w_gate", "ffn2_w_up"), wd2=whole("ffn2_w_down"))


def _layer_params(l, w, wf, conv_w_full, d):
    r2 = lambda v: v[l].reshape(1, -1)
    p = {}
    for n in ["mix_pre_g", "mix_post_g", "ffn2_pre_g", "ffn2_post_g", "s5_d"]:
        p[n] = r2(w[n])
    whole = lambda *names: jnp.concatenate([s for n in names for s in wf(n)], axis=0)
    p["w_in"] = _w_in_perm(wf("w_in"), d)
    p["w_a"], p["w_glu"], p["w_b"], p["w_out"] = whole("w_branch_a"), whole("s5_w_glu"), whole("w_branch_b"), whole("w_out")
    p["conv_w"] = conv_w_full[l]
    p["conv_b"] = r2(w["ssd_conv_b"])
    p["dt_bias"] = _head_pad(r2(w["ssd_dt_bias"]), d)
    p["a_log"] = _head_pad(r2(w["ssd_a_log"]), d)
    p["d_skip"] = _head_pad(r2(w["ssd_d"]), d)
    p["norm_g"] = r2(w["ssd_norm_g"])
    g5, n5, c5 = d["S5G"], d["S5N"], d["S5C"]
    expand = jnp.repeat(jnp.eye(n5, dtype=f32), c5, axis=1)
    prep_in = [w["s5_lambda_re"][l], w["s5_lambda_im"][l], w["s5_log_step"][l].reshape(g5, 1),
               w["s5_b_re"][l].reshape(g5, n5 * c5), w["s5_b_im"][l].reshape(g5, n5 * c5), expand]
    lbr, lbi, bbr, bbi = _s5_prep(prep_in, "s5_prep")
    p["s5_prep_in"] = prep_in
    p["lam_re_rows"], p["lam_im_rows"] = _s5_lam_rows(lbr, d), _s5_lam_rows(lbi, d)
    to_cn = lambda v: v.astype(bf16).reshape(g5, n5, c5).transpose(0, 2, 1)
    p["bsb"] = jnp.concatenate([_s5_block_diag(to_cn(bbr), d), _s5_block_diag(to_cn(bbi), d)], axis=1)
    c_re = w["s5_c_re"][l].astype(bf16).transpose(0, 2, 1)
    c_im = w["s5_c_im"][l].astype(bf16).transpose(0, 2, 1)
    nsb = d["NSB"]
    csb = jnp.stack([_s5_block_diag(c_re, d).reshape(nsb, -1, S5_SUPERBLOCK),
                     _s5_block_diag(-c_im, d).reshape(nsb, -1, S5_SUPERBLOCK)], axis=1)
    p["csb"] = csb.reshape(-1, S5_SUPERBLOCK)
    return p


def _s5_param_grads(g, p, d, l):
    g5, n5, c5, nsb, gsb = d["S5G"], d["S5N"], d["S5C"], d["NSB"], d["GSB"]
    wst = gsb * n5
    dbsb = g["bsb"]
    from_cn = lambda v: v.transpose(0, 2, 1).reshape(g5, n5 * c5)
    dbbr = from_cn(_s5_diag_blocks(dbsb[:, :wst], d, c5, n5))
    dbbi = from_cn(_s5_diag_blocks(dbsb[:, wst:], d, c5, n5))
    rows = lambda v: v.reshape(nsb, 8, wst)[:, 0, :].reshape(g5, n5)
    cots = [rows(g["lam_re_rows"]), rows(g["lam_im_rows"]), dbbr, dbbi]
    dlr, dli, dls, dbr, dbi = _s5_prep_bwd(p["s5_prep_in"], cots, "s5_prep_bwd")
    dcsb = g["csb"].reshape(nsb, 2, wst, S5_SUPERBLOCK)
    dcr = _s5_diag_blocks(dcsb[:, 0].reshape(-1, S5_SUPERBLOCK), d, n5, c5).transpose(0, 2, 1)
    dci = -_s5_diag_blocks(dcsb[:, 1].reshape(-1, S5_SUPERBLOCK), d, n5, c5).transpose(0, 2, 1)
    return dict(s5_lambda_re=dlr, s5_lambda_im=dli, s5_log_step=dls.reshape(g5), s5_b_re=dbr.reshape(g5, n5, c5),
                s5_b_im=dbi.reshape(g5, n5, c5), s5_c_re=dcr, s5_c_im=dci)


def kernel(x, ffn1_pre_g, ffn1_post_g, ffn1_w_gate, ffn1_w_up, ffn1_w_down, mix_pre_g, mix_post_g, w_in, ssd_conv_w, ssd_conv_b, ssd_dt_bias, ssd_a_log, ssd_d, ssd_norm_g, w_branch_a, s5_lambda_re, s5_lambda_im, s5_b_re, s5_b_im, s5_c_re, s5_c_im, s5_log_step, s5_d, s5_w_glu, w_branch_b, w_out, ffn2_pre_g, ffn2_post_g, ffn2_w_gate, ffn2_w_up, ffn2_w_down, loss_target, m_ffn1_pre_g, m_ffn1_post_g, m_ffn1_w_gate, m_ffn1_w_up, m_ffn1_w_down, m_mix_pre_g, m_mix_post_g, m_w_in, m_ssd_conv_w, m_ssd_conv_b, m_ssd_dt_bias, m_ssd_a_log, m_ssd_d, m_ssd_norm_g, m_w_branch_a, m_s5_lambda_re, m_s5_lambda_im, m_s5_b_re, m_s5_b_im, m_s5_c_re, m_s5_c_im, m_s5_log_step, m_s5_d, m_s5_w_glu, m_w_branch_b, m_w_out, m_ffn2_pre_g, m_ffn2_post_g, m_ffn2_w_gate, m_ffn2_w_up, m_ffn2_w_down, v_ffn1_pre_g, v_ffn1_post_g, v_ffn1_w_gate, v_ffn1_w_up, v_ffn1_w_down, v_mix_pre_g, v_mix_post_g, v_w_in, v_ssd_conv_w, v_ssd_conv_b, v_ssd_dt_bias, v_ssd_a_log, v_ssd_d, v_ssd_norm_g, v_w_branch_a, v_s5_lambda_re, v_s5_lambda_im, v_s5_b_re, v_s5_b_im, v_s5_c_re, v_s5_c_im, v_s5_log_step, v_s5_d, v_s5_w_glu, v_w_branch_b, v_w_out, v_ffn2_pre_g, v_ffn2_post_g, v_ffn2_w_gate, v_ffn2_w_up, v_ffn2_w_down):
    given = dict(locals())
    for n in COL_SHARDED:
        for prefix in ("", "m_", "v_"):
            given[prefix + n] = given[prefix + n].transpose(0, 2, 1)
    w = {n: given[n] for n in WEIGHTS}
    mom = {n: given["m_" + n] for n in WEIGHTS}
    var = {n: given["v_" + n] for n in WEIGHTS}
    d = _dims(w, x)
    n_layers = w["ffn1_pre_g"].shape[0]
    T, D = d["T"], d["D"]

    conv_w = w["ssd_conv_w"]
    first, middle, last = BIG[:3], BIG[3:8], BIG[8:]
    wf, _ = _unpack_weights(_chip_all_gather(_pack_weights(w, 0, first), "gather_weights_first"), w, 0, first)
    coming = _chip_gather_start(_pack_weights(w, 0, middle, conv_w), "gather_start_l0")
    h = x.reshape(T, D) + coming[-1][0, 0]
    saved, layers = [], []
    for l in range(n_layers):
        if l > 0:
            pack, land = _chip_gather_wait(*coming[:4], h, f"gather_wait_l{l}")
            wf, _ = _unpack_weights(_chip_gather_finish(pack, land, f"gather_finish_l{l}"), w, l, BIG)
        p = _ffn1_params(l, w, wf)
        h, s1 = _ffn_fwd(h, p["ffn1_pre_g"], p["ffn1_post_g"], p["wgu1"], p["wd1"], "ffn1")
        if l == 0:
            pack, land = _chip_gather_wait(*coming[:4], h, "gather_wait_l0")
            wf, conv_w_full = _unpack_weights(_chip_gather_finish(pack, land, "gather_finish_l0"), w, 0, middle, conv_w)
            ffn2_coming = _chip_gather_start(_pack_weights(w, 0, last), "gather_start_l0_ffn2")
            h = h + ffn2_coming[-1][0, 0]
        if l + 1 < n_layers:
            coming = _chip_gather_start(_pack_weights(w, l + 1, BIG), f"gather_start_l{l + 1}")
            h = h + coming[-1][0, 0]
        p.update(_layer_params(l, w, wf, conv_w_full, d))
        layers.append(p)
        h, sm = _mixer_fwd(h, p, d)
        if l == 0:
            pack, land = _chip_gather_wait(*ffn2_coming[:4], h, "gather_wait_l0_ffn2")
            wf, _ = _unpack_weights(_chip_gather_finish(pack, land, "gather_finish_l0_ffn2"), w, 0, last)
        p.update(_ffn2_params(wf))
        h, s2 = _ffn_fwd(h, p["ffn2_pre_g"], p["ffn2_post_g"], p["wgu2"], p["wd2"], "ffn2")
        saved.append((s1, sm, s2))
    dh, loss_part = _row_kernel("loss", _loss_fn, [(h, D, 0), (loss_target.reshape(T, D), D, 0)], [], [(D, f32)], [(8, LANES)])
    loss = lax.psum(loss_part[0, 0], ("x", "y", "c"))

    my_core = lax.axis_index("c").astype(jnp.int32).reshape(1)
    my_chip = (2 * lax.axis_index("x") + lax.axis_index("y")).astype(jnp.int32).reshape(1)
    lg, exchanging, in_flight = [None] * n_layers, [None] * n_layers, [None] * n_layers
    for l in reversed(range(n_layers)):
        p = layers[l]
        s1, sm, s2 = saved[l]
        dh, g2 = _ffn_bwd(dh, s2, p["ffn2_pre_g"], p["ffn2_post_g"], p["wgu2"], p["wd2"], "ffn2")
        if l + 1 < n_layers:
            in_flight[l + 1] = _reduce_start(exchanging[l + 1], my_core, dh, f"grads_l{l + 1}")
            dh = dh + in_flight[l + 1][-1][0, 0]
        dh, gm = _mixer_bwd(dh, sm, p, d)
        dh, g1 = _ffn_bwd(dh, s1, p["ffn1_pre_g"], p["ffn1_post_g"], p["wgu1"], p["wd1"], "ffn1")
        H = p["wd1"].shape[0]
        gl = dict(ffn1_pre_g=g1["pre_g"], ffn1_post_g=g1["post_g"], ffn1_w_gate=g1["wgu"][:H], ffn1_w_up=g1["wgu"][H:],
                  ffn1_w_down=g1["wd"], ffn2_pre_g=g2["pre_g"], ffn2_post_g=g2["post_g"], ffn2_w_gate=g2["wgu"][:H],
                  ffn2_w_up=g2["wgu"][H:], ffn2_w_down=g2["wd"], mix_pre_g=gm["mix_pre_g"], mix_post_g=gm["mix_post_g"],
                  w_in=_w_in_unperm(gm["w_in"], d), ssd_conv_w=gm["conv_w"], ssd_conv_b=gm["conv_b"],
                  ssd_dt_bias=_head_unpad(gm["dt_bias"], d), ssd_a_log=_head_unpad(gm["a_log"], d),
                  ssd_d=_head_unpad(gm["d_skip"], d), ssd_norm_g=gm["norm_g"], w_branch_a=gm["w_a"], s5_d=gm["s5_d"],
                  s5_w_glu=gm["w_glu"], w_branch_b=gm["w_b"], w_out=gm["w_out"])
        gl.update(_s5_param_grads(gm, p, d, l))
        lg[l] = gl
        exchanging[l] = _core_send_other_half_start(_pack_big_grads([gl]), f"exchange_start_grads_l{l}")
        if l > 0:
            dh = dh + exchanging[l][-1][0, 0]
    grad_x = dh.reshape(x.shape)
    summed = [None] * n_layers
    for l in range(1, n_layers):
        summed[l] = _reduce_finish(in_flight[l], my_chip, exchanging[0][-1], f"grads_l{l}")
    in_flight[0] = _reduce_start(exchanging[0], my_core, summed[-1] if n_layers > 1 else grad_x, "grads_l0")
    small_names = SMALL + ["ssd_conv_w"]
    small_parts = {n: [g[n] for g in lg] for n in small_names}
    small_like = {n: jax.ShapeDtypeStruct((n_layers,) + lg[0][n].shape, f32) for n in small_names}
    small_like.update({n: w[n] for n in SMALL})
    small_pack = _pack_small(small_parts, small_names)
    small_sum = _reduce_to_chips(jnp.broadcast_to(small_pack, (N_CHIPS,) + small_pack.shape), my_core, "small")
    small = _unpack_small(small_sum, small_like, small_names)
    summed[0] = _reduce_finish(in_flight[0], my_chip, small_sum, "grads_l0")
    big_grads = _unpack_big_grads(summed, w)
    grads = {}
    k_me = 2 * lax.axis_index("x") + lax.axis_index("y")
    cw = w["ssd_conv_w"].shape[-1]
    small["ssd_conv_w"] = lax.dynamic_slice_in_dim(small["ssd_conv_w"], k_me * cw, cw, axis=2)
    grads.update(small)

    delta, new_m, new_v = {}, {}, {}
    for n in BIG:
        grads[n], delta[n], new_m[n], new_v[n] = _adamw_layers(w[n], big_grads[n], mom[n], var[n], "adamw_" + n)
    n = "ssd_conv_w"
    delta[n], new_m[n], new_v[n] = _adamw(w[n], grads[n], mom[n], var[n], "adamw_" + n)
    pw, pm, pv = [_pack_small(t, SMALL) for t in (w, mom, var)]
    assert pw.shape[0] <= small_sum.shape[0]
    sd, sm_, sv = _adamw(pw, small_sum[:pw.shape[0]], pm, pv, "adamw_small")
    delta.update(_unpack_small(sd, w, SMALL))
    new_m.update(_unpack_small(sm_, w, SMALL))
    new_v.update(_unpack_small(sv, w, SMALL))
    for n in COL_SHARDED:
        for out in (grads, delta, new_m, new_v):
            out[n] = out[n].transpose(0, 2, 1)
    return (loss, grad_x, *[grads[n] for n in WEIGHTS], *[delta[n] for n in WEIGHTS],
            *[new_m[n] for n in WEIGHTS], *[new_v[n] for n in WEIGHTS])
```

```python
import functools

import numpy as np
import jax
import jax.numpy as jnp
from jax import lax
from jax.experimental import pallas as pl
from jax.experimental.pallas import tpu as pltpu

f32, bf16 = jnp.float32, jnp.bfloat16

SSD_N_GROUPS = 4
SSD_CHUNK = 128
RMS_EPS = 1e-6
S5_MAX_REAL = -1e-4
S5_SUPERBLOCK = 256
ADAM_LR, ADAM_B1, ADAM_B2, ADAM_EPS, ADAM_WD, ADAM_STEP = 0.001, 0.9, 0.999, 1e-08, 0.01, 10

LANES = 128
PACK_COLS = 1024
D2D_STREAMS = 16
PACK_ROW_MULT = 2 * D2D_STREAMS * 16
VMEM_LIMIT_BYTES = 48 * 1024 * 1024
N_CHIPS, N_CORES, N_DEV = 4, 2, 8
MESH = pl.DeviceIdType.MESH

BIG = ["ffn1_w_gate", "ffn1_w_up", "ffn1_w_down", "w_in", "w_branch_a", "s5_w_glu", "w_branch_b", "w_out",
       "ffn2_w_gate", "ffn2_w_up", "ffn2_w_down"]
COL_SHARDED = ["ffn1_w_gate", "ffn1_w_up", "w_in", "s5_w_glu", "ffn2_w_gate", "ffn2_w_up"]
SMALL = ["ffn1_pre_g", "ffn1_post_g", "mix_pre_g", "mix_post_g", "ssd_conv_b", "ssd_norm_g", "s5_lambda_re", "s5_lambda_im",
         "s5_b_re", "s5_b_im", "s5_c_re", "s5_c_im", "s5_d", "ffn2_pre_g", "ffn2_post_g", "s5_log_step", "ssd_dt_bias",
         "ssd_a_log", "ssd_d"]
WEIGHTS = ["ffn1_pre_g", "ffn1_post_g", "ffn1_w_gate", "ffn1_w_up", "ffn1_w_down", "mix_pre_g", "mix_post_g", "w_in",
           "ssd_conv_w", "ssd_conv_b", "ssd_dt_bias", "ssd_a_log", "ssd_d", "ssd_norm_g", "w_branch_a", "s5_lambda_re",
           "s5_lambda_im", "s5_b_re", "s5_b_im", "s5_c_re", "s5_c_im", "s5_log_step", "s5_d", "s5_w_glu", "w_branch_b",
           "w_out", "ffn2_pre_g", "ffn2_post_g", "ffn2_w_gate", "ffn2_w_up", "ffn2_w_down"]


def _params(sem=None):
    return pltpu.CompilerParams(dimension_semantics=sem, vmem_limit_bytes=VMEM_LIMIT_BYTES)


def _pick(n, target, mult=LANES):
    best = None
    for d in range(mult, min(n, target) + 1, mult):
        if n % d == 0:
            best = d
    return best if best is not None else n


_DIMS = {"nn": (((1,), (0,)), ((), ())), "nt": (((1,), (1,)), ((), ())), "tn": (((0,), (0,)), ((), ()))}


def _mm(a, b, mode, out_dtype, name, bm_t=1024, bn_t=1024, bk_t=2816):
    if mode == "nn":
        (M, K), (K2, N) = a.shape, b.shape
    elif mode == "nt":
        (M, K), (N, K2) = a.shape, b.shape
    else:
        (K, M), (K2, N) = a.shape, b.shape
    assert K == K2, (name, a.shape, b.shape)
    bm, bn, bk = _pick(M, bm_t), _pick(N, bn_t), _pick(K, bk_t)
    nk = K // bk
    dn = _DIMS[mode]

    def body(a_ref, b_ref, o_ref, *scratch):
        p = lax.dot_general(a_ref[...].astype(bf16), b_ref[...].astype(bf16), dn, preferred_element_type=f32)
        if nk == 1:
            o_ref[...] = p.astype(o_ref.dtype)
        else:
            acc = scratch[0]
            k = pl.program_id(2)

            @pl.when(k == 0)
            def _():
                acc[...] = p

            @pl.when(k > 0)
            def _():
                acc[...] += p

            @pl.when(k == nk - 1)
            def _():
                o_ref[...] = acc[...].astype(o_ref.dtype)

    if mode == "tn":
        a_spec = pl.BlockSpec((bk, bm), lambda i, j, k: (k, i))
    else:
        a_spec = pl.BlockSpec((bm, bk), lambda i, j, k: (i, k))
    if mode == "nt":
        b_spec = pl.BlockSpec((bn, bk), lambda i, j, k: (j, k))
    else:
        b_spec = pl.BlockSpec((bk, bn), lambda i, j, k: (k, j))
    return pl.pallas_call(
        body, name=name, grid=(M // bm, N // bn, nk), in_specs=[a_spec, b_spec],
        out_specs=pl.BlockSpec((bm, bn), lambda i, j, k: (i, j)), out_shape=jax.ShapeDtypeStruct((M, N), out_dtype),
        scratch_shapes=[pltpu.VMEM((bm, bn), f32)] if nk > 1 else [],
        compiler_params=_params(("parallel", "parallel", "arbitrary")),
    )(a, b)


def _mm_swiglu(x, wgu, name, bm_t=1024, bn_t=512):
    M, K = x.shape
    H = wgu.shape[0] // 2
    bm, bn = _pick(M, bm_t), _pick(H, bn_t)
    nj = H // bn

    def body(x_ref, g_ref, u_ref, a_ref, b_ref, h_ref):
        xb = x_ref[...].astype(bf16)
        a = lax.dot_general(xb, g_ref[...].astype(bf16), _DIMS["nt"], preferred_element_type=f32).astype(bf16)
        b = lax.dot_general(xb, u_ref[...].astype(bf16), _DIMS["nt"], preferred_element_type=f32).astype(bf16)
        a_ref[...] = a
        b_ref[...] = b
        h_ref[...] = (jax.nn.silu(a.astype(f32)) * b.astype(f32)).astype(bf16)

    out = pl.BlockSpec((bm, bn), lambda i, j: (i, j))
    return pl.pallas_call(
        body, name=name, grid=(M // bm, nj),
        in_specs=[pl.BlockSpec((bm, K), lambda i, j: (i, 0)), pl.BlockSpec((bn, K), lambda i, j: (j, 0)),
                  pl.BlockSpec((bn, K), lambda i, j: (nj + j, 0))],
        out_specs=[out] * 3, out_shape=[jax.ShapeDtypeStruct((M, H), bf16)] * 3,
        compiler_params=_params(("parallel", "parallel")),
    )(x, wgu, wgu)


def _bdmm(a, w, mode, nb, out_dtype, name, bt_t=1024):
    if mode == "tn":
        T = a.shape[0]
        ka, nw = a.shape[1] // nb, w.shape[1] // nb
        bt = _pick(T, bt_t)
        nt = T // bt

        def body_tn(a_ref, b_ref, o_ref):
            p = lax.dot_general(a_ref[...].astype(bf16), b_ref[...].astype(bf16), _DIMS["tn"], preferred_element_type=f32)
            k = pl.program_id(1)

            @pl.when(k == 0)
            def _():
                o_ref[...] = p

            @pl.when(k > 0)
            def _():
                o_ref[...] += p

        return pl.pallas_call(
            body_tn, name=name, grid=(nb, nt),
            in_specs=[pl.BlockSpec((bt, ka), lambda j, k: (k, j)), pl.BlockSpec((bt, nw), lambda j, k: (k, j))],
            out_specs=pl.BlockSpec((ka, nw), lambda j, k: (j, 0)), out_shape=jax.ShapeDtypeStruct((nb * ka, nw), f32),
            compiler_params=_params(("parallel", "arbitrary")),
        )(a, w)
    T = a.shape[0]
    ka, nw = w.shape[0] // nb, w.shape[1]
    bt = _pick(T, bt_t)
    kin, kout = (ka, nw) if mode == "nn" else (nw, ka)
    dn = _DIMS[mode]

    def body(a_ref, w_ref, o_ref):
        o_ref[...] = lax.dot_general(a_ref[...].astype(bf16), w_ref[...].astype(bf16), dn,
                                     preferred_element_type=f32).astype(o_ref.dtype)

    return pl.pallas_call(
        body, name=name, grid=(nb, T // bt),
        in_specs=[pl.BlockSpec((bt, kin), lambda j, i: (i, j)), pl.BlockSpec((ka, nw), lambda j, i: (j, 0))],
        out_specs=pl.BlockSpec((bt, kout), lambda j, i: (i, j)), out_shape=jax.ShapeDtypeStruct((T, nb * kout), out_dtype),
        compiler_params=_params(("parallel", "parallel")),
    )(a, w)


def _row_index(i, cb):
    return (i, cb)


def _row_kernel(name, fn, rows, pars, row_outs, par_outs=(), block_rows=256):
    T = rows[0][0].shape[0]
    widest = max([nc for (_, nc, _) in rows] + [nc for (nc, _) in row_outs])
    R = min(block_rows if widest > 1024 else 2 * block_rows, T)
    assert T % R == 0
    nr, npar, nro = len(rows), len(pars), len(row_outs)

    def body(*refs):
        rv = [r[...] for r in refs[:nr]]
        pv = [r[...] for r in refs[nr:nr + npar]]
        ro, po = fn(rv, pv)
        for ref, v in zip(refs[nr + npar:nr + npar + nro], ro):
            ref[...] = v.astype(ref.dtype)
        if par_outs:
            i = pl.program_id(0)
            prefs = refs[nr + npar + nro:]

            @pl.when(i == 0)
            def _():
                for ref, v in zip(prefs, po):
                    ref[...] = v.astype(f32)

            @pl.when(i > 0)
            def _():
                for ref, v in zip(prefs, po):
                    ref[...] += v.astype(f32)

    in_specs = [pl.BlockSpec((R, nc), functools.partial(_row_index, cb=cb)) for (_, nc, cb) in rows]
    in_specs += [pl.BlockSpec(p.shape, lambda i: (0, 0)) for p in pars]
    out_specs = [pl.BlockSpec((R, nc), lambda i: (i, 0)) for (nc, _) in row_outs]
    out_specs += [pl.BlockSpec(s, lambda i: (0, 0)) for s in par_outs]
    out_shape = [jax.ShapeDtypeStruct((T, nc), dt) for (nc, dt) in row_outs]
    out_shape += [jax.ShapeDtypeStruct(s, f32) for s in par_outs]
    outs = pl.pallas_call(
        body, name=name, grid=(T // R,), in_specs=in_specs, out_specs=out_specs, out_shape=out_shape,
        compiler_params=_params(("arbitrary",) if par_outs else ("parallel",)),
    )(*[r[0] for r in rows], *pars)
    return list(outs)


def _fwd_of(f):
    def fn(rv, pv):
        return f([v.astype(f32) for v in rv], [v.astype(f32) for v in pv]), []
    return fn


def _vjp_of(f, n_x, n_cot, grad_idx, n_add=0):
    def fn(rv, pv):
        xs = [v.astype(f32) for v in rv[:n_x]]
        cots = [v.astype(f32) for v in rv[n_x:n_x + n_cot]]
        adds = rv[n_x + n_cot:n_x + n_cot + n_add]
        ps = [v.astype(f32) for v in pv]
        _, vjp = jax.vjp(lambda *a: f(list(a[:n_x]), list(a[n_x:])), *xs, *ps)
        g = vjp(cots)
        row_g = [g[i] for i in grad_idx]
        for k, a in enumerate(adds):
            row_g[k] = row_g[k] + a.astype(f32)
        return row_g, list(g[n_x:])
    return fn


def _rms(x, g):
    return x * lax.rsqrt(jnp.mean(x * x, axis=-1, keepdims=True) + RMS_EPS) * g


def _f_norm(xs, ps):
    return [_rms(xs[0], ps[0])]


def _f_post(scale):
    def f(xs, ps):
        return [scale * _rms(xs[0], ps[0])]
    return f


def _f_resnorm(scale):
    def f(xs, ps):
        return [xs[0] + scale * _rms(xs[1], ps[0])]
    return f


def _f_dt(xs, ps):
    dt = jax.nn.softplus(xs[0] + ps[0])
    return [dt, -jnp.exp(ps[1]) * dt]


def _f_ssdpost(n_groups):
    def f(xs, ps):
        y = xs[0] * jax.nn.silu(xs[1])
        width = y.shape[-1] // n_groups
        lane = lax.broadcasted_iota(jnp.int32, y.shape, 1)
        scale = jnp.zeros_like(y)
        for k in range(n_groups):
            m = ((lane >= k * width) & (lane < (k + 1) * width)).astype(f32)
            ms = jnp.sum(y * y * m, axis=-1, keepdims=True) / width
            scale = scale + lax.rsqrt(ms + RMS_EPS) * m
        return [y * scale * ps[0]]
    return f


def _f_s5post(xs, ps):
    return [jax.nn.gelu(xs[0] + ps[0] * xs[1])]


def _f_merge(xs, ps):
    return [jax.nn.sigmoid(xs[0]) * xs[1] + jax.nn.sigmoid(xs[2]) * xs[3]]


def _swiglu_bwd(rv, pv):
    a, b, d = rv[0].astype(f32), rv[1].astype(f32), rv[2].astype(f32)
    s = jax.nn.sigmoid(a)
    return [jnp.concatenate([d * b * (s * (1.0 + a * (1.0 - s))), d * (a * s)], axis=1)], []


def _glu_fwd(rv, pv):
    vg = rv[0].astype(f32)
    h = vg.shape[1] // 2
    return [vg[:, :h] * jax.nn.sigmoid(vg[:, h:])], []


def _glu_bwd(rv, pv):
    vg, d = rv[0].astype(f32), rv[1].astype(f32)
    h = vg.shape[1] // 2
    s = jax.nn.sigmoid(vg[:, h:])
    return [jnp.concatenate([d * s, d * vg[:, :h] * s * (1.0 - s)], axis=1)], []


def _loss_fn(rv, pv):
    e = rv[0].astype(f32) - rv[1].astype(f32)
    per_tok = jnp.mean(e * e, axis=-1, keepdims=True)
    part = 0.5 * jnp.sum(per_tok, axis=0, keepdims=True)
    return [e / e.shape[-1]], [jnp.broadcast_to(part, (8, LANES))]


def _add_fn(rv, pv):
    return [rv[0].astype(f32) + rv[1].astype(f32)], []


def _adamw_fn(rv, pv):
    w, g, m, v = [x.astype(f32) for x in rv]
    m = ADAM_B1 * m + (1.0 - ADAM_B1) * g
    v = ADAM_B2 * v + (1.0 - ADAM_B2) * (g * g)
    m_hat = m / (1.0 - ADAM_B1 ** ADAM_STEP)
    v_hat = v / (1.0 - ADAM_B2 ** ADAM_STEP)
    return [-ADAM_LR * (m_hat / (jnp.sqrt(v_hat) + ADAM_EPS) + ADAM_WD * w), m, v], []


def _adamw_layers(w, gs, m, v, name):
    n_layers, r, cols = w.shape
    br, bc = _pick(r, 256, 8), cols
    if br < 64 and cols % LANES == 0:
        br, bc = r, LANES

    def body(w_ref, *refs):
        g_refs, (m_ref, v_ref, g_out, d_ref, nm_ref, nv_ref) = refs[:n_layers], refs[n_layers:]
        layer = pl.program_id(0)
        g = g_refs[0][...]
        for k in range(1, n_layers):
            g = jnp.where(layer == k, g_refs[k][...], g)
        outs, _ = _adamw_fn([w_ref[0], g, m_ref[0], v_ref[0]], [])
        g_out[0] = g
        d_ref[0], nm_ref[0], nv_ref[0] = outs

    stacked = pl.BlockSpec((1, br, bc), lambda l, i, j: (l, i, j))
    single = pl.BlockSpec((br, bc), lambda l, i, j: (i, j))
    return pl.pallas_call(
        body, name=name, grid=(n_layers, r // br, cols // bc), in_specs=[stacked] + [single] * n_layers + [stacked] * 2,
        out_specs=[stacked] * 4, out_shape=[jax.ShapeDtypeStruct(w.shape, f32)] * 4,
        compiler_params=_params(("parallel", "parallel", "parallel")),
    )(w, *gs, m, v)


def _adamw(w, g, m, v, name):
    shape = w.shape
    cols = shape[-1] if (w.ndim >= 2 and shape[-1] >= LANES) else None
    if cols is None:
        n = int(np.prod(shape))
        cols = LANES if n % LANES == 0 else n
    n_rows = int(np.prod(shape)) // cols
    br, bc = _pick(n_rows, 256, 8), cols
    if br < 64 and cols % LANES == 0:
        br, bc = n_rows, LANES

    def body(w_ref, g_ref, m_ref, v_ref, d_ref, nm_ref, nv_ref):
        outs, _ = _adamw_fn([w_ref[...], g_ref[...], m_ref[...], v_ref[...]], [])
        d_ref[...], nm_ref[...], nv_ref[...] = outs

    spec = pl.BlockSpec((br, bc), lambda i, j: (i, j))
    outs = pl.pallas_call(
        body, name=name, grid=(n_rows // br, cols // bc), in_specs=[spec] * 4, out_specs=[spec] * 3,
        out_shape=[jax.ShapeDtypeStruct((n_rows, cols), f32)] * 3, compiler_params=_params(("parallel", "parallel")),
    )(*[t.reshape(n_rows, cols) for t in (w, g, m, v)])
    return [o.reshape(shape) for o in outs]


def _shift_down(x, s, row):
    if s == 0:
        return x
    return jnp.where(row >= s, pltpu.roll(x, s, 0), 0.0)


def _shift_up(x, s, row):
    if s == 0:
        return x
    n = x.shape[0]
    return jnp.where(row < n - s, pltpu.roll(x, n - s, 0), 0.0)


def _conv_pre(x, w, b, row):
    kw = w.shape[0]
    c = b
    for k in range(kw):
        c = c + w[k:k + 1, :] * _shift_down(x, kw - 1 - k, row)
    return c


def _conv_fwd(xsrc, col0, w, b, name, bc_t=512):
    T = xsrc.shape[0]
    kw, ncols = w.shape
    bc = _pick(ncols, bc_t)
    off = col0 // bc
    assert col0 % bc == 0

    def body(x_ref, w_ref, b_ref, o_ref):
        x = x_ref[...].astype(f32)
        row = lax.broadcasted_iota(jnp.int32, x.shape, 0)
        c = _conv_pre(x, w_ref[...], b_ref[...], row)
        o_ref[...] = c * jax.nn.sigmoid(c)

    return pl.pallas_call(
        body, name=name, grid=(ncols // bc,),
        in_specs=[pl.BlockSpec((T, bc), lambda j: (0, off + j)), pl.BlockSpec((kw, bc), lambda j: (0, j)),
                  pl.BlockSpec((1, bc), lambda j: (0, j))],
        out_specs=pl.BlockSpec((T, bc), lambda j: (0, j)), out_shape=jax.ShapeDtypeStruct((T, ncols), f32),
        compiler_params=_params(("parallel",)),
    )(xsrc, w, b)


def _conv_bwd(xsrc, col0, w, b, dact, name, bc_t=512):
    T = xsrc.shape[0]
    kw, ncols = w.shape
    bc = _pick(ncols, bc_t)
    off = col0 // bc
    assert col0 % bc == 0

    def body(x_ref, w_ref, b_ref, d_ref, dx_ref, dw_ref, db_ref):
        x = x_ref[...].astype(f32)
        w = w_ref[...]
        row = lax.broadcasted_iota(jnp.int32, x.shape, 0)
        c = _conv_pre(x, w, b_ref[...], row)
        s = jax.nn.sigmoid(c)
        dc = d_ref[...].astype(f32) * (s * (1.0 + c * (1.0 - s)))
        dx = jnp.zeros_like(x)
        dws = []
        for k in range(kw):
            dx = dx + w[k:k + 1, :] * _shift_up(dc, kw - 1 - k, row)
            dws.append(jnp.sum(dc * _shift_down(x, kw - 1 - k, row), axis=0, keepdims=True))
        dx_ref[...] = dx.astype(dx_ref.dtype)
        dw_ref[...] = jnp.concatenate(dws, axis=0)
        db_ref[...] = jnp.sum(dc, axis=0, keepdims=True)

    return pl.pallas_call(
        body, name=name, grid=(ncols // bc,),
        in_specs=[pl.BlockSpec((T, bc), lambda j: (0, off + j)), pl.BlockSpec((kw, bc), lambda j: (0, j)),
                  pl.BlockSpec((1, bc), lambda j: (0, j)), pl.BlockSpec((T, bc), lambda j: (0, j))],
        out_specs=[pl.BlockSpec((T, bc), lambda j: (0, j)), pl.BlockSpec((kw, bc), lambda j: (0, j)),
                   pl.BlockSpec((1, bc), lambda j: (0, j))],
        out_shape=[jax.ShapeDtypeStruct((T, ncols), bf16), jax.ShapeDtypeStruct((kw, ncols), f32),
                   jax.ShapeDtypeStruct((1, ncols), f32)],
        compiler_params=_params(("parallel",)),
    )(xsrc, w, b, dact)


_HI = lax.Precision.HIGHEST


def _dot(a, b, dims="nn", precision=None):
    return lax.dot_general(a, b, _DIMS[dims], preferred_element_type=f32, precision=precision)


def _dot01(a, b, dims="nn", ones="b"):
    x = a if ones == "b" else b
    hi = x.astype(bf16)
    rest = x - hi.astype(f32)
    mid = rest.astype(bf16)
    lo = (rest - mid.astype(f32)).astype(bf16)
    if ones == "b":
        e = b.astype(bf16)
        return _dot(hi, e, dims) + _dot(mid, e, dims) + _dot(lo, e, dims)
    e = a.astype(bf16)
    return _dot(e, hi, dims) + _dot(e, mid, dims) + _dot(e, lo, dims)


def _ssd_common(x_ref, b_ref, c_ref, dt_ref, adt_ref, d_ref, hpg, p):
    q = b_ref.shape[0]
    hp = hpg * p
    bb, cb = b_ref[...].astype(bf16), c_ref[...].astype(bf16)
    r = lax.broadcasted_iota(jnp.int32, (q, q), 0)
    s = lax.broadcasted_iota(jnp.int32, (q, q), 1)
    tril = r >= s
    trilf = tril.astype(f32)
    eh = lax.broadcasted_iota(jnp.int32, (LANES, hp), 0)
    ec = lax.broadcasted_iota(jnp.int32, (LANES, hp), 1)
    expand = ((ec >= eh * p) & (ec < (eh + 1) * p)).astype(f32)
    adt = adt_ref[...]
    cum = _dot01(trilf, adt, "nn", "a")
    cum_t = _dot01(adt, (r <= s).astype(f32), "tn")
    cum_e = _dot01(cum, expand)
    dt_e = _dot01(dt_ref[...], expand)
    d_e = _dot01(jnp.broadcast_to(d_ref[...], (8, LANES)), expand)[0:1, :]
    gmat = _dot(cb, bb, "nt")
    x = x_ref[...]
    xdt = x * dt_e
    e_all = jnp.exp(cum_e)
    dec = jnp.exp(cum_e[q - 1:q, :] - cum_e)
    lms, ms = [], []
    for h in range(hpg):
        lm = jnp.exp(jnp.where(tril, cum[:, h:h + 1] - cum_t[h:h + 1, :], -1e30))
        lms.append(lm)
        ms.append(gmat * lm)
    et = [jnp.exp(cum[q - 1:q, h:h + 1]) for h in range(hpg)]
    return dict(bb=bb, cb=cb, trilf=trilf, expand=expand, cum=cum, x=x, xdt=xdt, dt_e=dt_e, d_e=d_e, e=e_all, dec=dec,
                lms=lms, ms=ms, et=et)


def _ssd_specs(q, hp, n, g_n, nc, rev):
    def cidx(c):
        return (nc - 1 - c) if rev else c
    x_spec = pl.BlockSpec((q, hp), lambda g, c: (cidx(c), g))
    boff = (g_n * hp) // n
    b_spec = pl.BlockSpec((q, n), lambda g, c: (cidx(c), boff + g))
    c_spec = pl.BlockSpec((q, n), lambda g, c: (cidx(c), boff + g_n + g))
    dt_spec = pl.BlockSpec((q, LANES), lambda g, c: (cidx(c), g))
    d_spec = pl.BlockSpec((1, LANES), lambda g, c: (0, g))
    st_spec = pl.BlockSpec((1, 1, hp, n), lambda g, c: (cidx(c), g, 0, 0))
    return x_spec, b_spec, c_spec, dt_spec, d_spec, st_spec


def _ssd_fwd(act, dt, adt, dpad, hpg, p, n, name):
    T = act.shape[0]
    g_n, q = SSD_N_GROUPS, SSD_CHUNK
    nc, hp = T // q, hpg * p
    x_spec, b_spec, c_spec, dt_spec, d_spec, st_spec = _ssd_specs(q, hp, n, g_n, nc, False)

    def body(x_ref, b_ref, c_ref, dt_ref, adt_ref, d_ref, y_ref, st_ref, s_scr):
        @pl.when(pl.program_id(1) == 0)
        def _():
            s_scr[...] = jnp.zeros_like(s_scr)

        k = _ssd_common(x_ref, b_ref, c_ref, dt_ref, adt_ref, d_ref, hpg, p)
        s0 = s_scr[...]
        st_ref[0, 0] = s0
        xdtb = k["xdt"].astype(bf16)
        ydiag = [_dot(k["ms"][h].astype(bf16), xdtb[:, h * p:(h + 1) * p]) for h in range(hpg)]
        z = _dot(k["cb"], s0.astype(bf16), "nt")
        y_ref[...] = jnp.concatenate(ydiag, axis=1) + k["e"] * z + k["d_e"] * k["x"]
        upd = _dot((k["xdt"] * k["dec"]).astype(bf16), k["bb"], "tn")
        for h in range(hpg):
            s_scr[h * p:(h + 1) * p, :] = k["et"][h] * s0[h * p:(h + 1) * p, :] + upd[h * p:(h + 1) * p, :]

    return pl.pallas_call(
        body, name=name, grid=(g_n, nc),
        in_specs=[x_spec, b_spec, c_spec, dt_spec, dt_spec, d_spec],
        out_specs=[pl.BlockSpec((q, hp), lambda g, c: (c, g)), st_spec],
        out_shape=[jax.ShapeDtypeStruct((T, g_n * hp), f32), jax.ShapeDtypeStruct((nc, g_n, hp, n), f32)],
        scratch_shapes=[pltpu.VMEM((hp, n), f32)],
        compiler_params=_params(("parallel", "arbitrary")),
    )(act, act, act, dt, adt, dpad)


def _ssd_bwd(act, dt, adt, dpad, states, dy, hpg, p, n, name):
    T = act.shape[0]
    g_n, q = SSD_N_GROUPS, SSD_CHUNK
    nc, hp = T // q, hpg * p
    x_spec, b_spec, c_spec, dt_spec, d_spec, st_spec = _ssd_specs(q, hp, n, g_n, nc, True)

    def body(x_ref, b_ref, c_ref, dt_ref, adt_ref, d_ref, st_ref, dy_ref,
             dx_ref, db_ref, dc_ref, ddt_ref, dadt_ref, dd_ref, ds_scr):
        first = pl.program_id(1) == 0

        @pl.when(first)
        def _():
            ds_scr[...] = jnp.zeros_like(ds_scr)

        k = _ssd_common(x_ref, b_ref, c_ref, dt_ref, adt_ref, d_ref, hpg, p)
        bb, cb, expand, x, xdt, dec = k["bb"], k["cb"], k["expand"], k["x"], k["xdt"], k["dec"]
        heads = lambda t: _dot01(t, expand, "nt")
        s0 = st_ref[0, 0]
        ds1 = ds_scr[...]
        s0b, ds1b = s0.astype(bf16), ds1.astype(bf16)
        dy = dy_ref[...]
        dyb, xdtb = dy.astype(bf16), xdt.astype(bf16)
        lane = lax.broadcasted_iota(jnp.int32, (1, LANES), 1)
        dg = jnp.zeros((q, q), f32)
        w_rows = jnp.zeros((q, LANES), f32)
        w_cols, dxdt_parts = [], []
        for h in range(hpg):
            hs = slice(h * p, (h + 1) * p)
            dm = _dot(dyb[:, hs], xdtb[:, hs], "nt")
            dg = dg + dm * k["lms"][h]
            wm = dm * k["ms"][h]
            w_rows = w_rows + jnp.sum(wm, axis=1, keepdims=True) * (lane == h).astype(f32)
            w_cols.append(jnp.sum(wm, axis=0, keepdims=True))
            dxdt_parts.append(_dot(k["ms"][h].astype(bf16), dyb[:, hs], "tn"))
        dxdt_diag = jnp.concatenate(dxdt_parts, axis=1)
        w_cols = jnp.concatenate(w_cols + [jnp.zeros((LANES - hpg, q), f32)], axis=0).T
        dgb = dg.astype(bf16)
        z = _dot(cb, s0b, "nt")
        dz = dy * k["e"]
        dzb = dz.astype(bf16)
        dxd = _dot(bb, ds1b, "nt")
        ddec = dxd * xdt * dec
        db_ref[...] = _dot(dgb, cb, "tn") + _dot((xdt * dec).astype(bf16), ds1b)
        dc_ref[...] = _dot(dgb, bb) + _dot(dzb, s0b)
        ds0 = _dot(dzb, cb, "tn")
        for h in range(hpg):
            hs = slice(h * p, (h + 1) * p)
            ds_scr[hs, :] = ds0[hs, :] + k["et"][h] * ds1[hs, :]
        dxdt = dxdt_diag + dxd * dec
        ddec_h = heads(ddec)
        dcum = w_rows - w_cols + heads(dz * z) - ddec_h
        et_row = jnp.exp(k["cum"][q - 1:q, :])
        dsum = _dot01(jnp.ones((8, n), f32), _dot01(expand, ds1 * s0, "nn", "a"), "nt", "a")[0:1, :]
        dcl = dsum * et_row + jnp.sum(ddec_h, axis=0, keepdims=True)
        rowq = lax.broadcasted_iota(jnp.int32, (q, 1), 0)
        dcum = dcum + (rowq == q - 1).astype(f32) * dcl
        ddt_ref[...] = heads(dxdt * x)
        dadt_ref[...] = _dot01(k["trilf"], dcum, "tn", "a")
        dx_ref[...] = k["d_e"] * dy + dxdt * k["dt_e"]
        dd8 = heads(jnp.broadcast_to(jnp.sum(dy * x, axis=0, keepdims=True), (8, hp)))

        @pl.when(first)
        def _():
            dd_ref[...] = dd8

        @pl.when(jnp.logical_not(first))
        def _():
            dd_ref[...] += dd8

    rc = lambda g, c: (nc - 1 - c, g)
    return pl.pallas_call(
        body, name=name, grid=(g_n, nc),
        in_specs=[x_spec, b_spec, c_spec, dt_spec, dt_spec, d_spec, st_spec, pl.BlockSpec((q, hp), rc)],
        out_specs=[pl.BlockSpec((q, hp), rc), pl.BlockSpec((q, n), rc), pl.BlockSpec((q, n), rc),
                   pl.BlockSpec((q, LANES), rc), pl.BlockSpec((q, LANES), rc), pl.BlockSpec((8, LANES), lambda g, c: (g, 0))],
        out_shape=[jax.ShapeDtypeStruct((T, g_n * hp), f32), jax.ShapeDtypeStruct((T, g_n * n), f32),
                   jax.ShapeDtypeStruct((T, g_n * n), f32), jax.ShapeDtypeStruct((T, g_n * LANES), f32),
                   jax.ShapeDtypeStruct((T, g_n * LANES), f32), jax.ShapeDtypeStruct((g_n * 8, LANES), f32)],
        scratch_shapes=[pltpu.VMEM((hp, n), f32)],
        compiler_params=_params(("parallel", "arbitrary")),
    )(act, act, act, dt, adt, dpad, states, dy)


def _cmul(ar, ai, br, bi):
    return ar * br - ai * bi, ar * bi + ai * br


def _s5_tile_powers(lr, li):
    p = [(lr, li)]
    for _ in range(7):
        p.append(_cmul(p[-1][0], p[-1][1], lr, li))
    tile = (jnp.concatenate([q[0] for q in p], axis=0), jnp.concatenate([q[1] for q in p], axis=0))
    return tile, (p[0], p[1], p[3])


def _s5_tile_scan(xr, xi, steps, reverse):
    row = lax.broadcasted_iota(jnp.int32, xr.shape, 0)
    for d, (pr, pi) in zip((1, 2, 4), steps):
        if reverse:
            keep = row < 8 - d
            sr, si = pltpu.roll(xr, 8 - d, 0), pltpu.roll(xi, 8 - d, 0)
        else:
            keep = row >= d
            sr, si = pltpu.roll(xr, d, 0), pltpu.roll(xi, d, 0)
        sr, si = jnp.where(keep, sr, 0.0), jnp.where(keep, si, 0.0)
        ar, ai = _cmul(sr, si, pr, pi)
        xr, xi = xr + ar, xi + ai
    return xr, xi


def _s5_scan_fwd(bu, lam_re, lam_im, nsb, name, tc_t=512):
    T = bu.shape[0]
    w2 = bu.shape[1] // nsb
    w = w2 // 2
    tc = _pick(T, tc_t, 8)

    def body(bu_ref, lr_ref, li_ref, st_ref, carry):
        @pl.when(pl.program_id(1) == 0)
        def _():
            carry[...] = jnp.zeros_like(carry)

        (pr8, pi8), steps = _s5_tile_powers(lr_ref[0:1, :], li_ref[0:1, :])

        def tile(i, c):
            r = pl.ds(pl.multiple_of(i * 8, 8), 8)
            x = bu_ref[r, :]
            xr, xi = _s5_tile_scan(x[:, :w], x[:, w:], steps, False)
            ar, ai = _cmul(pr8, pi8, c[0], c[1])
            xr, xi = xr + ar, xi + ai
            st_ref[r, :] = jnp.concatenate([xr, xi], axis=1)
            return xr[7:8, :], xi[7:8, :]

        c = lax.fori_loop(0, tc // 8, tile, (carry[0:1, :], carry[1:2, :]), unroll=2)
        carry[0:1, :] = c[0]
        carry[1:2, :] = c[1]

    return pl.pallas_call(
        body, name=name, grid=(nsb, T // tc),
        in_specs=[pl.BlockSpec((tc, w2), lambda j, i: (i, j)), pl.BlockSpec((8, w), lambda j, i: (j, 0)),
                  pl.BlockSpec((8, w), lambda j, i: (j, 0))],
        out_specs=pl.BlockSpec((tc, w2), lambda j, i: (i, j)), out_shape=jax.ShapeDtypeStruct(bu.shape, f32),
        scratch_shapes=[pltpu.VMEM((8, w), f32)],
        compiler_params=_params(("parallel", "arbitrary")),
    )(bu, lam_re, lam_im)


def _s5_scan_bwd(gst, states, lam_re, lam_im, nsb, name, tc_t=512):
    T = gst.shape[0]
    w2 = gst.shape[1] // nsb
    w = w2 // 2
    tc = _pick(T, tc_t, 8)
    nt = T // tc
    n_tiles = tc // 8

    def body(g_ref, s_ref, sp_ref, lr_ref, li_ref, a_ref, dlr_ref, dli_ref, carry, acc):
        chunk = pl.program_id(1)

        @pl.when(chunk == 0)
        def _():
            carry[...] = jnp.zeros_like(carry)
            acc[...] = jnp.zeros_like(acc)

        (qr8, qi8), steps = _s5_tile_powers(lr_ref[0:1, :], -li_ref[0:1, :])
        row = lax.broadcasted_iota(jnp.int32, (8, w), 0)
        rev_r, rev_i = jnp.zeros((8, w), f32), jnp.zeros((8, w), f32)
        for r in range(8):
            rev_r = jnp.where(row == r, qr8[7 - r:8 - r, :], rev_r)
            rev_i = jnp.where(row == r, qi8[7 - r:8 - r, :], rev_i)
        row2 = lax.broadcasted_iota(jnp.int32, (8, w2), 0)

        def tile(k, c):
            ar_in, ai_in, dr, di = c
            i = n_tiles - 1 - k
            r = pl.ds(pl.multiple_of(i * 8, 8), 8)
            x = g_ref[r, :]
            xr, xi = _s5_tile_scan(x[:, :w], x[:, w:], steps, True)
            pr, pi = _cmul(rev_r, rev_i, ar_in, ai_in)
            xr, xi = xr + pr, xi + pi
            a_ref[r, :] = jnp.concatenate([xr, xi], axis=1)
            before = jnp.where(i > 0, s_ref[pl.ds(pl.multiple_of(jnp.maximum(i - 1, 0) * 8, 8), 8), :],
                               sp_ref[tc - 8:tc, :] * (chunk < nt - 1).astype(f32))
            prev = jnp.where(row2 == 0, pltpu.roll(before, 1, 0), pltpu.roll(s_ref[r, :], 1, 0))
            spr, spi = prev[:, :w], prev[:, w:]
            return xr[0:1, :], xi[0:1, :], dr + xr * spr + xi * spi, di - xr * spi + xi * spr

        c0 = (carry[0:1, :], carry[1:2, :], acc[0:8, :], acc[8:16, :])
        ar, ai, dr, di = lax.fori_loop(0, n_tiles, tile, c0, unroll=2)
        carry[0:1, :] = ar
        carry[1:2, :] = ai
        acc[0:8, :] = dr
        acc[8:16, :] = di
        dlr_ref[...] = jnp.broadcast_to(jnp.sum(dr, axis=0, keepdims=True), (8, w))
        dli_ref[...] = jnp.broadcast_to(jnp.sum(di, axis=0, keepdims=True), (8, w))

    cur = lambda j, i: (nt - 1 - i, j)
    prv = lambda j, i: (jnp.maximum(nt - 2 - i, 0), j)
    return pl.pallas_call(
        body, name=name, grid=(nsb, nt),
        in_specs=[pl.BlockSpec((tc, w2), cur), pl.BlockSpec((tc, w2), cur), pl.BlockSpec((tc, w2), prv),
                  pl.BlockSpec((8, w), lambda j, i: (j, 0)), pl.BlockSpec((8, w), lambda j, i: (j, 0))],
        out_specs=[pl.BlockSpec((tc, w2), cur), pl.BlockSpec((8, w), lambda j, i: (j, 0)),
                   pl.BlockSpec((8, w), lambda j, i: (j, 0))],
        out_shape=[jax.ShapeDtypeStruct(gst.shape, f32), jax.ShapeDtypeStruct((nsb * 8, w), f32),
                   jax.ShapeDtypeStruct((nsb * 8, w), f32)],
        scratch_shapes=[pltpu.VMEM((8, w), f32), pltpu.VMEM((16, w), f32)],
        compiler_params=_params(("parallel", "arbitrary")),
    )(gst, states, states, lam_re, lam_im)


def _s5_prep_fn(xs, ps):
    lam_re, lam_im, log_step, b_re, b_im, expand = ps
    lr = jnp.minimum(lam_re, S5_MAX_REAL)
    li = lam_im
    step = jnp.exp(log_step)
    er = jnp.exp(lr * step)
    ang = li * step
    lbr, lbi = er * jnp.cos(ang), er * jnp.sin(ang)
    nr, ni = lbr - 1.0, lbi
    den = lr * lr + li * li
    qr, qi = (nr * lr + ni * li) / den, (ni * lr - nr * li) / den
    qre, qie = _dot(qr, expand, "nn", _HI), _dot(qi, expand, "nn", _HI)
    return [lbr, lbi, qre * b_re - qie * b_im, qre * b_im + qie * b_re]


def _s5_prep(pars, name):
    def body(*refs):
        outs = _s5_prep_fn([], [r[...] for r in refs[:6]])
        for ref, v in zip(refs[6:], outs):
            ref[...] = v

    g, nst = pars[0].shape
    nc = pars[3].shape[1]
    return pl.pallas_call(
        body, name=name,
        out_shape=[jax.ShapeDtypeStruct((g, nst), f32)] * 2 + [jax.ShapeDtypeStruct((g, nc), f32)] * 2,
        compiler_params=_params(),
    )(*pars)


def _s5_prep_bwd(pars, cots, name):
    def body(*refs):
        ps = [r[...] for r in refs[:6]]
        ct = [r[...] for r in refs[6:10]]
        _, vjp = jax.vjp(lambda *a: _s5_prep_fn([], list(a)), *ps)
        g = vjp(ct)
        for ref, v in zip(refs[10:], g[:5]):
            ref[...] = v

    return pl.pallas_call(
        body, name=name, out_shape=[jax.ShapeDtypeStruct(p.shape, f32) for p in pars[:5]], compiler_params=_params(),
    )(*pars, *cots)


_ANY = pl.BlockSpec(memory_space=pl.ANY)


def _remote(src, dst, send_sem, recv_sem, device):
    return pltpu.make_async_remote_copy(src_ref=src, dst_ref=dst, send_sem=send_sem, recv_sem=recv_sem, device_id=device,
                                        device_id_type=MESH)


def _staged_copy(src, dst, buf, in_sems, out_sems):
    n = D2D_STREAMS
    piece = src.shape[0] // n
    assert src.shape[0] % n == 0

    def load(i):
        return pltpu.make_async_copy(src.at[pl.ds(i * piece, piece)], buf.at[i % 2], in_sems.at[i % 2])

    def store(i):
        return pltpu.make_async_copy(buf.at[i % 2], dst.at[pl.ds(i * piece, piece)], out_sems.at[i % 2])

    load(0).start()
    for i in range(n):
        if i + 1 < n:
            if i >= 1:
                store(i - 1).wait()
            load(i + 1).start()
        load(i).wait()
        store(i).start()
    store(n - 2).wait()
    store(n - 1).wait()


def _stage_scratch(rows, cols, dtype):
    return [pltpu.VMEM((2, rows // D2D_STREAMS, cols), dtype), pltpu.SemaphoreType.DMA((2,)), pltpu.SemaphoreType.DMA((2,))]


def _chip_all_gather(block, name):
    rows = block.shape[0]
    half = rows // 2
    piece = half // D2D_STREAMS
    assert rows % (2 * D2D_STREAMS * 16) == 0

    def body(src, out, ici_send, ici_recv, d2d_send, d2d_recv, *stage):
        x, y, c = lax.axis_index("x"), lax.axis_index("y"), lax.axis_index("c")
        me = 2 * x + y
        sibling = (x, y, 1 - c)
        chips = [(1 - x, y), (x, 1 - y), (1 - x, 1 - y)]
        mine = pl.ds(pl.multiple_of(c * half, 16), half)
        sends = []
        for j, (px, py) in enumerate(chips):
            cp = _remote(src.at[mine], out.at[me, mine], ici_send.at[j], ici_recv.at[j], (px, py, c))
            cp.start()
            sends.append(cp)
        _staged_copy(src, out.at[me], *stage)
        for j, (px, py) in enumerate(chips):
            slot = 2 * px + py
            _remote(src.at[mine], out.at[slot, mine], ici_send.at[j], ici_recv.at[j], (px, py, c)).wait_recv()
            for s in range(D2D_STREAMS):
                r = pl.ds(pl.multiple_of(c * half + s * piece, 16), piece)
                k = j * D2D_STREAMS + s
                cp = _remote(out.at[slot, r], out.at[slot, r], d2d_send.at[k], d2d_recv.at[k], sibling)
                cp.start()
                sends.append(cp)
        for j, (px, py) in enumerate(chips):
            slot = 2 * px + py
            for s in range(D2D_STREAMS):
                r = pl.ds(pl.multiple_of((1 - c) * half + s * piece, 16), piece)
                k = j * D2D_STREAMS + s
                _remote(out.at[slot, r], out.at[slot, r], d2d_send.at[k], d2d_recv.at[k], sibling).wait_recv()
        for cp in sends:
            cp.wait_send()

    n_d2d = 3 * D2D_STREAMS
    return pl.pallas_call(
        body, name=name, in_specs=[_ANY], out_specs=_ANY,
        out_shape=jax.ShapeDtypeStruct((N_CHIPS,) + block.shape, block.dtype),
        scratch_shapes=[pltpu.SemaphoreType.DMA((3,)), pltpu.SemaphoreType.DMA((3,)), pltpu.SemaphoreType.DMA((n_d2d,)),
                        pltpu.SemaphoreType.DMA((n_d2d,))] + _stage_scratch(rows, block.shape[1], block.dtype),
    )(block)


def _chip_scatter(parts, name):
    def body(src, out, send_sems, recv_sems, *stage):
        x, y, c = lax.axis_index("x"), lax.axis_index("y"), lax.axis_index("c")
        me = 2 * x + y
        chips = [(1 - x, y), (x, 1 - y), (1 - x, 1 - y)]
        sends = []
        for j, (px, py) in enumerate(chips):
            cp = pltpu.make_async_remote_copy(src_ref=src.at[2 * px + py], dst_ref=out.at[me], send_sem=send_sems.at[j],
                                              recv_sem=recv_sems.at[j], device_id=(px, py, c), device_id_type=MESH)
            cp.start()
            sends.append(cp)
        _staged_copy(src.at[me], out.at[me], *stage)
        for j, (px, py) in enumerate(chips):
            pltpu.make_async_remote_copy(src_ref=src.at[me], dst_ref=out.at[2 * px + py], send_sem=send_sems.at[j],
                                         recv_sem=recv_sems.at[j], device_id=(px, py, c), device_id_type=MESH).wait_recv()
        for cp in sends:
            cp.wait_send()

    return pl.pallas_call(
        body, name=name, in_specs=[_ANY], out_specs=_ANY, out_shape=jax.ShapeDtypeStruct(parts.shape, parts.dtype),
        scratch_shapes=[pltpu.SemaphoreType.DMA((3,)), pltpu.SemaphoreType.DMA((3,))]
        + _stage_scratch(parts.shape[1], parts.shape[2], parts.dtype),
    )(parts)


_HBM = pl.BlockSpec(memory_space=pltpu.HBM)
_SEM = pl.BlockSpec(memory_space=pltpu.SEMAPHORE)
_EFFECT = pltpu.SideEffectType.DATAFLOW_SIDE_EFFECTING


def _gather_peers():
    x, y, c = lax.axis_index("x"), lax.axis_index("y"), lax.axis_index("c")
    return x, y, c, 2 * x + y, [(1 - x, y), (x, 1 - y), (1 - x, 1 - y)]


def _chip_gather_start(block, name):
    half = block.shape[0] // 2

    def body(src, land, send_sems, recv_sems, src_out, land_out, token):
        x, y, c, me, chips = _gather_peers()
        mine = pl.ds(pl.multiple_of(c * half, 16), half)
        for j, (px, py) in enumerate(chips):
            _remote(src.at[mine], land.at[me, mine], send_sems.at[j], recv_sems.at[j], (px, py, c)).start()
        token[...] = jnp.zeros_like(token)

    land_shape = (N_CHIPS,) + block.shape
    return pl.pallas_call(
        body, name=name,
        out_shape=(pltpu.SemaphoreType.DMA((3,)), pltpu.SemaphoreType.DMA((3,)), pltpu.HBM(block.shape, block.dtype),
                   pltpu.HBM(land_shape, block.dtype), jax.ShapeDtypeStruct((8, LANES), f32)),
        in_specs=(_HBM, _HBM), out_specs=(_SEM, _SEM, _HBM, _HBM, pl.BlockSpec(memory_space=pltpu.VMEM)),
        input_output_aliases={0: 2, 1: 3}, compiler_params=pltpu.CompilerParams(has_side_effects=_EFFECT),
    )(pltpu.with_memory_space_constraint(block, pltpu.HBM),
      pltpu.with_memory_space_constraint(lax.empty(land_shape, block.dtype), pltpu.HBM))


def _chip_gather_wait(send_sems, recv_sems, block, land, after, name):
    half = block.shape[0] // 2

    def body(src, land_ref, send_ref, recv_ref, after_ref, src_dead, land_out):
        x, y, c, me, chips = _gather_peers()
        mine = pl.ds(pl.multiple_of(c * half, 16), half)
        for j, (px, py) in enumerate(chips):
            cp = _remote(src.at[mine], land_ref.at[2 * px + py, mine], send_ref.at[j], recv_ref.at[j], (px, py, c))
            cp.wait_send()
            cp.wait_recv()

    return pl.pallas_call(
        body, name=name, out_shape=(pltpu.HBM(block.shape, block.dtype), pltpu.HBM(land.shape, land.dtype)),
        in_specs=(_HBM, _HBM, _SEM, _SEM, _ANY), out_specs=(_HBM, _HBM), input_output_aliases={0: 0, 1: 1},
        compiler_params=pltpu.CompilerParams(has_side_effects=_EFFECT),
    )(block, land, send_sems, recv_sems, after)


def _chip_gather_finish(block, land, name):
    rows = block.shape[0]
    half = rows // 2
    piece = half // D2D_STREAMS

    def body(src, land_ref, out, d2d_send, d2d_recv, *stage):
        x, y, c, me, chips = _gather_peers()
        sibling = (x, y, 1 - c)
        sends = []
        for j, (px, py) in enumerate(chips):
            slot = 2 * px + py
            for s in range(D2D_STREAMS):
                r = pl.ds(pl.multiple_of(c * half + s * piece, 16), piece)
                k = j * D2D_STREAMS + s
                cp = _remote(land_ref.at[slot, r], out.at[slot, r], d2d_send.at[k], d2d_recv.at[k], sibling)
                cp.start()
                sends.append(cp)
        _staged_copy(src, out.at[me], *stage)
        for j, (px, py) in enumerate(chips):
            slot = 2 * px + py
            for s in range(D2D_STREAMS):
                r = pl.ds(pl.multiple_of((1 - c) * half + s * piece, 16), piece)
                k = j * D2D_STREAMS + s
                _remote(land_ref.at[slot, r], out.at[slot, r], d2d_send.at[k], d2d_recv.at[k], sibling).wait_recv()
        for cp in sends:
            cp.wait_send()

    n_d2d = 3 * D2D_STREAMS
    return pl.pallas_call(
        body, name=name, in_specs=[_ANY, _ANY], out_specs=_ANY, out_shape=jax.ShapeDtypeStruct(land.shape, land.dtype),
        input_output_aliases={1: 0},
        scratch_shapes=[pltpu.SemaphoreType.DMA((n_d2d,)), pltpu.SemaphoreType.DMA((n_d2d,))]
        + _stage_scratch(rows, block.shape[1], block.dtype),
    )(block, land)


def _chip_scatter_start(parts, name):
    def body(src, land, send_sems, recv_sems, src_out, land_out, token):
        x, y, c = lax.axis_index("x"), lax.axis_index("y"), lax.axis_index("c")
        me = 2 * x + y
        for j, (px, py) in enumerate([(1 - x, y), (x, 1 - y), (1 - x, 1 - y)]):
            _remote(src.at[2 * px + py], land.at[me], send_sems.at[j], recv_sems.at[j], (px, py, c)).start()
        token[...] = jnp.zeros_like(token)

    return pl.pallas_call(
        body, name=name,
        out_shape=(pltpu.SemaphoreType.DMA((3,)), pltpu.SemaphoreType.DMA((3,)), pltpu.HBM(parts.shape, parts.dtype),
                   pltpu.HBM(parts.shape, parts.dtype), jax.ShapeDtypeStruct((8, LANES), f32)),
        in_specs=(_HBM, _HBM), out_specs=(_SEM, _SEM, _HBM, _HBM, pl.BlockSpec(memory_space=pltpu.VMEM)),
        input_output_aliases={0: 2, 1: 3}, compiler_params=pltpu.CompilerParams(has_side_effects=_EFFECT),
    )(pltpu.with_memory_space_constraint(parts, pltpu.HBM),
      pltpu.with_memory_space_constraint(lax.empty(parts.shape, parts.dtype), pltpu.HBM))


def _chip_scatter_wait(send_sems, recv_sems, parts, land, after, name):
    def body(src, land_ref, send_ref, recv_ref, after_ref, src_dead, land_out):
        x, y, c = lax.axis_index("x"), lax.axis_index("y"), lax.axis_index("c")
        me = 2 * x + y
        for j, (px, py) in enumerate([(1 - x, y), (x, 1 - y), (1 - x, 1 - y)]):
            cp = _remote(src.at[2 * px + py], land_ref.at[2 * px + py], send_ref.at[j], recv_ref.at[j], (px, py, c))
            cp.wait_send()
            cp.wait_recv()

    return pl.pallas_call(
        body, name=name, out_shape=(pltpu.HBM(parts.shape, parts.dtype), pltpu.HBM(land.shape, land.dtype)),
        in_specs=(_HBM, _HBM, _SEM, _SEM, _ANY), out_specs=(_HBM, _HBM), input_output_aliases={0: 0, 1: 1},
        compiler_params=pltpu.CompilerParams(has_side_effects=_EFFECT),
    )(parts, land, send_sems, recv_sems, after)


def _sum_slots_own(landed, own, chip, name, block_rows=512):
    s_n, r_n, c_n = landed.shape
    br = _pick(r_n, block_rows, 8)

    def body(chip_ref, land_ref, own_ref, o_ref):
        acc = jnp.zeros((br, c_n), f32)
        for s in range(s_n):
            acc = acc + jnp.where(chip_ref[0] == s, own_ref[s], land_ref[s]).astype(f32)
        o_ref[...] = acc

    spec = pl.BlockSpec((s_n, br, c_n), lambda i, c: (0, i, 0))
    grid_spec = pltpu.PrefetchScalarGridSpec(num_scalar_prefetch=1, grid=(r_n // br,), in_specs=[spec, spec],
                                             out_specs=pl.BlockSpec((br, c_n), lambda i, c: (i, 0)))
    return pl.pallas_call(
        body, name=name, grid_spec=grid_spec, out_shape=jax.ShapeDtypeStruct((r_n, c_n), f32),
        compiler_params=_params(("parallel",)),
    )(chip, landed, own)


def _core_send_other_half(parts, name):
    n_slots, rows, cols = parts.shape
    half = rows // 2
    piece = half // D2D_STREAMS
    assert rows % (2 * D2D_STREAMS * 16) == 0

    def body(src, out, send_sems, recv_sems):
        x, y, c = lax.axis_index("x"), lax.axis_index("y"), lax.axis_index("c")
        sibling = (x, y, 1 - c)
        sends = []
        for k in range(n_slots):
            for s in range(D2D_STREAMS):
                theirs = pl.ds(pl.multiple_of((1 - c) * half + s * piece, 16), piece)
                i = k * D2D_STREAMS + s
                cp = _remote(src.at[k, theirs], out.at[k, pl.ds(s * piece, piece)], send_sems.at[i], recv_sems.at[i], sibling)
                cp.start()
                sends.append(cp)
        for cp in sends:
            cp.wait_recv()
        for cp in sends:
            cp.wait_send()

    n = n_slots * D2D_STREAMS
    return pl.pallas_call(
        body, name=name, in_specs=[_ANY], out_specs=_ANY, out_shape=jax.ShapeDtypeStruct((n_slots, half, cols), parts.dtype),
        scratch_shapes=[pltpu.SemaphoreType.DMA((n,)), pltpu.SemaphoreType.DMA((n,))],
    )(parts)


def _other_half_copies(src, land, send_sems, recv_sems):
    n_slots, rows, _ = src.shape
    half = rows // 2
    piece = half // D2D_STREAMS
    x, y, c = lax.axis_index("x"), lax.axis_index("y"), lax.axis_index("c")
    copies = []
    for k in range(n_slots):
        for s in range(D2D_STREAMS):
            theirs = pl.ds(pl.multiple_of((1 - c) * half + s * piece, 16), piece)
            i = k * D2D_STREAMS + s
            copies.append(_remote(src.at[k, theirs], land.at[k, pl.ds(s * piece, piece)], send_sems.at[i], recv_sems.at[i],
                                  (x, y, 1 - c)))
    return copies


def _core_send_other_half_start(parts, name):
    n_slots, rows, cols = parts.shape
    assert rows % (2 * D2D_STREAMS * 16) == 0
    n = n_slots * D2D_STREAMS
    land_shape = (n_slots, rows // 2, cols)

    def body(src, land, send_sems, recv_sems, src_out, land_out, token):
        for cp in _other_half_copies(src, land, send_sems, recv_sems):
            cp.start()
        token[...] = jnp.zeros_like(token)

    return pl.pallas_call(
        body, name=name,
        out_shape=(pltpu.SemaphoreType.DMA((n,)), pltpu.SemaphoreType.DMA((n,)), pltpu.HBM(parts.shape, parts.dtype),
                   pltpu.HBM(land_shape, parts.dtype), jax.ShapeDtypeStruct((8, LANES), f32)),
        in_specs=(_HBM, _HBM), out_specs=(_SEM, _SEM, _HBM, _HBM, pl.BlockSpec(memory_space=pltpu.VMEM)),
        input_output_aliases={0: 2, 1: 3}, compiler_params=pltpu.CompilerParams(has_side_effects=_EFFECT),
    )(pltpu.with_memory_space_constraint(parts, pltpu.HBM),
      pltpu.with_memory_space_constraint(lax.empty(land_shape, parts.dtype), pltpu.HBM))


def _core_send_other_half_wait(send_sems, recv_sems, parts, land, after, name):
    def body(src, land_ref, send_ref, recv_ref, after_ref, src_dead, land_out):
        for cp in _other_half_copies(src, land_ref, send_ref, recv_ref):
            cp.wait_send()
            cp.wait_recv()

    return pl.pallas_call(
        body, name=name, out_shape=(pltpu.HBM(parts.shape, parts.dtype), pltpu.HBM(land.shape, land.dtype)),
        in_specs=(_HBM, _HBM, _SEM, _SEM, _ANY), out_specs=(_HBM, _HBM), input_output_aliases={0: 0, 1: 1},
        compiler_params=pltpu.CompilerParams(has_side_effects=_EFFECT),
    )(parts, land, send_sems, recv_sems, after)


def _add_my_half(parts, other, core, name, block_rows=1024):
    n_slots, rows, cols = parts.shape
    half = rows // 2
    br = _pick(half, block_rows, 16)
    nb = half // br

    def body(c_ref, a_ref, b_ref, o_ref):
        o_ref[...] = (a_ref[...].astype(f32) + b_ref[...].astype(f32)).astype(o_ref.dtype)

    grid_spec = pltpu.PrefetchScalarGridSpec(
        num_scalar_prefetch=1, grid=(n_slots, nb),
        in_specs=[pl.BlockSpec((1, br, cols), lambda k, i, c: (k, c[0] * nb + i, 0)),
                  pl.BlockSpec((1, br, cols), lambda k, i, c: (k, i, 0))],
        out_specs=pl.BlockSpec((1, br, cols), lambda k, i, c: (k, i, 0)))
    return pl.pallas_call(
        body, name=name, grid_spec=grid_spec, out_shape=jax.ShapeDtypeStruct((n_slots, half, cols), parts.dtype),
        compiler_params=_params(("parallel", "parallel")),
    )(core, parts, other)


def _core_join_halves(mine, name):
    half, cols = mine.shape
    piece = half // D2D_STREAMS
    assert half % (D2D_STREAMS * 16) == 0

    def body(src, out, send_sems, recv_sems, *stage):
        x, y, c = lax.axis_index("x"), lax.axis_index("y"), lax.axis_index("c")
        sibling = (x, y, 1 - c)
        sends = []
        for s in range(D2D_STREAMS):
            dst = out.at[pl.ds(pl.multiple_of(c * half + s * piece, 16), piece)]
            cp = _remote(src.at[pl.ds(s * piece, piece)], dst, send_sems.at[s], recv_sems.at[s], sibling)
            cp.start()
            sends.append(cp)
        _staged_copy(src, out.at[pl.ds(pl.multiple_of(c * half, 16), half)], *stage)
        for s in range(D2D_STREAMS):
            dst = out.at[pl.ds(pl.multiple_of((1 - c) * half + s * piece, 16), piece)]
            _remote(src.at[pl.ds(s * piece, piece)], dst, send_sems.at[s], recv_sems.at[s], sibling).wait_recv()
        for cp in sends:
            cp.wait_send()

    return pl.pallas_call(
        body, name=name, in_specs=[_ANY], out_specs=_ANY, out_shape=jax.ShapeDtypeStruct((2 * half, cols), mine.dtype),
        scratch_shapes=[pltpu.SemaphoreType.DMA((D2D_STREAMS,)), pltpu.SemaphoreType.DMA((D2D_STREAMS,))]
        + _stage_scratch(half, cols, mine.dtype),
    )(mine)


def _reduce_start(exchanging, core, after, tag):
    parts, other = _core_send_other_half_wait(*exchanging[:4], after, f"exchange_wait_{tag}")
    return _chip_scatter_start(_add_my_half(parts, other, core, f"sum_core_halves_{tag}"), f"scatter_start_{tag}")


def _reduce_finish(started, chip, after, tag):
    send_sems, recv_sems, chip_part, land, _ = started
    own, landed = _chip_scatter_wait(send_sems, recv_sems, chip_part, land, after, f"scatter_wait_{tag}")
    return _core_join_halves(_sum_slots_own(landed, own, chip, f"sum_chip_parts_{tag}"), f"join_core_halves_{tag}")


def _reduce_to_chips(parts, core, tag):
    chip_part = _add_my_half(parts, _core_send_other_half(parts, f"exchange_core_halves_{tag}"), core, f"sum_core_halves_{tag}")
    my_sum = _sum_slots(_chip_scatter(chip_part, f"scatter_{tag}"), f"sum_chip_parts_{tag}")
    return _core_join_halves(my_sum, f"join_core_halves_{tag}")


def _sum_slots(stack, name, block_rows=512):
    s_n, r_n, c_n = stack.shape
    br = _pick(r_n, block_rows, 8)

    def body(in_ref, o_ref):
        acc = in_ref[0].astype(f32)
        for s in range(1, s_n):
            acc = acc + in_ref[s].astype(f32)
        o_ref[...] = acc

    return pl.pallas_call(
        body, name=name, grid=(r_n // br,), in_specs=[pl.BlockSpec((s_n, br, c_n), lambda i: (0, i, 0))],
        out_specs=pl.BlockSpec((br, c_n), lambda i: (i, 0)), out_shape=jax.ShapeDtypeStruct((r_n, c_n), f32),
        compiler_params=_params(("parallel",)),
    )(stack)


def _concat_padded(parts, mult):
    rows = sum(p.shape[0] for p in parts)
    pad = (-rows) % mult
    if pad:
        parts = parts + [jnp.zeros((pad,) + parts[0].shape[1:], parts[0].dtype)]
    return jnp.concatenate(parts, axis=0)


def _pack_weights(w, l, names, conv_w=None):
    parts = [w[n][l].astype(bf16).reshape(-1, PACK_COLS) for n in names]
    if conv_w is not None:
        parts.append(_concat_padded([lax.bitcast_convert_type(conv_w, bf16).reshape(-1, PACK_COLS)], 16))
    return _concat_padded(parts, PACK_ROW_MULT)


def _unpack_weights(full, w, l, names, conv_w=None):
    start, r0 = {}, 0
    for n in names:
        start[n] = r0
        r0 += w[n][l].size // PACK_COLS

    def shards(n):
        rows = w[n][l].size // PACK_COLS
        return [full[k, start[n]:start[n] + rows].reshape(w[n].shape[1:]) for k in range(N_CHIPS)]

    if conv_w is None:
        return shards, None
    rows = conv_w.size * 2 // PACK_COLS
    pieces = lax.bitcast_convert_type(full[:, r0:r0 + rows].reshape((N_CHIPS,) + conv_w.shape + (2,)), f32)
    return shards, jnp.concatenate([pieces[k] for k in range(N_CHIPS)], axis=2)


def _pack_big_grads(layer_grads):
    parts, slot_rows = [], 0
    for k in range(N_CHIPS):
        slot = []
        for n in BIG:
            for g in layer_grads:
                width = g[n].shape[0] // N_CHIPS
                slot.append(g[n][k * width:(k + 1) * width].astype(bf16).reshape(-1, PACK_COLS))
        slot_rows = sum(p.shape[0] for p in slot)
        pad = (-slot_rows) % PACK_ROW_MULT
        if pad:
            slot.append(jnp.zeros((pad, PACK_COLS), bf16))
        slot_rows += pad
        parts += slot
    return jnp.concatenate(parts, axis=0).reshape(N_CHIPS, slot_rows, PACK_COLS)


def _unpack_big_grads(summed, w):
    out, r0 = {}, 0
    for n in BIG:
        rows = w[n][0].size // PACK_COLS
        out[n] = [s[r0:r0 + rows].reshape(w[n].shape[1:]) for s in summed]
        r0 += rows
    return out


_SMALL_TILE = 8 * LANES


def _pack_small(vals, names):
    parts = []
    for n in names:
        pieces = vals[n] if isinstance(vals[n], list) else [vals[n]]
        size = sum(p.size for p in pieces)
        if all(p.size % _SMALL_TILE == 0 for p in pieces):
            parts += [p.reshape(-1, LANES) for p in pieces]
        else:
            flat = [p.reshape(-1) for p in pieces] + [jnp.zeros(((-size) % _SMALL_TILE,), f32)]
            parts.append(jnp.concatenate(flat).reshape(-1, LANES))
    return _concat_padded(parts, PACK_ROW_MULT)


def _unpack_small(packed, like, names):
    out, r0 = {}, 0
    for n in names:
        size = like[n].size
        rows = -(-size // _SMALL_TILE) * 8
        out[n] = packed[r0:r0 + rows].reshape(-1)[:size].reshape(like[n].shape)
        r0 += rows
    return out


def _dims(w, x):
    d = {}
    d["D"] = x.shape[-1]
    d["T"] = x.shape[-2]
    d["DI"] = w["ssd_norm_g"].shape[-1]
    d["NH"] = w["ssd_dt_bias"].shape[-1]
    d["CD"] = w["ssd_conv_b"].shape[-1]
    d["G"] = SSD_N_GROUPS
    d["HPG"] = d["NH"] // d["G"]
    d["P"] = d["DI"] // d["NH"]
    d["N"] = (d["CD"] - d["DI"]) // (2 * d["G"])
    d["S5G"], d["S5N"] = w["s5_lambda_re"].shape[-2:]
    d["S5C"] = w["s5_b_re"].shape[-1]
    d["S5W"] = d["S5G"] * d["S5C"]
    d["NSB"] = d["S5W"] // S5_SUPERBLOCK
    d["GSB"] = d["S5G"] // d["NSB"]
    return d


def _head_pad(v, d):
    lead = v.shape[:-1]
    v = v.reshape(lead + (d["G"], d["HPG"]))
    v = jnp.concatenate([v, jnp.zeros(lead + (d["G"], LANES - d["HPG"]), v.dtype)], axis=-1)
    return v.reshape(lead + (d["G"] * LANES,))


def _head_unpad(v, d):
    lead = v.shape[:-1]
    return v.reshape(lead + (d["G"], LANES))[..., :d["HPG"]].reshape(lead + (d["NH"],))


def _w_in_perm(shards, d):
    o, nh = d["DI"] + d["CD"], d["NH"]
    r = shards[0].shape[0]

    def rows(lo, hi):
        out = []
        for k, s in enumerate(shards):
            a, b = max(lo, k * r), min(hi, (k + 1) * r)
            if a < b:
                out.append(s[a - k * r:b - k * r])
        return out

    dt = _head_pad(jnp.concatenate(rows(o, o + nh), axis=0).T, d).T
    return jnp.concatenate(rows(0, o) + rows(o + nh, len(shards) * r) + [dt], axis=0)


def _w_in_unperm(g, d):
    o = d["DI"] + d["CD"]
    rest = d["S5W"] + 2 * d["D"]
    return jnp.concatenate([g[:o], _head_unpad(g[o + rest:].T, d).T, g[o:o + rest]], axis=0)


def _s5_block_diag(v, d):
    gsb = d["GSB"]
    g, a, b = v.shape
    row_group = (lax.broadcasted_iota(jnp.int32, (g * a, gsb * b), 0) // a) % gsb
    col_group = lax.broadcasted_iota(jnp.int32, (g * a, gsb * b), 1) // b
    return jnp.where(row_group == col_group, jnp.tile(v.reshape(g * a, b), (1, gsb)), 0)


def _s5_diag_blocks(m, d, a, b):
    gsb = d["GSB"]
    rows = m.shape[0]
    m = m.reshape(rows, gsb, b)
    row_group = (lax.broadcasted_iota(jnp.int32, (rows, gsb, 1), 0) // a) % gsb
    col_group = lax.broadcasted_iota(jnp.int32, (rows, gsb, 1), 1)
    return jnp.sum(jnp.where(row_group == col_group, m, 0), axis=1).reshape(rows // a, a, b)


def _s5_lam_rows(v, d):
    v = v.reshape(d["NSB"], 1, d["GSB"] * d["S5N"])
    return jnp.broadcast_to(v, (d["NSB"], 8, v.shape[-1])).reshape(d["NSB"] * 8, -1)


def _ffn_fwd(h, pre_g, post_g, wgu, wd, tag):
    D = h.shape[1]
    H2 = wgu.shape[0]
    xn = _row_kernel(f"{tag}_norm", _fwd_of(_f_norm), [(h, D, 0)], [pre_g], [(D, bf16)])[0]
    a, b, hid = _mm_swiglu(xn, wgu, f"{tag}_mm_up")
    f = _mm(hid, wd, "nn", f32, f"{tag}_mm_down")
    out = _row_kernel(f"{tag}_resnorm", _fwd_of(_f_resnorm(0.5)), [(h, D, 0), (f, D, 0)], [post_g], [(D, f32)])[0]
    return out, dict(h=h, xn=xn, a=a, b=b, hid=hid, f=f)


def _ffn_bwd(dh_out, s, pre_g, post_g, wgu, wd, tag):
    D = dh_out.shape[1]
    H2 = wgu.shape[0]
    df, dpost = _row_kernel(f"{tag}_resnorm_bwd", _vjp_of(_f_post(0.5), 1, 1, [0]), [(s["f"], D, 0), (dh_out, D, 0)],
                            [post_g], [(D, bf16)], [post_g.shape])
    dwd = _mm(s["hid"], df, "tn", bf16, f"{tag}_mm_dwd")
    dhid = _mm(df, wd, "nt", bf16, f"{tag}_mm_dhid")
    dab = _row_kernel(f"{tag}_swiglu_bwd", _swiglu_bwd, [(s["a"], H2 // 2, 0), (s["b"], H2 // 2, 0), (dhid, H2 // 2, 0)], [], [(H2, bf16)])[0]
    dwgu = _mm(dab, s["xn"], "tn", bf16, f"{tag}_mm_dwgu")
    dxn = _mm(dab, wgu, "nn", f32, f"{tag}_mm_dxn")
    dh, dpre = _row_kernel(f"{tag}_norm_bwd", _vjp_of(_f_norm, 1, 1, [0], 1), [(s["h"], D, 0), (dxn, D, 0), (dh_out, D, 0)],
                           [pre_g], [(D, f32)], [pre_g.shape])
    return dh, dict(pre_g=dpre, post_g=dpost, wgu=dwgu, wd=dwd)


def _mixer_fwd(h, p, d):
    D, DI, CD, G, N = d["D"], d["DI"], d["CD"], d["G"], d["N"]
    gl = G * LANES
    c_u5, c_ga, c_gb, c_dt = DI + CD, DI + CD + d["S5W"], DI + CD + d["S5W"] + D, DI + CD + d["S5W"] + 2 * D
    u = _row_kernel("mix_norm", _fwd_of(_f_norm), [(h, D, 0)], [p["mix_pre_g"]], [(D, bf16)])[0]
    proj = _mm(u, p["w_in"], "nt", f32, "mix_mm_in", bn_t=512)
    act = _conv_fwd(proj, DI, p["conv_w"], p["conv_b"], "ssd_conv")
    dt, adt = _row_kernel("ssd_dt", _fwd_of(_f_dt), [(proj, gl, c_dt // gl)], [p["dt_bias"], p["a_log"]], [(gl, f32)] * 2)
    y_ssd, states = _ssd_fwd(act, dt, adt, p["d_skip"], d["HPG"], d["P"], N, "ssd_scan")
    nrm = _row_kernel("ssd_post", _fwd_of(_f_ssdpost(G)), [(y_ssd, DI, 0), (proj, DI, 0)], [p["norm_g"]], [(DI, bf16)])[0]
    y_a = _mm(nrm, p["w_a"], "nn", f32, "mix_mm_a")
    u5 =(proj, d["S5W"], c_u5 // d["S5W"])
    bu = _s5_in(proj, c_u5, p["bsb"], d)
    s5st = _s5_scan_fwd(bu, p["lam_re_rows"], p["lam_im_rows"], d["NSB"], "s5_scan")
    y5 = _bdmm(s5st, p["csb"], "nn", d["NSB"], f32, "s5_mm_c")
    gel = _row_kernel("s5_post", _fwd_of(_f_s5post), [(y5, d["S5W"], 0), u5], [p["s5_d"]], [(d["S5W"], bf16)])[0]
    vg = _mm(gel, p["w_glu"], "nt", bf16, "mix_mm_glu")
    glu = _row_kernel("s5_glu", _glu_fwd, [(vg, vg.shape[1], 0)], [], [(vg.shape[1] // 2, bf16)])[0]
    y_b = _mm(glu, p["w_b"], "nn", f32, "mix_mm_b")
    merged = _row_kernel("mix_merge", _fwd_of(_f_merge), [(proj, D, c_ga // D), (y_a, D, 0), (proj, D, c_gb // D), (y_b, D, 0)],
                         [], [(D, bf16)])[0]
    m = _mm(merged, p["w_out"], "nn", f32, "mix_mm_out")
    out = _row_kernel("mix_resnorm", _fwd_of(_f_resnorm(1.0)), [(h, D, 0), (m, D, 0)], [p["mix_post_g"]], [(D, f32)])[0]
    return out, dict(h=h, u=u, proj=proj, act=act, dt=dt, adt=adt, states=states, y_ssd=y_ssd, nrm=nrm, y_a=y_a, s5st=s5st,
                     y5=y5, gel=gel, vg=vg, glu=glu, y_b=y_b, merged=merged, m=m)


def _s5_in(proj, c_u5, bsb, d):
    T = proj.shape[0]
    nsb = d["NSB"]
    ka, nw = S5_SUPERBLOCK, bsb.shape[1]
    off = c_u5 // ka
    assert c_u5 % ka == 0
    bt = _pick(T, 1024)

    def body(a_ref, w_ref, o_ref):
        o_ref[...] = _dot(a_ref[...].astype(bf16), w_ref[...].astype(bf16))

    return pl.pallas_call(
        body, name="s5_mm_bu", grid=(nsb, T // bt),
        in_specs=[pl.BlockSpec((bt, ka), lambda j, i: (i, off + j)), pl.BlockSpec((ka, nw), lambda j, i: (j, 0))],
        out_specs=pl.BlockSpec((bt, nw), lambda j, i: (i, j)), out_shape=jax.ShapeDtypeStruct((T, nsb * nw), f32),
        compiler_params=_params(("parallel", "parallel")),
    )(proj, bsb)


def _s5_dbsb(proj, c_u5, a, d):
    T = proj.shape[0]
    nsb = d["NSB"]
    ka, nw = S5_SUPERBLOCK, a.shape[1] // nsb
    off = c_u5 // ka
    bt = _pick(T, 1024)

    def body(u_ref, a_ref, o_ref):
        pr = _dot(u_ref[...].astype(bf16), a_ref[...].astype(bf16), "tn")
        k = pl.program_id(1)

        @pl.when(k == 0)
        def _():
            o_ref[...] = pr

        @pl.when(k > 0)
        def _():
            o_ref[...] += pr

    return pl.pallas_call(
        body, name="s5_mm_dbsb", grid=(nsb, T // bt),
        in_specs=[pl.BlockSpec((bt, ka), lambda j, k: (k, off + j)), pl.BlockSpec((bt, nw), lambda j, k: (k, j))],
        out_specs=pl.BlockSpec((ka, nw), lambda j, k: (j, 0)), out_shape=jax.ShapeDtypeStruct((nsb * ka, nw), f32),
        compiler_params=_params(("parallel", "arbitrary")),
    )(proj, a)


def _mixer_bwd(dh_out, s, p, d):
    D, DI, CD, G, N, S5W = d["D"], d["DI"], d["CD"], d["G"], d["N"], d["S5W"]
    gl = G * LANES
    gn = G * N
    c_u5, c_ga, c_gb, c_dt = DI + CD, DI + CD + S5W, DI + CD + S5W + D, DI + CD + S5W + 2 * D
    proj = s["proj"]
    g = {}
    dm, g["mix_post_g"] = _row_kernel("mix_resnorm_bwd", _vjp_of(_f_post(1.0), 1, 1, [0]), [(s["m"], D, 0), (dh_out, D, 0)],
                                      [p["mix_post_g"]], [(D, bf16)], [p["mix_post_g"].shape])
    g["w_out"] = _mm(s["merged"], dm, "tn", bf16, "mix_mm_dwout")
    dmerged = _mm(dm, p["w_out"], "nt", f32, "mix_mm_dmerged")
    dga, dya, dgb, dyb = _row_kernel(
        "mix_merge_bwd", _vjp_of(_f_merge, 4, 1, [0, 1, 2, 3]),
        [(proj, D, c_ga // D), (s["y_a"], D, 0), (proj, D, c_gb // D), (s["y_b"], D, 0), (dmerged, D, 0)], [],
        [(D, bf16), (D, bf16), (D, bf16), (D, bf16)])
    g["w_a"] = _mm(s["nrm"], dya, "tn", bf16, "mix_mm_dwa")
    dnrm = _mm(dya, p["w_a"], "nt", f32, "mix_mm_dnrm")
    dy_ssd, dz, g["norm_g"] = _row_kernel(
        "ssd_post_bwd", _vjp_of(_f_ssdpost(G), 2, 1, [0, 1]), [(s["y_ssd"], DI, 0), (proj, DI, 0), (dnrm, DI, 0)],
        [p["norm_g"]], [(DI, f32), (DI, bf16)], [p["norm_g"].shape])
    dxs, d_b, d_c, ddt, dadt, dd = _ssd_bwd(s["act"], s["dt"], s["adt"], p["d_skip"], s["states"], dy_ssd,
                                            d["HPG"], d["P"], N, "ssd_scan_bwd")
    g["d_skip"] = dd.reshape(G, 8, LANES)[:, 0, :].reshape(1, gl)
    ddt_raw, g["dt_bias"], g["a_log"] = _row_kernel(
        "ssd_dt_bwd", _vjp_of(_f_dt, 1, 2, [0]), [(proj, gl, c_dt // gl), (ddt, gl, 0), (dadt, gl, 0)],
        [p["dt_bias"], p["a_log"]], [(gl, bf16)], [p["dt_bias"].shape, p["a_log"].shape])
    cw, cb = p["conv_w"], p["conv_b"]
    dxc_x, dw_x, db_x = _conv_bwd(proj, DI, cw[:, :DI], cb[:, :DI], dxs, "ssd_conv_bwd_x")
    dxc_b, dw_b, db_b = _conv_bwd(proj, 2 * DI, cw[:, DI:DI + gn], cb[:, DI:DI + gn], d_b, "ssd_conv_bwd_b")
    dxc_c, dw_c, db_c = _conv_bwd(proj, 2 * DI + gn, cw[:, DI + gn:], cb[:, DI + gn:], d_c, "ssd_conv_bwd_c")
    g["conv_w"] = jnp.concatenate([dw_x, dw_b, dw_c], axis=1)
    g["conv_b"] = jnp.concatenate([db_x, db_b, db_c], axis=1)
    g["w_b"] = _mm(s["glu"], dyb, "tn", bf16, "mix_mm_dwb")
    dglu = _mm(dyb, p["w_b"], "nt", f32, "mix_mm_dglu")
    dvg = _row_kernel("s5_glu_bwd", _glu_bwd, [(s["vg"], s["vg"].shape[1], 0), (dglu, S5W, 0)], [], [(s["vg"].shape[1], bf16)])[0]
    g["w_glu"] = _mm(dvg, s["gel"], "tn", bf16, "mix_mm_dwglu")
    dgel = _mm(dvg, p["w_glu"], "nn", f32, "mix_mm_dgel")
    dy5, du5a, g["s5_d"] = _row_kernel(
        "s5_post_bwd", _vjp_of(_f_s5post, 2, 1, [0, 1]), [(s["y5"], S5W, 0), (proj, S5W, c_u5 // S5W), (dgel, S5W, 0)],
        [p["s5_d"]], [(S5W, bf16), (S5W, f32)], [p["s5_d"].shape])
    g["csb"] = _bdmm(s["s5st"], dy5, "tn", d["NSB"], f32, "s5_mm_dcsb")
    gst = _bdmm(dy5, p["csb"], "nt", d["NSB"], f32, "s5_mm_gst")
    a, g["lam_re_rows"], g["lam_im_rows"] = _s5_scan_bwd(gst, s["s5st"], p["lam_re_rows"], p["lam_im_rows"], d["NSB"], "s5_scan_bwd")
    g["bsb"] = _s5_dbsb(proj, c_u5, a, d)
    du5b = _bdmm(a, p["bsb"], "nt", d["NSB"], f32, "s5_mm_du5")
    du5 = _row_kernel("s5_du5", _add_fn, [(du5a, S5W, 0), (du5b, S5W, 0)], [], [(S5W, bf16)])[0]
    dproj = jnp.concatenate([dz, dxc_x, dxc_b, dxc_c, du5, dga, dgb, ddt_raw], axis=1)
    g["w_in"] = _mm(dproj, s["u"], "tn", bf16, "mix_mm_dwin")
    du = _mm(dproj, p["w_in"], "nn", f32, "mix_mm_du", bk_t=2176)
    dh, g["mix_pre_g"] = _row_kernel("mix_norm_bwd", _vjp_of(_f_norm, 1, 1, [0], 1), [(s["h"], D, 0), (du, D, 0), (dh_out, D, 0)],
                                     [p["mix_pre_g"]], [(D, f32)], [p["mix_pre_g"].shape])
    return dh, g


def _ffn1_params(l, w, wf):
    whole = lambda *names: jnp.concatenate([s for n in names for s in wf(n)], axis=0)
    return dict(ffn1_pre_g=w["ffn1_pre_g"][l].reshape(1, -1), ffn1_post_g=w["ffn1_post_g"][l].reshape(1, -1),
                wgu1=whole("ffn1_w_gate", "ffn1_w_up"), wd1=whole("ffn1_w_down"))


def _ffn2_params(wf):
    whole = lambda *names: jnp.concatenate([s for n in names for s in wf(n)], axis=0)
    return dict(wgu2=whole("ffn2_w_gate", "ffn2_w_up"), wd2=whole("ffn2_w_down"))


def _layer_params(l, w, wf, conv_w_full, d):
    r2 = lambda v: v[l].reshape(1, -1)
    p = {}
    for n in ["mix_pre_g", "mix_post_g", "ffn2_pre_g", "ffn2_post_g", "s5_d"]:
        p[n] = r2(w[n])
    whole = lambda *names: jnp.concatenate([s for n in names for s in wf(n)], axis=0)
    p["w_in"] = _w_in_perm(wf("w_in"), d)
    p["w_a"], p["w_glu"], p["w_b"], p["w_out"] = whole("w_branch_a"), whole("s5_w_glu"), whole("w_branch_b"), whole("w_out")
    p["conv_w"] = conv_w_full[l]
    p["conv_b"] = r2(w["ssd_conv_b"])
    p["dt_bias"] = _head_pad(r2(w["ssd_dt_bias"]), d)
    p["a_log"] = _head_pad(r2(w["ssd_a_log"]), d)
    p["d_skip"] = _head_pad(r2(w["ssd_d"]), d)
    p["norm_g"] = r2(w["ssd_norm_g"])
    g5, n5, c5 = d["S5G"], d["S5N"], d["S5C"]
    expand = jnp.repeat(jnp.eye(n5, dtype=f32), c5, axis=1)
    prep_in = [w["s5_lambda_re"][l], w["s5_lambda_im"][l], w["s5_log_step"][l].reshape(g5, 1),
               w["s5_b_re"][l].reshape(g5, n5 * c5), w["s5_b_im"][l].reshape(g5, n5 * c5), expand]
    lbr, lbi, bbr, bbi = _s5_prep(prep_in, "s5_prep")
    p["s5_prep_in"] = prep_in
    p["lam_re_rows"], p["lam_im_rows"] = _s5_lam_rows(lbr, d), _s5_lam_rows(lbi, d)
    to_cn = lambda v: v.astype(bf16).reshape(g5, n5, c5).transpose(0, 2, 1)
    p["bsb"] = jnp.concatenate([_s5_block_diag(to_cn(bbr), d), _s5_block_diag(to_cn(bbi), d)], axis=1)
    c_re = w["s5_c_re"][l].astype(bf16).transpose(0, 2, 1)
    c_im = w["s5_c_im"][l].astype(bf16).transpose(0, 2, 1)
    nsb = d["NSB"]
    csb = jnp.stack([_s5_block_diag(c_re, d).reshape(nsb, -1, S5_SUPERBLOCK),
                     _s5_block_diag(-c_im, d).reshape(nsb, -1, S5_SUPERBLOCK)], axis=1)
    p["csb"] = csb.reshape(-1, S5_SUPERBLOCK)
    return p


def _s5_param_grads(g, p, d, l):
    g5, n5, c5, nsb, gsb = d["S5G"], d["S5N"], d["S5C"], d["NSB"], d["GSB"]
    wst = gsb * n5
    dbsb = g["bsb"]
    from_cn = lambda v: v.transpose(0, 2, 1).reshape(g5, n5 * c5)
    dbbr = from_cn(_s5_diag_blocks(dbsb[:, :wst], d, c5, n5))
    dbbi = from_cn(_s5_diag_blocks(dbsb[:, wst:], d, c5, n5))
    rows = lambda v: v.reshape(nsb, 8, wst)[:, 0, :].reshape(g5, n5)
    cots = [rows(g["lam_re_rows"]), rows(g["lam_im_rows"]), dbbr, dbbi]
    dlr, dli, dls, dbr, dbi = _s5_prep_bwd(p["s5_prep_in"], cots, "s5_prep_bwd")
    dcsb = g["csb"].reshape(nsb, 2, wst, S5_SUPERBLOCK)
    dcr = _s5_diag_blocks(dcsb[:, 0].reshape(-1, S5_SUPERBLOCK), d, n5, c5).transpose(0, 2, 1)
    dci = -_s5_diag_blocks(dcsb[:, 1].reshape(-1, S5_SUPERBLOCK), d, n5, c5).transpose(0, 2, 1)
    return dict(s5_lambda_re=dlr, s5_lambda_im=dli, s5_log_step=dls.reshape(g5), s5_b_re=dbr.reshape(g5, n5, c5),
                s5_b_im=dbi.reshape(g5, n5, c5), s5_c_re=dcr, s5_c_im=dci)


def kernel(x, ffn1_pre_g, ffn1_post_g, ffn1_w_gate, ffn1_w_up, ffn1_w_down, mix_pre_g, mix_post_g, w_in, ssd_conv_w, ssd_conv_b, ssd_dt_bias, ssd_a_log, ssd_d, ssd_norm_g, w_branch_a, s5_lambda_re, s5_lambda_im, s5_b_re, s5_b_im, s5_c_re, s5_c_im, s5_log_step, s5_d, s5_w_glu, w_branch_b, w_out, ffn2_pre_g, ffn2_post_g, ffn2_w_gate, ffn2_w_up, ffn2_w_down, loss_target, m_ffn1_pre_g, m_ffn1_post_g, m_ffn1_w_gate, m_ffn1_w_up, m_ffn1_w_down, m_mix_pre_g, m_mix_post_g, m_w_in, m_ssd_conv_w, m_ssd_conv_b, m_ssd_dt_bias, m_ssd_a_log, m_ssd_d, m_ssd_norm_g, m_w_branch_a, m_s5_lambda_re, m_s5_lambda_im, m_s5_b_re, m_s5_b_im, m_s5_c_re, m_s5_c_im, m_s5_log_step, m_s5_d, m_s5_w_glu, m_w_branch_b, m_w_out, m_ffn2_pre_g, m_ffn2_post_g, m_ffn2_w_gate, m_ffn2_w_up, m_ffn2_w_down, v_ffn1_pre_g, v_ffn1_post_g, v_ffn1_w_gate, v_ffn1_w_up, v_ffn1_w_down, v_mix_pre_g, v_mix_post_g, v_w_in, v_ssd_conv_w, v_ssd_conv_b, v_ssd_dt_bias, v_ssd_a_log, v_ssd_d, v_ssd_norm_g, v_w_branch_a, v_s5_lambda_re, v_s5_lambda_im, v_s5_b_re, v_s5_b_im, v_s5_c_re, v_s5_c_im, v_s5_log_step, v_s5_d, v_s5_w_glu, v_w_branch_b, v_w_out, v_ffn2_pre_g, v_ffn2_post_g, v_ffn2_w_gate, v_ffn2_w_up, v_ffn2_w_down):
    given = dict(locals())
    for n in COL_SHARDED:
        for prefix in ("", "m_", "v_"):
            given[prefix + n] = given[prefix + n].transpose(0, 2, 1)
    w = {n: given[n] for n in WEIGHTS}
    mom = {n: given["m_" + n] for n in WEIGHTS}
    var = {n: given["v_" + n] for n in WEIGHTS}
    d = _dims(w, x)
    n_layers = w["ffn1_pre_g"].shape[0]
    T, D = d["T"], d["D"]

    conv_w = w["ssd_conv_w"]
    first, middle, last = BIG[:3], BIG[3:8], BIG[8:]
    wf, _ = _unpack_weights(_chip_all_gather(_pack_weights(w, 0, first), "gather_weights_first"), w, 0, first)
    coming = _chip_gather_start(_pack_weights(w, 0, middle, conv_w), "gather_start_l0")
    h = x.reshape(T, D) + coming[-1][0, 0]
    saved, layers = [], []
    for l in range(n_layers):
        if l > 0:
            pack, land = _chip_gather_wait(*coming[:4], h, f"gather_wait_l{l}")
            wf, _ = _unpack_weights(_chip_gather_finish(pack, land, f"gather_finish_l{l}"), w, l, BIG)
        p = _ffn1_params(l, w, wf)
        h, s1 = _ffn_fwd(h, p["ffn1_pre_g"], p["ffn1_post_g"], p["wgu1"], p["wd1"], "ffn1")
        if l == 0:
            pack, land = _chip_gather_wait(*coming[:4], h, "gather_wait_l0")
            wf, conv_w_full = _unpack_weights(_chip_gather_finish(pack, land, "gather_finish_l0"), w, 0, middle, conv_w)
            ffn2_coming = _chip_gather_start(_pack_weights(w, 0, last), "gather_start_l0_ffn2")
            h = h + ffn2_coming[-1][0, 0]
        if l + 1 < n_layers:
            coming = _chip_gather_start(_pack_weights(w, l + 1, BIG), f"gather_start_l{l + 1}")
            h = h + coming[-1][0, 0]
        p.update(_layer_params(l, w, wf, conv_w_full, d))
        layers.append(p)
        h, sm = _mixer_fwd(h, p, d)
        if l == 0:
            pack, land = _chip_gather_wait(*ffn2_coming[:4], h, "gather_wait_l0_ffn2")
            wf, _ = _unpack_weights(_chip_gather_finish(pack, land, "gather_finish_l0_ffn2"), w, 0, last)
        p.update(_ffn2_params(wf))
        h, s2 = _ffn_fwd(h, p["ffn2_pre_g"], p["ffn2_post_g"], p["wgu2"], p["wd2"], "ffn2")
        saved.append((s1, sm, s2))
    dh, loss_part = _row_kernel("loss", _loss_fn, [(h, D, 0), (loss_target.reshape(T, D), D, 0)], [], [(D, f32)], [(8, LANES)])
    loss = lax.psum(loss_part[0, 0], ("x", "y", "c"))

    my_core = lax.axis_index("c").astype(jnp.int32).reshape(1)
    my_chip = (2 * lax.axis_index("x") + lax.axis_index("y")).astype(jnp.int32).reshape(1)
    lg, exchanging, in_flight = [None] * n_layers, [None] * n_layers, [None] * n_layers
    for l in reversed(range(n_layers)):
        p = layers[l]
        s1, sm, s2 = saved[l]
        dh, g2 = _ffn_bwd(dh, s2, p["ffn2_pre_g"], p["ffn2_post_g"], p["wgu2"], p["wd2"], "ffn2")
        if l + 1 < n_layers:
            in_flight[l + 1] = _reduce_start(exchanging[l + 1], my_core, dh, f"grads_l{l + 1}")
            dh = dh + in_flight[l + 1][-1][0, 0]
        dh, gm = _mixer_bwd(dh, sm, p, d)
        dh, g1 = _ffn_bwd(dh, s1, p["ffn1_pre_g"], p["ffn1_post_g"], p["wgu1"], p["wd1"], "ffn1")
        H = p["wd1"].shape[0]
        gl = dict(ffn1_pre_g=g1["pre_g"], ffn1_post_g=g1["post_g"], ffn1_w_gate=g1["wgu"][:H], ffn1_w_up=g1["wgu"][H:],
                  ffn1_w_down=g1["wd"], ffn2_pre_g=g2["pre_g"], ffn2_post_g=g2["post_g"], ffn2_w_gate=g2["wgu"][:H],
                  ffn2_w_up=g2["wgu"][H:], ffn2_w_down=g2["wd"], mix_pre_g=gm["mix_pre_g"], mix_post_g=gm["mix_post_g"],
                  w_in=_w_in_unperm(gm["w_in"], d), ssd_conv_w=gm["conv_w"], ssd_conv_b=gm["conv_b"],
                  ssd_dt_bias=_head_unpad(gm["dt_bias"], d), ssd_a_log=_head_unpad(gm["a_log"], d),
                  ssd_d=_head_unpad(gm["d_skip"], d), ssd_norm_g=gm["norm_g"], w_branch_a=gm["w_a"], s5_d=gm["s5_d"],
                  s5_w_glu=gm["w_glu"], w_branch_b=gm["w_b"], w_out=gm["w_out"])
        gl.update(_s5_param_grads(gm, p, d, l))
        lg[l] = gl
        exchanging[l] = _core_send_other_half_start(_pack_big_grads([gl]), f"exchange_start_grads_l{l}")
        if l > 0:
            dh = dh + exchanging[l][-1][0, 0]
    grad_x = dh.reshape(x.shape)
    summed = [None] * n_layers
    for l in range(1, n_layers):
        summed[l] = _reduce_finish(in_flight[l], my_chip, exchanging[0][-1], f"grads_l{l}")
    in_flight[0] = _reduce_start(exchanging[0], my_core, summed[-1] if n_layers > 1 else grad_x, "grads_l0")
    small_names = SMALL + ["ssd_conv_w"]
    small_parts = {n: [g[n] for g in lg] for n in small_names}
    small_like = {n: jax.ShapeDtypeStruct((n_layers,) + lg[0][n].shape, f32) for n in small_names}
    small_like.update({n: w[n] for n in SMALL})
    small_pack = _pack_small(small_parts, small_names)
    small_sum = _reduce_to_chips(jnp.broadcast_to(small_pack, (N_CHIPS,) + small_pack.shape), my_core, "small")
    small = _unpack_small(small_sum, small_like, small_names)
    summed[0] = _reduce_finish(in_flight[0], my_chip, small_sum, "grads_l0")
    big_grads = _unpack_big_grads(summed, w)
    grads = {}
    k_me = 2 * lax.axis_index("x") + lax.axis_index("y")
    cw = w["ssd_conv_w"].shape[-1]
    small["ssd_conv_w"] = lax.dynamic_slice_in_dim(small["ssd_conv_w"], k_me * cw, cw, axis=2)
    grads.update(small)

    delta, new_m, new_v = {}, {}, {}
    for n in BIG:
        grads[n], delta[n], new_m[n], new_v[n] = _adamw_layers(w[n], big_grads[n], mom[n], var[n], "adamw_" + n)
    n = "ssd_conv_w"
    delta[n], new_m[n], new_v[n] = _adamw(w[n], grads[n], mom[n], var[n], "adamw_" + n)
    pw, pm, pv = [_pack_small(t, SMALL) for t in (w, mom, var)]
    assert pw.shape[0] <= small_sum.shape[0]
    sd, sm_, sv = _adamw(pw, small_sum[:pw.shape[0]], pm, pv, "adamw_small")
    delta.update(_unpack_small(sd, w, SMALL))
    new_m.update(_unpack_small(sm_, w, SMALL))
    new_v.update(_unpack_small(sv, w, SMALL))
    for n in COL_SHARDED:
        for out in (grads, delta, new_m, new_v):
            out[n] = out[n].transpose(0, 2, 1)
    return (loss, grad_x, *[grads[n] for n in WEIGHTS], *[delta[n] for n in WEIGHTS],
            *[new_m[n] for n in WEIGHTS], *[new_v[n] for n in WEIGHTS])
```
